```python
import jax, jax.numpy as jnp
from jax import lax
import numpy as np

D_MODEL = 1024
BATCH = 8
SEQ = 16384
DEPTH = 1

PLE_DIM = 256
CONV_WIDTH = 1024
CONV_K = 3
N_HEADS = 4
QK_DIM = 1024
V_DIM = 2048
DK = QK_DIM // N_HEADS
DV = V_DIM // N_HEADS
CHUNK = 64
EPS = 1e-6
SPLIT_SIZES = (CONV_WIDTH, CONV_WIDTH, CONV_WIDTH, CONV_WIDTH,
               QK_DIM, QK_DIM, V_DIM, V_DIM, V_DIM, N_HEADS, N_HEADS,
               D_MODEL, D_MODEL)
N_IN = sum(SPLIT_SIZES)
SPLIT_POINTS = tuple(int(s) for s in np.cumsum(SPLIT_SIZES)[:-1])

kernel_name = "hybrid_shortconv_mlstm_gated_merge"


def rmsnorm(x, g):
    xf = x.astype(jnp.float32)
    y = xf * lax.rsqrt(jnp.mean(xf * xf, axis=-1, keepdims=True) + EPS)
    return (y * g.astype(jnp.float32)).astype(x.dtype)


def causal_short_conv(u, w, b):
    s = u.shape[1]
    up = jnp.pad(u, ((0, 0), (CONV_K - 1, 0), (0, 0)))
    y = up[:, 0:s] * w[0]
    for j in range(1, CONV_K):
        y = y + up[:, j:j + s] * w[j]
    return y + b


def mlstm_chunkwise(q, k, v, i_raw, f_raw):
    bsz, s = q.shape[0], q.shape[1]
    nc = s // CHUNK
    f32 = jnp.float32

    def to_chunks(t):
        return t.astype(f32).reshape(bsz, nc, CHUNK, N_HEADS, -1).transpose(1, 0, 3, 2, 4)

    def gate_chunks(t):
        return t.astype(f32).reshape(bsz, nc, CHUNK, N_HEADS).transpose(1, 0, 3, 2)

    qc = to_chunks(q) * (DK ** -0.5)
    kc = to_chunks(k)
    vc = to_chunks(v)
    lic = gate_chunks(i_raw)
    lfc = jax.nn.log_sigmoid(gate_chunks(f_raw))
    mask = jnp.tril(jnp.ones((CHUNK, CHUNK), dtype=bool))

    def step(carry, xs):
        c_st, n_st, m_st = carry
        qq, kk, vv, li, lf = xs
        bcum = jnp.cumsum(lf, axis=-1)
        dmat = bcum[..., :, None] - bcum[..., None, :] + li[..., None, :]
        dmat = jnp.where(mask, dmat, -jnp.inf)
        a = bcum + m_st[..., None]
        m_row = jnp.maximum(a, jnp.max(dmat, axis=-1))
        sc = jnp.einsum('bhld,bhsd->bhls', qq, kk) * jnp.exp(dmat - m_row[..., None])
        inter = jnp.exp(a - m_row)
        num = (jnp.einsum('bhls,bhsv->bhlv', sc, vv)
               + inter[..., None] * jnp.einsum('bhld,bhdv->bhlv', qq, c_st))
        den = jnp.sum(sc, axis=-1) + inter * jnp.einsum('bhld,bhd->bhl', qq, n_st)
        h = num / jnp.maximum(jnp.abs(den), jnp.exp(-m_row))[..., None]
        b_last = bcum[..., -1]
        g = b_last[..., None] - bcum + li
        m_new = jnp.maximum(b_last + m_st, jnp.max(g, axis=-1))
        w = jnp.exp(g - m_new[..., None])
        decay = jnp.exp(b_last + m_st - m_new)
        c_new = decay[..., None, None] * c_st + jnp.einsum('bhsd,bhsv->bhdv', kk * w[..., None], vv)
        n_new = decay[..., None] * n_st + jnp.einsum('bhs,bhsd->bhd', w, kk)
        return (c_new, n_new, m_new), h

    init = (jnp.zeros((bsz, N_HEADS, DK, DV), f32),
            jnp.zeros((bsz, N_HEADS, DK), f32),
            jnp.full((bsz, N_HEADS), -jnp.inf, f32))
    _, hs = lax.scan(step, init, (qc, kc, vc, lic, lfc))
    return hs.transpose(1, 0, 3, 2, 4).reshape(bsz, s, N_HEADS, DV)


def _fwd_setup_inputs(seed: int = 0) -> dict:
    key = jax.random.key(seed)
    ks = jax.random.split(key, 16)
    nrm = jax.random.normal
    f32 = jnp.float32
    x = nrm(ks[0], (BATCH, SEQ, D_MODEL), f32)
    p = nrm(ks[1], (DEPTH, BATCH, SEQ, PLE_DIM), f32)
    g_mix = 1.0 + 0.02 * nrm(ks[2], (DEPTH, D_MODEL), f32)
    w_in = nrm(ks[3], (DEPTH, D_MODEL, N_IN), f32) * D_MODEL ** -0.5
    conv_w = nrm(ks[4], (DEPTH, CONV_K, CONV_WIDTH), f32) * CONV_K ** -0.5
    conv_b = 0.02 * nrm(ks[5], (DEPTH, CONV_WIDTH), f32)
    w_a_out = nrm(ks[6], (DEPTH, CONV_WIDTH, D_MODEL), f32) * CONV_WIDTH ** -0.5
    gb_noise = 0.1 * nrm(ks[7], (DEPTH, 2 * N_HEADS), f32)
    f_offset = jnp.concatenate([jnp.zeros((N_HEADS,), f32), jnp.linspace(3.0, 6.0, N_HEADS, dtype=f32)])
    b_gates = gb_noise + f_offset
    g_head = 1.0 + 0.02 * nrm(ks[8], (DEPTH, V_DIM), f32)
    w_b_out = nrm(ks[9], (DEPTH, V_DIM, D_MODEL), f32) * V_DIM ** -0.5
    w_o = nrm(ks[10], (DEPTH, D_MODEL, D_MODEL), f32) * D_MODEL ** -0.5
    g_ple = 1.0 + 0.02 * nrm(ks[11], (DEPTH, D_MODEL), f32)
    w_ple_gate = nrm(ks[12], (DEPTH, D_MODEL, D_MODEL), f32) * D_MODEL ** -0.5
    w_ple = nrm(ks[13], (DEPTH, PLE_DIM, D_MODEL), f32) * PLE_DIM ** -0.5
    g_final = 1.0 + 0.02 * nrm(ks[14], (D_MODEL,), f32)
    return {"x": x, "p": p, "g_mix": g_mix, "w_in": w_in, "conv_w": conv_w,
            "conv_b": conv_b, "w_a_out": w_a_out, "b_gates": b_gates,
            "g_head": g_head, "w_b_out": w_b_out, "w_o": w_o, "g_ple": g_ple,
            "w_ple_gate": w_ple_gate, "w_ple": w_ple, "g_final": g_final}


def _fwd_reference(x, p, g_mix, w_in, conv_w, conv_b, w_a_out, b_gates, g_head,
              w_b_out, w_o, g_ple, w_ple_gate, w_ple, g_final):
    bsz, s = x.shape[0], x.shape[1]
    for l in range(DEPTH):
        hn = rmsnorm(x, g_mix[l])
        proj = jnp.einsum('bsd,de->bse', hn, w_in[l])
        (xa, ba, ca, za, q, k, v, o, zb, ig, fg, ga, gb) = jnp.split(proj, SPLIT_POINTS, axis=-1)

        ya = ba * causal_short_conv(ca * xa, conv_w[l], conv_b[l])
        ya = jnp.einsum('bsc,cd->bsd', ya * jax.nn.silu(za), w_a_out[l])

        ig = ig + b_gates[l, :N_HEADS]
        fg = fg + b_gates[l, N_HEADS:]
        hb = mlstm_chunkwise(q.reshape(bsz, s, N_HEADS, DK), k.reshape(bsz, s, N_HEADS, DK),
                             v.reshape(bsz, s, N_HEADS, DV), ig, fg)
        hb = rmsnorm(hb, g_head[l].reshape(N_HEADS, DV)).reshape(bsz, s, V_DIM).astype(x.dtype)
        yb = jax.nn.sigmoid(o) * hb * jax.nn.silu(zb)
        yb = jnp.einsum('bsv,vd->bsd', yb, w_b_out[l])

        merged = jax.nn.sigmoid(ga) * ya + jax.nn.sigmoid(gb) * yb
        x = x + jnp.einsum('bsd,de->bse', merged, w_o[l])

        gate = jax.nn.sigmoid(jnp.einsum('bsd,de->bse', rmsnorm(x, g_ple[l]), w_ple_gate[l]))
        x = x + gate * jnp.einsum('bsp,pd->bsd', p[l], w_ple[l])
    return rmsnorm(x, g_final)


import jax as _jax
import jax.numpy as _jnp

TWIN_FORMAT = 'train_step'
FWD_PARAMS = ['x', 'p', 'g_mix', 'w_in', 'conv_w', 'conv_b', 'w_a_out', 'b_gates', 'g_head', 'w_b_out', 'w_o', 'g_ple', 'w_ple_gate', 'w_ple', 'g_final']
TWIN_WEIGHTS = ['g_mix', 'w_in', 'conv_w', 'conv_b', 'w_a_out', 'b_gates', 'g_head', 'w_b_out', 'w_o', 'g_ple', 'w_ple_gate', 'w_ple', 'g_final']
TWIN_DIFF_INPUT = 'x'
TWIN_INPUTS = ['x', 'p', 'g_mix', 'w_in', 'conv_w', 'conv_b', 'w_a_out', 'b_gates', 'g_head', 'w_b_out', 'w_o', 'g_ple', 'w_ple_gate', 'w_ple', 'g_final', 'loss_target', 'm_g_mix', 'm_w_in', 'm_conv_w', 'm_conv_b', 'm_w_a_out', 'm_b_gates', 'm_g_head', 'm_w_b_out', 'm_w_o', 'm_g_ple', 'm_w_ple_gate', 'm_w_ple', 'm_g_final', 'v_g_mix', 'v_w_in', 'v_conv_w', 'v_conv_b', 'v_w_a_out', 'v_b_gates', 'v_g_head', 'v_w_b_out', 'v_w_o', 'v_g_ple', 'v_w_ple_gate', 'v_w_ple', 'v_g_final']
TWIN_OUTPUTS = ['loss', 'grad_x', 'grad_g_mix', 'grad_w_in', 'grad_conv_w', 'grad_conv_b', 'grad_w_a_out', 'grad_b_gates', 'grad_g_head', 'grad_w_b_out', 'grad_w_o', 'grad_g_ple', 'grad_w_ple_gate', 'grad_w_ple', 'grad_g_final', 'delta_g_mix', 'delta_w_in', 'delta_conv_w', 'delta_conv_b', 'delta_w_a_out', 'delta_b_gates', 'delta_g_head', 'delta_w_b_out', 'delta_w_o', 'delta_g_ple', 'delta_w_ple_gate', 'delta_w_ple', 'delta_g_final', 'new_m_g_mix', 'new_m_w_in', 'new_m_conv_w', 'new_m_conv_b', 'new_m_w_a_out', 'new_m_b_gates', 'new_m_g_head', 'new_m_w_b_out', 'new_m_w_o', 'new_m_g_ple', 'new_m_w_ple_gate', 'new_m_w_ple', 'new_m_g_final', 'new_v_g_mix', 'new_v_w_in', 'new_v_conv_w', 'new_v_conv_b', 'new_v_w_a_out', 'new_v_b_gates', 'new_v_g_head', 'new_v_w_b_out', 'new_v_w_o', 'new_v_g_ple', 'new_v_w_ple_gate', 'new_v_w_ple', 'new_v_g_final']
TWIN_LEAF_KINDS = {'loss': 'loss', 'grad_x': 'grad_x', 'grad_g_mix': 'grad_w', 'grad_w_in': 'grad_w', 'grad_conv_w': 'grad_w', 'grad_conv_b': 'grad_w', 'grad_w_a_out': 'grad_w', 'grad_b_gates': 'grad_w', 'grad_g_head': 'grad_w', 'grad_w_b_out': 'grad_w', 'grad_w_o': 'grad_w', 'grad_g_ple': 'grad_w', 'grad_w_ple_gate': 'grad_w', 'grad_w_ple': 'grad_w', 'grad_g_final': 'grad_w', 'delta_g_mix': 'delta_w', 'delta_w_in': 'delta_w', 'delta_conv_w': 'delta_w', 'delta_conv_b': 'delta_w', 'delta_w_a_out': 'delta_w', 'delta_b_gates': 'delta_w', 'delta_g_head': 'delta_w', 'delta_w_b_out': 'delta_w', 'delta_w_o': 'delta_w', 'delta_g_ple': 'delta_w', 'delta_w_ple_gate': 'delta_w', 'delta_w_ple': 'delta_w', 'delta_g_final': 'delta_w', 'new_m_g_mix': 'new_m', 'new_m_w_in': 'new_m', 'new_m_conv_w': 'new_m', 'new_m_conv_b': 'new_m', 'new_m_w_a_out': 'new_m', 'new_m_b_gates': 'new_m', 'new_m_g_head': 'new_m', 'new_m_w_b_out': 'new_m', 'new_m_w_o': 'new_m', 'new_m_g_ple': 'new_m', 'new_m_w_ple_gate': 'new_m', 'new_m_w_ple': 'new_m', 'new_m_g_final': 'new_m', 'new_v_g_mix': 'new_v', 'new_v_w_in': 'new_v', 'new_v_conv_w': 'new_v', 'new_v_conv_b': 'new_v', 'new_v_w_a_out': 'new_v', 'new_v_b_gates': 'new_v', 'new_v_g_head': 'new_v', 'new_v_w_b_out': 'new_v', 'new_v_w_o': 'new_v', 'new_v_g_ple': 'new_v', 'new_v_w_ple_gate': 'new_v', 'new_v_w_ple': 'new_v', 'new_v_g_final': 'new_v'}


def _forward(args):
    return _fwd_reference(*[args[k] for k in FWD_PARAMS])


def _output_shape():
    def fwd():
        inp = _fwd_setup_inputs(0)
        return _fwd_reference(*[inp[k] for k in FWD_PARAMS])
    out = _jax.eval_shape(fwd)
    return out.shape, out.dtype

N_MICROBATCH = 1
ADAM_LR = 0.001
ADAM_B1 = 0.9
ADAM_B2 = 0.999
ADAM_EPS = 1e-08
ADAM_WD = 0.01
ADAM_STEP = 10
PER_EXAMPLE_BATCH_AXIS = {'x': 0, 'p': 1, 'loss_target': 0}
SHARED_INPUTS = []
_WEIGHT_DTYPES = {'g_mix': _jnp.float32, 'w_in': _jnp.float32, 'conv_w': _jnp.float32, 'conv_b': _jnp.float32, 'w_a_out': _jnp.float32, 'b_gates': _jnp.float32, 'g_head': _jnp.float32, 'w_b_out': _jnp.float32, 'w_o': _jnp.float32, 'g_ple': _jnp.float32, 'w_ple_gate': _jnp.float32, 'w_ple': _jnp.float32, 'g_final': _jnp.float32}
MOMENT_SCALE = {'g_mix': 2.604455e-01, 'w_in': 6.337723e-02, 'conv_w': 9.942455e-02, 'conv_b': 1.063604e-01, 'w_a_out': 9.863862e-02, 'b_gates': 1.539393e+00, 'g_head': 3.821457e-02, 'w_b_out': 5.311933e-02, 'w_o': 1.118841e-01, 'g_ple': 6.582734e-02, 'w_ple_gate': 6.318045e-02, 'w_ple': 1.625195e-01, 'g_final': 1.280759e+02}


def _to_microbatches(a, axis):
    t = _jnp.moveaxis(a, axis, 0)
    t = t.reshape((N_MICROBATCH, t.shape[0] // N_MICROBATCH) + t.shape[1:])
    return _jnp.moveaxis(t, 1, axis + 1)


def setup_inputs(seed: int = 0) -> dict:
    inp = _fwd_setup_inputs(seed)
    key = _jax.random.fold_in(_jax.random.key(seed), 7919)
    shape, _ = _output_shape()
    out = dict(inp)
    out["loss_target"] = _jax.random.normal(_jax.random.fold_in(key, 0), shape, _jnp.float32)
    for i, name in enumerate(TWIN_WEIGHTS):
        w = inp[name].astype(_jnp.float32)
        if MOMENT_SCALE is None:
            s = _jnp.sqrt(_jnp.mean(_jnp.square(w)) + 1e-30)
        else:
            s = MOMENT_SCALE[name]
        km, kv = _jax.random.split(_jax.random.fold_in(key, i + 1))
        out[name] = w
        out["m_" + name] = s * _jax.random.normal(km, w.shape, _jnp.float32)
        out["v_" + name] = (s * s) * _jax.random.uniform(kv, w.shape, _jnp.float32, 0.5, 1.5)
    if N_MICROBATCH > 1:
        for name, axis in PER_EXAMPLE_BATCH_AXIS.items():
            out[name] = _to_microbatches(out[name], axis)
    return {'x': out['x'], 'p': out['p'], 'g_mix': out['g_mix'], 'w_in': out['w_in'], 'conv_w': out['conv_w'], 'conv_b': out['conv_b'], 'w_a_out': out['w_a_out'], 'b_gates': out['b_gates'], 'g_head': out['g_head'], 'w_b_out': out['w_b_out'], 'w_o': out['w_o'], 'g_ple': out['g_ple'], 'w_ple_gate': out['w_ple_gate'], 'w_ple': out['w_ple'], 'g_final': out['g_final'], 'loss_target': out['loss_target'], 'm_g_mix': out['m_g_mix'], 'm_w_in': out['m_w_in'], 'm_conv_w': out['m_conv_w'], 'm_conv_b': out['m_conv_b'], 'm_w_a_out': out['m_w_a_out'], 'm_b_gates': out['m_b_gates'], 'm_g_head': out['m_g_head'], 'm_w_b_out': out['m_w_b_out'], 'm_w_o': out['m_w_o'], 'm_g_ple': out['m_g_ple'], 'm_w_ple_gate': out['m_w_ple_gate'], 'm_w_ple': out['m_w_ple'], 'm_g_final': out['m_g_final'], 'v_g_mix': out['v_g_mix'], 'v_w_in': out['v_w_in'], 'v_conv_w': out['v_conv_w'], 'v_conv_b': out['v_conv_b'], 'v_w_a_out': out['v_w_a_out'], 'v_b_gates': out['v_b_gates'], 'v_g_head': out['v_g_head'], 'v_w_b_out': out['v_w_b_out'], 'v_w_o': out['v_w_o'], 'v_g_ple': out['v_g_ple'], 'v_w_ple_gate': out['v_w_ple_gate'], 'v_w_ple': out['v_w_ple'], 'v_g_final': out['v_g_final']}


def _loss(weights, diff, rest, loss_target):
    with _jax.named_scope("forward"):
        args = {**rest, TWIN_DIFF_INPUT: diff, **{k: w.astype(_WEIGHT_DTYPES[k]) for k, w in weights.items()}}
        y = _forward(args)
    with _jax.named_scope("loss_head"):
        err = _jnp.square(y.astype(_jnp.float32) - loss_target)
        return 0.5 * _jnp.sum(_jnp.mean(err, axis=-1)) if err.ndim else 0.5 * err


def _adamw(w, g, m, v):
    m = ADAM_B1 * m + (1.0 - ADAM_B1) * g
    v = ADAM_B2 * v + (1.0 - ADAM_B2) * _jnp.square(g)
    m_hat = m / (1.0 - ADAM_B1 ** ADAM_STEP)
    v_hat = v / (1.0 - ADAM_B2 ** ADAM_STEP)
    delta = -ADAM_LR * (m_hat / (_jnp.sqrt(v_hat) + ADAM_EPS) + ADAM_WD * w)
    return delta, m, v


def reference(x, p, g_mix, w_in, conv_w, conv_b, w_a_out, b_gates, g_head, w_b_out, w_o, g_ple, w_ple_gate, w_ple, g_final, loss_target, m_g_mix, m_w_in, m_conv_w, m_conv_b, m_w_a_out, m_b_gates, m_g_head, m_w_b_out, m_w_o, m_g_ple, m_w_ple_gate, m_w_ple, m_g_final, v_g_mix, v_w_in, v_conv_w, v_conv_b, v_w_a_out, v_b_gates, v_g_head, v_w_b_out, v_w_o, v_g_ple, v_w_ple_gate, v_w_ple, v_g_final):
    given = dict(x=x, p=p, g_mix=g_mix, w_in=w_in, conv_w=conv_w, conv_b=conv_b, w_a_out=w_a_out, b_gates=b_gates, g_head=g_head, w_b_out=w_b_out, w_o=w_o, g_ple=g_ple, w_ple_gate=w_ple_gate, w_ple=w_ple, g_final=g_final, loss_target=loss_target, m_g_mix=m_g_mix, m_w_in=m_w_in, m_conv_w=m_conv_w, m_conv_b=m_conv_b, m_w_a_out=m_w_a_out, m_b_gates=m_b_gates, m_g_head=m_g_head, m_w_b_out=m_w_b_out, m_w_o=m_w_o, m_g_ple=m_g_ple, m_w_ple_gate=m_w_ple_gate, m_w_ple=m_w_ple, m_g_final=m_g_final, v_g_mix=v_g_mix, v_w_in=v_w_in, v_conv_w=v_conv_w, v_conv_b=v_conv_b, v_w_a_out=v_w_a_out, v_b_gates=v_b_gates, v_g_head=v_g_head, v_w_b_out=v_w_b_out, v_w_o=v_w_o, v_g_ple=v_g_ple, v_w_ple_gate=v_w_ple_gate, v_w_ple=v_w_ple, v_g_final=v_g_final)
    weights = {n: given[n] for n in TWIN_WEIGHTS}
    shared = {n: given[n] for n in SHARED_INPUTS}
    per_example = {n: given[n] for n in ['x', 'p']}
    grad_fn = _jax.value_and_grad(_loss, argnums=(0, 1))

    def one_microbatch(ex, loss_target):
        ex = dict(ex)
        diff = ex.pop(TWIN_DIFF_INPUT)
        return grad_fn(weights, diff, {**shared, **ex}, loss_target)

    if N_MICROBATCH == 1:
        loss, (grad_w, grad_x) = one_microbatch(per_example, given["loss_target"])
    else:
        def body(carry, xs):
            loss_sum, grad_sum = carry
            l_k, (gw_k, gx_k) = one_microbatch(xs[0], xs[1])
            with _jax.named_scope("update"):
                return (loss_sum + l_k, _jax.tree.map(_jnp.add, grad_sum, gw_k)), gx_k

        init = (_jnp.zeros((), _jnp.float32), _jax.tree.map(_jnp.zeros_like, weights))
        (loss, grad_w), grad_x = _jax.lax.scan(body, init, (per_example, given["loss_target"]))
    with _jax.named_scope("update"):
        delta_w, new_m, new_v = {}, {}, {}
        for n in TWIN_WEIGHTS:
            delta_w[n], new_m[n], new_v[n] = _adamw(weights[n], grad_w[n], given["m_" + n], given["v_" + n])
    return (loss, grad_x, *[grad_w[n] for n in TWIN_WEIGHTS], *[delta_w[n] for n in TWIN_WEIGHTS],
            *[new_m[n] for n in TWIN_WEIGHTS], *[new_v[n] for n in TWIN_WEIGHTS])
```

```python
import jax
import jax.numpy as jnp
from jax import lax
from jax.experimental import pallas as pl
from jax.experimental.pallas import tpu as pltpu

F32 = jnp.float32
MXU = jnp.bfloat16
WIRE = jnp.bfloat16

D = 1024
NH, DK, DV = 4, 256, 512
VD = NH * DV
PLE = 256
LCH = 256
EPS = 1e-6
N_IN = 14344
NMAIN = 14336
GATE_W = 128
N_CHIPS, N_DEV = 4, 8

C_BA, C_ZA, C_O, C_ZB, C_GA, C_GB = 0, 1024, 2048, 4096, 6144, 7168
C_QKV, C_XA, C_CA = 8192, 12288, 13312
HEAD_W = 2 * DK + DV
TAIL_W = 8192
CONV_W = 2048
SEGMENTS = (
    (0, 1024, C_XA), (1024, 1024, C_BA), (2048, 1024, C_CA), (3072, 1024, C_ZA),
    (8192, 2048, C_O), (10240, 2048, C_ZB), (12296, 1024, C_GA), (13320, 1024, C_GB),
) + tuple((4096 + DK * h, DK, C_QKV + HEAD_W * h) for h in range(NH)) + tuple(
    (5120 + DK * h, DK, C_QKV + HEAD_W * h + DK) for h in range(NH)) + tuple(
    (6144 + DV * h, DV, C_QKV + HEAD_W * h + 2 * DK) for h in range(NH))
GATE_COL = 12288

PACK_ROWS = (256, 512, 256, 256, 64)
PACK_TOTAL = sum(PACK_ROWS)
SMALL_ROWS = 16

ADAM_LR, ADAM_B1, ADAM_B2, ADAM_EPS, ADAM_WD, ADAM_STEP = 0.001, 0.9, 0.999, 1e-08, 0.01, 10

VMEM_LIMIT = 56 * 1024 * 1024
MESH = pl.DeviceIdType.MESH
ANY = pl.BlockSpec(memory_space=pl.ANY)


def _cparams(*sem):
    return pltpu.CompilerParams(dimension_semantics=sem, vmem_limit_bytes=VMEM_LIMIT)


def _dot(a, b):
    return jnp.dot(a, b, preferred_element_type=F32)


def _dot_nt(a, b):
    return lax.dot_general(a, b, (((1,), (1,)), ((), ())), preferred_element_type=F32)


def _dot_tn(a, b):
    return lax.dot_general(a, b, (((0,), (0,)), ((), ())), preferred_element_type=F32)


def _sigmoid(x):
    return 1.0 / (1.0 + jnp.exp(-x))


def _logsig(x):
    return jnp.minimum(x, 0.0) - jnp.log(1.0 + jnp.exp(-jnp.abs(x)))


def _rstd(x):
    return lax.rsqrt(jnp.mean(x * x, axis=-1, keepdims=True) + EPS)


def _norm_bwd(dy, xhat, r, g):
    dxh = dy * g
    return r * (dxh - xhat * jnp.mean(dxh * xhat, axis=-1, keepdims=True))


def _f32(ref):
    return ref[...].astype(F32)


def _prenorm(x, g_mix, wg):
    n = x.shape[0]
    tm = min(512, n)

    def body(x_ref, g_ref, wg_ref, hn_ref, gate_ref):
        xv = x_ref[...]
        hn = (xv * _rstd(xv) * g_ref[...]).astype(MXU)
        hn_ref[...] = hn
        gate_ref[...] = _dot(hn, wg_ref[...])

    return pl.pallas_call(
        body, name="prenorm", grid=(n // tm,),
        in_specs=[pl.BlockSpec((tm, D), lambda i: (i, 0)),
                  pl.BlockSpec((1, D), lambda i: (0, 0)),
                  pl.BlockSpec((D, GATE_W), lambda i: (0, 0))],
        out_specs=[pl.BlockSpec((tm, D), lambda i: (i, 0)),
                   pl.BlockSpec((tm, GATE_W), lambda i: (i, 0))],
        out_shape=[jax.ShapeDtypeStruct((n, D), MXU), jax.ShapeDtypeStruct((n, GATE_W), F32)],
        compiler_params=_cparams("arbitrary"),
    )(x, g_mix, wg)


def _matmul(a, b, out_dtype, name, tm=2048, tn=1024):
    m, k = a.shape
    n = b.shape[1]
    tm, tn = min(tm, m), min(tn, n)

    def body(a_ref, b_ref, o_ref):
        o_ref[...] = _dot(a_ref[...], b_ref[...]).astype(out_dtype)

    return pl.pallas_call(
        body, name=name, grid=(n // tn, m // tm),
        in_specs=[pl.BlockSpec((tm, k), lambda j, i: (i, 0)),
                  pl.BlockSpec((k, tn), lambda j, i: (0, j))],
        out_specs=pl.BlockSpec((tm, tn), lambda j, i: (i, j)),
        out_shape=jax.ShapeDtypeStruct((m, n), out_dtype),
        compiler_params=_cparams("arbitrary", "arbitrary"),
    )(a, b)


def _matmul_tn(a, b, name, ta=1024, tb=1024, tk=1024):
    n, ka = a.shape
    kb = b.shape[1]
    ta, tb, tk = min(ta, ka), min(tb, kb), min(tk, n)
    nk = n // tk

    def body(a_ref, b_ref, o_ref, acc):
        kk = pl.program_id(2)

        @pl.when(kk == 0)
        def _():
            acc[...] = jnp.zeros_like(acc)

        acc[...] += _dot_tn(a_ref[...].astype(MXU), b_ref[...].astype(MXU))

        @pl.when(kk == nk - 1)
        def _():
            o_ref[...] = acc[...]

    return pl.pallas_call(
        body, name=name, grid=(ka // ta, kb // tb, nk),
        in_specs=[pl.BlockSpec((tk, ta), lambda i, j, kk: (kk, i)),
                  pl.BlockSpec((tk, tb), lambda i, j, kk: (kk, j))],
        out_specs=pl.BlockSpec((ta, tb), lambda i, j, kk: (i, j)),
        out_shape=jax.ShapeDtypeStruct((ka, kb), F32),
        scratch_shapes=[pltpu.VMEM((ta, tb), F32)],
        compiler_params=_cparams("arbitrary", "arbitrary", "arbitrary"),
    )(a, b)


def _gate_vectors(g, hd):
    gt = g.T[0:8, :]
    lane = lax.broadcasted_iota(jnp.int32, g.shape, 1)
    sub = lax.broadcasted_iota(jnp.int32, gt.shape, 0)
    col = lambda j: jnp.sum(jnp.where(lane == j, g, 0.0), axis=1, keepdims=True)
    row = lambda j: jnp.sum(jnp.where(sub == j, gt, 0.0), axis=0, keepdims=True)
    return col(hd), col(hd + NH), row(hd), row(hd + NH)


def _chunk_decay(li_col, li_row, lf_col, lf_row, m_prev):
    n = li_col.shape[0]
    r = lax.broadcasted_iota(jnp.int32, (n, n), 0)
    c = lax.broadcasted_iota(jnp.int32, (n, n), 1)
    tri = r >= c
    b_col = jnp.sum(jnp.where(tri, lf_row, 0.0), axis=1, keepdims=True)
    b_row = jnp.sum(jnp.where(r <= c, lf_col, 0.0), axis=0, keepdims=True)
    b_last = jnp.sum(lf_row, axis=1, keepdims=True)
    dmat = jnp.where(tri, b_col - b_row + li_row, -jnp.inf)
    a_col = b_col + m_prev
    g_col = b_last - b_col + li_col
    m_new = jnp.maximum(b_last + m_prev, jnp.max(g_col, axis=0, keepdims=True))
    w_col = jnp.exp(g_col - m_new)
    decay = jnp.exp(b_last + m_prev - m_new)
    return tri, dmat, a_col, m_new, w_col, decay


def _qkv_specs(row_of):
    base = C_QKV // DK
    q_spec = pl.BlockSpec((LCH, DK), lambda h, c: (row_of(c), base + (HEAD_W // DK) * h))
    k_spec = pl.BlockSpec((LCH, DK), lambda h, c: (row_of(c), base + (HEAD_W // DK) * h + 1))
    v_spec = pl.BlockSpec((LCH, DV), lambda h, c: (row_of(c), C_QKV // DV + (HEAD_W // DV) * h + 1))
    return q_spec, k_spec, v_spec


def _mlstm_fwd(proj, gates, bias):
    n = proj.shape[0]
    nc = n // LCH

    def body(q_ref, k_ref, v_ref, g_ref, bias_ref,
             h_ref, den_ref, mrow_ref, cs_ref, ns_ref, ms_ref, c_scr, n_scr, m_scr):
        hd = pl.program_id(0)
        c = pl.program_id(1)

        @pl.when(c == 0)
        def _():
            c_scr[...] = jnp.zeros_like(c_scr)
            n_scr[...] = jnp.zeros_like(n_scr)
            m_scr[...] = jnp.full_like(m_scr, -jnp.inf)

        g = g_ref[...] + bias_ref[...]
        li_col, fr_col, li_row, fr_row = _gate_vectors(g, hd)
        m_prev = m_scr[0:1, 0:1]
        tri, dmat, a_col, m_new, w_col, decay = _chunk_decay(
            li_col, li_row, _logsig(fr_col), _logsig(fr_row), m_prev)
        m_col = jnp.maximum(a_col, jnp.max(dmat, axis=1, keepdims=True))
        dl = jnp.exp(dmat - m_col)
        inter = jnp.exp(a_col - m_col)

        qs = q_ref[...] * (DK ** -0.5)
        kk = k_ref[...]
        vv = v_ref[...]
        cst = c_scr[...]
        nst = n_scr[...]
        cs_ref[...] = cst
        ns_ref[...] = nst
        ms_ref[...] = m_scr[...]

        sc = _dot_nt(qs, kk) * dl
        num = _dot(sc.astype(MXU), vv) + inter * _dot(qs, cst.astype(MXU))
        den = (jnp.sum(sc, axis=1, keepdims=True)
               + inter * jnp.sum(qs.astype(F32) * nst, axis=1, keepdims=True))
        nrm = jnp.maximum(jnp.abs(den), jnp.exp(-m_col))
        h_ref[...] = num / nrm
        den_ref[...] = den
        mrow_ref[...] = m_col

        kw = kk.astype(F32) * w_col
        c_scr[...] = decay * cst + _dot_tn(kw.astype(MXU), vv)
        n_scr[...] = decay * nst + jnp.sum(kw, axis=0, keepdims=True)
        m_scr[...] = jnp.broadcast_to(m_new, m_scr.shape)

    q_spec, k_spec, v_spec = _qkv_specs(lambda c: c)
    col_spec = pl.BlockSpec((None, LCH, 1), lambda h, c: (h, c, 0))
    return pl.pallas_call(
        body, name="mlstm_fwd", grid=(NH, nc),
        in_specs=[q_spec, k_spec, v_spec,
                  pl.BlockSpec((LCH, GATE_W), lambda h, c: (c, 0)),
                  pl.BlockSpec((1, GATE_W), lambda h, c: (0, 0))],
        out_specs=[pl.BlockSpec((LCH, DV), lambda h, c: (c, h)),
                   col_spec, col_spec,
                   pl.BlockSpec((None, None, DK, DV), lambda h, c: (h, c, 0, 0)),
                   pl.BlockSpec((None, None, 1, DK), lambda h, c: (h, c, 0, 0)),
                   pl.BlockSpec((None, None, 1, GATE_W), lambda h, c: (h, c, 0, 0))],
        out_shape=[jax.ShapeDtypeStruct((n, VD), F32),
                   jax.ShapeDtypeStruct((NH, n, 1), F32),
                   jax.ShapeDtypeStruct((NH, n, 1), F32),
                   jax.ShapeDtypeStruct((NH, nc, DK, DV), F32),
                   jax.ShapeDtypeStruct((NH, nc, 1, DK), F32),
                   jax.ShapeDtypeStruct((NH, nc, 1, GATE_W), F32)],
        scratch_shapes=[pltpu.VMEM((DK, DV), F32), pltpu.VMEM((1, DK), F32),
                        pltpu.VMEM((1, GATE_W), F32)],
        compiler_params=_cparams("arbitrary", "arbitrary"),
    )(proj, proj, proj, gates, bias)


def _mlstm_bwd(proj, gates, bias, h, dh, den, mrow, cs, ns, ms, dproj):
    n = proj.shape[0]
    nc = n // LCH

    def body(q_ref, k_ref, v_ref, g_ref, bias_ref, h_ref, dh_ref, den_ref, mrow_ref,
             cs_ref, ns_ref, ms_ref, dproj_in,
             dqkv_ref, di_ref, df_ref, gsum_ref, dc_scr, dn_scr):
        del dproj_in
        hd = pl.program_id(0)
        step = pl.program_id(1)

        @pl.when(step == 0)
        def _():
            dc_scr[...] = jnp.zeros_like(dc_scr)
            dn_scr[...] = jnp.zeros_like(dn_scr)
            gsum_ref[...] = jnp.zeros_like(gsum_ref)

        g = g_ref[...] + bias_ref[...]
        li_col, fr_col, li_row, fr_row = _gate_vectors(g, hd)
        m_prev = ms_ref[0:1, 0:1]
        tri, dmat, a_col, m_new, w_col, decay = _chunk_decay(
            li_col, li_row, _logsig(fr_col), _logsig(fr_row), m_prev)
        m_col = mrow_ref[...]
        dl = jnp.exp(dmat - m_col)
        inter = jnp.exp(a_col - m_col)

        qs = q_ref[...] * (DK ** -0.5)
        kk = k_ref[...]
        vv = v_ref[...]
        qf = qs.astype(F32)
        kf = kk.astype(F32)
        cst = cs_ref[...]
        nst = ns_ref[...]
        cb = cst.astype(MXU)
        dcn = dc_scr[...]
        dnn = dn_scr[...]
        dcb = dcn.astype(MXU)

        den = den_ref[...]
        floor = jnp.exp(-m_col)
        nrm = jnp.maximum(jnp.abs(den), floor)
        dhv = dh_ref[...]
        dnum = dhv / nrm
        dnum_b = dnum.astype(MXU)
        dhh = jnp.sum(dhv * h_ref[...], axis=1, keepdims=True)
        dden = jnp.where(jnp.abs(den) > floor, -dhh / nrm * jnp.sign(den), 0.0)

        sc = _dot_nt(qs, kk) * dl
        dsc = _dot_nt(dnum_b, vv) + dden
        da = (dl * dsc).astype(MXU)
        gmat = sc * dsc

        dq = _dot(da, kk) + inter * (_dot_nt(dnum_b, cb) + dden * nst)
        dk_state = w_col * (_dot_nt(vv, dcb) + dnn)
        dk = _dot_tn(da, qs) + dk_state
        kw = (kf * w_col).astype(MXU)
        dv = _dot_tn(sc.astype(MXU), dnum_b) + _dot(kw, dcb)
        dqkv_ref[:, 0:DK] = (dq * (DK ** -0.5)).astype(dqkv_ref.dtype)
        dqkv_ref[:, DK:2 * DK] = dk.astype(dqkv_ref.dtype)
        dqkv_ref[:, 2 * DK:HEAD_W] = dv.astype(dqkv_ref.dtype)

        num_i = _dot(qs, cb)
        den_i = jnp.sum(qf * nst, axis=1, keepdims=True)
        e_col = inter * (jnp.sum(dnum * num_i, axis=1, keepdims=True) + dden * den_i)
        h_col = jnp.sum(kf * dk_state, axis=1, keepdims=True)
        f_dec = decay * (jnp.sum(jnp.sum(cst * dcn, axis=1, keepdims=True), axis=0, keepdims=True)
                         + jnp.sum(nst * dnn, axis=1, keepdims=True))
        r = lax.broadcasted_iota(jnp.int32, (LCH, LCH), 0)
        c = lax.broadcasted_iota(jnp.int32, (LCH, LCH), 1)
        eye = r == c
        to_col = lambda row: jnp.sum(jnp.where(eye, row, 0.0), axis=1, keepdims=True)
        row_g = jnp.sum(gmat, axis=1, keepdims=True)
        col_g = to_col(jnp.sum(gmat, axis=0, keepdims=True))
        last = lax.broadcasted_iota(jnp.int32, (LCH, 1), 0) == LCH - 1
        db_col = row_g - col_g + e_col - h_col
        db_col = db_col + jnp.where(last, jnp.sum(h_col, axis=0, keepdims=True) + f_dec, 0.0)
        dli_col = col_g + h_col
        dlf_row = jnp.sum(jnp.where(tri, db_col, 0.0), axis=0, keepdims=True)
        df_col = to_col(dlf_row) * _sigmoid(-fr_col)
        di_ref[...] = dli_col
        df_ref[...] = df_col
        gsum_ref[0:1, :] += jnp.broadcast_to(jnp.sum(dli_col, axis=0, keepdims=True), (1, GATE_W))
        gsum_ref[1:2, :] += jnp.broadcast_to(jnp.sum(df_col, axis=0, keepdims=True), (1, GATE_W))

        dc_scr[...] = decay * dcn + _dot_tn((qf * inter).astype(MXU), dnum_b)
        dn_scr[...] = decay * dnn + jnp.sum(qf * (inter * dden), axis=0, keepdims=True)

    rev = lambda c: nc - 1 - c
    q_spec, k_spec, v_spec = _qkv_specs(rev)
    hv_spec = pl.BlockSpec((LCH, DV), lambda h, c: (rev(c), h))
    col_spec = pl.BlockSpec((None, LCH, 1), lambda h, c: (h, rev(c), 0))
    return pl.pallas_call(
        body, name="mlstm_bwd", grid=(NH, nc),
        in_specs=[q_spec, k_spec, v_spec,
                  pl.BlockSpec((LCH, GATE_W), lambda h, c: (rev(c), 0)),
                  pl.BlockSpec((1, GATE_W), lambda h, c: (0, 0)),
                  hv_spec, hv_spec, col_spec, col_spec,
                  pl.BlockSpec((None, None, DK, DV), lambda h, c: (h, rev(c), 0, 0)),
                  pl.BlockSpec((None, None, 1, DK), lambda h, c: (h, rev(c), 0, 0)),
                  pl.BlockSpec((None, None, 1, GATE_W), lambda h, c: (h, rev(c), 0, 0)),
                  ANY],
        out_specs=[pl.BlockSpec((LCH, HEAD_W), lambda h, c: (rev(c), C_QKV // HEAD_W + h)),
                   col_spec, col_spec,
                   pl.BlockSpec((None, 8, GATE_W), lambda h, c: (h, 0, 0))],
        out_shape=[jax.ShapeDtypeStruct(dproj.shape, dproj.dtype),
                   jax.ShapeDtypeStruct((NH, n, 1), F32), jax.ShapeDtypeStruct((NH, n, 1), F32),
                   jax.ShapeDtypeStruct((NH, 8, GATE_W), F32)],
        scratch_shapes=[pltpu.VMEM((DK, DV), F32), pltpu.VMEM((1, DK), F32)],
        input_output_aliases={12: 0},
        compiler_params=_cparams("arbitrary", "arbitrary"),
    )(proj, proj, proj, gates, bias, h, dh, den, mrow, cs, ns, ms, dproj)


def _proj_spec(tm, col, width):
    return pl.BlockSpec((tm, width), lambda i: (i, col // width))


def _halo_prev(tm, col):
    return pl.BlockSpec((8, 1024), lambda i: (jnp.maximum(i * (tm // 8) - 1, 0), col // 1024))


def _const(shape):
    return pl.BlockSpec(shape, lambda i: (0,) * len(shape))


def _conv_inputs(i, tm, xa_ref, ca_ref, xah_ref, cah_ref):
    u = _f32(xa_ref) * _f32(ca_ref)
    uh = jnp.where(i > 0, _f32(xah_ref) * _f32(cah_ref), 0.0)
    rid = lax.broadcasted_iota(jnp.int32, u.shape, 0)
    u1 = jnp.where(rid == 0, uh[7:8, :], pltpu.roll(u, 1, 0))
    u2 = jnp.where(rid == 0, uh[6:7, :], jnp.where(rid == 1, uh[7:8, :], pltpu.roll(u, 2, 0)))
    return u, u1, u2


def _head_norm(hh, gh):
    out = []
    for j in range(NH):
        hj = hh[:, j * DV:(j + 1) * DV]
        rj = _rstd(hj)
        out.append((hj * rj, rj, gh[:, j * DV:(j + 1) * DV]))
    return out


def _tail_fwd(proj, h, x, p, t, cw, cb, gh, gple, gfin, wa, wb, wo, wpg, wp):
    n = x.shape[0]
    tm = min(256, n)

    def body(ba_r, za_r, o_r, zb_r, ga_r, gb_r, xa_r, ca_r, xah_r, cah_r, h_r, x_r, p_r, t_r,
             cw_r, cb_r, gh_r, gple_r, gfin_r, wa_r, wb_r, wo_r, wpg_r, wp_r,
             apre_o, bpre_o, mg_o, xn1_o, de_o, dgp_o, ya_o, yb_o, x1_o, dx2_o, acc_o):
        i = pl.program_id(0)

        @pl.when(i == 0)
        def _():
            acc_o[...] = jnp.zeros_like(acc_o)

        u, u1, u2 = _conv_inputs(i, tm, xa_r, ca_r, xah_r, cah_r)
        cwv = cw_r[...]
        cv = cwv[0:1, :] * u2 + cwv[1:2, :] * u1 + cwv[2:3, :] * u + cb_r[...]
        za = _f32(za_r)
        a_pre = (_f32(ba_r) * cv * (za * _sigmoid(za))).astype(MXU)
        apre_o[...] = a_pre
        ya = _dot(a_pre, wa_r[...])

        hn = _head_norm(h_r[...], gh_r[...])
        hbn = jnp.concatenate([xh * g for xh, _, g in hn], axis=1)
        zb = _f32(zb_r)
        b_pre = (_sigmoid(_f32(o_r)) * hbn * (zb * _sigmoid(zb))).astype(MXU)
        bpre_o[...] = b_pre
        yb = _dot(b_pre, wb_r[...])
        ya_o[...] = ya.astype(MXU)
        yb_o[...] = yb.astype(MXU)

        mg = (_sigmoid(_f32(ga_r)) * ya + _sigmoid(_f32(gb_r)) * yb).astype(MXU)
        mg_o[...] = mg
        x1 = x_r[...] + _dot(mg, wo_r[...])
        x1_o[...] = x1
        xn1 = (x1 * _rstd(x1) * gple_r[...]).astype(MXU)
        xn1_o[...] = xn1
        gt = _sigmoid(_dot(xn1, wpg_r[...]))
        e = _dot(p_r[...].astype(MXU), wp_r[...])
        x2 = x1 + gt * e
        r2 = _rstd(x2)
        xh2 = x2 * r2
        gf = gfin_r[...]
        diff = xh2 * gf - t_r[...]
        dy = diff * (1.0 / D)
        dx2 = _norm_bwd(dy, xh2, r2, gf)
        dx2_o[...] = dx2
        de_o[...] = (dx2 * gt).astype(MXU)
        dgp_o[...] = (dx2 * e * gt * (1.0 - gt)).astype(MXU)
        acc_o[0:1, :] += jnp.sum(dy * xh2, axis=0, keepdims=True)
        loss = 0.5 * jnp.sum(jnp.sum(diff * diff, axis=1, keepdims=True), axis=0, keepdims=True) * (1.0 / D)
        acc_o[1:2, :] += jnp.broadcast_to(loss, (1, D))

    row = lambda w, dt: (pl.BlockSpec((tm, w), lambda i: (i, 0)), jax.ShapeDtypeStruct((n, w), dt))
    outs = [row(D, MXU), row(VD, MXU), row(D, MXU), row(D, MXU), row(D, MXU), row(D, MXU),
            row(D, MXU), row(D, MXU), row(D, F32), row(D, F32),
            (_const((8, D)), jax.ShapeDtypeStruct((8, D), F32))]
    return pl.pallas_call(
        body, name="tail_fwd", grid=(n // tm,),
        in_specs=[_proj_spec(tm, C_BA, 1024), _proj_spec(tm, C_ZA, 1024),
                  _proj_spec(tm, C_O, 2048), _proj_spec(tm, C_ZB, 2048),
                  _proj_spec(tm, C_GA, 1024), _proj_spec(tm, C_GB, 1024),
                  _proj_spec(tm, C_XA, 1024), _proj_spec(tm, C_CA, 1024),
                  _halo_prev(tm, C_XA), _halo_prev(tm, C_CA),
                  pl.BlockSpec((tm, VD), lambda i: (i, 0)),
                  pl.BlockSpec((tm, D), lambda i: (i, 0)),
                  pl.BlockSpec((tm, PLE), lambda i: (i, 0)),
                  pl.BlockSpec((tm, D), lambda i: (i, 0)),
                  _const((8, D)), _const((1, D)), _const((1, VD)), _const((1, D)), _const((1, D)),
                  _const((D, D)), _const((VD, D)), _const((D, D)), _const((D, D)), _const((PLE, D))],
        out_specs=[s for s, _ in outs],
        out_shape=[s for _, s in outs],
        compiler_params=_cparams("arbitrary"),
    )(*([proj] * 10), h, x, p, t, cw, cb, gh, gple, gfin, wa, wb, wo, wpg, wp)


def _tail_bwd(proj, h, dgp, dx2, x1, ya, yb, cw, cb, gh, gple, wpg_t, wo_t, wb_t, wa_t):
    n = x1.shape[0]
    tm = min(256, n)

    def body(ba_r, za_r, o_r, zb_r, ga_r, gb_r, xa_r, ca_r, xah_r, cah_r, h_r,
             dgp_r, dx2_r, x1_r, ya_r, yb_r, cw_r, cb_r, gh_r, gple_r,
             wpgt_r, wot_r, wbt_r, wat_r,
             dproj_o, dcv_o, dh_o, dx1_o, dx1b_o, dya_o, dyb_o, acc_o):
        i = pl.program_id(0)

        @pl.when(i == 0)
        def _():
            acc_o[...] = jnp.zeros_like(acc_o)

        dxn1 = _dot(dgp_r[...], wpgt_r[...])
        x1 = x1_r[...]
        r1 = _rstd(x1)
        xh1 = x1 * r1
        acc_o[0:1, 0:D] += jnp.sum(dxn1 * xh1, axis=0, keepdims=True)
        dx1 = dx2_r[...] + _norm_bwd(dxn1, xh1, r1, gple_r[...])
        dx1_o[...] = dx1
        dx1b = dx1.astype(MXU)
        dx1b_o[...] = dx1b

        dmg = _dot(dx1b, wot_r[...])
        sga = _sigmoid(_f32(ga_r))
        sgb = _sigmoid(_f32(gb_r))
        dya = (dmg * sga).astype(MXU)
        dyb = (dmg * sgb).astype(MXU)
        dya_o[...] = dya
        dyb_o[...] = dyb
        dproj_o[:, C_GA:C_GA + D] = (dmg * _f32(ya_r) * sga * (1.0 - sga)).astype(MXU)
        dproj_o[:, C_GB:C_GB + D] = (dmg * _f32(yb_r) * sgb * (1.0 - sgb)).astype(MXU)

        db_pre = _dot(dyb, wbt_r[...])
        hn = _head_norm(h_r[...], gh_r[...])
        hbn = jnp.concatenate([xh * g for xh, _, g in hn], axis=1)
        so = _sigmoid(_f32(o_r))
        zb = _f32(zb_r)
        szb = _sigmoid(zb)
        sb = zb * szb
        dproj_o[:, C_O:C_O + VD] = (db_pre * hbn * sb * so * (1.0 - so)).astype(MXU)
        dproj_o[:, C_ZB:C_ZB + VD] = (db_pre * so * hbn * szb * (1.0 + zb * (1.0 - szb))).astype(MXU)
        dhbn = db_pre * so * sb
        for j, (xh, rj, g) in enumerate(hn):
            dj = dhbn[:, j * DV:(j + 1) * DV]
            acc_o[1:2, j * DV:(j + 1) * DV] += jnp.sum(dj * xh, axis=0, keepdims=True)
            dh_o[:, j * DV:(j + 1) * DV] = _norm_bwd(dj, xh, rj, g)

        da_pre = _dot(dya, wat_r[...])
        u, u1, u2 = _conv_inputs(i, tm, xa_r, ca_r, xah_r, cah_r)
        cwv = cw_r[...]
        cv = cwv[0:1, :] * u2 + cwv[1:2, :] * u1 + cwv[2:3, :] * u + cb_r[...]
        za = _f32(za_r)
        sza = _sigmoid(za)
        sa = za * sza
        ba = _f32(ba_r)
        dproj_o[:, C_BA:C_BA + D] = (da_pre * cv * sa).astype(MXU)
        dproj_o[:, C_ZA:C_ZA + D] = (da_pre * ba * cv * sza * (1.0 + za * (1.0 - sza))).astype(MXU)
        dcv = da_pre * ba * sa
        dcv_o[...] = dcv.astype(MXU)
        acc_o[2:3, 0:D] += jnp.sum(dcv, axis=0, keepdims=True)
        acc_o[3:4, 0:D] += jnp.sum(dcv * u2, axis=0, keepdims=True)
        acc_o[4:5, 0:D] += jnp.sum(dcv * u1, axis=0, keepdims=True)
        acc_o[5:6, 0:D] += jnp.sum(dcv * u, axis=0, keepdims=True)

    row = lambda w, dt: (pl.BlockSpec((tm, w), lambda i: (i, 0)), jax.ShapeDtypeStruct((n, w), dt))
    outs = [(pl.BlockSpec((tm, TAIL_W), lambda i: (i, 0)), jax.ShapeDtypeStruct((n, NMAIN), MXU)),
            row(D, MXU), row(VD, F32), row(D, F32), row(D, MXU), row(D, MXU), row(D, MXU),
            (_const((8, VD)), jax.ShapeDtypeStruct((8, VD), F32))]
    rowin = lambda w: pl.BlockSpec((tm, w), lambda i: (i, 0))
    return pl.pallas_call(
        body, name="tail_bwd", grid=(n // tm,),
        in_specs=[_proj_spec(tm, C_BA, 1024), _proj_spec(tm, C_ZA, 1024),
                  _proj_spec(tm, C_O, 2048), _proj_spec(tm, C_ZB, 2048),
                  _proj_spec(tm, C_GA, 1024), _proj_spec(tm, C_GB, 1024),
                  _proj_spec(tm, C_XA, 1024), _proj_spec(tm, C_CA, 1024),
                  _halo_prev(tm, C_XA), _halo_prev(tm, C_CA),
                  rowin(VD), rowin(D), rowin(D), rowin(D), rowin(D), rowin(D),
                  _const((8, D)), _const((1, D)), _const((1, VD)), _const((1, D)),
                  _const((D, D)), _const((D, D)), _const((D, VD)), _const((D, D))],
        out_specs=[s for s, _ in outs],
        out_shape=[s for _, s in outs],
        compiler_params=_cparams("arbitrary"),
    )(*([proj] * 10), h, dgp, dx2, x1, ya, yb, cw, cb, gh, gple, wpg_t, wo_t, wb_t, wa_t)


def _conv_bwd(proj, dcv, cw, dproj):
    n = dcv.shape[0]
    tm = min(512, n)
    nt = n // tm

    def body(xa_r, ca_r, dcv_r, nxt_r, cw_r, dproj_in, dxc_o):
        del dproj_in
        i = pl.program_id(0)
        dcv_v = _f32(dcv_r)
        nxt = jnp.where(i < nt - 1, _f32(nxt_r), 0.0)
        rid = lax.broadcasted_iota(jnp.int32, dcv_v.shape, 0)
        d1 = jnp.where(rid == tm - 1, nxt[0:1, :], pltpu.roll(dcv_v, tm - 1, 0))
        d2 = jnp.where(rid == tm - 2, nxt[0:1, :],
                       jnp.where(rid == tm - 1, nxt[1:2, :], pltpu.roll(dcv_v, tm - 2, 0)))
        cwv = cw_r[...]
        du = cwv[2:3, :] * dcv_v + cwv[1:2, :] * d1 + cwv[0:1, :] * d2
        dxc_o[:, 0:D] = (du * _f32(ca_r)).astype(MXU)
        dxc_o[:, D:2 * D] = (du * _f32(xa_r)).astype(MXU)

    return pl.pallas_call(
        body, name="conv_bwd", grid=(nt,),
        in_specs=[_proj_spec(tm, C_XA, 1024), _proj_spec(tm, C_CA, 1024),
                  pl.BlockSpec((tm, D), lambda i: (i, 0)),
                  pl.BlockSpec((8, D), lambda i: (jnp.minimum((i + 1) * (tm // 8), n // 8 - 1), 0)),
                  _const((8, D)), ANY],
        out_specs=pl.BlockSpec((tm, CONV_W), lambda i: (i, C_XA // CONV_W)),
        out_shape=jax.ShapeDtypeStruct(dproj.shape, dproj.dtype),
        input_output_aliases={5: 0},
        compiler_params=_cparams("arbitrary"),
    )(proj, proj, dcv, dcv, cw, dproj)


def _input_grad(dproj, dgates, w_t, wg_t, x, dx1, g_mix):
    n = x.shape[0]
    tm, tk = min(1024, n), 2048
    nk = NMAIN // tk

    def body(dp_r, dg_r, wt_r, wgt_r, x_r, dx1_r, g_r, gx_o, acc_o, acc):
        i = pl.program_id(0)
        kk = pl.program_id(1)

        @pl.when((i == 0) & (kk == 0))
        def _():
            acc_o[...] = jnp.zeros_like(acc_o)

        @pl.when(kk == 0)
        def _():
            acc[...] = _dot(dg_r[...], wgt_r[...])

        acc[...] += _dot(dp_r[...], wt_r[...])

        @pl.when(kk == nk - 1)
        def _():
            dhn = acc[...]
            xv = x_r[...]
            r0 = _rstd(xv)
            xh = xv * r0
            acc_o[0:1, :] += jnp.sum(dhn * xh, axis=0, keepdims=True)
            gx_o[...] = dx1_r[...] + _norm_bwd(dhn, xh, r0, g_r[...])

    return pl.pallas_call(
        body, name="input_grad", grid=(n // tm, nk),
        in_specs=[pl.BlockSpec((tm, tk), lambda i, kk: (i, kk)),
                  pl.BlockSpec((tm, GATE_W), lambda i, kk: (i, 0)),
                  pl.BlockSpec((tk, D), lambda i, kk: (kk, 0)),
                  pl.BlockSpec((GATE_W, D), lambda i, kk: (0, 0)),
                  pl.BlockSpec((tm, D), lambda i, kk: (i, 0)),
                  pl.BlockSpec((tm, D), lambda i, kk: (i, 0)),
                  pl.BlockSpec((1, D), lambda i, kk: (0, 0))],
        out_specs=[pl.BlockSpec((tm, D), lambda i, kk: (i, 0)),
                   pl.BlockSpec((8, D), lambda i, kk: (0, 0))],
        out_shape=[jax.ShapeDtypeStruct((n, D), F32), jax.ShapeDtypeStruct((8, D), F32)],
        scratch_shapes=[pltpu.VMEM((tm, D), F32)],
        compiler_params=_cparams("arbitrary", "arbitrary"),
    )(dproj, dgates, w_t, wg_t, x, dx1, g_mix)


def _pack_small(acc_f, acc_b, acc_x, gsum):
    def body(f_r, b_r, x_r, s_r, o_r):
        o_r[...] = jnp.zeros_like(o_r)
        o_r[0:1, :] = x_r[0:1, :]
        o_r[1:2, :] = b_r[2:3, 0:D]
        o_r[2:3, :] = b_r[1:2, 0:D]
        o_r[3:4, :] = b_r[1:2, D:2 * D]
        o_r[4:5, :] = b_r[0:1, 0:D]
        o_r[5:6, :] = f_r[0:1, :]
        o_r[6:9, :] = b_r[3:6, 0:D]
        lane = lax.broadcasted_iota(jnp.int32, (1, D), 1)
        last = jnp.where(lane == 2 * NH, f_r[1:2, :], 0.0)
        for hd in range(NH):
            last = jnp.where(lane == hd, s_r[hd, 0:1, 0:1], last)
            last = jnp.where(lane == NH + hd, s_r[hd, 1:2, 0:1], last)
        o_r[9:10, :] = last

    return pl.pallas_call(
        body, name="pack_small",
        out_shape=jax.ShapeDtypeStruct((SMALL_ROWS, D), F32),
    )(acc_f, acc_b, acc_x, gsum)


def _sum_slots(r, name, tr=64):
    s, rows, w = r.shape
    tr = min(tr, rows)
    assert rows % tr == 0

    def body(r_ref, o_ref):
        tot = r_ref[0].astype(F32)
        for k in range(1, s):
            tot = tot + r_ref[k].astype(F32)
        o_ref[...] = tot

    return pl.pallas_call(
        body, name=name, grid=(rows // tr,),
        in_specs=[pl.BlockSpec((s, tr, w), lambda i: (0, i, 0))],
        out_specs=pl.BlockSpec((tr, w), lambda i: (i, 0)),
        out_shape=jax.ShapeDtypeStruct((rows, w), F32),
        compiler_params=_cparams("arbitrary"),
    )(r)


def _adamw(w, g, m, v, name, tr=64):
    rows, cols = w.shape
    tr = min(tr, rows)
    assert rows % tr == 0
    c1 =1.0 - ADAM_B1 ** ADAM_STEP
    c2 = 1.0 - ADAM_B2 ** ADAM_STEP

    def body(w_r, g_r, m_r, v_r, d_o, m_o, v_o):
        gv = g_r[...]
        mn = ADAM_B1 * m_r[...] + (1.0 - ADAM_B1) * gv
        vn = ADAM_B2 * v_r[...] + (1.0 - ADAM_B2) * (gv * gv)
        m_o[...] = mn
        v_o[...] = vn
        d_o[...] = -ADAM_LR * ((mn / c1) / (jnp.sqrt(vn / c2) + ADAM_EPS) + ADAM_WD * w_r[...])

    spec = pl.BlockSpec((tr, cols), lambda i: (i, 0))
    shp = jax.ShapeDtypeStruct((rows, cols), F32)
    return pl.pallas_call(
        body, name=name, grid=(rows // tr,),
        in_specs=[spec] * 4, out_specs=[spec] * 3, out_shape=[shp] * 3,
        compiler_params=_cparams("arbitrary"),
    )(w, g, m, v)


def _position():
    return lax.axis_index("x"), lax.axis_index("y"), lax.axis_index("c")


def _flip(v, bit):
    return 1 - v if bit else v


def _allgather_chips(arrs):
    na = len(arrs)
    relations = ((1, 0), (0, 1), (1, 1))

    def body(*refs):
        ins, outs = refs[:na], refs[na:2 * na]
        send_sems, recv_sems, local_sems = refs[2 * na:]
        x, y, c = _position()
        mine = 2 * x + y
        local = [pltpu.make_async_copy(ins[a], outs[a].at[mine], local_sems.at[a]) for a in range(na)]
        for cp in local:
            cp.start()
        remote = []
        for j, (rx, ry) in enumerate(relations):
            for a in range(na):
                remote.append(pltpu.make_async_remote_copy(
                    src_ref=ins[a], dst_ref=outs[a].at[mine],
                    send_sem=send_sems.at[j * na + a], recv_sem=recv_sems.at[j * na + a],
                    device_id=(_flip(x, rx), _flip(y, ry), c), device_id_type=MESH))
        for cp in remote:
            cp.start()
        for cp in remote:
            cp.wait()
        for cp in local:
            cp.wait()

    return pl.pallas_call(
        body, name="allgather_weights",
        in_specs=[ANY] * na, out_specs=[ANY] * na,
        out_shape=[jax.ShapeDtypeStruct((N_CHIPS,) + a.shape, a.dtype) for a in arrs],
        scratch_shapes=[pltpu.SemaphoreType.DMA((3 * na,)), pltpu.SemaphoreType.DMA((3 * na,)),
                        pltpu.SemaphoreType.DMA((na,))],
    )(*arrs)


def _scatter_grads(g_in, g_rest, small):
    r_in, r_rest = g_in.shape[1] // 2, g_rest.shape[1] // 2

    def body(gin, grest, sm, oin, orest, osm, send_sems, recv_sems, local_sems):
        x, y, c = _position()
        me = 4 * x + 2 * y + c
        copies = []
        for r in range(N_DEV):
            px, py, pc = _flip(x, (r >> 2) & 1), _flip(y, (r >> 1) & 1), _flip(c, r & 1)
            chip = 2 * px + py
            srcs = (gin.at[chip, pl.ds(pl.multiple_of(pc * r_in, r_in), r_in)],
                    grest.at[chip, pl.ds(pl.multiple_of(pc * r_rest, r_rest), r_rest)],
                    sm)
            dsts = (oin.at[me], orest.at[me], osm.at[me])
            for a in range(3):
                if r == 0:
                    copies.append(pltpu.make_async_copy(srcs[a], dsts[a], local_sems.at[a]))
                else:
                    k = (r - 1) * 3 + a
                    copies.append(pltpu.make_async_remote_copy(
                        src_ref=srcs[a], dst_ref=dsts[a],
                        send_sem=send_sems.at[k], recv_sem=recv_sems.at[k],
                        device_id=(px, py, pc), device_id_type=MESH))
        for cp in copies:
            cp.start()
        for cp in copies:
            cp.wait()

    nrem = 3 * (N_DEV - 1)
    return pl.pallas_call(
        body, name="scatter_grads",
        in_specs=[ANY] * 3, out_specs=[ANY] * 3,
        out_shape=[jax.ShapeDtypeStruct((N_DEV, r_in) + g_in.shape[2:], g_in.dtype),
                   jax.ShapeDtypeStruct((N_DEV, r_rest) + g_rest.shape[2:], g_rest.dtype),
                   jax.ShapeDtypeStruct((N_DEV,) + small.shape, small.dtype)],
        scratch_shapes=[pltpu.SemaphoreType.DMA((nrem,)), pltpu.SemaphoreType.DMA((nrem,)),
                        pltpu.SemaphoreType.DMA((3,))],
    )(g_in, g_rest, small)


def _join_halves(halves):
    na = len(halves)

    def body(*refs):
        ins, outs = refs[:na], refs[na:2 * na]
        send_sems, recv_sems, local_sems = refs[2 * na:]
        x, y, c = _position()
        copies = []
        for a in range(na):
            rows = ins[a].shape[0]
            dst = outs[a].at[pl.ds(pl.multiple_of(c * rows, rows), rows)]
            copies.append(pltpu.make_async_copy(ins[a], dst, local_sems.at[a]))
            copies.append(pltpu.make_async_remote_copy(
                src_ref=ins[a], dst_ref=dst, send_sem=send_sems.at[a], recv_sem=recv_sems.at[a],
                device_id=(x, y, 1 - c), device_id_type=MESH))
        for cp in copies:
            cp.start()
        for cp in copies:
            cp.wait()

    return pl.pallas_call(
        body, name="join_halves",
        in_specs=[ANY] * na, out_specs=[ANY] * na,
        out_shape=[jax.ShapeDtypeStruct((2 * a.shape[0],) + a.shape[1:], a.dtype) for a in halves],
        scratch_shapes=[pltpu.SemaphoreType.DMA((na,)), pltpu.SemaphoreType.DMA((na,)),
                        pltpu.SemaphoreType.DMA((na,))],
    )(*halves)


def _to_internal(w_glob):
    order = sorted(SEGMENTS, key=lambda s: s[2])
    main = jnp.concatenate([w_glob[..., g0:g0 + w] for g0, w, _ in order], axis=-1)
    gate = w_glob[..., GATE_COL:GATE_COL + 2 * NH]
    pad = [(0, 0)] * (w_glob.ndim - 1) + [(0, GATE_W - 2 * NH)]
    return main, jnp.pad(gate, pad)


def _to_global(main, gate):
    parts = sorted([(g0, main[..., i0:i0 + w]) for g0, w, i0 in SEGMENTS]
                   + [(GATE_COL, gate[..., 0:2 * NH])], key=lambda s: s[0])
    return jnp.concatenate([p for _, p in parts], axis=-1)


def _pack_rows(wa, wb, wo, wpg, wp):
    return jnp.concatenate([wa, wb, wo, wpg, wp.reshape(PACK_ROWS[4], D)], axis=0)


def _unpack_rows(pk):
    offs = [0]
    for r in PACK_ROWS:
        offs.append(offs[-1] + r)
    parts = [pk[offs[k]:offs[k + 1]] for k in range(5)]
    parts[4] = parts[4].reshape(PLE, PLE)
    return parts


def _pad_rows(a, rows=8):
    return jnp.pad(a, ((0, rows - a.shape[0]), (0, 0)))


def _local_step(x, p, t, g_mix, w_main, w_gate, cw, conv_b, wa, b_gates, g_head, wb, wo, g_ple, wpg, wp,
                g_final):
    bias = jnp.pad(b_gates, ((0, 0), (0, GATE_W - 2 * NH)))
    hn, gates = _prenorm(x, g_mix, w_gate)
    proj = _matmul(hn, w_main, MXU, "proj")
    h, den, mrow, cs, ns, ms = _mlstm_fwd(proj, gates, bias)
    (a_pre, b_pre, mg, xn1, de, dgp, ya, yb, x1, dx2, acc_f) = _tail_fwd(
        proj, h, x, p, t, cw, conv_b, g_head, g_ple, g_final, wa, wb, wo, wpg, wp)
    dproj, dcv, dh, dx1, dx1b, dya, dyb, acc_b = _tail_bwd(
        proj, h, dgp, dx2, x1, ya, yb, cw, conv_b, g_head, g_ple, wpg.T, wo.T, wb.T, wa.T)
    dproj = _conv_bwd(proj, dcv, cw, dproj)
    dproj, di, df, gsum = _mlstm_bwd(proj, gates, bias, h, dh, den, mrow, cs, ns, ms, dproj)
    dgates = jnp.pad(jnp.concatenate([di[:, :, 0].T, df[:, :, 0].T], axis=1),
                     ((0, 0), (0, GATE_W - 2 * NH))).astype(MXU)
    grad_x, acc_x = _input_grad(dproj, dgates, w_main.T, w_gate.T, x, dx1, g_mix)
    d_main = _matmul_tn(hn, dproj, "dw_in")
    d_gate = _matmul_tn(hn, dgates, "dw_gate")
    d_wa = _matmul_tn(a_pre, dya, "dw_a_out")
    d_wb = _matmul_tn(b_pre, dyb, "dw_b_out")
    d_wo = _matmul_tn(mg, dx1b, "dw_o")
    d_wpg = _matmul_tn(xn1, dgp, "dw_ple_gate")
    d_wp = _matmul_tn(p, de, "dw_ple")
    small = _pack_small(acc_f, acc_b, acc_x, gsum)
    return grad_x, d_main, d_gate, d_wa, d_wb, d_wo, d_wpg, d_wp, small


def kernel(x, p, g_mix, w_in, conv_w, conv_b, w_a_out, b_gates, g_head, w_b_out, w_o, g_ple, w_ple_gate, w_ple, g_final, loss_target, m_g_mix, m_w_in, m_conv_w, m_conv_b, m_w_a_out, m_b_gates, m_g_head, m_w_b_out, m_w_o, m_g_ple, m_w_ple_gate, m_w_ple, m_g_final, v_g_mix, v_w_in, v_conv_w, v_conv_b, v_w_a_out, v_b_gates, v_g_head, v_w_b_out, v_w_o, v_g_ple, v_w_ple_gate, v_w_ple, v_g_final):
    chip = 2 * lax.axis_index("x") + lax.axis_index("y")

    pack_w = _pack_rows(w_a_out[0], w_b_out[0], w_o[0], w_ple_gate[0], w_ple[0])
    g_win, g_pack, g_cw = _allgather_chips(
        [w_in[0].astype(MXU), pack_w.astype(MXU), _pad_rows(conv_w[0])])
    w_glob = jnp.transpose(g_win, (1, 0, 2)).reshape(D, N_IN)
    w_main, w_gate = _to_internal(w_glob)
    offs = [0, 256, 768, 1024, 1280, 1344]
    wa, wb, wo, wpg = [g_pack[:, offs[k]:offs[k + 1]].reshape(-1, D) for k in range(4)]
    wp = jnp.transpose(g_pack[:, offs[4]:offs[5]].reshape(N_CHIPS, PLE, PLE), (1, 0, 2)).reshape(PLE, D)
    cw = jnp.transpose(g_cw, (1, 0, 2)).reshape(8, D)

    grad_x, d_main, d_gate, d_wa, d_wb, d_wo, d_wpg, d_wp, small = _local_step(
        x[0], p[0, 0], loss_target[0], g_mix, w_main, w_gate, cw, conv_b, wa, b_gates, g_head, wb, wo,
        g_ple, wpg, wp, g_final.reshape(1, D))

    g_in = jnp.transpose(_to_global(d_main, d_gate).astype(WIRE).reshape(D, N_CHIPS, N_IN // N_CHIPS), (1, 0, 2))
    d_wp_c = jnp.transpose(d_wp.reshape(PLE, N_CHIPS, PLE), (1, 0, 2)).reshape(N_CHIPS, PACK_ROWS[4], D)
    g_rest = jnp.concatenate(
        [d_wa.reshape(N_CHIPS, -1, D), d_wb.reshape(N_CHIPS, -1, D), d_wo.reshape(N_CHIPS, -1, D),
         d_wpg.reshape(N_CHIPS, -1, D), d_wp_c], axis=1).astype(WIRE)
    r_in, r_rest, r_small = _scatter_grads(g_in, g_rest, small)
    gw_in, gw_rest = _join_halves([_sum_slots(r_in, "sum_w_in"), _sum_slots(r_rest, "sum_rest", tr=96)])
    gs = _sum_slots(r_small, "sum_small", tr=SMALL_ROWS)

    d_in, nm_in, nv_in = _adamw(w_in[0], gw_in, m_w_in[0], v_w_in[0], "adamw_w_in")
    packs = [_pack_rows(a[0], b[0], c_[0], d_[0], e_[0]) for a, b, c_, d_, e_ in (
        (w_a_out, w_b_out, w_o, w_ple_gate, w_ple),
        (m_w_a_out, m_w_b_out, m_w_o, m_w_ple_gate, m_w_ple),
        (v_w_a_out, v_w_b_out, v_w_o, v_w_ple_gate, v_w_ple))]
    d_rest, nm_rest, nv_rest = _adamw(packs[0], gw_rest, packs[1], packs[2], "adamw_rest")

    lane = lax.broadcasted_iota(jnp.int32, (1, D), 1)
    g_small = jnp.concatenate([gs[0:6], jnp.where(lane < 2 * NH, gs[9:10], 0.0), jnp.zeros((1, D), F32)], axis=0)

    def small_pack(gm, cb_, bg, gh, gp, gf):
        return jnp.concatenate([gm, cb_, gh.reshape(2, D), gp, gf.reshape(1, D),
                                jnp.pad(bg, ((0, 0), (0, D - 2 * NH))), jnp.zeros((1, D), F32)], axis=0)

    ws = small_pack(g_mix, conv_b, b_gates, g_head, g_ple, g_final)
    ms_ = small_pack(m_g_mix, m_conv_b, m_b_gates, m_g_head, m_g_ple, m_g_final)
    vs = small_pack(v_g_mix, v_conv_b, v_b_gates, v_g_head, v_g_ple, v_g_final)
    d_s, nm_s, nv_s = _adamw(ws, g_small, ms_, vs, "adamw_small")
    g_cw_mine = lax.dynamic_slice(gs[6:9], (0, chip * PLE), (3, PLE))
    d_c, nm_c, nv_c = _adamw(_pad_rows(conv_w[0]), _pad_rows(g_cw_mine), _pad_rows(m_conv_w[0]),
                             _pad_rows(v_conv_w[0]), "adamw_conv_w")

    def leaves(big_in, rest, sm, cwv):
        ra, rb, ro, rpg, rp = _unpack_rows(rest)
        return [sm[0:1], big_in[None], cwv[0:3][None], sm[1:2], ra[None], sm[6:7, 0:2 * NH],
                sm[2:4].reshape(1, VD), rb[None], ro[None], sm[4:5], rpg[None], rp[None], sm[5]]

    loss = gs[9, 2 * NH]
    grads = leaves(gw_in, gw_rest, g_small, _pad_rows(g_cw_mine))
    deltas = leaves(d_in, d_rest, d_s, d_c)
    new_m = leaves(nm_in, nm_rest, nm_s, nm_c)
    new_v = leaves(nv_in, nv_rest, nv_s, nv_c)
    return (loss, grad_x[None], *grads, *deltas, *new_m, *new_v)
```

```python
import jax
import jax.numpy as jnp
from jax import lax
from jax.experimental import pallas as pl
from jax.experimental.pallas import tpu as pltpu

F32 = jnp.float32
MXU = jnp.bfloat16
WIRE = jnp.bfloat16

D = 1024
NH, DK, DV = 4, 256, 512
VD = NH * DV
PLE = 256
LCH = 256
EPS = 1e-6
N_IN = 14344
NMAIN = 14336
GATE_W = 128
N_CHIPS, N_DEV = 4, 8

C_BA, C_ZA, C_O, C_ZB, C_GA, C_GB = 0, 1024, 2048, 4096, 6144, 7168
C_QKV, C_XA, C_CA = 8192, 12288, 13312
HEAD_W = 2 * DK + DV
TAIL_W = 8192
CONV_W = 2048
SEGMENTS = (
    (0, 1024, C_XA), (1024, 1024, C_BA), (2048, 1024, C_CA), (3072, 1024, C_ZA),
    (8192, 2048, C_O), (10240, 2048, C_ZB), (12296, 1024, C_GA), (13320, 1024, C_GB),
) + tuple((4096 + DK * h, DK, C_QKV + HEAD_W * h) for h in range(NH)) + tuple(
    (5120 + DK * h, DK, C_QKV + HEAD_W * h + DK) for h in range(NH)) + tuple(
    (6144 + DV * h, DV, C_QKV + HEAD_W * h + 2 * DK) for h in range(NH))
GATE_COL = 12288

PACK_ROWS = (256, 512, 256, 256, 64)
PACK_TOTAL = sum(PACK_ROWS)
SMALL_ROWS = 16

ADAM_LR, ADAM_B1, ADAM_B2, ADAM_EPS, ADAM_WD, ADAM_STEP = 0.001, 0.9, 0.999, 1e-08, 0.01, 10

VMEM_LIMIT = 56 * 1024 * 1024
MESH = pl.DeviceIdType.MESH
ANY = pl.BlockSpec(memory_space=pl.ANY)


def _cparams(*sem):
    return pltpu.CompilerParams(dimension_semantics=sem, vmem_limit_bytes=VMEM_LIMIT)


def _dot(a, b):
    return jnp.dot(a, b, preferred_element_type=F32)


def _dot_nt(a, b):
    return lax.dot_general(a, b, (((1,), (1,)), ((), ())), preferred_element_type=F32)


def _dot_tn(a, b):
    return lax.dot_general(a, b, (((0,), (0,)), ((), ())), preferred_element_type=F32)


def _sigmoid(x):
    return 1.0 / (1.0 + jnp.exp(-x))


def _logsig(x):
    return jnp.minimum(x, 0.0) - jnp.log(1.0 + jnp.exp(-jnp.abs(x)))


def _rstd(x):
    return lax.rsqrt(jnp.mean(x * x, axis=-1, keepdims=True) + EPS)


def _norm_bwd(dy, xhat, r, g):
    dxh = dy * g
    return r * (dxh - xhat * jnp.mean(dxh * xhat, axis=-1, keepdims=True))


def _f32(ref):
    return ref[...].astype(F32)


def _prenorm(x, g_mix, wg):
    n = x.shape[0]
    tm = min(512, n)

    def body(x_ref, g_ref, wg_ref, hn_ref, gate_ref):
        xv = x_ref[...]
        hn = (xv * _rstd(xv) * g_ref[...]).astype(MXU)
        hn_ref[...] = hn
        gate_ref[...] = _dot(hn, wg_ref[...])

    return pl.pallas_call(
        body, name="prenorm", grid=(n // tm,),
        in_specs=[pl.BlockSpec((tm, D), lambda i: (i, 0)),
                  pl.BlockSpec((1, D), lambda i: (0, 0)),
                  pl.BlockSpec((D, GATE_W), lambda i: (0, 0))],
        out_specs=[pl.BlockSpec((tm, D), lambda i: (i, 0)),
                   pl.BlockSpec((tm, GATE_W), lambda i: (i, 0))],
        out_shape=[jax.ShapeDtypeStruct((n, D), MXU), jax.ShapeDtypeStruct((n, GATE_W), F32)],
        compiler_params=_cparams("arbitrary"),
    )(x, g_mix, wg)


def _matmul(a, b, out_dtype, name, tm=2048, tn=1024):
    m, k = a.shape
    n = b.shape[1]
    tm, tn = min(tm, m), min(tn, n)

    def body(a_ref, b_ref, o_ref):
        o_ref[...] = _dot(a_ref[...], b_ref[...]).astype(out_dtype)

    return pl.pallas_call(
        body, name=name, grid=(n // tn, m // tm),
        in_specs=[pl.BlockSpec((tm, k), lambda j, i: (i, 0)),
                  pl.BlockSpec((k, tn), lambda j, i: (0, j))],
        out_specs=pl.BlockSpec((tm, tn), lambda j, i: (i, j)),
        out_shape=jax.ShapeDtypeStruct((m, n), out_dtype),
        compiler_params=_cparams("arbitrary", "arbitrary"),
    )(a, b)


def _matmul_tn(a, b, name, out_dtype=F32, ta=1024, tb=1024, tk=1024):
    n, ka = a.shape
    kb = b.shape[1]
    ta, tb, tk = min(ta, ka), min(tb, kb), min(tk, n)
    nk = n // tk

    def body(a_ref, b_ref, o_ref, acc):
        kk = pl.program_id(2)

        @pl.when(kk == 0)
        def _():
            acc[...] = jnp.zeros_like(acc)

        acc[...] += _dot_tn(a_ref[...].astype(MXU), b_ref[...].astype(MXU))

        @pl.when(kk == nk - 1)
        def _():
            o_ref[...] = acc[...].astype(out_dtype)

    return pl.pallas_call(
        body, name=name, grid=(ka // ta, kb // tb, nk),
        in_specs=[pl.BlockSpec((tk, ta), lambda i, j, kk: (kk, i)),
                  pl.BlockSpec((tk, tb), lambda i, j, kk: (kk, j))],
        out_specs=pl.BlockSpec((ta, tb), lambda i, j, kk: (i, j)),
        out_shape=jax.ShapeDtypeStruct((ka, kb), out_dtype),
        scratch_shapes=[pltpu.VMEM((ta, tb), F32)],
        compiler_params=_cparams("arbitrary", "arbitrary", "arbitrary"),
    )(a, b)


def _gate_vectors(g, hd):
    gt = g.T[0:8, :]
    lane = lax.broadcasted_iota(jnp.int32, g.shape, 1)
    sub = lax.broadcasted_iota(jnp.int32, gt.shape, 0)
    col = lambda j: jnp.sum(jnp.where(lane == j, g, 0.0), axis=1, keepdims=True)
    row = lambda j: jnp.sum(jnp.where(sub == j, gt, 0.0), axis=0, keepdims=True)
    return col(hd), col(hd + NH), row(hd), row(hd + NH)


def _chunk_decay(li_col, li_row, lf_col, lf_row, m_prev):
    n = li_col.shape[0]
    r = lax.broadcasted_iota(jnp.int32, (n, n), 0)
    c = lax.broadcasted_iota(jnp.int32, (n, n), 1)
    tri = r >= c
    b_col = jnp.sum(jnp.where(tri, lf_row, 0.0), axis=1, keepdims=True)
    b_row = jnp.sum(jnp.where(r <= c, lf_col, 0.0), axis=0, keepdims=True)
    b_last = jnp.sum(lf_row, axis=1, keepdims=True)
    dmat = jnp.where(tri, b_col - b_row + li_row, -jnp.inf)
    a_col = b_col + m_prev
    g_col = b_last - b_col + li_col
    m_new = jnp.maximum(b_last + m_prev, jnp.max(g_col, axis=0, keepdims=True))
    w_col = jnp.exp(g_col - m_new)
    decay = jnp.exp(b_last + m_prev - m_new)
    return tri, dmat, a_col, m_new, w_col, decay


def _qkv_specs(row_of):
    base = C_QKV // DK
    q_spec = pl.BlockSpec((LCH, DK), lambda c, h: (row_of(c), base + (HEAD_W // DK) * h))
    k_spec = pl.BlockSpec((LCH, DK), lambda c, h: (row_of(c), base + (HEAD_W // DK) * h + 1))
    v_spec = pl.BlockSpec((LCH, DV), lambda c, h: (row_of(c), C_QKV // DV + (HEAD_W // DV) * h + 1))
    return q_spec, k_spec, v_spec


def _lane_put(col, lane_id, width=GATE_W):
    lane = lax.broadcasted_iota(jnp.int32, (col.shape[0], width), 1)
    return jnp.where(lane == lane_id, col, 0.0)


def _lane_get(block, lane_id):
    lane = lax.broadcasted_iota(jnp.int32, block.shape, 1)
    return jnp.sum(jnp.where(lane == lane_id, block, 0.0), axis=1, keepdims=True)


def _mlstm_fwd(proj, gates, bias):
    n = proj.shape[0]
    nc = n // LCH

    def body(q_ref, k_ref, v_ref, g_ref, bias_ref,
             h_ref, st_ref, cs_ref, ns_ref, ms_ref, c_scr, n_scr, m_scr):
        c = pl.program_id(0)
        hd = pl.program_id(1)

        @pl.when(c == 0)
        def _():
            c_scr[hd] = jnp.zeros((DK, DV), F32)
            n_scr[hd] = jnp.zeros((1, DK), F32)
            m_scr[hd] = jnp.full((1, GATE_W), -jnp.inf, F32)

        @pl.when(hd == 0)
        def _():
            st_ref[...] = jnp.zeros_like(st_ref)

        g = g_ref[...] + bias_ref[...]
        li_col, fr_col, li_row, fr_row = _gate_vectors(g, hd)
        m_all = m_scr[hd]
        m_prev = m_all[0:1, 0:1]
        tri, dmat, a_col, m_new, w_col, decay = _chunk_decay(
            li_col, li_row, _logsig(fr_col), _logsig(fr_row), m_prev)
        m_col = jnp.maximum(a_col, jnp.max(dmat, axis=1, keepdims=True))
        dl = jnp.exp(dmat - m_col)
        inter = jnp.exp(a_col - m_col)

        qs = q_ref[...] * (DK ** -0.5)
        kk = k_ref[...]
        vv = v_ref[...]
        cst = c_scr[hd]
        nst = n_scr[hd]
        cs_ref[...] = cst
        ns_ref[...] = nst
        ms_ref[...] = m_all

        sc = _dot_nt(qs, kk) * dl
        num = _dot(sc.astype(MXU), vv) + inter * _dot(qs, cst.astype(MXU))
        den = (jnp.sum(sc, axis=1, keepdims=True)
               + inter * jnp.sum(qs.astype(F32) * nst, axis=1, keepdims=True))
        nrm = jnp.maximum(jnp.abs(den), jnp.exp(-m_col))
        h_ref[...] = num / nrm
        st_ref[...] += _lane_put(den, hd) + _lane_put(m_col, NH + hd)

        kw = kk.astype(F32) * w_col
        c_scr[hd] = decay * cst + _dot_tn(kw.astype(MXU), vv)
        n_scr[hd] = decay * nst + jnp.sum(kw, axis=0, keepdims=True)
        m_scr[hd] = jnp.broadcast_to(m_new, (1, GATE_W))

    q_spec, k_spec, v_spec = _qkv_specs(lambda c: c)
    return pl.pallas_call(
        body, name="mlstm_fwd", grid=(nc, NH),
        in_specs=[q_spec, k_spec, v_spec,
                  pl.BlockSpec((LCH, GATE_W), lambda c, h: (c, 0)),
                  pl.BlockSpec((1, GATE_W), lambda c, h: (0, 0))],
        out_specs=[pl.BlockSpec((LCH, DV), lambda c, h: (c, h)),
                   pl.BlockSpec((LCH, GATE_W), lambda c, h: (c, 0)),
                   pl.BlockSpec((None, None, DK, DV), lambda c, h: (h, c, 0, 0)),
                   pl.BlockSpec((None, None, 1, DK), lambda c, h: (h, c, 0, 0)),
                   pl.BlockSpec((None, None, 1, GATE_W), lambda c, h: (h, c, 0, 0))],
        out_shape=[jax.ShapeDtypeStruct((n, VD), F32),
                   jax.ShapeDtypeStruct((n, GATE_W), F32),
                   jax.ShapeDtypeStruct((NH, nc, DK, DV), F32),
                   jax.ShapeDtypeStruct((NH, nc, 1, DK), F32),
                   jax.ShapeDtypeStruct((NH, nc, 1, GATE_W), F32)],
        scratch_shapes=[pltpu.VMEM((NH, DK, DV), F32), pltpu.VMEM((NH, 1, DK), F32),
                        pltpu.VMEM((NH, 1, GATE_W), F32)],
        compiler_params=_cparams("arbitrary", "arbitrary"),
    )(proj, proj, proj, gates, bias)


def _mlstm_bwd(proj, gates, bias, h, dh, stats, cs, ns, ms, dproj):
    n = proj.shape[0]
    nc = n // LCH

    def body(q_ref, k_ref, v_ref, g_ref, bias_ref, h_ref, dh_ref, st_ref,
             cs_ref, ns_ref, ms_ref, dproj_in,
             dqkv_ref, dg_ref, gsum_ref, dc_scr, dn_scr):
        del dproj_in
        step = pl.program_id(0)
        hd = pl.program_id(1)

        @pl.when(step == 0)
        def _():
            dc_scr[hd] = jnp.zeros((DK, DV), F32)
            dn_scr[hd] = jnp.zeros((1, DK), F32)

        @pl.when((step == 0) & (hd == 0))
        def _():
            gsum_ref[...] = jnp.zeros_like(gsum_ref)

        @pl.when(hd == 0)
        def _():
            dg_ref[...] = jnp.zeros_like(dg_ref)

        g = g_ref[...] + bias_ref[...]
        li_col, fr_col, li_row, fr_row = _gate_vectors(g, hd)
        m_prev = ms_ref[0:1, 0:1]
        tri, dmat, a_col, m_new, w_col, decay = _chunk_decay(
            li_col, li_row, _logsig(fr_col), _logsig(fr_row), m_prev)
        stats = st_ref[...]
        m_col = _lane_get(stats, NH + hd)
        dl = jnp.exp(dmat - m_col)
        inter = jnp.exp(a_col - m_col)

        qs = q_ref[...] * (DK ** -0.5)
        kk = k_ref[...]
        vv = v_ref[...]
        qf = qs.astype(F32)
        kf = kk.astype(F32)
        cst = cs_ref[...]
        nst = ns_ref[...]
        cb = cst.astype(MXU)
        dcn = dc_scr[hd]
        dnn = dn_scr[hd]
        dcb = dcn.astype(MXU)

        den = _lane_get(stats, hd)
        floor = jnp.exp(-m_col)
        nrm = jnp.maximum(jnp.abs(den), floor)
        dhv = dh_ref[...]
        dnum = dhv / nrm
        dnum_b = dnum.astype(MXU)
        dhh = jnp.sum(dhv * h_ref[...], axis=1, keepdims=True)
        dden = jnp.where(jnp.abs(den) > floor, -dhh / nrm * jnp.sign(den), 0.0)

        sc = _dot_nt(qs, kk) * dl
        dsc = _dot_nt(dnum_b, vv) + dden
        da = (dl * dsc).astype(MXU)
        gmat = sc * dsc

        dq = _dot(da, kk) + inter * (_dot_nt(dnum_b, cb) + dden * nst)
        dk_state = w_col * (_dot_nt(vv, dcb) + dnn)
        dk = _dot_tn(da, qs) + dk_state
        kw = (kf * w_col).astype(MXU)
        dv = _dot_tn(sc.astype(MXU), dnum_b) + _dot(kw, dcb)
        dqkv_ref[:, 0:DK] = (dq * (DK ** -0.5)).astype(dqkv_ref.dtype)
        dqkv_ref[:, DK:2 * DK] = dk.astype(dqkv_ref.dtype)
        dqkv_ref[:, 2 * DK:HEAD_W] = dv.astype(dqkv_ref.dtype)

        num_i = _dot(qs, cb)
        den_i = jnp.sum(qf * nst, axis=1, keepdims=True)
        e_col = inter * (jnp.sum(dnum * num_i, axis=1, keepdims=True) + dden * den_i)
        h_col = jnp.sum(kf * dk_state, axis=1, keepdims=True)
        f_dec = decay * (jnp.sum(jnp.sum(cst * dcn, axis=1, keepdims=True), axis=0, keepdims=True)
                         + jnp.sum(nst * dnn, axis=1, keepdims=True))
        r = lax.broadcasted_iota(jnp.int32, (LCH, LCH), 0)
        c = lax.broadcasted_iota(jnp.int32, (LCH, LCH), 1)
        eye = r == c
        to_col = lambda row: jnp.sum(jnp.where(eye, row, 0.0), axis=1, keepdims=True)
        row_g = jnp.sum(gmat, axis=1, keepdims=True)
        col_g = to_col(jnp.sum(gmat, axis=0, keepdims=True))
        last = lax.broadcasted_iota(jnp.int32, (LCH, 1), 0) == LCH - 1
        db_col = row_g - col_g + e_col - h_col
        db_col = db_col + jnp.where(last, jnp.sum(h_col, axis=0, keepdims=True) + f_dec, 0.0)
        dli_col = col_g + h_col
        dlf_row = jnp.sum(jnp.where(tri, db_col, 0.0), axis=0, keepdims=True)
        df_col = to_col(dlf_row) * _sigmoid(-fr_col)
        dg = _lane_put(dli_col, hd) + _lane_put(df_col, NH + hd)
        dg_ref[...] += dg
        gsum_ref[0:1, 0:GATE_W] += jnp.sum(dg, axis=0, keepdims=True)

        dc_scr[hd] = decay * dcn + _dot_tn((qf * inter).astype(MXU), dnum_b)
        dn_scr[hd] = decay * dnn + jnp.sum(qf * (inter * dden), axis=0, keepdims=True)

    rev = lambda c: nc - 1 - c
    q_spec, k_spec, v_spec = _qkv_specs(rev)
    hv_spec = pl.BlockSpec((LCH, DV), lambda c, h: (rev(c), h))
    gate_spec = pl.BlockSpec((LCH, GATE_W), lambda c, h: (rev(c), 0))
    return pl.pallas_call(
        body, name="mlstm_bwd", grid=(nc, NH),
        in_specs=[q_spec, k_spec, v_spec, gate_spec,
                  pl.BlockSpec((1, GATE_W), lambda c, h: (0, 0)),
                  hv_spec, hv_spec, gate_spec,
                  pl.BlockSpec((None, None, DK, DV), lambda c, h: (h, rev(c), 0, 0)),
                  pl.BlockSpec((None, None, 1, DK), lambda c, h: (h, rev(c), 0, 0)),
                  pl.BlockSpec((None, None, 1, GATE_W), lambda c, h: (h, rev(c), 0, 0)),
                  ANY],
        out_specs=[pl.BlockSpec((LCH, HEAD_W), lambda c, h: (rev(c), C_QKV // HEAD_W + h)),
                   gate_spec,
                   pl.BlockSpec((8, D), lambda c, h: (0, 0))],
        out_shape=[jax.ShapeDtypeStruct(dproj.shape, dproj.dtype),
                   jax.ShapeDtypeStruct((n, GATE_W), F32),
                   jax.ShapeDtypeStruct((8, D), F32)],
        scratch_shapes=[pltpu.VMEM((NH, DK, DV), F32), pltpu.VMEM((NH, 1, DK), F32)],
        input_output_aliases={11: 0},
        compiler_params=_cparams("arbitrary", "arbitrary"),
    )(proj, proj, proj, gates, bias, h, dh, stats, cs, ns, ms, dproj)


def _proj_spec(tm, col, width):
    return pl.BlockSpec((tm, width), lambda i: (i, col // width))


def _halo_prev(tm, col):
    return pl.BlockSpec((8, 1024), lambda i: (jnp.maximum(i * (tm // 8) - 1, 0), col // 1024))


def _const(shape):
    return pl.BlockSpec(shape, lambda i: (0,) * len(shape))


def _conv_inputs(i, tm, xa_ref, ca_ref, xah_ref, cah_ref):
    u = _f32(xa_ref) * _f32(ca_ref)
    uh = jnp.where(i > 0, _f32(xah_ref) * _f32(cah_ref), 0.0)
    rid = lax.broadcasted_iota(jnp.int32, u.shape, 0)
    u1 = jnp.where(rid == 0, uh[7:8, :], pltpu.roll(u, 1, 0))
    u2 = jnp.where(rid == 0, uh[6:7, :], jnp.where(rid == 1, uh[7:8, :], pltpu.roll(u, 2, 0)))
    return u, u1, u2


def _head_norm(hh, gh):
    out = []
    for j in range(NH):
        hj = hh[:, j * DV:(j + 1) * DV]
        rj = _rstd(hj)
        out.append((hj * rj, rj, gh[:, j * DV:(j + 1) * DV]))
    return out


def _tail_fwd(proj, h, x, p, t, cw, cb, gh, gple, gfin, wa, wb, wo, wpg, wp):
    n = x.shape[0]
    tm = min(256, n)

    def body(ba_r, za_r, o_r, zb_r, ga_r, gb_r, xa_r, ca_r, xah_r, cah_r, h_r, x_r, p_r, t_r,
             cw_r, cb_r, gh_r, gple_r, gfin_r, wa_r, wb_r, wo_r, wpg_r, wp_r,
             apre_o, bpre_o, mg_o, xn1_o, de_o, dgp_o, ya_o, yb_o, x1_o, dx2_o, acc_o):
        i = pl.program_id(0)

        @pl.when(i == 0)
        def _():
            acc_o[...] = jnp.zeros_like(acc_o)

        u, u1, u2 = _conv_inputs(i, tm, xa_r, ca_r, xah_r, cah_r)
        cwv = cw_r[...]
        cv = cwv[0:1, :] * u2 + cwv[1:2, :] * u1 + cwv[2:3, :] * u + cb_r[...]
        za = _f32(za_r)
        a_pre = (_f32(ba_r) * cv * (za * _sigmoid(za))).astype(MXU)
        apre_o[...] = a_pre
        ya = _dot(a_pre, wa_r[...])

        hn = _head_norm(h_r[...], gh_r[...])
        hbn = jnp.concatenate([xh * g for xh, _, g in hn], axis=1)
        zb = _f32(zb_r)
        b_pre = (_sigmoid(_f32(o_r)) * hbn * (zb * _sigmoid(zb))).astype(MXU)
        bpre_o[...] = b_pre
        yb = _dot(b_pre, wb_r[...])
        ya_o[...] = ya.astype(MXU)
        yb_o[...] = yb.astype(MXU)

        mg = (_sigmoid(_f32(ga_r)) * ya + _sigmoid(_f32(gb_r)) * yb).astype(MXU)
        mg_o[...] = mg
        x1 = x_r[...] + _dot(mg, wo_r[...])
        x1_o[...] = x1
        xn1 = (x1 * _rstd(x1) * gple_r[...]).astype(MXU)
        xn1_o[...] = xn1
        gt = _sigmoid(_dot(xn1, wpg_r[...]))
        e = _dot(p_r[...].astype(MXU), wp_r[...])
        x2 = x1 + gt * e
        r2 = _rstd(x2)
        xh2 = x2 * r2
        gf = gfin_r[...]
        diff = xh2 * gf - t_r[...]
        dy = diff * (1.0 / D)
        dx2 = _norm_bwd(dy, xh2, r2, gf)
        dx2_o[...] = dx2
        de_o[...] = (dx2 * gt).astype(MXU)
        dgp_o[...] = (dx2 * e * gt * (1.0 - gt)).astype(MXU)
        acc_o[0:1, :] += jnp.sum(dy * xh2, axis=0, keepdims=True)
        loss = 0.5 * jnp.sum(jnp.sum(diff * diff, axis=1, keepdims=True), axis=0, keepdims=True) * (1.0 / D)
        acc_o[1:2, :] += jnp.broadcast_to(loss, (1, D))

    row = lambda w, dt: (pl.BlockSpec((tm, w), lambda i: (i, 0)), jax.ShapeDtypeStruct((n, w), dt))
    outs = [row(D, MXU), row(VD, MXU), row(D, MXU), row(D, MXU), row(D, MXU), row(D, MXU),
            row(D, MXU), row(D, MXU), row(D, F32), row(D, F32),
            (_const((8, D)), jax.ShapeDtypeStruct((8, D), F32))]
    return pl.pallas_call(
        body, name="tail_fwd", grid=(n // tm,),
        in_specs=[_proj_spec(tm, C_BA, 1024), _proj_spec(tm, C_ZA, 1024),
                  _proj_spec(tm, C_O, 2048), _proj_spec(tm, C_ZB, 2048),
                  _proj_spec(tm, C_GA, 1024), _proj_spec(tm, C_GB, 1024),
                  _proj_spec(tm, C_XA, 1024), _proj_spec(tm, C_CA, 1024),
                  _halo_prev(tm, C_XA), _halo_prev(tm, C_CA),
                  pl.BlockSpec((tm, VD), lambda i: (i, 0)),
                  pl.BlockSpec((tm, D), lambda i: (i, 0)),
                  pl.BlockSpec((tm, PLE), lambda i: (i, 0)),
                  pl.BlockSpec((tm, D), lambda i: (i, 0)),
                  _const((8, D)), _const((1, D)), _const((1, VD)), _const((1, D)), _const((1, D)),
                  _const((D, D)), _const((VD, D)), _const((D, D)), _const((D, D)), _const((PLE, D))],
        out_specs=[s for s, _ in outs],
        out_shape=[s for _, s in outs],
        compiler_params=_cparams("arbitrary"),
    )(*([proj] * 10), h, x, p, t, cw, cb, gh, gple, gfin, wa, wb, wo, wpg, wp)


def _tail_bwd(proj, h, dgp, dx2, x1, ya, yb, cw, cb, gh, gple, wpg, wo, wb, wa):
    n = x1.shape[0]
    tm = min(256, n)

    def body(ba_r, za_r, o_r, zb_r, ga_r, gb_r, xa_r, ca_r, xah_r, cah_r, h_r,
             dgp_r, dx2_r, x1_r, ya_r, yb_r, cw_r, cb_r, gh_r, gple_r,
             wpg_r, wo_r, wb_r, wa_r,
             dproj_o, dcv_o, dh_o, dx1_o, dx1b_o, dya_o, dyb_o, acc_o):
        i = pl.program_id(0)

        @pl.when(i == 0)
        def _():
            acc_o[...] = jnp.zeros_like(acc_o)

        dxn1 = _dot_nt(dgp_r[...], wpg_r[...])
        x1 = x1_r[...]
        r1 = _rstd(x1)
        xh1 = x1 * r1
        acc_o[0:1, 0:D] += jnp.sum(dxn1 * xh1, axis=0, keepdims=True)
        dx1 = dx2_r[...] + _norm_bwd(dxn1, xh1, r1, gple_r[...])
        dx1_o[...] = dx1
        dx1b = dx1.astype(MXU)
        dx1b_o[...] = dx1b

        dmg = _dot_nt(dx1b, wo_r[...])
        sga = _sigmoid(_f32(ga_r))
        sgb = _sigmoid(_f32(gb_r))
        dya = (dmg * sga).astype(MXU)
        dyb = (dmg * sgb).astype(MXU)
        dya_o[...] = dya
        dyb_o[...] = dyb
        dproj_o[:, C_GA:C_GA + D] = (dmg * _f32(ya_r) * sga * (1.0 - sga)).astype(MXU)
        dproj_o[:, C_GB:C_GB + D] = (dmg * _f32(yb_r) * sgb * (1.0 - sgb)).astype(MXU)

        db_pre = _dot_nt(dyb, wb_r[...])
        hn = _head_norm(h_r[...], gh_r[...])
        hbn = jnp.concatenate([xh * g for xh, _, g in hn], axis=1)
        so = _sigmoid(_f32(o_r))
        zb = _f32(zb_r)
        szb = _sigmoid(zb)
        sb = zb * szb
        dproj_o[:, C_O:C_O + VD] = (db_pre * hbn * sb * so * (1.0 - so)).astype(MXU)
        dproj_o[:, C_ZB:C_ZB + VD] = (db_pre * so * hbn * szb * (1.0 + zb * (1.0 - szb))).astype(MXU)
        dhbn = db_pre * so * sb
        for j, (xh, rj, g) in enumerate(hn):
            dj = dhbn[:, j * DV:(j + 1) * DV]
            acc_o[1:2, j * DV:(j + 1) * DV] += jnp.sum(dj * xh, axis=0, keepdims=True)
            dh_o[:, j * DV:(j + 1) * DV] = _norm_bwd(dj, xh, rj, g)

        da_pre = _dot_nt(dya, wa_r[...])
        u, u1, u2 = _conv_inputs(i, tm, xa_r, ca_r, xah_r, cah_r)
        cwv = cw_r[...]
        cv = cwv[0:1, :] * u2 + cwv[1:2, :] * u1 + cwv[2:3, :] * u + cb_r[...]
        za = _f32(za_r)
        sza = _sigmoid(za)
        sa = za * sza
        ba = _f32(ba_r)
        dproj_o[:, C_BA:C_BA + D] = (da_pre * cv * sa).astype(MXU)
        dproj_o[:, C_ZA:C_ZA + D] = (da_pre * ba * cv * sza * (1.0 + za * (1.0 - sza))).astype(MXU)
        dcv = da_pre * ba * sa
        dcv_o[...] = dcv.astype(MXU)
        acc_o[2:3, 0:D] += jnp.sum(dcv, axis=0, keepdims=True)
        acc_o[3:4, 0:D] += jnp.sum(dcv * u2, axis=0, keepdims=True)
        acc_o[4:5, 0:D] += jnp.sum(dcv * u1, axis=0, keepdims=True)
        acc_o[5:6, 0:D] += jnp.sum(dcv * u, axis=0, keepdims=True)

    row = lambda w, dt: (pl.BlockSpec((tm, w), lambda i: (i, 0)), jax.ShapeDtypeStruct((n, w), dt))
    outs = [(pl.BlockSpec((tm, TAIL_W), lambda i: (i, 0)), jax.ShapeDtypeStruct((n, NMAIN), MXU)),
            row(D, MXU), row(VD, F32), row(D, F32), row(D, MXU), row(D, MXU), row(D, MXU),
            (_const((8, VD)), jax.ShapeDtypeStruct((8, VD), F32))]
    rowin = lambda w: pl.BlockSpec((tm, w), lambda i: (i, 0))
    return pl.pallas_call(
        body, name="tail_bwd", grid=(n // tm,),
        in_specs=[_proj_spec(tm, C_BA, 1024), _proj_spec(tm, C_ZA, 1024),
                  _proj_spec(tm, C_O, 2048), _proj_spec(tm, C_ZB, 2048),
                  _proj_spec(tm, C_GA, 1024), _proj_spec(tm, C_GB, 1024),
                  _proj_spec(tm, C_XA, 1024), _proj_spec(tm, C_CA, 1024),
                  _halo_prev(tm, C_XA), _halo_prev(tm, C_CA),
                  rowin(VD), rowin(D), rowin(D), rowin(D), rowin(D), rowin(D),
                  _const((8, D)), _const((1, D)), _const((1, VD)), _const((1, D)),
                  _const((D, D)), _const((D, D)), _const((VD, D)), _const((D, D))],
        out_specs=[s for s, _ in outs],
        out_shape=[s for _, s in outs],
        compiler_params=_cparams("arbitrary"),
    )(*([proj] * 10), h, dgp, dx2, x1, ya, yb, cw, cb, gh, gple, wpg, wo, wb, wa)


def _conv_bwd(proj, dcv, cw, dproj):
    n = dcv.shape[0]
    tm = min(512, n)
    nt = n // tm

    def body(xa_r, ca_r, dcv_r, nxt_r, cw_r, dproj_in, dxc_o):
        del dproj_in
        i = pl.program_id(0)
        dcv_v = _f32(dcv_r)
        nxt = jnp.where(i < nt - 1, _f32(nxt_r), 0.0)
        rid = lax.broadcasted_iota(jnp.int32, dcv_v.shape, 0)
        d1 = jnp.where(rid == tm - 1, nxt[0:1, :], pltpu.roll(dcv_v, tm - 1, 0))
        d2 = jnp.where(rid == tm - 2, nxt[0:1, :],
                       jnp.where(rid == tm - 1, nxt[1:2, :], pltpu.roll(dcv_v, tm - 2, 0)))
        cwv = cw_r[...]
        du = cwv[2:3, :] * dcv_v + cwv[1:2, :] * d1 + cwv[0:1, :] * d2
        dxc_o[:, 0:D] = (du * _f32(ca_r)).astype(MXU)
        dxc_o[:, D:2 * D] = (du * _f32(xa_r)).astype(MXU)

    return pl.pallas_call(
        body, name="conv_bwd", grid=(nt,),
        in_specs=[_proj_spec(tm, C_XA, 1024), _proj_spec(tm, C_CA, 1024),
                  pl.BlockSpec((tm, D), lambda i: (i, 0)),
                  pl.BlockSpec((8, D), lambda i: (jnp.minimum((i + 1) * (tm // 8), n // 8 - 1), 0)),
                  _const((8, D)), ANY],
        out_specs=pl.BlockSpec((tm, CONV_W), lambda i: (i, C_XA // CONV_W)),
        out_shape=jax.ShapeDtypeStruct(dproj.shape, dproj.dtype),
        input_output_aliases={5: 0},
        compiler_params=_cparams("arbitrary"),
    )(proj, proj, dcv, dcv, cw, dproj)


def _position():
    return lax.axis_index("x"), lax.axis_index("y"), lax.axis_index("c")


def _flip(v, bit):
    return 1 - v if bit else v


def _scatter_copies(srcs, dsts, send_sems, recv_sems, local_sems):
    x, y, c = _position()
    me = 4 * x + 2 * y + c
    na = len(srcs)
    copies = []
    for r in range(N_DEV):
        px, py, pc = _flip(x, (r >> 2) & 1), _flip(y, (r >> 1) & 1), _flip(c, r & 1)
        for a in range(na):
            rows = dsts[a].shape[1]
            src = srcs[a].at[2 * px + py, pl.ds(pl.multiple_of(pc * rows, rows), rows)]
            dst = dsts[a].at[me]
            if r == 0:
                copies.append(pltpu.make_async_copy(src, dst, local_sems.at[a]))
            else:
                k = (r - 1) * na + a
                copies.append(pltpu.make_async_remote_copy(
                    src_ref=src, dst_ref=dst, send_sem=send_sems.at[k], recv_sem=recv_sems.at[k],
                    device_id=(px, py, pc), device_id_type=MESH))
    return copies


def _input_grad(dproj, dgates, w_main, w_gate, x, dx1, g_mix, g_in, g_rest):
    n = x.shape[0]
    tm, tk = min(1024, n), 2048
    nk = NMAIN // tk
    nt = n // tm

    def body(dp_r, dg_r, w_r, wg_r, x_r, dx1_r, g_r, gin, grest,
             gx_o, acc_o, oin, orest, acc, send_sems, recv_sems, local_sems):
        i = pl.program_id(0)
        kk = pl.program_id(1)
        copies = _scatter_copies((gin, grest), (oin, orest), send_sems, recv_sems, local_sems)

        @pl.when((i == 0) & (kk == 0))
        def _():
            acc_o[...] = jnp.zeros_like(acc_o)
            for cp in copies:
                cp.start()

        @pl.when(kk == 0)
        def _():
            acc[...] = _dot_nt(dg_r[...].astype(MXU), wg_r[...])

        acc[...] += _dot_nt(dp_r[...], w_r[...])

        @pl.when(kk == nk - 1)
        def _():
            dhn = acc[...]
            xv = x_r[...]
            r0 = _rstd(xv)
            xh = xv * r0
            acc_o[0:1, :] += jnp.sum(dhn * xh, axis=0, keepdims=True)
            gx_o[...] = dx1_r[...] + _norm_bwd(dhn, xh, r0, g_r[...])

        @pl.when((i == nt - 1) & (kk == nk - 1))
        def _():
            for cp in copies:
                cp.wait()

    nrem = 2 * (N_DEV - 1)
    r_in, r_rest = g_in.shape[1] // 2, g_rest.shape[1] // 2
    return pl.pallas_call(
        body, name="input_grad", grid=(nt, nk),
        in_specs=[pl.BlockSpec((tm, tk), lambda i, kk: (i, kk)),
                  pl.BlockSpec((tm, GATE_W), lambda i, kk: (i, 0)),
                  pl.BlockSpec((D, tk), lambda i, kk: (0, kk)),
                  pl.BlockSpec((D, GATE_W), lambda i, kk: (0, 0)),
                  pl.BlockSpec((tm, D), lambda i, kk: (i, 0)),
                  pl.BlockSpec((tm, D), lambda i, kk: (i, 0)),
                  pl.BlockSpec((1, D), lambda i, kk: (0, 0)),
                  ANY, ANY],
        out_specs=[pl.BlockSpec((tm, D), lambda i, kk: (i, 0)),
                   pl.BlockSpec((8, D), lambda i, kk: (0, 0)),
                   ANY, ANY],
        out_shape=[jax.ShapeDtypeStruct((n, D), F32), jax.ShapeDtypeStruct((8, D), F32),
                   jax.ShapeDtypeStruct((N_DEV, r_in) + g_in.shape[2:], g_in.dtype),
                   jax.ShapeDtypeStruct((N_DEV, r_rest) + g_rest.shape[2:], g_rest.dtype)],
        scratch_shapes=[pltpu.VMEM((tm, D), F32),
                        pltpu.SemaphoreType.DMA((nrem,)), pltpu.SemaphoreType.DMA((nrem,)),
                        pltpu.SemaphoreType.DMA((2,))],
        compiler_params=_cparams("arbitrary", "arbitrary"),
    )(dproj, dgates, w_main, w_gate, x, dx1, g_mix, g_in, g_rest)


def _pack_small(acc_f, acc_b, acc_x, gsum):
    def body(f_r, b_r, x_r, s_r, o_r):
        o_r[...] = jnp.zeros_like(o_r)
        o_r[0:1, :] = x_r[0:1, :]
        o_r[1:2, :] = b_r[2:3, 0:D]
        o_r[2:3, :] = b_r[1:2, 0:D]
        o_r[3:4, :] = b_r[1:2, D:2 * D]
        o_r[4:5, :] = b_r[0:1, 0:D]
        o_r[5:6, :] = f_r[0:1, :]
        o_r[6:9, :] = b_r[3:6, 0:D]
        lane = lax.broadcasted_iota(jnp.int32, (1, D), 1)
        o_r[9:10, :] = jnp.where(lane < 2 * NH, s_r[0:1, :], jnp.where(lane == 2 * NH, f_r[1:2, :], 0.0))

    return pl.pallas_call(
        body, name="pack_small",
        out_shape=jax.ShapeDtypeStruct((SMALL_ROWS, D), F32),
    )(acc_f, acc_b, acc_x, gsum)


def _sum_slots(r, name, tr=64):
    s, rows, w = r.shape
    tr = min(tr, rows)
    assert rows % tr == 0

    def body(r_ref, o_ref):
        tot = r_ref[0].astype(F32)
        for k in range(1, s):
            tot = tot + r_ref[k].astype(F32)
        o_ref[...] = tot

    return pl.pallas_call(
        body, name=name, grid=(rows // tr,),
        in_specs=[pl.BlockSpec((s, tr, w), lambda i: (0, i, 0))],
        out_specs=pl.BlockSpec((tr, w), lambda i: (i, 0)),
        out_shape=jax.ShapeDtypeStruct((rows, w), F32),
        compiler_params=_cparams("arbitrary"),
    )(r)


def _adamw(w, g, m, v, name, g_row0=0, tr=64):
    lead = w.ndim == 3
    rows, cols = w.shape[-2:]
    tr = min(tr, rows)
    assert rows % tr == 0 and g_row0 % tr == 0 and g.shape[1] == cols
    c1 = 1.0 - ADAM_B1 ** ADAM_STEP
    c2 = 1.0 - ADAM_B2 ** ADAM_STEP

    def body(w_r, g_r, m_r, v_r, d_o, m_o, v_o):
        gv = g_r[...]
        mn = ADAM_B1 * m_r[...] + (1.0 - ADAM_B1) * gv
        vn = ADAM_B2 * v_r[...] + (1.0 - ADAM_B2) * (gv * gv)
        m_o[...] = mn
        v_o[...] = vn
        d_o[...] = -ADAM_LR * ((mn / c1) / (jnp.sqrt(vn / c2) + ADAM_EPS) + ADAM_WD * w_r[...])

    if lead:
        spec = pl.BlockSpec((None, tr, cols), lambda i: (0, i, 0))
    else:
        spec = pl.BlockSpec((tr, cols), lambda i: (i, 0))
    g_spec = pl.BlockSpec((tr, cols), lambda i: (g_row0 // tr + i, 0))
    shp = jax.ShapeDtypeStruct(w.shape, F32)
    return pl.pallas_call(
        body, name=name, grid=(rows // tr,),
        in_specs=[spec, g_spec, spec, spec], out_specs=[spec] * 3, out_shape=[shp] * 3,
        compiler_params=_cparams("arbitrary"),
    )(w, g, m, v)


def _allgather_chips(halved, whole):
    arrs = list(halved) + list(whole)
    nh, na = len(halved), len(arrs)
    relations = ((1, 0), (0, 1), (1, 1))
    n_ici = 3 * na

    def body(*refs):
        ins, outs = refs[:na], refs[na:2 * na]
        send_sems, recv_sems, local_sems = refs[2 * na:]
        x, y, c = _position()
        mine = 2 * x + y

        def half(a, chip, core):
            rows = ins[a].shape[0] // 2
            return outs[a].at[chip, pl.ds(pl.multiple_of(core * rows, rows), rows)]

        def remote(src, dst, k, to):
            return pltpu.make_async_remote_copy(src_ref=src, dst_ref=dst, send_sem=send_sems.at[k],
                                                recv_sem=recv_sems.at[k], device_id=to, device_id_type=MESH)

        local = [pltpu.make_async_copy(ins[a], outs[a].at[mine], local_sems.at[a]) for a in range(na)]
        for cp in local:
            cp.start()
        first, passed = [], []
        for j, (rx, ry) in enumerate(relations):
            px, py = _flip(x, rx), _flip(y, ry)
            for a in range(na):
                if a < nh:
                    rows = ins[a].shape[0] // 2
                    src = ins[a].at[pl.ds(pl.multiple_of(c * rows, rows), rows)]
                    first.append(remote(src, half(a, mine, c), j * na + a, (px, py, c)))
                    landed = half(a, 2 * px + py, c)
                    passed.append(remote(landed, landed, n_ici + j * nh + a, (x, y, 1 - c)))
                else:
                    first.append(remote(ins[a], outs[a].at[mine], j * na + a, (px, py, c)))
        for cp in first:
            cp.start()
        k = 0
        for j in range(3):
            for a in range(na):
                first[j * na + a].wait_recv()
                if a < nh:
                    passed[k].start()
                    k += 1
        for cp in passed:
            cp.wait_recv()
        for cp in first + passed:
            cp.wait_send()
        for cp in local:
            cp.wait()

    nsem = n_ici + 3 * nh
    return pl.pallas_call(
        body, name="allgather_weights",
        in_specs=[ANY] * na, out_specs=[ANY] * na,
        out_shape=[jax.ShapeDtypeStruct((N_CHIPS,) + a.shape, a.dtype) for a in arrs],
        scratch_shapes=[pltpu.SemaphoreType.DMA((nsem,)), pltpu.SemaphoreType.DMA((nsem,)),
                        pltpu.SemaphoreType.DMA((na,))],
    )(*arrs)


def _join_halves(halves, small):
    na = len(halves)

    def body(*refs):
        ins, sm = refs[:na], refs[na]
        outs, osm = refs[na + 1:2 * na + 1], refs[2 * na + 1]
        send_sems, recv_sems, local_sems = refs[2 * na + 2:]
        x, y, c = _position()
        me = 4 * x + 2 * y + c
        copies = []
        for a in range(na):
            rows = ins[a].shape[0]
            dst = outs[a].at[pl.ds(pl.multiple_of(c * rows, rows), rows)]
            copies.append(pltpu.make_async_copy(ins[a], dst, local_sems.at[a]))
            copies.append(pltpu.make_async_remote_copy(
                src_ref=ins[a], dst_ref=dst, send_sem=send_sems.at[a], recv_sem=recv_sems.at[a],
                device_id=(x, y, 1 - c), device_id_type=MESH))
        copies.append(pltpu.make_async_copy(sm, osm.at[me], local_sems.at[na]))
        for r in range(1, N_DEV):
            to = (_flip(x, (r >> 2) & 1), _flip(y, (r >> 1) & 1), _flip(c, r & 1))
            copies.append(pltpu.make_async_remote_copy(
                src_ref=sm, dst_ref=osm.at[me], send_sem=send_sems.at[na + r - 1],
                recv_sem=recv_sems.at[na + r - 1], device_id=to, device_id_type=MESH))
        for cp in copies:
            cp.start()
        for cp in copies:
            cp.wait()

    nsem = na + N_DEV - 1
    return pl.pallas_call(
        body, name="join_halves",
        in_specs=[ANY] * (na + 1), out_specs=[ANY] * (na + 1),
        out_shape=[jax.ShapeDtypeStruct((2 * a.shape[0],) + a.shape[1:], a.dtype) for a in halves]
        + [jax.ShapeDtypeStruct((N_DEV,) + small.shape, small.dtype)],
        scratch_shapes=[pltpu.SemaphoreType.DMA((nsem,)), pltpu.SemaphoreType.DMA((nsem,)),
                        pltpu.SemaphoreType.DMA((na + 1,))],
    )(*halves, small)


def _to_internal(w_glob):
    order = sorted(SEGMENTS, key=lambda s: s[2])
    main = jnp.concatenate([w_glob[..., g0:g0 + w] for g0, w, _ in order], axis=-1)
    gate = w_glob[..., GATE_COL:GATE_COL + 2 * NH]
    pad = [(0, 0)] * (w_glob.ndim - 1) + [(0, GATE_W - 2 * NH)]
    return main, jnp.pad(gate, pad)


def _to_global(main, gate):
    parts = sorted([(g0, main[..., i0:i0 + w]) for g0, w, i0 in SEGMENTS]
                   + [(GATE_COL, gate[..., 0:2 * NH])], key=lambda s: s[0])
    return jnp.concatenate([p for _, p in parts], axis=-1)


def _pack_rows(wa, wb, wo, wpg, wp):
    return jnp.concatenate([wa, wb, wo, wpg, wp.reshape(PACK_ROWS[4], D)], axis=0)


def _pad_rows(a, rows=8):
    return jnp.pad(a, ((0, rows - a.shape[0]), (0, 0)))


def kernel(x, p, g_mix, w_in, conv_w, conv_b, w_a_out, b_gates, g_head, w_b_out, w_o, g_ple, w_ple_gate, w_ple, g_final, loss_target, m_g_mix, m_w_in, m_conv_w, m_conv_b, m_w_a_out, m_b_gates, m_g_head, m_w_b_out, m_w_o, m_g_ple, m_w_ple_gate, m_w_ple, m_g_final, v_g_mix, v_w_in, v_conv_w, v_conv_b, v_w_a_out, v_b_gates, v_g_head, v_w_b_out, v_w_o, v_g_ple, v_w_ple_gate, v_w_ple, v_g_final):
    chip = 2 * lax.axis_index("x") + lax.axis_index("y")
    xs, ps, ts = x[0], p[0, 0], loss_target[0]
    g_fin = g_final.reshape(1, D)

    pack_w = _pack_rows(w_a_out[0], w_b_out[0], w_o[0], w_ple_gate[0], w_ple[0])
    g_win, g_pack, g_cw = _allgather_chips(
        [w_in[0].astype(MXU), pack_w.astype(MXU)], [_pad_rows(conv_w[0])])
    w_glob = jnp.transpose(g_win, (1, 0, 2)).reshape(D, N_IN)
    w_main, w_gate = _to_internal(w_glob)
    offs = [0, 256, 768, 1024, 1280, 1344]
    wa, wb, wo, wpg = [g_pack[:, offs[k]:offs[k + 1]].reshape(-1, D) for k in range(4)]
    wp = jnp.transpose(g_pack[:, offs[4]:offs[5]].reshape(N_CHIPS, PLE, PLE), (1, 0, 2)).reshape(PLE, D)
    cw = jnp.transpose(g_cw, (1, 0, 2)).reshape(8, D)

    bias = jnp.pad(b_gates, ((0, 0), (0, GATE_W - 2 * NH)))
    hn, gates = _prenorm(xs, g_mix, w_gate)
    proj = _matmul(hn, w_main, MXU, "proj")
    h, stats, cs, ns, ms = _mlstm_fwd(proj, gates, bias)
    (a_pre, b_pre, mg, xn1, de, dgp, ya, yb, x1, dx2, acc_f) = _tail_fwd(
        proj, h, xs, ps, ts, cw, conv_b, g_head, g_ple, g_fin, wa, wb, wo, wpg, wp)
    dproj, dcv, dh, dx1, dx1b, dya, dyb, acc_b = _tail_bwd(
        proj, h, dgp, dx2, x1, ya, yb, cw, conv_b, g_head, g_ple, wpg, wo, wb, wa)
    dproj = _conv_bwd(proj, dcv, cw, dproj)
    dproj, dgates, gsum = _mlstm_bwd(proj, gates, bias, h, dh, stats, cs, ns, ms, dproj)
    d_main = _matmul_tn(hn, dproj, "dw_in", out_dtype=WIRE)
    d_gate = _matmul_tn(hn, dgates, "dw_gate", out_dtype=WIRE)
    d_wa = _matmul_tn(a_pre, dya, "dw_a_out", out_dtype=WIRE)
    d_wb = _matmul_tn(b_pre, dyb, "dw_b_out", out_dtype=WIRE)
    d_wo = _matmul_tn(mg, dx1b, "dw_o", out_dtype=WIRE)
    d_wpg = _matmul_tn(xn1, dgp, "dw_ple_gate", out_dtype=WIRE)
    d_wp = _matmul_tn(ps, de, "dw_ple", out_dtype=WIRE)

    g_in = jnp.transpose(_to_global(d_main, d_gate).reshape(D, N_CHIPS, N_IN // N_CHIPS), (1, 0, 2))
    d_wp_c = jnp.transpose(d_wp.reshape(PLE, N_CHIPS, PLE), (1, 0, 2)).reshape(N_CHIPS, PACK_ROWS[4], D)
    g_rest = jnp.concatenate(
        [d_wa.reshape(N_CHIPS, -1, D), d_wb.reshape(N_CHIPS, -1, D), d_wo.reshape(N_CHIPS, -1, D),
         d_wpg.reshape(N_CHIPS, -1, D), d_wp_c], axis=1)
    grad_x, acc_x, r_in, r_rest = _input_grad(dproj, dgates, w_main, w_gate, xs, dx1, g_mix, g_in, g_rest)
    small = _pack_small(acc_f, acc_b, acc_x, gsum)
    gw_in, gw_rest, r_small = _join_halves(
        [_sum_slots(r_in, "sum_w_in"), _sum_slots(r_rest, "sum_rest", tr=96)], small)
    gs = _sum_slots(r_small, "sum_small", tr=SMALL_ROWS)

    big = [("w_in", w_in, m_w_in, v_w_in, gw_in, 0)]
    row0 = 0
    for name, w, m, v in (("w_a_out", w_a_out, m_w_a_out, v_w_a_out), ("w_b_out", w_b_out, m_w_b_out, v_w_b_out),
                          ("w_o", w_o, m_w_o, v_w_o), ("w_ple_gate", w_ple_gate, m_w_ple_gate, v_w_ple_gate)):
        big.append((name, w, m, v, gw_rest, row0))
        row0 += w.shape[1]
    g_wp = gw_rest[row0:row0 + PACK_ROWS[4]].reshape(PLE, PLE)
    big.append(("w_ple", w_ple, m_w_ple, v_w_ple, g_wp, 0))
    upd = {name: _adamw(w, g, m, v, "adamw_" + name, g_row0=r0) for name, w, m, v, g, r0 in big}
    g_big = {name: (g if name in ("w_in", "w_ple") else g[r0:r0 + w.shape[1]])[None]
             for name, w, m, v, g, r0 in big}

    lane = lax.broadcasted_iota(jnp.int32, (1, D), 1)
    g_small = jnp.concatenate([gs[0:6], jnp.where(lane < 2 * NH, gs[9:10], 0.0), jnp.zeros((1, D), F32)], axis=0)

    def small_pack(gm, cb_, bg, gh, gp, gf):
        return jnp.concatenate([gm, cb_, gh.reshape(2, D), gp, gf.reshape(1, D),
                                jnp.pad(bg, ((0, 0), (0, D - 2 * NH))), jnp.zeros((1, D), F32)], axis=0)

    ws = small_pack(g_mix, conv_b, b_gates, g_head, g_ple, g_final)
    ms_ = small_pack(m_g_mix, m_conv_b, m_b_gates, m_g_head, m_g_ple, m_g_final)
    vs = small_pack(v_g_mix, v_conv_b, v_b_gates, v_g_head, v_g_ple, v_g_final)
    upd_s = _adamw(ws, g_small, ms_, vs, "adamw_small")
    g_cw_mine = _pad_rows(lax.dynamic_slice(gs[6:9], (0, chip * PLE), (3, PLE)))
    upd_c = _adamw(_pad_rows(conv_w[0]), g_cw_mine, _pad_rows(m_conv_w[0]), _pad_rows(v_conv_w[0]),
                   "adamw_conv_w")

    def leaves(bigs, sm, cwv):
        return [sm[0:1], bigs["w_in"], cwv[0:3][None], sm[1:2], bigs["w_a_out"], sm[6:7, 0:2 * NH],
                sm[2:4].reshape(1, VD), bigs["w_b_out"], bigs["w_o"], sm[4:5], bigs["w_ple_gate"],
                bigs["w_ple"], sm[5]]

    loss = gs[9, 2 * NH]
    outs = [loss, grad_x[None]] + leaves(g_big, g_small, g_cw_mine)
    for k in range(3):
        outs += leaves({name: u[k] for name, u in upd.items()}, upd_s[k], upd_c[k])
    return tuple(outs)
```

```python
import jax
import jax.numpy as jnp
from jax import lax
from jax.experimental import pallas as pl
from jax.experimental.pallas import tpu as pltpu

F32 = jnp.float32
MXU = jnp.bfloat16
WIRE = jnp.bfloat16

D = 1024
NH, DK, DV = 4, 256, 512
VD = NH * DV
PLE = 256
LCH = 256
EPS = 1e-6
N_IN = 14344
NMAIN = 14336
GATE_W = 128
N_CHIPS, N_DEV = 4, 8

C_BA, C_ZA, C_O, C_ZB, C_GA, C_GB = 0, 1024, 2048, 4096, 6144, 7168
C_QKV, C_XA, C_CA = 8192, 12288, 13312
HEAD_W = 2 * DK + DV
TAIL_W = 8192
CONV_W = 2048
SEGMENTS = (
    (0, 1024, C_XA), (1024, 1024, C_BA), (2048, 1024, C_CA), (3072, 1024, C_ZA),
    (8192, 2048, C_O), (10240, 2048, C_ZB), (12296, 1024, C_GA), (13320, 1024, C_GB),
) + tuple((4096 + DK * h, DK, C_QKV + HEAD_W * h) for h in range(NH)) + tuple(
    (5120 + DK * h, DK, C_QKV + HEAD_W * h + DK) for h in range(NH)) + tuple(
    (6144 + DV * h, DV, C_QKV + HEAD_W * h + 2 * DK) for h in range(NH))
GATE_COL = 12288

PACK_ROWS = (256, 512, 256, 256, 64)
PACK_TOTAL = sum(PACK_ROWS)
SMALL_ROWS = 16

ADAM_LR, ADAM_B1, ADAM_B2, ADAM_EPS, ADAM_WD, ADAM_STEP = 0.001, 0.9, 0.999, 1e-08, 0.01, 10

VMEM_LIMIT = 56 * 1024 * 1024
MESH = pl.DeviceIdType.MESH
ANY = pl.BlockSpec(memory_space=pl.ANY)


def _cparams(*sem):
    return pltpu.CompilerParams(dimension_semantics=sem, vmem_limit_bytes=VMEM_LIMIT)


def _dot(a, b):
    return jnp.dot(a, b, preferred_element_type=F32)


def _dot_nt(a, b):
    return lax.dot_general(a, b, (((1,), (1,)), ((), ())), preferred_element_type=F32)


def _dot_tn(a, b):
    return lax.dot_general(a, b, (((0,), (0,)), ((), ())), preferred_element_type=F32)


def _sigmoid(x):
    return 1.0 / (1.0 + jnp.exp(-x))


def _logsig(x):
    return jnp.minimum(x, 0.0) - jnp.log(1.0 + jnp.exp(-jnp.abs(x)))


def _rstd(x):
    return lax.rsqrt(jnp.mean(x * x, axis=-1, keepdims=True) + EPS)


def _norm_bwd(dy, xhat, r, g):
    dxh = dy * g
    return r * (dxh - xhat * jnp.mean(dxh * xhat, axis=-1, keepdims=True))


def _f32(ref):
    return ref[...].astype(F32)


def _prenorm(x, g_mix, wg):
    n = x.shape[0]
    tm = min(512, n)

    def body(x_ref, g_ref, wg_ref, hn_ref, gate_ref):
        xv = x_ref[...]
        hn = (xv * _rstd(xv) * g_ref[...]).astype(MXU)
        hn_ref[...] = hn
        gate_ref[...] = _dot(hn, wg_ref[...])

    return pl.pallas_call(
        body, name="prenorm", grid=(n // tm,),
        in_specs=[pl.BlockSpec((tm, D), lambda i: (i, 0)),
                  pl.BlockSpec((1, D), lambda i: (0, 0)),
                  pl.BlockSpec((D, GATE_W), lambda i: (0, 0))],
        out_specs=[pl.BlockSpec((tm, D), lambda i: (i, 0)),
                   pl.BlockSpec((tm, GATE_W), lambda i: (i, 0))],
        out_shape=[jax.ShapeDtypeStruct((n, D), MXU), jax.ShapeDtypeStruct((n, GATE_W), F32)],
        compiler_params=_cparams("arbitrary"),
    )(x, g_mix, wg)


def _matmul(a, b, out_dtype, name, tm=2048, tn=1024):
    m, k = a.shape
    n = b.shape[1]
    tm, tn = min(tm, m), min(tn, n)

    def body(a_ref, b_ref, o_ref):
        o_ref[...] = _dot(a_ref[...], b_ref[...]).astype(out_dtype)

    return pl.pallas_call(
        body, name=name, grid=(n // tn, m // tm),
        in_specs=[pl.BlockSpec((tm, k), lambda j, i: (i, 0)),
                  pl.BlockSpec((k, tn), lambda j, i: (0, j))],
        out_specs=pl.BlockSpec((tm, tn), lambda j, i: (i, j)),
        out_shape=jax.ShapeDtypeStruct((m, n), out_dtype),
        compiler_params=_cparams("arbitrary", "arbitrary"),
    )(a, b)


def _matmul_tn(a, b, name, out_dtype=F32, ta=1024, tb=1024, tk=1024):
    n, ka = a.shape
    kb = b.shape[1]
    ta, tb, tk = min(ta, ka), min(tb, kb), min(tk, n)
    nk = n // tk

    def body(a_ref, b_ref, o_ref, acc):
        kk = pl.program_id(2)

        @pl.when(kk == 0)
        def _():
            acc[...] = jnp.zeros_like(acc)

        acc[...] += _dot_tn(a_ref[...].astype(MXU), b_ref[...].astype(MXU))

        @pl.when(kk == nk - 1)
        def _():
            o_ref[...] = acc[...].astype(out_dtype)

    return pl.pallas_call(
        body, name=name, grid=(ka // ta, kb // tb, nk),
        in_specs=[pl.BlockSpec((tk, ta), lambda i, j, kk: (kk, i)),
                  pl.BlockSpec((tk, tb), lambda i, j, kk: (kk, j))],
        out_specs=pl.BlockSpec((ta, tb), lambda i, j, kk: (i, j)),
        out_shape=jax.ShapeDtypeStruct((ka, kb), out_dtype),
        scratch_shapes=[pltpu.VMEM((ta, tb), F32)],
        compiler_params=_cparams("arbitrary", "arbitrary", "arbitrary"),
    )(a, b)


def _gate_vectors(g, hd):
    gt = g.T[0:8, :]
    lane = lax.broadcasted_iota(jnp.int32, g.shape, 1)
    sub = lax.broadcasted_iota(jnp.int32, gt.shape, 0)
    col = lambda j: jnp.sum(jnp.where(lane == j, g, 0.0), axis=1, keepdims=True)
    row = lambda j: jnp.sum(jnp.where(sub == j, gt, 0.0), axis=0, keepdims=True)
    return col(hd), col(hd + NH), row(hd), row(hd + NH)


def _chunk_decay(li_col, li_row, lf_col, lf_row, m_prev):
    n = li_col.shape[0]
    r = lax.broadcasted_iota(jnp.int32, (n, n), 0)
    c = lax.broadcasted_iota(jnp.int32, (n, n), 1)
    tri = r >= c
    b_col = jnp.sum(jnp.where(tri, lf_row, 0.0), axis=1, keepdims=True)
    b_row = jnp.sum(jnp.where(r <= c, lf_col, 0.0), axis=0, keepdims=True)
    b_last = jnp.sum(lf_row, axis=1, keepdims=True)
    dmat = jnp.where(tri, b_col - b_row + li_row, -jnp.inf)
    a_col = b_col + m_prev
    g_col = b_last - b_col + li_col
    m_new = jnp.maximum(b_last + m_prev, jnp.max(g_col, axis=0, keepdims=True))
    w_col = jnp.exp(g_col - m_new)
    decay = jnp.exp(b_last + m_prev - m_new)
    return tri, dmat, a_col, m_new, w_col, decay


def _qkv_specs(row_of):
    base = C_QKV // DK
    q_spec = pl.BlockSpec((LCH, DK), lambda c, h: (row_of(c), base + (HEAD_W // DK) * h))
    k_spec = pl.BlockSpec((LCH, DK), lambda c, h: (row_of(c), base + (HEAD_W // DK) * h + 1))
    v_spec = pl.BlockSpec((LCH, DV), lambda c, h: (row_of(c), C_QKV // DV + (HEAD_W // DV) * h + 1))
    return q_spec, k_spec, v_spec


def _lane_put(col, lane_id, width=GATE_W):
    lane = lax.broadcasted_iota(jnp.int32, (col.shape[0], width), 1)
    return jnp.where(lane == lane_id, col, 0.0)


def _lane_get(block, lane_id):
    lane = lax.broadcasted_iota(jnp.int32, block.shape, 1)
    return jnp.sum(jnp.where(lane == lane_id, block, 0.0), axis=1, keepdims=True)


def _mlstm_fwd(proj, gates, bias):
    n = proj.shape[0]
    nc = n // LCH

    def body(q_ref, k_ref, v_ref, g_ref, bias_ref,
             h_ref, st_ref, cs_ref, ns_ref, ms_ref, c_scr, n_scr, m_scr):
        c = pl.program_id(0)
        hd = pl.program_id(1)

        @pl.when(c == 0)
        def _():
            c_scr[hd] = jnp.zeros((DK, DV), F32)
            n_scr[hd] = jnp.zeros((1, DK), F32)
            m_scr[hd] = jnp.full((1, GATE_W), -jnp.inf, F32)

        @pl.when(hd == 0)
        def _():
            st_ref[...] = jnp.zeros_like(st_ref)

        g = g_ref[...] + bias_ref[...]
        li_col, fr_col, li_row, fr_row = _gate_vectors(g, hd)
        m_all = m_scr[hd]
        m_prev = m_all[0:1, 0:1]
        tri, dmat, a_col, m_new, w_col, decay = _chunk_decay(
            li_col, li_row, _logsig(fr_col), _logsig(fr_row), m_prev)
        m_col = jnp.maximum(a_col, jnp.max(dmat, axis=1, keepdims=True))
        dl = jnp.exp(dmat - m_col)
        inter = jnp.exp(a_col - m_col)

        qs = q_ref[...] * (DK ** -0.5)
        kk = k_ref[...]
        vv = v_ref[...]
        cst = c_scr[hd]
        nst = n_scr[hd]
        cs_ref[...] = cst
        ns_ref[...] = nst
        ms_ref[...] = m_all

        sc = _dot_nt(qs, kk) * dl
        num = _dot(sc.astype(MXU), vv) + inter * _dot(qs, cst.astype(MXU))
        den = (jnp.sum(sc, axis=1, keepdims=True)
               + inter * jnp.sum(qs.astype(F32) * nst, axis=1, keepdims=True))
        nrm = jnp.maximum(jnp.abs(den), jnp.exp(-m_col))
        h_ref[...] = num / nrm
        st_ref[...] += _lane_put(den, hd) + _lane_put(m_col, NH + hd)

        kw = kk.astype(F32) * w_col
        c_scr[hd] = decay * cst + _dot_tn(kw.astype(MXU), vv)
        n_scr[hd] = decay * nst + jnp.sum(kw, axis=0, keepdims=True)
        m_scr[hd] = jnp.broadcast_to(m_new, (1, GATE_W))

    q_spec, k_spec, v_spec = _qkv_specs(lambda c: c)
    return pl.pallas_call(
        body, name="mlstm_fwd", grid=(nc, NH),
        in_specs=[q_spec, k_spec, v_spec,
                  pl.BlockSpec((LCH, GATE_W), lambda c, h: (c, 0)),
                  pl.BlockSpec((1, GATE_W), lambda c, h: (0, 0))],
        out_specs=[pl.BlockSpec((LCH, DV), lambda c, h: (c, h)),
                   pl.BlockSpec((LCH, GATE_W), lambda c, h: (c, 0)),
                   pl.BlockSpec((None, None, DK, DV), lambda c, h: (h, c, 0, 0)),
                   pl.BlockSpec((None, None, 1, DK), lambda c, h: (h, c, 0, 0)),
                   pl.BlockSpec((None, None, 1, GATE_W), lambda c, h: (h, c, 0, 0))],
        out_shape=[jax.ShapeDtypeStruct((n, VD), F32),
                   jax.ShapeDtypeStruct((n, GATE_W), F32),
                   jax.ShapeDtypeStruct((NH, nc, DK, DV), F32),
                   jax.ShapeDtypeStruct((NH, nc, 1, DK), F32),
                   jax.ShapeDtypeStruct((NH, nc, 1, GATE_W), F32)],
        scratch_shapes=[pltpu.VMEM((NH, DK, DV), F32), pltpu.VMEM((NH, 1, DK), F32),
                        pltpu.VMEM((NH, 1, GATE_W), F32)],
        compiler_params=_cparams("arbitrary", "arbitrary"),
    )(proj, proj, proj, gates, bias)


def _mlstm_bwd(proj, gates, bias, h, dh, stats, cs, ns, ms, dproj):
    n = proj.shape[0]
    nc = n // LCH

    def body(q_ref, k_ref, v_ref, g_ref, bias_ref, h_ref, dh_ref, st_ref,
             cs_ref, ns_ref, ms_ref, dproj_in,
             dqkv_ref, dg_ref, gsum_ref, dc_scr, dn_scr):
        del dproj_in
        step = pl.program_id(0)
        hd = pl.program_id(1)

        @pl.when(step == 0)
        def _():
            dc_scr[hd] = jnp.zeros((DK, DV), F32)
            dn_scr[hd] = jnp.zeros((1, DK), F32)

        @pl.when((step == 0) & (hd == 0))
        def _():
            gsum_ref[...] = jnp.zeros_like(gsum_ref)

        @pl.when(hd == 0)
        def _():
            dg_ref[...] = jnp.zeros_like(dg_ref)

        g = g_ref[...] + bias_ref[...]
        li_col, fr_col, li_row, fr_row = _gate_vectors(g, hd)
        m_prev = ms_ref[0:1, 0:1]
        tri, dmat, a_col, m_new, w_col, decay = _chunk_decay(
            li_col, li_row, _logsig(fr_col), _logsig(fr_row), m_prev)
        stats = st_ref[...]
        m_col = _lane_get(stats, NH + hd)
        dl = jnp.exp(dmat - m_col)
        inter = jnp.exp(a_col - m_col)

        qs = q_ref[...] * (DK ** -0.5)
        kk = k_ref[...]
        vv = v_ref[...]
        qf = qs.astype(F32)
        kf = kk.astype(F32)
        cst = cs_ref[...]
        nst = ns_ref[...]
        cb = cst.astype(MXU)
        dcn = dc_scr[hd]
        dnn = dn_scr[hd]
        dcb = dcn.astype(MXU)

        den = _lane_get(stats, hd)
        floor = jnp.exp(-m_col)
        nrm = jnp.maximum(jnp.abs(den), floor)
        dhv = dh_ref[...]
        dnum = dhv / nrm
        dnum_b = dnum.astype(MXU)
        dhh = jnp.sum(dhv * h_ref[...], axis=1, keepdims=True)
        dden = jnp.where(jnp.abs(den) > floor, -dhh / nrm * jnp.sign(den), 0.0)

        sc = _dot_nt(qs, kk) * dl
        dsc = _dot_nt(dnum_b, vv) + dden
        da = (dl * dsc).astype(MXU)
        gmat = sc * dsc

        dq = _dot(da, kk) + inter * (_dot_nt(dnum_b, cb) + dden * nst)
        dk_state = w_col * (_dot_nt(vv, dcb) + dnn)
        dk = _dot_tn(da, qs) + dk_state
        kw = (kf * w_col).astype(MXU)
        dv = _dot_tn(sc.astype(MXU), dnum_b) + _dot(kw, dcb)
        dqkv_ref[:, 0:DK] = (dq * (DK ** -0.5)).astype(dqkv_ref.dtype)
        dqkv_ref[:, DK:2 * DK] = dk.astype(dqkv_ref.dtype)
        dqkv_ref[:, 2 * DK:HEAD_W] = dv.astype(dqkv_ref.dtype)

        num_i = _dot(qs, cb)
        den_i = jnp.sum(qf * nst, axis=1, keepdims=True)
        e_col = inter * (jnp.sum(dnum * num_i, axis=1, keepdims=True) + dden * den_i)
        h_col = jnp.sum(kf * dk_state, axis=1, keepdims=True)
        f_dec = decay * (jnp.sum(jnp.sum(cst * dcn, axis=1, keepdims=True), axis=0, keepdims=True)
                         + jnp.sum(nst * dnn, axis=1, keepdims=True))
        r = lax.broadcasted_iota(jnp.int32, (LCH, LCH), 0)
        c = lax.broadcasted_iota(jnp.int32, (LCH, LCH), 1)
        eye = r == c
        to_col = lambda row: jnp.sum(jnp.where(eye, row, 0.0), axis=1, keepdims=True)
        row_g = jnp.sum(gmat, axis=1, keepdims=True)
        col_g = to_col(jnp.sum(gmat, axis=0, keepdims=True))
        last = lax.broadcasted_iota(jnp.int32, (LCH, 1), 0) == LCH - 1
        db_col = row_g - col_g + e_col - h_col
        db_col = db_col + jnp.where(last, jnp.sum(h_col, axis=0, keepdims=True) + f_dec, 0.0)
        dli_col = col_g + h_col
        dlf_row = jnp.sum(jnp.where(tri, db_col, 0.0), axis=0, keepdims=True)
        df_col = to_col(dlf_row) * _sigmoid(-fr_col)
        dg = _lane_put(dli_col, hd) + _lane_put(df_col, NH + hd)
        dg_ref[...] += dg
        gsum_ref[0:1, 0:GATE_W] += jnp.sum(dg, axis=0, keepdims=True)

        dc_scr[hd] = decay * dcn + _dot_tn((qf * inter).astype(MXU), dnum_b)
        dn_scr[hd] = decay * dnn + jnp.sum(qf * (inter * dden), axis=0, keepdims=True)

    rev = lambda c: nc - 1 - c
    q_spec, k_spec, v_spec = _qkv_specs(rev)
    hv_spec = pl.BlockSpec((LCH, DV), lambda c, h: (rev(c), h))
    gate_spec = pl.BlockSpec((LCH, GATE_W), lambda c, h: (rev(c), 0))
    return pl.pallas_call(
        body, name="mlstm_bwd", grid=(nc, NH),
        in_specs=[q_spec, k_spec, v_spec, gate_spec,
                  pl.BlockSpec((1, GATE_W), lambda c, h: (0, 0)),
                  hv_spec, hv_spec, gate_spec,
                  pl.BlockSpec((None, None, DK, DV), lambda c, h: (h, rev(c), 0, 0)),
                  pl.BlockSpec((None, None, 1, DK), lambda c, h: (h, rev(c), 0, 0)),
                  pl.BlockSpec((None, None, 1, GATE_W), lambda c, h: (h, rev(c), 0, 0)),
                  ANY],
        out_specs=[pl.BlockSpec((LCH, HEAD_W), lambda c, h: (rev(c), C_QKV // HEAD_W + h)),
                   gate_spec,
                   pl.BlockSpec((8, D), lambda c, h: (0, 0))],
        out_shape=[jax.ShapeDtypeStruct(dproj.shape, dproj.dtype),
                   jax.ShapeDtypeStruct((n, GATE_W), F32),
                   jax.ShapeDtypeStruct((8, D), F32)],
        scratch_shapes=[pltpu.VMEM((NH, DK, DV), F32), pltpu.VMEM((NH, 1, DK), F32)],
        input_output_aliases={11: 0},
        compiler_params=_cparams("arbitrary", "arbitrary"),
    )(proj, proj, proj, gates, bias, h, dh, stats, cs, ns, ms, dproj)


def _proj_spec(tm, col, width):
    return pl.BlockSpec((tm, width), lambda i: (i, col // width))


def _halo_prev(tm, col):
    return pl.BlockSpec((8, 1024), lambda i: (jnp.maximum(i * (tm // 8) - 1, 0), col // 1024))


def _const(shape):
    return pl.BlockSpec(shape, lambda i: (0,) * len(shape))


def _conv_inputs(i, tm, xa_ref, ca_ref, xah_ref, cah_ref):
    u = _f32(xa_ref) * _f32(ca_ref)
    uh = jnp.where(i > 0, _f32(xah_ref) * _f32(cah_ref), 0.0)
    rid = lax.broadcasted_iota(jnp.int32, u.shape, 0)
    u1 = jnp.where(rid == 0, uh[7:8, :], pltpu.roll(u, 1, 0))
    u2 = jnp.where(rid == 0, uh[6:7, :], jnp.where(rid == 1, uh[7:8, :], pltpu.roll(u, 2, 0)))
    return u, u1, u2


def _head_norm(hh, gh):
    out = []
    for j in range(NH):
        hj = hh[:, j * DV:(j + 1) * DV]
        rj = _rstd(hj)
        out.append((hj * rj, rj, gh[:, j * DV:(j + 1) * DV]))
    return out


def _tail_fwd(proj, h, x, p, t, cw, cb, gh, gple, gfin, wa, wb, wo, wpg, wp):
    n = x.shape[0]
    tm = min(256, n)

    def body(ba_r, za_r, o_r, zb_r, ga_r, gb_r, xa_r, ca_r, xah_r, cah_r, h_r, x_r, p_r, t_r,
             cw_r, cb_r, gh_r, gple_r, gfin_r, wa_r, wb_r, wo_r, wpg_r, wp_r,
             apre_o, bpre_o, mg_o, xn1_o, de_o, dgp_o, ya_o, yb_o, x1_o, dx2_o, acc_o):
        i = pl.program_id(0)

        @pl.when(i == 0)
        def _():
            acc_o[...] = jnp.zeros_like(acc_o)

        u, u1, u2 = _conv_inputs(i, tm, xa_r, ca_r, xah_r, cah_r)
        cwv = cw_r[...]
        cv = cwv[0:1, :] * u2 + cwv[1:2, :] * u1 + cwv[2:3, :] * u + cb_r[...]
        za = _f32(za_r)
        a_pre = (_f32(ba_r) * cv * (za * _sigmoid(za))).astype(MXU)
        apre_o[...] = a_pre
        ya = _dot(a_pre, wa_r[...])

        hn = _head_norm(h_r[...], gh_r[...])
        hbn = jnp.concatenate([xh * g for xh, _, g in hn], axis=1)
        zb = _f32(zb_r)
        b_pre = (_sigmoid(_f32(o_r)) * hbn * (zb * _sigmoid(zb))).astype(MXU)
        bpre_o[...] = b_pre
        yb = _dot(b_pre, wb_r[...])
        ya_o[...] = ya.astype(MXU)
        yb_o[...] = yb.astype(MXU)

        mg = (_sigmoid(_f32(ga_r)) * ya + _sigmoid(_f32(gb_r)) * yb).astype(MXU)
        mg_o[...] = mg
        x1 = x_r[...] + _dot(mg, wo_r[...])
        x1_o[...] = x1
        xn1 = (x1 * _rstd(x1) * gple_r[...]).astype(MXU)
        xn1_o[...] = xn1
        gt = _sigmoid(_dot(xn1, wpg_r[...]))
        e = _dot(p_r[...].astype(MXU), wp_r[...])
        x2 = x1 + gt * e
        r2 = _rstd(x2)
        xh2 = x2 * r2
        gf = gfin_r[...]
        diff = xh2 * gf - t_r[...]
        dy = diff * (1.0 / D)
        dx2 = _norm_bwd(dy, xh2, r2, gf)
        dx2_o[...] = dx2
        de_o[...] = (dx2 * gt).astype(MXU)
        dgp_o[...] = (dx2 * e * gt * (1.0 - gt)).astype(MXU)
        acc_o[0:1, :] += jnp.sum(dy * xh2, axis=0, keepdims=True)
        loss = 0.5 * jnp.sum(jnp.sum(diff * diff, axis=1, keepdims=True), axis=0, keepdims=True) * (1.0 / D)
        acc_o[1:2, :] += jnp.broadcast_to(loss, (1, D))

    row = lambda w, dt: (pl.BlockSpec((tm, w), lambda i: (i, 0)), jax.ShapeDtypeStruct((n, w), dt))
    outs = [row(D, MXU), row(VD, MXU), row(D, MXU), row(D, MXU), row(D, MXU), row(D, MXU),
            row(D, MXU), row(D, MXU), row(D, F32), row(D, F32),
            (_const((8, D)), jax.ShapeDtypeStruct((8, D), F32))]
    return pl.pallas_call(
        body, name="tail_fwd", grid=(n // tm,),
        in_specs=[_proj_spec(tm, C_BA, 1024), _proj_spec(tm, C_ZA, 1024),
                  _proj_spec(tm, C_O, 2048), _proj_spec(tm, C_ZB, 2048),
                  _proj_spec(tm, C_GA, 1024), _proj_spec(tm, C_GB, 1024),
                  _proj_spec(tm, C_XA, 1024), _proj_spec(tm, C_CA, 1024),
                  _halo_prev(tm, C_XA), _halo_prev(tm, C_CA),
                  pl.BlockSpec((tm, VD), lambda i: (i, 0)),
                  pl.BlockSpec((tm, D), lambda i: (i, 0)),
                  pl.BlockSpec((tm, PLE), lambda i: (i, 0)),
                  pl.BlockSpec((tm, D), lambda i: (i, 0)),
                  _const((8, D)), _const((1, D)), _const((1, VD)), _const((1, D)), _const((1, D)),
                  _const((D, D)), _const((VD, D)), _const((D, D)), _const((D, D)), _const((PLE, D))],
        out_specs=[s for s, _ in outs],
        out_shape=[s for _, s in outs],
        compiler_params=_cparams("arbitrary"),
    )(*([proj] * 10), h, x, p, t, cw, cb, gh, gple, gfin, wa, wb, wo, wpg, wp)


def _tail_bwd(proj, h, dgp, dx2, x1, ya, yb, cw, cb, gh, gple, wpg, wo, wb, wa):
    n = x1.shape[0]
    tm = min(256, n)

    def body(ba_r, za_r, o_r, zb_r, ga_r, gb_r, xa_r, ca_r, xah_r, cah_r, h_r,
             dgp_r, dx2_r, x1_r, ya_r, yb_r, cw_r, cb_r, gh_r, gple_r,
             wpg_r, wo_r, wb_r, wa_r,
             dproj_o, dcv_o, dh_o, dx1_o, dx1b_o, dya_o, dyb_o, acc_o):
        i = pl.program_id(0)

        @pl.when(i == 0)
        def _():
            acc_o[...] = jnp.zeros_like(acc_o)

        dxn1 = _dot_nt(dgp_r[...], wpg_r[...])
        x1 = x1_r[...]
        r1 = _rstd(x1)
        xh1 = x1 * r1
        acc_o[0:1, 0:D] += jnp.sum(dxn1 * xh1, axis=0, keepdims=True)
        dx1 = dx2_r[...] + _norm_bwd(dxn1, xh1, r1, gple_r[...])
        dx1_o[...] = dx1
        dx1b = dx1.astype(MXU)
        dx1b_o[...] = dx1b

        dmg = _dot_nt(dx1b, wo_r[...])
        sga = _sigmoid(_f32(ga_r))
        sgb = _sigmoid(_f32(gb_r))
        dya = (dmg * sga).astype(MXU)
        dyb = (dmg * sgb).astype(MXU)
        dya_o[...] = dya
        dyb_o[...] = dyb
        dproj_o[:, C_GA:C_GA + D] = (dmg * _f32(ya_r) * sga * (1.0 - sga)).astype(MXU)
        dproj_o[:, C_GB:C_GB + D] = (dmg * _f32(yb_r) * sgb * (1.0 - sgb)).astype(MXU)

        db_pre = _dot_nt(dyb, wb_r[...])
        hn = _head_norm(h_r[...], gh_r[...])
        hbn = jnp.concatenate([xh * g for xh, _, g in hn], axis=1)
        so = _sigmoid(_f32(o_r))
        zb = _f32(zb_r)
        szb = _sigmoid(zb)
        sb = zb * szb
        dproj_o[:, C_O:C_O + VD] = (db_pre * hbn * sb * so * (1.0 - so)).astype(MXU)
        dproj_o[:, C_ZB:C_ZB + VD] = (db_pre * so * hbn * szb * (1.0 + zb * (1.0 - szb))).astype(MXU)
        dhbn = db_pre * so * sb
        for j, (xh, rj, g) in enumerate(hn):
            dj = dhbn[:, j * DV:(j + 1) * DV]
            acc_o[1:2, j * DV:(j + 1) * DV] += jnp.sum(dj * xh, axis=0, keepdims=True)
            dh_o[:, j * DV:(j + 1) * DV] = _norm_bwd(dj, xh, rj, g)

        da_pre = _dot_nt(dya, wa_r[...])
        u, u1, u2 = _conv_inputs(i, tm, xa_r, ca_r, xah_r, cah_r)
        cwv = cw_r[...]
        cv = cwv[0:1, :] * u2 + cwv[1:2, :] * u1 + cwv[2:3, :] * u + cb_r[...]
        za = _f32(za_r)
        sza = _sigmoid(za)
        sa = za * sza
        ba = _f32(ba_r)
        dproj_o[:, C_BA:C_BA + D] = (da_pre * cv * sa).astype(MXU)
        dproj_o[:, C_ZA:C_ZA + D] = (da_pre * ba * cv * sza * (1.0 + za * (1.0 - sza))).astype(MXU)
        dcv = da_pre * ba * sa
        dcv_o[...] = dcv.astype(MXU)
        acc_o[2:3, 0:D] += jnp.sum(dcv, axis=0, keepdims=True)
        acc_o[3:4, 0:D] += jnp.sum(dcv * u2, axis=0, keepdims=True)
        acc_o[4:5, 0:D] += jnp.sum(dcv * u1, axis=0, keepdims=True)
        acc_o[5:6, 0:D] += jnp.sum(dcv * u, axis=0, keepdims=True)

    row = lambda w, dt: (pl.BlockSpec((tm, w), lambda i: (i, 0)), jax.ShapeDtypeStruct((n, w), dt))
    outs = [(pl.BlockSpec((tm, TAIL_W), lambda i: (i, 0)), jax.ShapeDtypeStruct((n, NMAIN), MXU)),
            row(D, MXU), row(VD, F32), row(D, F32), row(D, MXU), row(D, MXU), row(D, MXU),
            (_const((8, VD)), jax.ShapeDtypeStruct((8, VD), F32))]
    rowin = lambda w: pl.BlockSpec((tm, w), lambda i: (i, 0))
    return pl.pallas_call(
        body, name="tail_bwd", grid=(n // tm,),
        in_specs=[_proj_spec(tm, C_BA, 1024), _proj_spec(tm, C_ZA, 1024),
                  _proj_spec(tm, C_O, 2048), _proj_spec(tm, C_ZB, 2048),
                  _proj_spec(tm, C_GA, 1024), _proj_spec(tm, C_GB, 1024),
                  _proj_spec(tm, C_XA, 1024), _proj_spec(tm, C_CA, 1024),
                  _halo_prev(tm, C_XA), _halo_prev(tm, C_CA),
                  rowin(VD), rowin(D), rowin(D), rowin(D), rowin(D), rowin(D),
                  _const((8, D)), _const((1, D)), _const((1, VD)), _const((1, D)),
                  _const((D, D)), _const((D, D)), _const((VD, D)), _const((D, D))],
        out_specs=[s for s, _ in outs],
        out_shape=[s for _, s in outs],
        compiler_params=_cparams("arbitrary"),
    )(*([proj] * 10), h, dgp, dx2, x1, ya, yb, cw, cb, gh, gple, wpg, wo, wb, wa)


def _conv_bwd(proj, dcv, cw, dproj):
    n = dcv.shape[0]
    tm = min(512, n)
    nt = n // tm

    def body(xa_r, ca_r, dcv_r, nxt_r, cw_r, dproj_in, dxc_o):
        del dproj_in
        i = pl.program_id(0)
        dcv_v = _f32(dcv_r)
        nxt = jnp.where(i < nt - 1, _f32(nxt_r), 0.0)
        rid = lax.broadcasted_iota(jnp.int32, dcv_v.shape, 0)
        d1 = jnp.where(rid == tm - 1, nxt[0:1, :], pltpu.roll(dcv_v, tm - 1, 0))
        d2 = jnp.where(rid == tm - 2, nxt[0:1, :],
                       jnp.where(rid == tm - 1, nxt[1:2, :], pltpu.roll(dcv_v, tm - 2, 0)))
        cwv = cw_r[...]
        du = cwv[2:3, :] * dcv_v + cwv[1:2, :] * d1 + cwv[0:1, :] * d2
        dxc_o[:, 0:D] = (du * _f32(ca_r)).astype(MXU)
        dxc_o[:, D:2 * D] = (du * _f32(xa_r)).astype(MXU)

    return pl.pallas_call(
        body, name="conv_bwd", grid=(nt,),
        in_specs=[_proj_spec(tm, C_XA, 1024), _proj_spec(tm, C_CA, 1024),
                  pl.BlockSpec((tm, D), lambda i: (i, 0)),
                  pl.BlockSpec((8, D), lambda i: (jnp.minimum((i + 1) * (tm // 8), n // 8 - 1), 0)),
                  _const((8, D)), ANY],
        out_specs=pl.BlockSpec((tm, CONV_W), lambda i: (i, C_XA // CONV_W)),
        out_shape=jax.ShapeDtypeStruct(dproj.shape, dproj.dtype),
        input_output_aliases={5: 0},
        compiler_params=_cparams("arbitrary"),
    )(proj, proj, dcv, dcv, cw, dproj)


def _position():
    return lax.axis_index("x"), lax.axis_index("y"), lax.axis_index("c")


def _flip(v, bit):
    return 1 - v if bit else v


def _scatter_copies(srcs, dsts, send_sems, recv_sems, local_sems):
    x, y, c = _position()
    me = 4 * x + 2 * y + c
    na = len(srcs)
    copies = []
    for r in range(N_DEV):
        px, py, pc = _flip(x, (r >> 2) & 1), _flip(y, (r >> 1) & 1), _flip(c, r & 1)
        for a in range(na):
            rows = dsts[a].shape[1]
            src = srcs[a].at[2 * px + py, pl.ds(pl.multiple_of(pc * rows, rows), rows)]
            dst = dsts[a].at[me]
            if r == 0:
                copies.append(pltpu.make_async_copy(src, dst, local_sems.at[a]))
            else:
                k = (r - 1) * na + a
                copies.append(pltpu.make_async_remote_copy(
                    src_ref=src, dst_ref=dst, send_sem=send_sems.at[k], recv_sem=recv_sems.at[k],
                    device_id=(px, py, pc), device_id_type=MESH))
    return copies


def _input_grad(dproj, dgates, w_main, w_gate, x, dx1, g_mix, g_in, g_rest):
    n = x.shape[0]
    tm, tk = min(1024, n), 2048
    nk = NMAIN // tk
    nt = n // tm

    def body(dp_r, dg_r, w_r, wg_r, x_r, dx1_r, g_r, gin, grest,
             gx_o, acc_o, oin, orest, acc, send_sems, recv_sems, local_sems):
        i = pl.program_id(0)
        kk = pl.program_id(1)
        copies = _scatter_copies((gin, grest), (oin, orest), send_sems, recv_sems, local_sems)

        @pl.when((i == 0) & (kk == 0))
        def _():
            acc_o[...] = jnp.zeros_like(acc_o)
            for cp in copies:
                cp.start()

        @pl.when(kk == 0)
        def _():
            acc[...] = _dot_nt(dg_r[...].astype(MXU), wg_r[...])

        acc[...] += _dot_nt(dp_r[...], w_r[...])

        @pl.when(kk == nk - 1)
        def _():
            dhn = acc[...]
            xv = x_r[...]
            r0 = _rstd(xv)
            xh = xv * r0
            acc_o[0:1, :] += jnp.sum(dhn * xh, axis=0, keepdims=True)
            gx_o[...] = dx1_r[...] + _norm_bwd(dhn, xh, r0, g_r[...])

        @pl.when((i == nt - 1) & (kk == nk - 1))
        def _():
            for cp in copies:
                cp.wait()

    nrem = 2 * (N_DEV - 1)
    r_in, r_rest = g_in.shape[1] // 2, g_rest.shape[1] // 2
    return pl.pallas_call(
        body, name="input_grad", grid=(nt, nk),
        in_specs=[pl.BlockSpec((tm, tk), lambda i, kk: (i, kk)),
                  pl.BlockSpec((tm, GATE_W), lambda i, kk: (i, 0)),
                  pl.BlockSpec((D, tk), lambda i, kk: (0, kk)),
                  pl.BlockSpec((D, GATE_W), lambda i, kk: (0, 0)),
                  pl.BlockSpec((tm, D), lambda i, kk: (i, 0)),
                  pl.BlockSpec((tm, D), lambda i, kk: (i, 0)),
                  pl.BlockSpec((1, D), lambda i, kk: (0, 0)),
                  ANY, ANY],
        out_specs=[pl.BlockSpec((tm, D), lambda i, kk: (i, 0)),
                   pl.BlockSpec((8, D), lambda i, kk: (0, 0)),
                   ANY, ANY],
        out_shape=[jax.ShapeDtypeStruct((n, D), F32), jax.ShapeDtypeStruct((8, D), F32),
                   jax.ShapeDtypeStruct((N_DEV, r_in) + g_in.shape[2:], g_in.dtype),
                   jax.ShapeDtypeStruct((N_DEV, r_rest) + g_rest.shape[2:], g_rest.dtype)],
        scratch_shapes=[pltpu.VMEM((tm, D), F32),
                        pltpu.SemaphoreType.DMA((nrem,)), pltpu.SemaphoreType.DMA((nrem,)),
                        pltpu.SemaphoreType.DMA((2,))],
        compiler_params=_cparams("arbitrary", "arbitrary"),
    )(dproj, dgates, w_main, w_gate, x, dx1, g_mix, g_in, g_rest)


def _pack_small(acc_f, acc_b, acc_x, gsum):
    def body(f_r, b_r, x_r, s_r, o_r):
        o_r[...] = jnp.zeros_like(o_r)
        o_r[0:1, :] = x_r[0:1, :]
        o_r[1:2, :] = b_r[2:3, 0:D]
        o_r[2:3, :] = b_r[1:2, 0:D]
        o_r[3:4, :] = b_r[1:2, D:2 * D]
        o_r[4:5, :] = b_r[0:1, 0:D]
        o_r[5:6, :] = f_r[0:1, :]
        o_r[6:9, :] = b_r[3:6, 0:D]
        lane = lax.broadcasted_iota(jnp.int32, (1, D), 1)
        o_r[9:10, :] = jnp.where(lane < 2 * NH, s_r[0:1, :], jnp.where(lane == 2 * NH, f_r[1:2, :], 0.0))

    return pl.pallas_call(
        body, name="pack_small",
        out_shape=jax.ShapeDtypeStruct((SMALL_ROWS, D), F32),
    )(acc_f, acc_b, acc_x, gsum)


def _sum_slots(r, name, tr=64, half=None):
    s, rows, w = r.shape
    tr = min(tr, rows)
    assert rows % tr == 0
    nt = rows // tr

    def body(*refs):
        r_ref, o_ref = refs[-2:]
        tot = r_ref[0].astype(F32)
        for k in range(1, s):
            tot = tot + r_ref[k].astype(F32)
        o_ref[...] = tot

    if half is None:
        return pl.pallas_call(
            body, name=name, grid=(nt,),
            in_specs=[pl.BlockSpec((s, tr, w), lambda i: (0, i, 0))],
            out_specs=pl.BlockSpec((tr, w), lambda i: (i, 0)),
            out_shape=jax.ShapeDtypeStruct((rows, w), F32),
            compiler_params=_cparams("arbitrary"),
        )(r)
    return pl.pallas_call(
        body, name=name,
        grid_spec=pltpu.PrefetchScalarGridSpec(
            num_scalar_prefetch=1, grid=(nt,),
            in_specs=[pl.BlockSpec((s, tr, w), lambda i, hf: (0, i, 0))],
            out_specs=pl.BlockSpec((tr, w), lambda i, hf: (hf[0] * nt + i, 0))),
        out_shape=jax.ShapeDtypeStruct((2 * rows, w), F32),
        compiler_params=_cparams("arbitrary"),
    )(half, r)


def _adamw(w, g, m, v, name, g_row0=0, tr=64):
    lead = w.ndim == 3
    rows, cols = w.shape[-2:]
    tr = min(tr, rows)
    assert rows % tr == 0 and g_row0 % tr == 0 and g.shape[1] == cols
    c1 = 1.0 - ADAM_B1 ** ADAM_STEP
    c2 = 1.0 - ADAM_B2 ** ADAM_STEP

    def body(w_r, g_r, m_r, v_r, d_o, m_o, v_o):
        gv = g_r[...]
        mn = ADAM_B1 * m_r[...] + (1.0 - ADAM_B1) * gv
        vn = ADAM_B2 * v_r[...] + (1.0 - ADAM_B2) * (gv * gv)
        m_o[...] = mn
        v_o[...] = vn
        d_o[...] = -ADAM_LR * ((mn / c1) / (jnp.sqrt(vn / c2) + ADAM_EPS) + ADAM_WD * w_r[...])

    if lead:
        spec = pl.BlockSpec((None, tr, cols), lambda i: (0, i, 0))
    else:
        spec = pl.BlockSpec((tr, cols), lambda i: (i, 0))
    g_spec = pl.BlockSpec((tr, cols), lambda i: (g_row0 // tr + i, 0))
    shp = jax.ShapeDtypeStruct(w.shape, F32)
    return pl.pallas_call(
        body, name=name, grid=(rows // tr,),
        in_specs=[spec, g_spec, spec, spec], out_specs=[spec] * 3, out_shape=[shp] * 3,
        compiler_params=_cparams("arbitrary"),
    )(w, g, m, v)


def _allgather_chips(halved, whole):
    arrs = list(halved) + list(whole)
    nh, na = len(halved), len(arrs)
    relations = ((1, 0), (0, 1), (1, 1))
    n_ici = 3 * na

    def body(*refs):
        ins, outs = refs[:na], refs[na:2 * na]
        send_sems, recv_sems = refs[2 * na:]
        x, y, c = _position()
        mine = 2 * x + y

        def half(a, chip, core):
            rows = ins[a].shape[0] // 2
            return outs[a].at[chip, pl.ds(pl.multiple_of(core * rows, rows), rows)]

        def remote(src, dst, k, to):
            return pltpu.make_async_remote_copy(src_ref=src, dst_ref=dst, send_sem=send_sems.at[k],
                                                recv_sem=recv_sems.at[k], device_id=to, device_id_type=MESH)

        first, passed = [], []
        for j, (rx, ry) in enumerate(relations):
            px, py = _flip(x, rx), _flip(y, ry)
            for a in range(na):
                if a < nh:
                    rows = ins[a].shape[0] // 2
                    src = ins[a].at[pl.ds(pl.multiple_of(c * rows, rows), rows)]
                    first.append(remote(src, half(a, mine, c), j * na + a, (px, py, c)))
                    landed = half(a, 2 * px + py, c)
                    passed.append(remote(landed, landed, n_ici + j * nh + a, (x, y, 1 - c)))
                else:
                    first.append(remote(ins[a], outs[a].at[mine], j * na + a, (px, py, c)))
        for cp in first:
            cp.start()
        k = 0
        for j in range(3):
            for a in range(na):
                first[j * na + a].wait_recv()
                if a < nh:
                    passed[k].start()
                    k += 1
        for cp in passed:
            cp.wait_recv()
        for cp in first + passed:
            cp.wait_send()

    nsem = n_ici + 3 * nh
    outs = pl.pallas_call(
        body, name="allgather_weights",
        in_specs=[ANY] * na, out_specs=[ANY] * na,
        out_shape=[jax.ShapeDtypeStruct((N_CHIPS,) + a.shape, a.dtype) for a in arrs],
        scratch_shapes=[pltpu.SemaphoreType.DMA((nsem,)), pltpu.SemaphoreType.DMA((nsem,))],
    )(*arrs)
    chip = 2 * lax.axis_index("x") + lax.axis_index("y")
    return [lax.dynamic_update_slice(o, a[None], (chip,) + (0,) * a.ndim) for o, a in zip(outs, arrs)]


def _join_halves(fulls, small):
    na = len(fulls)

    def body(*refs):
        ins, sm = refs[:na], refs[na]
        outs, osm = refs[na + 1:2 * na + 1], refs[2 * na + 1]
        send_sems, recv_sems, local_sem = refs[2 * na + 2:]
        del ins
        x, y, c = _position()
        me = 4 * x + 2 * y + c
        copies = []
        for a in range(na):
            rows = outs[a].shape[0] // 2
            mine = outs[a].at[pl.ds(pl.multiple_of(c * rows, rows), rows)]
            copies.append(pltpu.make_async_remote_copy(
                src_ref=mine, dst_ref=mine, send_sem=send_sems.at[a], recv_sem=recv_sems.at[a],
                device_id=(x, y, 1 - c), device_id_type=MESH))
        copies.append(pltpu.make_async_copy(sm, osm.at[me], local_sem))
        for r in range(1, N_DEV):
            to = (_flip(x, (r >> 2) & 1), _flip(y, (r >> 1) & 1), _flip(c, r & 1))
            copies.append(pltpu.make_async_remote_copy(
                src_ref=sm, dst_ref=osm.at[me], send_sem=send_sems.at[na + r - 1],
                recv_sem=recv_sems.at[na + r - 1], device_id=to, device_id_type=MESH))
        for cp in copies:
            cp.start()
        for cp in copies:
            cp.wait()

    nsem = na + N_DEV - 1
    return pl.pallas_call(
        body, name="join_halves",
        in_specs=[ANY] * (na + 1), out_specs=[ANY] * (na + 1),
        out_shape=[jax.ShapeDtypeStruct(a.shape, a.dtype) for a in fulls]
        + [jax.ShapeDtypeStruct((N_DEV,) + small.shape, small.dtype)],
        scratch_shapes=[pltpu.SemaphoreType.DMA((nsem,)), pltpu.SemaphoreType.DMA((nsem,)),
                        pltpu.SemaphoreType.DMA],
        input_output_aliases={a: a for a in range(na)},
    )(*fulls, small)


def _to_internal(w_glob):
    order = sorted(SEGMENTS, key=lambda s: s[2])
    main = jnp.concatenate([w_glob[..., g0:g0 + w] for g0, w, _ in order], axis=-1)
    gate = w_glob[..., GATE_COL:GATE_COL + 2 * NH]
    pad = [(0, 0)] * (w_glob.ndim - 1) + [(0, GATE_W - 2 * NH)]
    return main, jnp.pad(gate, pad)


def _to_global(main, gate):
    parts = sorted([(g0, main[..., i0:i0 + w]) for g0, w, i0 in SEGMENTS]
                   + [(GATE_COL, gate[..., 0:2 * NH])], key=lambda s: s[0])
    return jnp.concatenate([p for _, p in parts], axis=-1)


def _pack_rows(wa, wb, wo, wpg, wp):
    return jnp.concatenate([wa, wb, wo, wpg, wp.reshape(PACK_ROWS[4], D)], axis=0)


def _pad_rows(a, rows=8):
    return jnp.pad(a, ((0, rows - a.shape[0]), (0, 0)))


def kernel(x, p, g_mix, w_in, conv_w, conv_b, w_a_out, b_gates, g_head, w_b_out, w_o, g_ple, w_ple_gate, w_ple, g_final, loss_target, m_g_mix, m_w_in, m_conv_w, m_conv_b, m_w_a_out, m_b_gates, m_g_head, m_w_b_out, m_w_o, m_g_ple, m_w_ple_gate, m_w_ple, m_g_final, v_g_mix, v_w_in, v_conv_w, v_conv_b, v_w_a_out, v_b_gates, v_g_head, v_w_b_out, v_w_o, v_g_ple, v_w_ple_gate, v_w_ple, v_g_final):
    chip = 2 * lax.axis_index("x") + lax.axis_index("y")
    xs, ps, ts = x[0], p[0, 0], loss_target[0]
    g_fin = g_final.reshape(1, D)

    pack_w = _pack_rows(w_a_out[0], w_b_out[0], w_o[0], w_ple_gate[0], w_ple[0])
    g_win, g_pack, g_cw = _allgather_chips(
        [w_in[0].astype(MXU), pack_w.astype(MXU)], [_pad_rows(conv_w[0])])
    w_glob = jnp.transpose(g_win, (1, 0, 2)).reshape(D, N_IN)
    w_main, w_gate = _to_internal(w_glob)
    offs = [0, 256, 768, 1024, 1280, 1344]
    wa, wb, wo, wpg = [g_pack[:, offs[k]:offs[k + 1]].reshape(-1, D) for k in range(4)]
    wp = jnp.transpose(g_pack[:, offs[4]:offs[5]].reshape(N_CHIPS, PLE, PLE), (1, 0, 2)).reshape(PLE, D)
    cw = jnp.transpose(g_cw, (1, 0, 2)).reshape(8, D)

    bias = jnp.pad(b_gates, ((0, 0), (0, GATE_W - 2 * NH)))
    hn, gates = _prenorm(xs, g_mix, w_gate)
    proj = _matmul(hn, w_main, MXU, "proj")
    h, stats, cs, ns, ms = _mlstm_fwd(proj, gates, bias)
    (a_pre, b_pre, mg, xn1, de, dgp, ya, yb, x1, dx2, acc_f) = _tail_fwd(
        proj, h, xs, ps, ts, cw, conv_b, g_head, g_ple, g_fin, wa, wb, wo, wpg, wp)
    dproj, dcv, dh, dx1, dx1b, dya, dyb, acc_b = _tail_bwd(
        proj, h, dgp, dx2, x1, ya, yb, cw, conv_b, g_head, g_ple, wpg, wo, wb, wa)
    dproj = _conv_bwd(proj, dcv, cw, dproj)
    dproj, dgates, gsum = _mlstm_bwd(proj, gates, bias, h, dh, stats, cs, ns, ms, dproj)
    d_main = _matmul_tn(hn, dproj, "dw_in", out_dtype=WIRE)
    d_gate = _matmul_tn(hn, dgates, "dw_gate", out_dtype=WIRE)
    d_wa = _matmul_tn(a_pre, dya, "dw_a_out", out_dtype=WIRE)
    d_wb = _matmul_tn(b_pre, dyb, "dw_b_out", out_dtype=WIRE)
    d_wo = _matmul_tn(mg, dx1b, "dw_o", out_dtype=WIRE)
    d_wpg = _matmul_tn(xn1, dgp, "dw_ple_gate", out_dtype=WIRE)
    d_wp = _matmul_tn(ps, de, "dw_ple", out_dtype=WIRE)

    g_in = jnp.transpose(_to_global(d_main, d_gate).reshape(D, N_CHIPS, N_IN // N_CHIPS), (1, 0, 2))
    d_wp_c = jnp.transpose(d_wp.reshape(PLE, N_CHIPS, PLE), (1, 0, 2)).reshape(N_CHIPS, PACK_ROWS[4], D)
    g_rest = jnp.concatenate(
        [d_wa.reshape(N_CHIPS, -1, D), d_wb.reshape(N_CHIPS, -1, D), d_wo.reshape(N_CHIPS, -1, D),
         d_wpg.reshape(N_CHIPS, -1, D), d_wp_c], axis=1)
    grad_x, acc_x, r_in, r_rest = _input_grad(dproj, dgates, w_main, w_gate, xs, dx1, g_mix, g_in, g_rest)
    small = _pack_small(acc_f, acc_b, acc_x, gsum)
    core = lax.axis_index("c").astype(jnp.int32).reshape(1)
    gw_in, gw_rest, r_small = _join_halves(
        [_sum_slots(r_in, "sum_w_in", half=core), _sum_slots(r_rest, "sum_rest", tr=96, half=core)], small)
    gs = _sum_slots(r_small, "sum_small", tr=SMALL_ROWS)

    big = [("w_in", w_in, m_w_in, v_w_in, gw_in, 0)]
    row0 = 0
    for name, w, m, v in (("w_a_out", w_a_out, m_w_a_out, v_w_a_out), ("w_b_out", w_b_out, m_w_b_out, v_w_b_out),
                          ("w_o", w_o, m_w_o, v_w_o), ("w_ple_gate", w_ple_gate, m_w_ple_gate, v_w_ple_gate)):
        big.append((name, w, m, v, gw_rest, row0))
        row0 += w.shape[1]
    g_wp = gw_rest[row0:row0 + PACK_ROWS[4]].reshape(PLE, PLE)
    big.append(("w_ple", w_ple, m_w_ple, v_w_ple, g_wp, 0))
    upd = {name: _adamw(w, g, m, v, "adamw_" + name, g_row0=r0) for name, w, m, v, g, r0 in big}
    g_big = {name: (g if name in ("w_in", "w_ple") else g[r0:r0 + w.shape[1]])[None]
             for name, w, m, v, g, r0 in big}

    lane = lax.broadcasted_iota(jnp.int32, (1, D), 1)
    g_small = jnp.concatenate([gs[0:6], jnp.where(lane < 2 * NH, gs[9:10], 0.0), jnp.zeros((1, D), F32)], axis=0)

    def small_pack(gm, cb_, bg, gh, gp, gf):
        return jnp.concatenate([gm, cb_, gh.reshape(2, D), gp, gf.reshape(1, D),
                                jnp.pad(bg, ((0, 0), (0, D - 2 * NH))), jnp.zeros((1, D), F32)], axis=0)

    ws = small_pack(g_mix, conv_b, b_gates, g_head, g_ple, g_final)
    ms_ = small_pack(m_g_mix, m_conv_b, m_b_gates, m_g_head, m_g_ple, m_g_final)
    vs = small_pack(v_g_mix, v_conv_b, v_b_gates, v_g_head, v_g_ple, v_g_final)
    upd_s = _adamw(ws, g_small, ms_, vs, "adamw_small")
    g_cw_mine = _pad_rows(lax.dynamic_slice(gs[6:9], (0, chip * PLE), (3, PLE)))
    upd_c = _adamw(_pad_rows(conv_w[0]), g_cw_mine, _pad_rows(m_conv_w[0]), _pad_rows(v_conv_w[0]),
                   "adamw_conv_w")

    def leaves(bigs, sm, cwv):
        return [sm[0:1], bigs["w_in"], cwv[0:3][None], sm[1:2], bigs["w_a_out"], sm[6:7, 0:2 * NH],
                sm[2:4].reshape(1, VD), bigs["w_b_out"], bigs["w_o"], sm[4:5], bigs["w_ple_gate"],
                bigs["w_ple"], sm[5]]

    loss = gs[9, 2 * NH]
    outs = [loss, grad_x[None]] + leaves(g_big, g_small, g_cw_mine)
    for k in range(3):
        outs += leaves({name: u[k] for name, u in upd.items()}, upd_s[k], upd_c[k])
    return tuple(outs)
```

```python
import jax
import jax.numpy as jnp
from jax import lax
from jax.experimental import pallas as pl
from jax.experimental.pallas import tpu as pltpu

F32 = jnp.float32
MXU = jnp.bfloat16
WIRE = jnp.bfloat16

D = 1024
NH, DK, DV = 4, 256, 512
VD = NH * DV
PLE = 256
LCH = 256
EPS = 1e-6
N_IN = 14344
NMAIN = 14336
GATE_W = 128
N_CHIPS, N_DEV = 4, 8

C_BA, C_ZA, C_O, C_ZB, C_GA, C_GB = 0, 1024, 2048, 4096, 6144, 7168
C_QKV, C_XA, C_CA = 8192, 12288, 13312
HEAD_W = 2 * DK + DV
TAIL_W = 8192
CONV_W = 2048
SEGMENTS = (
    (0, 1024, C_XA), (1024, 1024, C_BA), (2048, 1024, C_CA), (3072, 1024, C_ZA),
    (8192, 2048, C_O), (10240, 2048, C_ZB), (12296, 1024, C_GA), (13320, 1024, C_GB),
) + tuple((4096 + DK * h, DK, C_QKV + HEAD_W * h) for h in range(NH)) + tuple(
    (5120 + DK * h, DK, C_QKV + HEAD_W * h + DK) for h in range(NH)) + tuple(
    (6144 + DV * h, DV, C_QKV + HEAD_W * h + 2 * DK) for h in range(NH))
GATE_COL = 12288

PACK_ROWS = (256, 512, 256, 256, 64)
PACK_TOTAL = sum(PACK_ROWS)
SMALL_ROWS = 16

ADAM_LR, ADAM_B1, ADAM_B2, ADAM_EPS, ADAM_WD, ADAM_STEP = 0.001, 0.9, 0.999, 1e-08, 0.01, 10

VMEM_LIMIT = 56 * 1024 * 1024
MESH = pl.DeviceIdType.MESH
ANY = pl.BlockSpec(memory_space=pl.ANY)


def _cparams(*sem):
    return pltpu.CompilerParams(dimension_semantics=sem, vmem_limit_bytes=VMEM_LIMIT)


def _dot(a, b):
    return jnp.dot(a, b, preferred_element_type=F32)


def _dot_nt(a, b):
    return lax.dot_general(a, b, (((1,), (1,)), ((), ())), preferred_element_type=F32)


def _dot_tn(a, b):
    return lax.dot_general(a, b, (((0,), (0,)), ((), ())), preferred_element_type=F32)


def _sigmoid(x):
    return 1.0 / (1.0 + jnp.exp(-x))


def _logsig(x):
    return jnp.minimum(x, 0.0) - jnp.log(1.0 + jnp.exp(-jnp.abs(x)))


def _rstd(x):
    return lax.rsqrt(jnp.mean(x * x, axis=-1, keepdims=True) + EPS)


def _norm_bwd(dy, xhat, r, g):
    dxh = dy * g
    return r * (dxh - xhat * jnp.mean(dxh * xhat, axis=-1, keepdims=True))


def _f32(ref):
    return ref[...].astype(F32)


def _prenorm(x, g_mix, wg):
    n = x.shape[0]
    tm = min(512, n)

    def body(x_ref, g_ref, wg_ref, hn_ref, gate_ref):
        xv = x_ref[...]
        hn = (xv * _rstd(xv) * g_ref[...]).astype(MXU)
        hn_ref[...] = hn
        gate_ref[...] = _dot(hn, wg_ref[...])

    return pl.pallas_call(
        body, name="prenorm", grid=(n // tm,),
        in_specs=[pl.BlockSpec((tm, D), lambda i: (i, 0)),
                  pl.BlockSpec((1, D), lambda i: (0, 0)),
                  pl.BlockSpec((D, GATE_W), lambda i: (0, 0))],
        out_specs=[pl.BlockSpec((tm, D), lambda i: (i, 0)),
                   pl.BlockSpec((tm, GATE_W), lambda i: (i, 0))],
        out_shape=[jax.ShapeDtypeStruct((n, D), MXU), jax.ShapeDtypeStruct((n, GATE_W), F32)],
        compiler_params=_cparams("arbitrary"),
    )(x, g_mix, wg)


def _matmul(a, b, out_dtype, name, tm=2048, tn=1024):
    m, k = a.shape
    n = b.shape[1]
    tm, tn = min(tm, m), min(tn, n)

    def body(a_ref, b_ref, o_ref):
        o_ref[...] = _dot(a_ref[...], b_ref[...]).astype(out_dtype)

    return pl.pallas_call(
        body, name=name, grid=(n // tn, m // tm),
        in_specs=[pl.BlockSpec((tm, k), lambda j, i: (i, 0)),
                  pl.BlockSpec((k, tn), lambda j, i: (0, j))],
        out_specs=pl.BlockSpec((tm, tn), lambda j, i: (i, j)),
        out_shape=jax.ShapeDtypeStruct((m, n), out_dtype),
        compiler_params=_cparams("arbitrary", "arbitrary"),
    )(a, b)


def _matmul_tn(a, b, name, out_dtype=F32, ta=1024, tb=1024, tk=1024):
    n, ka = a.shape
    kb = b.shape[1]
    ta, tb, tk = min(ta, ka), min(tb, kb), min(tk, n)
    nk = n // tk

    def body(a_ref, b_ref, o_ref, acc):
        kk = pl.program_id(2)

        @pl.when(kk == 0)
        def _():
            acc[...] = jnp.zeros_like(acc)

        acc[...] += _dot_tn(a_ref[...].astype(MXU), b_ref[...].astype(MXU))

        @pl.when(kk == nk - 1)
        def _():
            o_ref[...] = acc[...].astype(out_dtype)

    return pl.pallas_call(
        body, name=name, grid=(ka // ta, kb // tb, nk),
        in_specs=[pl.BlockSpec((tk, ta), lambda i, j, kk: (kk, i)),
                  pl.BlockSpec((tk, tb), lambda i, j, kk: (kk, j))],
        out_specs=pl.BlockSpec((ta, tb), lambda i, j, kk: (i, j)),
        out_shape=jax.ShapeDtypeStruct((ka, kb), out_dtype),
        scratch_shapes=[pltpu.VMEM((ta, tb), F32)],
        compiler_params=_cparams("arbitrary", "arbitrary", "arbitrary"),
    )(a, b)


def _gate_vectors(g, hd):
    gt = g.T[0:8, :]
    lane = lax.broadcasted_iota(jnp.int32, g.shape, 1)
    sub = lax.broadcasted_iota(jnp.int32, gt.shape, 0)
    col = lambda j: jnp.sum(jnp.where(lane == j, g, 0.0), axis=1, keepdims=True)
    row = lambda j: jnp.sum(jnp.where(sub == j, gt, 0.0), axis=0, keepdims=True)
    return col(hd), col(hd + NH), row(hd), row(hd + NH)


def _chunk_decay(li_col, li_row, lf_col, lf_row, m_prev):
    n = li_col.shape[0]
    r = lax.broadcasted_iota(jnp.int32, (n, n), 0)
    c = lax.broadcasted_iota(jnp.int32, (n, n), 1)
    tri = r >= c
    b_col = jnp.sum(jnp.where(tri, lf_row, 0.0), axis=1, keepdims=True)
    b_row = jnp.sum(jnp.where(r <= c, lf_col, 0.0), axis=0, keepdims=True)
    b_last = jnp.sum(lf_row, axis=1, keepdims=True)
    dmat = jnp.where(tri, b_col - b_row + li_row, -jnp.inf)
    a_col = b_col + m_prev
    g_col = b_last - b_col + li_col
    m_new = jnp.maximum(b_last + m_prev, jnp.max(g_col, axis=0, keepdims=True))
    w_col = jnp.exp(g_col - m_new)
    decay = jnp.exp(b_last + m_prev - m_new)
    return tri, dmat, a_col, m_new, w_col, decay


def _qkv_specs(row_of):
    base = C_QKV // DK
    q_spec = pl.BlockSpec((LCH, DK), lambda c, h: (row_of(c), base + (HEAD_W // DK) * h))
    k_spec = pl.BlockSpec((LCH, DK), lambda c, h: (row_of(c), base + (HEAD_W // DK) * h + 1))
    v_spec = pl.BlockSpec((LCH, DV), lambda c, h: (row_of(c), C_QKV // DV + (HEAD_W // DV) * h + 1))
    return q_spec, k_spec, v_spec


def _lane_put(col, lane_id, width=GATE_W):
    lane = lax.broadcasted_iota(jnp.int32, (col.shape[0], width), 1)
    return jnp.where(lane == lane_id, col, 0.0)


def _lane_get(block, lane_id):
    lane = lax.broadcasted_iota(jnp.int32, block.shape, 1)
    return jnp.sum(jnp.where(lane == lane_id, block, 0.0), axis=1, keepdims=True)


def _mlstm_fwd(proj, gates, bias):
    n = proj.shape[0]
    nc = n // LCH

    def body(q_ref, k_ref, v_ref, g_ref, bias_ref,
             h_ref, st_ref, cs_ref, ns_ref, ms_ref, c_scr, n_scr, m_scr):
        c = pl.program_id(0)
        hd = pl.program_id(1)

        @pl.when(c == 0)
        def _():
            c_scr[hd] = jnp.zeros((DK, DV), F32)
            n_scr[hd] = jnp.zeros((1, DK), F32)
            m_scr[hd] = jnp.full((1, GATE_W), -jnp.inf, F32)

        @pl.when(hd == 0)
        def _():
            st_ref[...] = jnp.zeros_like(st_ref)

        g = g_ref[...] + bias_ref[...]
        li_col, fr_col, li_row, fr_row = _gate_vectors(g, hd)
        m_all = m_scr[hd]
        m_prev = m_all[0:1, 0:1]
        tri, dmat, a_col, m_new, w_col, decay = _chunk_decay(
            li_col, li_row, _logsig(fr_col), _logsig(fr_row), m_prev)
        m_col = jnp.maximum(a_col, jnp.max(dmat, axis=1, keepdims=True))
        dl = jnp.exp(dmat - m_col)
        inter = jnp.exp(a_col - m_col)

        qs = q_ref[...] * (DK ** -0.5)
        kk = k_ref[...]
        vv = v_ref[...]
        cst = c_scr[hd]
        nst = n_scr[hd]
        cs_ref[...] = cst
        ns_ref[...] = nst
        ms_ref[...] = m_all

        sc = _dot_nt(qs, kk) * dl
        num = _dot(sc.astype(MXU), vv) + inter * _dot(qs, cst.astype(MXU))
        den = (jnp.sum(sc, axis=1, keepdims=True)
               + inter * jnp.sum(qs.astype(F32) * nst, axis=1, keepdims=True))
        nrm = jnp.maximum(jnp.abs(den), jnp.exp(-m_col))
        h_ref[...] = num / nrm
        st_ref[...] += _lane_put(den, hd) + _lane_put(m_col, NH + hd)

        kw = kk.astype(F32) * w_col
        c_scr[hd] = decay * cst + _dot_tn(kw.astype(MXU), vv)
        n_scr[hd] = decay * nst + jnp.sum(kw, axis=0, keepdims=True)
        m_scr[hd] = jnp.broadcast_to(m_new, (1, GATE_W))

    q_spec, k_spec, v_spec = _qkv_specs(lambda c: c)
    return pl.pallas_call(
        body, name="mlstm_fwd", grid=(nc, NH),
        in_specs=[q_spec, k_spec, v_spec,
                  pl.BlockSpec((LCH, GATE_W), lambda c, h: (c, 0)),
                  pl.BlockSpec((1, GATE_W), lambda c, h: (0, 0))],
        out_specs=[pl.BlockSpec((LCH, DV), lambda c, h: (c, h)),
                   pl.BlockSpec((LCH, GATE_W), lambda c, h: (c, 0)),
                   pl.BlockSpec((None, None, DK, DV), lambda c, h: (h, c, 0, 0)),
                   pl.BlockSpec((None, None, 1, DK), lambda c, h: (h, c, 0, 0)),
                   pl.BlockSpec((None, None, 1, GATE_W), lambda c, h: (h, c, 0, 0))],
        out_shape=[jax.ShapeDtypeStruct((n, VD), F32),
                   jax.ShapeDtypeStruct((n, GATE_W), F32),
                   jax.ShapeDtypeStruct((NH, nc, DK, DV), F32),
                   jax.ShapeDtypeStruct((NH, nc, 1, DK), F32),
                   jax.ShapeDtypeStruct((NH, nc, 1, GATE_W), F32)],
        scratch_shapes=[pltpu.VMEM((NH, DK, DV), F32), pltpu.VMEM((NH, 1, DK), F32),
                        pltpu.VMEM((NH, 1, GATE_W), F32)],
        compiler_params=_cparams("arbitrary", "arbitrary"),
    )(proj, proj, proj, gates, bias)


def _mlstm_bwd(proj, gates, bias, h, dh, stats, cs, ns, ms, dproj):
    n = proj.shape[0]
    nc = n // LCH

    def body(q_ref, k_ref, v_ref, g_ref, bias_ref, h_ref, dh_ref, st_ref,
             cs_ref, ns_ref, ms_ref, dproj_in,
             dqkv_ref, dg_ref, gsum_ref, dc_scr, dn_scr):
        del dproj_in
        step = pl.program_id(0)
        hd = pl.program_id(1)

        @pl.when(step == 0)
        def _():
            dc_scr[hd] = jnp.zeros((DK, DV), F32)
            dn_scr[hd] = jnp.zeros((1, DK), F32)

        @pl.when((step == 0) & (hd == 0))
        def _():
            gsum_ref[...] = jnp.zeros_like(gsum_ref)

        @pl.when(hd == 0)
        def _():
            dg_ref[...] = jnp.zeros_like(dg_ref)

        g = g_ref[...] + bias_ref[...]
        li_col, fr_col, li_row, fr_row = _gate_vectors(g, hd)
        m_prev = ms_ref[0:1, 0:1]
        tri, dmat, a_col, m_new, w_col, decay = _chunk_decay(
            li_col, li_row, _logsig(fr_col), _logsig(fr_row), m_prev)
        stats = st_ref[...]
        m_col = _lane_get(stats, NH + hd)
        dl = jnp.exp(dmat - m_col)
        inter = jnp.exp(a_col - m_col)

        qs = q_ref[...] * (DK ** -0.5)
        kk = k_ref[...]
        vv = v_ref[...]
        qf = qs.astype(F32)
        kf = kk.astype(F32)
        cst = cs_ref[...]
        nst = ns_ref[...]
        cb = cst.astype(MXU)
        dcn = dc_scr[hd]
        dnn = dn_scr[hd]
        dcb = dcn.astype(MXU)

        den = _lane_get(stats, hd)
        floor = jnp.exp(-m_col)
        nrm = jnp.maximum(jnp.abs(den), floor)
        dhv = dh_ref[...]
        dnum = dhv / nrm
        dnum_b = dnum.astype(MXU)
        dhh = jnp.sum(dhv * h_ref[...], axis=1, keepdims=True)
        dden = jnp.where(jnp.abs(den) > floor, -dhh / nrm * jnp.sign(den), 0.0)

        sc = _dot_nt(qs, kk) * dl
        dsc = _dot_nt(dnum_b, vv) + dden
        da = (dl * dsc).astype(MXU)
        gmat = sc * dsc

        dq = _dot(da, kk) + inter * (_dot_nt(dnum_b, cb) + dden * nst)
        dk_state = w_col * (_dot_nt(vv, dcb) + dnn)
        dk = _dot_tn(da, qs) + dk_state
        kw = (kf * w_col).astype(MXU)
        dv = _dot_tn(sc.astype(MXU), dnum_b) + _dot(kw, dcb)
        dqkv_ref[:, 0:DK] = (dq * (DK ** -0.5)).astype(dqkv_ref.dtype)
        dqkv_ref[:, DK:2 * DK] = dk.astype(dqkv_ref.dtype)
        dqkv_ref[:, 2 * DK:HEAD_W] = dv.astype(dqkv_ref.dtype)

        num_i = _dot(qs, cb)
        den_i = jnp.sum(qf * nst, axis=1, keepdims=True)
        e_col = inter * (jnp.sum(dnum * num_i, axis=1, keepdims=True) + dden * den_i)
        h_col = jnp.sum(kf * dk_state, axis=1, keepdims=True)
        f_dec = decay * (jnp.sum(jnp.sum(cst * dcn, axis=1, keepdims=True), axis=0, keepdims=True)
                         + jnp.sum(nst * dnn, axis=1, keepdims=True))
        r = lax.broadcasted_iota(jnp.int32, (LCH, LCH), 0)
        c = lax.broadcasted_iota(jnp.int32, (LCH, LCH), 1)
        eye = r == c
        to_col = lambda row: jnp.sum(jnp.where(eye, row, 0.0), axis=1, keepdims=True)
        row_g = jnp.sum(gmat, axis=1, keepdims=True)
        col_g = to_col(jnp.sum(gmat, axis=0, keepdims=True))
        last = lax.broadcasted_iota(jnp.int32, (LCH, 1), 0) == LCH - 1
        db_col = row_g - col_g + e_col - h_col
        db_col = db_col + jnp.where(last, jnp.sum(h_col, axis=0, keepdims=True) + f_dec, 0.0)
        dli_col = col_g + h_col
        dlf_row = jnp.sum(jnp.where(tri, db_col, 0.0), axis=0, keepdims=True)
        df_col = to_col(dlf_row) * _sigmoid(-fr_col)
        dg = _lane_put(dli_col, hd) + _lane_put(df_col, NH + hd)
        dg_ref[...] += dg
        gsum_ref[0:1, 0:GATE_W] += jnp.sum(dg, axis=0, keepdims=True)

        dc_scr[hd] = decay * dcn + _dot_tn((qf * inter).astype(MXU), dnum_b)
        dn_scr[hd] = decay * dnn + jnp.sum(qf * (inter * dden), axis=0, keepdims=True)

    rev = lambda c: nc - 1 - c
    q_spec, k_spec, v_spec = _qkv_specs(rev)
    hv_spec = pl.BlockSpec((LCH, DV), lambda c, h: (rev(c), h))
    gate_spec = pl.BlockSpec((LCH, GATE_W), lambda c, h: (rev(c), 0))
    return pl.pallas_call(
        body, name="mlstm_bwd", grid=(nc, NH),
        in_specs=[q_spec, k_spec, v_spec, gate_spec,
                  pl.BlockSpec((1, GATE_W), lambda c, h: (0, 0)),
                  hv_spec, hv_spec, gate_spec,
                  pl.BlockSpec((None, None, DK, DV), lambda c, h: (h, rev(c), 0, 0)),
                  pl.BlockSpec((None, None, 1, DK), lambda c, h: (h, rev(c), 0, 0)),
                  pl.BlockSpec((None, None, 1, GATE_W), lambda c, h: (h, rev(c), 0, 0)),
                  ANY],
        out_specs=[pl.BlockSpec((LCH, HEAD_W), lambda c, h: (rev(c), C_QKV // HEAD_W + h)),
                   gate_spec,
                   pl.BlockSpec((8, D), lambda c, h: (0, 0))],
        out_shape=[jax.ShapeDtypeStruct(dproj.shape, dproj.dtype),
                   jax.ShapeDtypeStruct((n, GATE_W), F32),
                   jax.ShapeDtypeStruct((8, D), F32)],
        scratch_shapes=[pltpu.VMEM((NH, DK, DV), F32), pltpu.VMEM((NH, 1, DK), F32)],
        input_output_aliases={11: 0},
        compiler_params=_cparams("arbitrary", "arbitrary"),
    )(proj, proj, proj, gates, bias, h, dh, stats, cs, ns, ms, dproj)


def _proj_spec(tm, col, width):
    return pl.BlockSpec((tm, width), lambda i: (i, col // width))


def _halo_prev(tm, col):
    return pl.BlockSpec((8, 1024), lambda i: (jnp.maximum(i * (tm // 8) - 1, 0), col // 1024))


def _const(shape):
    return pl.BlockSpec(shape, lambda i: (0,) * len(shape))


def _conv_inputs(i, tm, xa_ref, ca_ref, xah_ref, cah_ref):
    u = _f32(xa_ref) * _f32(ca_ref)
    uh = jnp.where(i > 0, _f32(xah_ref) * _f32(cah_ref), 0.0)
    rid = lax.broadcasted_iota(jnp.int32, u.shape, 0)
    u1 = jnp.where(rid == 0, uh[7:8, :], pltpu.roll(u, 1, 0))
    u2 = jnp.where(rid == 0, uh[6:7, :], jnp.where(rid == 1, uh[7:8, :], pltpu.roll(u, 2, 0)))
    return u, u1, u2


def _head_norm(hh, gh):
    out = []
    for j in range(NH):
        hj = hh[:, j * DV:(j + 1) * DV]
        rj = _rstd(hj)
        out.append((hj * rj, rj, gh[:, j * DV:(j + 1) * DV]))
    return out


def _tail_fwd(proj, h, x, p, t, cw, cb, gh, gple, gfin, wa, wb, wo, wpg, wp):
    n = x.shape[0]
    tm = min(256, n)

    def body(ba_r, za_r, o_r, zb_r, ga_r, gb_r, xa_r, ca_r, xah_r, cah_r, h_r, x_r, p_r, t_r,
             cw_r, cb_r, gh_r, gple_r, gfin_r, wa_r, wb_r, wo_r, wpg_r, wp_r,
             apre_o, bpre_o, mg_o, xn1_o, de_o, dgp_o, ya_o, yb_o, x1_o, dx2_o, acc_o):
        i = pl.program_id(0)

        @pl.when(i == 0)
        def _():
            acc_o[...] = jnp.zeros_like(acc_o)

        u, u1, u2 = _conv_inputs(i, tm, xa_r, ca_r, xah_r, cah_r)
        cwv = cw_r[...]
        cv = cwv[0:1, :] * u2 + cwv[1:2, :] * u1 + cwv[2:3, :] * u + cb_r[...]
        za = _f32(za_r)
        a_pre = (_f32(ba_r) * cv * (za * _sigmoid(za))).astype(MXU)
        apre_o[...] = a_pre
        ya = _dot(a_pre, wa_r[...])

        hn = _head_norm(h_r[...], gh_r[...])
        hbn = jnp.concatenate([xh * g for xh, _, g in hn], axis=1)
        zb = _f32(zb_r)
        b_pre = (_sigmoid(_f32(o_r)) * hbn * (zb * _sigmoid(zb))).astype(MXU)
        bpre_o[...] = b_pre
        yb = _dot(b_pre, wb_r[...])
        ya_o[...] = ya.astype(MXU)
        yb_o[...] = yb.astype(MXU)

        mg = (_sigmoid(_f32(ga_r)) * ya + _sigmoid(_f32(gb_r)) * yb).astype(MXU)
        mg_o[...] = mg
        x1 = x_r[...] + _dot(mg, wo_r[...])
        x1_o[...] = x1
        xn1 = (x1 * _rstd(x1) * gple_r[...]).astype(MXU)
        xn1_o[...] = xn1
        gt = _sigmoid(_dot(xn1, wpg_r[...]))
        e = _dot(p_r[...].astype(MXU), wp_r[...])
        x2 = x1 + gt * e
        r2 = _rstd(x2)
        xh2 = x2 * r2
        gf = gfin_r[...]
        diff = xh2 * gf - t_r[...]
        dy = diff * (1.0 / D)
        dx2 = _norm_bwd(dy, xh2, r2, gf)
        dx2_o[...] = dx2
        de_o[...] = (dx2 * gt).astype(MXU)
        dgp_o[...] = (dx2 * e * gt * (1.0 - gt)).astype(MXU)
        acc_o[0:1, :] += jnp.sum(dy * xh2, axis=0, keepdims=True)
        loss = 0.5 * jnp.sum(jnp.sum(diff * diff, axis=1, keepdims=True), axis=0, keepdims=True) * (1.0 / D)
        acc_o[1:2, :] += jnp.broadcast_to(loss, (1, D))

    row = lambda w, dt: (pl.BlockSpec((tm, w), lambda i: (i, 0)), jax.ShapeDtypeStruct((n, w), dt))
    outs = [row(D, MXU), row(VD, MXU), row(D, MXU), row(D, MXU), row(D, MXU), row(D, MXU),
            row(D, MXU), row(D, MXU), row(D, F32), row(D, F32),
            (_const((8, D)), jax.ShapeDtypeStruct((8, D), F32))]
    return pl.pallas_call(
        body, name="tail_fwd", grid=(n // tm,),
        in_specs=[_proj_spec(tm, C_BA, 1024), _proj_spec(tm, C_ZA, 1024),
                  _proj_spec(tm, C_O, 2048), _proj_spec(tm, C_ZB, 2048),
                  _proj_spec(tm, C_GA, 1024), _proj_spec(tm, C_GB, 1024),
                  _proj_spec(tm, C_XA, 1024), _proj_spec(tm, C_CA, 1024),
                  _halo_prev(tm, C_XA), _halo_prev(tm, C_CA),
                  pl.BlockSpec((tm, VD), lambda i: (i, 0)),
                  pl.BlockSpec((tm, D), lambda i: (i, 0)),
                  pl.BlockSpec((tm, PLE), lambda i: (i, 0)),
                  pl.BlockSpec((tm, D), lambda i: (i, 0)),
                  _const((8, D)), _const((1, D)), _const((1, VD)), _const((1, D)), _const((1, D)),
                  _const((D, D)), _const((VD, D)), _const((D, D)), _const((D, D)), _const((PLE, D))],
        out_specs=[s for s, _ in outs],
        out_shape=[s for _, s in outs],
        compiler_params=_cparams("arbitrary"),
    )(*([proj] * 10), h, x, p, t, cw, cb, gh, gple, gfin, wa, wb, wo, wpg, wp)


def _tail_bwd(proj, h, dgp, dx2, x1, ya, yb, cw, cb, gh, gple, wpg, wo, wb, wa):
    n = x1.shape[0]
    tm = min(256, n)

    def body(ba_r, za_r, o_r, zb_r, ga_r, gb_r, xa_r, ca_r, xah_r, cah_r, h_r,
             dgp_r, dx2_r, x1_r, ya_r, yb_r, cw_r, cb_r, gh_r, gple_r,
             wpg_r, wo_r, wb_r, wa_r,
             dproj_o, dcv_o, dh_o, dx1_o, dx1b_o, dya_o, dyb_o, acc_o):
        i = pl.program_id(0)

        @pl.when(i == 0)
        def _():
            acc_o[...] = jnp.zeros_like(acc_o)

        dxn1 = _dot_nt(dgp_r[...], wpg_r[...])
        x1 = x1_r[...]
        r1 = _rstd(x1)
        xh1 = x1 * r1
        acc_o[0:1, 0:D] += jnp.sum(dxn1 * xh1, axis=0, keepdims=True)
        dx1 = dx2_r[...] + _norm_bwd(dxn1, xh1, r1, gple_r[...])
        dx1_o[...] = dx1
        dx1b = dx1.astype(MXU)
        dx1b_o[...] = dx1b

        dmg = _dot_nt(dx1b, wo_r[...])
        sga = _sigmoid(_f32(ga_r))
        sgb = _sigmoid(_f32(gb_r))
        dya = (dmg * sga).astype(MXU)
        dyb = (dmg * sgb).astype(MXU)
        dya_o[...] = dya
        dyb_o[...] = dyb
        dproj_o[:, C_GA:C_GA + D] = (dmg * _f32(ya_r) * sga * (1.0 - sga)).astype(MXU)
        dproj_o[:, C_GB:C_GB + D] = (dmg * _f32(yb_r) * sgb * (1.0 - sgb)).astype(MXU)

        db_pre = _dot_nt(dyb, wb_r[...])
        hn = _head_norm(h_r[...], gh_r[...])
        hbn = jnp.concatenate([xh * g for xh, _, g in hn], axis=1)
        so = _sigmoid(_f32(o_r))
        zb = _f32(zb_r)
        szb = _sigmoid(zb)
        sb = zb * szb
        dproj_o[:, C_O:C_O + VD] = (db_pre * hbn * sb * so * (1.0 - so)).astype(MXU)
        dproj_o[:, C_ZB:C_ZB + VD] = (db_pre * so * hbn * szb * (1.0 + zb * (1.0 - szb))).astype(MXU)
        dhbn = db_pre * so * sb
        for j, (xh, rj, g) in enumerate(hn):
            dj = dhbn[:, j * DV:(j + 1) * DV]
            acc_o[1:2, j * DV:(j + 1) * DV] += jnp.sum(dj * xh, axis=0, keepdims=True)
            dh_o[:, j * DV:(j + 1) * DV] = _norm_bwd(dj, xh, rj, g)

        da_pre = _dot_nt(dya, wa_r[...])
        u, u1, u2 = _conv_inputs(i, tm, xa_r, ca_r, xah_r, cah_r)
        cwv = cw_r[...]
        cv = cwv[0:1, :] * u2 + cwv[1:2, :] * u1 + cwv[2:3, :] * u + cb_r[...]
        za = _f32(za_r)
        sza = _sigmoid(za)
        sa = za * sza
        ba = _f32(ba_r)
        dproj_o[:, C_BA:C_BA + D] = (da_pre * cv * sa).astype(MXU)
        dproj_o[:, C_ZA:C_ZA + D] = (da_pre * ba * cv * sza * (1.0 + za * (1.0 - sza))).astype(MXU)
        dcv = da_pre * ba * sa
        dcv_o[...] = dcv.astype(MXU)
        acc_o[2:3, 0:D] += jnp.sum(dcv, axis=0, keepdims=True)
        acc_o[3:4, 0:D] += jnp.sum(dcv * u2, axis=0, keepdims=True)
        acc_o[4:5, 0:D] += jnp.sum(dcv * u1, axis=0, keepdims=True)
        acc_o[5:6, 0:D] += jnp.sum(dcv * u, axis=0, keepdims=True)

    row = lambda w, dt: (pl.BlockSpec((tm, w), lambda i: (i, 0)), jax.ShapeDtypeStruct((n, w), dt))
    outs = [(pl.BlockSpec((tm, TAIL_W), lambda i: (i, 0)), jax.ShapeDtypeStruct((n, NMAIN), MXU)),
            row(D, MXU), row(VD, F32), row(D, F32), row(D, MXU), row(D, MXU), row(D, MXU),
            (_const((8, VD)), jax.ShapeDtypeStruct((8, VD), F32))]
    rowin = lambda w: pl.BlockSpec((tm, w), lambda i: (i, 0))
    return pl.pallas_call(
        body, name="tail_bwd", grid=(n // tm,),
        in_specs=[_proj_spec(tm, C_BA, 1024), _proj_spec(tm, C_ZA, 1024),
                  _proj_spec(tm, C_O, 2048), _proj_spec(tm, C_ZB, 2048),
                  _proj_spec(tm, C_GA, 1024), _proj_spec(tm, C_GB, 1024),
                  _proj_spec(tm, C_XA, 1024), _proj_spec(tm, C_CA, 1024),
                  _halo_prev(tm, C_XA), _halo_prev(tm, C_CA),
                  rowin(VD), rowin(D), rowin(D), rowin(D), rowin(D), rowin(D),
                  _const((8, D)), _const((1, D)), _const((1, VD)), _const((1, D)),
                  _const((D, D)), _const((D, D)), _const((VD, D)), _const((D, D))],
        out_specs=[s for s, _ in outs],
        out_shape=[s for _, s in outs],
        compiler_params=_cparams("arbitrary"),
    )(*([proj] * 10), h, dgp, dx2, x1, ya, yb, cw, cb, gh, gple, wpg, wo, wb, wa)


def _conv_bwd(proj, dcv, cw, dproj):
    n = dcv.shape[0]
    tm = min(512, n)
    nt = n // tm

    def body(xa_r, ca_r, dcv_r, nxt_r, cw_r, dproj_in, dxc_o):
        del dproj_in
        i = pl.program_id(0)
        dcv_v = _f32(dcv_r)
        nxt = jnp.where(i < nt - 1, _f32(nxt_r), 0.0)
        rid = lax.broadcasted_iota(jnp.int32, dcv_v.shape, 0)
        d1 = jnp.where(rid == tm - 1, nxt[0:1, :], pltpu.roll(dcv_v, tm - 1, 0))
        d2 = jnp.where(rid == tm - 2, nxt[0:1, :],
                       jnp.where(rid == tm - 1, nxt[1:2, :], pltpu.roll(dcv_v, tm - 2, 0)))
        cwv = cw_r[...]
        du = cwv[2:3, :] * dcv_v + cwv[1:2, :] * d1 + cwv[0:1, :] * d2
        dxc_o[:, 0:D] = (du * _f32(ca_r)).astype(MXU)
        dxc_o[:, D:2 * D] = (du * _f32(xa_r)).astype(MXU)

    return pl.pallas_call(
        body, name="conv_bwd", grid=(nt,),
        in_specs=[_proj_spec(tm, C_XA, 1024), _proj_spec(tm, C_CA, 1024),
                  pl.BlockSpec((tm, D), lambda i: (i, 0)),
                  pl.BlockSpec((8, D), lambda i: (jnp.minimum((i + 1) * (tm // 8), n // 8 - 1), 0)),
                  _const((8, D)), ANY],
        out_specs=pl.BlockSpec((tm, CONV_W), lambda i: (i, C_XA // CONV_W)),
        out_shape=jax.ShapeDtypeStruct(dproj.shape, dproj.dtype),
        input_output_aliases={5: 0},
        compiler_params=_cparams("arbitrary"),
    )(proj, proj, dcv, dcv, cw, dproj)


def _position():
    return lax.axis_index("x"), lax.axis_index("y"), lax.axis_index("c")


def _flip(v, bit):
    return 1 - v if bit else v


def _scatter_copies(srcs, dsts, send_sems, recv_sems, local_sems):
    x, y, c = _position()
    me = 4 * x + 2 * y + c
    na = len(srcs)
    copies = []
    for r in range(N_DEV):
        px, py, pc = _flip(x, (r >> 2) & 1), _flip(y, (r >> 1) & 1), _flip(c, r & 1)
        for a in range(na):
            rows = dsts[a].shape[1]
            src = srcs[a].at[2 * px + py, pl.ds(pl.multiple_of(pc * rows, rows), rows)]
            dst = dsts[a].at[me]
            if r == 0:
                copies.append(pltpu.make_async_copy(src, dst, local_sems.at[a]))
            else:
                k = (r - 1) * na + a
                copies.append(pltpu.make_async_remote_copy(
                    src_ref=src, dst_ref=dst, send_sem=send_sems.at[k], recv_sem=recv_sems.at[k],
                    device_id=(px, py, pc), device_id_type=MESH))
    return copies


def _input_grad(dproj, dgates, w_main, w_gate, x, dx1, g_mix, g_in, g_rest):
    n = x.shape[0]
    tm, tk = min(1024, n), 2048
    nk = NMAIN // tk
    nt = n // tm

    def body(dp_r, dg_r, w_r, wg_r, x_r, dx1_r, g_r, gin, grest,
             gx_o, acc_o, oin, orest, acc, send_sems, recv_sems, local_sems):
        i = pl.program_id(0)
        kk = pl.program_id(1)
        copies = _scatter_copies((gin, grest), (oin, orest), send_sems, recv_sems, local_sems)

        @pl.when((i == 0) & (kk == 0))
        def _():
            acc_o[...] = jnp.zeros_like(acc_o)
            for cp in copies:
                cp.start()

        @pl.when(kk == 0)
        def _():
            acc[...] = _dot_nt(dg_r[...].astype(MXU), wg_r[...])

        acc[...] += _dot_nt(dp_r[...], w_r[...])

        @pl.when(kk == nk - 1)
        def _():
            dhn = acc[...]
            xv = x_r[...]
            r0 = _rstd(xv)
            xh = xv * r0
            acc_o[0:1, :] += jnp.sum(dhn * xh, axis=0, keepdims=True)
            gx_o[...] = dx1_r[...] + _norm_bwd(dhn, xh, r0, g_r[...])

        @pl.when((i == nt - 1) & (kk == nk - 1))
        def _():
            for cp in copies:
                cp.wait()

    nrem = 2 * (N_DEV - 1)
    r_in, r_rest = g_in.shape[1] // 2, g_rest.shape[1] // 2
    return pl.pallas_call(
        body, name="input_grad", grid=(nt, nk),
        in_specs=[pl.BlockSpec((tm, tk), lambda i, kk: (i, kk)),
                  pl.BlockSpec((tm, GATE_W), lambda i, kk: (i, 0)),
                  pl.BlockSpec((D, tk), lambda i, kk: (0, kk)),
                  pl.BlockSpec((D, GATE_W), lambda i, kk: (0, 0)),
                  pl.BlockSpec((tm, D), lambda i, kk: (i, 0)),
                  pl.BlockSpec((tm, D), lambda i, kk: (i, 0)),
                  pl.BlockSpec((1, D), lambda i, kk: (0, 0)),
                  ANY, ANY],
        out_specs=[pl.BlockSpec((tm, D), lambda i, kk: (i, 0)),
                   pl.BlockSpec((8, D), lambda i, kk: (0, 0)),
                   ANY, ANY],
        out_shape=[jax.ShapeDtypeStruct((n, D), F32), jax.ShapeDtypeStruct((8, D), F32),
                   jax.ShapeDtypeStruct((N_DEV, r_in) + g_in.shape[2:], g_in.dtype),
                   jax.ShapeDtypeStruct((N_DEV, r_rest) + g_rest.shape[2:], g_rest.dtype)],
        scratch_shapes=[pltpu.VMEM((tm, D), F32),
                        pltpu.SemaphoreType.DMA((nrem,)), pltpu.SemaphoreType.DMA((nrem,)),
                        pltpu.SemaphoreType.DMA((2,))],
        compiler_params=_cparams("arbitrary", "arbitrary"),
    )(dproj, dgates, w_main, w_gate, x, dx1, g_mix, g_in, g_rest)


def _pack_small(acc_f, acc_b, acc_x, gsum):
    def body(f_r, b_r, x_r, s_r, o_r):
        o_r[...] = jnp.zeros_like(o_r)
        o_r[0:1, :] = x_r[0:1, :]
        o_r[1:2, :] = b_r[2:3, 0:D]
        o_r[2:3, :] = b_r[1:2, 0:D]
        o_r[3:4, :] = b_r[1:2, D:2 * D]
        o_r[4:5, :] = b_r[0:1, 0:D]
        o_r[5:6, :] = f_r[0:1, :]
        o_r[6:9, :] = b_r[3:6, 0:D]
        lane = lax.broadcasted_iota(jnp.int32, (1, D), 1)
        o_r[9:10, :] = jnp.where(lane < 2 * NH, s_r[0:1, :], jnp.where(lane == 2 * NH, f_r[1:2, :], 0.0))

    return pl.pallas_call(
        body, name="pack_small",
        out_shape=jax.ShapeDtypeStruct((SMALL_ROWS, D), F32),
    )(acc_f, acc_b, acc_x, gsum)


def _sum_slots(r, name, tr=64, half=None):
    s, rows, w = r.shape
    tr = min(tr, rows)
    assert rows % tr == 0
    nt = rows // tr

    def body(*refs):
        r_ref, o_ref = refs[-2:]
        tot = r_ref[0].astype(F32)
        for k in range(1, s):
            tot = tot + r_ref[k].astype(F32)
        o_ref[...] = tot

    if half is None:
        return pl.pallas_call(
            body, name=name, grid=(nt,),
            in_specs=[pl.BlockSpec((s, tr, w), lambda i: (0, i, 0))],
            out_specs=pl.BlockSpec((tr, w), lambda i: (i, 0)),
            out_shape=jax.ShapeDtypeStruct((rows, w), F32),
            compiler_params=_cparams("arbitrary"),
        )(r)
    return pl.pallas_call(
        body, name=name,
        grid_spec=pltpu.PrefetchScalarGridSpec(
            num_scalar_prefetch=1, grid=(nt,),
            in_specs=[pl.BlockSpec((s, tr, w), lambda i, hf: (0, i, 0))],
            out_specs=pl.BlockSpec((tr, w), lambda i, hf: (hf[0] * nt + i, 0))),
        out_shape=jax.ShapeDtypeStruct((2 * rows, w), F32),
        compiler_params=_cparams("arbitrary"),
    )(half, r)


def _adamw_body(w_r, g_r, m_r, v_r, d_o, m_o, v_o):
    c1 = 1.0 - ADAM_B1 ** ADAM_STEP
    c2 = 1.0 - ADAM_B2 ** ADAM_STEP
    gv = g_r[...]
    mn = ADAM_B1 * m_r[...] + (1.0 - ADAM_B1) * gv
    vn = ADAM_B2 * v_r[...] + (1.0 - ADAM_B2) * (gv * gv)
    m_o[...] = mn
    v_o[...] = vn
    d_o[...] = -ADAM_LR * ((mn / c1) / (jnp.sqrt(vn / c2) + ADAM_EPS) + ADAM_WD * w_r[...])


def _adamw_slabs(w, g, m, v, name, tr):
    rows = w.shape[0]
    assert rows % tr == 0 and w.shape == g.shape

    def body(*refs):
        _adamw_body(*refs)

    spec = pl.BlockSpec((tr,) + w.shape[1:], lambda i: (i, 0, 0))
    shp = jax.ShapeDtypeStruct(w.shape, F32)
    return pl.pallas_call(
        body, name=name, grid=(rows // tr,),
        in_specs=[spec] * 4, out_specs=[spec] * 3, out_shape=[shp] * 3,
        compiler_params=_cparams("arbitrary"),
    )(w, g, m, v)


def _adamw(w, g, m, v, name, g_row0=0, tr=64):
    lead = w.ndim == 3
    rows, cols = w.shape[-2:]
    tr = min(tr, rows)
    assert rows % tr == 0 and g_row0 % tr == 0 and g.shape[1] == cols

    def body(*refs):
        _adamw_body(*refs)

    if lead:
        spec = pl.BlockSpec((None, tr, cols), lambda i: (0, i, 0))
    else:
        spec = pl.BlockSpec((tr, cols), lambda i: (i, 0))
    g_spec = pl.BlockSpec((tr, cols), lambda i: (g_row0 // tr + i, 0))
    shp = jax.ShapeDtypeStruct(w.shape, F32)
    return pl.pallas_call(
        body, name=name, grid=(rows // tr,),
        in_specs=[spec, g_spec, spec, spec], out_specs=[spec] * 3, out_shape=[shp] * 3,
        compiler_params=_cparams("arbitrary"),
    )(w, g, m, v)


def _allgather_chips(halved, whole):
    arrs = list(halved) + list(whole)
    nh, na = len(halved), len(arrs)
    relations = ((1, 0), (0, 1), (1, 1))
    n_ici = 3 * na

    def body(*refs):
        ins, outs = refs[:na], refs[na:2 * na]
        send_sems, recv_sems = refs[2 * na:]
        x, y, c = _position()
        mine = 2 * x + y

        def half(a, chip, core):
            rows = ins[a].shape[0] // 2
            return outs[a].at[chip, pl.ds(pl.multiple_of(core * rows, rows), rows)]

        def remote(src, dst, k, to):
            return pltpu.make_async_remote_copy(src_ref=src, dst_ref=dst, send_sem=send_sems.at[k],
                                                recv_sem=recv_sems.at[k], device_id=to, device_id_type=MESH)

        first, passed = [], []
        for j, (rx, ry) in enumerate(relations):
            px, py = _flip(x, rx), _flip(y, ry)
            for a in range(na):
                if a < nh:
                    rows = ins[a].shape[0] // 2
                    src = ins[a].at[pl.ds(pl.multiple_of(c * rows, rows), rows)]
                    first.append(remote(src, half(a, mine, c), j * na + a, (px, py, c)))
                    landed = half(a, 2 * px + py, c)
                    passed.append(remote(landed, landed, n_ici + j * nh + a, (x, y, 1 - c)))
                else:
                    first.append(remote(ins[a], outs[a].at[mine], j * na + a, (px, py, c)))
        for cp in first:
            cp.start()
        k = 0
        for j in range(3):
            for a in range(na):
                first[j * na + a].wait_recv()
                if a < nh:
                    passed[k].start()
                    k += 1
        for cp in passed:
            cp.wait_recv()
        for cp in first + passed:
            cp.wait_send()

    nsem = n_ici + 3 * nh
    outs = pl.pallas_call(
        body, name="allgather_weights",
        in_specs=[ANY] * na, out_specs=[ANY] * na,
        out_shape=[jax.ShapeDtypeStruct((N_CHIPS,) + a.shape, a.dtype) for a in arrs],
        scratch_shapes=[pltpu.SemaphoreType.DMA((nsem,)), pltpu.SemaphoreType.DMA((nsem,))],
    )(*arrs)
    chip = 2 * lax.axis_index("x") + lax.axis_index("y")
    return [lax.dynamic_update_slice(o, a[None], (chip,) + (0,) * a.ndim) for o, a in zip(outs, arrs)]


def _join_halves(fulls, small):
    na = len(fulls)

    def body(*refs):
        ins, sm = refs[:na], refs[na]
        outs, osm = refs[na + 1:2 * na + 1], refs[2 * na + 1]
        send_sems, recv_sems, local_sem = refs[2 * na + 2:]
        del ins
        x, y, c = _position()
        me = 4 * x + 2 * y + c
        copies = []
        for a in range(na):
            rows = outs[a].shape[0] // 2
            mine = outs[a].at[pl.ds(pl.multiple_of(c * rows, rows), rows)]
            copies.append(pltpu.make_async_remote_copy(
                src_ref=mine, dst_ref=mine, send_sem=send_sems.at[a], recv_sem=recv_sems.at[a],
                device_id=(x, y, 1 - c), device_id_type=MESH))
        copies.append(pltpu.make_async_copy(sm, osm.at[me], local_sem))
        for r in range(1, N_DEV):
            to = (_flip(x, (r >> 2) & 1), _flip(y, (r >> 1) & 1), _flip(c, r & 1))
            copies.append(pltpu.make_async_remote_copy(
                src_ref=sm, dst_ref=osm.at[me], send_sem=send_sems.at[na + r - 1],
                recv_sem=recv_sems.at[na + r - 1], device_id=to, device_id_type=MESH))
        for cp in copies:
            cp.start()
        for cp in copies:
            cp.wait()

    nsem = na + N_DEV - 1
    return pl.pallas_call(
        body, name="join_halves",
        in_specs=[ANY] * (na + 1), out_specs=[ANY] * (na + 1),
        out_shape=[jax.ShapeDtypeStruct(a.shape, a.dtype) for a in fulls]
        + [jax.ShapeDtypeStruct((N_DEV,) + small.shape, small.dtype)],
        scratch_shapes=[pltpu.SemaphoreType.DMA((nsem,)), pltpu.SemaphoreType.DMA((nsem,)),
                        pltpu.SemaphoreType.DMA],
        input_output_aliases={a: a for a in range(na)},
    )(*fulls, small)


def _to_internal(w_glob):
    order = sorted(SEGMENTS, key=lambda s: s[2])
    main = jnp.concatenate([w_glob[..., g0:g0 + w] for g0, w, _ in order], axis=-1)
    gate = w_glob[..., GATE_COL:GATE_COL + 2 * NH]
    pad = [(0, 0)] * (w_glob.ndim - 1) + [(0, GATE_W - 2 * NH)]
    return main, jnp.pad(gate, pad)


def _to_global(main, gate):
    parts = sorted([(g0, main[..., i0:i0 + w]) for g0, w, i0 in SEGMENTS]
                   + [(GATE_COL, gate[..., 0:2 * NH])], key=lambda s: s[0])
    return jnp.concatenate([p for _, p in parts], axis=-1)


def _pack_rows(wa, wb, wo, wpg, wp):
    return jnp.concatenate([wa, wb, wo, wpg, wp.reshape(PACK_ROWS[4], D)], axis=0)


def _pad_rows(a, rows=8):
    return jnp.pad(a, ((0, rows - a.shape[0]), (0, 0)))


def kernel(x, p, g_mix, w_in, conv_w, conv_b, w_a_out, b_gates, g_head, w_b_out, w_o, g_ple, w_ple_gate, w_ple, g_final, loss_target, m_g_mix, m_w_in, m_conv_w, m_conv_b, m_w_a_out, m_b_gates, m_g_head, m_w_b_out, m_w_o, m_g_ple, m_w_ple_gate, m_w_ple, m_g_final, v_g_mix, v_w_in, v_conv_w, v_conv_b, v_w_a_out, v_b_gates, v_g_head, v_w_b_out, v_w_o, v_g_ple, v_w_ple_gate, v_w_ple, v_g_final):
    chip = 2 * lax.axis_index("x") + lax.axis_index("y")
    xs, ps, ts = x[0], p[0, 0], loss_target[0]
    g_fin = g_final.reshape(1, D)

    pack_w = _pack_rows(w_a_out[0], w_b_out[0], w_o[0], w_ple_gate[0], w_ple[0])
    g_win, g_pack, g_cw = _allgather_chips(
        [w_in[0].astype(MXU), pack_w.astype(MXU)], [_pad_rows(conv_w[0])])
    w_glob = jnp.transpose(g_win, (1, 0, 2)).reshape(D, N_IN)
    w_main, w_gate = _to_internal(w_glob)
    offs = [0, 256, 768, 1024, 1280, 1344]
    wa, wb, wo, wpg = [g_pack[:, offs[k]:offs[k + 1]].reshape(-1, D) for k in range(4)]
    wp = jnp.transpose(g_pack[:, offs[4]:offs[5]].reshape(N_CHIPS, PLE, PLE), (1, 0, 2)).reshape(PLE, D)
    cw = jnp.transpose(g_cw, (1, 0, 2)).reshape(8, D)

    bias = jnp.pad(b_gates, ((0, 0), (0, GATE_W - 2 * NH)))
    hn, gates = _prenorm(xs, g_mix, w_gate)
    proj = _matmul(hn, w_main, MXU, "proj")
    h, stats, cs, ns, ms = _mlstm_fwd(proj, gates, bias)
    (a_pre, b_pre, mg, xn1, de, dgp, ya, yb, x1, dx2, acc_f) = _tail_fwd(
        proj, h, xs, ps, ts, cw, conv_b, g_head, g_ple, g_fin, wa, wb, wo, wpg, wp)
    dproj, dcv, dh, dx1, dx1b, dya, dyb, acc_b = _tail_bwd(
        proj, h, dgp, dx2, x1, ya, yb, cw, conv_b, g_head, g_ple, wpg, wo, wb, wa)
    dproj = _conv_bwd(proj, dcv, cw, dproj)
    dproj, dgates, gsum = _mlstm_bwd(proj, gates, bias, h, dh, stats, cs, ns, ms, dproj)
    d_main = _matmul_tn(hn, dproj, "dw_in", out_dtype=WIRE, tk=2048)
    d_gate = _matmul_tn(hn, dgates, "dw_gate", out_dtype=WIRE)
    d_wa = _matmul_tn(a_pre, dya, "dw_a_out", out_dtype=WIRE)
    d_wb = _matmul_tn(b_pre, dyb, "dw_b_out", out_dtype=WIRE)
    d_wo = _matmul_tn(mg, dx1b, "dw_o", out_dtype=WIRE)
    d_wpg = _matmul_tn(xn1, dgp, "dw_ple_gate", out_dtype=WIRE)
    d_wp = _matmul_tn(ps, de, "dw_ple", out_dtype=WIRE)

    g_in = jnp.transpose(_to_global(d_main, d_gate).reshape(D, N_CHIPS, N_IN // N_CHIPS), (1, 0, 2))
    d_wp_c = jnp.transpose(d_wp.reshape(PLE, N_CHIPS, PLE), (1, 0, 2)).reshape(N_CHIPS, PACK_ROWS[4], D)
    g_rest = jnp.concatenate(
        [d_wa.reshape(N_CHIPS, -1, D), d_wb.reshape(N_CHIPS, -1, D), d_wo.reshape(N_CHIPS, -1, D),
         d_wpg.reshape(N_CHIPS, -1, D), d_wp_c], axis=1)
    grad_x, acc_x, r_in, r_rest = _input_grad(dproj, dgates, w_main, w_gate, xs, dx1, g_mix, g_in, g_rest)
    small = _pack_small(acc_f, acc_b, acc_x, gsum)
    core = lax.axis_index("c").astype(jnp.int32).reshape(1)
    gw_in, gw_rest, r_small = _join_halves(
        [_sum_slots(r_in, "sum_w_in", half=core), _sum_slots(r_rest, "sum_rest", tr=96, half=core)], small)
    gs = _sum_slots(r_small, "sum_small", tr=SMALL_ROWS)

    big = []
    row0 = 0
    for name, w, m, v in (("w_a_out", w_a_out, m_w_a_out, v_w_a_out), ("w_b_out", w_b_out, m_w_b_out, v_w_b_out),
                          ("w_o", w_o, m_w_o, v_w_o), ("w_ple_gate", w_ple_gate, m_w_ple_gate, v_w_ple_gate)):
        big.append((name, w, m, v, gw_rest, row0))
        row0 += w.shape[1]
    g_wp = gw_rest[row0:row0 + PACK_ROWS[4]].reshape(PLE, PLE)
    big.append(("w_ple", w_ple, m_w_ple, v_w_ple, g_wp, 0))
    upd = {name: _adamw(w, g, m, v, "adamw_" + name, g_row0=r0) for name, w, m, v, g, r0 in big}
    g_big = {name: (g if name == "w_ple" else g[r0:r0 + w.shape[1]])[None] for name, w, m, v, g, r0 in big}
    ncol = w_in.shape[2]
    slabs = lambda a: jnp.transpose(a, (2, 0, 1)).reshape(ncol, 8, D // 8)
    unslab = lambda a: jnp.transpose(a, (1, 2, 0)).reshape(1, D, ncol)
    upd["w_in"] = [unslab(u) for u in _adamw_slabs(
        slabs(w_in), gw_in.T.reshape(ncol, 8, D // 8), slabs(m_w_in), slabs(v_w_in), "adamw_w_in", tr=326)]
    g_big["w_in"] = gw_in[None]

    lane = lax.broadcasted_iota(jnp.int32, (1, D), 1)
    g_small = jnp.concatenate([gs[0:6], jnp.where(lane < 2 * NH, gs[9:10], 0.0), jnp.zeros((1, D), F32)], axis=0)

    def small_pack(gm, cb_, bg, gh, gp, gf):
        return jnp.concatenate([gm, cb_, gh.reshape(2, D), gp, gf.reshape(1, D),
                                jnp.pad(bg, ((0, 0), (0, D - 2 * NH))), jnp.zeros((1, D), F32)], axis=0)

    ws = small_pack(g_mix, conv_b, b_gates, g_head, g_ple, g_final)
    ms_ = small_pack(m_g_mix, m_conv_b, m_b_gates, m_g_head, m_g_ple, m_g_final)
    vs = small_pack(v_g_mix, v_conv_b, v_b_gates, v_g_head, v_g_ple, v_g_final)
    upd_s = _adamw(ws, g_small, ms_, vs, "adamw_small")
    g_cw_mine = _pad_rows(lax.dynamic_slice(gs[6:9], (0, chip * PLE), (3, PLE)))
    upd_c = _adamw(_pad_rows(conv_w[0]), g_cw_mine, _pad_rows(m_conv_w[0]), _pad_rows(v_conv_w[0]),
                   "adamw_conv_w")

    def leaves(bigs, sm, cwv):
        return [sm[0:1], bigs["w_in"], cwv[0:3][None], sm[1:2], bigs["w_a_out"], sm[6:7, 0:2 * NH],
                sm[2:4].reshape(1, VD), bigs["w_b_out"], bigs["w_o"], sm[4:5], bigs["w_ple_gate"],
                bigs["w_ple"], sm[5]]

    loss = gs[9, 2 * NH]
    outs = [loss, grad_x[None]] + leaves(g_big, g_small, g_cw_mine)
    for k in range(3):
        outs += leaves({name: u[k] for name, u in upd.items()}, upd_s[k], upd_c[k])
    return tuple(outs)
```

```python
import jax
import jax.numpy as jnp
from jax import lax
from jax.experimental import pallas as pl
from jax.experimental.pallas import tpu as pltpu

F32 = jnp.float32
MXU = jnp.bfloat16
WIRE = jnp.bfloat16

D = 1024
NH, DK, DV = 4, 256, 512
VD = NH * DV
PLE = 256
LCH = 256
EPS = 1e-6
N_IN = 14344
NMAIN = 14336
GATE_W = 128
N_CHIPS, N_DEV = 4, 8

C_BA, C_ZA, C_O, C_ZB, C_GA, C_GB = 0, 1024, 2048, 4096, 6144, 7168
C_QKV, C_XA, C_CA = 8192, 12288, 13312
HEAD_W = 2 * DK + DV
TAIL_W = 8192
CONV_W = 2048
SEGMENTS = (
    (0, 1024, C_XA), (1024, 1024, C_BA), (2048, 1024, C_CA), (3072, 1024, C_ZA),
    (8192, 2048, C_O), (10240, 2048, C_ZB), (12296, 1024, C_GA), (13320, 1024, C_GB),
) + tuple((4096 + DK * h, DK, C_QKV + HEAD_W * h) for h in range(NH)) + tuple(
    (5120 + DK * h, DK, C_QKV + HEAD_W * h + DK) for h in range(NH)) + tuple(
    (6144 + DV * h, DV, C_QKV + HEAD_W * h + 2 * DK) for h in range(NH))
GATE_COL = 12288

PACK_ROWS = (256, 512, 256, 256, 64)
PACK_TOTAL = sum(PACK_ROWS)
SMALL_ROWS = 16

ADAM_LR, ADAM_B1, ADAM_B2, ADAM_EPS, ADAM_WD, ADAM_STEP = 0.001, 0.9, 0.999, 1e-08, 0.01, 10

VMEM_LIMIT = 56 * 1024 * 1024
MESH = pl.DeviceIdType.MESH
ANY = pl.BlockSpec(memory_space=pl.ANY)


def _cparams(*sem):
    return pltpu.CompilerParams(dimension_semantics=sem, vmem_limit_bytes=VMEM_LIMIT)


def _dot(a, b):
    return jnp.dot(a, b, preferred_element_type=F32)


def _dot_nt(a, b):
    return lax.dot_general(a, b, (((1,), (1,)), ((), ())), preferred_element_type=F32)


def _dot_tn(a, b):
    return lax.dot_general(a, b, (((0,), (0,)), ((), ())), preferred_element_type=F32)


def _sigmoid(x):
    return 1.0 / (1.0 + jnp.exp(-x))


def _logsig(x):
    return jnp.minimum(x, 0.0) - jnp.log(1.0 + jnp.exp(-jnp.abs(x)))


def _rstd(x):
    return lax.rsqrt(jnp.mean(x * x, axis=-1, keepdims=True) + EPS)


def _norm_bwd(dy, xhat, r, g):
    dxh = dy * g
    return r * (dxh - xhat * jnp.mean(dxh * xhat, axis=-1, keepdims=True))


def _f32(ref):
    return ref[...].astype(F32)


def _prenorm(x, g_mix, wg):
    n = x.shape[0]
    tm = min(512, n)

    def body(x_ref, g_ref, wg_ref, hn_ref, gate_ref):
        xv = x_ref[...]
        hn = (xv * _rstd(xv) * g_ref[...]).astype(MXU)
        hn_ref[...] = hn
        gate_ref[...] = _dot(hn, wg_ref[...])

    return pl.pallas_call(
        body, name="prenorm", grid=(n // tm,),
        in_specs=[pl.BlockSpec((tm, D), lambda i: (i, 0)),
                  pl.BlockSpec((1, D), lambda i: (0, 0)),
                  pl.BlockSpec((D, GATE_W), lambda i: (0, 0))],
        out_specs=[pl.BlockSpec((tm, D), lambda i: (i, 0)),
                   pl.BlockSpec((tm, GATE_W), lambda i: (i, 0))],
        out_shape=[jax.ShapeDtypeStruct((n, D), MXU), jax.ShapeDtypeStruct((n, GATE_W), F32)],
        compiler_params=_cparams("arbitrary"),
    )(x, g_mix, wg)


def _proj_tail(hn, w_main):
    n, k = hn.shape
    tm, tn = min(1024, n), 1024
    silu_tiles = (C_ZA // tn, C_ZB // tn, C_ZB // tn + 1)

    def body(a_ref, b_ref, o_ref, act_ref, dact_ref):
        j = pl.program_id(0)
        yb = _dot(a_ref[...], b_ref[...]).astype(MXU)
        o_ref[...] = yb
        is_silu = (j == silu_tiles[0]) | (j == silu_tiles[1]) | (j == silu_tiles[2])

        @pl.when(is_silu)
        def _():
            z = yb.astype(F32)
            s = _sigmoid(z)
            act_ref[...] = (z * s).astype(MXU)
            dact_ref[...] = (s * (1.0 + z * (1.0 - s))).astype(MXU)

        @pl.when(jnp.logical_not(is_silu) & (j != C_BA // tn))
        def _():
            act_ref[...] = _sigmoid(yb.astype(F32)).astype(MXU)

    out_spec = pl.BlockSpec((tm, tn), lambda j, i: (i, j))
    return pl.pallas_call(
        body, name="proj_tail", grid=(TAIL_W // tn, n // tm),
        in_specs=[pl.BlockSpec((tm, k), lambda j, i: (i, 0)),
                  pl.BlockSpec((k, tn), lambda j, i: (0, j))],
        out_specs=[out_spec, out_spec, out_spec],
        out_shape=[jax.ShapeDtypeStruct((n, NMAIN), MXU), jax.ShapeDtypeStruct((n, TAIL_W), MXU),
                   jax.ShapeDtypeStruct((n, TAIL_W), MXU)],
        compiler_params=_cparams("arbitrary", "arbitrary"),
    )(hn, w_main)


def _proj_rest(hn, w_main, proj, tm=2048, tn=1024):
    n, k = hn.shape
    tm = min(tm, n)
    j0 = TAIL_W // tn

    def body(a_ref, b_ref, proj_in, o_ref):
        del proj_in
        o_ref[...] = _dot(a_ref[...], b_ref[...]).astype(MXU)

    return pl.pallas_call(
        body, name="proj_rest", grid=((NMAIN - TAIL_W) // tn, n // tm),
        in_specs=[pl.BlockSpec((tm, k), lambda j, i: (i, 0)),
                  pl.BlockSpec((k, tn), lambda j, i: (0, j0 + j)), ANY],
        out_specs=pl.BlockSpec((tm, tn), lambda j, i: (i, j0 + j)),
        out_shape=jax.ShapeDtypeStruct(proj.shape, proj.dtype),
        input_output_aliases={2: 0},
        compiler_params=_cparams("arbitrary", "arbitrary"),
    )(hn, w_main, proj)


def _matmul_tn(a, b, name, out_dtype=F32, ta=1024, tb=1024, tk=1024):
    n, ka = a.shape
    kb = b.shape[1]
    ta, tb, tk = min(ta, ka), min(tb, kb), min(tk, n)
    nk = n // tk

    def body(a_ref, b_ref, o_ref, acc):
        kk = pl.program_id(2)

        @pl.when(kk == 0)
        def _():
            acc[...] = jnp.zeros_like(acc)

        acc[...] += _dot_tn(a_ref[...].astype(MXU), b_ref[...].astype(MXU))

        @pl.when(kk == nk - 1)
        def _():
            o_ref[...] = acc[...].astype(out_dtype)

    return pl.pallas_call(
        body, name=name, grid=(ka // ta, kb // tb, nk),
        in_specs=[pl.BlockSpec((tk, ta), lambda i, j, kk: (kk, i)),
                  pl.BlockSpec((tk, tb), lambda i, j, kk: (kk, j))],
        out_specs=pl.BlockSpec((ta, tb), lambda i, j, kk: (i, j)),
        out_shape=jax.ShapeDtypeStruct((ka, kb), out_dtype),
        scratch_shapes=[pltpu.VMEM((ta, tb), F32)],
        compiler_params=_cparams("arbitrary", "arbitrary", "arbitrary"),
    )(a, b)


def _gate_vectors(g, hd):
    gt = g.T[0:8, :]
    lane = lax.broadcasted_iota(jnp.int32, g.shape, 1)
    sub = lax.broadcasted_iota(jnp.int32, gt.shape, 0)
    col = lambda j: jnp.sum(jnp.where(lane == j, g, 0.0), axis=1, keepdims=True)
    row = lambda j: jnp.sum(jnp.where(sub == j, gt, 0.0), axis=0, keepdims=True)
    return col(hd), col(hd + NH), row(hd), row(hd + NH)


def _chunk_decay(li_col, li_row, lf_col, lf_row, m_prev):
    n = li_col.shape[0]
    r = lax.broadcasted_iota(jnp.int32, (n, n), 0)
    c = lax.broadcasted_iota(jnp.int32, (n, n), 1)
    tri = r >= c
    b_col = jnp.sum(jnp.where(tri, lf_row, 0.0), axis=1, keepdims=True)
    b_row = jnp.sum(jnp.where(r <= c, lf_col, 0.0), axis=0, keepdims=True)
    b_last = jnp.sum(lf_row, axis=1, keepdims=True)
    dmat = jnp.where(tri, b_col - b_row + li_row, -jnp.inf)
    a_col = b_col + m_prev
    g_col = b_last - b_col + li_col
    m_new = jnp.maximum(b_last + m_prev, jnp.max(g_col, axis=0, keepdims=True))
    w_col = jnp.exp(g_col - m_new)
    decay = jnp.exp(b_last + m_prev - m_new)
    return tri, dmat, a_col, m_new, w_col, decay


def _qkv_specs(row_of):
    base = C_QKV // DK
    q_spec = pl.BlockSpec((LCH, DK), lambda c, h: (row_of(c), base + (HEAD_W // DK) * h))
    k_spec = pl.BlockSpec((LCH, DK), lambda c, h: (row_of(c), base + (HEAD_W // DK) * h + 1))
    v_spec = pl.BlockSpec((LCH, DV), lambda c, h: (row_of(c), C_QKV // DV + (HEAD_W // DV) * h + 1))
    return q_spec, k_spec, v_spec


def _lane_put(col, lane_id, width=GATE_W):
    lane = lax.broadcasted_iota(jnp.int32, (col.shape[0], width), 1)
    return jnp.where(lane == lane_id, col, 0.0)


def _lane_get(block, lane_id):
    lane = lax.broadcasted_iota(jnp.int32, block.shape, 1)
    return jnp.sum(jnp.where(lane == lane_id, block, 0.0), axis=1, keepdims=True)


def _mlstm_fwd(proj, gates, bias):
    n = proj.shape[0]
    nc = n // LCH

    def body(q_ref, k_ref, v_ref, g_ref, bias_ref,
             h_ref, st_ref, cs_ref, ns_ref, ms_ref, c_scr, n_scr, m_scr):
        c = pl.program_id(0)
        hd = pl.program_id(1)

        @pl.when(c == 0)
        def _():
            c_scr[hd] = jnp.zeros((DK, DV), F32)
            n_scr[hd] = jnp.zeros((1, DK), F32)
            m_scr[hd] = jnp.full((1, GATE_W), -jnp.inf, F32)

        @pl.when(hd == 0)
        def _():
            st_ref[...] = jnp.zeros_like(st_ref)

        g = g_ref[...] + bias_ref[...]
        li_col, fr_col, li_row, fr_row = _gate_vectors(g, hd)
        m_all = m_scr[hd]
        m_prev = m_all[0:1, 0:1]
        tri, dmat, a_col, m_new, w_col, decay = _chunk_decay(
            li_col, li_row, _logsig(fr_col), _logsig(fr_row), m_prev)
        m_col = jnp.maximum(a_col, jnp.max(dmat, axis=1, keepdims=True))
        dl = jnp.exp(dmat - m_col)
        inter = jnp.exp(a_col - m_col)

        qs = q_ref[...] * (DK ** -0.5)
        kk = k_ref[...]
        vv = v_ref[...]
        cst = c_scr[hd]
        nst = n_scr[hd]
        cs_ref[...] = cst
        ns_ref[...] = nst
        ms_ref[...] = m_all

        sc = _dot_nt(qs, kk) * dl
        num = _dot(sc.astype(MXU), vv) + inter * _dot(qs, cst.astype(MXU))
        den = (jnp.sum(sc, axis=1, keepdims=True)
               + inter * jnp.sum(qs.astype(F32) * nst, axis=1, keepdims=True))
        nrm = jnp.maximum(jnp.abs(den), jnp.exp(-m_col))
        h_ref[...] = num / nrm
        st_ref[...] += _lane_put(den, hd) + _lane_put(m_col, NH + hd)

        kw = kk.astype(F32) * w_col
        c_scr[hd] = decay * cst + _dot_tn(kw.astype(MXU), vv)
        n_scr[hd] = decay * nst + jnp.sum(kw, axis=0, keepdims=True)
        m_scr[hd] = jnp.broadcast_to(m_new, (1, GATE_W))

    q_spec, k_spec, v_spec = _qkv_specs(lambda c: c)
    return pl.pallas_call(
        body, name="mlstm_fwd", grid=(nc, NH),
        in_specs=[q_spec, k_spec, v_spec,
                  pl.BlockSpec((LCH, GATE_W), lambda c, h: (c, 0)),
                  pl.BlockSpec((1, GATE_W), lambda c, h: (0, 0))],
        out_specs=[pl.BlockSpec((LCH, DV), lambda c, h: (c, h)),
                   pl.BlockSpec((LCH, GATE_W), lambda c, h: (c, 0)),
                   pl.BlockSpec((None, None, DK, DV), lambda c, h: (h, c, 0, 0)),
                   pl.BlockSpec((None, None, 1, DK), lambda c, h: (h, c, 0, 0)),
                   pl.BlockSpec((None, None, 1, GATE_W), lambda c, h: (h, c, 0, 0))],
        out_shape=[jax.ShapeDtypeStruct((n, VD), F32),
                   jax.ShapeDtypeStruct((n, GATE_W), F32),
                   jax.ShapeDtypeStruct((NH, nc, DK, DV), F32),
                   jax.ShapeDtypeStruct((NH, nc, 1, DK), F32),
                   jax.ShapeDtypeStruct((NH, nc, 1, GATE_W), F32)],
        scratch_shapes=[pltpu.VMEM((NH, DK, DV), F32), pltpu.VMEM((NH, 1, DK), F32),
                        pltpu.VMEM((NH, 1, GATE_W), F32)],
        compiler_params=_cparams("arbitrary", "arbitrary"),
    )(proj, proj, proj, gates, bias)


def _mlstm_bwd(proj, gates, bias, h, dh, stats, cs, ns, ms, dproj):
    n = proj.shape[0]
    nc = n // LCH

    def body(q_ref, k_ref, v_ref, g_ref, bias_ref, h_ref, dh_ref, st_ref,
             cs_ref, ns_ref, ms_ref, dproj_in,
             dqkv_ref, dg_ref, gsum_ref, dc_scr, dn_scr):
        del dproj_in
        step = pl.program_id(0)
        hd = pl.program_id(1)

        @pl.when(step == 0)
        def _():
            dc_scr[hd] = jnp.zeros((DK, DV), F32)
            dn_scr[hd] = jnp.zeros((1, DK), F32)

        @pl.when((step == 0) & (hd == 0))
        def _():
            gsum_ref[...] = jnp.zeros_like(gsum_ref)

        @pl.when(hd == 0)
        def _():
            dg_ref[...] = jnp.zeros_like(dg_ref)

        g = g_ref[...] + bias_ref[...]
        li_col, fr_col, li_row, fr_row = _gate_vectors(g, hd)
        m_prev = ms_ref[0:1, 0:1]
        tri, dmat, a_col, m_new, w_col, decay = _chunk_decay(
            li_col, li_row, _logsig(fr_col), _logsig(fr_row), m_prev)
        stats = st_ref[...]
        m_col = _lane_get(stats, NH + hd)
        dl = jnp.exp(dmat - m_col)
        inter = jnp.exp(a_col - m_col)

        qs = q_ref[...] * (DK ** -0.5)
        kk = k_ref[...]
        vv = v_ref[...]
        qf = qs.astype(F32)
        kf = kk.astype(F32)
        cst = cs_ref[...]
        nst = ns_ref[...]
        cb = cst.astype(MXU)
        dcn = dc_scr[hd]
        dnn = dn_scr[hd]
        dcb = dcn.astype(MXU)

        den = _lane_get(stats, hd)
        floor = jnp.exp(-m_col)
        nrm = jnp.maximum(jnp.abs(den), floor)
        dhv = dh_ref[...]
        dnum = dhv / nrm
        dnum_b = dnum.astype(MXU)
        dhh = jnp.sum(dhv * h_ref[...], axis=1, keepdims=True)
        dden = jnp.where(jnp.abs(den) > floor, -dhh / nrm * jnp.sign(den), 0.0)

        sc = _dot_nt(qs, kk) * dl
        dsc = _dot_nt(dnum_b, vv) + dden
        da = (dl * dsc).astype(MXU)
        gmat = sc * dsc

        dq = _dot(da, kk) + inter * (_dot_nt(dnum_b, cb) + dden * nst)
        dk_state = w_col * (_dot_nt(vv, dcb) + dnn)
        dk = _dot_tn(da, qs) + dk_state
        kw = (kf * w_col).astype(MXU)
        dv = _dot_tn(sc.astype(MXU), dnum_b) + _dot(kw, dcb)
        dqkv_ref[:, 0:DK] = (dq * (DK ** -0.5)).astype(dqkv_ref.dtype)
        dqkv_ref[:, DK:2 * DK] = dk.astype(dqkv_ref.dtype)
        dqkv_ref[:, 2 * DK:HEAD_W] = dv.astype(dqkv_ref.dtype)

        num_i = _dot(qs, cb)
        den_i = jnp.sum(qf * nst, axis=1, keepdims=True)
        e_col = inter * (jnp.sum(dnum * num_i, axis=1, keepdims=True) + dden * den_i)
        h_col = jnp.sum(kf * dk_state, axis=1, keepdims=True)
        f_dec = decay * (jnp.sum(jnp.sum(cst * dcn, axis=1, keepdims=True), axis=0, keepdims=True)
                         + jnp.sum(nst * dnn, axis=1, keepdims=True))
        r = lax.broadcasted_iota(jnp.int32, (LCH, LCH), 0)
        c = lax.broadcasted_iota(jnp.int32, (LCH, LCH), 1)
        eye = r == c
        to_col = lambda row: jnp.sum(jnp.where(eye, row, 0.0), axis=1, keepdims=True)
        row_g = jnp.sum(gmat, axis=1, keepdims=True)
        col_g = to_col(jnp.sum(gmat, axis=0, keepdims=True))
        last = lax.broadcasted_iota(jnp.int32, (LCH, 1), 0) == LCH - 1
        db_col = row_g - col_g + e_col - h_col
        db_col = db_col + jnp.where(last, jnp.sum(h_col, axis=0, keepdims=True) + f_dec, 0.0)
        dli_col = col_g + h_col
        dlf_row = jnp.sum(jnp.where(tri, db_col, 0.0), axis=0, keepdims=True)
        df_col = to_col(dlf_row) * _sigmoid(-fr_col)
        dg = _lane_put(dli_col, hd) + _lane_put(df_col, NH + hd)
        dg_ref[...] += dg
        gsum_ref[0:1, 0:GATE_W] += jnp.sum(dg, axis=0, keepdims=True)

        dc_scr[hd] = decay * dcn + _dot_tn((qf * inter).astype(MXU), dnum_b)
        dn_scr[hd] = decay * dnn + jnp.sum(qf * (inter * dden), axis=0, keepdims=True)

    rev = lambda c: nc - 1 - c
    q_spec, k_spec, v_spec = _qkv_specs(rev)
    hv_spec = pl.BlockSpec((LCH, DV), lambda c, h: (rev(c), h))
    gate_spec = pl.BlockSpec((LCH, GATE_W), lambda c, h: (rev(c), 0))
    return pl.pallas_call(
        body, name="mlstm_bwd", grid=(nc, NH),
        in_specs=[q_spec, k_spec, v_spec, gate_spec,
                  pl.BlockSpec((1, GATE_W), lambda c, h: (0, 0)),
                  hv_spec, hv_spec, gate_spec,
                  pl.BlockSpec((None, None, DK, DV), lambda c, h: (h, rev(c), 0, 0)),
                  pl.BlockSpec((None, None, 1, DK), lambda c, h: (h, rev(c), 0, 0)),
                  pl.BlockSpec((None, None, 1, GATE_W), lambda c, h: (h, rev(c), 0, 0)),
                  ANY],
        out_specs=[pl.BlockSpec((LCH, HEAD_W), lambda c, h: (rev(c), C_QKV // HEAD_W + h)),
                   gate_spec,
                   pl.BlockSpec((8, D), lambda c, h: (0, 0))],
        out_shape=[jax.ShapeDtypeStruct(dproj.shape, dproj.dtype),
                   jax.ShapeDtypeStruct((n, GATE_W), F32),
                   jax.ShapeDtypeStruct((8, D), F32)],
        scratch_shapes=[pltpu.VMEM((NH, DK, DV), F32), pltpu.VMEM((NH, 1, DK), F32)],
        input_output_aliases={11: 0},
        compiler_params=_cparams("arbitrary", "arbitrary"),
    )(proj, proj, proj, gates, bias, h, dh, stats, cs, ns, ms, dproj)


def _proj_spec(tm, col, width):
    return pl.BlockSpec((tm, width), lambda i: (i, col // width))


def _halo_prev(tm, col):
    return pl.BlockSpec((8, 1024), lambda i: (jnp.maximum(i * (tm // 8) - 1, 0), col // 1024))


def _const(shape):
    return pl.BlockSpec(shape, lambda i: (0,) * len(shape))


def _conv_inputs(i, tm, xa_ref, ca_ref, xah_ref, cah_ref):
    u = _f32(xa_ref) * _f32(ca_ref)
    uh = jnp.where(i > 0, _f32(xah_ref) * _f32(cah_ref), 0.0)
    rid = lax.broadcasted_iota(jnp.int32, u.shape, 0)
    u1 = jnp.where(rid == 0, uh[7:8, :], pltpu.roll(u, 1, 0))
    u2 = jnp.where(rid == 0, uh[6:7, :], jnp.where(rid == 1, uh[7:8, :], pltpu.roll(u, 2, 0)))
    return u, u1, u2


def _head_norm(hh, gh):
    out = []
    for j in range(NH):
        hj = hh[:, j * DV:(j + 1) * DV]
        rj = _rstd(hj)
        out.append((hj * rj, rj, gh[:, j * DV:(j + 1) * DV]))
    return out


def _tail_fwd(proj, act, h, x, p, t, cw, cb, gh, gple, gfin, wa, wb, wo, wpg, wp):
    n = x.shape[0]
    tm = min(256, n)

    def body(ba_r, sa_r, so_r, sb_r, sga_r, sgb_r, xa_r, ca_r, xah_r, cah_r, h_r, x_r, p_r, t_r,
             cw_r, cb_r, gh_r, gple_r, gfin_r, wa_r, wb_r, wo_r, wpg_r, wp_r,
             apre_o, bpre_o, mg_o, xn1_o, de_o, dgp_o, ya_o, yb_o, x1_o, dx2_o, acc_o):
        i = pl.program_id(0)

        @pl.when(i == 0)
        def _():
            acc_o[...] = jnp.zeros_like(acc_o)

        u, u1, u2 = _conv_inputs(i, tm, xa_r, ca_r, xah_r, cah_r)
        cwv = cw_r[...]
        cv = cwv[0:1, :] * u2 + cwv[1:2, :] * u1 + cwv[2:3, :] * u + cb_r[...]
        a_pre = (_f32(ba_r) * cv * _f32(sa_r)).astype(MXU)
        apre_o[...] = a_pre
        ya = _dot(a_pre, wa_r[...])

        hn = _head_norm(h_r[...], gh_r[...])
        hbn = jnp.concatenate([xh * g for xh, _, g in hn], axis=1)
        b_pre = (_f32(so_r) * hbn * _f32(sb_r)).astype(MXU)
        bpre_o[...] = b_pre
        yb = _dot(b_pre, wb_r[...])
        ya_o[...] = ya.astype(MXU)
        yb_o[...] = yb.astype(MXU)

        mg = (_f32(sga_r) * ya + _f32(sgb_r) * yb).astype(MXU)
        mg_o[...] = mg
        x1 = x_r[...] + _dot(mg, wo_r[...])
        x1_o[...] = x1
        xn1 = (x1 * _rstd(x1) * gple_r[...]).astype(MXU)
        xn1_o[...] = xn1
        gt = _sigmoid(_dot(xn1, wpg_r[...]))
        e = _dot(p_r[...].astype(MXU), wp_r[...])
        x2 = x1 + gt * e
        r2 = _rstd(x2)
        xh2 = x2 * r2
        gf = gfin_r[...]
        diff = xh2 * gf - t_r[...]
        dy = diff * (1.0 / D)
        dx2 = _norm_bwd(dy, xh2, r2, gf)
        dx2_o[...] = dx2
        de_o[...] = (dx2 * gt).astype(MXU)
        dgp_o[...] = (dx2 * e * gt * (1.0 - gt)).astype(MXU)
        acc_o[0:1, :] += jnp.sum(dy * xh2, axis=0, keepdims=True)
        loss = 0.5 * jnp.sum(jnp.sum(diff * diff, axis=1, keepdims=True), axis=0, keepdims=True) * (1.0 / D)
        acc_o[1:2, :] += jnp.broadcast_to(loss, (1, D))

    row = lambda w, dt: (pl.BlockSpec((tm, w), lambda i: (i, 0)), jax.ShapeDtypeStruct((n, w), dt))
    outs = [row(D, MXU), row(VD, MXU), row(D, MXU), row(D, MXU), row(D, MXU), row(D, MXU),
            row(D, MXU), row(D, MXU), row(D, F32), row(D, F32),
            (_const((8, D)), jax.ShapeDtypeStruct((8, D), F32))]
    return pl.pallas_call(
        body, name="tail_fwd", grid=(n // tm,),
        in_specs=[_proj_spec(tm, C_BA, 1024), _proj_spec(tm, C_ZA, 1024),
                  _proj_spec(tm, C_O, 2048), _proj_spec(tm, C_ZB, 2048),
                  _proj_spec(tm, C_GA, 1024), _proj_spec(tm, C_GB, 1024),
                  _proj_spec(tm, C_XA, 1024), _proj_spec(tm, C_CA, 1024),
                  _halo_prev(tm, C_XA), _halo_prev(tm, C_CA),
                  pl.BlockSpec((tm, VD), lambda i: (i, 0)),
                  pl.BlockSpec((tm, D), lambda i: (i, 0)),
                  pl.BlockSpec((tm, PLE), lambda i: (i, 0)),
                  pl.BlockSpec((tm, D), lambda i: (i, 0)),
                  _const((8, D)), _const((1, D)), _const((1, VD)), _const((1, D)), _const((1, D)),
                  _const((D, D)), _const((VD, D)), _const((D, D)), _const((D, D)), _const((PLE, D))],
        out_specs=[s for s, _ in outs],
        out_shape=[s for _, s in outs],
        compiler_params=_cparams("arbitrary"),
    )(proj, *([act] * 5), *([proj] * 4), h, x, p, t, cw, cb, gh, gple, gfin, wa, wb, wo, wpg, wp)


def _tail_bwd(proj, act, dact, h, dgp, dx2, x1, ya, yb, cw, cb, gh, gple, wpg, wo, wb, wa):
    n = x1.shape[0]
    tm = min(128, n)

    def body(ba_r, sa_r, so_r, sb_r, sga_r, sgb_r, dsa_r, dsb_r, xa_r, ca_r, xah_r, cah_r, h_r,
             dgp_r, dx2_r, x1_r, ya_r, yb_r, cw_r, cb_r, gh_r, gple_r,
             wpg_r, wo_r, wb_r, wa_r,
             dproj_o, dcv_o, dh_o, dx1_o, dx1b_o, dya_o, dyb_o, acc_o):
        i = pl.program_id(0)

        @pl.when(i == 0)
        def _():
            acc_o[...] = jnp.zeros_like(acc_o)

        dxn1 = _dot_nt(dgp_r[...], wpg_r[...])
        x1 = x1_r[...]
        r1 = _rstd(x1)
        xh1 = x1 * r1
        acc_o[0:1, 0:D] += jnp.sum(dxn1 * xh1, axis=0, keepdims=True)
        dx1 = dx2_r[...] + _norm_bwd(dxn1, xh1, r1, gple_r[...])
        dx1_o[...] = dx1
        dx1b = dx1.astype(MXU)
        dx1b_o[...] = dx1b

        dmg = _dot_nt(dx1b, wo_r[...])
        sga = _f32(sga_r)
        sgb = _f32(sgb_r)
        dya = (dmg * sga).astype(MXU)
        dyb = (dmg * sgb).astype(MXU)
        dya_o[...] = dya
        dyb_o[...] = dyb
        dproj_o[:, C_GA:C_GA + D] = (dmg * _f32(ya_r) * sga * (1.0 - sga)).astype(MXU)
        dproj_o[:, C_GB:C_GB + D] = (dmg * _f32(yb_r) * sgb * (1.0 - sgb)).astype(MXU)

        db_pre = _dot_nt(dyb, wb_r[...])
        hn = _head_norm(h_r[...], gh_r[...])
        hbn = jnp.concatenate([xh * g for xh, _, g in hn], axis=1)
        so = _f32(so_r)
        sb = _f32(sb_r)
        dproj_o[:, C_O:C_O + VD] = (db_pre * hbn * sb * so * (1.0 - so)).astype(MXU)
        dproj_o[:, C_ZB:C_ZB + VD] = (db_pre * so * hbn * _f32(dsb_r)).astype(MXU)
        dhbn = db_pre * so * sb
        for j, (xh, rj, g) in enumerate(hn):
            dj = dhbn[:, j * DV:(j + 1) * DV]
            acc_o[1:2, j * DV:(j + 1) * DV] += jnp.sum(dj * xh, axis=0, keepdims=True)
            dh_o[:, j * DV:(j + 1) * DV] = _norm_bwd(dj, xh, rj, g)

        da_pre = _dot_nt(dya, wa_r[...])
        u, u1, u2 = _conv_inputs(i, tm, xa_r, ca_r, xah_r, cah_r)
        cwv = cw_r[...]
        cv = cwv[0:1, :] * u2 + cwv[1:2, :] * u1 + cwv[2:3, :] * u + cb_r[...]
        sa = _f32(sa_r)
        ba = _f32(ba_r)
        dproj_o[:, C_BA:C_BA + D] = (da_pre * cv * sa).astype(MXU)
        dproj_o[:, C_ZA:C_ZA + D] = (da_pre * ba * cv * _f32(dsa_r)).astype(MXU)
        dcv = da_pre * ba * sa
        dcv_o[...] = dcv.astype(MXU)
        acc_o[2:3, 0:D] += jnp.sum(dcv, axis=0, keepdims=True)
        acc_o[3:4, 0:D] += jnp.sum(dcv * u2, axis=0, keepdims=True)
        acc_o[4:5, 0:D] += jnp.sum(dcv * u1, axis=0, keepdims=True)
        acc_o[5:6, 0:D] += jnp.sum(dcv * u, axis=0, keepdims=True)

    row = lambda w, dt: (pl.BlockSpec((tm, w), lambda i: (i, 0)), jax.ShapeDtypeStruct((n, w), dt))
    outs = [(pl.BlockSpec((tm, TAIL_W), lambda i: (i, 0)), jax.ShapeDtypeStruct((n, NMAIN), MXU)),
            row(D, MXU), row(VD, F32), row(D, F32), row(D, MXU), row(D, MXU), row(D, MXU),
            (_const((8, VD)), jax.ShapeDtypeStruct((8, VD), F32))]
    rowin = lambda w: pl.BlockSpec((tm, w), lambda i: (i, 0))
    return pl.pallas_call(
        body, name="tail_bwd", grid=(n // tm,),
        in_specs=[_proj_spec(tm, C_BA, 1024), _proj_spec(tm, C_ZA, 1024),
                  _proj_spec(tm, C_O, 2048), _proj_spec(tm, C_ZB, 2048),
                  _proj_spec(tm, C_GA, 1024), _proj_spec(tm, C_GB, 1024),
                  _proj_spec(tm, C_ZA, 1024), _proj_spec(tm, C_ZB, 2048),
                  _proj_spec(tm, C_XA, 1024), _proj_spec(tm, C_CA, 1024),
                  _halo_prev(tm, C_XA), _halo_prev(tm, C_CA),
                  rowin(VD), rowin(D), rowin(D), rowin(D), rowin(D), rowin(D),
                  _const((8, D)), _const((1, D)), _const((1, VD)), _const((1, D)),
                  _const((D, D)), _const((D, D)), _const((VD, D)), _const((D, D))],
        out_specs=[s for s, _ in outs],
        out_shape=[s for _, s in outs],
        compiler_params=_cparams("arbitrary"),
    )(proj, *([act] * 5), dact, dact, *([proj] * 4), h, dgp, dx2, x1, ya, yb, cw, cb, gh, gple, wpg, wo, wb, wa)


def _conv_bwd(proj, dcv, cw, dproj):
    n = dcv.shape[0]
    tm = min(512, n)
    nt = n // tm

    def body(xa_r, ca_r, dcv_r, nxt_r, cw_r, dproj_in, dxc_o):
        del dproj_in
        i = pl.program_id(0)
        dcv_v = _f32(dcv_r)
        nxt = jnp.where(i < nt - 1, _f32(nxt_r), 0.0)
        rid = lax.broadcasted_iota(jnp.int32, dcv_v.shape, 0)
        d1 = jnp.where(rid == tm - 1, nxt[0:1, :], pltpu.roll(dcv_v, tm - 1, 0))
        d2 = jnp.where(rid == tm - 2, nxt[0:1, :],
                       jnp.where(rid == tm - 1, nxt[1:2, :], pltpu.roll(dcv_v, tm - 2, 0)))
        cwv = cw_r[...]
        du = cwv[2:3, :] * dcv_v + cwv[1:2, :] * d1 + cwv[0:1, :] * d2
        dxc_o[:, 0:D] = (du * _f32(ca_r)).astype(MXU)
        dxc_o[:, D:2 * D] = (du * _f32(xa_r)).astype(MXU)

    return pl.pallas_call(
        body, name="conv_bwd", grid=(nt,),
        in_specs=[_proj_spec(tm, C_XA, 1024), _proj_spec(tm, C_CA, 1024),
                  pl.BlockSpec((tm, D), lambda i: (i, 0)),
                  pl.BlockSpec((8, D), lambda i: (jnp.minimum((i + 1) * (tm // 8), n // 8 - 1), 0)),
                  _const((8, D)), ANY],
        out_specs=pl.BlockSpec((tm, CONV_W), lambda i: (i, C_XA // CONV_W)),
        out_shape=jax.ShapeDtypeStruct(dproj.shape, dproj.dtype),
        input_output_aliases={5: 0},
        compiler_params=_cparams("arbitrary"),
    )(proj, proj, dcv, dcv, cw, dproj)


def _position():
    return lax.axis_index("x"), lax.axis_index("y"), lax.axis_index("c")


def _flip(v, bit):
    return 1 - v if bit else v


def _scatter_copies(srcs, dsts, send_sems, recv_sems, local_sems):
    x, y, c = _position()
    me = 4 * x + 2 * y + c
    na = len(srcs)
    copies = []
    for r in range(N_DEV):
        px, py, pc = _flip(x, (r >> 2) & 1), _flip(y, (r >> 1) & 1), _flip(c, r & 1)
        for a in range(na):
            rows = dsts[a].shape[1]
            src = srcs[a].at[2 * px + py, pl.ds(pl.multiple_of(pc * rows, rows), rows)]
            dst = dsts[a].at[me]
            if r == 0:
                copies.append(pltpu.make_async_copy(src, dst, local_sems.at[a]))
            else:
                k = (r - 1) * na + a
                copies.append(pltpu.make_async_remote_copy(
                    src_ref=src, dst_ref=dst, send_sem=send_sems.at[k], recv_sem=recv_sems.at[k],
                    device_id=(px, py, pc), device_id_type=MESH))
    return copies


def _input_grad(dproj, dgates, w_main, w_gate, x, dx1, g_mix, g_in, g_rest):
    n = x.shape[0]
    tm, tk = min(1024, n), 2048
    nk = NMAIN // tk
    nt = n // tm

    def body(dp_r, dg_r, w_r, wg_r, x_r, dx1_r, g_r, gin, grest,
             gx_o, acc_o, oin, orest, acc, send_sems, recv_sems, local_sems):
        i = pl.program_id(0)
        kk = pl.program_id(1)
        copies = _scatter_copies((gin, grest), (oin, orest), send_sems, recv_sems, local_sems)

        @pl.when((i == 0) & (kk == 0))
        def _():
            acc_o[...] = jnp.zeros_like(acc_o)
            for cp in copies:
                cp.start()

        @pl.when(kk == 0)
        def _():
            acc[...] = _dot_nt(dg_r[...].astype(MXU), wg_r[...])

        acc[...] += _dot_nt(dp_r[...], w_r[...])

        @pl.when(kk == nk - 1)
        def _():
            dhn = acc[...]
            xv = x_r[...]
            r0 = _rstd(xv)
            xh = xv * r0
            acc_o[0:1, :] += jnp.sum(dhn * xh, axis=0, keepdims=True)
            gx_o[...] = dx1_r[...] + _norm_bwd(dhn, xh, r0, g_r[...])

        @pl.when((i == nt - 1) & (kk == nk - 1))
        def _():
            for cp in copies:
                cp.wait()

    nrem = 2 * (N_DEV - 1)
    r_in, r_rest = g_in.shape[1] // 2, g_rest.shape[1] // 2
    return pl.pallas_call(
        body, name="input_grad", grid=(nt, nk),
        in_specs=[pl.BlockSpec((tm, tk), lambda i, kk: (i, kk)),
                  pl.BlockSpec((tm, GATE_W), lambda i, kk: (i, 0)),
                  pl.BlockSpec((D, tk), lambda i, kk: (0, kk)),
                  pl.BlockSpec((D, GATE_W), lambda i, kk: (0, 0)),
                  pl.BlockSpec((tm, D), lambda i, kk: (i, 0)),
                  pl.BlockSpec((tm, D), lambda i, kk: (i, 0)),
                  pl.BlockSpec((1, D), lambda i, kk: (0, 0)),
                  ANY, ANY],
        out_specs=[pl.BlockSpec((tm, D), lambda i, kk: (i, 0)),
                   pl.BlockSpec((8, D), lambda i, kk: (0, 0)),
                   ANY, ANY],
        out_shape=[jax.ShapeDtypeStruct((n, D), F32), jax.ShapeDtypeStruct((8, D), F32),
                   jax.ShapeDtypeStruct((N_DEV, r_in) + g_in.shape[2:], g_in.dtype),
                   jax.ShapeDtypeStruct((N_DEV, r_rest) + g_rest.shape[2:], g_rest.dtype)],
        scratch_shapes=[pltpu.VMEM((tm, D), F32),
                        pltpu.SemaphoreType.DMA((nrem,)), pltpu.SemaphoreType.DMA((nrem,)),
                        pltpu.SemaphoreType.DMA((2,))],
        compiler_params=_cparams("arbitrary", "arbitrary"),
    )(dproj, dgates, w_main, w_gate, x, dx1, g_mix, g_in, g_rest)


def _pack_small(acc_f, acc_b, acc_x, gsum):
    def body(f_r, b_r, x_r, s_r, o_r):
        o_r[...] = jnp.zeros_like(o_r)
        o_r[0:1, :] = x_r[0:1, :]
        o_r[1:2, :] = b_r[2:3, 0:D]
        o_r[2:3, :] = b_r[1:2, 0:D]
        o_r[3:4, :] = b_r[1:2, D:2 * D]
        o_r[4:5, :] = b_r[0:1, 0:D]
        o_r[5:6, :] = f_r[0:1, :]
        o_r[6:9, :] = b_r[3:6, 0:D]
        lane = lax.broadcasted_iota(jnp.int32, (1, D), 1)
        o_r[9:10, :] = jnp.where(lane < 2 * NH, s_r[0:1, :], jnp.where(lane == 2 * NH, f_r[1:2, :], 0.0))

    return pl.pallas_call(
        body, name="pack_small",
        out_shape=jax.ShapeDtypeStruct((SMALL_ROWS, D), F32),
    )(acc_f, acc_b, acc_x, gsum)


def _sum_slots(r, name, tr=64, half=None):
    s, rows, w = r.shape
    tr = min(tr, rows)
    assert rows % tr == 0
    nt = rows // tr

    def body(*refs):
        r_ref, o_ref = refs[-2:]
        tot = r_ref[0].astype(F32)
        for k in range(1, s):
            tot = tot + r_ref[k].astype(F32)
        o_ref[...] = tot

    if half is None:
        return pl.pallas_call(
            body, name=name, grid=(nt,),
            in_specs=[pl.BlockSpec((s, tr, w), lambda i: (0, i, 0))],
            out_specs=pl.BlockSpec((tr, w), lambda i: (i, 0)),
            out_shape=jax.ShapeDtypeStruct((rows, w), F32),
            compiler_params=_cparams("arbitrary"),
        )(r)
    return pl.pallas_call(
        body, name=name,
        grid_spec=pltpu.PrefetchScalarGridSpec(
            num_scalar_prefetch=1, grid=(nt,),
            in_specs=[pl.BlockSpec((s, tr, w), lambda i, hf: (0, i, 0))],
            out_specs=pl.BlockSpec((tr, w), lambda i, hf: (hf[0] * nt + i, 0))),
        out_shape=jax.ShapeDtypeStruct((2 * rows, w), F32),
        compiler_params=_cparams("arbitrary"),
    )(half, r)


def _adamw_body(w_r, g_r, m_r, v_r, d_o, m_o, v_o):
    c1 = 1.0 - ADAM_B1 ** ADAM_STEP
    c2 = 1.0 - ADAM_B2 ** ADAM_STEP
    gv = g_r[...]
    mn = ADAM_B1 * m_r[...] + (1.0 - ADAM_B1) * gv
    vn = ADAM_B2 * v_r[...] + (1.0 - ADAM_B2) * (gv * gv)
    m_o[...] = mn
    v_o[...] = vn
    d_o[...] = -ADAM_LR * ((mn / c1) / (jnp.sqrt(vn / c2) + ADAM_EPS) + ADAM_WD * w_r[...])


def _adamw_slabs(w, g, m, v, name, tr):
    rows = w.shape[0]
    assert rows % tr == 0 and w.shape == g.shape

    def body(*refs):
        _adamw_body(*refs)

    spec = pl.BlockSpec((tr,) + w.shape[1:], lambda i: (i, 0, 0))
    shp = jax.ShapeDtypeStruct(w.shape, F32)
    return pl.pallas_call(
        body, name=name, grid=(rows // tr,),
        in_specs=[spec] * 4, out_specs=[spec] * 3, out_shape=[shp] * 3,
        compiler_params=_cparams("arbitrary"),
    )(w, g, m, v)


def _adamw(w, g, m, v, name, g_row0=0, tr=64):
    lead = w.ndim == 3
    rows, cols = w.shape[-2:]
    tr = min(tr, rows)
    assert rows % tr == 0 and g_row0 % tr == 0 and g.shape[1] == cols

    def body(*refs):
        _adamw_body(*refs)

    if lead:
        spec = pl.BlockSpec((None, tr, cols), lambda i: (0, i, 0))
    else:
        spec = pl.BlockSpec((tr, cols), lambda i: (i, 0))
    g_spec = pl.BlockSpec((tr, cols), lambda i: (g_row0 // tr + i, 0))
    shp = jax.ShapeDtypeStruct(w.shape, F32)
    return pl.pallas_call(
        body, name=name, grid=(rows // tr,),
        in_specs=[spec, g_spec, spec, spec], out_specs=[spec] * 3, out_shape=[shp] * 3,
        compiler_params=_cparams("arbitrary"),
    )(w, g, m, v)


def _allgather_chips(halved, whole):
    arrs = list(halved) + list(whole)
    nh, na = len(halved), len(arrs)
    relations = ((1, 0), (0, 1), (1, 1))
    n_ici = 3 * na

    def body(*refs):
        ins, outs = refs[:na], refs[na:2 * na]
        send_sems, recv_sems = refs[2 * na:]
        x, y, c = _position()
        mine = 2 * x + y

        def half(a, chip, core):
            rows = ins[a].shape[0] // 2
            return outs[a].at[chip, pl.ds(pl.multiple_of(core * rows, rows), rows)]

        def remote(src, dst, k, to):
            return pltpu.make_async_remote_copy(src_ref=src, dst_ref=dst, send_sem=send_sems.at[k],
                                                recv_sem=recv_sems.at[k], device_id=to, device_id_type=MESH)

        first, passed = [], []
        for j, (rx, ry) in enumerate(relations):
            px, py = _flip(x, rx), _flip(y, ry)
            for a in range(na):
                if a < nh:
                    rows = ins[a].shape[0] // 2
                    src = ins[a].at[pl.ds(pl.multiple_of(c * rows, rows), rows)]
                    first.append(remote(src, half(a, mine, c), j * na + a, (px, py, c)))
                    landed = half(a, 2 * px + py, c)
                    passed.append(remote(landed, landed, n_ici + j * nh + a, (x, y, 1 - c)))
                else:
                    first.append(remote(ins[a], outs[a].at[mine], j * na + a, (px, py, c)))
        for cp in first:
            cp.start()
        k = 0
        for j in range(3):
            for a in range(na):
                first[j * na + a].wait_recv()
                if a < nh:
                    passed[k].start()
                    k += 1
        for cp in passed:
            cp.wait_recv()
        for cp in first + passed:
            cp.wait_send()

    nsem = n_ici + 3 * nh
    outs = pl.pallas_call(
        body, name="allgather_weights",
        in_specs=[ANY] * na, out_specs=[ANY] * na,
        out_shape=[jax.ShapeDtypeStruct((N_CHIPS,) + a.shape, a.dtype) for a in arrs],
        scratch_shapes=[pltpu.SemaphoreType.DMA((nsem,)), pltpu.SemaphoreType.DMA((nsem,))],
    )(*arrs)
    chip = 2 * lax.axis_index("x") + lax.axis_index("y")
    return [lax.dynamic_update_slice(o, a[None], (chip,) + (0,) * a.ndim) for o, a in zip(outs, arrs)]


def _join_halves(fulls, small):
    na = len(fulls)

    def body(*refs):
        ins, sm = refs[:na], refs[na]
        outs, osm = refs[na + 1:2 * na + 1], refs[2 * na + 1]
        send_sems, recv_sems, local_sem = refs[2 * na + 2:]
        del ins
        x, y, c = _position()
        me = 4 * x + 2 * y + c
        copies = []
        for a in range(na):
            rows = outs[a].shape[0] // 2
            mine = outs[a].at[pl.ds(pl.multiple_of(c * rows, rows), rows)]
            copies.append(pltpu.make_async_remote_copy(
                src_ref=mine, dst_ref=mine, send_sem=send_sems.at[a], recv_sem=recv_sems.at[a],
                device_id=(x, y, 1 - c), device_id_type=MESH))
        copies.append(pltpu.make_async_copy(sm, osm.at[me], local_sem))
        for r in range(1, N_DEV):
            to = (_flip(x, (r >> 2) & 1), _flip(y, (r >> 1) & 1), _flip(c, r & 1))
            copies.append(pltpu.make_async_remote_copy(
                src_ref=sm, dst_ref=osm.at[me], send_sem=send_sems.at[na + r - 1],
                recv_sem=recv_sems.at[na + r - 1], device_id=to, device_id_type=MESH))
        for cp in copies:
            cp.start()
        for cp in copies:
            cp.wait()

    nsem = na + N_DEV - 1
    return pl.pallas_call(
        body, name="join_halves",
        in_specs=[ANY] * (na + 1), out_specs=[ANY] * (na + 1),
        out_shape=[jax.ShapeDtypeStruct(a.shape, a.dtype) for a in fulls]
        + [jax.ShapeDtypeStruct((N_DEV,) + small.shape, small.dtype)],
        scratch_shapes=[pltpu.SemaphoreType.DMA((nsem,)), pltpu.SemaphoreType.DMA((nsem,)),
                        pltpu.SemaphoreType.DMA],
        input_output_aliases={a: a for a in range(na)},
    )(*fulls, small)


def _to_internal(w_glob):
    order = sorted(SEGMENTS, key=lambda s: s[2])
    main = jnp.concatenate([w_glob[..., g0:g0 + w] for g0, w, _ in order], axis=-1)
    gate = w_glob[..., GATE_COL:GATE_COL + 2 * NH]
    pad = [(0, 0)] * (w_glob.ndim - 1) + [(0, GATE_W - 2 * NH)]
    return main, jnp.pad(gate, pad)


def _to_global(main, gate):
    parts = sorted([(g0, main[..., i0:i0 + w]) for g0, w, i0 in SEGMENTS]
                   + [(GATE_COL, gate[..., 0:2 * NH])], key=lambda s: s[0])
    return jnp.concatenate([p for _, p in parts], axis=-1)


def _pack_rows(wa, wb, wo, wpg, wp):
    return jnp.concatenate([wa, wb, wo, wpg, wp.reshape(PACK_ROWS[4], D)], axis=0)


def _pad_rows(a, rows=8):
    return jnp.pad(a, ((0, rows - a.shape[0]), (0, 0)))


def kernel(x, p, g_mix, w_in, conv_w, conv_b, w_a_out, b_gates, g_head, w_b_out, w_o, g_ple, w_ple_gate, w_ple, g_final, loss_target, m_g_mix, m_w_in, m_conv_w, m_conv_b, m_w_a_out, m_b_gates, m_g_head, m_w_b_out, m_w_o, m_g_ple, m_w_ple_gate, m_w_ple, m_g_final, v_g_mix, v_w_in, v_conv_w, v_conv_b, v_w_a_out, v_b_gates, v_g_head, v_w_b_out, v_w_o, v_g_ple, v_w_ple_gate, v_w_ple, v_g_final):
    chip = 2 * lax.axis_index("x") + lax.axis_index("y")
    xs, ps, ts = x[0], p[0, 0], loss_target[0]
    g_fin = g_final.reshape(1, D)

    pack_w = _pack_rows(w_a_out[0], w_b_out[0], w_o[0], w_ple_gate[0], w_ple[0])
    g_win, g_pack, g_cw = _allgather_chips(
        [w_in[0].astype(MXU), pack_w.astype(MXU)], [_pad_rows(conv_w[0])])
    w_glob = jnp.transpose(g_win, (1, 0, 2)).reshape(D, N_IN)
    w_main, w_gate = _to_internal(w_glob)
    offs = [0, 256, 768, 1024, 1280, 1344]
    wa, wb, wo, wpg = [g_pack[:, offs[k]:offs[k + 1]].reshape(-1, D) for k in range(4)]
    wp = jnp.transpose(g_pack[:, offs[4]:offs[5]].reshape(N_CHIPS, PLE, PLE), (1, 0, 2)).reshape(PLE, D)
    cw = jnp.transpose(g_cw, (1, 0, 2)).reshape(8, D)

    bias = jnp.pad(b_gates, ((0, 0), (0, GATE_W - 2 * NH)))
    hn, gates = _prenorm(xs, g_mix, w_gate)
    proj, act, dact = _proj_tail(hn, w_main)
    proj = _proj_rest(hn, w_main, proj)
    h, stats, cs, ns, ms = _mlstm_fwd(proj, gates, bias)
    (a_pre, b_pre, mg, xn1, de, dgp, ya, yb, x1, dx2, acc_f) = _tail_fwd(
        proj, act, h, xs, ps, ts, cw, conv_b, g_head, g_ple, g_fin, wa, wb, wo, wpg, wp)
    dproj, dcv, dh, dx1, dx1b, dya, dyb, acc_b = _tail_bwd(
        proj, act, dact, h, dgp, dx2, x1, ya, yb, cw, conv_b, g_head, g_ple, wpg, wo, wb, wa)
    dproj = _conv_bwd(proj, dcv, cw, dproj)
    dproj, dgates, gsum = _mlstm_bwd(proj, gates, bias, h, dh, stats, cs, ns, ms, dproj)
    d_main = _matmul_tn(hn, dproj, "dw_in", out_dtype=WIRE, tk=2048)
    d_gate = _matmul_tn(hn, dgates, "dw_gate", out_dtype=WIRE)
    d_wa = _matmul_tn(a_pre, dya, "dw_a_out", out_dtype=WIRE)
    d_wb = _matmul_tn(b_pre, dyb, "dw_b_out", out_dtype=WIRE)
    d_wo = _matmul_tn(mg, dx1b, "dw_o", out_dtype=WIRE)
    d_wpg = _matmul_tn(xn1, dgp, "dw_ple_gate", out_dtype=WIRE)
    d_wp = _matmul_tn(ps, de, "dw_ple", out_dtype=WIRE)

    g_in = jnp.transpose(_to_global(d_main, d_gate).reshape(D, N_CHIPS, N_IN // N_CHIPS), (1, 0, 2))
    d_wp_c = jnp.transpose(d_wp.reshape(PLE, N_CHIPS, PLE), (1, 0, 2)).reshape(N_CHIPS, PACK_ROWS[4], D)
    g_rest = jnp.concatenate(
        [d_wa.reshape(N_CHIPS, -1, D), d_wb.reshape(N_CHIPS, -1, D), d_wo.reshape(N_CHIPS, -1, D),
         d_wpg.reshape(N_CHIPS, -1, D), d_wp_c], axis=1)
    grad_x, acc_x, r_in, r_rest = _input_grad(dproj, dgates, w_main, w_gate, xs, dx1, g_mix, g_in, g_rest)
    small = _pack_small(acc_f, acc_b, acc_x, gsum)
    core = lax.axis_index("c").astype(jnp.int32).reshape(1)
    gw_in, gw_rest, r_small = _join_halves(
        [_sum_slots(r_in, "sum_w_in", half=core), _sum_slots(r_rest, "sum_rest", tr=96, half=core)], small)
    gs = _sum_slots(r_small, "sum_small", tr=SMALL_ROWS)

    big = []
    row0 = 0
    for name, w, m, v in (("w_a_out", w_a_out, m_w_a_out, v_w_a_out), ("w_b_out", w_b_out, m_w_b_out, v_w_b_out),
                          ("w_o", w_o, m_w_o, v_w_o), ("w_ple_gate", w_ple_gate, m_w_ple_gate, v_w_ple_gate)):
        big.append((name, w, m, v, gw_rest, row0))
        row0 += w.shape[1]
    g_wp = gw_rest[row0:row0 + PACK_ROWS[4]].reshape(PLE, PLE)
    big.append(("w_ple", w_ple, m_w_ple, v_w_ple, g_wp, 0))
    upd = {name: _adamw(w, g, m, v, "adamw_" + name, g_row0=r0) for name, w, m, v, g, r0 in big}
    g_big = {name: (g if name == "w_ple" else g[r0:r0 + w.shape[1]])[None] for name, w, m, v, g, r0 in big}
    ncol = w_in.shape[2]
    slabs = lambda a: jnp.transpose(a, (2, 0, 1)).reshape(ncol, 8, D // 8)
    unslab = lambda a: jnp.transpose(a, (1, 2, 0)).reshape(1, D, ncol)
    upd["w_in"] = [unslab(u) for u in _adamw_slabs(
        slabs(w_in), gw_in.T.reshape(ncol, 8, D // 8), slabs(m_w_in), slabs(v_w_in), "adamw_w_in", tr=326)]
    g_big["w_in"] = gw_in[None]

    lane = lax.broadcasted_iota(jnp.int32, (1, D), 1)
    g_small = jnp.concatenate([gs[0:6], jnp.where(lane < 2 * NH, gs[9:10], 0.0), jnp.zeros((1, D), F32)], axis=0)

    def small_pack(gm, cb_, bg, gh, gp, gf):
        return jnp.concatenate([gm, cb_, gh.reshape(2, D), gp, gf.reshape(1, D),
                                jnp.pad(bg, ((0, 0), (0, D - 2 * NH))), jnp.zeros((1, D), F32)], axis=0)

    ws = small_pack(g_mix, conv_b, b_gates, g_head, g_ple, g_final)
    ms_ = small_pack(m_g_mix, m_conv_b, m_b_gates, m_g_head, m_g_ple, m_g_final)
    vs = small_pack(v_g_mix, v_conv_b, v_b_gates, v_g_head, v_g_ple, v_g_final)
    upd_s = _adamw(ws, g_small, ms_, vs, "adamw_small")
    g_cw_mine = _pad_rows(lax.dynamic_slice(gs[6:9], (0, chip * PLE), (3, PLE)))
    upd_c = _adamw(_pad_rows(conv_w[0]), g_cw_mine, _pad_rows(m_conv_w[0]), _pad_rows(v_conv_w[0]),
                   "adamw_conv_w")

    def leaves(bigs, sm, cwv):
        return [sm[0:1], bigs["w_in"], cwv[0:3][None], sm[1:2], bigs["w_a_out"], sm[6:7, 0:2 * NH],
                sm[2:4].reshape(1, VD), bigs["w_b_out"], bigs["w_o"], sm[4:5], bigs["w_ple_gate"],
                bigs["w_ple"], sm[5]]

    loss = gs[9, 2 * NH]
    outs = [loss, grad_x[None]] + leaves(g_big, g_small, g_cw_mine)
    for k in range(3):
        outs += leaves({name: u[k] for name, u in upd.items()}, upd_s[k], upd_c[k])
    return tuple(outs)
```

```python
import jax
import jax.numpy as jnp
from jax import lax
from jax.experimental import pallas as pl
from jax.experimental.pallas import tpu as pltpu

F32 = jnp.float32
MXU = jnp.bfloat16
WIRE = jnp.bfloat16

D = 1024
NH, DK, DV = 4, 256, 512
VD = NH * DV
PLE = 256
LCH = 256
EPS = 1e-6
N_IN = 14344
NMAIN = 14336
GATE_W = 128
N_CHIPS, N_DEV = 4, 8

C_BA, C_ZA, C_O, C_ZB, C_GA, C_GB = 0, 1024, 2048, 4096, 6144, 7168
C_QKV, C_XA, C_CA = 8192, 12288, 13312
HEAD_W = 2 * DK + DV
TAIL_W = 8192
CONV_W = 2048
SEGMENTS = (
    (0, 1024, C_XA), (1024, 1024, C_BA), (2048, 1024, C_CA), (3072, 1024, C_ZA),
    (8192, 2048, C_O), (10240, 2048, C_ZB), (12296, 1024, C_GA), (13320, 1024, C_GB),
) + tuple((4096 + DK * h, DK, C_QKV + HEAD_W * h) for h in range(NH)) + tuple(
    (5120 + DK * h, DK, C_QKV + HEAD_W * h + DK) for h in range(NH)) + tuple(
    (6144 + DV * h, DV, C_QKV + HEAD_W * h + 2 * DK) for h in range(NH))
GATE_COL = 12288

COLS_CHIP = N_IN // N_CHIPS
SLABS_CHIP = COLS_CHIP // 2
HALF_START = SLABS_CHIP // 2
HALF_SLABS = SLABS_CHIP - HALF_START

PACK_ROWS = (256, 512, 256, 256, 64)
PACK_TOTAL = sum(PACK_ROWS)
SMALL_ROWS = 16

ADAM_LR, ADAM_B1, ADAM_B2, ADAM_EPS, ADAM_WD, ADAM_STEP = 0.001, 0.9, 0.999, 1e-08, 0.01, 10

VMEM_LIMIT = 56 * 1024 * 1024
MESH = pl.DeviceIdType.MESH
ANY = pl.BlockSpec(memory_space=pl.ANY)


def _cparams(*sem):
    return pltpu.CompilerParams(dimension_semantics=sem, vmem_limit_bytes=VMEM_LIMIT)


def _dot(a, b):
    return jnp.dot(a, b, preferred_element_type=F32)


def _dot_nt(a, b):
    return lax.dot_general(a, b, (((1,), (1,)), ((), ())), preferred_element_type=F32)


def _dot_tn(a, b):
    return lax.dot_general(a, b, (((0,), (0,)), ((), ())), preferred_element_type=F32)


def _sigmoid(x):
    return 1.0 / (1.0 + jnp.exp(-x))


def _logsig(x):
    return jnp.minimum(x, 0.0) - jnp.log(1.0 + jnp.exp(-jnp.abs(x)))


def _rstd(x):
    return lax.rsqrt(jnp.mean(x * x, axis=-1, keepdims=True) + EPS)


def _norm_bwd(dy, xhat, r, g):
    dxh = dy * g
    return r * (dxh - xhat * jnp.mean(dxh * xhat, axis=-1, keepdims=True))


def _f32(ref):
    return ref[...].astype(F32)


def _prenorm(x, g_mix, wg_t):
    n = x.shape[0]
    tm = min(512, n)

    def body(x_ref, g_ref, wg_ref, hn_ref, gate_ref):
        xv = x_ref[...]
        hn = (xv * _rstd(xv) * g_ref[...]).astype(MXU)
        hn_ref[...] = hn
        gate_ref[...] = _dot_nt(hn, wg_ref[...])

    return pl.pallas_call(
        body, name="prenorm", grid=(n // tm,),
        in_specs=[pl.BlockSpec((tm, D), lambda i: (i, 0)),
                  pl.BlockSpec((1, D), lambda i: (0, 0)),
                  pl.BlockSpec((GATE_W, D), lambda i: (0, 0))],
        out_specs=[pl.BlockSpec((tm, D), lambda i: (i, 0)),
                   pl.BlockSpec((tm, GATE_W), lambda i: (i, 0))],
        out_shape=[jax.ShapeDtypeStruct((n, D), MXU), jax.ShapeDtypeStruct((n, GATE_W), F32)],
        compiler_params=_cparams("arbitrary"),
    )(x, g_mix, wg_t)


def _proj(hn, w_t, tm=2048, tn=1024):
    n, k = hn.shape
    m = w_t.shape[0]
    tm = min(tm, n)

    def body(a_ref, b_ref, o_ref):
        o_ref[...] = _dot_nt(a_ref[...], b_ref[...]).astype(MXU)

    return pl.pallas_call(
        body, name="proj", grid=(m // tn, n // tm),
        in_specs=[pl.BlockSpec((tm, k), lambda j, i: (i, 0)),
                  pl.BlockSpec((tn, k), lambda j, i: (j, 0))],
        out_specs=pl.BlockSpec((tm, tn), lambda j, i: (i, j)),
        out_shape=jax.ShapeDtypeStruct((n, m), MXU),
        compiler_params=_cparams("arbitrary", "arbitrary"),
    )(hn, w_t)


def _matmul_tn(a, b, name, out_dtype=F32, ta=1024, tb=1024, tk=1024):
    n, ka = a.shape
    kb = b.shape[1]
    ta, tb, tk = min(ta, ka), min(tb, kb), min(tk, n)
    nk = n // tk

    def body(a_ref, b_ref, o_ref, acc):
        kk = pl.program_id(2)

        @pl.when(kk == 0)
        def _():
            acc[...] = jnp.zeros_like(acc)

        acc[...] += _dot_tn(a_ref[...].astype(MXU), b_ref[...].astype(MXU))

        @pl.when(kk == nk - 1)
        def _():
            o_ref[...] = acc[...].astype(out_dtype)

    return pl.pallas_call(
        body, name=name, grid=(ka // ta, kb // tb, nk),
        in_specs=[pl.BlockSpec((tk, ta), lambda i, j, kk: (kk, i)),
                  pl.BlockSpec((tk, tb), lambda i, j, kk: (kk, j))],
        out_specs=pl.BlockSpec((ta, tb), lambda i, j, kk: (i, j)),
        out_shape=jax.ShapeDtypeStruct((ka, kb), out_dtype),
        scratch_shapes=[pltpu.VMEM((ta, tb), F32)],
        compiler_params=_cparams("arbitrary", "arbitrary", "arbitrary"),
    )(a, b)


def _gate_vectors(g, hd):
    gt = g.T[0:8, :]
    lane = lax.broadcasted_iota(jnp.int32, g.shape, 1)
    sub = lax.broadcasted_iota(jnp.int32, gt.shape, 0)
    col = lambda j: jnp.sum(jnp.where(lane == j, g, 0.0), axis=1, keepdims=True)
    row = lambda j: jnp.sum(jnp.where(sub == j, gt, 0.0), axis=0, keepdims=True)
    return col(hd), col(hd + NH), row(hd), row(hd + NH)


def _chunk_decay(li_col, li_row, lf_col, lf_row, m_prev):
    n = li_col.shape[0]
    r = lax.broadcasted_iota(jnp.int32, (n, n), 0)
    c = lax.broadcasted_iota(jnp.int32, (n, n), 1)
    tri = r >= c
    b_col = jnp.sum(jnp.where(tri, lf_row, 0.0), axis=1, keepdims=True)
    b_row = jnp.sum(jnp.where(r <= c, lf_col, 0.0), axis=0, keepdims=True)
    b_last = jnp.sum(lf_row, axis=1, keepdims=True)
    dmat = jnp.where(tri, b_col - b_row + li_row, -jnp.inf)
    a_col = b_col + m_prev
    g_col = b_last - b_col + li_col
    m_new = jnp.maximum(b_last + m_prev, jnp.max(g_col, axis=0, keepdims=True))
    w_col = jnp.exp(g_col - m_new)
    decay = jnp.exp(b_last + m_prev - m_new)
    return tri, dmat, a_col, m_new, w_col, decay


def _qkv_specs(row_of):
    base = C_QKV // DK
    q_spec = pl.BlockSpec((LCH, DK), lambda c, h: (row_of(c), base + (HEAD_W // DK) * h))
    k_spec = pl.BlockSpec((LCH, DK), lambda c, h: (row_of(c), base + (HEAD_W // DK) * h + 1))
    v_spec = pl.BlockSpec((LCH, DV), lambda c, h: (row_of(c), C_QKV // DV + (HEAD_W // DV) * h + 1))
    return q_spec, k_spec, v_spec


def _lane_put(col, lane_id, width=GATE_W):
    lane = lax.broadcasted_iota(jnp.int32, (col.shape[0], width), 1)
    return jnp.where(lane == lane_id, col, 0.0)


def _lane_get(block, lane_id):
    lane = lax.broadcasted_iota(jnp.int32, block.shape, 1)
    return jnp.sum(jnp.where(lane == lane_id, block, 0.0), axis=1, keepdims=True)


def _mlstm_fwd(proj, gates, bias):
    n = proj.shape[0]
    nc = n // LCH

    def body(q_ref, k_ref, v_ref, g_ref, bias_ref,
             h_ref, st_ref, cs_ref, ns_ref, ms_ref, c_scr, n_scr, m_scr):
        c = pl.program_id(0)
        hd = pl.program_id(1)

        @pl.when(c == 0)
        def _():
            c_scr[hd] = jnp.zeros((DK, DV), F32)
            n_scr[hd] = jnp.zeros((1, DK), F32)
            m_scr[hd] = jnp.full((1, GATE_W), -jnp.inf, F32)

        @pl.when(hd == 0)
        def _():
            st_ref[...] = jnp.zeros_like(st_ref)

        g = g_ref[...] + bias_ref[...]
        li_col, fr_col, li_row, fr_row = _gate_vectors(g, hd)
        m_all = m_scr[hd]
        m_prev = m_all[0:1, 0:1]
        tri, dmat, a_col, m_new, w_col, decay = _chunk_decay(
            li_col, li_row, _logsig(fr_col), _logsig(fr_row), m_prev)
        m_col = jnp.maximum(a_col, jnp.max(dmat, axis=1, keepdims=True))
        dl = jnp.exp(dmat - m_col)
        inter = jnp.exp(a_col - m_col)

        qs = q_ref[...] * (DK ** -0.5)
        kk = k_ref[...]
        vv = v_ref[...]
        cst = c_scr[hd]
        nst = n_scr[hd]
        cs_ref[...] = cst
        ns_ref[...] = nst
        ms_ref[...] = m_all

        sc = _dot_nt(qs, kk) * dl
        num = _dot(sc.astype(MXU), vv) + inter * _dot(qs, cst.astype(MXU))
        den = (jnp.sum(sc, axis=1, keepdims=True)
               + inter * jnp.sum(qs.astype(F32) * nst, axis=1, keepdims=True))
        nrm = jnp.maximum(jnp.abs(den), jnp.exp(-m_col))
        h_ref[...] = num / nrm
        st_ref[...] += _lane_put(den, hd) + _lane_put(m_col, NH + hd)

        kw = kk.astype(F32) * w_col
        c_scr[hd] = decay * cst + _dot_tn(kw.astype(MXU), vv)
        n_scr[hd] = decay * nst + jnp.sum(kw, axis=0, keepdims=True)
        m_scr[hd] = jnp.broadcast_to(m_new, (1, GATE_W))

    q_spec, k_spec, v_spec = _qkv_specs(lambda c: c)
    return pl.pallas_call(
        body, name="mlstm_fwd", grid=(nc, NH),
        in_specs=[q_spec, k_spec, v_spec,
                  pl.BlockSpec((LCH, GATE_W), lambda c, h: (c, 0)),
                  pl.BlockSpec((1, GATE_W), lambda c, h: (0, 0))],
        out_specs=[pl.BlockSpec((LCH, DV), lambda c, h: (c, h)),
                   pl.BlockSpec((LCH, GATE_W), lambda c, h: (c, 0)),
                   pl.BlockSpec((None, None, DK, DV), lambda c, h: (h, c, 0, 0)),
                   pl.BlockSpec((None, None, 1, DK), lambda c, h: (h, c, 0, 0)),
                   pl.BlockSpec((None, None, 1, GATE_W), lambda c, h: (h, c, 0, 0))],
        out_shape=[jax.ShapeDtypeStruct((n, VD), F32),
                   jax.ShapeDtypeStruct((n, GATE_W), F32),
                   jax.ShapeDtypeStruct((NH, nc, DK, DV), F32),
                   jax.ShapeDtypeStruct((NH, nc, 1, DK), F32),
                   jax.ShapeDtypeStruct((NH, nc, 1, GATE_W), F32)],
        scratch_shapes=[pltpu.VMEM((NH, DK, DV), F32), pltpu.VMEM((NH, 1, DK), F32),
                        pltpu.VMEM((NH, 1, GATE_W), F32)],
        compiler_params=_cparams("arbitrary", "arbitrary"),
    )(proj, proj, proj, gates, bias)


def _mlstm_bwd(proj, gates, bias, h, dh, stats, cs, ns, ms, dproj):
    n = proj.shape[0]
    nc = n // LCH

    def body(q_ref, k_ref, v_ref, g_ref, bias_ref, h_ref, dh_ref, st_ref,
             cs_ref, ns_ref, ms_ref, dproj_in,
             dqkv_ref, dg_ref, gsum_ref, dc_scr, dn_scr):
        del dproj_in
        step = pl.program_id(0)
        hd = pl.program_id(1)

        @pl.when(step == 0)
        def _():
            dc_scr[hd] = jnp.zeros((DK, DV), F32)
            dn_scr[hd] = jnp.zeros((1, DK), F32)

        @pl.when((step == 0) & (hd == 0))
        def _():
            gsum_ref[...] = jnp.zeros_like(gsum_ref)

        @pl.when(hd == 0)
        def _():
            dg_ref[...] = jnp.zeros_like(dg_ref)

        g = g_ref[...] + bias_ref[...]
        li_col, fr_col, li_row, fr_row = _gate_vectors(g, hd)
        m_prev = ms_ref[0:1, 0:1]
        tri, dmat, a_col, m_new, w_col, decay = _chunk_decay(
            li_col, li_row, _logsig(fr_col), _logsig(fr_row), m_prev)
        stats = st_ref[...]
        m_col = _lane_get(stats, NH + hd)
        dl = jnp.exp(dmat - m_col)
        inter = jnp.exp(a_col - m_col)

        qs = q_ref[...] * (DK ** -0.5)
        kk = k_ref[...]
        vv = v_ref[...]
        qf = qs.astype(F32)
        kf = kk.astype(F32)
        cst = cs_ref[...]
        nst = ns_ref[...]
        cb = cst.astype(MXU)
        dcn = dc_scr[hd]
        dnn = dn_scr[hd]
        dcb = dcn.astype(MXU)

        den = _lane_get(stats, hd)
        floor = jnp.exp(-m_col)
        nrm = jnp.maximum(jnp.abs(den), floor)
        dhv = dh_ref[...]
        dnum = dhv / nrm
        dnum_b = dnum.astype(MXU)
        dhh = jnp.sum(dhv * h_ref[...], axis=1, keepdims=True)
        dden = jnp.where(jnp.abs(den) > floor, -dhh / nrm * jnp.sign(den), 0.0)

        sc = _dot_nt(qs, kk) * dl
        dsc = _dot_nt(dnum_b, vv) + dden
        da = (dl * dsc).astype(MXU)
        gmat = sc * dsc

        dq = _dot(da, kk) + inter * (_dot_nt(dnum_b, cb) + dden * nst)
        dk_state = w_col * (_dot_nt(vv, dcb) + dnn)
        dk = _dot_tn(da, qs) + dk_state
        kw = (kf * w_col).astype(MXU)
        dv = _dot_tn(sc.astype(MXU), dnum_b) + _dot(kw, dcb)
        dqkv_ref[:, 0:DK] = (dq * (DK ** -0.5)).astype(dqkv_ref.dtype)
        dqkv_ref[:, DK:2 * DK] = dk.astype(dqkv_ref.dtype)
        dqkv_ref[:, 2 * DK:HEAD_W] = dv.astype(dqkv_ref.dtype)

        num_i = _dot(qs, cb)
        den_i = jnp.sum(qf * nst, axis=1, keepdims=True)
        e_col = inter * (jnp.sum(dnum * num_i, axis=1, keepdims=True) + dden * den_i)
        h_col = jnp.sum(kf * dk_state, axis=1, keepdims=True)
        f_dec = decay * (jnp.sum(jnp.sum(cst * dcn, axis=1, keepdims=True), axis=0, keepdims=True)
                         + jnp.sum(nst * dnn, axis=1, keepdims=True))
        r = lax.broadcasted_iota(jnp.int32, (LCH, LCH), 0)
        c = lax.broadcasted_iota(jnp.int32, (LCH, LCH), 1)
        eye = r == c
        to_col = lambda row: jnp.sum(jnp.where(eye, row, 0.0), axis=1, keepdims=True)
        row_g = jnp.sum(gmat, axis=1, keepdims=True)
        col_g = to_col(jnp.sum(gmat, axis=0, keepdims=True))
        last = lax.broadcasted_iota(jnp.int32, (LCH, 1), 0) == LCH - 1
        db_col = row_g - col_g + e_col - h_col
        db_col = db_col + jnp.where(last, jnp.sum(h_col, axis=0, keepdims=True) + f_dec, 0.0)
        dli_col = col_g + h_col
        dlf_row = jnp.sum(jnp.where(tri, db_col, 0.0), axis=0, keepdims=True)
        df_col = to_col(dlf_row) * _sigmoid(-fr_col)
        dg = _lane_put(dli_col, hd) + _lane_put(df_col, NH + hd)
        dg_ref[...] += dg
        gsum_ref[0:1, 0:GATE_W] += jnp.sum(dg, axis=0, keepdims=True)

        dc_scr[hd] = decay * dcn + _dot_tn((qf * inter).astype(MXU), dnum_b)
        dn_scr[hd] = decay * dnn + jnp.sum(qf * (inter * dden), axis=0, keepdims=True)

    rev = lambda c: nc - 1 - c
    q_spec, k_spec, v_spec = _qkv_specs(rev)
    hv_spec = pl.BlockSpec((LCH, DV), lambda c, h: (rev(c), h))
    gate_spec = pl.BlockSpec((LCH, GATE_W), lambda c, h: (rev(c), 0))
    return pl.pallas_call(
        body, name="mlstm_bwd", grid=(nc, NH),
        in_specs=[q_spec, k_spec, v_spec, gate_spec,
                  pl.BlockSpec((1, GATE_W), lambda c, h: (0, 0)),
                  hv_spec, hv_spec, gate_spec,
                  pl.BlockSpec((None, None, DK, DV), lambda c, h: (h, rev(c), 0, 0)),
                  pl.BlockSpec((None, None, 1, DK), lambda c, h: (h, rev(c), 0, 0)),
                  pl.BlockSpec((None, None, 1, GATE_W), lambda c, h: (h, rev(c), 0, 0)),
                  ANY],
        out_specs=[pl.BlockSpec((LCH, HEAD_W), lambda c, h: (rev(c), C_QKV // HEAD_W + h)),
                   gate_spec,
                   pl.BlockSpec((8, D), lambda c, h: (0, 0))],
        out_shape=[jax.ShapeDtypeStruct(dproj.shape, dproj.dtype),
                   jax.ShapeDtypeStruct((n, GATE_W), F32),
                   jax.ShapeDtypeStruct((8, D), F32)],
        scratch_shapes=[pltpu.VMEM((NH, DK, DV), F32), pltpu.VMEM((NH, 1, DK), F32)],
        input_output_aliases={11: 0},
        compiler_params=_cparams("arbitrary", "arbitrary"),
    )(proj, proj, proj, gates, bias, h, dh, stats, cs, ns, ms, dproj)


def _proj_spec(tm, col, width):
    return pl.BlockSpec((tm, width), lambda i: (i, col // width))


def _halo_prev(tm, col):
    return pl.BlockSpec((8, 1024), lambda i: (jnp.maximum(i * (tm // 8) - 1, 0), col // 1024))


def _const(shape):
    return pl.BlockSpec(shape, lambda i: (0,) * len(shape))


def _conv_inputs(i, tm, xa_ref, ca_ref, xah_ref, cah_ref):
    u = _f32(xa_ref) * _f32(ca_ref)
    uh = jnp.where(i > 0, _f32(xah_ref) * _f32(cah_ref), 0.0)
    rid = lax.broadcasted_iota(jnp.int32, u.shape, 0)
    u1 = jnp.where(rid == 0, uh[7:8, :], pltpu.roll(u, 1, 0))
    u2 = jnp.where(rid == 0, uh[6:7, :], jnp.where(rid == 1, uh[7:8, :], pltpu.roll(u, 2, 0)))
    return u, u1, u2


def _head_norm(hh, gh):
    out = []
    for j in range(NH):
        hj = hh[:, j * DV:(j + 1) * DV]
        rj = _rstd(hj)
        out.append((hj * rj, rj, gh[:, j * DV:(j + 1) * DV]))
    return out


def _tail_fwd(proj, h, x, p, t, cw, cb, gh, gple, gfin, wa, wb, wo, wpg, wp):
    n = x.shape[0]
    tm = min(256, n)

    def body(ba_r, za_r, o_r, zb_r, ga_r, gb_r, xa_r, ca_r, xah_r, cah_r, h_r, x_r, p_r, t_r,
             cw_r, cb_r, gh_r, gple_r, gfin_r, wa_r, wb_r, wo_r, wpg_r, wp_r,
             apre_o, bpre_o, mg_o, xn1_o, de_o, dgp_o, ya_o, yb_o, x1_o, dx2_o, acc_o):
        i = pl.program_id(0)

        @pl.when(i == 0)
        def _():
            acc_o[...] = jnp.zeros_like(acc_o)

        u, u1, u2 = _conv_inputs(i, tm, xa_r, ca_r, xah_r, cah_r)
        cwv = cw_r[...]
        cv = cwv[0:1, :] * u2 + cwv[1:2, :] * u1 + cwv[2:3, :] * u + cb_r[...]
        za = _f32(za_r)
        a_pre = (_f32(ba_r) * cv * (za * _sigmoid(za))).astype(MXU)
        apre_o[...] = a_pre
        ya = _dot(a_pre, wa_r[...])

        hn = _head_norm(h_r[...], gh_r[...])
        hbn = jnp.concatenate([xh * g for xh, _, g in hn], axis=1)
        zb = _f32(zb_r)
        b_pre = (_sigmoid(_f32(o_r)) * hbn * (zb * _sigmoid(zb))).astype(MXU)
        bpre_o[...] = b_pre
        yb = _dot(b_pre, wb_r[...])
        ya_o[...] = ya.astype(MXU)
        yb_o[...] = yb.astype(MXU)

        mg = (_sigmoid(_f32(ga_r)) * ya + _sigmoid(_f32(gb_r)) * yb).astype(MXU)
        mg_o[...] = mg
        x1 = x_r[...] + _dot(mg, wo_r[...])
        x1_o[...] = x1
        xn1 = (x1 * _rstd(x1) * gple_r[...]).astype(MXU)
        xn1_o[...] = xn1
        gt = _sigmoid(_dot(xn1, wpg_r[...]))
        e = _dot(p_r[...].astype(MXU), wp_r[...])
        x2 = x1 + gt * e
        r2 = _rstd(x2)
        xh2 = x2 * r2
        gf = gfin_r[...]
        diff = xh2 * gf - t_r[...]
        dy = diff * (1.0 / D)
        dx2 = _norm_bwd(dy, xh2, r2, gf)
        dx2_o[...] = dx2
        de_o[...] = (dx2 * gt).astype(MXU)
        dgp_o[...] = (dx2 * e * gt * (1.0 - gt)).astype(MXU)
        acc_o[0:1, :] += jnp.sum(dy * xh2, axis=0, keepdims=True)
        loss = 0.5 * jnp.sum(jnp.sum(diff * diff, axis=1, keepdims=True), axis=0, keepdims=True) * (1.0 / D)
        acc_o[1:2, :] += jnp.broadcast_to(loss, (1, D))

    row = lambda w, dt: (pl.BlockSpec((tm, w), lambda i: (i, 0)), jax.ShapeDtypeStruct((n, w), dt))
    outs = [row(D, MXU), row(VD, MXU), row(D, MXU), row(D, MXU), row(D, MXU), row(D, MXU),
            row(D, MXU), row(D, MXU), row(D, F32), row(D, F32),
            (_const((8, D)), jax.ShapeDtypeStruct((8, D), F32))]
    return pl.pallas_call(
        body, name="tail_fwd", grid=(n // tm,),
        in_specs=[_proj_spec(tm, C_BA, 1024), _proj_spec(tm, C_ZA, 1024),
                  _proj_spec(tm, C_O, 2048), _proj_spec(tm, C_ZB, 2048),
                  _proj_spec(tm, C_GA, 1024), _proj_spec(tm, C_GB, 1024),
                  _proj_spec(tm, C_XA, 1024), _proj_spec(tm, C_CA, 1024),
                  _halo_prev(tm, C_XA), _halo_prev(tm, C_CA),
                  pl.BlockSpec((tm, VD), lambda i: (i, 0)),
                  pl.BlockSpec((tm, D), lambda i: (i, 0)),
                  pl.BlockSpec((tm, PLE), lambda i: (i, 0)),
                  pl.BlockSpec((tm, D), lambda i: (i, 0)),
                  _const((8, D)), _const((1, D)), _const((1, VD)), _const((1, D)), _const((1, D)),
                  _const((D, D)), _const((VD, D)), _const((D, D)), _const((D, D)), _const((PLE, D))],
        out_specs=[s for s, _ in outs],
        out_shape=[s for _, s in outs],
        compiler_params=_cparams("arbitrary"),
    )(*([proj] * 10), h, x, p, t, cw, cb, gh, gple, gfin, wa, wb, wo, wpg, wp)


def _tail_bwd(proj, h, dgp, dx2, x1, ya, yb, cw, cb, gh, gple, wpg, wo, wb, wa):
    n = x1.shape[0]
    tm = min(256, n)

    def body(ba_r, za_r, o_r, zb_r, ga_r, gb_r, xa_r, ca_r, xah_r, cah_r, h_r,
             dgp_r, dx2_r, x1_r, ya_r, yb_r, cw_r, cb_r, gh_r, gple_r,
             wpg_r, wo_r, wb_r, wa_r,
             dproj_o, dcv_o, dh_o, dx1_o, dx1b_o, dya_o, dyb_o, acc_o):
        i = pl.program_id(0)

        @pl.when(i == 0)
        def _():
            acc_o[...] = jnp.zeros_like(acc_o)

        dxn1 = _dot_nt(dgp_r[...], wpg_r[...])
        x1 = x1_r[...]
        r1 = _rstd(x1)
        xh1 = x1 * r1
        acc_o[0:1, 0:D] += jnp.sum(dxn1 * xh1, axis=0, keepdims=True)
        dx1 = dx2_r[...] + _norm_bwd(dxn1, xh1, r1, gple_r[...])
        dx1_o[...] = dx1
        dx1b = dx1.astype(MXU)
        dx1b_o[...] = dx1b

        dmg = _dot_nt(dx1b, wo_r[...])
        sga = _sigmoid(_f32(ga_r))
        sgb = _sigmoid(_f32(gb_r))
        dya = (dmg * sga).astype(MXU)
        dyb = (dmg * sgb).astype(MXU)
        dya_o[...] = dya
        dyb_o[...] = dyb
        dproj_o[:, C_GA:C_GA + D] = (dmg * _f32(ya_r) * sga * (1.0 - sga)).astype(MXU)
        dproj_o[:, C_GB:C_GB + D] = (dmg * _f32(yb_r) * sgb * (1.0 - sgb)).astype(MXU)

        db_pre = _dot_nt(dyb, wb_r[...])
        hn = _head_norm(h_r[...], gh_r[...])
        hbn = jnp.concatenate([xh * g for xh, _, g in hn], axis=1)
        so = _sigmoid(_f32(o_r))
        zb = _f32(zb_r)
        szb = _sigmoid(zb)
        sb = zb * szb
        t1 = db_pre * hbn
        dproj_o[:, C_O:C_O + VD] = (t1 * (sb * (so - so * so))).astype(MXU)
        dproj_o[:, C_ZB:C_ZB + VD] = (t1 * so * (szb + sb - sb * szb)).astype(MXU)
        dhbn = db_pre * so * sb
        for j, (xh, rj, g) in enumerate(hn):
            dj = dhbn[:, j * DV:(j + 1) * DV]
            acc_o[1:2, j * DV:(j + 1) * DV] += jnp.sum(dj * xh, axis=0, keepdims=True)
            dh_o[:, j * DV:(j + 1) * DV] = _norm_bwd(dj, xh, rj, g)

        da_pre = _dot_nt(dya, wa_r[...])
        u, u1, u2 = _conv_inputs(i, tm, xa_r, ca_r, xah_r, cah_r)
        cwv = cw_r[...]
        cv = cwv[0:1, :] * u2 + cwv[1:2, :] * u1 + cwv[2:3, :] * u + cb_r[...]
        za = _f32(za_r)
        sza = _sigmoid(za)
        sa = za * sza
        ba = _f32(ba_r)
        t2 = da_pre * cv
        dproj_o[:, C_BA:C_BA + D] = (t2 * sa).astype(MXU)
        dproj_o[:, C_ZA:C_ZA + D] = (t2 * ba * (sza + sa - sa * sza)).astype(MXU)
        dcv = da_pre * ba * sa
        dcv_o[...] = dcv.astype(MXU)
        acc_o[2:3, 0:D] += jnp.sum(dcv, axis=0, keepdims=True)
        acc_o[3:4, 0:D] += jnp.sum(dcv * u2, axis=0, keepdims=True)
        acc_o[4:5, 0:D] += jnp.sum(dcv * u1, axis=0, keepdims=True)
        acc_o[5:6, 0:D] += jnp.sum(dcv * u, axis=0, keepdims=True)

    row = lambda w, dt: (pl.BlockSpec((tm, w), lambda i: (i, 0)), jax.ShapeDtypeStruct((n, w), dt))
    outs = [(pl.BlockSpec((tm, TAIL_W), lambda i: (i, 0)), jax.ShapeDtypeStruct((n, NMAIN), MXU)),
            row(D, MXU), row(VD, F32), row(D, F32), row(D, MXU), row(D, MXU), row(D, MXU),
            (_const((8, VD)), jax.ShapeDtypeStruct((8, VD), F32))]
    rowin = lambda w: pl.BlockSpec((tm, w), lambda i: (i, 0))
    return pl.pallas_call(
        body, name="tail_bwd", grid=(n // tm,),
        in_specs=[_proj_spec(tm, C_BA, 1024), _proj_spec(tm, C_ZA, 1024),
                  _proj_spec(tm, C_O, 2048), _proj_spec(tm, C_ZB, 2048),
                  _proj_spec(tm, C_GA, 1024), _proj_spec(tm, C_GB, 1024),
                  _proj_spec(tm, C_XA, 1024), _proj_spec(tm, C_CA, 1024),
                  _halo_prev(tm, C_XA), _halo_prev(tm, C_CA),
                  rowin(VD), rowin(D), rowin(D), rowin(D), rowin(D), rowin(D),
                  _const((8, D)), _const((1, D)), _const((1, VD)), _const((1, D)),
                  _const((D, D)), _const((D, D)), _const((VD, D)), _const((D, D))],
        out_specs=[s for s, _ in outs],
        out_shape=[s for _, s in outs],
        compiler_params=_cparams("arbitrary"),
    )(*([proj] * 10), h, dgp, dx2, x1, ya, yb, cw, cb, gh, gple, wpg, wo, wb, wa)


def _conv_bwd(proj, dcv, cw, dproj):
    n = dcv.shape[0]
    tm = min(512, n)
    nt = n // tm

    def body(xa_r, ca_r, dcv_r, nxt_r, cw_r, dproj_in, dxc_o):
        del dproj_in
        i = pl.program_id(0)
        dcv_v = _f32(dcv_r)
        nxt = jnp.where(i < nt - 1, _f32(nxt_r), 0.0)
        rid = lax.broadcasted_iota(jnp.int32, dcv_v.shape, 0)
        d1 = jnp.where(rid == tm - 1, nxt[0:1, :], pltpu.roll(dcv_v, tm - 1, 0))
        d2 = jnp.where(rid == tm - 2, nxt[0:1, :],
                       jnp.where(rid == tm - 1, nxt[1:2, :], pltpu.roll(dcv_v, tm - 2, 0)))
        cwv = cw_r[...]
        du = cwv[2:3, :] * dcv_v + cwv[1:2, :] * d1 + cwv[0:1, :] * d2
        dxc_o[:, 0:D] = (du * _f32(ca_r)).astype(MXU)
        dxc_o[:, D:2 * D] = (du * _f32(xa_r)).astype(MXU)

    return pl.pallas_call(
        body, name="conv_bwd", grid=(nt,),
        in_specs=[_proj_spec(tm, C_XA, 1024), _proj_spec(tm, C_CA, 1024),
                  pl.BlockSpec((tm, D), lambda i: (i, 0)),
                  pl.BlockSpec((8, D), lambda i: (jnp.minimum((i + 1) * (tm // 8), n // 8 - 1), 0)),
                  _const((8, D)), ANY],
        out_specs=pl.BlockSpec((tm, CONV_W), lambda i: (i, C_XA // CONV_W)),
        out_shape=jax.ShapeDtypeStruct(dproj.shape, dproj.dtype),
        input_output_aliases={5: 0},
        compiler_params=_cparams("arbitrary"),
    )(proj, proj, dcv, dcv, cw, dproj)


def _position():
    return lax.axis_index("x"), lax.axis_index("y"), lax.axis_index("c")


def _flip(v, bit):
    return 1 - v if bit else v


def _part_start(core, stride, tiled):
    return pl.multiple_of(core * stride, stride) if tiled else core * stride


def _scatter_copies(srcs, dsts, strides, send_sems, recv_sems, local_sems):
    x, y, c = _position()
    me = 4 * x + 2 * y + c
    na = len(srcs)
    copies = []
    for r in range(N_DEV):
        px, py, pc = _flip(x, (r >> 2) & 1), _flip(y, (r >> 1) & 1), _flip(c, r & 1)
        for a in range(na):
            rows = dsts[a].shape[1]
            src = srcs[a].at[2 * px + py, pl.ds(_part_start(pc, strides[a], len(dsts[a].shape) == 3), rows)]
            dst = dsts[a].at[me]
            if r == 0:
                copies.append(pltpu.make_async_copy(src, dst, local_sems.at[a]))
            else:
                k = (r - 1) * na + a
                copies.append(pltpu.make_async_remote_copy(
                    src_ref=src, dst_ref=dst, send_sem=send_sems.at[k], recv_sem=recv_sems.at[k],
                    device_id=(px, py, pc), device_id_type=MESH))
    return copies


def _input_grad(dproj, dgates, w_t, wg_t, x, dx1, g_mix, g_in, g_rest):
    n = x.shape[0]
    tm, tk = min(1024, n), 2048
    nk = NMAIN // tk
    nt = n // tm

    def body(dp_r, dg_r, w_r, wg_r, x_r, dx1_r, g_r, gin, grest,
             gx_o, acc_o, oin, orest, acc, send_sems, recv_sems, local_sems):
        i = pl.program_id(0)
        kk = pl.program_id(1)
        copies = _scatter_copies((gin, grest), (oin, orest), strides, send_sems, recv_sems, local_sems)

        @pl.when((i == 0) & (kk == 0))
        def _():
            acc_o[...] = jnp.zeros_like(acc_o)
            for cp in copies:
                cp.start()

        @pl.when(kk == 0)
        def _():
            acc[...] = _dot(dg_r[...].astype(MXU), wg_r[...])

        acc[...] += _dot(dp_r[...], w_r[...])

        @pl.when(kk == nk - 1)
        def _():
            dhn = acc[...]
            xv = x_r[...]
            r0 = _rstd(xv)
            xh = xv * r0
            acc_o[0:1, :] += jnp.sum(dhn * xh, axis=0, keepdims=True)
            gx_o[...] = dx1_r[...] + _norm_bwd(dhn, xh, r0, g_r[...])

        @pl.when((i == nt - 1) & (kk == nk - 1))
        def _():
            for cp in copies:
                cp.wait()

    nrem = 2 * (N_DEV - 1)
    r_in, r_rest = HALF_SLABS, g_rest.shape[1] // 2
    strides = (HALF_START, r_rest)
    return pl.pallas_call(
        body, name="input_grad", grid=(nt, nk),
        in_specs=[pl.BlockSpec((tm, tk), lambda i, kk: (i, kk)),
                  pl.BlockSpec((tm, GATE_W), lambda i, kk: (i, 0)),
                  pl.BlockSpec((tk, D), lambda i, kk: (kk, 0)),
                  pl.BlockSpec((GATE_W, D), lambda i, kk: (0, 0)),
                  pl.BlockSpec((tm, D), lambda i, kk: (i, 0)),
                  pl.BlockSpec((tm, D), lambda i, kk: (i, 0)),
                  pl.BlockSpec((1, D), lambda i, kk: (0, 0)),
                  ANY, ANY],
        out_specs=[pl.BlockSpec((tm, D), lambda i, kk: (i, 0)),
                   pl.BlockSpec((8, D), lambda i, kk: (0, 0)),
                   ANY, ANY],
        out_shape=[jax.ShapeDtypeStruct((n, D), F32), jax.ShapeDtypeStruct((8, D), F32),
                   jax.ShapeDtypeStruct((N_DEV, r_in) + g_in.shape[2:], g_in.dtype),
                   jax.ShapeDtypeStruct((N_DEV, r_rest) + g_rest.shape[2:], g_rest.dtype)],
        scratch_shapes=[pltpu.VMEM((tm, D), F32),
                        pltpu.SemaphoreType.DMA((nrem,)), pltpu.SemaphoreType.DMA((nrem,)),
                        pltpu.SemaphoreType.DMA((2,))],
        compiler_params=_cparams("arbitrary", "arbitrary"),
    )(dproj, dgates, w_t, wg_t, x, dx1, g_mix, g_in, g_rest)


def _pack_small(acc_f, acc_b, acc_x, gsum):
    def body(f_r, b_r, x_r, s_r, o_r):
        o_r[...] = jnp.zeros_like(o_r)
        o_r[0:1, :] = x_r[0:1, :]
        o_r[1:2, :] = b_r[2:3, 0:D]
        o_r[2:3, :] = b_r[1:2, 0:D]
        o_r[3:4, :] = b_r[1:2, D:2 * D]
        o_r[4:5, :] = b_r[0:1, 0:D]
        o_r[5:6, :] = f_r[0:1, :]
        o_r[6:9, :] = b_r[3:6, 0:D]
        lane = lax.broadcasted_iota(jnp.int32, (1, D), 1)
        o_r[9:10, :] = jnp.where(lane < 2 * NH, s_r[0:1, :], jnp.where(lane == 2 * NH, f_r[1:2, :], 0.0))

    return pl.pallas_call(
        body, name="pack_small",
        out_shape=jax.ShapeDtypeStruct((SMALL_ROWS, D), F32),
    )(acc_f, acc_b, acc_x, gsum)


def _sum_slots(r, name, tr=64, half=None):
    s, rows, w = r.shape
    tr = min(tr, rows)
    assert rows % tr == 0
    nt = rows // tr

    def body(*refs):
        r_ref, o_ref = refs[-2:]
        tot = r_ref[0].astype(F32)
        for k in range(1, s):
            tot = tot + r_ref[k].astype(F32)
        o_ref[...] = tot

    if half is None:
        return pl.pallas_call(
            body, name=name, grid=(nt,),
            in_specs=[pl.BlockSpec((s, tr, w), lambda i: (0, i, 0))],
            out_specs=pl.BlockSpec((tr, w), lambda i: (i, 0)),
            out_shape=jax.ShapeDtypeStruct((rows, w), F32),
            compiler_params=_cparams("arbitrary"),
        )(r)
    return pl.pallas_call(
        body, name=name,
        grid_spec=pltpu.PrefetchScalarGridSpec(
            num_scalar_prefetch=1, grid=(nt,),
            in_specs=[pl.BlockSpec((s, tr, w), lambda i, hf: (0, i, 0))],
            out_specs=pl.BlockSpec((tr, w), lambda i, hf: (hf[0] * nt + i, 0))),
        out_shape=jax.ShapeDtypeStruct((2 * rows, w), F32),
        compiler_params=_cparams("arbitrary"),
    )(half, r)


def _sum_slabs(r, name, tr=69):
    s, rows = r.shape[:2]
    assert rows % tr == 0

    def body(r_ref, o_ref):
        tot = r_ref[0].astype(F32)
        for k in range(1, s):
            tot = tot + r_ref[k].astype(F32)
        o_ref[...] = tot

    return pl.pallas_call(
        body, name=name, grid=(rows // tr,),
        in_specs=[pl.BlockSpec((s, tr) + r.shape[2:], lambda i: (0, i, 0, 0))],
        out_specs=pl.BlockSpec((tr,) + r.shape[2:], lambda i: (i, 0, 0)),
        out_shape=jax.ShapeDtypeStruct(r.shape[1:], F32),
        compiler_params=_cparams("arbitrary"),
    )(r)


def _adamw_body(w_r, g_r, m_r, v_r, d_o, m_o, v_o):
    c1 = 1.0 - ADAM_B1 ** ADAM_STEP
    c2 = 1.0 - ADAM_B2 ** ADAM_STEP
    gv = g_r[...]
    mn = ADAM_B1 * m_r[...] + (1.0 - ADAM_B1) * gv
    vn = ADAM_B2 * v_r[...] + (1.0 - ADAM_B2) * (gv * gv)
    m_o[...] = mn
    v_o[...] = vn
    d_o[...] = -ADAM_LR * ((mn / c1) / (jnp.sqrt(vn / c2) + ADAM_EPS) + ADAM_WD * w_r[...])


def _adamw_slabs(w, g, m, v, name, tr):
    rows = w.shape[0]
    assert rows % tr == 0 and w.shape == g.shape

    def body(*refs):
        _adamw_body(*refs)

    spec = pl.BlockSpec((tr,) + w.shape[1:], lambda i: (i, 0, 0))
    shp = jax.ShapeDtypeStruct(w.shape, F32)
    return pl.pallas_call(
        body, name=name, grid=(rows // tr,),
        in_specs=[spec] * 4, out_specs=[spec] * 3, out_shape=[shp] * 3,
        compiler_params=_cparams("arbitrary"),
    )(w, g, m, v)


def _adamw(w, g, m, v, name, g_row0=0, tr=64):
    lead = w.ndim == 3
    rows, cols = w.shape[-2:]
    tr = min(tr, rows)
    assert rows % tr == 0 and g_row0 % tr == 0 and g.shape[1] == cols

    def body(*refs):
        _adamw_body(*refs)

    if lead:
        spec = pl.BlockSpec((None, tr, cols), lambda i: (0, i, 0))
    else:
        spec = pl.BlockSpec((tr, cols), lambda i: (i, 0))
    g_spec = pl.BlockSpec((tr, cols), lambda i: (g_row0 // tr + i, 0))
    shp = jax.ShapeDtypeStruct(w.shape, F32)
    return pl.pallas_call(
        body, name=name, grid=(rows // tr,),
        in_specs=[spec, g_spec, spec, spec], out_specs=[spec] * 3, out_shape=[shp] * 3,
        compiler_params=_cparams("arbitrary"),
    )(w, g, m, v)


def _allgather_chips(halved, whole):
    arrs = [a for a, _ in halved] + list(whole)
    parts = [p for _, p in halved]
    nh, na = len(halved), len(arrs)
    relations = ((1, 0), (0, 1), (1, 1))
    n_ici = 3 * na

    def body(*refs):
        ins, outs = refs[:na], refs[na:2 * na]
        send_sems, recv_sems = refs[2 * na:]
        x, y, c = _position()
        mine = 2 * x + y

        def rows(a, core, stride, size):
            return pl.ds(_part_start(core, stride, len(ins[a].shape) == 2), size)

        def remote(src, dst, k, to):
            return pltpu.make_async_remote_copy(src_ref=src, dst_ref=dst, send_sem=send_sems.at[k],
                                                recv_sem=recv_sems.at[k], device_id=to, device_id_type=MESH)

        first, passed = [], []
        for j, (rx, ry) in enumerate(relations):
            px, py = _flip(x, rx), _flip(y, ry)
            for a in range(na):
                if a < nh:
                    s1, n1, s2, n2 = parts[a]
                    first.append(remote(ins[a].at[rows(a, c, s1, n1)], outs[a].at[mine, rows(a, c, s1, n1)],
                                        j * na + a, (px, py, c)))
                    landed = outs[a].at[2 * px + py, rows(a, c, s2, n2)]
                    passed.append(remote(landed, landed, n_ici + j * nh + a, (x, y, 1 - c)))
                else:
                    first.append(remote(ins[a], outs[a].at[mine], j * na + a, (px, py, c)))
        for cp in first:
            cp.start()
        k = 0
        for j in range(3):
            for a in range(na):
                first[j * na + a].wait_recv()
                if a < nh:
                    passed[k].start()
                    k += 1
        for cp in passed:
            cp.wait_recv()
        for cp in first + passed:
            cp.wait_send()

    nsem = n_ici + 3 * nh
    outs = pl.pallas_call(
        body, name="allgather_weights",
        in_specs=[ANY] * na, out_specs=[ANY] * na,
        out_shape=[jax.ShapeDtypeStruct((N_CHIPS,) + a.shape, a.dtype) for a in arrs],
        scratch_shapes=[pltpu.SemaphoreType.DMA((nsem,)), pltpu.SemaphoreType.DMA((nsem,))],
    )(*arrs)
    chip = 2 * lax.axis_index("x") + lax.axis_index("y")
    return [lax.dynamic_update_slice(o, a[None], (chip,) + (0,) * a.ndim) for o, a in zip(outs, arrs)]


def _join_halves(piece, full_rest, small):
    def body(pc_ref, rest_in, sm, oin, orest, osm, send_sems, recv_sems, local_sem):
        del rest_in
        x, y, c = _position()
        me = 4 * x + 2 * y + c
        sib = (x, y, 1 - c)
        rows = orest.shape[0] // 2
        mine = orest.at[pl.ds(pl.multiple_of(c * rows, rows), rows)]
        copies = [
            pltpu.make_async_remote_copy(
                src_ref=pc_ref.at[pl.ds(c, HALF_START)], dst_ref=oin.at[pl.ds(c * HALF_SLABS, HALF_START)],
                send_sem=send_sems.at[0], recv_sem=recv_sems.at[0], device_id=sib, device_id_type=MESH),
            pltpu.make_async_remote_copy(
                src_ref=mine, dst_ref=mine, send_sem=send_sems.at[1], recv_sem=recv_sems.at[1],
                device_id=sib, device_id_type=MESH)]
        na = 2
        copies.append(pltpu.make_async_copy(sm, osm.at[me], local_sem))
        for r in range(1, N_DEV):
            to = (_flip(x, (r >> 2) & 1), _flip(y, (r >> 1) & 1), _flip(c, r & 1))
            copies.append(pltpu.make_async_remote_copy(
                src_ref=sm, dst_ref=osm.at[me], send_sem=send_sems.at[na + r - 1],
                recv_sem=recv_sems.at[na + r - 1], device_id=to, device_id_type=MESH))
        for cp in copies:
            cp.start()
        for cp in copies:
            cp.wait()

    nsem = 2 + N_DEV - 1
    return pl.pallas_call(
        body, name="join_halves",
        in_specs=[ANY] * 3, out_specs=[ANY] * 3,
        out_shape=[jax.ShapeDtypeStruct((SLABS_CHIP,) + piece.shape[1:], piece.dtype),
                   jax.ShapeDtypeStruct(full_rest.shape, full_rest.dtype),
                   jax.ShapeDtypeStruct((N_DEV,) + small.shape, small.dtype)],
        scratch_shapes=[pltpu.SemaphoreType.DMA((nsem,)), pltpu.SemaphoreType.DMA((nsem,)),
                        pltpu.SemaphoreType.DMA],
        input_output_aliases={1: 1},
    )(piece, full_rest, small)


def _to_internal(w_slabs):
    order = sorted(SEGMENTS, key=lambda s: s[2])
    main = jnp.concatenate([w_slabs[g0 // 2:(g0 + w) // 2] for g0, w, _ in order], axis=0)
    gate = w_slabs[GATE_COL // 2:GATE_COL // 2 + NH].reshape(2 * NH, D)
    return main.reshape(NMAIN, D), jnp.pad(gate, ((0, GATE_W - 2 * NH), (0, 0)))


def _to_global(main_t, gate_t):
    main = main_t.reshape(NMAIN // 2, 16, D // 8)
    parts = sorted([(g0, main[i0 // 2:(i0 + w) // 2]) for g0, w, i0 in SEGMENTS]
                   + [(GATE_COL, gate_t[0:2 * NH].reshape(NH, 16, D // 8))], key=lambda s: s[0])
    return jnp.concatenate([p for _, p in parts], axis=0)


def _pack_rows(wa, wb, wo, wpg, wp):
    return jnp.concatenate([wa, wb, wo, wpg, wp.reshape(PACK_ROWS[4], D)], axis=0)


def _pad_rows(a, rows=8):
    return jnp.pad(a, ((0, rows - a.shape[0]), (0, 0)))


def kernel(x, p, g_mix, w_in, conv_w, conv_b, w_a_out, b_gates, g_head, w_b_out, w_o, g_ple, w_ple_gate, w_ple, g_final, loss_target, m_g_mix, m_w_in, m_conv_w, m_conv_b, m_w_a_out, m_b_gates, m_g_head, m_w_b_out, m_w_o, m_g_ple, m_w_ple_gate, m_w_ple, m_g_final, v_g_mix, v_w_in, v_conv_w, v_conv_b, v_w_a_out, v_b_gates, v_g_head, v_w_b_out, v_w_o, v_g_ple, v_w_ple_gate, v_w_ple, v_g_final):
    chip = 2 * lax.axis_index("x") + lax.axis_index("y")
    xs, ps, ts = x[0], p[0, 0], loss_target[0]
    g_fin = g_final.reshape(1, D)

    pack_w = _pack_rows(w_a_out[0], w_b_out[0], w_o[0], w_ple_gate[0], w_ple[0])
    w_slabs = jnp.transpose(w_in, (2, 0, 1)).reshape(SLABS_CHIP, 16, D // 8).astype(MXU)
    half_rows = PACK_TOTAL // 2
    g_win, g_pack, g_cw = _allgather_chips(
        [(w_slabs, (HALF_START, HALF_SLABS, HALF_SLABS, HALF_START)),
         (pack_w.astype(MXU), (half_rows, half_rows, half_rows, half_rows))], [_pad_rows(conv_w[0])])
    w_t, wg_t = _to_internal(g_win.reshape(N_IN // 2, 16, D // 8))
    offs = [0, 256, 768, 1024, 1280, 1344]
    wa, wb, wo, wpg = [g_pack[:, offs[k]:offs[k + 1]].reshape(-1, D) for k in range(4)]
    wp = jnp.transpose(g_pack[:, offs[4]:offs[5]].reshape(N_CHIPS, PLE, PLE), (1, 0, 2)).reshape(PLE, D)
    cw = jnp.transpose(g_cw, (1, 0, 2)).reshape(8, D)

    bias = jnp.pad(b_gates, ((0, 0), (0, GATE_W - 2 * NH)))
    hn, gates = _prenorm(xs, g_mix, wg_t)
    proj = _proj(hn, w_t)
    h, stats, cs, ns, ms = _mlstm_fwd(proj, gates, bias)
    (a_pre, b_pre, mg, xn1, de, dgp, ya, yb, x1, dx2, acc_f) = _tail_fwd(
        proj, h, xs, ps, ts, cw, conv_b, g_head, g_ple, g_fin, wa, wb, wo, wpg, wp)
    dproj, dcv, dh, dx1, dx1b, dya, dyb, acc_b = _tail_bwd(
        proj, h, dgp, dx2, x1, ya, yb, cw, conv_b, g_head, g_ple, wpg, wo, wb, wa)
    dproj = _conv_bwd(proj, dcv, cw, dproj)
    dproj, dgates, gsum = _mlstm_bwd(proj, gates, bias, h, dh, stats, cs, ns, ms, dproj)
    d_main = _matmul_tn(dproj, hn, "dw_in", out_dtype=WIRE, tk=2048)
    d_gate = _matmul_tn(dgates, hn, "dw_gate", out_dtype=WIRE)
    d_wa = _matmul_tn(a_pre, dya, "dw_a_out", out_dtype=WIRE)
    d_wb = _matmul_tn(b_pre, dyb, "dw_b_out", out_dtype=WIRE)
    d_wo = _matmul_tn(mg, dx1b, "dw_o", out_dtype=WIRE)
    d_wpg = _matmul_tn(xn1, dgp, "dw_ple_gate", out_dtype=WIRE)
    d_wp = _matmul_tn(ps, de, "dw_ple", out_dtype=WIRE)

    g_in = _to_global(d_main, d_gate).reshape(N_CHIPS, SLABS_CHIP, 16, D // 8)
    d_wp_c = jnp.transpose(d_wp.reshape(PLE, N_CHIPS, PLE), (1, 0, 2)).reshape(N_CHIPS, PACK_ROWS[4], D)
    g_rest = jnp.concatenate(
        [d_wa.reshape(N_CHIPS, -1, D), d_wb.reshape(N_CHIPS, -1, D), d_wo.reshape(N_CHIPS, -1, D),
         d_wpg.reshape(N_CHIPS, -1, D), d_wp_c], axis=1)
    grad_x, acc_x, r_in, r_rest = _input_grad(dproj, dgates, w_t, wg_t, xs, dx1, g_mix, g_in, g_rest)
    small = _pack_small(acc_f, acc_b, acc_x, gsum)
    core = lax.axis_index("c").astype(jnp.int32)
    piece = _sum_slabs(r_in, "sum_w_in")
    gw_in, gw_rest, r_small = _join_halves(
        piece, _sum_slots(r_rest, "sum_rest", tr=96, half=core.reshape(1)), small)
    gw_in = lax.dynamic_update_slice(gw_in, piece, (core * HALF_START, 0, 0)).reshape(COLS_CHIP, 8, D // 8)
    gs = _sum_slots(r_small, "sum_small", tr=SMALL_ROWS)

    big = []
    row0 = 0
    for name, w, m, v in (("w_a_out", w_a_out, m_w_a_out, v_w_a_out), ("w_b_out", w_b_out, m_w_b_out, v_w_b_out),
                          ("w_o", w_o, m_w_o, v_w_o), ("w_ple_gate", w_ple_gate, m_w_ple_gate, v_w_ple_gate)):
        big.append((name, w, m, v, gw_rest, row0))
        row0 += w.shape[1]
    g_wp = gw_rest[row0:row0 + PACK_ROWS[4]].reshape(PLE, PLE)
    big.append(("w_ple", w_ple, m_w_ple, v_w_ple, g_wp, 0))
    upd = {name: _adamw(w, g, m, v, "adamw_" + name, g_row0=r0) for name, w, m, v, g, r0 in big}
    g_big = {name: (g if name == "w_ple" else g[r0:r0 + w.shape[1]])[None] for name, w, m, v, g, r0 in big}
    slabs = lambda a: jnp.transpose(a, (2, 0, 1)).reshape(COLS_CHIP, 8, D // 8)
    unslab = lambda a: jnp.transpose(a, (1, 2, 0)).reshape(1, D, COLS_CHIP)
    upd["w_in"] = [unslab(u) for u in _adamw_slabs(
        slabs(w_in), gw_in, slabs(m_w_in), slabs(v_w_in), "adamw_w_in", tr=326)]
    g_big["w_in"] = unslab(gw_in)

    lane = lax.broadcasted_iota(jnp.int32, (1, D), 1)
    g_small = jnp.concatenate([gs[0:6], jnp.where(lane < 2 * NH, gs[9:10], 0.0), jnp.zeros((1, D), F32)], axis=0)

    def small_pack(gm, cb_, bg, gh, gp, gf):
        return jnp.concatenate([gm, cb_, gh.reshape(2, D), gp, gf.reshape(1, D),
                                jnp.pad(bg, ((0, 0), (0, D - 2 * NH))), jnp.zeros((1, D), F32)], axis=0)

    ws = small_pack(g_mix, conv_b, b_gates, g_head, g_ple, g_final)
    ms_ = small_pack(m_g_mix, m_conv_b, m_b_gates, m_g_head, m_g_ple, m_g_final)
    vs = small_pack(v_g_mix, v_conv_b, v_b_gates, v_g_head, v_g_ple, v_g_final)
    upd_s = _adamw(ws, g_small, ms_, vs, "adamw_small")
    g_cw_mine = _pad_rows(lax.dynamic_slice(gs[6:9], (0, chip * PLE), (3, PLE)))
    upd_c = _adamw(_pad_rows(conv_w[0]), g_cw_mine, _pad_rows(m_conv_w[0]), _pad_rows(v_conv_w[0]),
                   "adamw_conv_w")

    def leaves(bigs, sm, cwv):
        return [sm[0:1], bigs["w_in"], cwv[0:3][None], sm[1:2], bigs["w_a_out"], sm[6:7, 0:2 * NH],
                sm[2:4].reshape(1, VD), bigs["w_b_out"], bigs["w_o"], sm[4:5], bigs["w_ple_gate"],
                bigs["w_ple"], sm[5]]

    loss = gs[9, 2 * NH]
    outs = [loss, grad_x[None]] + leaves(g_big, g_small, g_cw_mine)
    for k in range(3):
        outs += leaves({name: u[k] for name, u in upd.items()}, upd_s[k], upd_c[k])
    return tuple(outs)
```

```python
import jax
import jax.numpy as jnp
from jax import lax
from jax.experimental import pallas as pl
from jax.experimental.pallas import tpu as pltpu

F32 = jnp.float32
MXU = jnp.bfloat16
WIRE = jnp.bfloat16

D = 1024
NH, DK, DV = 4, 256, 512
VD = NH * DV
PLE = 256
LCH = 256
EPS = 1e-6
N_IN = 14344
NMAIN = 14336
GATE_W = 128
N_CHIPS, N_DEV = 4, 8

C_BA, C_ZA, C_O, C_ZB, C_GA, C_GB = 0, 1024, 2048, 4096, 6144, 7168
QK = NH * DK
C_Q, C_K, C_V, C_XA, C_CA = 8192, 9216, 10240, 12288, 13312
QKV_W = 2 * QK + VD
TAIL_W = 8192
CONV_W = 2048
SEGMENTS = (
    (0, 1024, C_XA), (1024, 1024, C_BA), (2048, 1024, C_CA), (3072, 1024, C_ZA),
    (4096, QKV_W, C_Q), (8192, 4096, C_O), (12296, 2048, C_GA),
)
GATE_COL = 12288

COLS_CHIP = N_IN // N_CHIPS
SLABS_CHIP = COLS_CHIP // 2
HALF_START = SLABS_CHIP // 2
HALF_SLABS = SLABS_CHIP - HALF_START

PACK_ROWS = (256, 512, 256, 256, 64)
PACK_TOTAL = sum(PACK_ROWS)
SMALL_ROWS = 16

ADAM_LR, ADAM_B1, ADAM_B2, ADAM_EPS, ADAM_WD, ADAM_STEP = 0.001, 0.9, 0.999, 1e-08, 0.01, 10

VMEM_LIMIT = 56 * 1024 * 1024
MESH = pl.DeviceIdType.MESH
ANY = pl.BlockSpec(memory_space=pl.ANY)


def _cparams(*sem):
    return pltpu.CompilerParams(dimension_semantics=sem, vmem_limit_bytes=VMEM_LIMIT)


def _dot(a, b):
    return jnp.dot(a, b, preferred_element_type=F32)


def _dot_nt(a, b):
    return lax.dot_general(a, b, (((1,), (1,)), ((), ())), preferred_element_type=F32)


def _dot_tn(a, b):
    return lax.dot_general(a, b, (((0,), (0,)), ((), ())), preferred_element_type=F32)


def _sigmoid(x):
    return 1.0 / (1.0 + jnp.exp(-x))


def _logsig(x):
    return jnp.minimum(x, 0.0) - jnp.log(1.0 + jnp.exp(-jnp.abs(x)))


def _rstd(x):
    return lax.rsqrt(jnp.mean(x * x, axis=-1, keepdims=True) + EPS)


def _norm_bwd(dy, xhat, r, g):
    dxh = dy * g
    return r * (dxh - xhat * jnp.mean(dxh * xhat, axis=-1, keepdims=True))


def _f32(ref):
    return ref[...].astype(F32)


def _prenorm(x, g_mix, wg_t):
    n = x.shape[0]
    tm = min(512, n)

    def body(x_ref, g_ref, wg_ref, hn_ref, gate_ref):
        xv = x_ref[...]
        hn = (xv * _rstd(xv) * g_ref[...]).astype(MXU)
        hn_ref[...] = hn
        gate_ref[...] = _dot_nt(hn, wg_ref[...])

    return pl.pallas_call(
        body, name="prenorm", grid=(n // tm,),
        in_specs=[pl.BlockSpec((tm, D), lambda i: (i, 0)),
                  pl.BlockSpec((1, D), lambda i: (0, 0)),
                  pl.BlockSpec((GATE_W, D), lambda i: (0, 0))],
        out_specs=[pl.BlockSpec((tm, D), lambda i: (i, 0)),
                   pl.BlockSpec((tm, GATE_W), lambda i: (i, 0))],
        out_shape=[jax.ShapeDtypeStruct((n, D), MXU), jax.ShapeDtypeStruct((n, GATE_W), F32)],
        compiler_params=_cparams("arbitrary"),
    )(x, g_mix, wg_t)


def _proj(hn, w_t, tm=2048, tn=1024):
    n, k = hn.shape
    m = w_t.shape[0]
    tm = min(tm, n)

    def body(a_ref, b_ref, o_ref):
        o_ref[...] = _dot_nt(a_ref[...], b_ref[...]).astype(MXU)

    return pl.pallas_call(
        body, name="proj", grid=(m // tn, n // tm),
        in_specs=[pl.BlockSpec((tm, k), lambda j, i: (i, 0)),
                  pl.BlockSpec((tn, k), lambda j, i: (j, 0))],
        out_specs=pl.BlockSpec((tm, tn), lambda j, i: (i, j)),
        out_shape=jax.ShapeDtypeStruct((n, m), MXU),
        compiler_params=_cparams("arbitrary", "arbitrary"),
    )(hn, w_t)


def _matmul_tn(a, b, name, out_dtype=F32, ta=1024, tb=1024, tk=1024):
    n, ka = a.shape
    kb = b.shape[1]
    ta, tb, tk = min(ta, ka), min(tb, kb), min(tk, n)
    nk = n // tk

    def body(a_ref, b_ref, o_ref, acc):
        kk = pl.program_id(2)

        @pl.when(kk == 0)
        def _():
            acc[...] = jnp.zeros_like(acc)

        acc[...] += _dot_tn(a_ref[...].astype(MXU), b_ref[...].astype(MXU))

        @pl.when(kk == nk - 1)
        def _():
            o_ref[...] = acc[...].astype(out_dtype)

    return pl.pallas_call(
        body, name=name, grid=(ka // ta, kb // tb, nk),
        in_specs=[pl.BlockSpec((tk, ta), lambda i, j, kk: (kk, i)),
                  pl.BlockSpec((tk, tb), lambda i, j, kk: (kk, j))],
        out_specs=pl.BlockSpec((ta, tb), lambda i, j, kk: (i, j)),
        out_shape=jax.ShapeDtypeStruct((ka, kb), out_dtype),
        scratch_shapes=[pltpu.VMEM((ta, tb), F32)],
        compiler_params=_cparams("arbitrary", "arbitrary", "arbitrary"),
    )(a, b)


def _gate_vectors(g, gt, hd):
    lane = lax.broadcasted_iota(jnp.int32, g.shape, 1)
    sub = lax.broadcasted_iota(jnp.int32, gt.shape, 0)
    col = lambda j: jnp.sum(jnp.where(lane == j, g, 0.0), axis=1, keepdims=True)
    row = lambda j: jnp.sum(jnp.where(sub == j, gt, 0.0), axis=0, keepdims=True)
    return col(hd), col(hd + NH), row(hd), row(hd + NH)


def _chunk_decay(li_col, li_row, lf_col, lf_row, m_prev):
    n = li_col.shape[0]
    r = lax.broadcasted_iota(jnp.int32, (n, n), 0)
    c = lax.broadcasted_iota(jnp.int32, (n, n), 1)
    tri = r >= c
    b_col = jnp.sum(jnp.where(tri, lf_row, 0.0), axis=1, keepdims=True)
    b_row = jnp.sum(jnp.where(r <= c, lf_col, 0.0), axis=0, keepdims=True)
    b_last = jnp.sum(lf_row, axis=1, keepdims=True)
    dmat = jnp.where(tri, b_col - b_row + li_row, -jnp.inf)
    a_col = b_col + m_prev
    g_col = b_last - b_col + li_col
    m_new = jnp.maximum(b_last + m_prev, jnp.max(g_col, axis=0, keepdims=True))
    w_col = jnp.exp(g_col - m_new)
    decay = jnp.exp(b_last + m_prev - m_new)
    return tri, dmat, a_col, m_new, w_col, decay


def _qkv_specs(row_of):
    q_spec = pl.BlockSpec((LCH, QK), lambda c: (row_of(c), C_Q // QK))
    k_spec = pl.BlockSpec((LCH, QK), lambda c: (row_of(c), C_K // QK))
    v_spec = pl.BlockSpec((LCH, VD), lambda c: (row_of(c), C_V // VD))
    return q_spec, k_spec, v_spec


def _state_specs(row_of):
    return [pl.BlockSpec((NH, None, DK, DV), lambda c: (0, row_of(c), 0, 0)),
            pl.BlockSpec((NH, None, 1, DK), lambda c: (0, row_of(c), 0, 0)),
            pl.BlockSpec((NH, None, 1, GATE_W), lambda c: (0, row_of(c), 0, 0))]


def _lane_put(col, lane_id, width=GATE_W):
    lane = lax.broadcasted_iota(jnp.int32, (col.shape[0], width), 1)
    return jnp.where(lane == lane_id, col, 0.0)


def _lane_get(block, lane_id):
    lane = lax.broadcasted_iota(jnp.int32, block.shape, 1)
    return jnp.sum(jnp.where(lane == lane_id, block, 0.0), axis=1, keepdims=True)


def _mlstm_fwd(proj, gates, bias):
    n = proj.shape[0]
    nc = n // LCH

    def body(q_ref, k_ref, v_ref, g_ref, bias_ref,
             h_ref, st_ref, cs_ref, ns_ref, ms_ref, c_scr, n_scr, m_scr):
        @pl.when(pl.program_id(0) == 0)
        def _():
            c_scr[...] = jnp.zeros_like(c_scr)
            n_scr[...] = jnp.zeros_like(n_scr)
            m_scr[...] = jnp.full_like(m_scr, -jnp.inf)

        g = g_ref[...] + bias_ref[...]
        gt = g.T[0:8, :]
        stats = jnp.zeros((LCH, GATE_W), F32)
        for hd in range(NH):
            li_col, fr_col, li_row, fr_row = _gate_vectors(g, gt, hd)
            m_all = m_scr[hd]
            m_prev = m_all[0:1, 0:1]
            tri, dmat, a_col, m_new, w_col, decay = _chunk_decay(
                li_col, li_row, _logsig(fr_col), _logsig(fr_row), m_prev)
            m_col = jnp.maximum(a_col, jnp.max(dmat, axis=1, keepdims=True))
            dl = jnp.exp(dmat - m_col)
            inter = jnp.exp(a_col - m_col)

            qs = q_ref[:, hd * DK:(hd + 1) * DK] * (DK ** -0.5)
            kk = k_ref[:, hd * DK:(hd + 1) * DK]
            vv = v_ref[:, hd * DV:(hd + 1) * DV]
            cst = c_scr[hd]
            nst = n_scr[hd]
            cs_ref[hd] = cst
            ns_ref[hd] = nst
            ms_ref[hd] = m_all

            sc = _dot_nt(qs, kk) * dl
            num = _dot(sc.astype(MXU), vv) + inter * _dot(qs, cst.astype(MXU))
            den = (jnp.sum(sc, axis=1, keepdims=True)
                   + inter * jnp.sum(qs.astype(F32) * nst, axis=1, keepdims=True))
            nrm = jnp.maximum(jnp.abs(den), jnp.exp(-m_col))
            h_ref[:, hd * DV:(hd + 1) * DV] = num / nrm
            stats = stats + _lane_put(den, hd) + _lane_put(m_col, NH + hd)

            kw = kk.astype(F32) * w_col
            c_scr[hd] = decay * cst + _dot_tn(kw.astype(MXU), vv)
            n_scr[hd] = decay * nst + jnp.sum(kw, axis=0, keepdims=True)
            m_scr[hd] = jnp.broadcast_to(m_new, (1, GATE_W))
        st_ref[...] = stats

    q_spec, k_spec, v_spec = _qkv_specs(lambda c: c)
    return pl.pallas_call(
        body, name="mlstm_fwd", grid=(nc,),
        in_specs=[q_spec, k_spec, v_spec,
                  pl.BlockSpec((LCH, GATE_W), lambda c: (c, 0)),
                  pl.BlockSpec((1, GATE_W), lambda c: (0, 0))],
        out_specs=[pl.BlockSpec((LCH, VD), lambda c: (c, 0)),
                   pl.BlockSpec((LCH, GATE_W), lambda c: (c, 0))] + _state_specs(lambda c: c),
        out_shape=[jax.ShapeDtypeStruct((n, VD), F32),
                   jax.ShapeDtypeStruct((n, GATE_W), F32),
                   jax.ShapeDtypeStruct((NH, nc, DK, DV), F32),
                   jax.ShapeDtypeStruct((NH, nc, 1, DK), F32),
                   jax.ShapeDtypeStruct((NH, nc, 1, GATE_W), F32)],
        scratch_shapes=[pltpu.VMEM((NH, DK, DV), F32), pltpu.VMEM((NH, 1, DK), F32),
                        pltpu.VMEM((NH, 1, GATE_W), F32)],
        compiler_params=_cparams("arbitrary"),
    )(proj, proj, proj, gates, bias)


def _mlstm_bwd(proj, gates, bias, h, dh, stats, cs, ns, ms, dproj):
    n = proj.shape[0]
    nc = n // LCH

    def body(q_ref, k_ref, v_ref, g_ref, bias_ref, h_ref, dh_ref, st_ref,
             cs_ref, ns_ref, ms_ref, dproj_in,
             dqkv_ref, dg_ref, gsum_ref, dc_scr, dn_scr):
        del dproj_in

        @pl.when(pl.program_id(0) == 0)
        def _():
            dc_scr[...] = jnp.zeros_like(dc_scr)
            dn_scr[...] = jnp.zeros_like(dn_scr)
            gsum_ref[...] = jnp.zeros_like(gsum_ref)

        g = g_ref[...] + bias_ref[...]
        gt = g.T[0:8, :]
        stats = st_ref[...]
        r = lax.broadcasted_iota(jnp.int32, (LCH, LCH), 0)
        c = lax.broadcasted_iota(jnp.int32, (LCH, LCH), 1)
        eye = r == c
        to_col = lambda row: jnp.sum(jnp.where(eye, row, 0.0), axis=1, keepdims=True)
        last = lax.broadcasted_iota(jnp.int32, (LCH, 1), 0) == LCH - 1
        dg = jnp.zeros((LCH, GATE_W), F32)
        for hd in range(NH):
            li_col, fr_col, li_row, fr_row = _gate_vectors(g, gt, hd)
            m_prev = ms_ref[hd][0:1, 0:1]
            tri, dmat, a_col, m_new, w_col, decay = _chunk_decay(
                li_col, li_row, _logsig(fr_col), _logsig(fr_row), m_prev)
            m_col = _lane_get(stats, NH + hd)
            dl = jnp.exp(dmat - m_col)
            inter = jnp.exp(a_col - m_col)

            qs = q_ref[:, hd * DK:(hd + 1) * DK] * (DK ** -0.5)
            kk = k_ref[:, hd * DK:(hd + 1) * DK]
            vv = v_ref[:, hd * DV:(hd + 1) * DV]
            qf = qs.astype(F32)
            kf = kk.astype(F32)
            cst = cs_ref[hd]
            nst = ns_ref[hd]
            cb = cst.astype(MXU)
            dcn = dc_scr[hd]
            dnn = dn_scr[hd]
            dcb = dcn.astype(MXU)

            den = _lane_get(stats, hd)
            floor = jnp.exp(-m_col)
            nrm = jnp.maximum(jnp.abs(den), floor)
            dhv = dh_ref[:, hd * DV:(hd + 1) * DV]
            dnum = dhv / nrm
            dnum_b = dnum.astype(MXU)
            dhh = jnp.sum(dhv * h_ref[:, hd * DV:(hd + 1) * DV], axis=1, keepdims=True)
            dden = jnp.where(jnp.abs(den) > floor, -dhh / nrm * jnp.sign(den), 0.0)

            sc = _dot_nt(qs, kk) * dl
            dsc = _dot_nt(dnum_b, vv) + dden
            da = (dl * dsc).astype(MXU)
            gmat = sc * dsc

            dq = _dot(da, kk) + inter * (_dot_nt(dnum_b, cb) + dden * nst)
            dk_state = w_col * (_dot_nt(vv, dcb) + dnn)
            dk = _dot_tn(da, qs) + dk_state
            kw = (kf * w_col).astype(MXU)
            dv = _dot_tn(sc.astype(MXU), dnum_b) + _dot(kw, dcb)
            dqkv_ref[:, hd * DK:(hd + 1) * DK] = (dq * (DK ** -0.5)).astype(dqkv_ref.dtype)
            dqkv_ref[:, QK + hd * DK:QK + (hd + 1) * DK] = dk.astype(dqkv_ref.dtype)
            dqkv_ref[:, 2 * QK + hd * DV:2 * QK + (hd + 1) * DV] = dv.astype(dqkv_ref.dtype)

            num_i = _dot(qs, cb)
            den_i = jnp.sum(qf * nst, axis=1, keepdims=True)
            e_col = inter * (jnp.sum(dnum * num_i, axis=1, keepdims=True) + dden * den_i)
            h_col = jnp.sum(kf * dk_state, axis=1, keepdims=True)
            f_dec = decay * (jnp.sum(jnp.sum(cst * dcn, axis=1, keepdims=True), axis=0, keepdims=True)
                             + jnp.sum(nst * dnn, axis=1, keepdims=True))
            row_g = jnp.sum(gmat, axis=1, keepdims=True)
            col_g = to_col(jnp.sum(gmat, axis=0, keepdims=True))
            db_col = row_g - col_g + e_col - h_col
            db_col = db_col + jnp.where(last, jnp.sum(h_col, axis=0, keepdims=True) + f_dec, 0.0)
            dli_col = col_g + h_col
            dlf_row = jnp.sum(jnp.where(tri, db_col, 0.0), axis=0, keepdims=True)
            df_col = to_col(dlf_row) * _sigmoid(-fr_col)
            dg = dg + _lane_put(dli_col, hd) + _lane_put(df_col, NH + hd)

            dc_scr[hd] = decay * dcn + _dot_tn((qf * inter).astype(MXU), dnum_b)
            dn_scr[hd] = decay * dnn + jnp.sum(qf * (inter * dden), axis=0, keepdims=True)
        dg_ref[...] = dg
        gsum_ref[0:1, 0:GATE_W] += jnp.sum(dg, axis=0, keepdims=True)

    rev = lambda c: nc - 1 - c
    q_spec, k_spec, v_spec = _qkv_specs(rev)
    hv_spec = pl.BlockSpec((LCH, VD), lambda c: (rev(c), 0))
    gate_spec = pl.BlockSpec((LCH, GATE_W), lambda c: (rev(c), 0))
    return pl.pallas_call(
        body, name="mlstm_bwd", grid=(nc,),
        in_specs=[q_spec, k_spec, v_spec, gate_spec,
                  pl.BlockSpec((1, GATE_W), lambda c: (0, 0)),
                  hv_spec, hv_spec, gate_spec] + _state_specs(rev) + [ANY],
        out_specs=[pl.BlockSpec((LCH, QKV_W), lambda c: (rev(c), C_Q // QKV_W)),
                   gate_spec,
                   pl.BlockSpec((8, D), lambda c: (0, 0))],
        out_shape=[jax.ShapeDtypeStruct(dproj.shape, dproj.dtype),
                   jax.ShapeDtypeStruct((n, GATE_W), F32),
                   jax.ShapeDtypeStruct((8, D), F32)],
        scratch_shapes=[pltpu.VMEM((NH, DK, DV), F32), pltpu.VMEM((NH, 1, DK), F32)],
        input_output_aliases={11: 0},
        compiler_params=_cparams("arbitrary"),
    )(proj, proj, proj, gates, bias, h, dh, stats, cs, ns, ms, dproj)


def _proj_spec(tm, col, width):
    return pl.BlockSpec((tm, width), lambda i: (i, col // width))


def _halo_prev(tm, col):
    return pl.BlockSpec((8, 1024), lambda i: (jnp.maximum(i * (tm // 8) - 1, 0), col // 1024))


def _const(shape):
    return pl.BlockSpec(shape, lambda i: (0,) * len(shape))


def _conv_inputs(i, tm, xa_ref, ca_ref, xah_ref, cah_ref):
    u = _f32(xa_ref) * _f32(ca_ref)
    uh = jnp.where(i > 0, _f32(xah_ref) * _f32(cah_ref), 0.0)
    rid = lax.broadcasted_iota(jnp.int32, u.shape, 0)
    u1 = jnp.where(rid == 0, uh[7:8, :], pltpu.roll(u, 1, 0))
    u2 = jnp.where(rid == 0, uh[6:7, :], jnp.where(rid == 1, uh[7:8, :], pltpu.roll(u, 2, 0)))
    return u, u1, u2


def _head_norm(hh, gh):
    out = []
    for j in range(NH):
        hj = hh[:, j * DV:(j + 1) * DV]
        rj = _rstd(hj)
        out.append((hj * rj, rj, gh[:, j * DV:(j + 1) * DV]))
    return out


def _tail_fwd(proj, h, x, p, t, cw, cb, gh, gple, gfin, wa, wb, wo, wpg, wp):
    n = x.shape[0]
    tm = min(256, n)

    def body(ba_r, za_r, o_r, zb_r, ga_r, gb_r, xa_r, ca_r, xah_r, cah_r, h_r, x_r, p_r, t_r,
             cw_r, cb_r, gh_r, gple_r, gfin_r, wa_r, wb_r, wo_r, wpg_r, wp_r,
             apre_o, bpre_o, mg_o, xn1_o, de_o, dgp_o, ya_o, yb_o, x1_o, dx2_o, acc_o):
        i = pl.program_id(0)

        @pl.when(i == 0)
        def _():
            acc_o[...] = jnp.zeros_like(acc_o)

        u, u1, u2 = _conv_inputs(i, tm, xa_r, ca_r, xah_r, cah_r)
        cwv = cw_r[...]
        cv = cwv[0:1, :] * u2 + cwv[1:2, :] * u1 + cwv[2:3, :] * u + cb_r[...]
        za = _f32(za_r)
        a_pre = (_f32(ba_r) * cv * (za * _sigmoid(za))).astype(MXU)
        apre_o[...] = a_pre
        ya = _dot(a_pre, wa_r[...])

        hn = _head_norm(h_r[...], gh_r[...])
        hbn = jnp.concatenate([xh * g for xh, _, g in hn], axis=1)
        zb = _f32(zb_r)
        b_pre = (_sigmoid(_f32(o_r)) * hbn * (zb * _sigmoid(zb))).astype(MXU)
        bpre_o[...] = b_pre
        yb = _dot(b_pre, wb_r[...])
        ya_o[...] = ya.astype(MXU)
        yb_o[...] = yb.astype(MXU)

        mg = (_sigmoid(_f32(ga_r)) * ya + _sigmoid(_f32(gb_r)) * yb).astype(MXU)
        mg_o[...] = mg
        x1 = x_r[...] + _dot(mg, wo_r[...])
        x1_o[...] = x1
        xn1 = (x1 * _rstd(x1) * gple_r[...]).astype(MXU)
        xn1_o[...] = xn1
        gt = _sigmoid(_dot(xn1, wpg_r[...]))
        e = _dot(p_r[...].astype(MXU), wp_r[...])
        x2 = x1 + gt * e
        r2 = _rstd(x2)
        xh2 = x2 * r2
        gf = gfin_r[...]
        diff = xh2 * gf - t_r[...]
        dy = diff * (1.0 / D)
        dx2 = _norm_bwd(dy, xh2, r2, gf)
        dx2_o[...] = dx2
        de_o[...] = (dx2 * gt).astype(MXU)
        dgp_o[...] = (dx2 * e * gt * (1.0 - gt)).astype(MXU)
        acc_o[0:1, :] += jnp.sum(dy * xh2, axis=0, keepdims=True)
        loss = 0.5 * jnp.sum(jnp.sum(diff * diff, axis=1, keepdims=True), axis=0, keepdims=True) * (1.0 / D)
        acc_o[1:2, :] += jnp.broadcast_to(loss, (1, D))

    row = lambda w, dt: (pl.BlockSpec((tm, w), lambda i: (i, 0)), jax.ShapeDtypeStruct((n, w), dt))
    outs = [row(D, MXU), row(VD, MXU), row(D, MXU), row(D, MXU), row(D, MXU), row(D, MXU),
            row(D, MXU), row(D, MXU), row(D, F32), row(D, F32),
            (_const((8, D)), jax.ShapeDtypeStruct((8, D), F32))]
    return pl.pallas_call(
        body, name="tail_fwd", grid=(n // tm,),
        in_specs=[_proj_spec(tm, C_BA, 1024), _proj_spec(tm, C_ZA, 1024),
                  _proj_spec(tm, C_O, 2048), _proj_spec(tm, C_ZB, 2048),
                  _proj_spec(tm, C_GA, 1024), _proj_spec(tm, C_GB, 1024),
                  _proj_spec(tm, C_XA, 1024), _proj_spec(tm, C_CA, 1024),
                  _halo_prev(tm, C_XA), _halo_prev(tm, C_CA),
                  pl.BlockSpec((tm, VD), lambda i: (i, 0)),
                  pl.BlockSpec((tm, D), lambda i: (i, 0)),
                  pl.BlockSpec((tm, PLE), lambda i: (i, 0)),
                  pl.BlockSpec((tm, D), lambda i: (i, 0)),
                  _const((8, D)), _const((1, D)), _const((1, VD)), _const((1, D)), _const((1, D)),
                  _const((D, D)), _const((VD, D)), _const((D, D)), _const((D, D)), _const((PLE, D))],
        out_specs=[s for s, _ in outs],
        out_shape=[s for _, s in outs],
        compiler_params=_cparams("arbitrary"),
    )(*([proj] * 10), h, x, p, t, cw, cb, gh, gple, gfin, wa, wb, wo, wpg, wp)


def _tail_bwd(proj, h, dgp, dx2, x1, ya, yb, cw, cb, gh, gple, wpg, wo, wb, wa):
    n = x1.shape[0]
    tm = min(256, n)

    def body(ba_r, za_r, o_r, zb_r, ga_r, gb_r, xa_r, ca_r, xah_r, cah_r, h_r,
             dgp_r, dx2_r, x1_r, ya_r, yb_r, cw_r, cb_r, gh_r, gple_r,
             wpg_r, wo_r, wb_r, wa_r,
             dproj_o, dcv_o, dh_o, dx1_o, dx1b_o, dya_o, dyb_o, acc_o):
        i = pl.program_id(0)

        @pl.when(i == 0)
        def _():
            acc_o[...] = jnp.zeros_like(acc_o)

        dxn1 = _dot_nt(dgp_r[...], wpg_r[...])
        x1 = x1_r[...]
        r1 = _rstd(x1)
        xh1 = x1 * r1
        acc_o[0:1, 0:D] += jnp.sum(dxn1 * xh1, axis=0, keepdims=True)
        dx1 = dx2_r[...] + _norm_bwd(dxn1, xh1, r1, gple_r[...])
        dx1_o[...] = dx1
        dx1b = dx1.astype(MXU)
        dx1b_o[...] = dx1b

        dmg = _dot_nt(dx1b, wo_r[...])
        sga = _sigmoid(_f32(ga_r))
        sgb = _sigmoid(_f32(gb_r))
        dya = (dmg * sga).astype(MXU)
        dyb = (dmg * sgb).astype(MXU)
        dya_o[...] = dya
        dyb_o[...] = dyb
        dproj_o[:, C_GA:C_GA + D] = (dmg * _f32(ya_r) * sga * (1.0 - sga)).astype(MXU)
        dproj_o[:, C_GB:C_GB + D] = (dmg * _f32(yb_r) * sgb * (1.0 - sgb)).astype(MXU)

        db_pre = _dot_nt(dyb, wb_r[...])
        hn = _head_norm(h_r[...], gh_r[...])
        hbn = jnp.concatenate([xh * g for xh, _, g in hn], axis=1)
        so = _sigmoid(_f32(o_r))
        zb = _f32(zb_r)
        szb = _sigmoid(zb)
        sb = zb * szb
        t1 = db_pre * hbn
        dproj_o[:, C_O:C_O + VD] = (t1 * (sb * (so - so * so))).astype(MXU)
        dproj_o[:, C_ZB:C_ZB + VD] = (t1 * so * (szb + sb - sb * szb)).astype(MXU)
        dhbn = db_pre * so * sb
        for j, (xh, rj, g) in enumerate(hn):
            dj = dhbn[:, j * DV:(j + 1) * DV]
            acc_o[1:2, j * DV:(j + 1) * DV] += jnp.sum(dj * xh, axis=0, keepdims=True)
            dh_o[:, j * DV:(j + 1) * DV] = _norm_bwd(dj, xh, rj, g)

        da_pre = _dot_nt(dya, wa_r[...])
        u, u1, u2 = _conv_inputs(i, tm, xa_r, ca_r, xah_r, cah_r)
        cwv = cw_r[...]
        cv = cwv[0:1, :] * u2 + cwv[1:2, :] * u1 + cwv[2:3, :] * u + cb_r[...]
        za = _f32(za_r)
        sza = _sigmoid(za)
        sa = za * sza
        ba = _f32(ba_r)
        t2 = da_pre * cv
        dproj_o[:, C_BA:C_BA + D] = (t2 * sa).astype(MXU)
        dproj_o[:, C_ZA:C_ZA + D] = (t2 * ba * (sza + sa - sa * sza)).astype(MXU)
        dcv = da_pre * ba * sa
        dcv_o[...] = dcv.astype(MXU)
        acc_o[2:3, 0:D] += jnp.sum(dcv, axis=0, keepdims=True)
        acc_o[3:4, 0:D] += jnp.sum(dcv * u2, axis=0, keepdims=True)
        acc_o[4:5, 0:D] += jnp.sum(dcv * u1, axis=0, keepdims=True)
        acc_o[5:6, 0:D] += jnp.sum(dcv * u, axis=0, keepdims=True)

    row = lambda w, dt: (pl.BlockSpec((tm, w), lambda i: (i, 0)), jax.ShapeDtypeStruct((n, w), dt))
    outs = [(pl.BlockSpec((tm, TAIL_W), lambda i: (i, 0)), jax.ShapeDtypeStruct((n, NMAIN), MXU)),
            row(D, MXU), row(VD, F32), row(D, F32), row(D, MXU), row(D, MXU), row(D, MXU),
            (_const((8, VD)), jax.ShapeDtypeStruct((8, VD), F32))]
    rowin = lambda w: pl.BlockSpec((tm, w), lambda i: (i, 0))
    return pl.pallas_call(
        body, name="tail_bwd", grid=(n // tm,),
        in_specs=[_proj_spec(tm, C_BA, 1024), _proj_spec(tm, C_ZA, 1024),
                  _proj_spec(tm, C_O, 2048), _proj_spec(tm, C_ZB, 2048),
                  _proj_spec(tm, C_GA, 1024), _proj_spec(tm, C_GB, 1024),
                  _proj_spec(tm, C_XA, 1024), _proj_spec(tm, C_CA, 1024),
                  _halo_prev(tm, C_XA), _halo_prev(tm, C_CA),
                  rowin(VD), rowin(D), rowin(D), rowin(D), rowin(D), rowin(D),
                  _const((8, D)), _const((1, D)), _const((1, VD)), _const((1, D)),
                  _const((D, D)), _const((D, D)), _const((VD, D)), _const((D, D))],
        out_specs=[s for s, _ in outs],
        out_shape=[s for _, s in outs],
        compiler_params=_cparams("arbitrary"),
    )(*([proj] * 10), h, dgp, dx2, x1, ya, yb, cw, cb, gh, gple, wpg, wo, wb, wa)


def _conv_bwd(proj, dcv, cw, dproj):
    n = dcv.shape[0]
    tm = min(512, n)
    nt = n // tm

    def body(xa_r, ca_r, dcv_r, nxt_r, cw_r, dproj_in, dxc_o):
        del dproj_in
        i = pl.program_id(0)
        dcv_v = _f32(dcv_r)
        nxt = jnp.where(i < nt - 1, _f32(nxt_r), 0.0)
        rid = lax.broadcasted_iota(jnp.int32, dcv_v.shape, 0)
        d1 = jnp.where(rid == tm - 1, nxt[0:1, :], pltpu.roll(dcv_v, tm - 1, 0))
        d2 = jnp.where(rid == tm - 2, nxt[0:1, :],
                       jnp.where(rid == tm - 1, nxt[1:2, :], pltpu.roll(dcv_v, tm - 2, 0)))
        cwv = cw_r[...]
        du = cwv[2:3, :] * dcv_v + cwv[1:2, :] * d1 + cwv[0:1, :] * d2
        dxc_o[:, 0:D] = (du * _f32(ca_r)).astype(MXU)
        dxc_o[:, D:2 * D] = (du * _f32(xa_r)).astype(MXU)

    return pl.pallas_call(
        body, name="conv_bwd", grid=(nt,),
        in_specs=[_proj_spec(tm, C_XA, 1024), _proj_spec(tm, C_CA, 1024),
                  pl.BlockSpec((tm, D), lambda i: (i, 0)),
                  pl.BlockSpec((8, D), lambda i: (jnp.minimum((i + 1) * (tm // 8), n // 8 - 1), 0)),
                  _const((8, D)), ANY],
        out_specs=pl.BlockSpec((tm, CONV_W), lambda i: (i, C_XA // CONV_W)),
        out_shape=jax.ShapeDtypeStruct(dproj.shape, dproj.dtype),
        input_output_aliases={5: 0},
        compiler_params=_cparams("arbitrary"),
    )(proj, proj, dcv, dcv, cw, dproj)


def _position():
    return lax.axis_index("x"), lax.axis_index("y"), lax.axis_index("c")


def _flip(v, bit):
    return 1 - v if bit else v


def _part_start(core, stride, tiled):
    return pl.multiple_of(core * stride, stride) if tiled else core * stride


def _scatter_copies(srcs, dsts, strides, send_sems, recv_sems, local_sems):
    x, y, c = _position()
    me = 4 * x + 2 * y + c
    na = len(srcs)
    copies = []
    for r in range(N_DEV):
        px, py, pc = _flip(x, (r >> 2) & 1), _flip(y, (r >> 1) & 1), _flip(c, r & 1)
        for a in range(na):
            rows = dsts[a].shape[1]
            src = srcs[a].at[2 * px + py, pl.ds(_part_start(pc, strides[a], len(dsts[a].shape) == 3), rows)]
            dst = dsts[a].at[me]
            if r == 0:
                copies.append(pltpu.make_async_copy(src, dst, local_sems.at[a]))
            else:
                k = (r - 1) * na + a
                copies.append(pltpu.make_async_remote_copy(
                    src_ref=src, dst_ref=dst, send_sem=send_sems.at[k], recv_sem=recv_sems.at[k],
                    device_id=(px, py, pc), device_id_type=MESH))
    return copies


def _input_grad(dproj, dgates, w_t, wg_t, x, dx1, g_mix, g_in, g_rest):
    n = x.shape[0]
    tm, tk = min(1024, n), 2048
    nk = NMAIN // tk
    nt = n // tm

    def body(dp_r, dg_r, w_r, wg_r, x_r, dx1_r, g_r, gin, grest,
             gx_o, acc_o, oin, orest, acc, send_sems, recv_sems, local_sems):
        i = pl.program_id(0)
        kk = pl.program_id(1)
        copies = _scatter_copies((gin, grest), (oin, orest), strides, send_sems, recv_sems, local_sems)

        @pl.when((i == 0) & (kk == 0))
        def _():
            acc_o[...] = jnp.zeros_like(acc_o)
            for cp in copies:
                cp.start()

        @pl.when(kk == 0)
        def _():
            acc[...] = _dot(dg_r[...].astype(MXU), wg_r[...])

        acc[...] += _dot(dp_r[...], w_r[...])

        @pl.when(kk == nk - 1)
        def _():
            dhn = acc[...]
            xv = x_r[...]
            r0 = _rstd(xv)
            xh = xv * r0
            acc_o[0:1, :] += jnp.sum(dhn * xh, axis=0, keepdims=True)
            gx_o[...] = dx1_r[...] + _norm_bwd(dhn, xh, r0, g_r[...])

        @pl.when((i == nt - 1) & (kk == nk - 1))
        def _():
            for cp in copies:
                cp.wait()

    nrem = 2 * (N_DEV - 1)
    r_in, r_rest = HALF_SLABS, g_rest.shape[1] // 2
    strides = (HALF_START, r_rest)
    return pl.pallas_call(
        body, name="input_grad", grid=(nt, nk),
        in_specs=[pl.BlockSpec((tm, tk), lambda i, kk: (i, kk)),
                  pl.BlockSpec((tm, GATE_W), lambda i, kk: (i, 0)),
                  pl.BlockSpec((tk, D), lambda i, kk: (kk, 0)),
                  pl.BlockSpec((GATE_W, D), lambda i, kk: (0, 0)),
                  pl.BlockSpec((tm, D), lambda i, kk: (i, 0)),
                  pl.BlockSpec((tm, D), lambda i, kk: (i, 0)),
                  pl.BlockSpec((1, D), lambda i, kk: (0, 0)),
                  ANY, ANY],
        out_specs=[pl.BlockSpec((tm, D), lambda i, kk: (i, 0)),
                   pl.BlockSpec((8, D), lambda i, kk: (0, 0)),
                   ANY, ANY],
        out_shape=[jax.ShapeDtypeStruct((n, D), F32), jax.ShapeDtypeStruct((8, D), F32),
                   jax.ShapeDtypeStruct((N_DEV, r_in) + g_in.shape[2:], g_in.dtype),
                   jax.ShapeDtypeStruct((N_DEV, r_rest) + g_rest.shape[2:], g_rest.dtype)],
        scratch_shapes=[pltpu.VMEM((tm, D), F32),
                        pltpu.SemaphoreType.DMA((nrem,)), pltpu.SemaphoreType.DMA((nrem,)),
                        pltpu.SemaphoreType.DMA((2,))],
        compiler_params=_cparams("arbitrary", "arbitrary"),
    )(dproj, dgates, w_t, wg_t, x, dx1, g_mix, g_in, g_rest)


def _pack_small(acc_f, acc_b, acc_x, gsum):
    def body(f_r, b_r, x_r, s_r, o_r):
        o_r[...] = jnp.zeros_like(o_r)
        o_r[0:1, :] = x_r[0:1, :]
        o_r[1:2, :] = b_r[2:3, 0:D]
        o_r[2:3, :] = b_r[1:2, 0:D]
        o_r[3:4, :] = b_r[1:2, D:2 * D]
        o_r[4:5, :] = b_r[0:1, 0:D]
        o_r[5:6, :] = f_r[0:1, :]
        o_r[6:9, :] = b_r[3:6, 0:D]
        lane = lax.broadcasted_iota(jnp.int32, (1, D), 1)
        o_r[9:10, :] = jnp.where(lane < 2 * NH, s_r[0:1, :], jnp.where(lane == 2 * NH, f_r[1:2, :], 0.0))

    return pl.pallas_call(
        body, name="pack_small",
        out_shape=jax.ShapeDtypeStruct((SMALL_ROWS, D), F32),
    )(acc_f, acc_b, acc_x, gsum)


def _sum_slots(r, name, tr=64, half=None):
    s, rows, w = r.shape
    tr = min(tr, rows)
    assert rows % tr == 0
    nt = rows // tr

    def body(*refs):
        r_ref, o_ref = refs[-2:]
        tot = r_ref[0].astype(F32)
        for k in range(1, s):
            tot = tot + r_ref[k].astype(F32)
        o_ref[...] = tot

    if half is None:
        return pl.pallas_call(
            body, name=name, grid=(nt,),
            in_specs=[pl.BlockSpec((s, tr, w), lambda i: (0, i, 0))],
            out_specs=pl.BlockSpec((tr, w), lambda i: (i, 0)),
            out_shape=jax.ShapeDtypeStruct((rows, w), F32),
            compiler_params=_cparams("arbitrary"),
        )(r)
    return pl.pallas_call(
        body, name=name,
        grid_spec=pltpu.PrefetchScalarGridSpec(
            num_scalar_prefetch=1, grid=(nt,),
            in_specs=[pl.BlockSpec((s, tr, w), lambda i, hf: (0, i, 0))],
            out_specs=pl.BlockSpec((tr, w), lambda i, hf: (hf[0] * nt + i, 0))),
        out_shape=jax.ShapeDtypeStruct((2 * rows, w), F32),
        compiler_params=_cparams("arbitrary"),
    )(half, r)


def _cast_slabs(a, tr=163):
    rows = a.shape[0]
    assert rows % tr == 0

    def body(a_ref, o_ref):
        o_ref[...] = a_ref[...].astype(MXU)

    spec = pl.BlockSpec((tr,) + a.shape[1:], lambda i: (i, 0, 0))
    return pl.pallas_call(
        body, name="cast_w_in", grid=(rows // tr,), in_specs=[spec], out_specs=spec,
        out_shape=jax.ShapeDtypeStruct(a.shape, MXU), compiler_params=_cparams("arbitrary"),
    )(a)


def _sum_slabs(r, name, tr=69):
    s, rows = r.shape[:2]
    assert rows % tr == 0

    def body(r_ref, o_ref):
        tot = r_ref[0].astype(F32)
        for k in range(1, s):
            tot = tot + r_ref[k].astype(F32)
        o_ref[...] = tot

    return pl.pallas_call(
        body, name=name, grid=(rows // tr,),
        in_specs=[pl.BlockSpec((s, tr) + r.shape[2:], lambda i: (0, i, 0, 0))],
        out_specs=pl.BlockSpec((tr,) + r.shape[2:], lambda i: (i, 0, 0)),
        out_shape=jax.ShapeDtypeStruct(r.shape[1:], F32),
        compiler_params=_cparams("arbitrary"),
    )(r)


def _adamw_body(w_r, g_r, m_r, v_r, d_o, m_o, v_o):
    c1 = 1.0 - ADAM_B1 ** ADAM_STEP
    c2 = 1.0 - ADAM_B2 ** ADAM_STEP
    gv = g_r[...]
    mn = ADAM_B1 * m_r[...] + (1.0 - ADAM_B1) * gv
    vn = ADAM_B2 * v_r[...] + (1.0 - ADAM_B2) * (gv * gv)
    m_o[...] = mn
    v_o[...] = vn
    d_o[...] = -ADAM_LR * ((mn / c1) / (jnp.sqrt(vn / c2) + ADAM_EPS) + ADAM_WD * w_r[...])


def _adamw_slabs(w, g, m, v, name, tr):
    rows = w.shape[0]
    assert rows % tr == 0 and w.shape == g.shape

    def body(*refs):
        _adamw_body(*refs)

    spec = pl.BlockSpec((tr,) + w.shape[1:], lambda i: (i, 0, 0))
    shp = jax.ShapeDtypeStruct(w.shape, F32)
    return pl.pallas_call(
        body, name=name, grid=(rows // tr,),
        in_specs=[spec] * 4, out_specs=[spec] * 3, out_shape=[shp] * 3,
        compiler_params=_cparams("arbitrary"),
    )(w, g, m, v)


def _adamw(w, g, m, v, name, g_row0=0, tr=64):
    lead = w.ndim == 3
    rows, cols = w.shape[-2:]
    tr = min(tr, rows)
    assert rows % tr == 0 and g_row0 % tr == 0 and g.shape[1] == cols

    def body(*refs):
        _adamw_body(*refs)

    if lead:
        spec = pl.BlockSpec((None, tr, cols), lambda i: (0, i, 0))
    else:
        spec = pl.BlockSpec((tr, cols), lambda i: (i, 0))
    g_spec = pl.BlockSpec((tr, cols), lambda i: (g_row0 // tr + i, 0))
    shp = jax.ShapeDtypeStruct(w.shape, F32)
    return pl.pallas_call(
        body, name=name, grid=(rows // tr,),
        in_specs=[spec, g_spec, spec, spec], out_specs=[spec] * 3, out_shape=[shp] * 3,
        compiler_params=_cparams("arbitrary"),
    )(w, g, m, v)


def _allgather_chips(halved, whole):
    arrs = [a for a, _ in halved] + list(whole)
    parts = [p for _, p in halved]
    nh, na = len(halved), len(arrs)
    relations = ((1, 0), (0, 1), (1, 1))
    n_ici = 3 * na

    def body(*refs):
        ins, outs = refs[:na], refs[na:2 * na]
        send_sems, recv_sems = refs[2 * na:]
        x, y, c = _position()
        mine = 2 * x + y

        def rows(a, core, stride, size):
            return pl.ds(_part_start(core, stride, len(ins[a].shape) == 2), size)

        def remote(src, dst, k, to):
            return pltpu.make_async_remote_copy(src_ref=src, dst_ref=dst, send_sem=send_sems.at[k],
                                                recv_sem=recv_sems.at[k], device_id=to, device_id_type=MESH)

        first, passed = [], []
        for j, (rx, ry) in enumerate(relations):
            px, py = _flip(x, rx), _flip(y, ry)
            for a in range(na):
                if a < nh:
                    s1, n1, s2, n2 = parts[a]
                    first.append(remote(ins[a].at[rows(a, c, s1, n1)], outs[a].at[mine, rows(a, c, s1, n1)],
                                        j * na + a, (px, py, c)))
                    landed = outs[a].at[2 * px + py, rows(a, c, s2, n2)]
                    passed.append(remote(landed, landed, n_ici + j * nh + a, (x, y, 1 - c)))
                else:
                    first.append(remote(ins[a], outs[a].at[mine], j * na + a, (px, py, c)))
        for cp in first:
            cp.start()
        k = 0
        for j in range(3):
            for a in range(na):
                first[j * na + a].wait_recv()
                if a < nh:
                    passed[k].start()
                    k += 1
        for cp in passed:
            cp.wait_recv()
        for cp in first + passed:
            cp.wait_send()

    nsem = n_ici + 3 * nh
    outs = pl.pallas_call(
        body, name="allgather_weights",
        in_specs=[ANY] * na, out_specs=[ANY] * na,
        out_shape=[jax.ShapeDtypeStruct((N_CHIPS,) + a.shape, a.dtype) for a in arrs],
        scratch_shapes=[pltpu.SemaphoreType.DMA((nsem,)), pltpu.SemaphoreType.DMA((nsem,))],
    )(*arrs)
    chip = 2 * lax.axis_index("x") + lax.axis_index("y")
    return [lax.dynamic_update_slice(o, a[None], (chip,) + (0,) * a.ndim) for o, a in zip(outs, arrs)]


def _join_halves(piece, full_rest, small):
    def body(pc_ref, rest_in, sm, oin, orest, osm, send_sems, recv_sems, local_sem):
        del rest_in
        x, y, c = _position()
        me = 4 * x + 2 * y + c
        sib = (x, y, 1 - c)
        rows = orest.shape[0] // 2
        mine = orest.at[pl.ds(pl.multiple_of(c * rows, rows), rows)]
        copies = [
            pltpu.make_async_remote_copy(
                src_ref=pc_ref.at[pl.ds(c, HALF_START)], dst_ref=oin.at[pl.ds(c * HALF_SLABS, HALF_START)],
                send_sem=send_sems.at[0], recv_sem=recv_sems.at[0], device_id=sib, device_id_type=MESH),
            pltpu.make_async_remote_copy(
                src_ref=mine, dst_ref=mine, send_sem=send_sems.at[1], recv_sem=recv_sems.at[1],
                device_id=sib, device_id_type=MESH)]
        na = 2
        copies.append(pltpu.make_async_copy(sm, osm.at[me], local_sem))
        for r in range(1, N_DEV):
            to = (_flip(x, (r >> 2) & 1), _flip(y, (r >> 1) & 1), _flip(c, r & 1))
            copies.append(pltpu.make_async_remote_copy(
                src_ref=sm, dst_ref=osm.at[me], send_sem=send_sems.at[na + r - 1],
                recv_sem=recv_sems.at[na + r - 1], device_id=to, device_id_type=MESH))
        for cp in copies:
            cp.start()
        for cp in copies:
            cp.wait()

    nsem = 2 + N_DEV - 1
    return pl.pallas_call(
        body, name="join_halves",
        in_specs=[ANY] * 3, out_specs=[ANY] * 3,
        out_shape=[jax.ShapeDtypeStruct((SLABS_CHIP,) + piece.shape[1:], piece.dtype),
                   jax.ShapeDtypeStruct(full_rest.shape, full_rest.dtype),
                   jax.ShapeDtypeStruct((N_DEV,) + small.shape, small.dtype)],
        scratch_shapes=[pltpu.SemaphoreType.DMA((nsem,)), pltpu.SemaphoreType.DMA((nsem,)),
                        pltpu.SemaphoreType.DMA],
        input_output_aliases={1: 1},
    )(piece, full_rest, small)


def _to_internal(w_slabs):
    order = sorted(SEGMENTS, key=lambda s: s[2])
    main = jnp.concatenate([w_slabs[g0 // 2:(g0 + w) // 2] for g0, w, _ in order], axis=0)
    gate = w_slabs[GATE_COL // 2:GATE_COL // 2 + NH].reshape(2 * NH, D)
    return main.reshape(NMAIN, D), jnp.pad(gate, ((0, GATE_W - 2 * NH), (0, 0)))


def _to_global(main_t, gate_t):
    main = main_t.reshape(NMAIN // 2, 16, D // 8)
    parts = sorted([(g0, main[i0 // 2:(i0 + w) // 2]) for g0, w, i0 in SEGMENTS]
                   + [(GATE_COL, gate_t[0:2 * NH].reshape(NH, 16, D // 8))], key=lambda s: s[0])
    return jnp.concatenate([p for _, p in parts], axis=0)


def _pack_rows(wa, wb, wo, wpg, wp):
    return jnp.concatenate([wa, wb, wo, wpg, wp.reshape(PACK_ROWS[4], D)], axis=0)


def _pad_rows(a, rows=8):
    return jnp.pad(a, ((0, rows - a.shape[0]), (0, 0)))


def kernel(x, p, g_mix, w_in, conv_w, conv_b, w_a_out, b_gates, g_head, w_b_out, w_o, g_ple, w_ple_gate, w_ple, g_final, loss_target, m_g_mix, m_w_in, m_conv_w, m_conv_b, m_w_a_out, m_b_gates, m_g_head, m_w_b_out, m_w_o, m_g_ple, m_w_ple_gate, m_w_ple, m_g_final, v_g_mix, v_w_in, v_conv_w, v_conv_b, v_w_a_out, v_b_gates, v_g_head, v_w_b_out, v_w_o, v_g_ple, v_w_ple_gate, v_w_ple, v_g_final):
    chip = 2 * lax.axis_index("x") + lax.axis_index("y")
    xs, ps, ts = x[0], p[0, 0], loss_target[0]
    g_fin = g_final.reshape(1, D)

    pack_w = _pack_rows(w_a_out[0], w_b_out[0], w_o[0], w_ple_gate[0], w_ple[0])
    w_slabs = _cast_slabs(jnp.transpose(w_in, (2, 0, 1)).reshape(SLABS_CHIP, 16, D // 8))
    half_rows = PACK_TOTAL // 2
    g_win, g_pack, g_cw = _allgather_chips(
        [(w_slabs, (HALF_START, HALF_SLABS, HALF_SLABS, HALF_START)),
         (pack_w.astype(MXU), (half_rows, half_rows, half_rows, half_rows))], [_pad_rows(conv_w[0])])
    w_t, wg_t = _to_internal(g_win.reshape(N_IN // 2, 16, D // 8))
    offs = [0, 256, 768, 1024, 1280, 1344]
    wa, wb, wo, wpg = [g_pack[:, offs[k]:offs[k + 1]].reshape(-1, D) for k in range(4)]
    wp = jnp.transpose(g_pack[:, offs[4]:offs[5]].reshape(N_CHIPS, PLE, PLE), (1, 0, 2)).reshape(PLE, D)
    cw = jnp.transpose(g_cw, (1, 0, 2)).reshape(8, D)

    bias = jnp.pad(b_gates, ((0, 0), (0, GATE_W - 2 * NH)))
    hn, gates = _prenorm(xs, g_mix, wg_t)
    proj = _proj(hn, w_t)
    h, stats, cs, ns, ms = _mlstm_fwd(proj, gates, bias)
    (a_pre, b_pre, mg, xn1, de, dgp, ya, yb, x1, dx2, acc_f) = _tail_fwd(
        proj, h, xs, ps, ts, cw, conv_b, g_head, g_ple, g_fin, wa, wb, wo, wpg, wp)
    dproj, dcv, dh, dx1, dx1b, dya, dyb, acc_b = _tail_bwd(
        proj, h, dgp, dx2, x1, ya, yb, cw, conv_b, g_head, g_ple, wpg, wo, wb, wa)
    dproj = _conv_bwd(proj, dcv, cw, dproj)
    dproj, dgates, gsum = _mlstm_bwd(proj, gates, bias, h, dh, stats, cs, ns, ms, dproj)
    d_main = _matmul_tn(dproj, hn, "dw_in", out_dtype=WIRE, tk=2048)
    d_gate = _matmul_tn(dgates, hn, "dw_gate", out_dtype=WIRE)
    d_wa = _matmul_tn(a_pre, dya, "dw_a_out", out_dtype=WIRE)
    d_wb = _matmul_tn(b_pre, dyb, "dw_b_out", out_dtype=WIRE)
    d_wo = _matmul_tn(mg, dx1b, "dw_o", out_dtype=WIRE)
    d_wpg = _matmul_tn(xn1, dgp, "dw_ple_gate", out_dtype=WIRE)
    d_wp = _matmul_tn(ps, de, "dw_ple", out_dtype=WIRE)

    g_in = _to_global(d_main, d_gate).reshape(N_CHIPS, SLABS_CHIP, 16, D // 8)
    d_wp_c = jnp.transpose(d_wp.reshape(PLE, N_CHIPS, PLE), (1, 0, 2)).reshape(N_CHIPS, PACK_ROWS[4], D)
    g_rest = jnp.concatenate(
        [d_wa.reshape(N_CHIPS, -1, D), d_wb.reshape(N_CHIPS, -1, D), d_wo.reshape(N_CHIPS, -1, D),
         d_wpg.reshape(N_CHIPS, -1, D), d_wp_c], axis=1)
    grad_x, acc_x, r_in, r_rest = _input_grad(dproj, dgates, w_t, wg_t, xs, dx1, g_mix, g_in, g_rest)
    small = _pack_small(acc_f, acc_b, acc_x, gsum)
    core = lax.axis_index("c").astype(jnp.int32)
    piece = _sum_slabs(r_in, "sum_w_in")
    gw_in, gw_rest, r_small = _join_halves(
        piece, _sum_slots(r_rest, "sum_rest", tr=96, half=core.reshape(1)), small)
    gw_in = lax.dynamic_update_slice(gw_in, piece, (core * HALF_START, 0, 0)).reshape(COLS_CHIP, 8, D // 8)
    gs = _sum_slots(r_small, "sum_small", tr=SMALL_ROWS)

    big = []
    row0 = 0
    for name, w, m, v in (("w_a_out", w_a_out, m_w_a_out, v_w_a_out), ("w_b_out", w_b_out, m_w_b_out, v_w_b_out),
                          ("w_o", w_o, m_w_o, v_w_o), ("w_ple_gate", w_ple_gate, m_w_ple_gate, v_w_ple_gate)):
        big.append((name, w, m, v, gw_rest, row0))
        row0 += w.shape[1]
    g_wp = gw_rest[row0:row0 + PACK_ROWS[4]].reshape(PLE, PLE)
    big.append(("w_ple", w_ple, m_w_ple, v_w_ple, g_wp, 0))
    upd = {name: _adamw(w, g, m, v, "adamw_" + name, g_row0=r0) for name, w, m, v, g, r0 in big}
    g_big = {name: (g if name == "w_ple" else g[r0:r0 + w.shape[1]])[None] for name, w, m, v, g, r0 in big}
    slabs = lambda a: jnp.transpose(a, (2, 0, 1)).reshape(COLS_CHIP, 8, D // 8)
    unslab = lambda a: jnp.transpose(a, (1, 2, 0)).reshape(1, D, COLS_CHIP)
    upd["w_in"] = [unslab(u) for u in _adamw_slabs(
        slabs(w_in), gw_in, slabs(m_w_in), slabs(v_w_in), "adamw_w_in", tr=326)]
    g_big["w_in"] = unslab(gw_in)

    lane = lax.broadcasted_iota(jnp.int32, (1, D), 1)
    g_small = jnp.concatenate([gs[0:6], jnp.where(lane < 2 * NH, gs[9:10], 0.0), jnp.zeros((1, D), F32)], axis=0)

    def small_pack(gm, cb_, bg, gh, gp, gf):
        return jnp.concatenate([gm, cb_, gh.reshape(2, D), gp, gf.reshape(1, D),
                                jnp.pad(bg, ((0, 0), (0, D - 2 * NH))), jnp.zeros((1, D), F32)], axis=0)

    ws = small_pack(g_mix, conv_b, b_gates, g_head, g_ple, g_final)
    ms_ = small_pack(m_g_mix, m_conv_b, m_b_gates, m_g_head, m_g_ple, m_g_final)
    vs = small_pack(v_g_mix, v_conv_b, v_b_gates, v_g_head, v_g_ple, v_g_final)
    upd_s = _adamw(ws, g_small, ms_, vs, "adamw_small")
    g_cw_mine = _pad_rows(lax.dynamic_slice(gs[6:9], (0, chip * PLE), (3, PLE)))
    upd_c = _adamw(_pad_rows(conv_w[0]), g_cw_mine, _pad_rows(m_conv_w[0]), _pad_rows(v_conv_w[0]),
                   "adamw_conv_w")

    def leaves(bigs, sm, cwv):
        return [sm[0:1], bigs["w_in"], cwv[0:3][None], sm[1:2], bigs["w_a_out"], sm[6:7, 0:2 * NH],
                sm[2:4].reshape(1, VD), bigs["w_b_out"], bigs["w_o"], sm[4:5], bigs["w_ple_gate"],
                bigs["w_ple"], sm[5]]

    loss = gs[9, 2 * NH]
    outs = [loss, grad_x[None]] + leaves(g_big, g_small, g_cw_mine)
    for k in range(3):
        outs += leaves({name: u[k] for name, u in upd.items()}, upd_s[k], upd_c[k])
    return tuple(outs)
```

```python
import jax
import jax.numpy as jnp
from jax import lax
from jax.experimental import pallas as pl
from jax.experimental.pallas import tpu as pltpu

F32 = jnp.float32
MXU = jnp.bfloat16
WIRE = jnp.bfloat16

D = 1024
NH, DK, DV = 4, 256, 512
VD = NH * DV
PLE = 256
LCH = 256
EPS = 1e-6
N_IN = 14344
NMAIN = 14336
GATE_W = 128
N_CHIPS, N_DEV = 4, 8

C_BA, C_ZA, C_O, C_ZB, C_GA, C_GB = 0, 1024, 2048, 4096, 6144, 7168
QK = NH * DK
C_Q, C_K, C_V, C_XA, C_CA = 8192, 9216, 10240, 12288, 13312
QKV_W = 2 * QK + VD
TAIL_W = 8192
CONV_W = 2048
SEGMENTS = (
    (0, 1024, C_XA), (1024, 1024, C_BA), (2048, 1024, C_CA), (3072, 1024, C_ZA),
    (4096, QKV_W, C_Q), (8192, 4096, C_O), (12296, 2048, C_GA),
)
GATE_COL = 12288

COLS_CHIP = N_IN // N_CHIPS
SLABS_CHIP = COLS_CHIP // 2
HALF_START = SLABS_CHIP // 2
HALF_SLABS = SLABS_CHIP - HALF_START

PACK_ROWS = (256, 512, 256, 256, 64)
PACK_TOTAL = sum(PACK_ROWS)
SMALL_ROWS = 16

ADAM_LR, ADAM_B1, ADAM_B2, ADAM_EPS, ADAM_WD, ADAM_STEP = 0.001, 0.9, 0.999, 1e-08, 0.01, 10

VMEM_LIMIT = 56 * 1024 * 1024
MESH = pl.DeviceIdType.MESH
ANY = pl.BlockSpec(memory_space=pl.ANY)


def _cparams(*sem):
    return pltpu.CompilerParams(dimension_semantics=sem, vmem_limit_bytes=VMEM_LIMIT)


def _dot(a, b):
    return jnp.dot(a, b, preferred_element_type=F32)


def _dot_nt(a, b):
    return lax.dot_general(a, b, (((1,), (1,)), ((), ())), preferred_element_type=F32)


def _dot_tn(a, b):
    return lax.dot_general(a, b, (((0,), (0,)), ((), ())), preferred_element_type=F32)


def _sigmoid(x):
    return 1.0 / (1.0 + jnp.exp(-x))


def _logsig(x):
    return jnp.minimum(x, 0.0) - jnp.log(1.0 + jnp.exp(-jnp.abs(x)))


def _gate(v):
    one = jnp.asarray(1.0, v.dtype)
    t = jnp.exp(-jnp.abs(v))
    r = one / (one + t)
    return jnp.where(v >= 0, r, t * r), t * r * r


def _rstd(x):
    return lax.rsqrt(jnp.mean(x * x, axis=-1, keepdims=True) + EPS)


def _norm_bwd(dy, xhat, r, g):
    dxh = dy * g
    return r * (dxh - xhat * jnp.mean(dxh * xhat, axis=-1, keepdims=True))


def _f32(ref):
    return ref[...].astype(F32)


def _prenorm(x, g_mix, wg_t):
    n = x.shape[0]
    tm = min(512, n)

    def body(x_ref, g_ref, wg_ref, hn_ref, gate_ref):
        xv = x_ref[...]
        hn = (xv * _rstd(xv) * g_ref[...]).astype(MXU)
        hn_ref[...] = hn
        gate_ref[...] = _dot_nt(hn, wg_ref[...])

    return pl.pallas_call(
        body, name="prenorm", grid=(n // tm,),
        in_specs=[pl.BlockSpec((tm, D), lambda i: (i, 0)),
                  pl.BlockSpec((1, D), lambda i: (0, 0)),
                  pl.BlockSpec((GATE_W, D), lambda i: (0, 0))],
        out_specs=[pl.BlockSpec((tm, D), lambda i: (i, 0)),
                   pl.BlockSpec((tm, GATE_W), lambda i: (i, 0))],
        out_shape=[jax.ShapeDtypeStruct((n, D), MXU), jax.ShapeDtypeStruct((n, GATE_W), F32)],
        compiler_params=_cparams("arbitrary"),
    )(x, g_mix, wg_t)


def _proj(hn, w_t, tm=2048, tn=1024):
    n, k = hn.shape
    m = w_t.shape[0]
    tm = min(tm, n)

    def body(a_ref, b_ref, o_ref):
        o_ref[...] = _dot_nt(a_ref[...], b_ref[...]).astype(MXU)

    return pl.pallas_call(
        body, name="proj", grid=(m // tn, n // tm),
        in_specs=[pl.BlockSpec((tm, k), lambda j, i: (i, 0)),
                  pl.BlockSpec((tn, k), lambda j, i: (j, 0))],
        out_specs=pl.BlockSpec((tm, tn), lambda j, i: (i, j)),
        out_shape=jax.ShapeDtypeStruct((n, m), MXU),
        compiler_params=_cparams("arbitrary", "arbitrary"),
    )(hn, w_t)


def _matmul_tn(a, b, name, out_dtype=F32, ta=1024, tb=1024, tk=1024):
    n, ka = a.shape
    kb = b.shape[1]
    ta, tb, tk = min(ta, ka), min(tb, kb), min(tk, n)
    nk = n // tk

    def body(a_ref, b_ref, o_ref, acc):
        kk = pl.program_id(2)

        @pl.when(kk == 0)
        def _():
            acc[...] = jnp.zeros_like(acc)

        acc[...] += _dot_tn(a_ref[...].astype(MXU), b_ref[...].astype(MXU))

        @pl.when(kk == nk - 1)
        def _():
            o_ref[...] = acc[...].astype(out_dtype)

    return pl.pallas_call(
        body, name=name, grid=(ka // ta, kb // tb, nk),
        in_specs=[pl.BlockSpec((tk, ta), lambda i, j, kk: (kk, i)),
                  pl.BlockSpec((tk, tb), lambda i, j, kk: (kk, j))],
        out_specs=pl.BlockSpec((ta, tb), lambda i, j, kk: (i, j)),
        out_shape=jax.ShapeDtypeStruct((ka, kb), out_dtype),
        scratch_shapes=[pltpu.VMEM((ta, tb), F32)],
        compiler_params=_cparams("arbitrary", "arbitrary", "arbitrary"),
    )(a, b)


def _gate_vectors(g, gt, hd):
    lane = lax.broadcasted_iota(jnp.int32, g.shape, 1)
    sub = lax.broadcasted_iota(jnp.int32, gt.shape, 0)
    col = lambda j: jnp.sum(jnp.where(lane == j, g, 0.0), axis=1, keepdims=True)
    row = lambda j: jnp.sum(jnp.where(sub == j, gt, 0.0), axis=0, keepdims=True)
    return col(hd), col(hd + NH), row(hd), row(hd + NH)


def _chunk_decay(li_col, li_row, lf_col, lf_row, m_prev):
    n = li_col.shape[0]
    r = lax.broadcasted_iota(jnp.int32, (n, n), 0)
    c = lax.broadcasted_iota(jnp.int32, (n, n), 1)
    tri = r >= c
    b_col = jnp.sum(jnp.where(tri, lf_row, 0.0), axis=1, keepdims=True)
    b_row = jnp.sum(jnp.where(r <= c, lf_col, 0.0), axis=0, keepdims=True)
    b_last = jnp.sum(lf_row, axis=1, keepdims=True)
    dmat = jnp.where(tri, b_col - b_row + li_row, -jnp.inf)
    a_col = b_col + m_prev
    g_col = b_last - b_col + li_col
    m_new = jnp.maximum(b_last + m_prev, jnp.max(g_col, axis=0, keepdims=True))
    w_col = jnp.exp(g_col - m_new)
    decay = jnp.exp(b_last + m_prev - m_new)
    return tri, dmat, a_col, m_new, w_col, decay


def _qkv_specs(row_of):
    q_spec = pl.BlockSpec((LCH, QK), lambda c: (row_of(c), C_Q // QK))
    k_spec = pl.BlockSpec((LCH, QK), lambda c: (row_of(c), C_K // QK))
    v_spec = pl.BlockSpec((LCH, VD), lambda c: (row_of(c), C_V // VD))
    return q_spec, k_spec, v_spec


def _state_specs(row_of):
    return [pl.BlockSpec((NH, None, DK, DV), lambda c: (0, row_of(c), 0, 0)),
            pl.BlockSpec((NH, None, 1, DK), lambda c: (0, row_of(c), 0, 0)),
            pl.BlockSpec((NH, None, 1, GATE_W), lambda c: (0, row_of(c), 0, 0))]


def _lane_put(col, lane_id, width=GATE_W):
    lane = lax.broadcasted_iota(jnp.int32, (col.shape[0], width), 1)
    return jnp.where(lane == lane_id, col, 0.0)


def _lane_get(block, lane_id):
    lane = lax.broadcasted_iota(jnp.int32, block.shape, 1)
    return jnp.sum(jnp.where(lane == lane_id, block, 0.0), axis=1, keepdims=True)


def _mlstm_fwd(proj, gates, bias):
    n = proj.shape[0]
    nc = n // LCH

    def body(q_ref, k_ref, v_ref, g_ref, bias_ref,
             h_ref, st_ref, cs_ref, ns_ref, ms_ref, c_scr, n_scr, m_scr):
        @pl.when(pl.program_id(0) == 0)
        def _():
            c_scr[...] = jnp.zeros_like(c_scr)
            n_scr[...] = jnp.zeros_like(n_scr)
            m_scr[...] = jnp.full_like(m_scr, -jnp.inf)

        g = g_ref[...] + bias_ref[...]
        gt = g.T[0:8, :]
        stats = jnp.zeros((LCH, GATE_W), F32)
        for hd in range(NH):
            li_col, fr_col, li_row, fr_row = _gate_vectors(g, gt, hd)
            m_all = m_scr[hd]
            m_prev = m_all[0:1, 0:1]
            tri, dmat, a_col, m_new, w_col, decay = _chunk_decay(
                li_col, li_row, _logsig(fr_col), _logsig(fr_row), m_prev)
            m_col = jnp.maximum(a_col, jnp.max(dmat, axis=1, keepdims=True))
            dl = jnp.exp(dmat - m_col)
            inter = jnp.exp(a_col - m_col)

            qs = q_ref[:, hd * DK:(hd + 1) * DK] * (DK ** -0.5)
            kk = k_ref[:, hd * DK:(hd + 1) * DK]
            vv = v_ref[:, hd * DV:(hd + 1) * DV]
            cst = c_scr[hd]
            nst = n_scr[hd]
            cs_ref[hd] = cst
            ns_ref[hd] = nst
            ms_ref[hd] = m_all

            sc = _dot_nt(qs, kk) * dl
            num = _dot(sc.astype(MXU), vv) + inter * _dot(qs, cst.astype(MXU))
            den = (jnp.sum(sc, axis=1, keepdims=True)
                   + inter * jnp.sum(qs.astype(F32) * nst, axis=1, keepdims=True))
            nrm = jnp.maximum(jnp.abs(den), jnp.exp(-m_col))
            h_ref[:, hd * DV:(hd + 1) * DV] = num / nrm
            stats = stats + _lane_put(den, hd) + _lane_put(m_col, NH + hd)

            kw = kk.astype(F32) * w_col
            c_scr[hd] = decay * cst + _dot_tn(kw.astype(MXU), vv)
            n_scr[hd] = decay * nst + jnp.sum(kw, axis=0, keepdims=True)
            m_scr[hd] = jnp.broadcast_to(m_new, (1, GATE_W))
        st_ref[...] = stats

    q_spec, k_spec, v_spec = _qkv_specs(lambda c: c)
    return pl.pallas_call(
        body, name="mlstm_fwd", grid=(nc,),
        in_specs=[q_spec, k_spec, v_spec,
                  pl.BlockSpec((LCH, GATE_W), lambda c: (c, 0)),
                  pl.BlockSpec((1, GATE_W), lambda c: (0, 0))],
        out_specs=[pl.BlockSpec((LCH, VD), lambda c: (c, 0)),
                   pl.BlockSpec((LCH, GATE_W), lambda c: (c, 0))] + _state_specs(lambda c: c),
        out_shape=[jax.ShapeDtypeStruct((n, VD), F32),
                   jax.ShapeDtypeStruct((n, GATE_W), F32),
                   jax.ShapeDtypeStruct((NH, nc, DK, DV), F32),
                   jax.ShapeDtypeStruct((NH, nc, 1, DK), F32),
                   jax.ShapeDtypeStruct((NH, nc, 1, GATE_W), F32)],
        scratch_shapes=[pltpu.VMEM((NH, DK, DV), F32), pltpu.VMEM((NH, 1, DK), F32),
                        pltpu.VMEM((NH, 1, GATE_W), F32)],
        compiler_params=_cparams("arbitrary"),
    )(proj, proj, proj, gates, bias)


def _mlstm_bwd(proj, gates, bias, h, dh, stats, cs, ns, ms, dproj):
    n = proj.shape[0]
    nc = n // LCH

    def body(q_ref, k_ref, v_ref, g_ref, bias_ref, h_ref, dh_ref, st_ref,
             cs_ref, ns_ref, ms_ref, dproj_in,
             dqkv_ref, dg_ref, gsum_ref, dc_scr, dn_scr):
        del dproj_in

        @pl.when(pl.program_id(0) == 0)
        def _():
            dc_scr[...] = jnp.zeros_like(dc_scr)
            dn_scr[...] = jnp.zeros_like(dn_scr)
            gsum_ref[...] = jnp.zeros_like(gsum_ref)

        g = g_ref[...] + bias_ref[...]
        gt = g.T[0:8, :]
        stats = st_ref[...]
        r = lax.broadcasted_iota(jnp.int32, (LCH, LCH), 0)
        c = lax.broadcasted_iota(jnp.int32, (LCH, LCH), 1)
        eye = r == c
        to_col = lambda row: jnp.sum(jnp.where(eye, row, 0.0), axis=1, keepdims=True)
        last = lax.broadcasted_iota(jnp.int32, (LCH, 1), 0) == LCH - 1
        dg = jnp.zeros((LCH, GATE_W), F32)
        for hd in range(NH):
            li_col, fr_col, li_row, fr_row = _gate_vectors(g, gt, hd)
            m_prev = ms_ref[hd][0:1, 0:1]
            tri, dmat, a_col, m_new, w_col, decay = _chunk_decay(
                li_col, li_row, _logsig(fr_col), _logsig(fr_row), m_prev)
            m_col = _lane_get(stats, NH + hd)
            dl = jnp.exp(dmat - m_col)
            inter = jnp.exp(a_col - m_col)

            qs = q_ref[:, hd * DK:(hd + 1) * DK] * (DK ** -0.5)
            kk = k_ref[:, hd * DK:(hd + 1) * DK]
            vv = v_ref[:, hd * DV:(hd + 1) * DV]
            qf = qs.astype(F32)
            kf = kk.astype(F32)
            cst = cs_ref[hd]
            nst = ns_ref[hd]
            cb = cst.astype(MXU)
            dcn = dc_scr[hd]
            dnn = dn_scr[hd]
            dcb = dcn.astype(MXU)

            den = _lane_get(stats, hd)
            floor = jnp.exp(-m_col)
            nrm = jnp.maximum(jnp.abs(den), floor)
            dhv = dh_ref[:, hd * DV:(hd + 1) * DV]
            dnum = dhv / nrm
            dnum_b = dnum.astype(MXU)
            dhh = jnp.sum(dhv * h_ref[:, hd * DV:(hd + 1) * DV], axis=1, keepdims=True)
            dden = jnp.where(jnp.abs(den) > floor, -dhh / nrm * jnp.sign(den), 0.0)

            sc = _dot_nt(qs, kk) * dl
            dsc = _dot_nt(dnum_b, vv) + dden
            da = (dl * dsc).astype(MXU)
            gmat = sc * dsc

            dq = _dot(da, kk) + inter * (_dot_nt(dnum_b, cb) + dden * nst)
            dk_state = w_col * (_dot_nt(vv, dcb) + dnn)
            dk = _dot_tn(da, qs) + dk_state
            kw = (kf * w_col).astype(MXU)
            dv = _dot_tn(sc.astype(MXU), dnum_b) + _dot(kw, dcb)
            dqkv_ref[:, hd * DK:(hd + 1) * DK] = (dq * (DK ** -0.5)).astype(dqkv_ref.dtype)
            dqkv_ref[:, QK + hd * DK:QK + (hd + 1) * DK] = dk.astype(dqkv_ref.dtype)
            dqkv_ref[:, 2 * QK + hd * DV:2 * QK + (hd + 1) * DV] = dv.astype(dqkv_ref.dtype)

            num_i = _dot(qs, cb)
            den_i = jnp.sum(qf * nst, axis=1, keepdims=True)
            e_col = inter * (jnp.sum(dnum * num_i, axis=1, keepdims=True) + dden * den_i)
            h_col = jnp.sum(kf * dk_state, axis=1, keepdims=True)
            f_dec = decay * (jnp.sum(jnp.sum(cst * dcn, axis=1, keepdims=True), axis=0, keepdims=True)
                             + jnp.sum(nst * dnn, axis=1, keepdims=True))
            row_g = jnp.sum(gmat, axis=1, keepdims=True)
            col_g = to_col(jnp.sum(gmat, axis=0, keepdims=True))
            db_col = row_g - col_g + e_col - h_col
            db_col = db_col + jnp.where(last, jnp.sum(h_col, axis=0, keepdims=True) + f_dec, 0.0)
            dli_col = col_g + h_col
            dlf_row = jnp.sum(jnp.where(tri, db_col, 0.0), axis=0, keepdims=True)
            df_col = to_col(dlf_row) * _sigmoid(-fr_col)
            dg = dg + _lane_put(dli_col, hd) + _lane_put(df_col, NH + hd)

            dc_scr[hd] = decay * dcn + _dot_tn((qf * inter).astype(MXU), dnum_b)
            dn_scr[hd] = decay * dnn + jnp.sum(qf * (inter * dden), axis=0, keepdims=True)
        dg_ref[...] = dg
        gsum_ref[0:1, 0:GATE_W] += jnp.sum(dg, axis=0, keepdims=True)

    rev = lambda c: nc - 1 - c
    q_spec, k_spec, v_spec = _qkv_specs(rev)
    hv_spec = pl.BlockSpec((LCH, VD), lambda c: (rev(c), 0))
    gate_spec = pl.BlockSpec((LCH, GATE_W), lambda c: (rev(c), 0))
    return pl.pallas_call(
        body, name="mlstm_bwd", grid=(nc,),
        in_specs=[q_spec, k_spec, v_spec, gate_spec,
                  pl.BlockSpec((1, GATE_W), lambda c: (0, 0)),
                  hv_spec, hv_spec, gate_spec] + _state_specs(rev) + [ANY],
        out_specs=[pl.BlockSpec((LCH, QKV_W), lambda c: (rev(c), C_Q // QKV_W)),
                   gate_spec,
                   pl.BlockSpec((8, D), lambda c: (0, 0))],
        out_shape=[jax.ShapeDtypeStruct(dproj.shape, dproj.dtype),
                   jax.ShapeDtypeStruct((n, GATE_W), F32),
                   jax.ShapeDtypeStruct((8, D), F32)],
        scratch_shapes=[pltpu.VMEM((NH, DK, DV), F32), pltpu.VMEM((NH, 1, DK), F32)],
        input_output_aliases={11: 0},
        compiler_params=_cparams("arbitrary"),
    )(proj, proj, proj, gates, bias, h, dh, stats, cs, ns, ms, dproj)


def _proj_spec(tm, col, width):
    return pl.BlockSpec((tm, width), lambda i: (i, col // width))


def _halo_prev(tm, col):
    return pl.BlockSpec((8, 1024), lambda i: (jnp.maximum(i * (tm // 8) - 1, 0), col // 1024))


def _const(shape):
    return pl.BlockSpec(shape, lambda i: (0,) * len(shape))


def _conv_inputs(i, tm, xa_ref, ca_ref, xah_ref, cah_ref):
    u = _f32(xa_ref) * _f32(ca_ref)
    uh = jnp.where(i > 0, _f32(xah_ref) * _f32(cah_ref), 0.0)
    rid = lax.broadcasted_iota(jnp.int32, u.shape, 0)
    u1 = jnp.where(rid == 0, uh[7:8, :], pltpu.roll(u, 1, 0))
    u2 = jnp.where(rid == 0, uh[6:7, :], jnp.where(rid == 1, uh[7:8, :], pltpu.roll(u, 2, 0)))
    return u, u1, u2


def _head_norm(hh, gh):
    out = []
    for j in range(NH):
        hj = hh[:, j * DV:(j + 1) * DV]
        rj = _rstd(hj)
        out.append((hj * rj, rj, gh[:, j * DV:(j + 1) * DV]))
    return out


def _tail_fwd(proj, h, x, p, t, cw, cb, gh, gple, gfin, wa, wb, wo, wpg, wp):
    n = x.shape[0]
    tm = min(256, n)

    def body(ba_r, za_r, o_r, zb_r, ga_r, gb_r, xa_r, ca_r, xah_r, cah_r, h_r, x_r, p_r, t_r,
             cw_r, cb_r, gh_r, gple_r, gfin_r, wa_r, wb_r, wo_r, wpg_r, wp_r,
             apre_o, bpre_o, mg_o, xn1_o, de_o, dgp_o, ya_o, yb_o, x1_o, dx2_o, acc_o):
        i = pl.program_id(0)

        @pl.when(i == 0)
        def _():
            acc_o[...] = jnp.zeros_like(acc_o)

        u, u1, u2 = _conv_inputs(i, tm, xa_r, ca_r, xah_r, cah_r)
        cwv = cw_r[...]
        cv = cwv[0:1, :] * u2 + cwv[1:2, :] * u1 + cwv[2:3, :] * u + cb_r[...]
        za = za_r[...]
        a_pre = ba_r[...] * cv.astype(MXU) * (za * _gate(za)[0])
        apre_o[...] = a_pre
        ya = _dot(a_pre, wa_r[...]).astype(MXU)

        hn = _head_norm(h_r[...], gh_r[...])
        hbn = jnp.concatenate([xh * g for xh, _, g in hn], axis=1)
        zb = zb_r[...]
        b_pre = _gate(o_r[...])[0] * hbn.astype(MXU) * (zb * _gate(zb)[0])
        bpre_o[...] = b_pre
        yb = _dot(b_pre, wb_r[...]).astype(MXU)
        ya_o[...] = ya
        yb_o[...] = yb

        mg = _gate(ga_r[...])[0] * ya + _gate(gb_r[...])[0] * yb
        mg_o[...] = mg
        x1 = x_r[...] + _dot(mg, wo_r[...])
        x1_o[...] = x1
        xn1 = (x1 * _rstd(x1) * gple_r[...]).astype(MXU)
        xn1_o[...] = xn1
        gt = _sigmoid(_dot(xn1, wpg_r[...]))
        e = _dot(p_r[...].astype(MXU), wp_r[...])
        x2 = x1 + gt * e
        r2 = _rstd(x2)
        xh2 = x2 * r2
        gf = gfin_r[...]
        diff = xh2 * gf - t_r[...]
        dy = diff * (1.0 / D)
        dx2 = _norm_bwd(dy, xh2, r2, gf)
        dx2_o[...] = dx2
        de_o[...] = (dx2 * gt).astype(MXU)
        dgp_o[...] = (dx2 * e * gt * (1.0 - gt)).astype(MXU)
        acc_o[0:1, :] += jnp.sum(dy * xh2, axis=0, keepdims=True)
        loss = 0.5 * jnp.sum(jnp.sum(diff * diff, axis=1, keepdims=True), axis=0, keepdims=True) * (1.0 / D)
        acc_o[1:2, :] += jnp.broadcast_to(loss, (1, D))

    row = lambda w, dt: (pl.BlockSpec((tm, w), lambda i: (i, 0)), jax.ShapeDtypeStruct((n, w), dt))
    outs = [row(D, MXU), row(VD, MXU), row(D, MXU), row(D, MXU), row(D, MXU), row(D, MXU),
            row(D, MXU), row(D, MXU), row(D, F32), row(D, F32),
            (_const((8, D)), jax.ShapeDtypeStruct((8, D), F32))]
    return pl.pallas_call(
        body, name="tail_fwd", grid=(n // tm,),
        in_specs=[_proj_spec(tm, C_BA, 1024), _proj_spec(tm, C_ZA, 1024),
                  _proj_spec(tm, C_O, 2048), _proj_spec(tm, C_ZB, 2048),
                  _proj_spec(tm, C_GA, 1024), _proj_spec(tm, C_GB, 1024),
                  _proj_spec(tm, C_XA, 1024), _proj_spec(tm, C_CA, 1024),
                  _halo_prev(tm, C_XA), _halo_prev(tm, C_CA),
                  pl.BlockSpec((tm, VD), lambda i: (i, 0)),
                  pl.BlockSpec((tm, D), lambda i: (i, 0)),
                  pl.BlockSpec((tm, PLE), lambda i: (i, 0)),
                  pl.BlockSpec((tm, D), lambda i: (i, 0)),
                  _const((8, D)), _const((1, D)), _const((1, VD)), _const((1, D)), _const((1, D)),
                  _const((D, D)), _const((VD, D)), _const((D, D)), _const((D, D)), _const((PLE, D))],
        out_specs=[s for s, _ in outs],
        out_shape=[s for _, s in outs],
        compiler_params=_cparams("arbitrary"),
    )(*([proj] * 10), h, x, p, t, cw, cb, gh, gple, gfin, wa, wb, wo, wpg, wp)


def _tail_bwd(proj, h, dgp, dx2, x1, ya, yb, cw, cb, gh, gple, wpg, wo, wb, wa):
    n = x1.shape[0]
    tm = min(256, n)

    def body(ba_r, za_r, o_r, zb_r, ga_r, gb_r, xa_r, ca_r, xah_r, cah_r, h_r,
             dgp_r, dx2_r, x1_r, ya_r, yb_r, cw_r, cb_r, gh_r, gple_r,
             wpg_r, wo_r, wb_r, wa_r,
             dproj_o, dcv_o, dh_o, dx1_o, dx1b_o, dya_o, dyb_o, acc_o):
        i = pl.program_id(0)

        @pl.when(i == 0)
        def _():
            acc_o[...] = jnp.zeros_like(acc_o)

        dxn1 = _dot_nt(dgp_r[...], wpg_r[...])
        x1 = x1_r[...]
        r1 = _rstd(x1)
        xh1 = x1 * r1
        acc_o[0:1, 0:D] += jnp.sum(dxn1 * xh1, axis=0, keepdims=True)
        dx1 = dx2_r[...] + _norm_bwd(dxn1, xh1, r1, gple_r[...])
        dx1_o[...] = dx1
        dx1b = dx1.astype(MXU)
        dx1b_o[...] = dx1b

        dmg = _dot_nt(dx1b, wo_r[...]).astype(MXU)
        sga, dsga = _gate(ga_r[...])
        sgb, dsgb = _gate(gb_r[...])
        dya = dmg * sga
        dyb = dmg * sgb
        dya_o[...] = dya
        dyb_o[...] = dyb
        dproj_o[:, C_GA:C_GA + D] = dmg * ya_r[...] * dsga
        dproj_o[:, C_GB:C_GB + D] = dmg * yb_r[...] * dsgb

        db_pre = _dot_nt(dyb, wb_r[...]).astype(MXU)
        hn = _head_norm(h_r[...], gh_r[...])
        hbn = jnp.concatenate([xh * g for xh, _, g in hn], axis=1).astype(MXU)
        so, dso = _gate(o_r[...])
        zb = zb_r[...]
        szb, dszb = _gate(zb)
        sb = zb * szb
        t1 = db_pre * hbn
        dproj_o[:, C_O:C_O + VD] = t1 * (sb * dso)
        dproj_o[:, C_ZB:C_ZB + VD] = t1 * (so * (szb + zb * dszb))
        dhbn = (db_pre * (so * sb)).astype(F32)
        for j, (xh, rj, g) in enumerate(hn):
            dj = dhbn[:, j * DV:(j + 1) * DV]
            acc_o[1:2, j * DV:(j + 1) * DV] += jnp.sum(dj * xh, axis=0, keepdims=True)
            dh_o[:, j * DV:(j + 1) * DV] = _norm_bwd(dj, xh, rj, g)

        da_pre = _dot_nt(dya, wa_r[...]).astype(MXU)
        u, u1, u2 = _conv_inputs(i, tm, xa_r, ca_r, xah_r, cah_r)
        cwv = cw_r[...]
        cv = cwv[0:1, :] * u2 + cwv[1:2, :] * u1 + cwv[2:3, :] * u + cb_r[...]
        za = za_r[...]
        sza, dsza = _gate(za)
        sa = za * sza
        ba = ba_r[...]
        t2 = da_pre * cv.astype(MXU)
        dproj_o[:, C_BA:C_BA + D] = t2 * sa
        dproj_o[:, C_ZA:C_ZA + D] = t2 * (ba * (sza + za * dsza))
        dcv_b = da_pre * (ba * sa)
        dcv_o[...] = dcv_b
        dcv = dcv_b.astype(F32)
        acc_o[2:3, 0:D] += jnp.sum(dcv, axis=0, keepdims=True)
        acc_o[3:4, 0:D] += jnp.sum(dcv * u2, axis=0, keepdims=True)
        acc_o[4:5, 0:D] += jnp.sum(dcv * u1, axis=0, keepdims=True)
        acc_o[5:6, 0:D] += jnp.sum(dcv * u, axis=0, keepdims=True)

    row = lambda w, dt: (pl.BlockSpec((tm, w), lambda i: (i, 0)), jax.ShapeDtypeStruct((n, w), dt))
    outs = [(pl.BlockSpec((tm, TAIL_W), lambda i: (i, 0)), jax.ShapeDtypeStruct((n, NMAIN), MXU)),
            row(D, MXU), row(VD, F32), row(D, F32), row(D, MXU), row(D, MXU), row(D, MXU),
            (_const((8, VD)), jax.ShapeDtypeStruct((8, VD), F32))]
    rowin = lambda w: pl.BlockSpec((tm, w), lambda i: (i, 0))
    return pl.pallas_call(
        body, name="tail_bwd", grid=(n // tm,),
        in_specs=[_proj_spec(tm, C_BA, 1024), _proj_spec(tm, C_ZA, 1024),
                  _proj_spec(tm, C_O, 2048), _proj_spec(tm, C_ZB, 2048),
                  _proj_spec(tm, C_GA, 1024), _proj_spec(tm, C_GB, 1024),
                  _proj_spec(tm, C_XA, 1024), _proj_spec(tm, C_CA, 1024),
                  _halo_prev(tm, C_XA), _halo_prev(tm, C_CA),
                  rowin(VD), rowin(D), rowin(D), rowin(D), rowin(D), rowin(D),
                  _const((8, D)), _const((1, D)), _const((1, VD)), _const((1, D)),
                  _const((D, D)), _const((D, D)), _const((VD, D)), _const((D, D))],
        out_specs=[s for s, _ in outs],
        out_shape=[s for _, s in outs],
        compiler_params=_cparams("arbitrary"),
    )(*([proj] * 10), h, dgp, dx2, x1, ya, yb, cw, cb, gh, gple, wpg, wo, wb, wa)


def _conv_bwd(proj, dcv, cw, dproj):
    n = dcv.shape[0]
    tm = min(512, n)
    nt = n // tm

    def body(xa_r, ca_r, dcv_r, nxt_r, cw_r, dproj_in, dxc_o):
        del dproj_in
        i = pl.program_id(0)
        dcv_v = _f32(dcv_r)
        nxt = jnp.where(i < nt - 1, _f32(nxt_r), 0.0)
        rid = lax.broadcasted_iota(jnp.int32, dcv_v.shape, 0)
        d1 = jnp.where(rid == tm - 1, nxt[0:1, :], pltpu.roll(dcv_v, tm - 1, 0))
        d2 = jnp.where(rid == tm - 2, nxt[0:1, :],
                       jnp.where(rid == tm - 1, nxt[1:2, :], pltpu.roll(dcv_v, tm - 2, 0)))
        cwv = cw_r[...]
        du = cwv[2:3, :] * dcv_v + cwv[1:2, :] * d1 + cwv[0:1, :] * d2
        dxc_o[:, 0:D] = (du * _f32(ca_r)).astype(MXU)
        dxc_o[:, D:2 * D] = (du * _f32(xa_r)).astype(MXU)

    return pl.pallas_call(
        body, name="conv_bwd", grid=(nt,),
        in_specs=[_proj_spec(tm, C_XA, 1024), _proj_spec(tm, C_CA, 1024),
                  pl.BlockSpec((tm, D), lambda i: (i, 0)),
                  pl.BlockSpec((8, D), lambda i: (jnp.minimum((i + 1) * (tm // 8), n // 8 - 1), 0)),
                  _const((8, D)), ANY],
        out_specs=pl.BlockSpec((tm, CONV_W), lambda i: (i, C_XA // CONV_W)),
        out_shape=jax.ShapeDtypeStruct(dproj.shape, dproj.dtype),
        input_output_aliases={5: 0},
        compiler_params=_cparams("arbitrary"),
    )(proj, proj, dcv, dcv, cw, dproj)


def _position():
    return lax.axis_index("x"), lax.axis_index("y"), lax.axis_index("c")


def _flip(v, bit):
    return 1 - v if bit else v


def _part_start(core, stride, tiled):
    return pl.multiple_of(core * stride, stride) if tiled else core * stride


def _scatter_copies(srcs, dsts, strides, send_sems, recv_sems, local_sems):
    x, y, c = _position()
    me = 4 * x + 2 * y + c
    na = len(srcs)
    copies = []
    for r in range(N_DEV):
        px, py, pc = _flip(x, (r >> 2) & 1), _flip(y, (r >> 1) & 1), _flip(c, r & 1)
        for a in range(na):
            rows = dsts[a].shape[1]
            src = srcs[a].at[2 * px + py, pl.ds(_part_start(pc, strides[a], len(dsts[a].shape) == 3), rows)]
            dst = dsts[a].at[me]
            if r == 0:
                copies.append(pltpu.make_async_copy(src, dst, local_sems.at[a]))
            else:
                k = (r - 1) * na + a
                copies.append(pltpu.make_async_remote_copy(
                    src_ref=src, dst_ref=dst, send_sem=send_sems.at[k], recv_sem=recv_sems.at[k],
                    device_id=(px, py, pc), device_id_type=MESH))
    return copies


def _input_grad(dproj, dgates, w_t, wg_t, x, dx1, g_mix, g_in, g_rest):
    n = x.shape[0]
    tm, tk = min(1024, n), 2048
    nk = NMAIN // tk
    nt = n // tm

    def body(dp_r, dg_r, w_r, wg_r, x_r, dx1_r, g_r, gin, grest,
             gx_o, acc_o, oin, orest, acc, send_sems, recv_sems, local_sems):
        i = pl.program_id(0)
        kk = pl.program_id(1)
        copies = _scatter_copies((gin, grest), (oin, orest), strides, send_sems, recv_sems, local_sems)

        @pl.when((i == 0) & (kk == 0))
        def _():
            acc_o[...] = jnp.zeros_like(acc_o)
            for cp in copies:
                cp.start()

        @pl.when(kk == 0)
        def _():
            acc[...] = _dot(dg_r[...].astype(MXU), wg_r[...])

        acc[...] += _dot(dp_r[...], w_r[...])

        @pl.when(kk == nk - 1)
        def _():
            dhn = acc[...]
            xv = x_r[...]
            r0 = _rstd(xv)
            xh = xv * r0
            acc_o[0:1, :] += jnp.sum(dhn * xh, axis=0, keepdims=True)
            gx_o[...] = dx1_r[...] + _norm_bwd(dhn, xh, r0, g_r[...])

        @pl.when((i == nt - 1) & (kk == nk - 1))
        def _():
            for cp in copies:
                cp.wait()

    nrem = 2 * (N_DEV - 1)
    r_in, r_rest = HALF_SLABS, g_rest.shape[1] // 2
    strides = (HALF_START, r_rest)
    return pl.pallas_call(
        body, name="input_grad", grid=(nt, nk),
        in_specs=[pl.BlockSpec((tm, tk), lambda i, kk: (i, kk)),
                  pl.BlockSpec((tm, GATE_W), lambda i, kk: (i, 0)),
                  pl.BlockSpec((tk, D), lambda i, kk: (kk, 0)),
                  pl.BlockSpec((GATE_W, D), lambda i, kk: (0, 0)),
                  pl.BlockSpec((tm, D), lambda i, kk: (i, 0)),
                  pl.BlockSpec((tm, D), lambda i, kk: (i, 0)),
                  pl.BlockSpec((1, D), lambda i, kk: (0, 0)),
                  ANY, ANY],
        out_specs=[pl.BlockSpec((tm, D), lambda i, kk: (i, 0)),
                   pl.BlockSpec((8, D), lambda i, kk: (0, 0)),
                   ANY, ANY],
        out_shape=[jax.ShapeDtypeStruct((n, D), F32), jax.ShapeDtypeStruct((8, D), F32),
                   jax.ShapeDtypeStruct((N_DEV, r_in) + g_in.shape[2:], g_in.dtype),
                   jax.ShapeDtypeStruct((N_DEV, r_rest) + g_rest.shape[2:], g_rest.dtype)],
        scratch_shapes=[pltpu.VMEM((tm, D), F32),
                        pltpu.SemaphoreType.DMA((nrem,)), pltpu.SemaphoreType.DMA((nrem,)),
                        pltpu.SemaphoreType.DMA((2,))],
        compiler_params=_cparams("arbitrary", "arbitrary"),
    )(dproj, dgates, w_t, wg_t, x, dx1, g_mix, g_in, g_rest)


def _pack_small(acc_f, acc_b, acc_x, gsum):
    def body(f_r, b_r, x_r, s_r, o_r):
        o_r[...] = jnp.zeros_like(o_r)
        o_r[0:1, :] = x_r[0:1, :]
        o_r[1:2, :] = b_r[2:3, 0:D]
        o_r[2:3, :] = b_r[1:2, 0:D]
        o_r[3:4, :] = b_r[1:2, D:2 * D]
        o_r[4:5, :] = b_r[0:1, 0:D]
        o_r[5:6, :] = f_r[0:1, :]
        o_r[6:9, :] = b_r[3:6, 0:D]
        lane = lax.broadcasted_iota(jnp.int32, (1, D), 1)
        o_r[9:10, :] = jnp.where(lane < 2 * NH, s_r[0:1, :], jnp.where(lane == 2 * NH, f_r[1:2, :], 0.0))

    return pl.pallas_call(
        body, name="pack_small",
        out_shape=jax.ShapeDtypeStruct((SMALL_ROWS, D), F32),
    )(acc_f, acc_b, acc_x, gsum)


def _sum_slots(r, name, tr=64, half=None):
    s, rows, w = r.shape
    tr = min(tr, rows)
    assert rows % tr == 0
    nt = rows // tr

    def body(*refs):
        r_ref, o_ref = refs[-2:]
        tot = r_ref[0].astype(F32)
        for k in range(1, s):
            tot = tot + r_ref[k].astype(F32)
        o_ref[...] = tot

    if half is None:
        return pl.pallas_call(
            body, name=name, grid=(nt,),
            in_specs=[pl.BlockSpec((s, tr, w), lambda i: (0, i, 0))],
            out_specs=pl.BlockSpec((tr, w), lambda i: (i, 0)),
            out_shape=jax.ShapeDtypeStruct((rows, w), F32),
            compiler_params=_cparams("arbitrary"),
        )(r)
    return pl.pallas_call(
        body, name=name,
        grid_spec=pltpu.PrefetchScalarGridSpec(
            num_scalar_prefetch=1, grid=(nt,),
            in_specs=[pl.BlockSpec((s, tr, w), lambda i, hf: (0, i, 0))],
            out_specs=pl.BlockSpec((tr, w), lambda i, hf: (hf[0] * nt + i, 0))),
        out_shape=jax.ShapeDtypeStruct((2 * rows, w), F32),
        compiler_params=_cparams("arbitrary"),
    )(half, r)


def _cast_slabs(a, tr=163):
    rows = a.shape[0]
    assert rows % tr == 0

    def body(a_ref, o_ref):
        o_ref[...] = a_ref[...].astype(MXU)

    spec = pl.BlockSpec((tr,) + a.shape[1:], lambda i: (i, 0, 0))
    return pl.pallas_call(
        body, name="cast_w_in", grid=(rows // tr,), in_specs=[spec], out_specs=spec,
        out_shape=jax.ShapeDtypeStruct(a.shape, MXU), compiler_params=_cparams("arbitrary"),
    )(a)


def _sum_slabs(r, name, tr=69):
    s, rows = r.shape[:2]
    assert rows % tr == 0

    def body(r_ref, o_ref):
        tot = r_ref[0].astype(F32)
        for k in range(1, s):
            tot = tot + r_ref[k].astype(F32)
        o_ref[...] = tot

    return pl.pallas_call(
        body, name=name, grid=(rows // tr,),
        in_specs=[pl.BlockSpec((s, tr) + r.shape[2:], lambda i: (0, i, 0, 0))],
        out_specs=pl.BlockSpec((tr,) + r.shape[2:], lambda i: (i, 0, 0)),
        out_shape=jax.ShapeDtypeStruct(r.shape[1:], F32),
        compiler_params=_cparams("arbitrary"),
    )(r)


def _adamw_body(w_r, g_r, m_r, v_r, d_o, m_o, v_o):
    c1 = 1.0 - ADAM_B1 ** ADAM_STEP
    c2 = 1.0 - ADAM_B2 ** ADAM_STEP
    gv = g_r[...]
    mn = ADAM_B1 * m_r[...] + (1.0 - ADAM_B1) * gv
    vn = ADAM_B2 * v_r[...] + (1.0 - ADAM_B2) * (gv * gv)
    m_o[...] = mn
    v_o[...] = vn
    d_o[...] = -ADAM_LR * ((mn / c1) / (jnp.sqrt(vn / c2) + ADAM_EPS) + ADAM_WD * w_r[...])


def _adamw_slabs(w, g, m, v, name, tr):
    rows = w.shape[0]
    assert rows % tr == 0 and w.shape == g.shape

    def body(*refs):
        _adamw_body(*refs)

    spec = pl.BlockSpec((tr,) + w.shape[1:], lambda i: (i, 0, 0))
    shp = jax.ShapeDtypeStruct(w.shape, F32)
    return pl.pallas_call(
        body, name=name, grid=(rows // tr,),
        in_specs=[spec] * 4, out_specs=[spec] * 3, out_shape=[shp] * 3,
        compiler_params=_cparams("arbitrary"),
    )(w, g, m, v)


def _adamw(w, g, m, v, name, g_row0=0, tr=64):
    lead = w.ndim == 3
    rows, cols = w.shape[-2:]
    tr = min(tr, rows)
    assert rows % tr == 0 and g_row0 % tr == 0 and g.shape[1] == cols

    def body(*refs):
        _adamw_body(*refs)

    if lead:
        spec = pl.BlockSpec((None, tr, cols), lambda i: (0, i, 0))
    else:
        spec = pl.BlockSpec((tr, cols), lambda i: (i, 0))
    g_spec = pl.BlockSpec((tr, cols), lambda i: (g_row0 // tr + i, 0))
    shp = jax.ShapeDtypeStruct(w.shape, F32)
    return pl.pallas_call(
        body, name=name, grid=(rows // tr,),
        in_specs=[spec, g_spec, spec, spec], out_specs=[spec] * 3, out_shape=[shp] * 3,
        compiler_params=_cparams("arbitrary"),
    )(w, g, m, v)


def _allgather_chips(halved, whole):
    arrs = [a for a, _ in halved] + list(whole)
    parts = [p for _, p in halved]
    nh, na = len(halved), len(arrs)
    relations = ((1, 0), (0, 1), (1, 1))
    n_ici = 3 * na

    def body(*refs):
        ins, outs = refs[:na], refs[na:2 * na]
        send_sems, recv_sems = refs[2 * na:]
        x, y, c = _position()
        mine = 2 * x + y

        def rows(a, core, stride, size):
            return pl.ds(_part_start(core, stride, len(ins[a].shape) == 2), size)

        def remote(src, dst, k, to):
            return pltpu.make_async_remote_copy(src_ref=src, dst_ref=dst, send_sem=send_sems.at[k],
                                                recv_sem=recv_sems.at[k], device_id=to, device_id_type=MESH)

        first, passed = [], []
        for j, (rx, ry) in enumerate(relations):
            px, py = _flip(x, rx), _flip(y, ry)
            for a in range(na):
                if a < nh:
                    s1, n1, s2, n2 = parts[a]
                    first.append(remote(ins[a].at[rows(a, c, s1, n1)], outs[a].at[mine, rows(a, c, s1, n1)],
                                        j * na + a, (px, py, c)))
                    landed = outs[a].at[2 * px + py, rows(a, c, s2, n2)]
                    passed.append(remote(landed, landed, n_ici + j * nh + a, (x, y, 1 - c)))
                else:
                    first.append(remote(ins[a], outs[a].at[mine], j * na + a, (px, py, c)))
        for cp in first:
            cp.start()
        k = 0
        for j in range(3):
            for a in range(na):
                first[j * na + a].wait_recv()
                if a < nh:
                    passed[k].start()
                    k += 1
        for cp in passed:
            cp.wait_recv()
        for cp in first + passed:
            cp.wait_send()

    nsem = n_ici + 3 * nh
    outs = pl.pallas_call(
        body, name="allgather_weights",
        in_specs=[ANY] * na, out_specs=[ANY] * na,
        out_shape=[jax.ShapeDtypeStruct((N_CHIPS,) + a.shape, a.dtype) for a in arrs],
        scratch_shapes=[pltpu.SemaphoreType.DMA((nsem,)), pltpu.SemaphoreType.DMA((nsem,))],
    )(*arrs)
    chip = 2 * lax.axis_index("x") + lax.axis_index("y")
    return [lax.dynamic_update_slice(o, a[None], (chip,) + (0,) * a.ndim) for o, a in zip(outs, arrs)]


def _join_halves(piece, full_rest, small):
    def body(pc_ref, rest_in, sm, oin, orest, osm, send_sems, recv_sems, local_sem):
        del rest_in
        x, y, c = _position()
        me = 4 * x + 2 * y + c
        sib = (x, y, 1 - c)
        rows = orest.shape[0] // 2
        mine = orest.at[pl.ds(pl.multiple_of(c * rows, rows), rows)]
        copies = [
            pltpu.make_async_remote_copy(
                src_ref=pc_ref.at[pl.ds(c, HALF_START)], dst_ref=oin.at[pl.ds(c * HALF_SLABS, HALF_START)],
                send_sem=send_sems.at[0], recv_sem=recv_sems.at[0], device_id=sib, device_id_type=MESH),
            pltpu.make_async_remote_copy(
                src_ref=mine, dst_ref=mine, send_sem=send_sems.at[1], recv_sem=recv_sems.at[1],
                device_id=sib, device_id_type=MESH)]
        na = 2
        copies.append(pltpu.make_async_copy(sm, osm.at[me], local_sem))
        for r in range(1, N_DEV):
            to = (_flip(x, (r >> 2) & 1), _flip(y, (r >> 1) & 1), _flip(c, r & 1))
            copies.append(pltpu.make_async_remote_copy(
                src_ref=sm, dst_ref=osm.at[me], send_sem=send_sems.at[na + r - 1],
                recv_sem=recv_sems.at[na + r - 1], device_id=to, device_id_type=MESH))
        for cp in copies:
            cp.start()
        for cp in copies:
            cp.wait()

    nsem = 2 + N_DEV - 1
    return pl.pallas_call(
        body, name="join_halves",
        in_specs=[ANY] * 3, out_specs=[ANY] * 3,
        out_shape=[jax.ShapeDtypeStruct((SLABS_CHIP,) + piece.shape[1:], piece.dtype),
                   jax.ShapeDtypeStruct(full_rest.shape, full_rest.dtype),
                   jax.ShapeDtypeStruct((N_DEV,) + small.shape, small.dtype)],
        scratch_shapes=[pltpu.SemaphoreType.DMA((nsem,)), pltpu.SemaphoreType.DMA((nsem,)),
                        pltpu.SemaphoreType.DMA],
        input_output_aliases={1: 1},
    )(piece, full_rest, small)


def _to_internal(w_slabs):
    order = sorted(SEGMENTS, key=lambda s: s[2])
    main = jnp.concatenate([w_slabs[g0 // 2:(g0 + w) // 2] for g0, w, _ in order], axis=0)
    gate = w_slabs[GATE_COL // 2:GATE_COL // 2 + NH].reshape(2 * NH, D)
    return main.reshape(NMAIN, D), jnp.pad(gate, ((0, GATE_W - 2 * NH), (0, 0)))


def _to_global(main_t, gate_t):
    main = main_t.reshape(NMAIN // 2, 16, D // 8)
    parts = sorted([(g0, main[i0 // 2:(i0 + w) // 2]) for g0, w, i0 in SEGMENTS]
                   + [(GATE_COL, gate_t[0:2 * NH].reshape(NH, 16, D // 8))], key=lambda s: s[0])
    return jnp.concatenate([p for _, p in parts], axis=0)


def _pack_rows(wa, wb, wo, wpg, wp):
    return jnp.concatenate([wa, wb, wo, wpg, wp.reshape(PACK_ROWS[4], D)], axis=0)


def _pad_rows(a, rows=8):
    return jnp.pad(a, ((0, rows - a.shape[0]), (0, 0)))


def kernel(x, p, g_mix, w_in, conv_w, conv_b, w_a_out, b_gates, g_head, w_b_out, w_o, g_ple, w_ple_gate, w_ple, g_final, loss_target, m_g_mix, m_w_in, m_conv_w, m_conv_b, m_w_a_out, m_b_gates, m_g_head, m_w_b_out, m_w_o, m_g_ple, m_w_ple_gate, m_w_ple, m_g_final, v_g_mix, v_w_in, v_conv_w, v_conv_b, v_w_a_out, v_b_gates, v_g_head, v_w_b_out, v_w_o, v_g_ple, v_w_ple_gate, v_w_ple, v_g_final):
    chip = 2 * lax.axis_index("x") + lax.axis_index("y")
    xs, ps, ts = x[0], p[0, 0], loss_target[0]
    g_fin = g_final.reshape(1, D)

    pack_w = _pack_rows(w_a_out[0], w_b_out[0], w_o[0], w_ple_gate[0], w_ple[0])
    w_slabs = _cast_slabs(jnp.transpose(w_in, (2, 0, 1)).reshape(SLABS_CHIP, 16, D // 8))
    half_rows = PACK_TOTAL // 2
    g_win, g_pack, g_cw = _allgather_chips(
        [(w_slabs, (HALF_START, HALF_SLABS, HALF_SLABS, HALF_START)),
         (pack_w.astype(MXU), (half_rows, half_rows, half_rows, half_rows))], [_pad_rows(conv_w[0])])
    w_t, wg_t = _to_internal(g_win.reshape(N_IN // 2, 16, D // 8))
    offs = [0, 256, 768, 1024, 1280, 1344]
    wa, wb, wo, wpg = [g_pack[:, offs[k]:offs[k + 1]].reshape(-1, D) for k in range(4)]
    wp = jnp.transpose(g_pack[:, offs[4]:offs[5]].reshape(N_CHIPS, PLE, PLE), (1, 0, 2)).reshape(PLE, D)
    cw = jnp.transpose(g_cw, (1, 0, 2)).reshape(8, D)

    bias = jnp.pad(b_gates, ((0, 0), (0, GATE_W - 2 * NH)))
    hn, gates = _prenorm(xs, g_mix, wg_t)
    proj = _proj(hn, w_t)
    h, stats, cs, ns, ms = _mlstm_fwd(proj, gates, bias)
    (a_pre, b_pre, mg, xn1, de, dgp, ya, yb, x1, dx2, acc_f) = _tail_fwd(
        proj, h, xs, ps, ts, cw, conv_b, g_head, g_ple, g_fin, wa, wb, wo, wpg, wp)
    dproj, dcv, dh, dx1, dx1b, dya, dyb, acc_b = _tail_bwd(
        proj, h, dgp, dx2, x1, ya, yb, cw, conv_b, g_head, g_ple, wpg, wo, wb, wa)
    dproj = _conv_bwd(proj, dcv, cw, dproj)
    dproj, dgates, gsum = _mlstm_bwd(proj, gates, bias, h, dh, stats, cs, ns, ms, dproj)
    d_main = _matmul_tn(dproj, hn, "dw_in", out_dtype=WIRE, tk=2048)
    d_gate = _matmul_tn(dgates, hn, "dw_gate", out_dtype=WIRE)
    d_wa = _matmul_tn(a_pre, dya, "dw_a_out", out_dtype=WIRE)
    d_wb = _matmul_tn(b_pre, dyb, "dw_b_out", out_dtype=WIRE)
    d_wo = _matmul_tn(mg, dx1b, "dw_o", out_dtype=WIRE)
    d_wpg = _matmul_tn(xn1, dgp, "dw_ple_gate", out_dtype=WIRE)
    d_wp = _matmul_tn(ps, de, "dw_ple", out_dtype=WIRE)

    g_in = _to_global(d_main, d_gate).reshape(N_CHIPS, SLABS_CHIP, 16, D // 8)
    d_wp_c = jnp.transpose(d_wp.reshape(PLE, N_CHIPS, PLE), (1, 0, 2)).reshape(N_CHIPS, PACK_ROWS[4], D)
    g_rest = jnp.concatenate(
        [d_wa.reshape(N_CHIPS, -1, D), d_wb.reshape(N_CHIPS, -1, D), d_wo.reshape(N_CHIPS, -1, D),
         d_wpg.reshape(N_CHIPS, -1, D), d_wp_c], axis=1)
    grad_x, acc_x, r_in, r_rest = _input_grad(dproj, dgates, w_t, wg_t, xs, dx1, g_mix, g_in, g_rest)
    small = _pack_small(acc_f, acc_b, acc_x, gsum)
    core = lax.axis_index("c").astype(jnp.int32)
    piece = _sum_slabs(r_in, "sum_w_in")
    gw_in, gw_rest, r_small = _join_halves(
        piece, _sum_slots(r_rest, "sum_rest", tr=96, half=core.reshape(1)), small)
    gw_in = lax.dynamic_update_slice(gw_in, piece, (core * HALF_START, 0, 0)).reshape(COLS_CHIP, 8, D // 8)
    gs = _sum_slots(r_small, "sum_small", tr=SMALL_ROWS)

    big = []
    row0 = 0
    for name, w, m, v in (("w_a_out", w_a_out, m_w_a_out, v_w_a_out), ("w_b_out", w_b_out, m_w_b_out, v_w_b_out),
                          ("w_o", w_o, m_w_o, v_w_o), ("w_ple_gate", w_ple_gate, m_w_ple_gate, v_w_ple_gate)):
        big.append((name, w, m, v, gw_rest, row0))
        row0 += w.shape[1]
    g_wp = gw_rest[row0:row0 + PACK_ROWS[4]].reshape(PLE, PLE)
    big.append(("w_ple", w_ple, m_w_ple, v_w_ple, g_wp, 0))
    upd = {name: _adamw(w, g, m, v, "adamw_" + name, g_row0=r0) for name, w, m, v, g, r0 in big}
    g_big = {name: (g if name == "w_ple" else g[r0:r0 + w.shape[1]])[None] for name, w, m, v, g, r0 in big}
    slabs = lambda a: jnp.transpose(a, (2, 0, 1)).reshape(COLS_CHIP, 8, D // 8)
    unslab = lambda a: jnp.transpose(a, (1, 2, 0)).reshape(1, D, COLS_CHIP)
    upd["w_in"] = [unslab(u) for u in _adamw_slabs(
        slabs(w_in), gw_in, slabs(m_w_in), slabs(v_w_in), "adamw_w_in", tr=326)]
    g_big["w_in"] = unslab(gw_in)

    lane = lax.broadcasted_iota(jnp.int32, (1, D), 1)
    g_small = jnp.concatenate([gs[0:6], jnp.where(lane < 2 * NH, gs[9:10], 0.0), jnp.zeros((1, D), F32)], axis=0)

    def small_pack(gm, cb_, bg, gh, gp, gf):
        return jnp.concatenate([gm, cb_, gh.reshape(2, D), gp, gf.reshape(1, D),
                                jnp.pad(bg, ((0, 0), (0, D - 2 * NH))), jnp.zeros((1, D), F32)], axis=0)

    ws = small_pack(g_mix, conv_b, b_gates, g_head, g_ple, g_final)
    ms_ = small_pack(m_g_mix, m_conv_b, m_b_gates, m_g_head, m_g_ple, m_g_final)
    vs = small_pack(v_g_mix, v_conv_b, v_b_gates, v_g_head, v_g_ple, v_g_final)
    upd_s = _adamw(ws, g_small, ms_, vs, "adamw_small")
    g_cw_mine = _pad_rows(lax.dynamic_slice(gs[6:9], (0, chip * PLE), (3, PLE)))
    upd_c = _adamw(_pad_rows(conv_w[0]), g_cw_mine, _pad_rows(m_conv_w[0]), _pad_rows(v_conv_w[0]),
                   "adamw_conv_w")

    def leaves(bigs, sm, cwv):
        return [sm[0:1], bigs["w_in"], cwv[0:3][None], sm[1:2], bigs["w_a_out"], sm[6:7, 0:2 * NH],
                sm[2:4].reshape(1, VD), bigs["w_b_out"], bigs["w_o"], sm[4:5], bigs["w_ple_gate"],
                bigs["w_ple"], sm[5]]

    loss = gs[9, 2 * NH]
    outs = [loss, grad_x[None]] + leaves(g_big, g_small, g_cw_mine)
    for k in range(3):
        outs += leaves({name: u[k] for name, u in upd.items()}, upd_s[k], upd_c[k])
    return tuple(outs)
```

```python
import jax
import jax.numpy as jnp
from jax import lax
from jax.experimental import pallas as pl
from jax.experimental.pallas import tpu as pltpu

F32 = jnp.float32
MXU = jnp.bfloat16
WIRE = jnp.bfloat16

D = 1024
NH, DK, DV = 4, 256, 512
VD = NH * DV
PLE = 256
LCH = 256
EPS = 1e-6
N_IN = 14344
NMAIN = 14336
GATE_W = 128
N_CHIPS, N_DEV = 4, 8

C_BA, C_ZA, C_O, C_ZB, C_GA, C_GB = 0, 1024, 2048, 4096, 6144, 7168
QK = NH * DK
C_Q, C_K, C_V, C_XA, C_CA = 8192, 9216, 10240, 12288, 13312
QKV_W = 2 * QK + VD
TAIL_W = 8192
CONV_W = 2048
SEGMENTS = (
    (0, 1024, C_XA), (1024, 1024, C_BA), (2048, 1024, C_CA), (3072, 1024, C_ZA),
    (4096, QKV_W, C_Q), (8192, 4096, C_O), (12296, 2048, C_GA),
)
GATE_COL = 12288

COLS_CHIP = N_IN // N_CHIPS
SLABS_CHIP = COLS_CHIP // 2
HALF_START = SLABS_CHIP // 2
HALF_SLABS = SLABS_CHIP - HALF_START

PACK_ROWS = (256, 512, 256, 256, 64)
PACK_TOTAL = sum(PACK_ROWS)
SMALL_ROWS = 16

ADAM_LR, ADAM_B1, ADAM_B2, ADAM_EPS, ADAM_WD, ADAM_STEP = 0.001, 0.9, 0.999, 1e-08, 0.01, 10

VMEM_LIMIT = 56 * 1024 * 1024
MESH = pl.DeviceIdType.MESH
ANY = pl.BlockSpec(memory_space=pl.ANY)


def _cparams(*sem):
    return pltpu.CompilerParams(dimension_semantics=sem, vmem_limit_bytes=VMEM_LIMIT)


def _dot(a, b):
    return jnp.dot(a, b, preferred_element_type=F32)


def _dot_nt(a, b):
    return lax.dot_general(a, b, (((1,), (1,)), ((), ())), preferred_element_type=F32)


def _dot_tn(a, b):
    return lax.dot_general(a, b, (((0,), (0,)), ((), ())), preferred_element_type=F32)


def _sigmoid(x):
    return 1.0 / (1.0 + jnp.exp(-x))


def _logsig(x):
    return jnp.minimum(x, 0.0) - jnp.log(1.0 + jnp.exp(-jnp.abs(x)))


def _gate(v):
    one = jnp.asarray(1.0, v.dtype)
    t = jnp.exp(-jnp.abs(v))
    r = one / (one + t)
    return jnp.where(v >= 0, r, t * r), t * r * r


def _rstd(x):
    return lax.rsqrt(jnp.mean(x * x, axis=-1, keepdims=True) + EPS)


def _norm_bwd(dy, xhat, r, g):
    dxh = dy * g
    return r * (dxh - xhat * jnp.mean(dxh * xhat, axis=-1, keepdims=True))


def _f32(ref):
    return ref[...].astype(F32)


def _prenorm(x, g_mix, wg_t):
    n = x.shape[0]
    tm = min(512, n)

    def body(x_ref, g_ref, wg_ref, hn_ref, gate_ref):
        xv = x_ref[...]
        hn = (xv * _rstd(xv) * g_ref[...]).astype(MXU)
        hn_ref[...] = hn
        gate_ref[...] = _dot_nt(hn, wg_ref[...])

    return pl.pallas_call(
        body, name="prenorm", grid=(n // tm,),
        in_specs=[pl.BlockSpec((tm, D), lambda i: (i, 0)),
                  pl.BlockSpec((1, D), lambda i: (0, 0)),
                  pl.BlockSpec((GATE_W, D), lambda i: (0, 0))],
        out_specs=[pl.BlockSpec((tm, D), lambda i: (i, 0)),
                   pl.BlockSpec((tm, GATE_W), lambda i: (i, 0))],
        out_shape=[jax.ShapeDtypeStruct((n, D), MXU), jax.ShapeDtypeStruct((n, GATE_W), F32)],
        compiler_params=_cparams("arbitrary"),
    )(x, g_mix, wg_t)


def _proj(hn, w_t, tm=2048, tn=1024):
    n, k = hn.shape
    m = w_t.shape[0]
    tm = min(tm, n)

    def body(a_ref, b_ref, o_ref):
        o_ref[...] = _dot_nt(a_ref[...], b_ref[...]).astype(MXU)

    return pl.pallas_call(
        body, name="proj", grid=(m // tn, n // tm),
        in_specs=[pl.BlockSpec((tm, k), lambda j, i: (i, 0)),
                  pl.BlockSpec((tn, k), lambda j, i: (j, 0))],
        out_specs=pl.BlockSpec((tm, tn), lambda j, i: (i, j)),
        out_shape=jax.ShapeDtypeStruct((n, m), MXU),
        compiler_params=_cparams("arbitrary", "arbitrary"),
    )(hn, w_t)


def _matmul_tn(a, b, name, out_dtype=F32, ta=1024, tb=1024, tk=1024):
    n, ka = a.shape
    kb = b.shape[1]
    ta, tb, tk = min(ta, ka), min(tb, kb), min(tk, n)
    nk = n // tk

    def body(a_ref, b_ref, o_ref, acc):
        kk = pl.program_id(2)

        @pl.when(kk == 0)
        def _():
            acc[...] = jnp.zeros_like(acc)

        acc[...] += _dot_tn(a_ref[...].astype(MXU), b_ref[...].astype(MXU))

        @pl.when(kk == nk - 1)
        def _():
            o_ref[...] = acc[...].astype(out_dtype)

    return pl.pallas_call(
        body, name=name, grid=(ka // ta, kb // tb, nk),
        in_specs=[pl.BlockSpec((tk, ta), lambda i, j, kk: (kk, i)),
                  pl.BlockSpec((tk, tb), lambda i, j, kk: (kk, j))],
        out_specs=pl.BlockSpec((ta, tb), lambda i, j, kk: (i, j)),
        out_shape=jax.ShapeDtypeStruct((ka, kb), out_dtype),
        scratch_shapes=[pltpu.VMEM((ta, tb), F32)],
        compiler_params=_cparams("arbitrary", "arbitrary", "arbitrary"),
    )(a, b)


def _gate_vectors(g, gt, hd):
    lane = lax.broadcasted_iota(jnp.int32, g.shape, 1)
    sub = lax.broadcasted_iota(jnp.int32, gt.shape, 0)
    col = lambda j: jnp.sum(jnp.where(lane == j, g, 0.0), axis=1, keepdims=True)
    row = lambda j: jnp.sum(jnp.where(sub == j, gt, 0.0), axis=0, keepdims=True)
    return col(hd), col(hd + NH), row(hd), row(hd + NH)


def _chunk_decay(li_col, li_row, lf_col, lf_row, m_prev):
    n = li_col.shape[0]
    r = lax.broadcasted_iota(jnp.int32, (n, n), 0)
    c = lax.broadcasted_iota(jnp.int32, (n, n), 1)
    tri = r >= c
    b_col = jnp.sum(jnp.where(tri, lf_row, 0.0), axis=1, keepdims=True)
    b_row = jnp.sum(jnp.where(r <= c, lf_col, 0.0), axis=0, keepdims=True)
    b_last = jnp.sum(lf_row, axis=1, keepdims=True)
    dmat = jnp.where(tri, b_col - b_row + li_row, -jnp.inf)
    a_col = b_col + m_prev
    g_col = b_last - b_col + li_col
    m_new = jnp.maximum(b_last + m_prev, jnp.max(g_col, axis=0, keepdims=True))
    w_col = jnp.exp(g_col - m_new)
    decay = jnp.exp(b_last + m_prev - m_new)
    return tri, dmat, a_col, m_new, w_col, decay


def _qkv_specs(row_of):
    q_spec = pl.BlockSpec((LCH, QK), lambda c: (row_of(c), C_Q // QK))
    k_spec = pl.BlockSpec((LCH, QK), lambda c: (row_of(c), C_K // QK))
    v_spec = pl.BlockSpec((LCH, VD), lambda c: (row_of(c), C_V // VD))
    return q_spec, k_spec, v_spec


def _state_specs(row_of):
    return [pl.BlockSpec((NH, None, DK, DV), lambda c: (0, row_of(c), 0, 0)),
            pl.BlockSpec((NH, None, 1, DK), lambda c: (0, row_of(c), 0, 0)),
            pl.BlockSpec((NH, None, 1, GATE_W), lambda c: (0, row_of(c), 0, 0))]


def _lane_put(col, lane_id, width=GATE_W):
    lane = lax.broadcasted_iota(jnp.int32, (col.shape[0], width), 1)
    return jnp.where(lane == lane_id, col, 0.0)


def _lane_get(block, lane_id):
    lane = lax.broadcasted_iota(jnp.int32, block.shape, 1)
    return jnp.sum(jnp.where(lane == lane_id, block, 0.0), axis=1, keepdims=True)


def _mlstm_fwd(proj, gates, bias):
    n = proj.shape[0]
    nc = n // LCH

    def body(q_ref, k_ref, v_ref, g_ref, bias_ref,
             h_ref, st_ref, cs_ref, ns_ref, ms_ref, c_scr, n_scr, m_scr):
        @pl.when(pl.program_id(0) == 0)
        def _():
            c_scr[...] = jnp.zeros_like(c_scr)
            n_scr[...] = jnp.zeros_like(n_scr)
            m_scr[...] = jnp.full_like(m_scr, -jnp.inf)

        g = g_ref[...] + bias_ref[...]
        gt = g.T[0:8, :]
        stats = jnp.zeros((LCH, GATE_W), F32)
        for hd in range(NH):
            li_col, fr_col, li_row, fr_row = _gate_vectors(g, gt, hd)
            m_all = m_scr[hd]
            m_prev = m_all[0:1, 0:1]
            tri, dmat, a_col, m_new, w_col, decay = _chunk_decay(
                li_col, li_row, _logsig(fr_col), _logsig(fr_row), m_prev)
            m_col = jnp.maximum(a_col, jnp.max(dmat, axis=1, keepdims=True))
            dl = jnp.exp(dmat - m_col)
            inter = jnp.exp(a_col - m_col)

            qs = q_ref[:, hd * DK:(hd + 1) * DK] * (DK ** -0.5)
            kk = k_ref[:, hd * DK:(hd + 1) * DK]
            vv = v_ref[:, hd * DV:(hd + 1) * DV]
            cst = c_scr[hd]
            nst = n_scr[hd]
            cs_ref[hd] = cst
            ns_ref[hd] = nst
            ms_ref[hd] = m_all

            sc = _dot_nt(qs, kk) * dl
            num = _dot(sc.astype(MXU), vv) + inter * _dot(qs, cst.astype(MXU))
            den = (jnp.sum(sc, axis=1, keepdims=True)
                   + inter * jnp.sum(qs.astype(F32) * nst, axis=1, keepdims=True))
            nrm = jnp.maximum(jnp.abs(den), jnp.exp(-m_col))
            h_ref[:, hd * DV:(hd + 1) * DV] = (num / nrm).astype(h_ref.dtype)
            stats = stats + _lane_put(den, hd) + _lane_put(m_col, NH + hd)

            kw = kk.astype(F32) * w_col
            c_scr[hd] = decay * cst + _dot_tn(kw.astype(MXU), vv)
            n_scr[hd] = decay * nst + jnp.sum(kw, axis=0, keepdims=True)
            m_scr[hd] = jnp.broadcast_to(m_new, (1, GATE_W))
        st_ref[...] = stats

    q_spec, k_spec, v_spec = _qkv_specs(lambda c: c)
    return pl.pallas_call(
        body, name="mlstm_fwd", grid=(nc,),
        in_specs=[q_spec, k_spec, v_spec,
                  pl.BlockSpec((LCH, GATE_W), lambda c: (c, 0)),
                  pl.BlockSpec((1, GATE_W), lambda c: (0, 0))],
        out_specs=[pl.BlockSpec((LCH, VD), lambda c: (c, 0)),
                   pl.BlockSpec((LCH, GATE_W), lambda c: (c, 0))] + _state_specs(lambda c: c),
        out_shape=[jax.ShapeDtypeStruct((n, VD), MXU),
                   jax.ShapeDtypeStruct((n, GATE_W), F32),
                   jax.ShapeDtypeStruct((NH, nc, DK, DV), F32),
                   jax.ShapeDtypeStruct((NH, nc, 1, DK), F32),
                   jax.ShapeDtypeStruct((NH, nc, 1, GATE_W), F32)],
        scratch_shapes=[pltpu.VMEM((NH, DK, DV), F32), pltpu.VMEM((NH, 1, DK), F32),
                        pltpu.VMEM((NH, 1, GATE_W), F32)],
        compiler_params=_cparams("arbitrary"),
    )(proj, proj, proj, gates, bias)


def _mlstm_bwd(proj, gates, bias, h, dh, stats, cs, ns, ms, dproj):
    n = proj.shape[0]
    nc = n // LCH

    def body(q_ref, k_ref, v_ref, g_ref, bias_ref, h_ref, dh_ref, st_ref,
             cs_ref, ns_ref, ms_ref, dproj_in,
             dqkv_ref, dg_ref, gsum_ref, dc_scr, dn_scr):
        del dproj_in

        @pl.when(pl.program_id(0) == 0)
        def _():
            dc_scr[...] = jnp.zeros_like(dc_scr)
            dn_scr[...] = jnp.zeros_like(dn_scr)
            gsum_ref[...] = jnp.zeros_like(gsum_ref)

        g = g_ref[...] + bias_ref[...]
        gt = g.T[0:8, :]
        stats = st_ref[...]
        r = lax.broadcasted_iota(jnp.int32, (LCH, LCH), 0)
        c = lax.broadcasted_iota(jnp.int32, (LCH, LCH), 1)
        eye = r == c
        to_col = lambda row: jnp.sum(jnp.where(eye, row, 0.0), axis=1, keepdims=True)
        last = lax.broadcasted_iota(jnp.int32, (LCH, 1), 0) == LCH - 1
        dg = jnp.zeros((LCH, GATE_W), F32)
        for hd in range(NH):
            li_col, fr_col, li_row, fr_row = _gate_vectors(g, gt, hd)
            m_prev = ms_ref[hd][0:1, 0:1]
            tri, dmat, a_col, m_new, w_col, decay = _chunk_decay(
                li_col, li_row, _logsig(fr_col), _logsig(fr_row), m_prev)
            m_col = _lane_get(stats, NH + hd)
            dl = jnp.exp(dmat - m_col)
            inter = jnp.exp(a_col - m_col)

            qs = q_ref[:, hd * DK:(hd + 1) * DK] * (DK ** -0.5)
            kk = k_ref[:, hd * DK:(hd + 1) * DK]
            vv = v_ref[:, hd * DV:(hd + 1) * DV]
            qf = qs.astype(F32)
            kf = kk.astype(F32)
            cst = cs_ref[hd]
            nst = ns_ref[hd]
            cb = cst.astype(MXU)
            dcn = dc_scr[hd]
            dnn = dn_scr[hd]
            dcb = dcn.astype(MXU)

            den = _lane_get(stats, hd)
            floor = jnp.exp(-m_col)
            nrm = jnp.maximum(jnp.abs(den), floor)
            dhv = dh_ref[:, hd * DV:(hd + 1) * DV].astype(F32)
            dnum = dhv / nrm
            dnum_b = dnum.astype(MXU)
            dhh = jnp.sum(dhv * h_ref[:, hd * DV:(hd + 1) * DV].astype(F32), axis=1, keepdims=True)
            dden = jnp.where(jnp.abs(den) > floor, -dhh / nrm * jnp.sign(den), 0.0)

            sc = _dot_nt(qs, kk) * dl
            dsc = _dot_nt(dnum_b, vv) + dden
            da = (dl * dsc).astype(MXU)
            gmat = sc * dsc

            dq = _dot(da, kk) + inter * (_dot_nt(dnum_b, cb) + dden * nst)
            dk_state = w_col * (_dot_nt(vv, dcb) + dnn)
            dk = _dot_tn(da, qs) + dk_state
            kw = (kf * w_col).astype(MXU)
            dv = _dot_tn(sc.astype(MXU), dnum_b) + _dot(kw, dcb)
            dqkv_ref[:, hd * DK:(hd + 1) * DK] = (dq * (DK ** -0.5)).astype(dqkv_ref.dtype)
            dqkv_ref[:, QK + hd * DK:QK + (hd + 1) * DK] = dk.astype(dqkv_ref.dtype)
            dqkv_ref[:, 2 * QK + hd * DV:2 * QK + (hd + 1) * DV] = dv.astype(dqkv_ref.dtype)

            num_i = _dot(qs, cb)
            den_i = jnp.sum(qf * nst, axis=1, keepdims=True)
            e_col = inter * (jnp.sum(dnum * num_i, axis=1, keepdims=True) + dden * den_i)
            h_col = jnp.sum(kf * dk_state, axis=1, keepdims=True)
            f_dec = decay * (jnp.sum(jnp.sum(cst * dcn, axis=1, keepdims=True), axis=0, keepdims=True)
                             + jnp.sum(nst * dnn, axis=1, keepdims=True))
            row_g = jnp.sum(gmat, axis=1, keepdims=True)
            col_g = to_col(jnp.sum(gmat, axis=0, keepdims=True))
            db_col = row_g - col_g + e_col - h_col
            db_col = db_col + jnp.where(last, jnp.sum(h_col, axis=0, keepdims=True) + f_dec, 0.0)
            dli_col = col_g + h_col
            dlf_row = jnp.sum(jnp.where(tri, db_col, 0.0), axis=0, keepdims=True)
            df_col = to_col(dlf_row) * _sigmoid(-fr_col)
            dg = dg + _lane_put(dli_col, hd) + _lane_put(df_col, NH + hd)

            dc_scr[hd] = decay * dcn + _dot_tn((qf * inter).astype(MXU), dnum_b)
            dn_scr[hd] = decay * dnn + jnp.sum(qf * (inter * dden), axis=0, keepdims=True)
        dg_ref[...] = dg
        gsum_ref[0:1, 0:GATE_W] += jnp.sum(dg, axis=0, keepdims=True)

    rev = lambda c: nc - 1 - c
    q_spec, k_spec, v_spec = _qkv_specs(rev)
    hv_spec = pl.BlockSpec((LCH, VD), lambda c: (rev(c), 0))
    gate_spec = pl.BlockSpec((LCH, GATE_W), lambda c: (rev(c), 0))
    return pl.pallas_call(
        body, name="mlstm_bwd", grid=(nc,),
        in_specs=[q_spec, k_spec, v_spec, gate_spec,
                  pl.BlockSpec((1, GATE_W), lambda c: (0, 0)),
                  hv_spec, hv_spec, gate_spec] + _state_specs(rev) + [ANY],
        out_specs=[pl.BlockSpec((LCH, QKV_W), lambda c: (rev(c), C_Q // QKV_W)),
                   gate_spec,
                   pl.BlockSpec((8, D), lambda c: (0, 0))],
        out_shape=[jax.ShapeDtypeStruct(dproj.shape, dproj.dtype),
                   jax.ShapeDtypeStruct((n, GATE_W), F32),
                   jax.ShapeDtypeStruct((8, D), F32)],
        scratch_shapes=[pltpu.VMEM((NH, DK, DV), F32), pltpu.VMEM((NH, 1, DK), F32)],
        input_output_aliases={11: 0},
        compiler_params=_cparams("arbitrary"),
    )(proj, proj, proj, gates, bias, h, dh, stats, cs, ns, ms, dproj)


def _proj_spec(tm, col, width):
    return pl.BlockSpec((tm, width), lambda i: (i, col // width))


def _halo_prev(tm, col):
    return pl.BlockSpec((8, 1024), lambda i: (jnp.maximum(i * (tm // 8) - 1, 0), col // 1024))


def _const(shape):
    return pl.BlockSpec(shape, lambda i: (0,) * len(shape))


def _conv_inputs(i, tm, xa_ref, ca_ref, xah_ref, cah_ref):
    u = _f32(xa_ref) * _f32(ca_ref)
    uh = jnp.where(i > 0, _f32(xah_ref) * _f32(cah_ref), 0.0)
    rid = lax.broadcasted_iota(jnp.int32, u.shape, 0)
    u1 = jnp.where(rid == 0, uh[7:8, :], pltpu.roll(u, 1, 0))
    u2 = jnp.where(rid == 0, uh[6:7, :], jnp.where(rid == 1, uh[7:8, :], pltpu.roll(u, 2, 0)))
    return u, u1, u2


def _head_norm(hh, gh):
    out = []
    for j in range(NH):
        hj = hh[:, j * DV:(j + 1) * DV]
        rj = _rstd(hj)
        out.append((hj * rj, rj, gh[:, j * DV:(j + 1) * DV]))
    return out


def _tail_fwd(proj, h, x, p, t, cw, cb, gh, gple, gfin, wa, wb, wo, wpg, wp):
    n = x.shape[0]
    tm = min(256, n)

    def body(ba_r, za_r, o_r, zb_r, ga_r, gb_r, xa_r, ca_r, xah_r, cah_r, h_r, x_r, p_r, t_r,
             cw_r, cb_r, gh_r, gple_r, gfin_r, wa_r, wb_r, wo_r, wpg_r, wp_r,
             apre_o, bpre_o, mg_o, xn1_o, de_o, dgp_o, ya_o, yb_o, x1_o, dx2_o, acc_o):
        i = pl.program_id(0)

        @pl.when(i == 0)
        def _():
            acc_o[...] = jnp.zeros_like(acc_o)

        u, u1, u2 = _conv_inputs(i, tm, xa_r, ca_r, xah_r, cah_r)
        cwv = cw_r[...]
        cv = cwv[0:1, :] * u2 + cwv[1:2, :] * u1 + cwv[2:3, :] * u + cb_r[...]
        za = za_r[...]
        a_pre = ba_r[...] * cv.astype(MXU) * (za * _gate(za)[0])
        apre_o[...] = a_pre
        ya = _dot(a_pre, wa_r[...]).astype(MXU)

        hn = _head_norm(_f32(h_r), gh_r[...])
        hbn = jnp.concatenate([xh * g for xh, _, g in hn], axis=1)
        zb = zb_r[...]
        b_pre =_gate(o_r[...])[0] * hbn.astype(MXU) * (zb * _gate(zb)[0])
        bpre_o[...] = b_pre
        yb = _dot(b_pre, wb_r[...]).astype(MXU)
        ya_o[...] = ya
        yb_o[...] = yb

        mg = _gate(ga_r[...])[0] * ya + _gate(gb_r[...])[0] * yb
        mg_o[...] = mg
        x1 = x_r[...] + _dot(mg, wo_r[...])
        x1_o[...] = x1.astype(MXU)
        xn1 =(x1 * _rstd(x1) * gple_r[...]).astype(MXU)
        xn1_o[...] = xn1
        gt = _sigmoid(_dot(xn1, wpg_r[...]))
        e = _dot(p_r[...].astype(MXU), wp_r[...])
        x2 = x1 + gt * e
        r2 = _rstd(x2)
        xh2 = x2 * r2
        gf = gfin_r[...]
        diff = xh2 * gf - t_r[...]
        dy = diff * (1.0 / D)
        dx2 = _norm_bwd(dy, xh2, r2, gf)
        dx2_o[...] = dx2
        de_o[...] = (dx2 * gt).astype(MXU)
        dgp_o[...] = (dx2 * e * gt * (1.0 - gt)).astype(MXU)
        acc_o[0:1, :] += jnp.sum(dy * xh2, axis=0, keepdims=True)
        loss = 0.5 * jnp.sum(jnp.sum(diff * diff, axis=1, keepdims=True), axis=0, keepdims=True) * (1.0 / D)
        acc_o[1:2, :] += jnp.broadcast_to(loss, (1, D))

    row = lambda w, dt: (pl.BlockSpec((tm, w), lambda i: (i, 0)), jax.ShapeDtypeStruct((n, w), dt))
    outs = [row(D, MXU), row(VD, MXU), row(D, MXU), row(D, MXU), row(D, MXU), row(D, MXU),
            row(D, MXU), row(D, MXU), row(D, MXU), row(D, F32),
            (_const((8, D)), jax.ShapeDtypeStruct((8, D), F32))]
    return pl.pallas_call(
        body, name="tail_fwd", grid=(n // tm,),
        in_specs=[_proj_spec(tm, C_BA, 1024), _proj_spec(tm, C_ZA, 1024),
                  _proj_spec(tm, C_O, 2048), _proj_spec(tm, C_ZB, 2048),
                  _proj_spec(tm, C_GA, 1024), _proj_spec(tm, C_GB, 1024),
                  _proj_spec(tm, C_XA, 1024), _proj_spec(tm, C_CA, 1024),
                  _halo_prev(tm, C_XA), _halo_prev(tm, C_CA),
                  pl.BlockSpec((tm, VD), lambda i: (i, 0)),
                  pl.BlockSpec((tm, D), lambda i: (i, 0)),
                  pl.BlockSpec((tm, PLE), lambda i: (i, 0)),
                  pl.BlockSpec((tm, D), lambda i: (i, 0)),
                  _const((8, D)), _const((1, D)), _const((1, VD)), _const((1, D)), _const((1, D)),
                  _const((D, D)), _const((VD, D)), _const((D, D)), _const((D, D)), _const((PLE, D))],
        out_specs=[s for s, _ in outs],
        out_shape=[s for _, s in outs],
        compiler_params=_cparams("arbitrary"),
    )(*([proj] * 10), h, x, p, t, cw, cb, gh, gple, gfin, wa, wb, wo, wpg, wp)


def _tail_bwd(proj, h, dgp, dx2, x1, ya, yb, cw, cb, gh, gple, wpg, wo, wb, wa):
    n = x1.shape[0]
    tm = min(256, n)

    def body(ba_r, za_r, o_r, zb_r, ga_r, gb_r, xa_r, ca_r, xah_r, cah_r, h_r,
             dgp_r, dx2_r, x1_r, ya_r, yb_r, cw_r, cb_r, gh_r, gple_r,
             wpg_r, wo_r, wb_r, wa_r,
             dproj_o, dcv_o, dh_o, dx1_o, dx1b_o, dya_o, dyb_o, acc_o):
        i = pl.program_id(0)

        @pl.when(i == 0)
        def _():
            acc_o[...] = jnp.zeros_like(acc_o)

        dxn1 = _dot_nt(dgp_r[...], wpg_r[...])
        x1 = _f32(x1_r)
        r1 = _rstd(x1)
        xh1 = x1 * r1
        acc_o[0:1, 0:D] += jnp.sum(dxn1 * xh1, axis=0, keepdims=True)
        dx1 = dx2_r[...] + _norm_bwd(dxn1, xh1, r1, gple_r[...])
        dx1_o[...] = dx1
        dx1b = dx1.astype(MXU)
        dx1b_o[...] = dx1b

        dmg = _dot_nt(dx1b, wo_r[...]).astype(MXU)
        sga, dsga = _gate(ga_r[...])
        sgb, dsgb = _gate(gb_r[...])
        dya = dmg * sga
        dyb = dmg * sgb
        dya_o[...] = dya
        dyb_o[...] = dyb
        dproj_o[:, C_GA:C_GA + D] = dmg * ya_r[...] * dsga
        dproj_o[:, C_GB:C_GB + D] = dmg * yb_r[...] * dsgb

        db_pre = _dot_nt(dyb, wb_r[...]).astype(MXU)
        hn = _head_norm(_f32(h_r), gh_r[...])
        hbn = jnp.concatenate([xh * g for xh, _, g in hn], axis=1).astype(MXU)
        so, dso = _gate(o_r[...])
        zb = zb_r[...]
        szb, dszb = _gate(zb)
        sb = zb * szb
        t1 = db_pre * hbn
        dproj_o[:, C_O:C_O + VD] = t1 * (sb * dso)
        dproj_o[:, C_ZB:C_ZB + VD] = t1 * (so * (szb + zb * dszb))
        dhbn = (db_pre * (so * sb)).astype(F32)
        for j, (xh, rj, g) in enumerate(hn):
            dj = dhbn[:, j * DV:(j + 1) * DV]
            acc_o[1:2, j * DV:(j + 1) * DV] += jnp.sum(dj * xh, axis=0, keepdims=True)
            dh_o[:, j * DV:(j + 1) * DV] = _norm_bwd(dj, xh, rj, g).astype(MXU)

        da_pre = _dot_nt(dya, wa_r[...]).astype(MXU)
        u, u1, u2 = _conv_inputs(i, tm, xa_r, ca_r, xah_r, cah_r)
        cwv = cw_r[...]
        cv = cwv[0:1, :] * u2 + cwv[1:2, :] * u1 + cwv[2:3, :] * u + cb_r[...]
        za = za_r[...]
        sza, dsza = _gate(za)
        sa = za * sza
        ba = ba_r[...]
        t2 = da_pre * cv.astype(MXU)
        dproj_o[:, C_BA:C_BA + D] = t2 * sa
        dproj_o[:, C_ZA:C_ZA + D] = t2 * (ba * (sza + za * dsza))
        dcv_b = da_pre * (ba * sa)
        dcv_o[...] = dcv_b
        dcv = dcv_b.astype(F32)
        acc_o[2:3, 0:D] += jnp.sum(dcv, axis=0, keepdims=True)
        acc_o[3:4, 0:D] += jnp.sum(dcv * u2, axis=0, keepdims=True)
        acc_o[4:5, 0:D] += jnp.sum(dcv * u1, axis=0, keepdims=True)
        acc_o[5:6, 0:D] += jnp.sum(dcv * u, axis=0, keepdims=True)

    row = lambda w, dt: (pl.BlockSpec((tm, w), lambda i: (i, 0)), jax.ShapeDtypeStruct((n, w), dt))
    outs = [(pl.BlockSpec((tm, TAIL_W), lambda i: (i, 0)), jax.ShapeDtypeStruct((n, NMAIN), MXU)),
            row(D, MXU), row(VD, MXU), row(D, F32), row(D, MXU), row(D, MXU), row(D, MXU),
            (_const((8, VD)), jax.ShapeDtypeStruct((8, VD), F32))]
    rowin = lambda w: pl.BlockSpec((tm, w), lambda i: (i, 0))
    return pl.pallas_call(
        body, name="tail_bwd", grid=(n // tm,),
        in_specs=[_proj_spec(tm, C_BA, 1024), _proj_spec(tm, C_ZA, 1024),
                  _proj_spec(tm, C_O, 2048), _proj_spec(tm, C_ZB, 2048),
                  _proj_spec(tm, C_GA, 1024), _proj_spec(tm, C_GB, 1024),
                  _proj_spec(tm, C_XA, 1024), _proj_spec(tm, C_CA, 1024),
                  _halo_prev(tm, C_XA), _halo_prev(tm, C_CA),
                  rowin(VD), rowin(D), rowin(D), rowin(D), rowin(D), rowin(D),
                  _const((8, D)), _const((1, D)), _const((1, VD)), _const((1, D)),
                  _const((D, D)), _const((D, D)), _const((VD, D)), _const((D, D))],
        out_specs=[s for s, _ in outs],
        out_shape=[s for _, s in outs],
        compiler_params=_cparams("arbitrary"),
    )(*([proj] * 10), h, dgp, dx2, x1, ya, yb, cw, cb, gh, gple, wpg, wo, wb, wa)


def _conv_bwd(proj, dcv, cw, dproj):
    n = dcv.shape[0]
    tm = min(512, n)
    nt = n // tm

    def body(xa_r, ca_r, dcv_r, nxt_r, cw_r, dproj_in, dxc_o):
        del dproj_in
        i = pl.program_id(0)
        dcv_v = _f32(dcv_r)
        nxt = jnp.where(i < nt - 1, _f32(nxt_r), 0.0)
        rid = lax.broadcasted_iota(jnp.int32, dcv_v.shape, 0)
        d1 = jnp.where(rid == tm - 1, nxt[0:1, :], pltpu.roll(dcv_v, tm - 1, 0))
        d2 = jnp.where(rid == tm - 2, nxt[0:1, :],
                       jnp.where(rid == tm - 1, nxt[1:2, :], pltpu.roll(dcv_v, tm - 2, 0)))
        cwv = cw_r[...]
        du = cwv[2:3, :] * dcv_v + cwv[1:2, :] * d1 + cwv[0:1, :] * d2
        dxc_o[:, 0:D] = (du * _f32(ca_r)).astype(MXU)
        dxc_o[:, D:2 * D] = (du * _f32(xa_r)).astype(MXU)

    return pl.pallas_call(
        body, name="conv_bwd", grid=(nt,),
        in_specs=[_proj_spec(tm, C_XA, 1024), _proj_spec(tm, C_CA, 1024),
                  pl.BlockSpec((tm, D), lambda i: (i, 0)),
                  pl.BlockSpec((8, D), lambda i: (jnp.minimum((i + 1) * (tm // 8), n // 8 - 1), 0)),
                  _const((8, D)), ANY],
        out_specs=pl.BlockSpec((tm, CONV_W), lambda i: (i, C_XA // CONV_W)),
        out_shape=jax.ShapeDtypeStruct(dproj.shape, dproj.dtype),
        input_output_aliases={5: 0},
        compiler_params=_cparams("arbitrary"),
    )(proj, proj, dcv, dcv, cw, dproj)


def _position():
    return lax.axis_index("x"), lax.axis_index("y"), lax.axis_index("c")


def _flip(v, bit):
    return 1 - v if bit else v


def _part_start(core, stride, tiled):
    return pl.multiple_of(core * stride, stride) if tiled else core * stride


def _scatter_copies(srcs, dsts, strides, send_sems, recv_sems, local_sems):
    x, y, c = _position()
    me = 4 * x + 2 * y + c
    na = len(srcs)
    copies = []
    for r in range(N_DEV):
        px, py, pc = _flip(x, (r >> 2) & 1), _flip(y, (r >> 1) & 1), _flip(c, r & 1)
        for a in range(na):
            rows = dsts[a].shape[1]
            src = srcs[a].at[2 * px + py, pl.ds(_part_start(pc, strides[a], len(dsts[a].shape) == 3), rows)]
            dst = dsts[a].at[me]
            if r == 0:
                copies.append(pltpu.make_async_copy(src, dst, local_sems.at[a]))
            else:
                k = (r - 1) * na + a
                copies.append(pltpu.make_async_remote_copy(
                    src_ref=src, dst_ref=dst, send_sem=send_sems.at[k], recv_sem=recv_sems.at[k],
                    device_id=(px, py, pc), device_id_type=MESH))
    return copies


def _input_grad(dproj, dgates, w_t, wg_t, x, dx1, g_mix, g_in, g_rest):
    n = x.shape[0]
    tm, tk = min(1024, n), 2048
    nk = NMAIN // tk
    nt = n // tm

    def body(dp_r, dg_r, w_r, wg_r, x_r, dx1_r, g_r, gin, grest,
             gx_o, acc_o, oin, orest, acc, send_sems, recv_sems, local_sems):
        i = pl.program_id(0)
        kk = pl.program_id(1)
        copies = _scatter_copies((gin, grest), (oin, orest), strides, send_sems, recv_sems, local_sems)

        @pl.when((i == 0) & (kk == 0))
        def _():
            acc_o[...] = jnp.zeros_like(acc_o)
            for cp in copies:
                cp.start()

        @pl.when(kk == 0)
        def _():
            acc[...] = _dot(dg_r[...].astype(MXU), wg_r[...])

        acc[...] += _dot(dp_r[...], w_r[...])

        @pl.when(kk == nk - 1)
        def _():
            dhn = acc[...]
            xv = x_r[...]
            r0 = _rstd(xv)
            xh = xv * r0
            acc_o[0:1, :] += jnp.sum(dhn * xh, axis=0, keepdims=True)
            gx_o[...] = dx1_r[...] + _norm_bwd(dhn, xh, r0, g_r[...])

        @pl.when((i == nt - 1) & (kk == nk - 1))
        def _():
            for cp in copies:
                cp.wait()

    nrem = 2 * (N_DEV - 1)
    r_in, r_rest = HALF_SLABS, g_rest.shape[1] // 2
    strides = (HALF_START, r_rest)
    return pl.pallas_call(
        body, name="input_grad", grid=(nt, nk),
        in_specs=[pl.BlockSpec((tm, tk), lambda i, kk: (i, kk)),
                  pl.BlockSpec((tm, GATE_W), lambda i, kk: (i, 0)),
                  pl.BlockSpec((tk, D), lambda i, kk: (kk, 0)),
                  pl.BlockSpec((GATE_W, D), lambda i, kk: (0, 0)),
                  pl.BlockSpec((tm, D), lambda i, kk: (i, 0)),
                  pl.BlockSpec((tm, D), lambda i, kk: (i, 0)),
                  pl.BlockSpec((1, D), lambda i, kk: (0, 0)),
                  ANY, ANY],
        out_specs=[pl.BlockSpec((tm, D), lambda i, kk: (i, 0)),
                   pl.BlockSpec((8, D), lambda i, kk: (0, 0)),
                   ANY, ANY],
        out_shape=[jax.ShapeDtypeStruct((n, D), F32), jax.ShapeDtypeStruct((8, D), F32),
                   jax.ShapeDtypeStruct((N_DEV, r_in) + g_in.shape[2:], g_in.dtype),
                   jax.ShapeDtypeStruct((N_DEV, r_rest) + g_rest.shape[2:], g_rest.dtype)],
        scratch_shapes=[pltpu.VMEM((tm, D), F32),
                        pltpu.SemaphoreType.DMA((nrem,)), pltpu.SemaphoreType.DMA((nrem,)),
                        pltpu.SemaphoreType.DMA((2,))],
        compiler_params=_cparams("arbitrary", "arbitrary"),
    )(dproj, dgates, w_t, wg_t, x, dx1, g_mix, g_in, g_rest)


def _pack_small(acc_f, acc_b, acc_x, gsum):
    def body(f_r, b_r, x_r, s_r, o_r):
        o_r[...] = jnp.zeros_like(o_r)
        o_r[0:1, :] = x_r[0:1, :]
        o_r[1:2, :] = b_r[2:3, 0:D]
        o_r[2:3, :] = b_r[1:2, 0:D]
        o_r[3:4, :] = b_r[1:2, D:2 * D]
        o_r[4:5, :] = b_r[0:1, 0:D]
        o_r[5:6, :] = f_r[0:1, :]
        o_r[6:9, :] = b_r[3:6, 0:D]
        lane = lax.broadcasted_iota(jnp.int32, (1, D), 1)
        o_r[9:10, :] = jnp.where(lane < 2 * NH, s_r[0:1, :], jnp.where(lane == 2 * NH, f_r[1:2, :], 0.0))

    return pl.pallas_call(
        body, name="pack_small",
        out_shape=jax.ShapeDtypeStruct((SMALL_ROWS, D), F32),
    )(acc_f, acc_b, acc_x, gsum)


def _sum_slots(r, name, tr=64, half=None):
    s, rows, w = r.shape
    tr = min(tr, rows)
    assert rows % tr == 0
    nt = rows // tr

    def body(*refs):
        r_ref, o_ref = refs[-2:]
        tot = r_ref[0].astype(F32)
        for k in range(1, s):
            tot = tot + r_ref[k].astype(F32)
        o_ref[...] = tot

    if half is None:
        return pl.pallas_call(
            body, name=name, grid=(nt,),
            in_specs=[pl.BlockSpec((s, tr, w), lambda i: (0, i, 0))],
            out_specs=pl.BlockSpec((tr, w), lambda i: (i, 0)),
            out_shape=jax.ShapeDtypeStruct((rows, w), F32),
            compiler_params=_cparams("arbitrary"),
        )(r)
    return pl.pallas_call(
        body, name=name,
        grid_spec=pltpu.PrefetchScalarGridSpec(
            num_scalar_prefetch=1, grid=(nt,),
            in_specs=[pl.BlockSpec((s, tr, w), lambda i, hf: (0, i, 0))],
            out_specs=pl.BlockSpec((tr, w), lambda i, hf: (hf[0] * nt + i, 0))),
        out_shape=jax.ShapeDtypeStruct((2 * rows, w), F32),
        compiler_params=_cparams("arbitrary"),
    )(half, r)


def _cast_slabs(a, tr=163):
    rows = a.shape[0]
    assert rows % tr == 0

    def body(a_ref, o_ref):
        o_ref[...] = a_ref[...].astype(MXU)

    spec = pl.BlockSpec((tr,) + a.shape[1:], lambda i: (i, 0, 0))
    return pl.pallas_call(
        body, name="cast_w_in", grid=(rows // tr,), in_specs=[spec], out_specs=spec,
        out_shape=jax.ShapeDtypeStruct(a.shape, MXU), compiler_params=_cparams("arbitrary"),
    )(a)


def _sum_slabs(r, name, tr=69):
    s, rows = r.shape[:2]
    assert rows % tr == 0

    def body(r_ref, o_ref):
        tot = r_ref[0].astype(F32)
        for k in range(1, s):
            tot = tot + r_ref[k].astype(F32)
        o_ref[...] = tot

    return pl.pallas_call(
        body, name=name, grid=(rows // tr,),
        in_specs=[pl.BlockSpec((s, tr) + r.shape[2:], lambda i: (0, i, 0, 0))],
        out_specs=pl.BlockSpec((tr,) + r.shape[2:], lambda i: (i, 0, 0)),
        out_shape=jax.ShapeDtypeStruct(r.shape[1:], F32),
        compiler_params=_cparams("arbitrary"),
    )(r)


def _adamw_body(w_r, g_r, m_r, v_r, d_o, m_o, v_o):
    c1 = 1.0 - ADAM_B1 ** ADAM_STEP
    c2 = 1.0 - ADAM_B2 ** ADAM_STEP
    gv = g_r[...]
    mn = ADAM_B1 * m_r[...] + (1.0 - ADAM_B1) * gv
    vn = ADAM_B2 * v_r[...] + (1.0 - ADAM_B2) * (gv * gv)
    m_o[...] = mn
    v_o[...] = vn
    d_o[...] = -ADAM_LR * ((mn / c1) / (jnp.sqrt(vn / c2) + ADAM_EPS) + ADAM_WD * w_r[...])


def _adamw_slabs(w, g, m, v, name, tr):
    rows = w.shape[0]
    assert rows % tr == 0 and w.shape == g.shape

    def body(*refs):
        _adamw_body(*refs)

    spec = pl.BlockSpec((tr,) + w.shape[1:], lambda i: (i, 0, 0))
    shp = jax.ShapeDtypeStruct(w.shape, F32)
    return pl.pallas_call(
        body, name=name, grid=(rows // tr,),
        in_specs=[spec] * 4, out_specs=[spec] * 3, out_shape=[shp] * 3,
        compiler_params=_cparams("arbitrary"),
    )(w, g, m, v)


def _adamw(w, g, m, v, name, g_row0=0, tr=64):
    lead = w.ndim == 3
    rows, cols = w.shape[-2:]
    tr = min(tr, rows)
    assert rows % tr == 0 and g_row0 % tr == 0 and g.shape[1] == cols

    def body(*refs):
        _adamw_body(*refs)

    if lead:
        spec = pl.BlockSpec((None, tr, cols), lambda i: (0, i, 0))
    else:
        spec = pl.BlockSpec((tr, cols), lambda i: (i, 0))
    g_spec = pl.BlockSpec((tr, cols), lambda i: (g_row0 // tr + i, 0))
    shp = jax.ShapeDtypeStruct(w.shape, F32)
    return pl.pallas_call(
        body, name=name, grid=(rows // tr,),
        in_specs=[spec, g_spec, spec, spec], out_specs=[spec] * 3, out_shape=[shp] * 3,
        compiler_params=_cparams("arbitrary"),
    )(w, g, m, v)


def _allgather_chips(halved, whole):
    arrs = [a for a, _ in halved] + list(whole)
    parts = [p for _, p in halved]
    nh, na = len(halved), len(arrs)
    relations = ((1, 0), (0, 1), (1, 1))
    n_ici = 3 * na

    def body(*refs):
        ins, outs = refs[:na], refs[na:2 * na]
        send_sems, recv_sems = refs[2 * na:]
        x, y, c = _position()
        mine = 2 * x + y

        def rows(a, core, stride, size):
            return pl.ds(_part_start(core, stride, len(ins[a].shape) == 2), size)

        def remote(src, dst, k, to):
            return pltpu.make_async_remote_copy(src_ref=src, dst_ref=dst, send_sem=send_sems.at[k],
                                                recv_sem=recv_sems.at[k], device_id=to, device_id_type=MESH)

        first, passed = [], []
        for j, (rx, ry) in enumerate(relations):
            px, py = _flip(x, rx), _flip(y, ry)
            for a in range(na):
                if a < nh:
                    s1, n1, s2, n2 = parts[a]
                    first.append(remote(ins[a].at[rows(a, c, s1, n1)], outs[a].at[mine, rows(a, c, s1, n1)],
                                        j * na + a, (px, py, c)))
                    landed = outs[a].at[2 * px + py, rows(a, c, s2, n2)]
                    passed.append(remote(landed, landed, n_ici + j * nh + a, (x, y, 1 - c)))
                else:
                    first.append(remote(ins[a], outs[a].at[mine], j * na + a, (px, py, c)))
        for cp in first:
            cp.start()
        k = 0
        for j in range(3):
            for a in range(na):
                first[j * na + a].wait_recv()
                if a < nh:
                    passed[k].start()
                    k += 1
        for cp in passed:
            cp.wait_recv()
        for cp in first + passed:
            cp.wait_send()

    nsem = n_ici + 3 * nh
    outs = pl.pallas_call(
        body, name="allgather_weights",
        in_specs=[ANY] * na, out_specs=[ANY] * na,
        out_shape=[jax.ShapeDtypeStruct((N_CHIPS,) + a.shape, a.dtype) for a in arrs],
        scratch_shapes=[pltpu.SemaphoreType.DMA((nsem,)), pltpu.SemaphoreType.DMA((nsem,))],
    )(*arrs)
    chip = 2 * lax.axis_index("x") + lax.axis_index("y")
    return [lax.dynamic_update_slice(o, a[None], (chip,) + (0,) * a.ndim) for o, a in zip(outs, arrs)]


def _join_halves(piece, full_rest, small):
    def body(pc_ref, rest_in, sm, oin, orest, osm, send_sems, recv_sems, local_sem):
        del rest_in
        x, y, c = _position()
        me = 4 * x + 2 * y + c
        sib = (x, y, 1 - c)
        rows = orest.shape[0] // 2
        mine = orest.at[pl.ds(pl.multiple_of(c * rows, rows), rows)]
        copies = [
            pltpu.make_async_remote_copy(
                src_ref=pc_ref.at[pl.ds(c, HALF_START)], dst_ref=oin.at[pl.ds(c * HALF_SLABS, HALF_START)],
                send_sem=send_sems.at[0], recv_sem=recv_sems.at[0], device_id=sib, device_id_type=MESH),
            pltpu.make_async_remote_copy(
                src_ref=mine, dst_ref=mine, send_sem=send_sems.at[1], recv_sem=recv_sems.at[1],
                device_id=sib, device_id_type=MESH)]
        na = 2
        copies.append(pltpu.make_async_copy(sm, osm.at[me], local_sem))
        for r in range(1, N_DEV):
            to = (_flip(x, (r >> 2) & 1), _flip(y, (r >> 1) & 1), _flip(c, r & 1))
            copies.append(pltpu.make_async_remote_copy(
                src_ref=sm, dst_ref=osm.at[me], send_sem=send_sems.at[na + r - 1],
                recv_sem=recv_sems.at[na + r - 1], device_id=to, device_id_type=MESH))
        for cp in copies:
            cp.start()
        for cp in copies:
            cp.wait()

    nsem = 2 + N_DEV - 1
    return pl.pallas_call(
        body, name="join_halves",
        in_specs=[ANY] * 3, out_specs=[ANY] * 3,
        out_shape=[jax.ShapeDtypeStruct((SLABS_CHIP,) + piece.shape[1:], piece.dtype),
                   jax.ShapeDtypeStruct(full_rest.shape, full_rest.dtype),
                   jax.ShapeDtypeStruct((N_DEV,) + small.shape, small.dtype)],
        scratch_shapes=[pltpu.SemaphoreType.DMA((nsem,)), pltpu.SemaphoreType.DMA((nsem,)),
                        pltpu.SemaphoreType.DMA],
        input_output_aliases={1: 1},
    )(piece, full_rest, small)


def _to_internal(w_slabs):
    order = sorted(SEGMENTS, key=lambda s: s[2])
    main = jnp.concatenate([w_slabs[g0 // 2:(g0 + w) // 2] for g0, w, _ in order], axis=0)
    gate = w_slabs[GATE_COL // 2:GATE_COL // 2 + NH].reshape(2 * NH, D)
    return main.reshape(NMAIN, D), jnp.pad(gate, ((0, GATE_W - 2 * NH), (0, 0)))


def _to_global(main_t, gate_t):
    main = main_t.reshape(NMAIN // 2, 16, D // 8)
    parts = sorted([(g0, main[i0 // 2:(i0 + w) // 2]) for g0, w, i0 in SEGMENTS]
                   + [(GATE_COL, gate_t[0:2 * NH].reshape(NH, 16, D // 8))], key=lambda s: s[0])
    return jnp.concatenate([p for _, p in parts], axis=0)


def _pack_rows(wa, wb, wo, wpg, wp):
    return jnp.concatenate([wa, wb, wo, wpg, wp.reshape(PACK_ROWS[4], D)], axis=0)


def _pad_rows(a, rows=8):
    return jnp.pad(a, ((0, rows - a.shape[0]), (0, 0)))


def kernel(x, p, g_mix, w_in, conv_w, conv_b, w_a_out, b_gates, g_head, w_b_out, w_o, g_ple, w_ple_gate, w_ple, g_final, loss_target, m_g_mix, m_w_in, m_conv_w, m_conv_b, m_w_a_out, m_b_gates, m_g_head, m_w_b_out, m_w_o, m_g_ple, m_w_ple_gate, m_w_ple, m_g_final, v_g_mix, v_w_in, v_conv_w, v_conv_b, v_w_a_out, v_b_gates, v_g_head, v_w_b_out, v_w_o, v_g_ple, v_w_ple_gate, v_w_ple, v_g_final):
    chip = 2 * lax.axis_index("x") + lax.axis_index("y")
    xs, ps, ts = x[0], p[0, 0], loss_target[0]
    g_fin = g_final.reshape(1, D)

    pack_w = _pack_rows(w_a_out[0], w_b_out[0], w_o[0], w_ple_gate[0], w_ple[0])
    w_slabs = _cast_slabs(jnp.transpose(w_in, (2, 0, 1)).reshape(SLABS_CHIP, 16, D // 8))
    half_rows = PACK_TOTAL // 2
    g_win, g_pack, g_cw = _allgather_chips(
        [(w_slabs, (HALF_START, HALF_SLABS, HALF_SLABS, HALF_START)),
         (pack_w.astype(MXU), (half_rows, half_rows, half_rows, half_rows))], [_pad_rows(conv_w[0])])
    w_t, wg_t = _to_internal(g_win.reshape(N_IN // 2, 16, D // 8))
    offs = [0, 256, 768, 1024, 1280, 1344]
    wa, wb, wo, wpg = [g_pack[:, offs[k]:offs[k + 1]].reshape(-1, D) for k in range(4)]
    wp = jnp.transpose(g_pack[:, offs[4]:offs[5]].reshape(N_CHIPS, PLE, PLE), (1, 0, 2)).reshape(PLE, D)
    cw = jnp.transpose(g_cw, (1, 0, 2)).reshape(8, D)

    bias = jnp.pad(b_gates, ((0, 0), (0, GATE_W - 2 * NH)))
    hn, gates = _prenorm(xs, g_mix, wg_t)
    proj = _proj(hn, w_t)
    h, stats, cs, ns, ms = _mlstm_fwd(proj, gates, bias)
    (a_pre, b_pre, mg, xn1, de, dgp, ya, yb, x1, dx2, acc_f) = _tail_fwd(
        proj, h, xs, ps, ts, cw, conv_b, g_head, g_ple, g_fin, wa, wb, wo, wpg, wp)
    dproj, dcv, dh, dx1, dx1b, dya, dyb, acc_b = _tail_bwd(
        proj, h, dgp, dx2, x1, ya, yb, cw, conv_b, g_head, g_ple, wpg, wo, wb, wa)
    dproj = _conv_bwd(proj, dcv, cw, dproj)
    dproj, dgates, gsum = _mlstm_bwd(proj, gates, bias, h, dh, stats, cs, ns, ms, dproj)
    d_main = _matmul_tn(dproj, hn, "dw_in", out_dtype=WIRE, tk=2048)
    d_gate = _matmul_tn(dgates, hn, "dw_gate", out_dtype=WIRE)
    d_wa = _matmul_tn(a_pre, dya, "dw_a_out", out_dtype=WIRE)
    d_wb = _matmul_tn(b_pre, dyb, "dw_b_out", out_dtype=WIRE)
    d_wo = _matmul_tn(mg, dx1b, "dw_o", out_dtype=WIRE)
    d_wpg = _matmul_tn(xn1, dgp, "dw_ple_gate", out_dtype=WIRE)
    d_wp = _matmul_tn(ps, de, "dw_ple", out_dtype=WIRE)

    g_in = _to_global(d_main, d_gate).reshape(N_CHIPS, SLABS_CHIP, 16, D // 8)
    d_wp_c = jnp.transpose(d_wp.reshape(PLE, N_CHIPS, PLE), (1, 0, 2)).reshape(N_CHIPS, PACK_ROWS[4], D)
    g_rest = jnp.concatenate(
        [d_wa.reshape(N_CHIPS, -1, D), d_wb.reshape(N_CHIPS, -1, D), d_wo.reshape(N_CHIPS, -1, D),
         d_wpg.reshape(N_CHIPS, -1, D), d_wp_c], axis=1)
    grad_x, acc_x, r_in, r_rest = _input_grad(dproj, dgates, w_t, wg_t, xs, dx1, g_mix, g_in, g_rest)
    small = _pack_small(acc_f, acc_b, acc_x, gsum)
    core = lax.axis_index("c").astype(jnp.int32)
    piece = _sum_slabs(r_in, "sum_w_in")
    gw_in, gw_rest, r_small = _join_halves(
        piece, _sum_slots(r_rest, "sum_rest", tr=96, half=core.reshape(1)), small)
    gw_in = lax.dynamic_update_slice(gw_in, piece, (core * HALF_START, 0, 0)).reshape(COLS_CHIP, 8, D // 8)
    gs = _sum_slots(r_small, "sum_small", tr=SMALL_ROWS)

    big = []
    row0 = 0
    for name, w, m, v in (("w_a_out", w_a_out, m_w_a_out, v_w_a_out), ("w_b_out", w_b_out, m_w_b_out, v_w_b_out),
                          ("w_o", w_o, m_w_o, v_w_o), ("w_ple_gate", w_ple_gate, m_w_ple_gate, v_w_ple_gate)):
        big.append((name, w, m, v, gw_rest, row0))
        row0 += w.shape[1]
    g_wp = gw_rest[row0:row0 + PACK_ROWS[4]].reshape(PLE, PLE)
    big.append(("w_ple", w_ple, m_w_ple, v_w_ple, g_wp, 0))
    upd = {name: _adamw(w, g, m, v, "adamw_" + name, g_row0=r0) for name, w, m, v, g, r0 in big}
    g_big = {name: (g if name == "w_ple" else g[r0:r0 + w.shape[1]])[None] for name, w, m, v, g, r0 in big}
    slabs = lambda a: jnp.transpose(a, (2, 0, 1)).reshape(COLS_CHIP, 8, D // 8)
    unslab = lambda a: jnp.transpose(a, (1, 2, 0)).reshape(1, D, COLS_CHIP)
    upd["w_in"] = [unslab(u) for u in _adamw_slabs(
        slabs(w_in), gw_in, slabs(m_w_in), slabs(v_w_in), "adamw_w_in", tr=326)]
    g_big["w_in"] = unslab(gw_in)

    lane = lax.broadcasted_iota(jnp.int32, (1, D), 1)
    g_small = jnp.concatenate([gs[0:6], jnp.where(lane < 2 * NH, gs[9:10], 0.0), jnp.zeros((1, D), F32)], axis=0)

    def small_pack(gm, cb_, bg, gh, gp, gf):
        return jnp.concatenate([gm, cb_, gh.reshape(2, D), gp, gf.reshape(1, D),
                                jnp.pad(bg, ((0, 0), (0, D - 2 * NH))), jnp.zeros((1, D), F32)], axis=0)

    ws = small_pack(g_mix, conv_b, b_gates, g_head, g_ple, g_final)
    ms_ = small_pack(m_g_mix, m_conv_b, m_b_gates, m_g_head, m_g_ple, m_g_final)
    vs = small_pack(v_g_mix, v_conv_b, v_b_gates, v_g_head, v_g_ple, v_g_final)
    upd_s = _adamw(ws, g_small, ms_, vs, "adamw_small")
    g_cw_mine = _pad_rows(lax.dynamic_slice(gs[6:9], (0, chip * PLE), (3, PLE)))
    upd_c = _adamw(_pad_rows(conv_w[0]), g_cw_mine, _pad_rows(m_conv_w[0]), _pad_rows(v_conv_w[0]),
                   "adamw_conv_w")

    def leaves(bigs, sm, cwv):
        return [sm[0:1], bigs["w_in"], cwv[0:3][None], sm[1:2], bigs["w_a_out"], sm[6:7, 0:2 * NH],
                sm[2:4].reshape(1, VD), bigs["w_b_out"], bigs["w_o"], sm[4:5], bigs["w_ple_gate"],
                bigs["w_ple"], sm[5]]

    loss = gs[9, 2 * NH]
    outs = [loss, grad_x[None]] + leaves(g_big, g_small, g_cw_mine)
    for k in range(3):
        outs += leaves({name: u[k] for name, u in upd.items()}, upd_s[k], upd_c[k])
    return tuple(outs)
```

```python
import jax
import jax.numpy as jnp
from jax import lax
from jax.experimental import pallas as pl
from jax.experimental.pallas import tpu as pltpu

F32 = jnp.float32
MXU = jnp.bfloat16
WIRE = jnp.bfloat16

D = 1024
NH, DK, DV = 4, 256, 512
VD = NH * DV
PLE = 256
LCH = 256
EPS = 1e-6
N_IN = 14344
NMAIN = 14336
GATE_W = 128
N_CHIPS, N_DEV = 4, 8

C_BA, C_ZA, C_O, C_ZB, C_GA, C_GB = 0, 1024, 2048, 4096, 6144, 7168
QK = NH * DK
C_Q, C_K, C_V, C_XA, C_CA = 8192, 9216, 10240, 12288, 13312
QKV_W = 2 * QK + VD
TAIL_W = 8192
CONV_W = 2048
SEGMENTS = (
    (0, 1024, C_XA), (1024, 1024, C_BA), (2048, 1024, C_CA), (3072, 1024, C_ZA),
    (4096, QKV_W, C_Q), (8192, 4096, C_O), (12296, 2048, C_GA),
)
GATE_COL = 12288

COLS_CHIP = N_IN // N_CHIPS
SLABS_CHIP = COLS_CHIP // 2
HALF_START = SLABS_CHIP // 2
HALF_SLABS = SLABS_CHIP - HALF_START

PACK_ROWS = (256, 512, 256, 256, 64)
PACK_TOTAL = sum(PACK_ROWS)
SMALL_ROWS = 16

ADAM_LR, ADAM_B1, ADAM_B2, ADAM_EPS, ADAM_WD, ADAM_STEP = 0.001, 0.9, 0.999, 1e-08, 0.01, 10

VMEM_LIMIT = 56 * 1024 * 1024
MESH = pl.DeviceIdType.MESH
ANY = pl.BlockSpec(memory_space=pl.ANY)


def _cparams(*sem):
    return pltpu.CompilerParams(dimension_semantics=sem, vmem_limit_bytes=VMEM_LIMIT)


def _dot(a, b):
    return jnp.dot(a, b, preferred_element_type=F32)


def _dot_nt(a, b):
    return lax.dot_general(a, b, (((1,), (1,)), ((), ())), preferred_element_type=F32)


def _dot_tn(a, b):
    return lax.dot_general(a, b, (((0,), (0,)), ((), ())), preferred_element_type=F32)


def _sigmoid(x):
    return 1.0 / (1.0 + jnp.exp(-x))


def _logsig(x):
    return jnp.minimum(x, 0.0) - jnp.log(1.0 + jnp.exp(-jnp.abs(x)))


GATE_FLOOR = -80.0


def _gate(v):
    e = jnp.exp(-jnp.maximum(v, GATE_FLOOR))
    s = 1.0 / (1.0 + e)
    return s, e * s * s


def _rstd(x):
    return lax.rsqrt(jnp.mean(x * x, axis=-1, keepdims=True) + EPS)


def _norm_bwd(dy, xhat, r, g):
    dxh = dy * g
    return r * (dxh - xhat * jnp.mean(dxh * xhat, axis=-1, keepdims=True))


def _f32(ref):
    return ref[...].astype(F32)


def _prenorm(x, g_mix, wg_t):
    n = x.shape[0]
    tm = min(512, n)

    def body(x_ref, g_ref, wg_ref, hn_ref, gate_ref):
        xv = x_ref[...]
        hn = (xv * _rstd(xv) * g_ref[...]).astype(MXU)
        hn_ref[...] = hn
        gate_ref[...] = _dot_nt(hn, wg_ref[...])

    return pl.pallas_call(
        body, name="prenorm", grid=(n // tm,),
        in_specs=[pl.BlockSpec((tm, D), lambda i: (i, 0)),
                  pl.BlockSpec((1, D), lambda i: (0, 0)),
                  pl.BlockSpec((GATE_W, D), lambda i: (0, 0))],
        out_specs=[pl.BlockSpec((tm, D), lambda i: (i, 0)),
                   pl.BlockSpec((tm, GATE_W), lambda i: (i, 0))],
        out_shape=[jax.ShapeDtypeStruct((n, D), MXU), jax.ShapeDtypeStruct((n, GATE_W), F32)],
        compiler_params=_cparams("arbitrary"),
    )(x, g_mix, wg_t)


def _proj(hn, w_t, tm=2048, tn=2048):
    n, k = hn.shape
    m = w_t.shape[0]
    tm = min(tm, n)

    def body(a_ref, b_ref, o_ref):
        o_ref[...] = _dot_nt(a_ref[...], b_ref[...]).astype(MXU)

    return pl.pallas_call(
        body, name="proj", grid=(m // tn, n // tm),
        in_specs=[pl.BlockSpec((tm, k), lambda j, i: (i, 0)),
                  pl.BlockSpec((tn, k), lambda j, i: (j, 0))],
        out_specs=pl.BlockSpec((tm, tn), lambda j, i: (i, j)),
        out_shape=jax.ShapeDtypeStruct((n, m), MXU),
        compiler_params=_cparams("arbitrary", "arbitrary"),
    )(hn, w_t)


def _matmul_tn(a, b, name, out_dtype=F32, ta=1024, tb=1024, tk=1024):
    n, ka = a.shape
    kb = b.shape[1]
    ta, tb, tk = min(ta, ka), min(tb, kb), min(tk, n)
    nk = n // tk

    def body(a_ref, b_ref, o_ref, acc):
        kk = pl.program_id(2)

        @pl.when(kk == 0)
        def _():
            acc[...] = jnp.zeros_like(acc)

        acc[...] += _dot_tn(a_ref[...].astype(MXU), b_ref[...].astype(MXU))

        @pl.when(kk == nk - 1)
        def _():
            o_ref[...] = acc[...].astype(out_dtype)

    return pl.pallas_call(
        body, name=name, grid=(ka // ta, kb // tb, nk),
        in_specs=[pl.BlockSpec((tk, ta), lambda i, j, kk: (kk, i)),
                  pl.BlockSpec((tk, tb), lambda i, j, kk: (kk, j))],
        out_specs=pl.BlockSpec((ta, tb), lambda i, j, kk: (i, j)),
        out_shape=jax.ShapeDtypeStruct((ka, kb), out_dtype),
        scratch_shapes=[pltpu.VMEM((ta, tb), F32)],
        compiler_params=_cparams("arbitrary", "arbitrary", "arbitrary"),
    )(a, b)


def _gate_vectors(g, gt, hd):
    lane = lax.broadcasted_iota(jnp.int32, g.shape, 1)
    sub = lax.broadcasted_iota(jnp.int32, gt.shape, 0)
    col = lambda j: jnp.sum(jnp.where(lane == j, g, 0.0), axis=1, keepdims=True)
    row = lambda j: jnp.sum(jnp.where(sub == j, gt, 0.0), axis=0, keepdims=True)
    return col(hd), col(hd + NH), row(hd), row(hd + NH)


def _chunk_decay(li_col, li_row, lf_col, lf_row, m_prev):
    n = li_col.shape[0]
    r = lax.broadcasted_iota(jnp.int32, (n, n), 0)
    c = lax.broadcasted_iota(jnp.int32, (n, n), 1)
    tri = r >= c
    b_col = jnp.sum(jnp.where(tri, lf_row, 0.0), axis=1, keepdims=True)
    b_row = jnp.sum(jnp.where(r <= c, lf_col, 0.0), axis=0, keepdims=True)
    b_last = jnp.sum(lf_row, axis=1, keepdims=True)
    dmat = jnp.where(tri, b_col - b_row + li_row, -jnp.inf)
    a_col = b_col + m_prev
    g_col = b_last - b_col + li_col
    m_new = jnp.maximum(b_last + m_prev, jnp.max(g_col, axis=0, keepdims=True))
    w_col = jnp.exp(g_col - m_new)
    decay = jnp.exp(b_last + m_prev - m_new)
    return tri, dmat, a_col, m_new, w_col, decay


def _qkv_specs(row_of):
    q_spec = pl.BlockSpec((LCH, QK), lambda c: (row_of(c), C_Q // QK))
    k_spec = pl.BlockSpec((LCH, QK), lambda c: (row_of(c), C_K // QK))
    v_spec = pl.BlockSpec((LCH, VD), lambda c: (row_of(c), C_V // VD))
    return q_spec, k_spec, v_spec


def _state_specs(row_of):
    return [pl.BlockSpec((NH, None, DK, DV), lambda c: (0, row_of(c), 0, 0)),
            pl.BlockSpec((NH, None, 1, DK), lambda c: (0, row_of(c), 0, 0)),
            pl.BlockSpec((NH, None, 1, GATE_W), lambda c: (0, row_of(c), 0, 0))]


def _lane_put(col, lane_id, width=GATE_W):
    lane = lax.broadcasted_iota(jnp.int32, (col.shape[0], width), 1)
    return jnp.where(lane == lane_id, col, 0.0)


def _lane_get(block, lane_id):
    lane = lax.broadcasted_iota(jnp.int32, block.shape, 1)
    return jnp.sum(jnp.where(lane == lane_id, block, 0.0), axis=1, keepdims=True)


def _mlstm_fwd(proj, gates, bias):
    n = proj.shape[0]
    nc = n // LCH

    def body(q_ref, k_ref, v_ref, g_ref, bias_ref,
             h_ref, st_ref, cs_ref, ns_ref, ms_ref, c_scr, n_scr, m_scr):
        @pl.when(pl.program_id(0) == 0)
        def _():
            c_scr[...] = jnp.zeros_like(c_scr)
            n_scr[...] = jnp.zeros_like(n_scr)
            m_scr[...] = jnp.full_like(m_scr, -jnp.inf)

        g = g_ref[...] + bias_ref[...]
        gt = g.T[0:8, :]
        stats = jnp.zeros((LCH, GATE_W), F32)
        for hd in range(NH):
            li_col, fr_col, li_row, fr_row = _gate_vectors(g, gt, hd)
            m_all = m_scr[hd]
            m_prev = m_all[0:1, 0:1]
            tri, dmat, a_col, m_new, w_col, decay = _chunk_decay(
                li_col, li_row, _logsig(fr_col), _logsig(fr_row), m_prev)
            m_col = jnp.maximum(a_col, jnp.max(dmat, axis=1, keepdims=True))
            dl = jnp.exp(dmat - m_col)
            inter = jnp.exp(a_col - m_col)

            qs = q_ref[:, hd * DK:(hd + 1) * DK] * (DK ** -0.5)
            kk = k_ref[:, hd * DK:(hd + 1) * DK]
            vv = v_ref[:, hd * DV:(hd + 1) * DV]
            cst = c_scr[hd]
            nst = n_scr[hd]
            cs_ref[hd] = cst
            ns_ref[hd] = nst
            ms_ref[hd] = m_all

            sc = _dot_nt(qs, kk) * dl
            num = _dot(sc.astype(MXU), vv) + inter * _dot(qs, cst.astype(MXU))
            den = (jnp.sum(sc, axis=1, keepdims=True)
                   + inter * jnp.sum(qs.astype(F32) * nst, axis=1, keepdims=True))
            nrm = jnp.maximum(jnp.abs(den), jnp.exp(-m_col))
            h_ref[:, hd * DV:(hd + 1) * DV] = (num / nrm).astype(h_ref.dtype)
            stats = stats + _lane_put(den, hd) + _lane_put(m_col, NH + hd)

            kw = kk.astype(F32) * w_col
            c_scr[hd] = decay * cst + _dot_tn(kw.astype(MXU), vv)
            n_scr[hd] = decay * nst + jnp.sum(kw, axis=0, keepdims=True)
            m_scr[hd] = jnp.broadcast_to(m_new, (1, GATE_W))
        st_ref[...] = stats

    q_spec, k_spec, v_spec = _qkv_specs(lambda c: c)
    return pl.pallas_call(
        body, name="mlstm_fwd", grid=(nc,),
        in_specs=[q_spec, k_spec, v_spec,
                  pl.BlockSpec((LCH, GATE_W), lambda c: (c, 0)),
                  pl.BlockSpec((1, GATE_W), lambda c: (0, 0))],
        out_specs=[pl.BlockSpec((LCH, VD), lambda c: (c, 0)),
                   pl.BlockSpec((LCH, GATE_W), lambda c: (c, 0))] + _state_specs(lambda c: c),
        out_shape=[jax.ShapeDtypeStruct((n, VD), MXU),
                   jax.ShapeDtypeStruct((n, GATE_W), F32),
                   jax.ShapeDtypeStruct((NH, nc, DK, DV), F32),
                   jax.ShapeDtypeStruct((NH, nc, 1, DK), F32),
                   jax.ShapeDtypeStruct((NH, nc, 1, GATE_W), F32)],
        scratch_shapes=[pltpu.VMEM((NH, DK, DV), F32), pltpu.VMEM((NH, 1, DK), F32),
                        pltpu.VMEM((NH, 1, GATE_W), F32)],
        compiler_params=_cparams("arbitrary"),
    )(proj, proj, proj, gates, bias)


def _mlstm_bwd(proj, gates, bias, h, dh, stats, cs, ns, ms, dproj):
    n = proj.shape[0]
    nc = n // LCH

    def body(q_ref, k_ref, v_ref, g_ref, bias_ref, h_ref, dh_ref, st_ref,
             cs_ref, ns_ref, ms_ref, dproj_in,
             dqkv_ref, dg_ref, gsum_ref, dc_scr, dn_scr):
        del dproj_in

        @pl.when(pl.program_id(0) == 0)
        def _():
            dc_scr[...] = jnp.zeros_like(dc_scr)
            dn_scr[...] = jnp.zeros_like(dn_scr)
            gsum_ref[...] = jnp.zeros_like(gsum_ref)

        g = g_ref[...] + bias_ref[...]
        gt = g.T[0:8, :]
        stats = st_ref[...]
        r = lax.broadcasted_iota(jnp.int32, (LCH, LCH), 0)
        c = lax.broadcasted_iota(jnp.int32, (LCH, LCH), 1)
        eye = r == c
        to_col = lambda row: jnp.sum(jnp.where(eye, row, 0.0), axis=1, keepdims=True)
        last = lax.broadcasted_iota(jnp.int32, (LCH, 1), 0) == LCH - 1
        dg = jnp.zeros((LCH, GATE_W), F32)
        for hd in range(NH):
            li_col, fr_col, li_row, fr_row = _gate_vectors(g, gt, hd)
            m_prev = ms_ref[hd][0:1, 0:1]
            tri, dmat, a_col, m_new, w_col, decay = _chunk_decay(
                li_col, li_row, _logsig(fr_col), _logsig(fr_row), m_prev)
            m_col = _lane_get(stats, NH + hd)
            dl = jnp.exp(dmat - m_col)
            inter = jnp.exp(a_col - m_col)

            qs = q_ref[:, hd * DK:(hd + 1) * DK] * (DK ** -0.5)
            kk = k_ref[:, hd * DK:(hd + 1) * DK]
            vv = v_ref[:, hd * DV:(hd + 1) * DV]
            qf = qs.astype(F32)
            kf = kk.astype(F32)
            cst = cs_ref[hd]
            nst = ns_ref[hd]
            cb = cst.astype(MXU)
            dcn = dc_scr[hd]
            dnn = dn_scr[hd]
            dcb = dcn.astype(MXU)

            den = _lane_get(stats, hd)
            floor = jnp.exp(-m_col)
            nrm = jnp.maximum(jnp.abs(den), floor)
            dhv = dh_ref[:, hd * DV:(hd + 1) * DV].astype(F32)
            dnum = dhv / nrm
            dnum_b = dnum.astype(MXU)
            dhh = jnp.sum(dhv * h_ref[:, hd * DV:(hd + 1) * DV].astype(F32), axis=1, keepdims=True)
            dden = jnp.where(jnp.abs(den) > floor, -dhh / nrm * jnp.sign(den), 0.0)

            sc = _dot_nt(qs, kk) * dl
            dsc = _dot_nt(dnum_b, vv) + dden
            da = (dl * dsc).astype(MXU)
            gmat = sc * dsc

            dq = _dot(da, kk) + inter * (_dot_nt(dnum_b, cb) + dden * nst)
            dk_state = w_col * (_dot_nt(vv, dcb) + dnn)
            dk = _dot_tn(da, qs) + dk_state
            kw = (kf * w_col).astype(MXU)
            dv = _dot_tn(sc.astype(MXU), dnum_b) + _dot(kw, dcb)
            dqkv_ref[:, hd * DK:(hd + 1) * DK] = (dq * (DK ** -0.5)).astype(dqkv_ref.dtype)
            dqkv_ref[:, QK + hd * DK:QK + (hd + 1) * DK] = dk.astype(dqkv_ref.dtype)
            dqkv_ref[:, 2 * QK + hd * DV:2 * QK + (hd + 1) * DV] = dv.astype(dqkv_ref.dtype)

            num_i = _dot(qs, cb)
            den_i = jnp.sum(qf * nst, axis=1, keepdims=True)
            e_col = inter * (jnp.sum(dnum * num_i, axis=1, keepdims=True) + dden * den_i)
            h_col = jnp.sum(kf * dk_state, axis=1, keepdims=True)
            f_dec = decay * (jnp.sum(jnp.sum(cst * dcn, axis=1, keepdims=True), axis=0, keepdims=True)
                             + jnp.sum(nst * dnn, axis=1, keepdims=True))
            row_g = jnp.sum(gmat, axis=1, keepdims=True)
            col_g = to_col(jnp.sum(gmat, axis=0, keepdims=True))
            db_col = row_g - col_g + e_col - h_col
            db_col = db_col + jnp.where(last, jnp.sum(h_col, axis=0, keepdims=True) + f_dec, 0.0)
            dli_col = col_g + h_col
            dlf_row = jnp.sum(jnp.where(tri, db_col, 0.0), axis=0, keepdims=True)
            df_col = to_col(dlf_row) * _sigmoid(-fr_col)
            dg = dg + _lane_put(dli_col, hd) + _lane_put(df_col, NH + hd)

            dc_scr[hd] = decay * dcn + _dot_tn((qf * inter).astype(MXU), dnum_b)
            dn_scr[hd] = decay * dnn + jnp.sum(qf * (inter * dden), axis=0, keepdims=True)
        dg_ref[...] = dg
        gsum_ref[0:1, 0:GATE_W] += jnp.sum(dg, axis=0, keepdims=True)

    rev = lambda c: nc - 1 - c
    q_spec, k_spec, v_spec = _qkv_specs(rev)
    hv_spec = pl.BlockSpec((LCH, VD), lambda c: (rev(c), 0))
    gate_spec = pl.BlockSpec((LCH, GATE_W), lambda c: (rev(c), 0))
    return pl.pallas_call(
        body, name="mlstm_bwd", grid=(nc,),
        in_specs=[q_spec, k_spec, v_spec, gate_spec,
                  pl.BlockSpec((1, GATE_W), lambda c: (0, 0)),
                  hv_spec, hv_spec, gate_spec] + _state_specs(rev) + [ANY],
        out_specs=[pl.BlockSpec((LCH, QKV_W), lambda c: (rev(c), C_Q // QKV_W)),
                   gate_spec,
                   pl.BlockSpec((8, D), lambda c: (0, 0))],
        out_shape=[jax.ShapeDtypeStruct(dproj.shape, dproj.dtype),
                   jax.ShapeDtypeStruct((n, GATE_W), F32),
                   jax.ShapeDtypeStruct((8, D), F32)],
        scratch_shapes=[pltpu.VMEM((NH, DK, DV), F32), pltpu.VMEM((NH, 1, DK), F32)],
        input_output_aliases={11: 0},
        compiler_params=_cparams("arbitrary"),
    )(proj, proj, proj, gates, bias, h, dh, stats, cs, ns, ms, dproj)


def _proj_spec(tm, col, width):
    return pl.BlockSpec((tm, width), lambda i: (i, col // width))


def _halo_prev(tm, col):
    return pl.BlockSpec((8, 1024), lambda i: (jnp.maximum(i * (tm // 8) - 1, 0), col // 1024))


def _const(shape):
    return pl.BlockSpec(shape, lambda i: (0,) * len(shape))


def _conv_inputs(i, tm, xa_ref, ca_ref, xah_ref, cah_ref):
    u = _f32(xa_ref) * _f32(ca_ref)
    uh = jnp.where(i > 0, _f32(xah_ref) * _f32(cah_ref), 0.0)
    rid = lax.broadcasted_iota(jnp.int32, u.shape, 0)
    u1 = jnp.where(rid == 0, uh[7:8, :], pltpu.roll(u, 1, 0))
    u2 = jnp.where(rid == 0, uh[6:7, :], jnp.where(rid == 1, uh[7:8, :], pltpu.roll(u, 2, 0)))
    return u, u1, u2


def _head_norm(hh, gh):
    out = []
    for j in range(NH):
        hj = hh[:, j * DV:(j + 1) * DV]
        rj = _rstd(hj)
        out.append((hj * rj, rj, gh[:, j * DV:(j + 1) * DV]))
    return out


def _tail_fwd(proj, h, x, p, t, cw, cb, gh, gple, gfin, wa, wb, wo, wpg, wp):
    n = x.shape[0]
    tm = min(256, n)

    def body(ba_r, za_r, o_r, zb_r, ga_r, gb_r, xa_r, ca_r, xah_r, cah_r, h_r, x_r, p_r, t_r,
             cw_r, cb_r, gh_r, gple_r, gfin_r, wa_r, wb_r, wo_r, wpg_r, wp_r,
             apre_o, bpre_o, mg_o, xn1_o, de_o, dgp_o, ya_o, yb_o, x1_o, dx2_o, acc_o):
        i = pl.program_id(0)

        @pl.when(i == 0)
        def _():
            acc_o[...] = jnp.zeros_like(acc_o)

        u, u1, u2 = _conv_inputs(i, tm, xa_r, ca_r, xah_r, cah_r)
        cwv = cw_r[...]
        cv = cwv[0:1, :] * u2 + cwv[1:2, :] * u1 + cwv[2:3, :] * u + cb_r[...]
        za = za_r[...]
        a_pre = ba_r[...] * cv.astype(MXU) * (za * _sigmoid(za))
        apre_o[...] = a_pre
        ya = _dot(a_pre, wa_r[...]).astype(MXU)

        hn = _head_norm(_f32(h_r), gh_r[...])
        hbn = jnp.concatenate([xh * g for xh, _, g in hn], axis=1)
        zb = zb_r[...]
        b_pre = _sigmoid(o_r[...]) * hbn.astype(MXU) * (zb * _sigmoid(zb))
        bpre_o[...] = b_pre
        yb = _dot(b_pre, wb_r[...]).astype(MXU)
        ya_o[...] = ya
        yb_o[...] = yb

        mg = _sigmoid(ga_r[...]) * ya + _sigmoid(gb_r[...]) * yb
        mg_o[...] = mg
        x1 = x_r[...] + _dot(mg, wo_r[...])
        x1_o[...] = x1.astype(MXU)
        xn1 = (x1 * _rstd(x1) * gple_r[...]).astype(MXU)
        xn1_o[...] = xn1
        gt = _sigmoid(_dot(xn1, wpg_r[...]))
        e = _dot(p_r[...].astype(MXU), wp_r[...])
        x2 = x1 + gt * e
        r2 = _rstd(x2)
        xh2 = x2 * r2
        gf = gfin_r[...]
        diff = xh2 * gf - t_r[...]
        dy = diff * (1.0 / D)
        dx2 = _norm_bwd(dy, xh2, r2, gf)
        dx2_o[...] = dx2
        de_o[...] = (dx2 * gt).astype(MXU)
        dgp_o[...] = (dx2 * e * gt * (1.0 - gt)).astype(MXU)
        acc_o[0:1, :] += jnp.sum(dy * xh2, axis=0, keepdims=True)
        loss = 0.5 * jnp.sum(jnp.sum(diff * diff, axis=1, keepdims=True), axis=0, keepdims=True) * (1.0 / D)
        acc_o[1:2, :] += jnp.broadcast_to(loss, (1, D))

    row = lambda w, dt: (pl.BlockSpec((tm, w), lambda i: (i, 0)), jax.ShapeDtypeStruct((n, w), dt))
    outs = [row(D, MXU), row(VD, MXU), row(D, MXU), row(D, MXU), row(D, MXU), row(D, MXU),
            row(D, MXU), row(D, MXU), row(D, MXU), row(D, F32),
            (_const((8, D)), jax.ShapeDtypeStruct((8, D), F32))]
    return pl.pallas_call(
        body, name="tail_fwd", grid=(n // tm,),
        in_specs=[_proj_spec(tm, C_BA, 1024), _proj_spec(tm, C_ZA, 1024),
                  _proj_spec(tm, C_O, 2048), _proj_spec(tm, C_ZB, 2048),
                  _proj_spec(tm, C_GA, 1024), _proj_spec(tm, C_GB, 1024),
                  _proj_spec(tm, C_XA, 1024), _proj_spec(tm, C_CA, 1024),
                  _halo_prev(tm, C_XA), _halo_prev(tm, C_CA),
                  pl.BlockSpec((tm, VD), lambda i: (i, 0)),
                  pl.BlockSpec((tm, D), lambda i: (i, 0)),
                  pl.BlockSpec((tm, PLE), lambda i: (i, 0)),
                  pl.BlockSpec((tm, D), lambda i: (i, 0)),
                  _const((8, D)), _const((1, D)), _const((1, VD)), _const((1, D)), _const((1, D)),
                  _const((D, D)), _const((VD, D)), _const((D, D)), _const((D, D)), _const((PLE, D))],
        out_specs=[s for s, _ in outs],
        out_shape=[s for _, s in outs],
        compiler_params=_cparams("arbitrary"),
    )(*([proj] * 10), h, x, p, t, cw, cb, gh, gple, gfin, wa, wb, wo, wpg, wp)


def _tail_bwd(proj, h, dgp, dx2, x1, ya, yb, cw, cb, gh, gple, wpg, wo, wb, wa):
    n = x1.shape[0]
    tm = min(256, n)

    def body(ba_r, za_r, o_r, zb_r, ga_r, gb_r, xa_r, ca_r, xah_r, cah_r, h_r,
             dgp_r, dx2_r, x1_r, ya_r, yb_r, cw_r, cb_r, gh_r, gple_r,
             wpg_r, wo_r, wb_r, wa_r,
             dproj_o, dcv_o, dh_o, dx1_o, dx1b_o, dya_o, dyb_o, acc_o):
        i = pl.program_id(0)

        @pl.when(i == 0)
        def _():
            acc_o[...] = jnp.zeros_like(acc_o)

        dxn1 = _dot_nt(dgp_r[...], wpg_r[...])
        x1 = _f32(x1_r)
        r1 = _rstd(x1)
        xh1 = x1 * r1
        acc_o[0:1, 0:D] += jnp.sum(dxn1 * xh1, axis=0, keepdims=True)
        dx1 = dx2_r[...] + _norm_bwd(dxn1, xh1, r1, gple_r[...])
        dx1_o[...] = dx1
        dx1b = dx1.astype(MXU)
        dx1b_o[...] = dx1b

        dmg = _dot_nt(dx1b, wo_r[...]).astype(MXU)
        sga, dsga = _gate(ga_r[...])
        sgb, dsgb = _gate(gb_r[...])
        dya = dmg * sga
        dyb = dmg * sgb
        dya_o[...] = dya
        dyb_o[...] = dyb
        dproj_o[:, C_GA:C_GA + D] = dmg * ya_r[...] * dsga
        dproj_o[:, C_GB:C_GB + D] = dmg * yb_r[...] * dsgb

        db_pre = _dot_nt(dyb, wb_r[...]).astype(MXU)
        hn = _head_norm(_f32(h_r), gh_r[...])
        hbn = jnp.concatenate([xh * g for xh, _, g in hn], axis=1).astype(MXU)
        so, dso = _gate(o_r[...])
        zb = zb_r[...]
        szb, dszb = _gate(zb)
        sb = zb * szb
        t1 = db_pre * hbn
        dproj_o[:, C_O:C_O + VD] = t1 * (sb * dso)
        dproj_o[:, C_ZB:C_ZB + VD] = t1 * (so * (szb + zb * dszb))
        dhbn = (db_pre * (so * sb)).astype(F32)
        for j, (xh, rj, g) in enumerate(hn):
            dj = dhbn[:, j * DV:(j + 1) * DV]
            acc_o[1:2, j * DV:(j + 1) * DV] += jnp.sum(dj * xh, axis=0, keepdims=True)
            dh_o[:, j * DV:(j + 1) * DV] = _norm_bwd(dj, xh, rj, g).astype(MXU)

        da_pre = _dot_nt(dya, wa_r[...]).astype(MXU)
        u, u1, u2 = _conv_inputs(i, tm, xa_r, ca_r, xah_r, cah_r)
        cwv = cw_r[...]
        cv = cwv[0:1, :] * u2 + cwv[1:2, :] * u1 + cwv[2:3, :] * u + cb_r[...]
        za = za_r[...]
        sza, dsza = _gate(za)
        sa = za * sza
        ba = ba_r[...]
        t2 = da_pre * cv.astype(MXU)
        dproj_o[:, C_BA:C_BA + D] = t2 * sa
        dproj_o[:, C_ZA:C_ZA + D] = t2 * (ba * (sza + za * dsza))
        dcv_b = da_pre * (ba * sa)
        dcv_o[...] = dcv_b
        dcv = dcv_b.astype(F32)
        acc_o[2:3, 0:D] += jnp.sum(dcv, axis=0, keepdims=True)
        acc_o[3:4, 0:D] += jnp.sum(dcv * u2, axis=0, keepdims=True)
        acc_o[4:5, 0:D] += jnp.sum(dcv * u1, axis=0, keepdims=True)
        acc_o[5:6, 0:D] += jnp.sum(dcv * u, axis=0, keepdims=True)

    row = lambda w, dt: (pl.BlockSpec((tm, w), lambda i: (i, 0)), jax.ShapeDtypeStruct((n, w), dt))
    outs = [(pl.BlockSpec((tm, TAIL_W), lambda i: (i, 0)), jax.ShapeDtypeStruct((n, NMAIN), MXU)),
            row(D, MXU), row(VD, MXU), row(D, F32), row(D, MXU), row(D, MXU), row(D, MXU),
            (_const((8, VD)), jax.ShapeDtypeStruct((8, VD), F32))]
    rowin = lambda w: pl.BlockSpec((tm, w), lambda i: (i, 0))
    return pl.pallas_call(
        body, name="tail_bwd", grid=(n // tm,),
        in_specs=[_proj_spec(tm, C_BA, 1024), _proj_spec(tm, C_ZA, 1024),
                  _proj_spec(tm, C_O, 2048), _proj_spec(tm, C_ZB, 2048),
                  _proj_spec(tm, C_GA, 1024), _proj_spec(tm, C_GB, 1024),
                  _proj_spec(tm, C_XA, 1024), _proj_spec(tm, C_CA, 1024),
                  _halo_prev(tm, C_XA), _halo_prev(tm, C_CA),
                  rowin(VD), rowin(D), rowin(D), rowin(D), rowin(D), rowin(D),
                  _const((8, D)), _const((1, D)), _const((1, VD)), _const((1, D)),
                  _const((D, D)), _const((D, D)), _const((VD, D)), _const((D, D))],
        out_specs=[s for s, _ in outs],
        out_shape=[s for _, s in outs],
        compiler_params=_cparams("arbitrary"),
    )(*([proj] * 10), h, dgp, dx2, x1, ya, yb, cw, cb, gh, gple, wpg, wo, wb, wa)


def _conv_bwd(proj, dcv, cw, dproj):
    n = dcv.shape[0]
    tm = min(512, n)
    nt = n // tm

    def body(xa_r, ca_r, dcv_r, nxt_r, cw_r, dproj_in, dxc_o):
        del dproj_in
        i = pl.program_id(0)
        dcv_v = _f32(dcv_r)
        nxt = jnp.where(i < nt - 1, _f32(nxt_r), 0.0)
        rid = lax.broadcasted_iota(jnp.int32, dcv_v.shape, 0)
        d1 = jnp.where(rid == tm - 1, nxt[0:1, :], pltpu.roll(dcv_v, tm - 1, 0))
        d2 = jnp.where(rid == tm - 2, nxt[0:1, :],
                       jnp.where(rid == tm - 1, nxt[1:2, :], pltpu.roll(dcv_v, tm - 2, 0)))
        cwv = cw_r[...]
        du = cwv[2:3, :] * dcv_v + cwv[1:2, :] * d1 + cwv[0:1, :] * d2
        dxc_o[:, 0:D] = (du * _f32(ca_r)).astype(MXU)
        dxc_o[:, D:2 * D] = (du * _f32(xa_r)).astype(MXU)

    return pl.pallas_call(
        body, name="conv_bwd", grid=(nt,),
        in_specs=[_proj_spec(tm, C_XA, 1024), _proj_spec(tm, C_CA, 1024),
                  pl.BlockSpec((tm, D), lambda i: (i, 0)),
                  pl.BlockSpec((8, D), lambda i: (jnp.minimum((i + 1) * (tm // 8), n // 8 - 1), 0)),
                  _const((8, D)), ANY],
        out_specs=pl.BlockSpec((tm, CONV_W), lambda i: (i, C_XA // CONV_W)),
        out_shape=jax.ShapeDtypeStruct(dproj.shape, dproj.dtype),
        input_output_aliases={5: 0},
        compiler_params=_cparams("arbitrary"),
    )(proj, proj, dcv, dcv, cw, dproj)


def _position():
    return lax.axis_index("x"), lax.axis_index("y"), lax.axis_index("c")


def _flip(v, bit):
    return 1 - v if bit else v


def _part_start(core, stride, tiled):
    return pl.multiple_of(core * stride, stride) if tiled else core * stride


def _scatter_copies(srcs, dsts, strides, send_sems, recv_sems, local_sems):
    x, y, c = _position()
    me = 4 * x + 2 * y + c
    na = len(srcs)
    copies = []
    for r in range(N_DEV):
        px, py, pc = _flip(x, (r >> 2) & 1), _flip(y, (r >> 1) & 1), _flip(c, r & 1)
        for a in range(na):
            rows = dsts[a].shape[1]
            src = srcs[a].at[2 * px + py, pl.ds(_part_start(pc, strides[a], len(dsts[a].shape) == 3), rows)]
            dst = dsts[a].at[me]
            if r == 0:
                copies.append(pltpu.make_async_copy(src, dst, local_sems.at[a]))
            else:
                k = (r - 1) * na + a
                copies.append(pltpu.make_async_remote_copy(
                    src_ref=src, dst_ref=dst, send_sem=send_sems.at[k], recv_sem=recv_sems.at[k],
                    device_id=(px, py, pc), device_id_type=MESH))
    return copies


def _input_grad(dproj, dgates, w_t, wg_t, x, dx1, g_mix, g_in, g_rest):
    n = x.shape[0]
    tm, tk = min(1024, n), 2048
    nk = NMAIN // tk
    nt = n // tm

    def body(dp_r, dg_r, w_r, wg_r, x_r, dx1_r, g_r, gin, grest,
             gx_o, acc_o, oin, orest, acc, send_sems, recv_sems, local_sems):
        i = pl.program_id(0)
        kk = pl.program_id(1)
        copies = _scatter_copies((gin, grest), (oin, orest), strides, send_sems, recv_sems, local_sems)

        @pl.when((i == 0) & (kk == 0))
        def _():
            acc_o[...] = jnp.zeros_like(acc_o)
            for cp in copies:
                cp.start()

        @pl.when(kk == 0)
        def _():
            acc[...] = _dot(dg_r[...].astype(MXU), wg_r[...])

        acc[...] += _dot(dp_r[...], w_r[...])

        @pl.when(kk == nk - 1)
        def _():
            dhn = acc[...]
            xv = x_r[...]
            r0 = _rstd(xv)
            xh = xv * r0
            acc_o[0:1, :] += jnp.sum(dhn * xh, axis=0, keepdims=True)
            gx_o[...] = dx1_r[...] + _norm_bwd(dhn, xh, r0, g_r[...])

        @pl.when((i == nt - 1) & (kk == nk - 1))
        def _():
            for cp in copies:
                cp.wait()

    nrem = 2 * (N_DEV - 1)
    r_in, r_rest = HALF_SLABS, g_rest.shape[1] // 2
    strides = (HALF_START, r_rest)
    return pl.pallas_call(
        body, name="input_grad", grid=(nt, nk),
        in_specs=[pl.BlockSpec((tm, tk), lambda i, kk: (i, kk)),
                  pl.BlockSpec((tm, GATE_W), lambda i, kk: (i, 0)),
                  pl.BlockSpec((tk, D), lambda i, kk: (kk, 0)),
                  pl.BlockSpec((GATE_W, D), lambda i, kk: (0, 0)),
                  pl.BlockSpec((tm, D), lambda i, kk: (i, 0)),
                  pl.BlockSpec((tm, D), lambda i, kk: (i, 0)),
                  pl.BlockSpec((1, D), lambda i, kk: (0, 0)),
                  ANY, ANY],
        out_specs=[pl.BlockSpec((tm, D), lambda i, kk: (i, 0)),
                   pl.BlockSpec((8, D), lambda i, kk: (0, 0)),
                   ANY, ANY],
        out_shape=[jax.ShapeDtypeStruct((n, D), F32), jax.ShapeDtypeStruct((8, D), F32),
                   jax.ShapeDtypeStruct((N_DEV, r_in) + g_in.shape[2:], g_in.dtype),
                   jax.ShapeDtypeStruct((N_DEV, r_rest) + g_rest.shape[2:], g_rest.dtype)],
        scratch_shapes=[pltpu.VMEM((tm, D), F32),
                        pltpu.SemaphoreType.DMA((nrem,)), pltpu.SemaphoreType.DMA((nrem,)),
                        pltpu.SemaphoreType.DMA((2,))],
        compiler_params=_cparams("arbitrary", "arbitrary"),
    )(dproj, dgates, w_t, wg_t, x, dx1, g_mix, g_in, g_rest)


def _pack_small(acc_f, acc_b, acc_x, gsum):
    def body(f_r, b_r, x_r, s_r, o_r):
        o_r[...] = jnp.zeros_like(o_r)
        o_r[0:1, :] = x_r[0:1, :]
        o_r[1:2, :] = b_r[2:3, 0:D]
        o_r[2:3, :] = b_r[1:2, 0:D]
        o_r[3:4, :] = b_r[1:2, D:2 * D]
        o_r[4:5, :] = b_r[0:1, 0:D]
        o_r[5:6, :] = f_r[0:1, :]
        o_r[6:9, :] = b_r[3:6, 0:D]
        lane = lax.broadcasted_iota(jnp.int32, (1, D), 1)
        o_r[9:10, :] = jnp.where(lane < 2 * NH, s_r[0:1, :], jnp.where(lane == 2 * NH, f_r[1:2, :], 0.0))

    return pl.pallas_call(
        body, name="pack_small",
        out_shape=jax.ShapeDtypeStruct((SMALL_ROWS, D), F32),
    )(acc_f, acc_b, acc_x, gsum)


def _sum_slots(r, name, tr=64, half=None):
    s, rows, w = r.shape
    tr = min(tr, rows)
    assert rows % tr == 0
    nt = rows // tr

    def body(*refs):
        r_ref, o_ref = refs[-2:]
        tot = r_ref[0].astype(F32)
        for k in range(1, s):
            tot = tot + r_ref[k].astype(F32)
        o_ref[...] = tot

    if half is None:
        return pl.pallas_call(
            body, name=name, grid=(nt,),
            in_specs=[pl.BlockSpec((s, tr, w), lambda i: (0, i, 0))],
            out_specs=pl.BlockSpec((tr, w), lambda i: (i, 0)),
            out_shape=jax.ShapeDtypeStruct((rows, w), F32),
            compiler_params=_cparams("arbitrary"),
        )(r)
    return pl.pallas_call(
        body, name=name,
        grid_spec=pltpu.PrefetchScalarGridSpec(
            num_scalar_prefetch=1, grid=(nt,),
            in_specs=[pl.BlockSpec((s, tr, w), lambda i, hf: (0, i, 0))],
            out_specs=pl.BlockSpec((tr, w), lambda i, hf: (hf[0] * nt + i, 0))),
        out_shape=jax.ShapeDtypeStruct((2 * rows, w), F32),
        compiler_params=_cparams("arbitrary"),
    )(half, r)


def _cast_slabs(a, tr=163):
    rows = a.shape[0]
    assert rows % tr == 0

    def body(a_ref, o_ref):
        o_ref[...] = a_ref[...].astype(MXU)

    spec = pl.BlockSpec((tr,) + a.shape[1:], lambda i: (i, 0, 0))
    return pl.pallas_call(
        body, name="cast_w_in", grid=(rows // tr,), in_specs=[spec], out_specs=spec,
        out_shape=jax.ShapeDtypeStruct(a.shape, MXU), compiler_params=_cparams("arbitrary"),
    )(a)


def _sum_slabs(r, name, tr=69):
    s, rows = r.shape[:2]
    assert rows % tr == 0

    def body(r_ref, o_ref):
        tot = r_ref[0].astype(F32)
        for k in range(1, s):
            tot = tot + r_ref[k].astype(F32)
        o_ref[...] = tot

    return pl.pallas_call(
        body, name=name, grid=(rows // tr,),
        in_specs=[pl.BlockSpec((s, tr) + r.shape[2:], lambda i: (0, i, 0, 0))],
        out_specs=pl.BlockSpec((tr,) + r.shape[2:], lambda i: (i, 0, 0)),
        out_shape=jax.ShapeDtypeStruct(r.shape[1:], F32),
        compiler_params=_cparams("arbitrary"),
    )(r)


def _adamw_body(w_r, g_r, m_r, v_r, d_o, m_o, v_o):
    c1 = 1.0 - ADAM_B1 ** ADAM_STEP
    c2 = 1.0 - ADAM_B2 ** ADAM_STEP
    gv = g_r[...]
    mn = ADAM_B1 * m_r[...] + (1.0 - ADAM_B1) * gv
    vn = ADAM_B2 * v_r[...] + (1.0 - ADAM_B2) * (gv * gv)
    m_o[...] = mn
    v_o[...] = vn
    d_o[...] = -ADAM_LR * ((mn / c1) / (jnp.sqrt(vn / c2) + ADAM_EPS) + ADAM_WD * w_r[...])


def _adamw_slabs(w, g, m, v, name, tr):
    rows = w.shape[0]
    assert rows % tr == 0 and w.shape == g.shape

    def body(*refs):
        _adamw_body(*refs)

    spec = pl.BlockSpec((tr,) + w.shape[1:], lambda i: (i, 0, 0))
    shp = jax.ShapeDtypeStruct(w.shape, F32)
    return pl.pallas_call(
        body, name=name, grid=(rows // tr,),
        in_specs=[spec] * 4, out_specs=[spec] * 3, out_shape=[shp] * 3,
        compiler_params=_cparams("arbitrary"),
    )(w, g, m, v)


def _adamw(w, g, m, v, name, g_row0=0, tr=64):
    lead = w.ndim == 3
    rows, cols = w.shape[-2:]
    tr = min(tr, rows)
    assert rows % tr == 0 and g_row0 % tr == 0 and g.shape[1] == cols

    def body(*refs):
        _adamw_body(*refs)

    if lead:
        spec = pl.BlockSpec((None, tr, cols), lambda i: (0, i, 0))
    else:
        spec = pl.BlockSpec((tr, cols), lambda i: (i, 0))
    g_spec = pl.BlockSpec((tr, cols), lambda i: (g_row0 // tr + i, 0))
    shp = jax.ShapeDtypeStruct(w.shape, F32)
    return pl.pallas_call(
        body, name=name, grid=(rows // tr,),
        in_specs=[spec, g_spec, spec, spec], out_specs=[spec] * 3, out_shape=[shp] * 3,
        compiler_params=_cparams("arbitrary"),
    )(w, g, m, v)


def _allgather_chips(halved, whole):
    arrs = [a for a, _ in halved] + list(whole)
    parts = [p for _, p in halved]
    nh, na = len(halved), len(arrs)
    relations = ((1, 0), (0, 1), (1, 1))
    n_ici = 3 * na

    def body(*refs):
        ins, outs = refs[:na], refs[na:2 * na]
        send_sems, recv_sems = refs[2 * na:]
        x, y, c = _position()
        mine = 2 * x + y

        def rows(a, core, stride, size):
            return pl.ds(_part_start(core, stride, len(ins[a].shape) == 2), size)

        def remote(src, dst, k, to):
            return pltpu.make_async_remote_copy(src_ref=src, dst_ref=dst, send_sem=send_sems.at[k],
                                                recv_sem=recv_sems.at[k], device_id=to, device_id_type=MESH)

        first, passed = [], []
        for j, (rx, ry) in enumerate(relations):
            px, py = _flip(x, rx), _flip(y, ry)
            for a in range(na):
                if a < nh:
                    s1, n1, s2, n2 = parts[a]
                    first.append(remote(ins[a].at[rows(a, c, s1, n1)], outs[a].at[mine, rows(a, c, s1, n1)],
                                        j * na + a, (px, py, c)))
                    landed = outs[a].at[2 * px + py, rows(a, c, s2, n2)]
                    passed.append(remote(landed, landed, n_ici + j * nh + a, (x, y, 1 - c)))
                else:
                    first.append(remote(ins[a], outs[a].at[mine], j * na + a, (px, py, c)))
        for cp in first:
            cp.start()
        k = 0
        for j in range(3):
            for a in range(na):
                first[j * na + a].wait_recv()
                if a < nh:
                    passed[k].start()
                    k += 1
        for cp in passed:
            cp.wait_recv()
        for cp in first + passed:
            cp.wait_send()

    nsem = n_ici + 3 * nh
    outs = pl.pallas_call(
        body, name="allgather_weights",
        in_specs=[ANY] * na, out_specs=[ANY] * na,
        out_shape=[jax.ShapeDtypeStruct((N_CHIPS,) + a.shape, a.dtype) for a in arrs],
        scratch_shapes=[pltpu.SemaphoreType.DMA((nsem,)), pltpu.SemaphoreType.DMA((nsem,))],
    )(*arrs)
    chip = 2 * lax.axis_index("x") + lax.axis_index("y")
    return [lax.dynamic_update_slice(o, a[None], (chip,) + (0,) * a.ndim) for o, a in zip(outs, arrs)]


def _join_halves(piece, full_rest, small):
    def body(pc_ref, rest_in, sm, oin, orest, osm, send_sems, recv_sems, local_sem):
        del rest_in
        x, y, c = _position()
        me = 4 * x + 2 * y + c
        sib = (x, y, 1 - c)
        rows = orest.shape[0] // 2
        mine = orest.at[pl.ds(pl.multiple_of(c * rows, rows), rows)]
        copies = [
            pltpu.make_async_remote_copy(
                src_ref=pc_ref.at[pl.ds(c, HALF_START)], dst_ref=oin.at[pl.ds(c * HALF_SLABS, HALF_START)],
                send_sem=send_sems.at[0], recv_sem=recv_sems.at[0], device_id=sib, device_id_type=MESH),
            pltpu.make_async_remote_copy(
                src_ref=mine, dst_ref=mine, send_sem=send_sems.at[1], recv_sem=recv_sems.at[1],
                device_id=sib, device_id_type=MESH)]
        na = 2
        copies.append(pltpu.make_async_copy(sm, osm.at[me], local_sem))
        for r in range(1, N_DEV):
            to = (_flip(x, (r >> 2) & 1), _flip(y, (r >> 1) & 1), _flip(c, r & 1))
            copies.append(pltpu.make_async_remote_copy(
                src_ref=sm, dst_ref=osm.at[me], send_sem=send_sems.at[na + r - 1],
                recv_sem=recv_sems.at[na + r - 1], device_id=to, device_id_type=MESH))
        for cp in copies:
            cp.start()
        for cp in copies:
            cp.wait()

    nsem = 2 + N_DEV - 1
    return pl.pallas_call(
        body, name="join_halves",
        in_specs=[ANY] * 3, out_specs=[ANY] * 3,
        out_shape=[jax.ShapeDtypeStruct((SLABS_CHIP,) + piece.shape[1:], piece.dtype),
                   jax.ShapeDtypeStruct(full_rest.shape, full_rest.dtype),
                   jax.ShapeDtypeStruct((N_DEV,) + small.shape, small.dtype)],
        scratch_shapes=[pltpu.SemaphoreType.DMA((nsem,)), pltpu.SemaphoreType.DMA((nsem,)),
                        pltpu.SemaphoreType.DMA],
        input_output_aliases={1: 1},
    )(piece, full_rest, small)


def _to_internal(w_slabs):
    order = sorted(SEGMENTS, key=lambda s: s[2])
    main = jnp.concatenate([w_slabs[g0 // 2:(g0 + w) // 2] for g0, w, _ in order], axis=0)
    gate = w_slabs[GATE_COL // 2:GATE_COL // 2 + NH].reshape(2 * NH, D)
    return main.reshape(NMAIN, D), jnp.pad(gate, ((0, GATE_W - 2 * NH), (0, 0)))


def _to_global(main_t, gate_t):
    main = main_t.reshape(NMAIN // 2, 16, D // 8)
    parts = sorted([(g0, main[i0 // 2:(i0 + w) // 2]) for g0, w, i0 in SEGMENTS]
                   + [(GATE_COL, gate_t[0:2 * NH].reshape(NH, 16, D // 8))], key=lambda s: s[0])
    return jnp.concatenate([p for _, p in parts], axis=0)


def _pack_rows(wa, wb, wo, wpg, wp):
    return jnp.concatenate([wa, wb, wo, wpg, wp.reshape(PACK_ROWS[4], D)], axis=0)


def _pad_rows(a, rows=8):
    return jnp.pad(a, ((0, rows - a.shape[0]), (0, 0)))


def kernel(x, p, g_mix, w_in, conv_w, conv_b, w_a_out, b_gates, g_head, w_b_out, w_o, g_ple, w_ple_gate, w_ple, g_final, loss_target, m_g_mix, m_w_in, m_conv_w, m_conv_b, m_w_a_out, m_b_gates, m_g_head, m_w_b_out, m_w_o, m_g_ple, m_w_ple_gate, m_w_ple, m_g_final, v_g_mix, v_w_in, v_conv_w, v_conv_b, v_w_a_out, v_b_gates, v_g_head, v_w_b_out, v_w_o, v_g_ple, v_w_ple_gate, v_w_ple, v_g_final):
    chip = 2 * lax.axis_index("x") + lax.axis_index("y")
    xs, ps, ts = x[0], p[0, 0], loss_target[0]
    g_fin = g_final.reshape(1, D)

    pack_w = _pack_rows(w_a_out[0], w_b_out[0], w_o[0], w_ple_gate[0], w_ple[0])
    w_slabs = _cast_slabs(jnp.transpose(w_in, (2, 0, 1)).reshape(SLABS_CHIP, 16, D // 8))
    half_rows = PACK_TOTAL // 2
    g_win, g_pack, g_cw = _allgather_chips(
        [(w_slabs, (HALF_START, HALF_SLABS, HALF_SLABS, HALF_START)),
         (pack_w.astype(MXU), (half_rows, half_rows, half_rows, half_rows))], [_pad_rows(conv_w[0])])
    w_t, wg_t = _to_internal(g_win.reshape(N_IN // 2, 16, D // 8))
    offs = [0, 256, 768, 1024, 1280, 1344]
    wa, wb, wo, wpg = [g_pack[:, offs[k]:offs[k + 1]].reshape(-1, D) for k in range(4)]
    wp = jnp.transpose(g_pack[:, offs[4]:offs[5]].reshape(N_CHIPS, PLE, PLE), (1, 0, 2)).reshape(PLE, D)
    cw = jnp.transpose(g_cw, (1, 0, 2)).reshape(8, D)

    bias = jnp.pad(b_gates, ((0, 0), (0, GATE_W - 2 * NH)))
    hn, gates = _prenorm(xs, g_mix, wg_t)
    proj = _proj(hn, w_t)
    h, stats, cs, ns, ms = _mlstm_fwd(proj, gates, bias)
    (a_pre, b_pre, mg, xn1, de, dgp, ya, yb, x1, dx2, acc_f) = _tail_fwd(
        proj, h, xs, ps, ts, cw, conv_b, g_head, g_ple, g_fin, wa, wb, wo, wpg, wp)
    dproj, dcv, dh, dx1, dx1b, dya, dyb, acc_b = _tail_bwd(
        proj, h, dgp, dx2, x1, ya, yb, cw, conv_b, g_head, g_ple, wpg, wo, wb, wa)
    dproj = _conv_bwd(proj, dcv, cw, dproj)
    dproj, dgates, gsum = _mlstm_bwd(proj, gates, bias, h, dh, stats, cs, ns, ms, dproj)
    d_main = _matmul_tn(dproj, hn, "dw_in", out_dtype=WIRE, tk=4096)
    d_gate = _matmul_tn(dgates, hn, "dw_gate", out_dtype=WIRE)
    d_wa = _matmul_tn(a_pre, dya, "dw_a_out", out_dtype=WIRE)
    d_wb = _matmul_tn(b_pre, dyb, "dw_b_out", out_dtype=WIRE)
    d_wo = _matmul_tn(mg, dx1b, "dw_o", out_dtype=WIRE)
    d_wpg = _matmul_tn(xn1, dgp, "dw_ple_gate", out_dtype=WIRE)
    d_wp = _matmul_tn(ps, de, "dw_ple", out_dtype=WIRE)

    g_in = _to_global(d_main, d_gate).reshape(N_CHIPS, SLABS_CHIP, 16, D // 8)
    d_wp_c = jnp.transpose(d_wp.reshape(PLE, N_CHIPS, PLE), (1, 0, 2)).reshape(N_CHIPS, PACK_ROWS[4], D)
    g_rest = jnp.concatenate(
        [d_wa.reshape(N_CHIPS, -1, D), d_wb.reshape(N_CHIPS, -1, D), d_wo.reshape(N_CHIPS, -1, D),
         d_wpg.reshape(N_CHIPS, -1, D), d_wp_c], axis=1)
    grad_x, acc_x, r_in, r_rest = _input_grad(dproj, dgates, w_t, wg_t, xs, dx1, g_mix, g_in, g_rest)
    small = _pack_small(acc_f, acc_b, acc_x, gsum)
    core = lax.axis_index("c").astype(jnp.int32)
    piece = _sum_slabs(r_in, "sum_w_in")
    gw_in, gw_rest, r_small = _join_halves(
        piece, _sum_slots(r_rest, "sum_rest", tr=96, half=core.reshape(1)), small)
    gw_in = lax.dynamic_update_slice(gw_in, piece, (core * HALF_START, 0, 0)).reshape(COLS_CHIP, 8, D // 8)
    gs = _sum_slots(r_small, "sum_small", tr=SMALL_ROWS)

    big = []
    row0 = 0
    for name, w, m, v in (("w_a_out", w_a_out, m_w_a_out, v_w_a_out), ("w_b_out", w_b_out, m_w_b_out, v_w_b_out),
                          ("w_o", w_o, m_w_o, v_w_o), ("w_ple_gate", w_ple_gate, m_w_ple_gate, v_w_ple_gate)):
        big.append((name, w, m, v, gw_rest, row0))
        row0 += w.shape[1]
    g_wp = gw_rest[row0:row0 + PACK_ROWS[4]].reshape(PLE, PLE)
    big.append(("w_ple", w_ple, m_w_ple, v_w_ple, g_wp, 0))
    upd = {name: _adamw(w, g, m, v, "adamw_" + name, g_row0=r0) for name, w, m, v, g, r0 in big}
    g_big = {name: (g if name == "w_ple" else g[r0:r0 + w.shape[1]])[None] for name, w, m, v, g, r0 in big}
    slabs = lambda a: jnp.transpose(a, (2, 0, 1)).reshape(COLS_CHIP, 8, D // 8)
    unslab = lambda a: jnp.transpose(a, (1, 2, 0)).reshape(1, D, COLS_CHIP)
    upd["w_in"] = [unslab(u) for u in _adamw_slabs(
        slabs(w_in), gw_in, slabs(m_w_in), slabs(v_w_in), "adamw_w_in", tr=326)]
    g_big["w_in"] = unslab(gw_in)

    lane = lax.broadcasted_iota(jnp.int32, (1, D), 1)
    g_small = jnp.concatenate([gs[0:6], jnp.where(lane < 2 * NH, gs[9:10], 0.0), jnp.zeros((1, D), F32)], axis=0)

    def small_pack(gm, cb_, bg, gh, gp, gf):
        return jnp.concatenate([gm, cb_, gh.reshape(2, D), gp, gf.reshape(1, D),
                                jnp.pad(bg, ((0, 0), (0, D - 2 * NH))), jnp.zeros((1, D), F32)], axis=0)

    ws = small_pack(g_mix, conv_b, b_gates, g_head, g_ple, g_final)
    ms_ = small_pack(m_g_mix, m_conv_b, m_b_gates, m_g_head, m_g_ple, m_g_final)
    vs = small_pack(v_g_mix, v_conv_b, v_b_gates, v_g_head, v_g_ple, v_g_final)
    upd_s = _adamw(ws, g_small, ms_, vs, "adamw_small")
    g_cw_mine = _pad_rows(lax.dynamic_slice(gs[6:9], (0, chip * PLE), (3, PLE)))
    upd_c = _adamw(_pad_rows(conv_w[0]), g_cw_mine, _pad_rows(m_conv_w[0]), _pad_rows(v_conv_w[0]),
                   "adamw_conv_w")

    def leaves(bigs, sm, cwv):
        return [sm[0:1], bigs["w_in"], cwv[0:3][None], sm[1:2], bigs["w_a_out"], sm[6:7, 0:2 * NH],
                sm[2:4].reshape(1, VD), bigs["w_b_out"], bigs["w_o"], sm[4:5], bigs["w_ple_gate"],
                bigs["w_ple"], sm[5]]

    loss = gs[9, 2 * NH]
    outs = [loss, grad_x[None]] + leaves(g_big, g_small, g_cw_mine)
    for k in range(3):
        outs += leaves({name: u[k] for name, u in upd.items()}, upd_s[k], upd_c[k])
    return tuple(outs)
```

```python
import jax
import jax.numpy as jnp
from jax import lax
from jax.experimental import pallas as pl
from jax.experimental.pallas import tpu as pltpu

F32 = jnp.float32
MXU = jnp.bfloat16
WIRE = jnp.bfloat16

D = 1024
NH, DK, DV = 4, 256, 512
VD = NH * DV
PLE = 256
LCH = 256
EPS = 1e-6
N_IN = 14344
NMAIN = 14336
GATE_W = 128
N_CHIPS, N_DEV = 4, 8

C_BA, C_ZA, C_O, C_ZB, C_GA, C_GB = 0, 1024, 2048, 4096, 6144, 7168
QK = NH * DK
C_Q, C_K, C_V, C_XA, C_CA = 8192, 9216, 10240, 12288, 13312
QKV_W = 2 * QK + VD
TAIL_W = 8192
CONV_W = 2048
SEGMENTS = (
    (0, 1024, C_XA), (1024, 1024, C_BA), (2048, 1024, C_CA), (3072, 1024, C_ZA),
    (4096, QKV_W, C_Q), (8192, 4096, C_O), (12296, 2048, C_GA),
)
GATE_COL = 12288

COLS_CHIP = N_IN // N_CHIPS
SLABS_CHIP = COLS_CHIP // 2
HALF_START = SLABS_CHIP // 2
HALF_SLABS = SLABS_CHIP - HALF_START

PACK_ROWS = (256, 512, 256, 256, 64)
PACK_TOTAL = sum(PACK_ROWS)
SMALL_ROWS = 16

ADAM_LR, ADAM_B1, ADAM_B2, ADAM_EPS, ADAM_WD, ADAM_STEP = 0.001, 0.9, 0.999, 1e-08, 0.01, 10

VMEM_LIMIT = 56 * 1024 * 1024
MESH = pl.DeviceIdType.MESH
ANY = pl.BlockSpec(memory_space=pl.ANY)


def _cparams(*sem):
    return pltpu.CompilerParams(dimension_semantics=sem, vmem_limit_bytes=VMEM_LIMIT)


def _dot(a, b):
    return jnp.dot(a, b, preferred_element_type=F32)


def _dot_nt(a, b):
    return lax.dot_general(a, b, (((1,), (1,)), ((), ())), preferred_element_type=F32)


def _dot_tn(a, b):
    return lax.dot_general(a, b, (((0,), (0,)), ((), ())), preferred_element_type=F32)


def _sigmoid(x):
    return 1.0 / (1.0 + jnp.exp(-x))


def _logsig(x):
    return jnp.minimum(x, 0.0) - jnp.log(1.0 + jnp.exp(-jnp.abs(x)))


GATE_FLOOR = -80.0


def _gate(v):
    e = jnp.exp(-jnp.maximum(v, GATE_FLOOR))
    s = 1.0 / (1.0 + e)
    return s, e * s * s


def _rstd(x):
    return lax.rsqrt(jnp.mean(x * x, axis=-1, keepdims=True) + EPS)


def _norm_bwd(dy, xhat, r, g):
    dxh = dy * g
    return r * (dxh - xhat * jnp.mean(dxh * xhat, axis=-1, keepdims=True))


def _f32(ref):
    return ref[...].astype(F32)


def _proj(x, g_mix, w_t, wg_t, tm=1024, tn=2048):
    n = x.shape[0]
    m = w_t.shape[0]
    tm = min(tm, n)

    def body(x_ref, g_ref, b_ref, wg_ref, o_ref, hn_ref, gate_ref):
        @pl.when(pl.program_id(1) == 0)
        def _():
            xv = x_ref[...]
            hn = (xv * _rstd(xv) * g_ref[...]).astype(MXU)
            hn_ref[...] = hn
            gate_ref[...] = _dot_nt(hn, wg_ref[...])

        o_ref[...] = _dot_nt(hn_ref[...], b_ref[...]).astype(MXU)

    return pl.pallas_call(
        body, name="proj", grid=(n // tm, m // tn),
        in_specs=[pl.BlockSpec((tm, D), lambda i, j: (i, 0)),
                  pl.BlockSpec((1, D), lambda i, j: (0, 0)),
                  pl.BlockSpec((tn, D), lambda i, j: (j, 0)),
                  pl.BlockSpec((GATE_W, D), lambda i, j: (0, 0))],
        out_specs=[pl.BlockSpec((tm, tn), lambda i, j: (i, j)),
                   pl.BlockSpec((tm, D), lambda i, j: (i, 0)),
                   pl.BlockSpec((tm, GATE_W), lambda i, j: (i, 0))],
        out_shape=[jax.ShapeDtypeStruct((n, m), MXU), jax.ShapeDtypeStruct((n, D), MXU),
                   jax.ShapeDtypeStruct((n, GATE_W), F32)],
        compiler_params=_cparams("arbitrary", "arbitrary"),
    )(x, g_mix, w_t, wg_t)


def _matmul_tn(a, b, name, out_dtype=F32, ta=1024, tb=1024, tk=2048):
    n, ka = a.shape
    kb = b.shape[1]
    ta, tb, tk = min(ta, ka), min(tb, kb), min(tk, n)
    nk = n // tk

    def body(a_ref, b_ref, o_ref, acc):
        kk = pl.program_id(2)

        @pl.when(kk == 0)
        def _():
            acc[...] = jnp.zeros_like(acc)

        acc[...] += _dot_tn(a_ref[...].astype(MXU), b_ref[...].astype(MXU))

        @pl.when(kk == nk - 1)
        def _():
            o_ref[...] = acc[...].astype(out_dtype)

    return pl.pallas_call(
        body, name=name, grid=(ka // ta, kb // tb, nk),
        in_specs=[pl.BlockSpec((tk, ta), lambda i, j, kk: (kk, i)),
                  pl.BlockSpec((tk, tb), lambda i, j, kk: (kk, j))],
        out_specs=pl.BlockSpec((ta, tb), lambda i, j, kk: (i, j)),
        out_shape=jax.ShapeDtypeStruct((ka, kb), out_dtype),
        scratch_shapes=[pltpu.VMEM((ta, tb), F32)],
        compiler_params=_cparams("arbitrary", "arbitrary", "arbitrary"),
    )(a, b)


def _gate_vectors(g, gt, hd):
    lane = lax.broadcasted_iota(jnp.int32, g.shape, 1)
    sub = lax.broadcasted_iota(jnp.int32, gt.shape, 0)
    col = lambda j: jnp.sum(jnp.where(lane == j, g, 0.0), axis=1, keepdims=True)
    row = lambda j: jnp.sum(jnp.where(sub == j, gt, 0.0), axis=0, keepdims=True)
    return col(hd), col(hd + NH), row(hd), row(hd + NH)


def _chunk_decay(li_col, li_row, lf_col, lf_row, m_prev):
    n = li_col.shape[0]
    r = lax.broadcasted_iota(jnp.int32, (n, n), 0)
    c = lax.broadcasted_iota(jnp.int32, (n, n), 1)
    tri = r >= c
    b_col = jnp.sum(jnp.where(tri, lf_row, 0.0), axis=1, keepdims=True)
    b_row = jnp.sum(jnp.where(r <= c, lf_col, 0.0), axis=0, keepdims=True)
    b_last = jnp.sum(lf_row, axis=1, keepdims=True)
    dmat = jnp.where(tri, b_col - b_row + li_row, -jnp.inf)
    a_col = b_col + m_prev
    g_col = b_last - b_col + li_col
    m_new = jnp.maximum(b_last + m_prev, jnp.max(g_col, axis=0, keepdims=True))
    w_col = jnp.exp(g_col - m_new)
    decay = jnp.exp(b_last + m_prev - m_new)
    return tri, dmat, a_col, m_new, w_col, decay


def _qkv_specs(row_of):
    q_spec = pl.BlockSpec((LCH, QK), lambda c: (row_of(c), C_Q // QK))
    k_spec = pl.BlockSpec((LCH, QK), lambda c: (row_of(c), C_K // QK))
    v_spec = pl.BlockSpec((LCH, VD), lambda c: (row_of(c), C_V // VD))
    return q_spec, k_spec, v_spec


def _state_specs(row_of):
    return [pl.BlockSpec((NH, None, DK, DV), lambda c: (0, row_of(c), 0, 0)),
            pl.BlockSpec((NH, None, 1, DK), lambda c: (0, row_of(c), 0, 0)),
            pl.BlockSpec((NH, None, 1, GATE_W), lambda c: (0, row_of(c), 0, 0))]


def _lane_put(col, lane_id, width=GATE_W):
    lane = lax.broadcasted_iota(jnp.int32, (col.shape[0], width), 1)
    return jnp.where(lane == lane_id, col, 0.0)


def _lane_get(block, lane_id):
    lane = lax.broadcasted_iota(jnp.int32, block.shape, 1)
    return jnp.sum(jnp.where(lane == lane_id, block, 0.0), axis=1, keepdims=True)


def _mlstm_fwd(proj, gates, bias):
    n = proj.shape[0]
    nc = n // LCH

    def body(q_ref, k_ref, v_ref, g_ref, bias_ref,
             h_ref, st_ref, cs_ref, ns_ref, ms_ref, c_scr, n_scr, m_scr):
        @pl.when(pl.program_id(0) == 0)
        def _():
            c_scr[...] = jnp.zeros_like(c_scr)
            n_scr[...] = jnp.zeros_like(n_scr)
            m_scr[...] = jnp.full_like(m_scr, -jnp.inf)

        g = g_ref[...] + bias_ref[...]
        gt = g.T[0:8, :]
        stats = jnp.zeros((LCH, GATE_W), F32)
        for hd in range(NH):
            li_col, fr_col, li_row, fr_row = _gate_vectors(g, gt, hd)
            m_all = m_scr[hd]
            m_prev = m_all[0:1, 0:1]
            tri, dmat, a_col, m_new, w_col, decay = _chunk_decay(
                li_col, li_row, _logsig(fr_col), _logsig(fr_row), m_prev)
            m_col = jnp.maximum(a_col, jnp.max(dmat, axis=1, keepdims=True))
            dl = jnp.exp(dmat - m_col)
            inter = jnp.exp(a_col - m_col)

            qs = q_ref[:, hd * DK:(hd + 1) * DK] * (DK ** -0.5)
            kk = k_ref[:, hd * DK:(hd + 1) * DK]
            vv = v_ref[:, hd * DV:(hd + 1) * DV]
            cst = c_scr[hd]
            nst = n_scr[hd]
            cs_ref[hd] = cst
            ns_ref[hd] = nst
            ms_ref[hd] = m_all

            sc = _dot_nt(qs, kk) * dl
            num = _dot(sc.astype(MXU), vv) + inter * _dot(qs, cst.astype(MXU))
            den = (jnp.sum(sc, axis=1, keepdims=True)
                   + inter * jnp.sum(qs.astype(F32) * nst, axis=1, keepdims=True))
            nrm = jnp.maximum(jnp.abs(den), jnp.exp(-m_col))
            h_ref[:, hd * DV:(hd + 1) * DV] = (num / nrm).astype(h_ref.dtype)
            stats = stats + _lane_put(den, hd) + _lane_put(m_col, NH + hd)

            kw = kk.astype(F32) * w_col
            c_scr[hd] = decay * cst + _dot_tn(kw.astype(MXU), vv)
            n_scr[hd] = decay * nst + jnp.sum(kw, axis=0, keepdims=True)
            m_scr[hd] = jnp.broadcast_to(m_new, (1, GATE_W))
        st_ref[...] = stats

    q_spec, k_spec, v_spec = _qkv_specs(lambda c: c)
    return pl.pallas_call(
        body, name="mlstm_fwd", grid=(nc,),
        in_specs=[q_spec, k_spec, v_spec,
                  pl.BlockSpec((LCH, GATE_W), lambda c: (c, 0)),
                  pl.BlockSpec((1, GATE_W), lambda c: (0, 0))],
        out_specs=[pl.BlockSpec((LCH, VD), lambda c: (c, 0)),
                   pl.BlockSpec((LCH, GATE_W), lambda c: (c, 0))] + _state_specs(lambda c: c),
        out_shape=[jax.ShapeDtypeStruct((n, VD), MXU),
                   jax.ShapeDtypeStruct((n, GATE_W), F32),
                   jax.ShapeDtypeStruct((NH, nc, DK, DV), F32),
                   jax.ShapeDtypeStruct((NH, nc, 1, DK), F32),
                   jax.ShapeDtypeStruct((NH, nc, 1, GATE_W), F32)],
        scratch_shapes=[pltpu.VMEM((NH, DK, DV), F32), pltpu.VMEM((NH, 1, DK), F32),
                        pltpu.VMEM((NH, 1, GATE_W), F32)],
        compiler_params=_cparams("arbitrary"),
    )(proj, proj, proj, gates, bias)


def _mlstm_bwd(proj, gates, bias, h, dh, stats, cs, ns, ms, dproj):
    n = proj.shape[0]
    nc = n // LCH

    def body(q_ref, k_ref, v_ref, g_ref, bias_ref, h_ref, dh_ref, st_ref,
             cs_ref, ns_ref, ms_ref, dproj_in,
             dqkv_ref, dg_ref, gsum_ref, dc_scr, dn_scr):
        del dproj_in

        @pl.when(pl.program_id(0) == 0)
        def _():
            dc_scr[...] = jnp.zeros_like(dc_scr)
            dn_scr[...] = jnp.zeros_like(dn_scr)
            gsum_ref[...] = jnp.zeros_like(gsum_ref)

        g = g_ref[...] + bias_ref[...]
        gt = g.T[0:8, :]
        stats = st_ref[...]
        r = lax.broadcasted_iota(jnp.int32, (LCH, LCH), 0)
        c = lax.broadcasted_iota(jnp.int32, (LCH, LCH), 1)
        eye = r == c
        to_col = lambda row: jnp.sum(jnp.where(eye, row, 0.0), axis=1, keepdims=True)
        last = lax.broadcasted_iota(jnp.int32, (LCH, 1), 0) == LCH - 1
        dg = jnp.zeros((LCH, GATE_W), F32)
        for hd in range(NH):
            li_col, fr_col, li_row, fr_row = _gate_vectors(g, gt, hd)
            m_prev = ms_ref[hd][0:1, 0:1]
            tri, dmat, a_col, m_new, w_col, decay = _chunk_decay(
                li_col, li_row, _logsig(fr_col), _logsig(fr_row), m_prev)
            m_col = _lane_get(stats, NH + hd)
            dl = jnp.exp(dmat - m_col)
            inter = jnp.exp(a_col - m_col)

            qs = q_ref[:, hd * DK:(hd + 1) * DK] * (DK ** -0.5)
            kk = k_ref[:, hd * DK:(hd + 1) * DK]
            vv = v_ref[:, hd * DV:(hd + 1) * DV]
            qf = qs.astype(F32)
            kf = kk.astype(F32)
            cst = cs_ref[hd]
            nst = ns_ref[hd]
            cb = cst.astype(MXU)
            dcn = dc_scr[hd]
            dnn = dn_scr[hd]
            dcb = dcn.astype(MXU)

            den = _lane_get(stats, hd)
            floor = jnp.exp(-m_col)
            nrm = jnp.maximum(jnp.abs(den), floor)
            dhv = dh_ref[:, hd * DV:(hd + 1) * DV].astype(F32)
            dnum = dhv / nrm
            dnum_b = dnum.astype(MXU)
            dhh = jnp.sum(dhv * h_ref[:, hd * DV:(hd + 1) * DV].astype(F32), axis=1, keepdims=True)
            dden = jnp.where(jnp.abs(den) > floor, -dhh / nrm * jnp.sign(den), 0.0)

            sc = _dot_nt(qs, kk) * dl
            dsc = _dot_nt(dnum_b, vv) + dden
            da = (dl * dsc).astype(MXU)
            gmat = sc * dsc

            dq = _dot(da, kk) + inter * (_dot_nt(dnum_b, cb) + dden * nst)
            dk_state = w_col * (_dot_nt(vv, dcb) + dnn)
            dk = _dot_tn(da, qs) + dk_state
            kw = (kf * w_col).astype(MXU)
            dv = _dot_tn(sc.astype(MXU), dnum_b) + _dot(kw, dcb)
            dqkv_ref[:, hd * DK:(hd + 1) * DK] = (dq * (DK ** -0.5)).astype(dqkv_ref.dtype)
            dqkv_ref[:, QK + hd * DK:QK + (hd + 1) * DK] = dk.astype(dqkv_ref.dtype)
            dqkv_ref[:, 2 * QK + hd * DV:2 * QK + (hd + 1) * DV] = dv.astype(dqkv_ref.dtype)

            num_i = _dot(qs, cb)
            den_i = jnp.sum(qf * nst, axis=1, keepdims=True)
            e_col = inter * (jnp.sum(dnum * num_i, axis=1, keepdims=True) + dden * den_i)
            h_col = jnp.sum(kf * dk_state, axis=1, keepdims=True)
            f_dec = decay * (jnp.sum(jnp.sum(cst * dcn, axis=1, keepdims=True), axis=0, keepdims=True)
                             + jnp.sum(nst * dnn, axis=1, keepdims=True))
            row_g = jnp.sum(gmat, axis=1, keepdims=True)
            col_g = to_col(jnp.sum(gmat, axis=0, keepdims=True))
            db_col = row_g - col_g + e_col - h_col
            db_col = db_col + jnp.where(last, jnp.sum(h_col, axis=0, keepdims=True) + f_dec, 0.0)
            dli_col = col_g + h_col
            dlf_row = jnp.sum(jnp.where(tri, db_col, 0.0), axis=0, keepdims=True)
            df_col = to_col(dlf_row) * _sigmoid(-fr_col)
            dg = dg + _lane_put(dli_col, hd) + _lane_put(df_col, NH + hd)

            dc_scr[hd] = decay * dcn + _dot_tn((qf * inter).astype(MXU), dnum_b)
            dn_scr[hd] = decay * dnn + jnp.sum(qf * (inter * dden), axis=0, keepdims=True)
        dg_ref[...] = dg
        gsum_ref[0:1, 0:GATE_W] += jnp.sum(dg, axis=0, keepdims=True)

    rev = lambda c: nc - 1 - c
    q_spec, k_spec, v_spec = _qkv_specs(rev)
    hv_spec = pl.BlockSpec((LCH, VD), lambda c: (rev(c), 0))
    gate_spec = pl.BlockSpec((LCH, GATE_W), lambda c: (rev(c), 0))
    return pl.pallas_call(
        body, name="mlstm_bwd", grid=(nc,),
        in_specs=[q_spec, k_spec, v_spec, gate_spec,
                  pl.BlockSpec((1, GATE_W), lambda c: (0, 0)),
                  hv_spec, hv_spec, gate_spec] + _state_specs(rev) + [ANY],
        out_specs=[pl.BlockSpec((LCH, QKV_W), lambda c: (rev(c), C_Q // QKV_W)),
                   gate_spec,
                   pl.BlockSpec((8, D), lambda c: (0, 0))],
        out_shape=[jax.ShapeDtypeStruct(dproj.shape, dproj.dtype),
                   jax.ShapeDtypeStruct((n, GATE_W), F32),
                   jax.ShapeDtypeStruct((8, D), F32)],
        scratch_shapes=[pltpu.VMEM((NH, DK, DV), F32), pltpu.VMEM((NH, 1, DK), F32)],
        input_output_aliases={11: 0},
        compiler_params=_cparams("arbitrary"),
    )(proj, proj, proj, gates, bias, h, dh, stats, cs, ns, ms, dproj)


def _proj_spec(tm, col, width):
    return pl.BlockSpec((tm, width), lambda i: (i, col // width))


def _halo_prev(tm, col):
    return pl.BlockSpec((8, 1024), lambda i: (jnp.maximum(i * (tm // 8) - 1, 0), col // 1024))


def _const(shape):
    return pl.BlockSpec(shape, lambda i: (0,) * len(shape))


def _conv_inputs(i, tm, xa_ref, ca_ref, xah_ref, cah_ref):
    u = _f32(xa_ref) * _f32(ca_ref)
    uh = jnp.where(i > 0, _f32(xah_ref) * _f32(cah_ref), 0.0)
    rid = lax.broadcasted_iota(jnp.int32, u.shape, 0)
    u1 = jnp.where(rid == 0, uh[7:8, :], pltpu.roll(u, 1, 0))
    u2 = jnp.where(rid == 0, uh[6:7, :], jnp.where(rid == 1, uh[7:8, :], pltpu.roll(u, 2, 0)))
    return u, u1, u2


def _head_norm(hh, gh):
    out = []
    for j in range(NH):
        hj = hh[:, j * DV:(j + 1) * DV]
        rj = _rstd(hj)
        out.append((hj * rj, rj, gh[:, j * DV:(j + 1) * DV]))
    return out


def _tail_fwd(proj, h, x, p, t, cw, cb, gh, gple, gfin, wa, wb, wo, wpg, wp):
    n = x.shape[0]
    tm = min(256, n)

    def body(ba_r, za_r, o_r, zb_r, ga_r, gb_r, xa_r, ca_r, xah_r, cah_r, h_r, x_r, p_r, t_r,
             cw_r, cb_r, gh_r, gple_r, gfin_r, wa_r, wb_r, wo_r, wpg_r, wp_r,
             apre_o, bpre_o, mg_o, xn1_o, de_o, dgp_o, ya_o, yb_o, x1_o, dx2_o, acc_o):
        i = pl.program_id(0)

        @pl.when(i == 0)
        def _():
            acc_o[...] = jnp.zeros_like(acc_o)

        u, u1, u2 = _conv_inputs(i, tm, xa_r, ca_r, xah_r, cah_r)
        cwv = cw_r[...]
        cv = cwv[0:1, :] * u2 + cwv[1:2, :] * u1 + cwv[2:3, :] * u + cb_r[...]
        za = za_r[...]
        a_pre = ba_r[...] * cv.astype(MXU) * (za * _sigmoid(za))
        apre_o[...] = a_pre
        ya = _dot(a_pre, wa_r[...]).astype(MXU)

        hn = _head_norm(_f32(h_r), gh_r[...])
        hbn = jnp.concatenate([xh * g for xh, _, g in hn], axis=1)
        zb = zb_r[...]
        b_pre = _sigmoid(o_r[...]) * hbn.astype(MXU) * (zb * _sigmoid(zb))
        bpre_o[...] = b_pre
        yb = _dot(b_pre, wb_r[...]).astype(MXU)
        ya_o[...] = ya
        yb_o[...] = yb

        mg = _sigmoid(ga_r[...]) * ya + _sigmoid(gb_r[...]) * yb
        mg_o[...] = mg
        x1 = x_r[...] + _dot(mg, wo_r[...])
        x1_o[...] = x1.astype(MXU)
        xn1 = (x1 * _rstd(x1) * gple_r[...]).astype(MXU)
        xn1_o[...] = xn1
        gt = _sigmoid(_dot(xn1, wpg_r[...]))
        e = _dot(p_r[...].astype(MXU), wp_r[...])
        x2 = x1 + gt * e
        r2 = _rstd(x2)
        xh2 = x2 * r2
        gf = gfin_r[...]
        diff = xh2 * gf - t_r[...]
        dy = diff * (1.0 / D)
        dx2 = _norm_bwd(dy, xh2, r2, gf)
        dx2_o[...] = dx2
        de_o[...] = (dx2 * gt).astype(MXU)
        dgp_o[...] = (dx2 * e * gt * (1.0 - gt)).astype(MXU)
        acc_o[0:1, :] += jnp.sum(dy * xh2, axis=0, keepdims=True)
        loss = 0.5 * jnp.sum(jnp.sum(diff * diff, axis=1, keepdims=True), axis=0, keepdims=True) * (1.0 / D)
        acc_o[1:2, :] += jnp.broadcast_to(loss, (1, D))

    row = lambda w, dt: (pl.BlockSpec((tm, w), lambda i: (i, 0)), jax.ShapeDtypeStruct((n, w), dt))
    outs = [row(D, MXU), row(VD, MXU), row(D, MXU), row(D, MXU), row(D, MXU), row(D, MXU),
            row(D, MXU), row(D, MXU), row(D, MXU), row(D, F32),
            (_const((8, D)), jax.ShapeDtypeStruct((8, D), F32))]
    return pl.pallas_call(
        body, name="tail_fwd", grid=(n // tm,),
        in_specs=[_proj_spec(tm, C_BA, 1024), _proj_spec(tm, C_ZA, 1024),
                  _proj_spec(tm, C_O, 2048), _proj_spec(tm, C_ZB, 2048),
                  _proj_spec(tm, C_GA, 1024), _proj_spec(tm, C_GB, 1024),
                  _proj_spec(tm, C_XA, 1024), _proj_spec(tm, C_CA, 1024),
                  _halo_prev(tm, C_XA), _halo_prev(tm, C_CA),
                  pl.BlockSpec((tm, VD), lambda i: (i, 0)),
                  pl.BlockSpec((tm, D), lambda i: (i, 0)),
                  pl.BlockSpec((tm, PLE), lambda i: (i, 0)),
                  pl.BlockSpec((tm, D), lambda i: (i, 0)),
                  _const((8, D)), _const((1, D)), _const((1, VD)), _const((1, D)), _const((1, D)),
                  _const((D, D)), _const((VD, D)), _const((D, D)), _const((D, D)), _const((PLE, D))],
        out_specs=[s for s, _ in outs],
        out_shape=[s for _, s in outs],
        compiler_params=_cparams("arbitrary"),
    )(*([proj] * 10), h, x, p, t, cw, cb, gh, gple, gfin, wa, wb, wo, wpg, wp)


def _tail_bwd(proj, h, dgp, dx2, x1, ya, yb, cw, cb, gh, gple, wpg, wo, wb, wa):
    n = x1.shape[0]
    tm = min(256, n)

    def body(ba_r, za_r, o_r, zb_r, ga_r, gb_r, xa_r, ca_r, xah_r, cah_r, h_r,
             dgp_r, dx2_r, x1_r, ya_r, yb_r, cw_r, cb_r, gh_r, gple_r,
             wpg_r, wo_r, wb_r, wa_r,
             dproj_o, dcv_o, dh_o, dx1_o, dx1b_o, dya_o, dyb_o, acc_o):
        i = pl.program_id(0)

        @pl.when(i == 0)
        def _():
            acc_o[...] = jnp.zeros_like(acc_o)

        dxn1 = _dot_nt(dgp_r[...], wpg_r[...])
        x1 = _f32(x1_r)
        r1 = _rstd(x1)
        xh1 = x1 * r1
        acc_o[0:1, 0:D] += jnp.sum(dxn1 * xh1, axis=0, keepdims=True)
        dx1 = dx2_r[...] + _norm_bwd(dxn1, xh1, r1, gple_r[...])
        dx1_o[...] = dx1
        dx1b = dx1.astype(MXU)
        dx1b_o[...] = dx1b

        dmg = _dot_nt(dx1b, wo_r[...]).astype(MXU)
        sga, dsga = _gate(ga_r[...])
        sgb, dsgb = _gate(gb_r[...])
        dya = dmg * sga
        dyb = dmg * sgb
        dya_o[...] = dya
        dyb_o[...] = dyb
        dproj_o[:, C_GA:C_GA + D] = dmg * ya_r[...] * dsga
        dproj_o[:, C_GB:C_GB + D] = dmg * yb_r[...] * dsgb

        db_pre = _dot_nt(dyb, wb_r[...]).astype(MXU)
        hn = _head_norm(_f32(h_r), gh_r[...])
        hbn = jnp.concatenate([xh * g for xh, _, g in hn], axis=1).astype(MXU)
        so, dso = _gate(o_r[...])
        zb = zb_r[...]
        szb, dszb = _gate(zb)
        sb = zb * szb
        t1 = db_pre * hbn
        dproj_o[:, C_O:C_O + VD] = t1 * (sb * dso)
        dproj_o[:, C_ZB:C_ZB + VD] = t1 * (so * (szb + zb * dszb))
        dhbn = (db_pre * (so * sb)).astype(F32)
        for j, (xh, rj, g) in enumerate(hn):
            dj = dhbn[:, j * DV:(j + 1) * DV]
            acc_o[1:2, j * DV:(j + 1) * DV] += jnp.sum(dj * xh, axis=0, keepdims=True)
            dh_o[:, j * DV:(j + 1) * DV] = _norm_bwd(dj, xh, rj, g).astype(MXU)

        da_pre = _dot_nt(dya, wa_r[...]).astype(MXU)
        u, u1, u2 = _conv_inputs(i, tm, xa_r, ca_r, xah_r, cah_r)
        cwv = cw_r[...]
        cv = cwv[0:1, :] * u2 + cwv[1:2, :] * u1 + cwv[2:3, :] * u + cb_r[...]
        za = za_r[...]
        sza, dsza = _gate(za)
        sa = za * sza
        ba = ba_r[...]
        t2 = da_pre * cv.astype(MXU)
        dproj_o[:, C_BA:C_BA + D] = t2 * sa
        dproj_o[:, C_ZA:C_ZA + D] = t2 * (ba * (sza + za * dsza))
        dcv_b = da_pre * (ba * sa)
        dcv_o[...] = dcv_b
        dcv = dcv_b.astype(F32)
        acc_o[2:3, 0:D] += jnp.sum(dcv, axis=0, keepdims=True)
        acc_o[3:4, 0:D] += jnp.sum(dcv * u2, axis=0, keepdims=True)
        acc_o[4:5, 0:D] += jnp.sum(dcv * u1, axis=0, keepdims=True)
        acc_o[5:6, 0:D] += jnp.sum(dcv * u, axis=0, keepdims=True)

    row = lambda w, dt: (pl.BlockSpec((tm, w), lambda i: (i, 0)), jax.ShapeDtypeStruct((n, w), dt))
    outs = [(pl.BlockSpec((tm, TAIL_W), lambda i: (i, 0)), jax.ShapeDtypeStruct((n, NMAIN), MXU)),
            row(D, MXU), row(VD, MXU), row(D, F32), row(D, MXU), row(D, MXU), row(D, MXU),
            (_const((8, VD)), jax.ShapeDtypeStruct((8, VD), F32))]
    rowin = lambda w: pl.BlockSpec((tm, w), lambda i: (i, 0))
    return pl.pallas_call(
        body, name="tail_bwd", grid=(n // tm,),
        in_specs=[_proj_spec(tm, C_BA, 1024), _proj_spec(tm, C_ZA, 1024),
                  _proj_spec(tm, C_O, 2048), _proj_spec(tm, C_ZB, 2048),
                  _proj_spec(tm, C_GA, 1024), _proj_spec(tm, C_GB, 1024),
                  _proj_spec(tm, C_XA, 1024), _proj_spec(tm, C_CA, 1024),
                  _halo_prev(tm, C_XA), _halo_prev(tm, C_CA),
                  rowin(VD), rowin(D), rowin(D), rowin(D), rowin(D), rowin(D),
                  _const((8, D)), _const((1, D)), _const((1, VD)), _const((1, D)),
                  _const((D, D)), _const((D, D)), _const((VD, D)), _const((D, D))],
        out_specs=[s for s, _ in outs],
        out_shape=[s for _, s in outs],
        compiler_params=_cparams("arbitrary"),
    )(*([proj] * 10), h, dgp, dx2, x1, ya, yb, cw, cb, gh, gple, wpg, wo, wb, wa)


def _conv_bwd(proj, dcv, cw, dproj):
    n = dcv.shape[0]
    tm = min(512, n)
    nt = n // tm

    def body(xa_r, ca_r, dcv_r, nxt_r, cw_r, dproj_in, dxc_o):
        del dproj_in
        i = pl.program_id(0)
        dcv_v = _f32(dcv_r)
        nxt = jnp.where(i < nt - 1, _f32(nxt_r), 0.0)
        rid = lax.broadcasted_iota(jnp.int32, dcv_v.shape, 0)
        d1 = jnp.where(rid == tm - 1, nxt[0:1, :], pltpu.roll(dcv_v, tm - 1, 0))
        d2 = jnp.where(rid == tm - 2, nxt[0:1, :],
                       jnp.where(rid == tm - 1, nxt[1:2, :], pltpu.roll(dcv_v, tm - 2, 0)))
        cwv = cw_r[...]
        du = cwv[2:3, :] * dcv_v + cwv[1:2, :] * d1 + cwv[0:1, :] * d2
        dxc_o[:, 0:D] = (du * _f32(ca_r)).astype(MXU)
        dxc_o[:, D:2 * D] = (du * _f32(xa_r)).astype(MXU)

    return pl.pallas_call(
        body, name="conv_bwd", grid=(nt,),
        in_specs=[_proj_spec(tm, C_XA, 1024), _proj_spec(tm, C_CA, 1024),
                  pl.BlockSpec((tm, D), lambda i: (i, 0)),
                  pl.BlockSpec((8, D), lambda i: (jnp.minimum((i + 1) * (tm // 8), n // 8 - 1), 0)),
                  _const((8, D)), ANY],
        out_specs=pl.BlockSpec((tm, CONV_W), lambda i: (i, C_XA // CONV_W)),
        out_shape=jax.ShapeDtypeStruct(dproj.shape, dproj.dtype),
        input_output_aliases={5: 0},
        compiler_params=_cparams("arbitrary"),
    )(proj, proj, dcv, dcv, cw, dproj)


def _position():
    return lax.axis_index("x"), lax.axis_index("y"), lax.axis_index("c")


def _flip(v, bit):
    return 1 - v if bit else v


def _part_start(core, stride, tiled):
    return pl.multiple_of(core * stride, stride) if tiled else core * stride


def _scatter_copies(srcs, dsts, strides, send_sems, recv_sems, local_sems):
    x, y, c = _position()
    me = 4 * x + 2 * y + c
    na = len(srcs)
    copies = []
    for r in range(N_DEV):
        px, py, pc = _flip(x, (r >> 2) & 1), _flip(y, (r >> 1) & 1), _flip(c, r & 1)
        for a in range(na):
            rows = dsts[a].shape[1]
            src = srcs[a].at[2 * px + py, pl.ds(_part_start(pc, strides[a], len(dsts[a].shape) == 3), rows)]
            dst = dsts[a].at[me]
            if r == 0:
                copies.append(pltpu.make_async_copy(src, dst, local_sems.at[a]))
            else:
                k = (r - 1) * na + a
                copies.append(pltpu.make_async_remote_copy(
                    src_ref=src, dst_ref=dst, send_sem=send_sems.at[k], recv_sem=recv_sems.at[k],
                    device_id=(px, py, pc), device_id_type=MESH))
    return copies


def _input_grad(dproj, dgates, w_t, wg_t, x, dx1, g_mix, g_in, g_rest):
    n = x.shape[0]
    tm, tk = min(1024, n), 2048
    nk = NMAIN // tk
    nt = n // tm

    def body(dp_r, dg_r, w_r, wg_r, x_r, dx1_r, g_r, gin, grest,
             gx_o, acc_o, oin, orest, acc, send_sems, recv_sems, local_sems):
        i = pl.program_id(0)
        kk = pl.program_id(1)
        copies = _scatter_copies((gin, grest), (oin, orest), strides, send_sems, recv_sems, local_sems)

        @pl.when((i == 0) & (kk == 0))
        def _():
            acc_o[...] = jnp.zeros_like(acc_o)
            for cp in copies:
                cp.start()

        @pl.when(kk == 0)
        def _():
            acc[...] = _dot(dg_r[...].astype(MXU), wg_r[...])

        acc[...] += _dot(dp_r[...], w_r[...])

        @pl.when(kk == nk - 1)
        def _():
            dhn = acc[...]
            xv = x_r[...]
            r0 = _rstd(xv)
            xh = xv * r0
            acc_o[0:1, :] += jnp.sum(dhn * xh, axis=0, keepdims=True)
            gx_o[...] = dx1_r[...] + _norm_bwd(dhn, xh, r0, g_r[...])

        @pl.when((i == nt - 1) & (kk == nk - 1))
        def _():
            for cp in copies:
                cp.wait()

    nrem = 2 * (N_DEV - 1)
    r_in, r_rest = HALF_SLABS, g_rest.shape[1] // 2
    strides = (HALF_START, r_rest)
    return pl.pallas_call(
        body, name="input_grad", grid=(nt, nk),
        in_specs=[pl.BlockSpec((tm, tk), lambda i, kk: (i, kk)),
                  pl.BlockSpec((tm, GATE_W), lambda i, kk: (i, 0)),
                  pl.BlockSpec((tk, D), lambda i, kk: (kk, 0)),
                  pl.BlockSpec((GATE_W, D), lambda i, kk: (0, 0)),
                  pl.BlockSpec((tm, D), lambda i, kk: (i, 0)),
                  pl.BlockSpec((tm, D), lambda i, kk: (i, 0)),
                  pl.BlockSpec((1, D), lambda i, kk: (0, 0)),
                  ANY, ANY],
        out_specs=[pl.BlockSpec((tm, D), lambda i, kk: (i, 0)),
                   pl.BlockSpec((8, D), lambda i, kk: (0, 0)),
                   ANY, ANY],
        out_shape=[jax.ShapeDtypeStruct((n, D), F32), jax.ShapeDtypeStruct((8, D), F32),
                   jax.ShapeDtypeStruct((N_DEV, r_in) + g_in.shape[2:], g_in.dtype),
                   jax.ShapeDtypeStruct((N_DEV, r_rest) + g_rest.shape[2:], g_rest.dtype)],
        scratch_shapes=[pltpu.VMEM((tm, D), F32),
                        pltpu.SemaphoreType.DMA((nrem,)), pltpu.SemaphoreType.DMA((nrem,)),
                        pltpu.SemaphoreType.DMA((2,))],
        compiler_params=_cparams("arbitrary", "arbitrary"),
    )(dproj, dgates, w_t, wg_t, x, dx1, g_mix, g_in, g_rest)


def _pack_small(acc_f, acc_b, acc_x, gsum):
    def body(f_r, b_r, x_r, s_r, o_r):
        o_r[...] = jnp.zeros_like(o_r)
        o_r[0:1, :] = x_r[0:1, :]
        o_r[1:2, :] = b_r[2:3, 0:D]
        o_r[2:3, :] = b_r[1:2, 0:D]
        o_r[3:4, :] = b_r[1:2, D:2 * D]
        o_r[4:5, :] = b_r[0:1, 0:D]
        o_r[5:6, :] = f_r[0:1, :]
        o_r[6:9, :] = b_r[3:6, 0:D]
        lane = lax.broadcasted_iota(jnp.int32, (1, D), 1)
        o_r[9:10, :] = jnp.where(lane < 2 * NH, s_r[0:1, :], jnp.where(lane == 2 * NH, f_r[1:2, :], 0.0))

    return pl.pallas_call(
        body, name="pack_small",
        out_shape=jax.ShapeDtypeStruct((SMALL_ROWS, D), F32),
    )(acc_f, acc_b, acc_x, gsum)


def _sum_slots(r, name, tr=64, half=None):
    s, rows, w = r.shape
    tr = min(tr, rows)
    assert rows % tr == 0
    nt = rows // tr

    def body(*refs):
        r_ref, o_ref = refs[-2:]
        tot = r_ref[0].astype(F32)
        for k in range(1, s):
            tot = tot + r_ref[k].astype(F32)
        o_ref[...] = tot

    if half is None:
        return pl.pallas_call(
            body, name=name, grid=(nt,),
            in_specs=[pl.BlockSpec((s, tr, w), lambda i: (0, i, 0))],
            out_specs=pl.BlockSpec((tr, w), lambda i: (i, 0)),
            out_shape=jax.ShapeDtypeStruct((rows, w), F32),
            compiler_params=_cparams("arbitrary"),
        )(r)
    return pl.pallas_call(
        body, name=name,
        grid_spec=pltpu.PrefetchScalarGridSpec(
            num_scalar_prefetch=1, grid=(nt,),
            in_specs=[pl.BlockSpec((s, tr, w), lambda i, hf: (0, i, 0))],
            out_specs=pl.BlockSpec((tr, w), lambda i, hf: (hf[0] * nt + i, 0))),
        out_shape=jax.ShapeDtypeStruct((2 * rows, w), F32),
        compiler_params=_cparams("arbitrary"),
    )(half, r)


def _cast_slabs(a, tr=163):
    rows = a.shape[0]
    assert rows % tr == 0

    def body(a_ref, o_ref):
        o_ref[...] = a_ref[...].astype(MXU)

    spec = pl.BlockSpec((tr,) + a.shape[1:], lambda i: (i, 0, 0))
    return pl.pallas_call(
        body, name="cast_w_in", grid=(rows // tr,), in_specs=[spec], out_specs=spec,
        out_shape=jax.ShapeDtypeStruct(a.shape, MXU), compiler_params=_cparams("arbitrary"),
    )(a)


def _sum_slabs(r, name, tr=69):
    s, rows = r.shape[:2]
    assert rows % tr == 0

    def body(r_ref, o_ref):
        tot = r_ref[0].astype(F32)
        for k in range(1, s):
            tot = tot + r_ref[k].astype(F32)
        o_ref[...] = tot

    return pl.pallas_call(
        body, name=name, grid=(rows // tr,),
        in_specs=[pl.BlockSpec((s, tr) + r.shape[2:], lambda i: (0, i, 0, 0))],
        out_specs=pl.BlockSpec((tr,) + r.shape[2:], lambda i: (i, 0, 0)),
        out_shape=jax.ShapeDtypeStruct(r.shape[1:], F32),
        compiler_params=_cparams("arbitrary"),
    )(r)


def _adamw_body(w_r, g_r, m_r, v_r, d_o, m_o, v_o):
    c1 = 1.0 - ADAM_B1 ** ADAM_STEP
    c2 = 1.0 - ADAM_B2 ** ADAM_STEP
    gv = g_r[...]
    mn = ADAM_B1 * m_r[...] + (1.0 - ADAM_B1) * gv
    vn = ADAM_B2 * v_r[...] + (1.0 - ADAM_B2) * (gv * gv)
    m_o[...] = mn
    v_o[...] = vn
    d_o[...] = -ADAM_LR * ((mn / c1) / (jnp.sqrt(vn / c2) + ADAM_EPS) + ADAM_WD * w_r[...])


def _adamw_slabs(w, g, m, v, name, tr):
    rows = w.shape[0]
    assert rows % tr == 0 and w.shape == g.shape

    def body(*refs):
        _adamw_body(*refs)

    spec = pl.BlockSpec((tr,) + w.shape[1:], lambda i: (i, 0, 0))
    shp = jax.ShapeDtypeStruct(w.shape, F32)
    return pl.pallas_call(
        body, name=name, grid=(rows // tr,),
        in_specs=[spec] * 4, out_specs=[spec] * 3, out_shape=[shp] * 3,
        compiler_params=_cparams("arbitrary"),
    )(w, g, m, v)


def _adamw(w, g, m, v, name, g_row0=0, tr=64):
    lead = w.ndim == 3
    rows, cols = w.shape[-2:]
    tr = min(tr, rows)
    assert rows % tr == 0 and g_row0 % tr == 0 and g.shape[1] == cols

    def body(*refs):
        _adamw_body(*refs)

    if lead:
        spec = pl.BlockSpec((None, tr, cols), lambda i: (0, i, 0))
    else:
        spec = pl.BlockSpec((tr, cols), lambda i: (i, 0))
    g_spec = pl.BlockSpec((tr, cols), lambda i: (g_row0 // tr + i, 0))
    shp = jax.ShapeDtypeStruct(w.shape, F32)
    return pl.pallas_call(
        body, name=name, grid=(rows // tr,),
        in_specs=[spec, g_spec, spec, spec], out_specs=[spec] * 3, out_shape=[shp] * 3,
        compiler_params=_cparams("arbitrary"),
    )(w, g, m, v)


def _allgather_chips(halved, whole):
    arrs = [a for a, _ in halved] + list(whole)
    parts = [p for _, p in halved]
    nh, na = len(halved), len(arrs)
    relations = ((1, 0), (0, 1), (1, 1))
    n_ici = 3 * na

    def body(*refs):
        ins, outs = refs[:na], refs[na:2 * na]
        send_sems, recv_sems = refs[2 * na:]
        x, y, c = _position()
        mine = 2 * x + y

        def rows(a, core, stride, size):
            return pl.ds(_part_start(core, stride, len(ins[a].shape) == 2), size)

        def remote(src, dst, k, to):
            return pltpu.make_async_remote_copy(src_ref=src, dst_ref=dst, send_sem=send_sems.at[k],
                                                recv_sem=recv_sems.at[k], device_id=to, device_id_type=MESH)

        first, passed = [], []
        for j, (rx, ry) in enumerate(relations):
            px, py = _flip(x, rx), _flip(y, ry)
            for a in range(na):
                if a < nh:
                    s1, n1, s2, n2 = parts[a]
                    first.append(remote(ins[a].at[rows(a, c, s1, n1)], outs[a].at[mine, rows(a, c, s1, n1)],
                                        j * na + a, (px, py, c)))
                    landed = outs[a].at[2 * px + py, rows(a, c, s2, n2)]
                    passed.append(remote(landed, landed, n_ici + j * nh + a, (x, y, 1 - c)))
                else:
                    first.append(remote(ins[a], outs[a].at[mine], j * na + a, (px, py, c)))
        for cp in first:
            cp.start()
        k = 0
        for j in range(3):
            for a in range(na):
                first[j * na + a].wait_recv()
                if a < nh:
                    passed[k].start()
                    k += 1
        for cp in passed:
            cp.wait_recv()
        for cp in first + passed:
            cp.wait_send()

    nsem = n_ici + 3 * nh
    outs = pl.pallas_call(
        body, name="allgather_weights",
        in_specs=[ANY] * na, out_specs=[ANY] * na,
        out_shape=[jax.ShapeDtypeStruct((N_CHIPS,) + a.shape, a.dtype) for a in arrs],
        scratch_shapes=[pltpu.SemaphoreType.DMA((nsem,)), pltpu.SemaphoreType.DMA((nsem,))],
    )(*arrs)
    chip = 2 * lax.axis_index("x") + lax.axis_index("y")
    return [lax.dynamic_update_slice(o, a[None], (chip,) + (0,) * a.ndim) for o, a in zip(outs, arrs)]


def _join_halves(piece, full_rest, small):
    def body(pc_ref, rest_in, sm, oin, orest, osm, send_sems, recv_sems, local_sem):
        del rest_in
        x, y, c = _position()
        me = 4 * x + 2 * y + c
        sib = (x, y, 1 - c)
        rows = orest.shape[0] // 2
        mine = orest.at[pl.ds(pl.multiple_of(c * rows, rows), rows)]
        copies = [
            pltpu.make_async_remote_copy(
                src_ref=pc_ref.at[pl.ds(c, HALF_START)], dst_ref=oin.at[pl.ds(c * HALF_SLABS, HALF_START)],
                send_sem=send_sems.at[0], recv_sem=recv_sems.at[0], device_id=sib, device_id_type=MESH),
            pltpu.make_async_remote_copy(
                src_ref=mine, dst_ref=mine, send_sem=send_sems.at[1], recv_sem=recv_sems.at[1],
                device_id=sib, device_id_type=MESH)]
        na = 2
        copies.append(pltpu.make_async_copy(sm, osm.at[me], local_sem))
        for r in range(1, N_DEV):
            to = (_flip(x, (r >> 2) & 1), _flip(y, (r >> 1) & 1), _flip(c, r & 1))
            copies.append(pltpu.make_async_remote_copy(
                src_ref=sm, dst_ref=osm.at[me], send_sem=send_sems.at[na + r - 1],
                recv_sem=recv_sems.at[na + r - 1], device_id=to, device_id_type=MESH))
        for cp in copies:
            cp.start()
        for cp in copies:
            cp.wait()

    nsem = 2 + N_DEV - 1
    return pl.pallas_call(
        body, name="join_halves",
        in_specs=[ANY] * 3, out_specs=[ANY] * 3,
        out_shape=[jax.ShapeDtypeStruct((SLABS_CHIP,) + piece.shape[1:], piece.dtype),
                   jax.ShapeDtypeStruct(full_rest.shape, full_rest.dtype),
                   jax.ShapeDtypeStruct((N_DEV,) + small.shape, small.dtype)],
        scratch_shapes=[pltpu.SemaphoreType.DMA((nsem,)), pltpu.SemaphoreType.DMA((nsem,)),
                        pltpu.SemaphoreType.DMA],
        input_output_aliases={1: 1},
    )(piece, full_rest, small)


def _to_internal(w_slabs):
    order = sorted(SEGMENTS, key=lambda s: s[2])
    main = jnp.concatenate([w_slabs[g0 // 2:(g0 + w) // 2] for g0, w, _ in order], axis=0)
    gate = w_slabs[GATE_COL // 2:GATE_COL // 2 + NH].reshape(2 * NH, D)
    return main.reshape(NMAIN, D), jnp.pad(gate, ((0, GATE_W - 2 * NH), (0, 0)))


def _to_global(main_t, gate_t):
    main = main_t.reshape(NMAIN // 2, 16, D // 8)
    parts = sorted([(g0, main[i0 // 2:(i0 + w) // 2]) for g0, w, i0 in SEGMENTS]
                   + [(GATE_COL, gate_t[0:2 * NH].reshape(NH, 16, D // 8))], key=lambda s: s[0])
    return jnp.concatenate([p for _, p in parts], axis=0)


def _pack_rows(wa, wb, wo, wpg, wp):
    return jnp.concatenate([wa, wb, wo, wpg, wp.reshape(PACK_ROWS[4], D)], axis=0)


def _pad_rows(a, rows=8):
    return jnp.pad(a, ((0, rows - a.shape[0]), (0, 0)))


def kernel(x, p, g_mix, w_in, conv_w, conv_b, w_a_out, b_gates, g_head, w_b_out, w_o, g_ple, w_ple_gate, w_ple, g_final, loss_target, m_g_mix, m_w_in, m_conv_w, m_conv_b, m_w_a_out, m_b_gates, m_g_head, m_w_b_out, m_w_o, m_g_ple, m_w_ple_gate, m_w_ple, m_g_final, v_g_mix, v_w_in, v_conv_w, v_conv_b, v_w_a_out, v_b_gates, v_g_head, v_w_b_out, v_w_o, v_g_ple, v_w_ple_gate, v_w_ple, v_g_final):
    chip = 2 * lax.axis_index("x") + lax.axis_index("y")
    xs, ps, ts = x[0], p[0, 0], loss_target[0]
    g_fin = g_final.reshape(1, D)

    pack_w = _pack_rows(w_a_out[0], w_b_out[0], w_o[0], w_ple_gate[0], w_ple[0])
    w_slabs = _cast_slabs(jnp.transpose(w_in, (2, 0, 1)).reshape(SLABS_CHIP, 16, D // 8))
    half_rows = PACK_TOTAL // 2
    g_win, g_pack, g_cw = _allgather_chips(
        [(w_slabs, (HALF_START, HALF_SLABS, HALF_SLABS, HALF_START)),
         (pack_w.astype(MXU), (half_rows, half_rows, half_rows, half_rows))], [_pad_rows(conv_w[0])])
    w_t, wg_t = _to_internal(g_win.reshape(N_IN // 2, 16, D // 8))
    offs = [0, 256, 768, 1024, 1280, 1344]
    wa, wb, wo, wpg = [g_pack[:, offs[k]:offs[k + 1]].reshape(-1, D) for k in range(4)]
    wp = jnp.transpose(g_pack[:, offs[4]:offs[5]].reshape(N_CHIPS, PLE, PLE), (1, 0, 2)).reshape(PLE, D)
    cw = jnp.transpose(g_cw, (1, 0, 2)).reshape(8, D)

    bias = jnp.pad(b_gates, ((0, 0), (0, GATE_W - 2 * NH)))
    proj, hn, gates = _proj(xs, g_mix, w_t, wg_t)
    h, stats, cs, ns, ms = _mlstm_fwd(proj, gates, bias)
    (a_pre, b_pre, mg, xn1, de, dgp, ya, yb, x1, dx2, acc_f) = _tail_fwd(
        proj, h, xs, ps, ts, cw, conv_b, g_head, g_ple, g_fin, wa, wb, wo, wpg, wp)
    dproj, dcv, dh, dx1, dx1b, dya, dyb, acc_b = _tail_bwd(
        proj, h, dgp, dx2, x1, ya, yb, cw, conv_b, g_head, g_ple, wpg, wo, wb, wa)
    dproj = _conv_bwd(proj, dcv, cw, dproj)
    dproj, dgates, gsum = _mlstm_bwd(proj, gates, bias, h, dh, stats, cs, ns, ms, dproj)
    d_main = _matmul_tn(dproj, hn, "dw_in", out_dtype=WIRE, tk=4096)
    d_gate = _matmul_tn(dgates, hn, "dw_gate", out_dtype=WIRE)
    d_wa = _matmul_tn(a_pre, dya, "dw_a_out", out_dtype=WIRE)
    d_wb = _matmul_tn(b_pre, dyb, "dw_b_out", out_dtype=WIRE)
    d_wo = _matmul_tn(mg, dx1b, "dw_o", out_dtype=WIRE)
    d_wpg = _matmul_tn(xn1, dgp, "dw_ple_gate", out_dtype=WIRE)
    d_wp = _matmul_tn(ps, de, "dw_ple", out_dtype=WIRE)

    g_in = _to_global(d_main, d_gate).reshape(N_CHIPS, SLABS_CHIP, 16, D // 8)
    d_wp_c = jnp.transpose(d_wp.reshape(PLE, N_CHIPS, PLE), (1, 0, 2)).reshape(N_CHIPS, PACK_ROWS[4], D)
    g_rest = jnp.concatenate(
        [d_wa.reshape(N_CHIPS, -1, D), d_wb.reshape(N_CHIPS, -1, D), d_wo.reshape(N_CHIPS, -1, D),
         d_wpg.reshape(N_CHIPS, -1, D), d_wp_c], axis=1)
    grad_x, acc_x, r_in, r_rest = _input_grad(dproj, dgates, w_t, wg_t, xs, dx1, g_mix, g_in, g_rest)
    small = _pack_small(acc_f, acc_b, acc_x, gsum)
    core = lax.axis_index("c").astype(jnp.int32)
    piece = _sum_slabs(r_in, "sum_w_in")
    gw_in, gw_rest, r_small = _join_halves(
        piece, _sum_slots(r_rest, "sum_rest", tr=96, half=core.reshape(1)), small)
    gw_in = lax.dynamic_update_slice(gw_in, piece, (core * HALF_START, 0, 0)).reshape(COLS_CHIP, 8, D // 8)
    gs = _sum_slots(r_small, "sum_small", tr=SMALL_ROWS)

    big = []
    row0 = 0
    for name, w, m, v in (("w_a_out", w_a_out, m_w_a_out, v_w_a_out), ("w_b_out", w_b_out, m_w_b_out, v_w_b_out),
                          ("w_o", w_o, m_w_o, v_w_o), ("w_ple_gate", w_ple_gate, m_w_ple_gate, v_w_ple_gate)):
        big.append((name, w, m, v, gw_rest, row0))
        row0 += w.shape[1]
    g_wp = gw_rest[row0:row0 + PACK_ROWS[4]].reshape(PLE, PLE)
    big.append(("w_ple", w_ple, m_w_ple, v_w_ple, g_wp, 0))
    upd = {name: _adamw(w, g, m, v, "adamw_" + name, g_row0=r0) for name, w, m, v, g, r0 in big}
    g_big = {name: (g if name == "w_ple" else g[r0:r0 + w.shape[1]])[None] for name, w, m, v, g, r0 in big}
    slabs = lambda a: jnp.transpose(a, (2, 0, 1)).reshape(COLS_CHIP, 8, D // 8)
    unslab = lambda a: jnp.transpose(a, (1, 2, 0)).reshape(1, D, COLS_CHIP)
    upd["w_in"] = [unslab(u) for u in _adamw_slabs(
        slabs(w_in), gw_in, slabs(m_w_in), slabs(v_w_in), "adamw_w_in", tr=326)]
    g_big["w_in"] = unslab(gw_in)

    lane = lax.broadcasted_iota(jnp.int32, (1, D), 1)
    g_small = jnp.concatenate([gs[0:6], jnp.where(lane < 2 * NH, gs[9:10], 0.0), jnp.zeros((1, D), F32)], axis=0)

    def small_pack(gm, cb_, bg, gh, gp, gf):
        return jnp.concatenate([gm, cb_, gh.reshape(2, D), gp, gf.reshape(1, D),
                                jnp.pad(bg, ((0, 0), (0, D - 2 * NH))), jnp.zeros((1, D), F32)], axis=0)

    ws = small_pack(g_mix, conv_b, b_gates, g_head, g_ple, g_final)
    ms_ = small_pack(m_g_mix, m_conv_b, m_b_gates, m_g_head, m_g_ple, m_g_final)
    vs = small_pack(v_g_mix, v_conv_b, v_b_gates, v_g_head, v_g_ple, v_g_final)
    upd_s = _adamw(ws, g_small, ms_, vs, "adamw_small")
    g_cw_mine = _pad_rows(lax.dynamic_slice(gs[6:9], (0, chip * PLE), (3, PLE)))
    upd_c = _adamw(_pad_rows(conv_w[0]), g_cw_mine, _pad_rows(m_conv_w[0]), _pad_rows(v_conv_w[0]),
                   "adamw_conv_w")

    def leaves(bigs, sm, cwv):
        return [sm[0:1], bigs["w_in"], cwv[0:3][None], sm[1:2], bigs["w_a_out"], sm[6:7, 0:2 * NH],
                sm[2:4].reshape(1, VD), bigs["w_b_out"], bigs["w_o"], sm[4:5], bigs["w_ple_gate"],
                bigs["w_ple"], sm[5]]

    loss = gs[9, 2 * NH]
    outs = [loss, grad_x[None]] + leaves(g_big, g_small, g_cw_mine)
    for k in range(3):
        outs += leaves({name: u[k] for name, u in upd.items()}, upd_s[k], upd_c[k])
    return tuple(outs)
```

```python
import jax
import jax.numpy as jnp
from jax import lax
from jax.experimental import pallas as pl
from jax.experimental.pallas import tpu as pltpu

F32 = jnp.float32
MXU = jnp.bfloat16
WIRE = jnp.bfloat16

D = 1024
NH, DK, DV = 4, 256, 512
VD = NH * DV
PLE = 256
LCH = 256
EPS = 1e-6
N_IN = 14344
NMAIN = 14336
GATE_W = 128
N_CHIPS, N_DEV = 4, 8

C_BA, C_ZA, C_O, C_ZB, C_GA, C_GB = 0, 1024, 2048, 4096, 6144, 7168
QK = NH * DK
C_Q, C_K, C_V, C_XA, C_CA = 8192, 9216, 10240, 12288, 13312
QKV_W = 2 * QK + VD
TAIL_W = 8192
CONV_W = 2048
SEGMENTS = (
    (0, 1024, C_XA), (1024, 1024, C_BA), (2048, 1024, C_CA), (3072, 1024, C_ZA),
    (4096, QKV_W, C_Q), (8192, 4096, C_O), (12296, 2048, C_GA),
)
GATE_COL = 12288

COLS_CHIP = N_IN // N_CHIPS
SLABS_CHIP = COLS_CHIP // 2
HALF_START = SLABS_CHIP // 2
HALF_SLABS = SLABS_CHIP - HALF_START

PACK_ROWS = (256, 512, 256, 256, 64)
PACK_TOTAL = sum(PACK_ROWS)
SMALL_ROWS = 16

ADAM_LR, ADAM_B1, ADAM_B2, ADAM_EPS, ADAM_WD, ADAM_STEP = 0.001, 0.9, 0.999, 1e-08, 0.01, 10

VMEM_LIMIT = 56 * 1024 * 1024
MESH = pl.DeviceIdType.MESH
ANY = pl.BlockSpec(memory_space=pl.ANY)


def _cparams(*sem):
    return pltpu.CompilerParams(dimension_semantics=sem, vmem_limit_bytes=VMEM_LIMIT)


def _dot(a, b):
    return jnp.dot(a, b, preferred_element_type=F32)


def _dot_nt(a, b):
    return lax.dot_general(a, b, (((1,), (1,)), ((), ())), preferred_element_type=F32)


def _dot_tn(a, b):
    return lax.dot_general(a, b, (((0,), (0,)), ((), ())), preferred_element_type=F32)


def _sigmoid(x):
    return 1.0 / (1.0 + jnp.exp(-x))


def _logsig(x):
    return jnp.minimum(x, 0.0) - jnp.log(1.0 + jnp.exp(-jnp.abs(x)))


GATE_FLOOR = -80.0


def _gate(v):
    e = jnp.exp(-jnp.maximum(v, GATE_FLOOR))
    s = 1.0 / (1.0 + e)
    return s, e * s * s


def _rstd(x):
    return lax.rsqrt(jnp.mean(x * x, axis=-1, keepdims=True) + EPS)


def _norm_bwd(dy, xhat, r, g):
    dxh = dy * g
    return r * (dxh - xhat * jnp.mean(dxh * xhat, axis=-1, keepdims=True))


def _f32(ref):
    return ref[...].astype(F32)


def _proj(x, g_mix, w_t, wg_t, tm=1024, tn=2048):
    n = x.shape[0]
    m = w_t.shape[0]
    tm = min(tm, n)

    def body(x_ref, g_ref, b_ref, wg_ref, o_ref, hn_ref, gate_ref):
        @pl.when(pl.program_id(1) == 0)
        def _():
            xv = x_ref[...]
            hn = (xv * _rstd(xv) * g_ref[...]).astype(MXU)
            hn_ref[...] = hn
            gate_ref[...] = _dot_nt(hn, wg_ref[...])

        o_ref[...] = _dot_nt(hn_ref[...], b_ref[...]).astype(MXU)

    return pl.pallas_call(
        body, name="proj", grid=(n // tm, m // tn),
        in_specs=[pl.BlockSpec((tm, D), lambda i, j: (i, 0)),
                  pl.BlockSpec((1, D), lambda i, j: (0, 0)),
                  pl.BlockSpec((tn, D), lambda i, j: (j, 0)),
                  pl.BlockSpec((GATE_W, D), lambda i, j: (0, 0))],
        out_specs=[pl.BlockSpec((tm, tn), lambda i, j: (i, j)),
                   pl.BlockSpec((tm, D), lambda i, j: (i, 0)),
                   pl.BlockSpec((tm, GATE_W), lambda i, j: (i, 0))],
        out_shape=[jax.ShapeDtypeStruct((n, m), MXU), jax.ShapeDtypeStruct((n, D), MXU),
                   jax.ShapeDtypeStruct((n, GATE_W), F32)],
        compiler_params=_cparams("arbitrary", "arbitrary"),
    )(x, g_mix, w_t, wg_t)


def _matmul_tn(a, b, name, out_dtype=F32, ta=1024, tb=1024, tk=2048):
    n, ka = a.shape
    kb = b.shape[1]
    ta, tb, tk = min(ta, ka), min(tb, kb), min(tk, n)
    nk = n // tk

    def body(a_ref, b_ref, o_ref, acc):
        kk = pl.program_id(2)

        @pl.when(kk == 0)
        def _():
            acc[...] = jnp.zeros_like(acc)

        acc[...] += _dot_tn(a_ref[...].astype(MXU), b_ref[...].astype(MXU))

        @pl.when(kk == nk - 1)
        def _():
            o_ref[...] = acc[...].astype(out_dtype)

    return pl.pallas_call(
        body, name=name, grid=(ka // ta, kb // tb, nk),
        in_specs=[pl.BlockSpec((tk, ta), lambda i, j, kk: (kk, i)),
                  pl.BlockSpec((tk, tb), lambda i, j, kk: (kk, j))],
        out_specs=pl.BlockSpec((ta, tb), lambda i, j, kk: (i, j)),
        out_shape=jax.ShapeDtypeStruct((ka, kb), out_dtype),
        scratch_shapes=[pltpu.VMEM((ta, tb), F32)],
        compiler_params=_cparams("arbitrary", "arbitrary", "arbitrary"),
    )(a, b)


def _gate_vectors(g, gt, hd):
    lane = lax.broadcasted_iota(jnp.int32, g.shape, 1)
    sub = lax.broadcasted_iota(jnp.int32, gt.shape, 0)
    col = lambda j: jnp.sum(jnp.where(lane == j, g, 0.0), axis=1, keepdims=True)
    row = lambda j: jnp.sum(jnp.where(sub == j, gt, 0.0), axis=0, keepdims=True)
    return col(hd), col(hd + NH), row(hd), row(hd + NH)


def _chunk_decay(li_col, li_row, lf_col, lf_row, m_prev):
    n = li_col.shape[0]
    r = lax.broadcasted_iota(jnp.int32, (n, n), 0)
    c = lax.broadcasted_iota(jnp.int32, (n, n), 1)
    tri = r >= c
    b_col = jnp.sum(jnp.where(tri, lf_row, 0.0), axis=1, keepdims=True)
    b_row = jnp.sum(jnp.where(r <= c, lf_col, 0.0), axis=0, keepdims=True)
    b_last = jnp.sum(lf_row, axis=1, keepdims=True)
    dmat = jnp.where(tri, b_col - b_row + li_row, -jnp.inf)
    a_col = b_col + m_prev
    g_col = b_last - b_col + li_col
    m_new = jnp.maximum(b_last + m_prev, jnp.max(g_col, axis=0, keepdims=True))
    w_col = jnp.exp(g_col - m_new)
    decay = jnp.exp(b_last + m_prev - m_new)
    return tri, dmat, a_col, m_new, w_col, decay


def _qkv_specs(row_of):
    q_spec = pl.BlockSpec((LCH, QK), lambda c: (row_of(c), C_Q // QK))
    k_spec = pl.BlockSpec((LCH, QK), lambda c: (row_of(c), C_K // QK))
    v_spec = pl.BlockSpec((LCH, VD), lambda c: (row_of(c), C_V // VD))
    return q_spec, k_spec, v_spec


def _state_specs(row_of):
    return [pl.BlockSpec((NH, None, DK, DV), lambda c: (0, row_of(c), 0, 0)),
            pl.BlockSpec((NH, None, 1, DK), lambda c: (0, row_of(c), 0, 0)),
            pl.BlockSpec((NH, None, 1, GATE_W), lambda c: (0, row_of(c), 0, 0))]


def _lane_put(col, lane_id, width=GATE_W):
    lane = lax.broadcasted_iota(jnp.int32, (col.shape[0], width), 1)
    return jnp.where(lane == lane_id, col, 0.0)


def _lane_get(block, lane_id):
    lane = lax.broadcasted_iota(jnp.int32, block.shape, 1)
    return jnp.sum(jnp.where(lane == lane_id, block, 0.0), axis=1, keepdims=True)


def _mlstm_fwd(proj, gates, bias):
    n = proj.shape[0]
    nc = n // LCH

    def body(q_ref, k_ref, v_ref, g_ref, bias_ref,
             h_ref, st_ref, cs_ref, ns_ref, ms_ref, c_scr, n_scr, m_scr):
        @pl.when(pl.program_id(0) == 0)
        def _():
            c_scr[...] = jnp.zeros_like(c_scr)
            n_scr[...] = jnp.zeros_like(n_scr)
            m_scr[...] = jnp.full_like(m_scr, -jnp.inf)

        g = g_ref[...] + bias_ref[...]
        gt = g.T[0:8, :]
        stats = jnp.zeros((LCH, GATE_W), F32)
        for hd in range(NH):
            li_col, fr_col, li_row, fr_row = _gate_vectors(g, gt, hd)
            m_all = m_scr[hd]
            m_prev = m_all[0:1, 0:1]
            tri, dmat, a_col, m_new, w_col, decay = _chunk_decay(
                li_col, li_row, _logsig(fr_col), _logsig(fr_row), m_prev)
            m_col = jnp.maximum(a_col, jnp.max(dmat, axis=1, keepdims=True))
            dl = jnp.exp(dmat - m_col)
            inter = jnp.exp(a_col - m_col)

            qs = q_ref[:, hd * DK:(hd + 1) * DK] * (DK ** -0.5)
            kk = k_ref[:, hd * DK:(hd + 1) * DK]
            vv = v_ref[:, hd * DV:(hd + 1) * DV]
            cst = c_scr[hd]
            nst = n_scr[hd]
            cs_ref[hd] = cst
            ns_ref[hd] = nst
            ms_ref[hd] = m_all

            sc = _dot_nt(qs, kk) * dl
            num = _dot(sc.astype(MXU), vv) + inter * _dot(qs, cst.astype(MXU))
            den = (jnp.sum(sc, axis=1, keepdims=True)
                   + inter * jnp.sum(qs.astype(F32) * nst, axis=1, keepdims=True))
            nrm = jnp.maximum(jnp.abs(den), jnp.exp(-m_col))
            h_ref[:, hd * DV:(hd + 1) * DV] = (num / nrm).astype(h_ref.dtype)
            stats = stats + _lane_put(den, hd) + _lane_put(m_col, NH + hd)

            kw = kk.astype(F32) * w_col
            c_scr[hd] = decay * cst + _dot_tn(kw.astype(MXU), vv)
            n_scr[hd] = decay * nst + jnp.sum(kw, axis=0, keepdims=True)
            m_scr[hd] = jnp.broadcast_to(m_new, (1, GATE_W))
        st_ref[...] = stats

    q_spec, k_spec, v_spec = _qkv_specs(lambda c: c)
    return pl.pallas_call(
        body, name="mlstm_fwd", grid=(nc,),
        in_specs=[q_spec, k_spec, v_spec,
                  pl.BlockSpec((LCH, GATE_W), lambda c: (c, 0)),
                  pl.BlockSpec((1, GATE_W), lambda c: (0, 0))],
        out_specs=[pl.BlockSpec((LCH, VD), lambda c: (c, 0)),
                   pl.BlockSpec((LCH, GATE_W), lambda c: (c, 0))] + _state_specs(lambda c: c),
        out_shape=[jax.ShapeDtypeStruct((n, VD), MXU),
                   jax.ShapeDtypeStruct((n, GATE_W), F32),
                   jax.ShapeDtypeStruct((NH, nc, DK, DV), F32),
                   jax.ShapeDtypeStruct((NH, nc, 1, DK), F32),
                   jax.ShapeDtypeStruct((NH, nc, 1, GATE_W), F32)],
        scratch_shapes=[pltpu.VMEM((NH, DK, DV), F32), pltpu.VMEM((NH, 1, DK), F32),
                        pltpu.VMEM((NH, 1, GATE_W), F32)],
        compiler_params=_cparams("arbitrary"),
    )(proj, proj, proj, gates, bias)


def _mlstm_bwd(proj, gates, bias, h, dh, stats, cs, ns, ms, dproj):
    n = proj.shape[0]
    nc = n // LCH

    def body(q_ref, k_ref, v_ref, g_ref, bias_ref, h_ref, dh_ref, st_ref,
             cs_ref, ns_ref, ms_ref, dproj_in,
             dqkv_ref, dg_ref, gsum_ref, dc_scr, dn_scr):
        del dproj_in

        @pl.when(pl.program_id(0) == 0)
        def _():
            dc_scr[...] = jnp.zeros_like(dc_scr)
            dn_scr[...] = jnp.zeros_like(dn_scr)
            gsum_ref[...] = jnp.zeros_like(gsum_ref)

        g = g_ref[...] + bias_ref[...]
        gt = g.T[0:8, :]
        stats = st_ref[...]
        r = lax.broadcasted_iota(jnp.int32, (LCH, LCH), 0)
        c = lax.broadcasted_iota(jnp.int32, (LCH, LCH), 1)
        eye = r == c
        to_col = lambda row: jnp.sum(jnp.where(eye, row, 0.0), axis=1, keepdims=True)
        last = lax.broadcasted_iota(jnp.int32, (LCH, 1), 0) == LCH - 1
        dg = jnp.zeros((LCH, GATE_W), F32)
        for hd in range(NH):
            li_col, fr_col, li_row, fr_row = _gate_vectors(g, gt, hd)
            m_prev = ms_ref[hd][0:1, 0:1]
            tri, dmat, a_col, m_new, w_col, decay = _chunk_decay(
                li_col, li_row, _logsig(fr_col), _logsig(fr_row), m_prev)
            m_col = _lane_get(stats, NH + hd)
            dl = jnp.exp(dmat - m_col)
            inter = jnp.exp(a_col - m_col)

            qs = q_ref[:, hd * DK:(hd + 1) * DK] * (DK ** -0.5)
            kk = k_ref[:, hd * DK:(hd + 1) * DK]
            vv = v_ref[:, hd * DV:(hd + 1) * DV]
            qf = qs.astype(F32)
            kf = kk.astype(F32)
            cst = cs_ref[hd]
            nst = ns_ref[hd]
            cb = cst.astype(MXU)
            dcn = dc_scr[hd]
            dnn = dn_scr[hd]
            dcb = dcn.astype(MXU)

            den = _lane_get(stats, hd)
            floor = jnp.exp(-m_col)
            nrm = jnp.maximum(jnp.abs(den), floor)
            dhv = dh_ref[:, hd * DV:(hd + 1) * DV].astype(F32)
            dnum = dhv / nrm
            dnum_b = dnum.astype(MXU)
            dhh = jnp.sum(dhv * h_ref[:, hd * DV:(hd + 1) * DV].astype(F32), axis=1, keepdims=True)
            dden = jnp.where(jnp.abs(den) > floor, -dhh / nrm * jnp.sign(den), 0.0)

            sc = _dot_nt(qs, kk) * dl
            dsc = _dot_nt(dnum_b, vv) + dden
            da = (dl * dsc).astype(MXU)
            gmat = sc * dsc

            dq = _dot(da, kk) + inter * (_dot_nt(dnum_b, cb) + dden * nst)
            dk_state = w_col * (_dot_nt(vv, dcb) + dnn)
            dk = _dot_tn(da, qs) + dk_state
            kw = (kf * w_col).astype(MXU)
            dv = _dot_tn(sc.astype(MXU), dnum_b) + _dot(kw, dcb)
            dqkv_ref[:, hd * DK:(hd + 1) * DK] = (dq * (DK ** -0.5)).astype(dqkv_ref.dtype)
            dqkv_ref[:, QK + hd * DK:QK + (hd + 1) * DK] = dk.astype(dqkv_ref.dtype)
            dqkv_ref[:, 2 * QK + hd * DV:2 * QK + (hd + 1) * DV] = dv.astype(dqkv_ref.dtype)

            num_i = _dot(qs, cb)
            den_i = jnp.sum(qf * nst, axis=1, keepdims=True)
            e_col = inter * (jnp.sum(dnum * num_i, axis=1, keepdims=True) + dden * den_i)
            h_col = jnp.sum(kf * dk_state, axis=1, keepdims=True)
            f_dec = decay * (jnp.sum(jnp.sum(cst * dcn, axis=1, keepdims=True), axis=0, keepdims=True)
                             + jnp.sum(nst * dnn, axis=1, keepdims=True))
            row_g = jnp.sum(gmat, axis=1, keepdims=True)
            col_g = to_col(jnp.sum(gmat, axis=0, keepdims=True))
            db_col = row_g - col_g + e_col - h_col
            db_col = db_col + jnp.where(last, jnp.sum(h_col, axis=0, keepdims=True) + f_dec, 0.0)
            dli_col = col_g + h_col
            dlf_row = jnp.sum(jnp.where(tri, db_col, 0.0), axis=0, keepdims=True)
            df_col = to_col(dlf_row) * _sigmoid(-fr_col)
            dg = dg + _lane_put(dli_col, hd) + _lane_put(df_col, NH + hd)

            dc_scr[hd] = decay * dcn + _dot_tn((qf * inter).astype(MXU), dnum_b)
            dn_scr[hd] = decay * dnn + jnp.sum(qf * (inter * dden), axis=0, keepdims=True)
        dg_ref[...] = dg
        gsum_ref[0:1, 0:GATE_W] += jnp.sum(dg, axis=0, keepdims=True)

    rev = lambda c: nc - 1 - c
    q_spec, k_spec, v_spec = _qkv_specs(rev)
    hv_spec = pl.BlockSpec((LCH, VD), lambda c: (rev(c), 0))
    gate_spec = pl.BlockSpec((LCH, GATE_W), lambda c: (rev(c), 0))
    return pl.pallas_call(
        body, name="mlstm_bwd", grid=(nc,),
        in_specs=[q_spec, k_spec, v_spec, gate_spec,
                  pl.BlockSpec((1, GATE_W), lambda c: (0, 0)),
                  hv_spec, hv_spec, gate_spec] + _state_specs(rev) + [ANY],
        out_specs=[pl.BlockSpec((LCH, QKV_W), lambda c: (rev(c), C_Q // QKV_W)),
                   gate_spec,
                   pl.BlockSpec((8, D), lambda c: (0, 0))],
        out_shape=[jax.ShapeDtypeStruct(dproj.shape, dproj.dtype),
                   jax.ShapeDtypeStruct((n, GATE_W), F32),
                   jax.ShapeDtypeStruct((8, D), F32)],
        scratch_shapes=[pltpu.VMEM((NH, DK, DV), F32), pltpu.VMEM((NH, 1, DK), F32)],
        input_output_aliases={11: 0},
        compiler_params=_cparams("arbitrary"),
    )(proj, proj, proj, gates, bias, h, dh, stats, cs, ns, ms, dproj)


def _proj_spec(tm, col, width):
    return pl.BlockSpec((tm, width), lambda i: (i, col // width))


def _tail_in_specs(tm):
    return [_proj_spec(tm, 0, TAIL_W), _proj_spec(tm, C_XA, CONV_W),
            pl.BlockSpec((8, CONV_W), lambda i: (jnp.maximum(i * (tm // 8) - 1, 0), C_XA // CONV_W))]


def _tail_views(tail_r, conv_r, halo_r):
    cols = lambda ref, c0, w: ref.at[:, pl.ds(c0, w)]
    return (cols(tail_r, C_BA, D), cols(tail_r, C_ZA, D), cols(tail_r, C_O, VD), cols(tail_r, C_ZB, VD),
            cols(tail_r, C_GA, D), cols(tail_r, C_GB, D), cols(conv_r, 0, D), cols(conv_r, D, D),
            cols(halo_r, 0, D), cols(halo_r, D, D))


def _const(shape):
    return pl.BlockSpec(shape, lambda i: (0,) * len(shape))


def _conv_inputs(i, tm, xa_ref, ca_ref, xah_ref, cah_ref):
    u = _f32(xa_ref) * _f32(ca_ref)
    uh = jnp.where(i > 0, _f32(xah_ref) * _f32(cah_ref), 0.0)
    rid = lax.broadcasted_iota(jnp.int32, u.shape, 0)
    u1 = jnp.where(rid == 0, uh[7:8, :], pltpu.roll(u, 1, 0))
    u2 = jnp.where(rid == 0, uh[6:7, :], jnp.where(rid == 1, uh[7:8, :], pltpu.roll(u, 2, 0)))
    return u, u1, u2


def _head_norm(hh, gh):
    out = []
    for j in range(NH):
        hj = hh[:, j * DV:(j + 1) * DV]
        rj = _rstd(hj)
        out.append((hj * rj, rj, gh[:, j * DV:(j + 1) * DV]))
    return out


def _tail_fwd(proj, h, x, p, t, cw, cb, gh, gple, gfin, wa, wb, wo, wpg, wp):
    n = x.shape[0]
    tm = min(256, n)

    def body(tail_r, conv_r, halo_r, h_r, x_r, p_r, t_r,
             cw_r, cb_r, gh_r, gple_r, gfin_r, wa_r, wb_r, wo_r, wpg_r, wp_r,
             apre_o, bpre_o, mg_o, xn1_o, de_o, dgp_o, ya_o, yb_o, x1_o, dx2_o, acc_o):
        ba_r, za_r, o_r, zb_r, ga_r, gb_r, xa_r, ca_r, xah_r, cah_r = _tail_views(tail_r, conv_r, halo_r)
        i = pl.program_id(0)

        @pl.when(i == 0)
        def _():
            acc_o[...] = jnp.zeros_like(acc_o)

        u, u1, u2 = _conv_inputs(i, tm, xa_r, ca_r, xah_r, cah_r)
        cwv = cw_r[...]
        cv = cwv[0:1, :] * u2 + cwv[1:2, :] * u1 + cwv[2:3, :] * u + cb_r[...]
        za = za_r[...]
        a_pre = ba_r[...] * cv.astype(MXU) * (za * _sigmoid(za))
        apre_o[...] = a_pre
        ya = _dot(a_pre, wa_r[...]).astype(MXU)

        hn = _head_norm(_f32(h_r), gh_r[...])
        hbn = jnp.concatenate([xh * g for xh, _, g in hn], axis=1)
        zb = zb_r[...]
        b_pre = _sigmoid(o_r[...]) * hbn.astype(MXU) * (zb * _sigmoid(zb))
        bpre_o[...] = b_pre
        yb = _dot(b_pre, wb_r[...]).astype(MXU)
        ya_o[...] = ya
        yb_o[...] = yb

        mg = _sigmoid(ga_r[...]) * ya + _sigmoid(gb_r[...]) * yb
        mg_o[...] = mg
        x1 = x_r[...] + _dot(mg, wo_r[...])
        x1_o[...] = x1.astype(MXU)
        xn1 = (x1 * _rstd(x1) * gple_r[...]).astype(MXU)
        xn1_o[...] = xn1
        gt = _sigmoid(_dot(xn1, wpg_r[...]))
        e = _dot(p_r[...].astype(MXU), wp_r[...])
        x2 = x1 + gt * e
        r2 = _rstd(x2)
        xh2 = x2 * r2
        gf = gfin_r[...]
        diff = xh2 * gf - t_r[...]
        dy = diff * (1.0 / D)
        dx2 = _norm_bwd(dy, xh2, r2, gf)
        dx2_o[...] = dx2
        de_o[...] = (dx2 * gt).astype(MXU)
        dgp_o[...] = (dx2 * e * gt * (1.0 - gt)).astype(MXU)
        acc_o[0:1, :] += jnp.sum(dy * xh2, axis=0, keepdims=True)
        loss = 0.5 * jnp.sum(jnp.sum(diff * diff, axis=1, keepdims=True), axis=0, keepdims=True) * (1.0 / D)
        acc_o[1:2, :] += jnp.broadcast_to(loss, (1, D))

    row = lambda w, dt: (pl.BlockSpec((tm, w), lambda i: (i, 0)), jax.ShapeDtypeStruct((n, w), dt))
    outs = [row(D, MXU), row(VD, MXU), row(D, MXU), row(D, MXU), row(D, MXU), row(D, MXU),
            row(D, MXU), row(D, MXU), row(D, MXU), row(D, F32),
            (_const((8, D)), jax.ShapeDtypeStruct((8, D), F32))]
    return pl.pallas_call(
        body, name="tail_fwd", grid=(n // tm,),
        in_specs=_tail_in_specs(tm) + [
                  pl.BlockSpec((tm, VD), lambda i: (i, 0)),
                  pl.BlockSpec((tm, D), lambda i: (i, 0)),
                  pl.BlockSpec((tm, PLE), lambda i: (i, 0)),
                  pl.BlockSpec((tm, D), lambda i: (i, 0)),
                  _const((8, D)), _const((1, D)), _const((1, VD)), _const((1, D)), _const((1, D)),
                  _const((D, D)), _const((VD, D)), _const((D, D)), _const((D, D)), _const((PLE, D))],
        out_specs=[s for s, _ in outs],
        out_shape=[s for _, s in outs],
        compiler_params=_cparams("arbitrary"),
    )(proj, proj, proj, h, x, p, t, cw, cb, gh, gple, gfin, wa, wb, wo, wpg, wp)


def _tail_bwd(proj, h, dgp, dx2, x1, ya, yb, cw, cb, gh, gple, wpg, wo, wb, wa):
    n = x1.shape[0]
    tm = min(256, n)

    def body(tail_r, conv_r, halo_r, h_r,
             dgp_r, dx2_r, x1_r, ya_r, yb_r, cw_r, cb_r, gh_r, gple_r,
             wpg_r, wo_r, wb_r, wa_r,
             dproj_o, dcv_o, dh_o, dx1_o, dx1b_o, dya_o, dyb_o, acc_o):
        ba_r, za_r, o_r, zb_r, ga_r, gb_r, xa_r, ca_r, xah_r, cah_r = _tail_views(tail_r, conv_r, halo_r)
        i = pl.program_id(0)

        @pl.when(i == 0)
        def _():
            acc_o[...] = jnp.zeros_like(acc_o)

        dxn1 = _dot_nt(dgp_r[...], wpg_r[...])
        x1 = _f32(x1_r)
        r1 = _rstd(x1)
        xh1 = x1 * r1
        acc_o[0:1, 0:D] += jnp.sum(dxn1 * xh1, axis=0, keepdims=True)
        dx1 = dx2_r[...] + _norm_bwd(dxn1, xh1, r1, gple_r[...])
        dx1_o[...] = dx1
        dx1b = dx1.astype(MXU)
        dx1b_o[...] = dx1b

        dmg = _dot_nt(dx1b, wo_r[...]).astype(MXU)
        sga, dsga = _gate(ga_r[...])
        sgb, dsgb = _gate(gb_r[...])
        dya = dmg * sga
        dyb = dmg * sgb
        dya_o[...] = dya
        dyb_o[...] = dyb
        dproj_o[:, C_GA:C_GA + D] = dmg * ya_r[...] * dsga
        dproj_o[:, C_GB:C_GB + D] = dmg * yb_r[...] * dsgb

        db_pre = _dot_nt(dyb, wb_r[...]).astype(MXU)
        hn = _head_norm(_f32(h_r), gh_r[...])
        hbn = jnp.concatenate([xh * g for xh, _, g in hn], axis=1).astype(MXU)
        so, dso = _gate(o_r[...])
        zb = zb_r[...]
        szb, dszb = _gate(zb)
        sb = zb * szb
        t1 = db_pre * hbn
        dproj_o[:, C_O:C_O + VD] = t1 * (sb * dso)
        dproj_o[:, C_ZB:C_ZB + VD] = t1 * (so * (szb + zb * dszb))
        dhbn = (db_pre * (so * sb)).astype(F32)
        for j, (xh, rj, g) in enumerate(hn):
            dj = dhbn[:, j * DV:(j + 1) * DV]
            acc_o[1:2, j * DV:(j + 1) * DV] += jnp.sum(dj * xh, axis=0, keepdims=True)
            dh_o[:, j * DV:(j + 1) * DV] = _norm_bwd(dj, xh, rj, g).astype(MXU)

        da_pre = _dot_nt(dya, wa_r[...]).astype(MXU)
        u, u1, u2 = _conv_inputs(i, tm, xa_r, ca_r, xah_r, cah_r)
        cwv = cw_r[...]
        cv = cwv[0:1, :] * u2 + cwv[1:2, :] * u1 + cwv[2:3, :] * u + cb_r[...]
        za = za_r[...]
        sza, dsza = _gate(za)
        sa = za * sza
        ba = ba_r[...]
        t2 = da_pre * cv.astype(MXU)
        dproj_o[:, C_BA:C_BA + D] = t2 * sa
        dproj_o[:, C_ZA:C_ZA + D] = t2 * (ba * (sza + za * dsza))
        dcv_b = da_pre * (ba * sa)
        dcv_o[...] = dcv_b
        dcv = dcv_b.astype(F32)
        acc_o[2:3, 0:D] += jnp.sum(dcv, axis=0, keepdims=True)
        acc_o[3:4, 0:D] += jnp.sum(dcv * u2, axis=0, keepdims=True)
        acc_o[4:5, 0:D] += jnp.sum(dcv * u1, axis=0, keepdims=True)
        acc_o[5:6, 0:D] += jnp.sum(dcv * u, axis=0, keepdims=True)

    row = lambda w, dt: (pl.BlockSpec((tm, w), lambda i: (i, 0)), jax.ShapeDtypeStruct((n, w), dt))
    outs = [(pl.BlockSpec((tm, TAIL_W), lambda i: (i, 0)), jax.ShapeDtypeStruct((n, NMAIN), MXU)),
            row(D, MXU), row(VD, MXU), row(D, F32), row(D, MXU), row(D, MXU), row(D, MXU),
            (_const((8, VD)), jax.ShapeDtypeStruct((8, VD), F32))]
    rowin = lambda w: pl.BlockSpec((tm, w), lambda i: (i, 0))
    return pl.pallas_call(
        body, name="tail_bwd", grid=(n // tm,),
        in_specs=_tail_in_specs(tm) + [
                  rowin(VD), rowin(D), rowin(D), rowin(D), rowin(D), rowin(D),
                  _const((8, D)), _const((1, D)), _const((1, VD)), _const((1, D)),
                  _const((D, D)), _const((D, D)), _const((VD, D)), _const((D, D))],
        out_specs=[s for s, _ in outs],
        out_shape=[s for _, s in outs],
        compiler_params=_cparams("arbitrary"),
    )(proj, proj, proj, h, dgp, dx2, x1, ya, yb, cw, cb, gh, gple, wpg, wo, wb, wa)


def _conv_bwd(proj, dcv, cw, dproj):
    n = dcv.shape[0]
    tm = min(512, n)
    nt = n // tm

    def body(xa_r, ca_r, dcv_r, nxt_r, cw_r, dproj_in, dxc_o):
        del dproj_in
        i = pl.program_id(0)
        dcv_v = _f32(dcv_r)
        nxt = jnp.where(i < nt - 1, _f32(nxt_r), 0.0)
        rid = lax.broadcasted_iota(jnp.int32, dcv_v.shape, 0)
        d1 = jnp.where(rid == tm - 1, nxt[0:1, :], pltpu.roll(dcv_v, tm - 1, 0))
        d2 = jnp.where(rid == tm - 2, nxt[0:1, :],
                       jnp.where(rid == tm - 1, nxt[1:2, :], pltpu.roll(dcv_v, tm - 2, 0)))
        cwv = cw_r[...]
        du = cwv[2:3, :] * dcv_v + cwv[1:2, :] * d1 + cwv[0:1, :] * d2
        dxc_o[:, 0:D] = (du * _f32(ca_r)).astype(MXU)
        dxc_o[:, D:2 * D] = (du * _f32(xa_r)).astype(MXU)

    return pl.pallas_call(
        body, name="conv_bwd", grid=(nt,),
        in_specs=[_proj_spec(tm, C_XA, 1024), _proj_spec(tm, C_CA, 1024),
                  pl.BlockSpec((tm, D), lambda i: (i, 0)),
                  pl.BlockSpec((8, D), lambda i: (jnp.minimum((i + 1) * (tm // 8), n // 8 - 1), 0)),
                  _const((8, D)), ANY],
        out_specs=pl.BlockSpec((tm, CONV_W), lambda i: (i, C_XA // CONV_W)),
        out_shape=jax.ShapeDtypeStruct(dproj.shape, dproj.dtype),
        input_output_aliases={5: 0},
        compiler_params=_cparams("arbitrary"),
    )(proj, proj, dcv, dcv, cw, dproj)


def _position():
    return lax.axis_index("x"), lax.axis_index("y"), lax.axis_index("c")


def _flip(v, bit):
    return 1 - v if bit else v


def _part_start(core, stride, tiled):
    return pl.multiple_of(core * stride, stride) if tiled else core * stride


def _scatter_copies(srcs, dsts, strides, send_sems, recv_sems, local_sems):
    x, y, c = _position()
    me = 4 * x + 2 * y + c
    na = len(srcs)
    copies = []
    for r in range(N_DEV):
        px, py, pc = _flip(x, (r >> 2) & 1), _flip(y, (r >> 1) & 1), _flip(c, r & 1)
        for a in range(na):
            rows = dsts[a].shape[1]
            src = srcs[a].at[2 * px + py, pl.ds(_part_start(pc, strides[a], len(dsts[a].shape) == 3), rows)]
            dst = dsts[a].at[me]
            if r == 0:
                copies.append(pltpu.make_async_copy(src, dst, local_sems.at[a]))
            else:
                k = (r - 1) * na + a
                copies.append(pltpu.make_async_remote_copy(
                    src_ref=src, dst_ref=dst, send_sem=send_sems.at[k], recv_sem=recv_sems.at[k],
                    device_id=(px, py, pc), device_id_type=MESH))
    return copies


def _input_grad(dproj, dgates, w_t, wg_t, x, dx1, g_mix, g_in, g_rest):
    n = x.shape[0]
    tm, tk = min(1024, n), 2048
    nk = NMAIN // tk
    nt = n // tm

    def body(dp_r, dg_r, w_r, wg_r, x_r, dx1_r, g_r, gin, grest,
             gx_o, acc_o, oin, orest, acc, send_sems, recv_sems, local_sems):
        i = pl.program_id(0)
        kk = pl.program_id(1)
        copies = _scatter_copies((gin, grest), (oin, orest), strides, send_sems, recv_sems, local_sems)

        @pl.when((i == 0) & (kk == 0))
        def _():
            acc_o[...] = jnp.zeros_like(acc_o)
            for cp in copies:
                cp.start()

        @pl.when(kk == 0)
        def _():
            acc[...] = _dot(dg_r[...].astype(MXU), wg_r[...])

        acc[...] += _dot(dp_r[...], w_r[...])

        @pl.when(kk == nk - 1)
        def _():
            dhn = acc[...]
            xv = x_r[...]
            r0 = _rstd(xv)
            xh = xv * r0
            acc_o[0:1, :] += jnp.sum(dhn * xh, axis=0, keepdims=True)
            gx_o[...] = dx1_r[...] + _norm_bwd(dhn, xh, r0, g_r[...])

        @pl.when((i == nt - 1) & (kk == nk - 1))
        def _():
            for cp in copies:
                cp.wait()

    nrem = 2 * (N_DEV - 1)
    r_in, r_rest = HALF_SLABS, g_rest.shape[1] // 2
    strides = (HALF_START, r_rest)
    return pl.pallas_call(
        body, name="input_grad", grid=(nt, nk),
        in_specs=[pl.BlockSpec((tm, tk), lambda i, kk: (i, kk)),
                  pl.BlockSpec((tm, GATE_W), lambda i, kk: (i, 0)),
                  pl.BlockSpec((tk, D), lambda i, kk: (kk, 0)),
                  pl.BlockSpec((GATE_W, D), lambda i, kk: (0, 0)),
                  pl.BlockSpec((tm, D), lambda i, kk: (i, 0)),
                  pl.BlockSpec((tm, D), lambda i, kk: (i, 0)),
                  pl.BlockSpec((1, D), lambda i, kk: (0, 0)),
                  ANY, ANY],
        out_specs=[pl.BlockSpec((tm, D), lambda i, kk: (i, 0)),
                   pl.BlockSpec((8, D), lambda i, kk: (0, 0)),
                   ANY, ANY],
        out_shape=[jax.ShapeDtypeStruct((n, D), F32), jax.ShapeDtypeStruct((8, D), F32),
                   jax.ShapeDtypeStruct((N_DEV, r_in) + g_in.shape[2:], g_in.dtype),
                   jax.ShapeDtypeStruct((N_DEV, r_rest) + g_rest.shape[2:], g_rest.dtype)],
        scratch_shapes=[pltpu.VMEM((tm, D), F32),
                        pltpu.SemaphoreType.DMA((nrem,)), pltpu.SemaphoreType.DMA((nrem,)),
                        pltpu.SemaphoreType.DMA((2,))],
        compiler_params=_cparams("arbitrary", "arbitrary"),
    )(dproj, dgates, w_t, wg_t, x, dx1, g_mix, g_in, g_rest)


def _pack_small(acc_f, acc_b, acc_x, gsum):
    def body(f_r, b_r, x_r, s_r, o_r):
        o_r[...] = jnp.zeros_like(o_r)
        o_r[0:1, :] = x_r[0:1, :]
        o_r[1:2, :] = b_r[2:3, 0:D]
        o_r[2:3, :] = b_r[1:2, 0:D]
        o_r[3:4, :] = b_r[1:2, D:2 * D]
        o_r[4:5, :] = b_r[0:1, 0:D]
        o_r[5:6, :] = f_r[0:1, :]
        o_r[6:9, :] = b_r[3:6, 0:D]
        lane = lax.broadcasted_iota(jnp.int32, (1, D), 1)
        o_r[9:10, :] = jnp.where(lane < 2 * NH, s_r[0:1, :], jnp.where(lane == 2 * NH, f_r[1:2, :], 0.0))

    return pl.pallas_call(
        body, name="pack_small",
        out_shape=jax.ShapeDtypeStruct((SMALL_ROWS, D), F32),
    )(acc_f, acc_b, acc_x, gsum)


def _sum_slots(r, name, tr=64, half=None):
    s, rows, w = r.shape
    tr = min(tr, rows)
    assert rows % tr == 0
    nt = rows // tr

    def body(*refs):
        r_ref, o_ref = refs[-2:]
        tot = r_ref[0].astype(F32)
        for k in range(1, s):
            tot = tot + r_ref[k].astype(F32)
        o_ref[...] = tot

    if half is None:
        return pl.pallas_call(
            body, name=name, grid=(nt,),
            in_specs=[pl.BlockSpec((s, tr, w), lambda i: (0, i, 0))],
            out_specs=pl.BlockSpec((tr, w), lambda i: (i, 0)),
            out_shape=jax.ShapeDtypeStruct((rows, w), F32),
            compiler_params=_cparams("arbitrary"),
        )(r)
    return pl.pallas_call(
        body, name=name,
        grid_spec=pltpu.PrefetchScalarGridSpec(
            num_scalar_prefetch=1, grid=(nt,),
            in_specs=[pl.BlockSpec((s, tr, w), lambda i, hf: (0, i, 0))],
            out_specs=pl.BlockSpec((tr, w), lambda i, hf: (hf[0] * nt + i, 0))),
        out_shape=jax.ShapeDtypeStruct((2 * rows, w), F32),
        compiler_params=_cparams("arbitrary"),
    )(half, r)


def _cast_slabs(a, tr=163):
    rows = a.shape[0]
    assert rows % tr == 0

    def body(a_ref, o_ref):
        o_ref[...] = a_ref[...].astype(MXU)

    spec = pl.BlockSpec((tr,) + a.shape[1:], lambda i: (i, 0, 0))
    return pl.pallas_call(
        body, name="cast_w_in", grid=(rows // tr,), in_specs=[spec], out_specs=spec,
        out_shape=jax.ShapeDtypeStruct(a.shape, MXU), compiler_params=_cparams("arbitrary"),
    )(a)


def _sum_slabs(r, name, tr=69):
    s, rows = r.shape[:2]
    assert rows % tr == 0

    def body(r_ref, o_ref):
        tot = r_ref[0].astype(F32)
        for k in range(1, s):
            tot = tot + r_ref[k].astype(F32)
        o_ref[...] = tot

    return pl.pallas_call(
        body, name=name, grid=(rows // tr,),
        in_specs=[pl.BlockSpec((s, tr) + r.shape[2:], lambda i: (0, i, 0, 0))],
        out_specs=pl.BlockSpec((tr,) + r.shape[2:], lambda i: (i, 0, 0)),
        out_shape=jax.ShapeDtypeStruct(r.shape[1:], F32),
        compiler_params=_cparams("arbitrary"),
    )(r)


def _adamw_body(w_r, g_r, m_r, v_r, d_o, m_o, v_o):
    c1 = 1.0 - ADAM_B1 ** ADAM_STEP
    c2 = 1.0 - ADAM_B2 ** ADAM_STEP
    gv = g_r[...]
    mn = ADAM_B1 * m_r[...] + (1.0 - ADAM_B1) * gv
    vn = ADAM_B2 * v_r[...] + (1.0 - ADAM_B2) * (gv * gv)
    m_o[...] = mn
    v_o[...] = vn
    d_o[...] = -ADAM_LR * ((mn / c1) / (jnp.sqrt(vn / c2) + ADAM_EPS) + ADAM_WD * w_r[...])


def _adamw_slabs(w, g, m, v, name, tr):
    rows = w.shape[0]
    assert rows % tr == 0 and w.shape == g.shape

    def body(*refs):
        _adamw_body(*refs)

    spec = pl.BlockSpec((tr,) + w.shape[1:], lambda i: (i, 0, 0))
    shp = jax.ShapeDtypeStruct(w.shape, F32)
    return pl.pallas_call(
        body, name=name, grid=(rows // tr,),
        in_specs=[spec] * 4, out_specs=[spec] * 3, out_shape=[shp] * 3,
        compiler_params=_cparams("arbitrary"),
    )(w, g, m, v)


def _adamw(w, g, m, v, name, g_row0=0, tr=64):
    lead = w.ndim == 3
    rows, cols = w.shape[-2:]
    tr = min(tr, rows)
    assert rows % tr == 0 and g_row0 % tr == 0 and g.shape[1] == cols

    def body(*refs):
        _adamw_body(*refs)

    if lead:
        spec = pl.BlockSpec((None, tr, cols), lambda i: (0, i, 0))
    else:
        spec = pl.BlockSpec((tr, cols), lambda i: (i, 0))
    g_spec = pl.BlockSpec((tr, cols), lambda i: (g_row0 // tr + i, 0))
    shp = jax.ShapeDtypeStruct(w.shape, F32)
    return pl.pallas_call(
        body, name=name, grid=(rows // tr,),
        in_specs=[spec, g_spec, spec, spec], out_specs=[spec] * 3, out_shape=[shp] * 3,
        compiler_params=_cparams("arbitrary"),
    )(w, g, m, v)


def _allgather_chips(halved, whole):
    arrs = [a for a, _ in halved] + list(whole)
    parts = [p for _, p in halved]
    nh, na = len(halved), len(arrs)
    relations = ((1, 0), (0, 1), (1, 1))
    n_ici = 3 * na

    def body(*refs):
        ins, outs = refs[:na], refs[na:2 * na]
        send_sems, recv_sems = refs[2 * na:]
        x, y, c = _position()
        mine = 2 * x + y

        def rows(a, core, stride, size):
            return pl.ds(_part_start(core, stride, len(ins[a].shape) == 2), size)

        def remote(src, dst, k, to):
            return pltpu.make_async_remote_copy(src_ref=src, dst_ref=dst, send_sem=send_sems.at[k],
                                                recv_sem=recv_sems.at[k], device_id=to, device_id_type=MESH)

        first, passed = [], []
        for j, (rx, ry) in enumerate(relations):
            px, py = _flip(x, rx), _flip(y, ry)
            for a in range(na):
                if a < nh:
                    s1, n1, s2, n2 = parts[a]
                    first.append(remote(ins[a].at[rows(a, c, s1, n1)], outs[a].at[mine, rows(a, c, s1, n1)],
                                        j * na + a, (px, py, c)))
                    landed = outs[a].at[2 * px + py, rows(a, c, s2, n2)]
                    passed.append(remote(landed, landed, n_ici + j * nh + a, (x, y, 1 - c)))
                else:
                    first.append(remote(ins[a], outs[a].at[mine], j * na + a, (px, py, c)))
        for cp in first:
            cp.start()
        k = 0
        for j in range(3):
            for a in range(na):
                first[j * na + a].wait_recv()
                if a < nh:
                    passed[k].start()
                    k += 1
        for cp in passed:
            cp.wait_recv()
        for cp in first + passed:
            cp.wait_send()

    nsem = n_ici + 3 * nh
    outs = pl.pallas_call(
        body, name="allgather_weights",
        in_specs=[ANY] * na, out_specs=[ANY] * na,
        out_shape=[jax.ShapeDtypeStruct((N_CHIPS,) + a.shape, a.dtype) for a in arrs],
        scratch_shapes=[pltpu.SemaphoreType.DMA((nsem,)), pltpu.SemaphoreType.DMA((nsem,))],
    )(*arrs)
    chip = 2 * lax.axis_index("x") + lax.axis_index("y")
    return [lax.dynamic_update_slice(o, a[None], (chip,) + (0,) * a.ndim) for o, a in zip(outs, arrs)]


def _join_halves(piece, full_rest, small):
    def body(pc_ref, rest_in, sm, oin, orest, osm, send_sems, recv_sems, local_sem):
        del rest_in
        x, y, c = _position()
        me = 4 * x + 2 * y + c
        sib = (x, y, 1 - c)
        rows = orest.shape[0] // 2
        mine = orest.at[pl.ds(pl.multiple_of(c * rows, rows), rows)]
        copies = [
            pltpu.make_async_remote_copy(
                src_ref=pc_ref.at[pl.ds(c, HALF_START)], dst_ref=oin.at[pl.ds(c * HALF_SLABS, HALF_START)],
                send_sem=send_sems.at[0], recv_sem=recv_sems.at[0], device_id=sib, device_id_type=MESH),
            pltpu.make_async_remote_copy(
                src_ref=mine, dst_ref=mine, send_sem=send_sems.at[1], recv_sem=recv_sems.at[1],
                device_id=sib, device_id_type=MESH)]
        na = 2
        copies.append(pltpu.make_async_copy(sm, osm.at[me], local_sem))
        for r in range(1, N_DEV):
            to = (_flip(x, (r >> 2) & 1), _flip(y, (r >> 1) & 1), _flip(c, r & 1))
            copies.append(pltpu.make_async_remote_copy(
                src_ref=sm, dst_ref=osm.at[me], send_sem=send_sems.at[na + r - 1],
                recv_sem=recv_sems.at[na + r - 1], device_id=to, device_id_type=MESH))
        for cp in copies:
            cp.start()
        for cp in copies:
            cp.wait()

    nsem = 2 + N_DEV - 1
    return pl.pallas_call(
        body, name="join_halves",
        in_specs=[ANY] * 3, out_specs=[ANY] * 3,
        out_shape=[jax.ShapeDtypeStruct((SLABS_CHIP,) + piece.shape[1:], piece.dtype),
                   jax.ShapeDtypeStruct(full_rest.shape, full_rest.dtype),
                   jax.ShapeDtypeStruct((N_DEV,) + small.shape, small.dtype)],
        scratch_shapes=[pltpu.SemaphoreType.DMA((nsem,)), pltpu.SemaphoreType.DMA((nsem,)),
                        pltpu.SemaphoreType.DMA],
        input_output_aliases={1: 1},
    )(piece, full_rest, small)


def _to_internal(w_slabs):
    order = sorted(SEGMENTS, key=lambda s: s[2])
    main = jnp.concatenate([w_slabs[g0 // 2:(g0 + w) // 2] for g0, w, _ in order], axis=0)
    gate = w_slabs[GATE_COL // 2:GATE_COL // 2 + NH].reshape(2 * NH, D)
    return main.reshape(NMAIN, D), jnp.pad(gate, ((0, GATE_W - 2 * NH), (0, 0)))


def _to_global(main_t, gate_t):
    main = main_t.reshape(NMAIN // 2, 16, D // 8)
    parts = sorted([(g0, main[i0 // 2:(i0 + w) // 2]) for g0, w, i0 in SEGMENTS]
                   + [(GATE_COL, gate_t[0:2 * NH].reshape(NH, 16, D // 8))], key=lambda s: s[0])
    return jnp.concatenate([p for _, p in parts], axis=0)


def _pack_rows(wa, wb, wo, wpg, wp):
    return jnp.concatenate([wa, wb, wo, wpg, wp.reshape(PACK_ROWS[4], D)], axis=0)


def _pad_rows(a, rows=8):
    return jnp.pad(a, ((0, rows - a.shape[0]), (0, 0)))


def kernel(x, p, g_mix, w_in, conv_w, conv_b, w_a_out, b_gates, g_head, w_b_out, w_o, g_ple, w_ple_gate, w_ple, g_final, loss_target, m_g_mix, m_w_in, m_conv_w, m_conv_b, m_w_a_out, m_b_gates, m_g_head, m_w_b_out, m_w_o, m_g_ple, m_w_ple_gate, m_w_ple, m_g_final, v_g_mix, v_w_in, v_conv_w, v_conv_b, v_w_a_out, v_b_gates, v_g_head, v_w_b_out, v_w_o, v_g_ple, v_w_ple_gate, v_w_ple, v_g_final):
    chip = 2 * lax.axis_index("x") + lax.axis_index("y")
    xs, ps, ts = x[0], p[0, 0], loss_target[0]
    g_fin = g_final.reshape(1, D)

    pack_w = _pack_rows(w_a_out[0], w_b_out[0], w_o[0], w_ple_gate[0], w_ple[0])
    w_slabs = _cast_slabs(jnp.transpose(w_in, (2, 0, 1)).reshape(SLABS_CHIP, 16, D // 8))
    half_rows = PACK_TOTAL // 2
    g_win, g_pack, g_cw = _allgather_chips(
        [(w_slabs, (HALF_START, HALF_SLABS, HALF_SLABS, HALF_START)),
         (pack_w.astype(MXU), (half_rows, half_rows, half_rows, half_rows))], [_pad_rows(conv_w[0])])
    w_t, wg_t = _to_internal(g_win.reshape(N_IN // 2, 16, D // 8))
    offs = [0, 256, 768, 1024, 1280, 1344]
    wa, wb, wo, wpg = [g_pack[:, offs[k]:offs[k + 1]].reshape(-1, D) for k in range(4)]
    wp = jnp.transpose(g_pack[:, offs[4]:offs[5]].reshape(N_CHIPS, PLE, PLE), (1, 0, 2)).reshape(PLE, D)
    cw = jnp.transpose(g_cw, (1, 0, 2)).reshape(8, D)

    bias = jnp.pad(b_gates, ((0, 0), (0, GATE_W - 2 * NH)))
    proj, hn, gates = _proj(xs, g_mix, w_t, wg_t)
    h, stats, cs, ns, ms = _mlstm_fwd(proj, gates, bias)
    (a_pre, b_pre, mg, xn1, de, dgp, ya, yb, x1, dx2, acc_f) = _tail_fwd(
        proj, h, xs, ps, ts, cw, conv_b, g_head, g_ple, g_fin, wa, wb, wo, wpg, wp)
    dproj, dcv, dh, dx1, dx1b, dya, dyb, acc_b = _tail_bwd(
        proj, h, dgp, dx2, x1, ya, yb, cw, conv_b, g_head, g_ple, wpg, wo, wb, wa)
    dproj = _conv_bwd(proj, dcv, cw, dproj)
    dproj, dgates, gsum = _mlstm_bwd(proj, gates, bias, h, dh, stats, cs, ns, ms, dproj)
    d_main = _matmul_tn(dproj, hn, "dw_in", out_dtype=WIRE, tk=4096)
    d_gate = _matmul_tn(dgates, hn, "dw_gate", out_dtype=WIRE)
    d_wa = _matmul_tn(a_pre, dya, "dw_a_out", out_dtype=WIRE)
    d_wb = _matmul_tn(b_pre, dyb, "dw_b_out", out_dtype=WIRE)
    d_wo = _matmul_tn(mg, dx1b, "dw_o", out_dtype=WIRE)
    d_wpg = _matmul_tn(xn1, dgp, "dw_ple_gate", out_dtype=WIRE)
    d_wp = _matmul_tn(ps, de, "dw_ple", out_dtype=WIRE)

    g_in = _to_global(d_main, d_gate).reshape(N_CHIPS, SLABS_CHIP, 16, D // 8)
    d_wp_c = jnp.transpose(d_wp.reshape(PLE, N_CHIPS, PLE), (1, 0, 2)).reshape(N_CHIPS, PACK_ROWS[4], D)
    g_rest = jnp.concatenate(
        [d_wa.reshape(N_CHIPS, -1, D), d_wb.reshape(N_CHIPS, -1, D), d_wo.reshape(N_CHIPS, -1, D),
         d_wpg.reshape(N_CHIPS, -1, D), d_wp_c], axis=1)
    grad_x, acc_x, r_in, r_rest = _input_grad(dproj, dgates, w_t, wg_t, xs, dx1, g_mix, g_in, g_rest)
    small = _pack_small(acc_f, acc_b, acc_x, gsum)
    core = lax.axis_index("c").astype(jnp.int32)
    piece = _sum_slabs(r_in, "sum_w_in")
    gw_in, gw_rest, r_small = _join_halves(
        piece, _sum_slots(r_rest, "sum_rest", tr=96, half=core.reshape(1)), small)
    gw_in = lax.dynamic_update_slice(gw_in, piece, (core * HALF_START, 0, 0)).reshape(COLS_CHIP, 8, D // 8)
    gs = _sum_slots(r_small, "sum_small", tr=SMALL_ROWS)

    big = []
    row0 = 0
    for name, w, m, v in (("w_a_out", w_a_out, m_w_a_out, v_w_a_out), ("w_b_out", w_b_out, m_w_b_out, v_w_b_out),
                          ("w_o", w_o, m_w_o, v_w_o), ("w_ple_gate", w_ple_gate, m_w_ple_gate, v_w_ple_gate)):
        big.append((name, w, m, v, gw_rest, row0))
        row0 += w.shape[1]
    g_wp = gw_rest[row0:row0 + PACK_ROWS[4]].reshape(PLE, PLE)
    big.append(("w_ple", w_ple, m_w_ple, v_w_ple, g_wp, 0))
    upd = {name: _adamw(w, g, m, v, "adamw_" + name, g_row0=r0) for name, w, m, v, g, r0 in big}
    g_big = {name: (g if name == "w_ple" else g[r0:r0 + w.shape[1]])[None] for name, w, m, v, g, r0 in big}
    slabs = lambda a: jnp.transpose(a, (2, 0, 1)).reshape(COLS_CHIP, 8, D // 8)
    unslab = lambda a: jnp.transpose(a, (1, 2, 0)).reshape(1, D, COLS_CHIP)
    upd["w_in"] = [unslab(u) for u in _adamw_slabs(
        slabs(w_in), gw_in, slabs(m_w_in), slabs(v_w_in), "adamw_w_in", tr=326)]
    g_big["w_in"] = unslab(gw_in)

    lane = lax.broadcasted_iota(jnp.int32, (1, D), 1)
    g_small = jnp.concatenate([gs[0:6], jnp.where(lane < 2 * NH, gs[9:10], 0.0), jnp.zeros((1, D), F32)], axis=0)

    def small_pack(gm, cb_, bg, gh, gp, gf):
        return jnp.concatenate([gm, cb_, gh.reshape(2, D), gp, gf.reshape(1, D),
                                jnp.pad(bg, ((0, 0), (0, D - 2 * NH))), jnp.zeros((1, D), F32)], axis=0)

    ws = small_pack(g_mix, conv_b, b_gates, g_head, g_ple, g_final)
    ms_ = small_pack(m_g_mix, m_conv_b, m_b_gates, m_g_head, m_g_ple, m_g_final)
    vs = small_pack(v_g_mix, v_conv_b, v_b_gates, v_g_head, v_g_ple, v_g_final)
    upd_s = _adamw(ws, g_small, ms_, vs, "adamw_small")
    g_cw_mine = _pad_rows(lax.dynamic_slice(gs[6:9], (0, chip * PLE), (3, PLE)))
    upd_c = _adamw(_pad_rows(conv_w[0]), g_cw_mine, _pad_rows(m_conv_w[0]), _pad_rows(v_conv_w[0]),
                   "adamw_conv_w")

    def leaves(bigs, sm, cwv):
        return [sm[0:1], bigs["w_in"], cwv[0:3][None], sm[1:2], bigs["w_a_out"], sm[6:7, 0:2 * NH],
                sm[2:4].reshape(1, VD), bigs["w_b_out"], bigs["w_o"], sm[4:5], bigs["w_ple_gate"],
                bigs["w_ple"], sm[5]]

    loss = gs[9, 2 * NH]
    outs = [loss, grad_x[None]] + leaves(g_big, g_small, g_cw_mine)
    for k in range(3):
        outs += leaves({name: u[k] for name, u in upd.items()}, upd_s[k], upd_c[k])
    return tuple(outs)
```

```python
import jax
import jax.numpy as jnp
from jax import lax
from jax.experimental import pallas as pl
from jax.experimental.pallas import tpu as pltpu

F32 = jnp.float32
MXU = jnp.bfloat16
WIRE = jnp.bfloat16

D = 1024
NH, DK, DV = 4, 256, 512
VD = NH * DV
PLE = 256
LCH = 256
EPS = 1e-6
N_IN = 14344
NMAIN = 14336
GATE_W = 128
N_CHIPS, N_DEV = 4, 8

C_BA, C_ZA, C_O, C_ZB, C_GA, C_GB = 0, 1024, 2048, 4096, 6144, 7168
QK = NH * DK
C_Q, C_K, C_V, C_XA, C_CA = 8192, 9216, 10240, 12288, 13312
QKV_W = 2 * QK + VD
TAIL_W = 8192
CONV_W = 2048
SEGMENTS = (
    (0, 1024, C_XA), (1024, 1024, C_BA), (2048, 1024, C_CA), (3072, 1024, C_ZA),
    (4096, QKV_W, C_Q), (8192, 4096, C_O), (12296, 2048, C_GA),
)
GATE_COL = 12288

COLS_CHIP = N_IN // N_CHIPS
SLABS_CHIP = COLS_CHIP // 2
HALF_START = SLABS_CHIP // 2
HALF_SLABS = SLABS_CHIP - HALF_START

PACK_ROWS = (256, 512, 256, 256, 64)
PACK_TOTAL = sum(PACK_ROWS)
SMALL_ROWS = 16

ADAM_LR, ADAM_B1, ADAM_B2, ADAM_EPS, ADAM_WD, ADAM_STEP = 0.001, 0.9, 0.999, 1e-08, 0.01, 10

VMEM_LIMIT = 56 * 1024 * 1024
MESH = pl.DeviceIdType.MESH
ANY = pl.BlockSpec(memory_space=pl.ANY)


def _cparams(*sem):
    return pltpu.CompilerParams(dimension_semantics=sem, vmem_limit_bytes=VMEM_LIMIT)


def _dot(a, b):
    return jnp.dot(a, b, preferred_element_type=F32)


def _dot_nt(a, b):
    return lax.dot_general(a, b, (((1,), (1,)), ((), ())), preferred_element_type=F32)


def _dot_tn(a, b):
    return lax.dot_general(a, b, (((0,), (0,)), ((), ())), preferred_element_type=F32)


def _sigmoid(x):
    return 1.0 / (1.0 + jnp.exp(-x))


def _logsig(x):
    return jnp.minimum(x, 0.0) - jnp.log(1.0 + jnp.exp(-jnp.abs(x)))


GATE_FLOOR = -80.0


def _gate(v):
    e = jnp.exp(-jnp.maximum(v, GATE_FLOOR))
    s = 1.0 / (1.0 + e)
    return s, e * s * s


def _rstd(x):
    return lax.rsqrt(jnp.mean(x * x, axis=-1, keepdims=True) + EPS)


def _norm_bwd(dy, xhat, r, g):
    dxh = dy * g
    return r * (dxh - xhat * jnp.mean(dxh * xhat, axis=-1, keepdims=True))


def _f32(ref):
    return ref[...].astype(F32)


def _proj(x, g_mix, w_t, wg_t, tm=1024, tn=2048):
    n = x.shape[0]
    m = w_t.shape[0]
    tm = min(tm, n)

    def body(x_ref, g_ref, b_ref, wg_ref, o_ref, hn_ref, gate_ref):
        @pl.when(pl.program_id(1) == 0)
        def _():
            xv = x_ref[...]
            hn = (xv * _rstd(xv) * g_ref[...]).astype(MXU)
            hn_ref[...] = hn
            gate_ref[...] = _dot_nt(hn, wg_ref[...])

        o_ref[...] = _dot_nt(hn_ref[...], b_ref[...]).astype(MXU)

    return pl.pallas_call(
        body, name="proj", grid=(n // tm, m // tn),
        in_specs=[pl.BlockSpec((tm, D), lambda i, j: (i, 0)),
                  pl.BlockSpec((1, D), lambda i, j: (0, 0)),
                  pl.BlockSpec((tn, D), lambda i, j: (j, 0)),
                  pl.BlockSpec((GATE_W, D), lambda i, j: (0, 0))],
        out_specs=[pl.BlockSpec((tm, tn), lambda i, j: (i, j)),
                   pl.BlockSpec((tm, D), lambda i, j: (i, 0)),
                   pl.BlockSpec((tm, GATE_W), lambda i, j: (i, 0))],
        out_shape=[jax.ShapeDtypeStruct((n, m), MXU), jax.ShapeDtypeStruct((n, D), MXU),
                   jax.ShapeDtypeStruct((n, GATE_W), F32)],
        compiler_params=_cparams("arbitrary", "arbitrary"),
    )(x, g_mix, w_t, wg_t)


def _matmul_tn(a, b, name, out_dtype=F32, ta=1024, tb=1024, tk=2048):
    n, ka = a.shape
    kb = b.shape[1]
    ta, tb, tk = min(ta, ka), min(tb, kb), min(tk, n)
    nk = n // tk

    def body(a_ref, b_ref, o_ref, acc):
        kk = pl.program_id(2)

        @pl.when(kk == 0)
        def _():
            acc[...] = jnp.zeros_like(acc)

        acc[...] += _dot_tn(a_ref[...].astype(MXU), b_ref[...].astype(MXU))

        @pl.when(kk == nk - 1)
        def _():
            o_ref[...] = acc[...].astype(out_dtype)

    return pl.pallas_call(
        body, name=name, grid=(ka // ta, kb // tb, nk),
        in_specs=[pl.BlockSpec((tk, ta), lambda i, j, kk: (kk, i)),
                  pl.BlockSpec((tk, tb), lambda i, j, kk: (kk, j))],
        out_specs=pl.BlockSpec((ta, tb), lambda i, j, kk: (i, j)),
        out_shape=jax.ShapeDtypeStruct((ka, kb), out_dtype),
        scratch_shapes=[pltpu.VMEM((ta, tb), F32)],
        compiler_params=_cparams("arbitrary", "arbitrary", "arbitrary"),
    )(a, b)


def _gate_vectors(g, gt, hd):
    lane = lax.broadcasted_iota(jnp.int32, g.shape, 1)
    sub = lax.broadcasted_iota(jnp.int32, gt.shape, 0)
    col = lambda j: jnp.sum(jnp.where(lane == j, g, 0.0), axis=1, keepdims=True)
    row = lambda j: jnp.sum(jnp.where(sub == j, gt, 0.0), axis=0, keepdims=True)
    return col(hd), col(hd + NH), row(hd), row(hd + NH)


def _chunk_decay(li_col, li_row, lf_col, lf_row, m_prev):
    n = li_col.shape[0]
    r = lax.broadcasted_iota(jnp.int32, (n, n), 0)
    c = lax.broadcasted_iota(jnp.int32, (n, n), 1)
    tri = r >= c
    b_col = jnp.sum(jnp.where(tri, lf_row, 0.0), axis=1, keepdims=True)
    b_row = jnp.sum(jnp.where(r <= c, lf_col, 0.0), axis=0, keepdims=True)
    b_last = jnp.sum(lf_row, axis=1, keepdims=True)
    dmat = jnp.where(tri, b_col - b_row + li_row, -jnp.inf)
    a_col = b_col + m_prev
    g_col = b_last - b_col + li_col
    m_new = jnp.maximum(b_last + m_prev, jnp.max(g_col, axis=0, keepdims=True))
    w_col = jnp.exp(g_col - m_new)
    decay = jnp.exp(b_last + m_prev - m_new)
    return tri, dmat, a_col, m_new, w_col, decay


def _qkv_specs(row_of):
    q_spec = pl.BlockSpec((LCH, QK), lambda c: (row_of(c), C_Q // QK))
    k_spec = pl.BlockSpec((LCH, QK), lambda c: (row_of(c), C_K // QK))
    v_spec = pl.BlockSpec((LCH, VD), lambda c: (row_of(c), C_V // VD))
    return q_spec, k_spec, v_spec


def _state_specs(row_of):
    return [pl.BlockSpec((NH, None, DK, DV), lambda c: (0, row_of(c), 0, 0)),
            pl.BlockSpec((NH, None, 1, DK), lambda c: (0, row_of(c), 0, 0)),
            pl.BlockSpec((NH, None, 1, GATE_W), lambda c: (0, row_of(c), 0, 0))]


def _lane_put(col, lane_id, width=GATE_W):
    lane = lax.broadcasted_iota(jnp.int32, (col.shape[0], width), 1)
    return jnp.where(lane == lane_id, col, 0.0)


def _lane_get(block, lane_id):
    lane = lax.broadcasted_iota(jnp.int32, block.shape, 1)
    return jnp.sum(jnp.where(lane == lane_id, block, 0.0), axis=1, keepdims=True)


def _mlstm_fwd(proj, gates, bias):
    n = proj.shape[0]
    nc = n // LCH

    def body(q_ref, k_ref, v_ref, g_ref, bias_ref,
             h_ref, st_ref, cs_ref, ns_ref, ms_ref, c_scr, n_scr, m_scr):
        @pl.when(pl.program_id(0) == 0)
        def _():
            c_scr[...] = jnp.zeros_like(c_scr)
            n_scr[...] = jnp.zeros_like(n_scr)
            m_scr[...] = jnp.full_like(m_scr, -jnp.inf)

        g = g_ref[...] + bias_ref[...]
        gt = g.T[0:8, :]
        stats = jnp.zeros((LCH, GATE_W), F32)
        for hd in range(NH):
            li_col, fr_col, li_row, fr_row = _gate_vectors(g, gt, hd)
            m_all = m_scr[hd]
            m_prev = m_all[0:1, 0:1]
            tri, dmat, a_col, m_new, w_col, decay = _chunk_decay(
                li_col, li_row, _logsig(fr_col), _logsig(fr_row), m_prev)
            m_col = jnp.maximum(a_col, jnp.max(dmat, axis=1, keepdims=True))
            dl = jnp.exp(dmat - m_col)
            inter = jnp.exp(a_col - m_col)

            qs = q_ref[:, hd * DK:(hd + 1) * DK] * (DK ** -0.5)
            kk = k_ref[:, hd * DK:(hd + 1) * DK]
            vv = v_ref[:, hd * DV:(hd + 1) * DV]
            cst = c_scr[hd]
            nst = n_scr[hd]
            cs_ref[hd] = cst
            ns_ref[hd] = nst
            ms_ref[hd] = m_all

            sc = _dot_nt(qs, kk) * dl
            num = _dot(sc.astype(MXU), vv) + inter * _dot(qs, cst.astype(MXU))
            den = (jnp.sum(sc, axis=1, keepdims=True)
                   + inter * jnp.sum(qs.astype(F32) * nst, axis=1, keepdims=True))
            nrm = jnp.maximum(jnp.abs(den), jnp.exp(-m_col))
            h_ref[:, hd * DV:(hd + 1) * DV] = (num / nrm).astype(h_ref.dtype)
            stats = stats + _lane_put(den, hd) + _lane_put(m_col, NH + hd)

            kw = kk.astype(F32) * w_col
            c_scr[hd] = decay * cst + _dot_tn(kw.astype(MXU), vv)
            n_scr[hd] = decay * nst + jnp.sum(kw, axis=0, keepdims=True)
            m_scr[hd] = jnp.broadcast_to(m_new, (1, GATE_W))
        st_ref[...] = stats

    q_spec, k_spec, v_spec = _qkv_specs(lambda c: c)
    return pl.pallas_call(
        body, name="mlstm_fwd", grid=(nc,),
        in_specs=[q_spec, k_spec, v_spec,
                  pl.BlockSpec((LCH, GATE_W), lambda c: (c, 0)),
                  pl.BlockSpec((1, GATE_W), lambda c: (0, 0))],
        out_specs=[pl.BlockSpec((LCH, VD), lambda c: (c, 0)),
                   pl.BlockSpec((LCH, GATE_W), lambda c: (c, 0))] + _state_specs(lambda c: c),
        out_shape=[jax.ShapeDtypeStruct((n, VD), MXU),
                   jax.ShapeDtypeStruct((n, GATE_W), F32),
                   jax.ShapeDtypeStruct((NH, nc, DK, DV), F32),
                   jax.ShapeDtypeStruct((NH, nc, 1, DK), F32),
                   jax.ShapeDtypeStruct((NH, nc, 1, GATE_W), F32)],
        scratch_shapes=[pltpu.VMEM((NH, DK, DV), F32), pltpu.VMEM((NH, 1, DK), F32),
                        pltpu.VMEM((NH, 1, GATE_W), F32)],
        compiler_params=_cparams("arbitrary"),
    )(proj, proj, proj, gates, bias)


def _mlstm_bwd(proj, gates, bias, h, dh, stats, cs, ns, ms, dproj):
    n = proj.shape[0]
    nc = n // LCH

    def body(q_ref, k_ref, v_ref, g_ref, bias_ref, h_ref, dh_ref, st_ref,
             cs_ref, ns_ref, ms_ref, dproj_in,
             dqkv_ref, dg_ref, gsum_ref, dc_scr, dn_scr):
        del dproj_in

        @pl.when(pl.program_id(0) == 0)
        def _():
            dc_scr[...] = jnp.zeros_like(dc_scr)
            dn_scr[...] = jnp.zeros_like(dn_scr)
            gsum_ref[...] = jnp.zeros_like(gsum_ref)

        g = g_ref[...] + bias_ref[...]
        gt = g.T[0:8, :]
        stats = st_ref[...]
        r = lax.broadcasted_iota(jnp.int32, (LCH, LCH), 0)
        c = lax.broadcasted_iota(jnp.int32, (LCH, LCH), 1)
        eye = r == c
        to_col = lambda row: jnp.sum(jnp.where(eye, row, 0.0), axis=1, keepdims=True)
        last = lax.broadcasted_iota(jnp.int32, (LCH, 1), 0) == LCH - 1
        dg = jnp.zeros((LCH, GATE_W), F32)
        for hd in range(NH):
            li_col, fr_col, li_row, fr_row = _gate_vectors(g, gt, hd)
            m_prev = ms_ref[hd][0:1, 0:1]
            tri, dmat, a_col, m_new, w_col, decay = _chunk_decay(
                li_col, li_row, _logsig(fr_col), _logsig(fr_row), m_prev)
            m_col = _lane_get(stats, NH + hd)
            dl = jnp.exp(dmat - m_col)
            inter = jnp.exp(a_col - m_col)

            qs = q_ref[:, hd * DK:(hd + 1) * DK] * (DK ** -0.5)
            kk = k_ref[:, hd * DK:(hd + 1) * DK]
            vv = v_ref[:, hd * DV:(hd + 1) * DV]
            qf = qs.astype(F32)
            kf = kk.astype(F32)
            cst = cs_ref[hd]
            nst = ns_ref[hd]
            cb = cst.astype(MXU)
            dcn = dc_scr[hd]
            dnn = dn_scr[hd]
            dcb = dcn.astype(MXU)

            den = _lane_get(stats, hd)
            floor = jnp.exp(-m_col)
            nrm = jnp.maximum(jnp.abs(den), floor)
            dhv = dh_ref[:, hd * DV:(hd + 1) * DV].astype(F32)
            dnum = dhv / nrm
            dnum_b = dnum.astype(MXU)
            dhh = jnp.sum(dhv * h_ref[:, hd * DV:(hd + 1) * DV].astype(F32), axis=1, keepdims=True)
            dden = jnp.where(jnp.abs(den) > floor, -dhh / nrm * jnp.sign(den), 0.0)

            sc = _dot_nt(qs, kk) * dl
            dsc = _dot_nt(dnum_b, vv) + dden
            da = (dl * dsc).astype(MXU)
            gmat = sc * dsc

            dq = _dot(da, kk) + inter * (_dot_nt(dnum_b, cb) + dden * nst)
            dk_state = w_col * (_dot_nt(vv, dcb) + dnn)
            dk = _dot_tn(da, qs) + dk_state
            kw = (kf * w_col).astype(MXU)
            dv = _dot_tn(sc.astype(MXU), dnum_b) + _dot(kw, dcb)
            dqkv_ref[:, hd * DK:(hd + 1) * DK] = (dq * (DK ** -0.5)).astype(dqkv_ref.dtype)
            dqkv_ref[:, QK + hd * DK:QK + (hd + 1) * DK] = dk.astype(dqkv_ref.dtype)
            dqkv_ref[:, 2 * QK + hd * DV:2 * QK + (hd + 1) * DV] = dv.astype(dqkv_ref.dtype)

            num_i = _dot(qs, cb)
            den_i = jnp.sum(qf * nst, axis=1, keepdims=True)
            e_col = inter * (jnp.sum(dnum * num_i, axis=1, keepdims=True) + dden * den_i)
            h_col = jnp.sum(kf * dk_state, axis=1, keepdims=True)
            f_dec = decay * (jnp.sum(jnp.sum(cst * dcn, axis=1, keepdims=True), axis=0, keepdims=True)
                             + jnp.sum(nst * dnn, axis=1, keepdims=True))
            row_g = jnp.sum(gmat, axis=1, keepdims=True)
            col_g = to_col(jnp.sum(gmat, axis=0, keepdims=True))
            db_col = row_g - col_g + e_col - h_col
            db_col = db_col + jnp.where(last, jnp.sum(h_col, axis=0, keepdims=True) + f_dec, 0.0)
            dli_col = col_g + h_col
            dlf_row = jnp.sum(jnp.where(tri, db_col, 0.0), axis=0, keepdims=True)
            df_col = to_col(dlf_row) * _sigmoid(-fr_col)
            dg = dg + _lane_put(dli_col, hd) + _lane_put(df_col, NH + hd)

            dc_scr[hd] = decay * dcn + _dot_tn((qf * inter).astype(MXU), dnum_b)
            dn_scr[hd] = decay * dnn + jnp.sum(qf * (inter * dden), axis=0, keepdims=True)
        dg_ref[...] = dg
        gsum_ref[0:1, 0:GATE_W] += jnp.sum(dg, axis=0, keepdims=True)

    rev = lambda c: nc - 1 - c
    q_spec, k_spec, v_spec = _qkv_specs(rev)
    hv_spec = pl.BlockSpec((LCH, VD), lambda c: (rev(c), 0))
    gate_spec = pl.BlockSpec((LCH, GATE_W), lambda c: (rev(c), 0))
    return pl.pallas_call(
        body, name="mlstm_bwd", grid=(nc,),
        in_specs=[q_spec, k_spec, v_spec, gate_spec,
                  pl.BlockSpec((1, GATE_W), lambda c: (0, 0)),
                  hv_spec, hv_spec, gate_spec] + _state_specs(rev) + [ANY],
        out_specs=[pl.BlockSpec((LCH, QKV_W), lambda c: (rev(c), C_Q // QKV_W)),
                   gate_spec,
                   pl.BlockSpec((8, D), lambda c: (0, 0))],
        out_shape=[jax.ShapeDtypeStruct(dproj.shape, dproj.dtype),
                   jax.ShapeDtypeStruct((n, GATE_W), F32),
                   jax.ShapeDtypeStruct((8, D), F32)],
        scratch_shapes=[pltpu.VMEM((NH, DK, DV), F32), pltpu.VMEM((NH, 1, DK), F32)],
        input_output_aliases={11: 0},
        compiler_params=_cparams("arbitrary"),
    )(proj, proj, proj, gates, bias, h, dh, stats, cs, ns, ms, dproj)


def _proj_spec(tm, col, width):
    return pl.BlockSpec((tm, width), lambda i: (i, col // width))


def _tail_in_specs(tm):
    return [_proj_spec(tm, 0, TAIL_W), _proj_spec(tm, C_XA, CONV_W),
            pl.BlockSpec((8, CONV_W), lambda i: (jnp.maximum(i * (tm // 8) - 1, 0), C_XA // CONV_W))]


RING = 3


def _ring_fetch(src, ring, sems, i, nt, tm, width):
    def copy(step):
        if isinstance(step, int):
            slot, first = step % RING, step * tm
        else:
            slot, first = lax.rem(step, RING), pl.multiple_of(step * tm, tm)
        rows = pl.ds(first, tm)
        return pltpu.make_async_copy(src.at[rows, pl.ds(0, width)], ring.at[slot], sems.at[slot])

    @pl.when(i == 0)
    def _():
        for s in range(min(RING - 1, nt)):
            copy(s).start()

    @pl.when(i + RING - 1 < nt)
    def _():
        copy(i + RING - 1).start()

    copy(i).wait()
    return ring.at[lax.rem(i, RING)]


def _tail_views(tail_r, conv_r, halo_r):
    cols = lambda ref, c0, w: ref.at[:, pl.ds(c0, w)]
    return (cols(tail_r, C_BA, D), cols(tail_r, C_ZA, D), cols(tail_r, C_O, VD), cols(tail_r, C_ZB, VD),
            cols(tail_r, C_GA, D), cols(tail_r, C_GB, D), cols(conv_r, 0, D), cols(conv_r, D, D),
            cols(halo_r, 0, D), cols(halo_r, D, D))


def _const(shape):
    return pl.BlockSpec(shape, lambda i: (0,) * len(shape))


def _conv_inputs(i, tm, xa_ref, ca_ref, xah_ref, cah_ref):
    u = _f32(xa_ref) * _f32(ca_ref)
    uh = jnp.where(i > 0, _f32(xah_ref) * _f32(cah_ref), 0.0)
    rid = lax.broadcasted_iota(jnp.int32, u.shape, 0)
    u1 = jnp.where(rid == 0, uh[7:8, :], pltpu.roll(u, 1, 0))
    u2 = jnp.where(rid == 0, uh[6:7, :], jnp.where(rid == 1, uh[7:8, :], pltpu.roll(u, 2, 0)))
    return u, u1, u2


def _head_norm(hh, gh):
    out = []
    for j in range(NH):
        hj = hh[:, j * DV:(j + 1) * DV]
        rj = _rstd(hj)
        out.append((hj * rj, rj, gh[:, j * DV:(j + 1) * DV]))
    return out


def _tail_fwd(proj, h, x, p, t, cw, cb, gh, gple, gfin, wa, wb, wo, wpg, wp):
    n = x.shape[0]
    tm = min(256, n)

    def body(proj_hbm, conv_r, halo_r, h_r, x_r, p_r, t_r,
             cw_r, cb_r, gh_r, gple_r, gfin_r, wa_r, wb_r, wo_r, wpg_r, wp_r,
             apre_o, bpre_o, mg_o, xn1_o, de_o, dgp_o, ya_o, yb_o, x1_o, dx2_o, acc_o, ring, ring_sems):
        i = pl.program_id(0)
        tail_r = _ring_fetch(proj_hbm, ring, ring_sems, i, n // tm, tm, TAIL_W)
        ba_r, za_r, o_r, zb_r, ga_r, gb_r, xa_r, ca_r, xah_r, cah_r = _tail_views(tail_r, conv_r, halo_r)

        @pl.when(i == 0)
        def _():
            acc_o[...] = jnp.zeros_like(acc_o)

        u, u1, u2 = _conv_inputs(i, tm, xa_r, ca_r, xah_r, cah_r)
        cwv = cw_r[...]
        cv = cwv[0:1, :] * u2 + cwv[1:2, :] * u1 + cwv[2:3, :] * u + cb_r[...]
        za = za_r[...]
        a_pre = ba_r[...] * cv.astype(MXU) * (za * _sigmoid(za))
        apre_o[...] = a_pre
        ya = _dot(a_pre, wa_r[...]).astype(MXU)

        hn = _head_norm(_f32(h_r), gh_r[...])
        hbn = jnp.concatenate([xh * g for xh, _, g in hn], axis=1)
        zb = zb_r[...]
        b_pre = _sigmoid(o_r[...]) * hbn.astype(MXU) * (zb * _sigmoid(zb))
        bpre_o[...] = b_pre
        yb = _dot(b_pre, wb_r[...]).astype(MXU)
        ya_o[...] = ya
        yb_o[...] = yb

        mg = _sigmoid(ga_r[...]) * ya + _sigmoid(gb_r[...]) * yb
        mg_o[...] = mg
        x1 = x_r[...] + _dot(mg, wo_r[...])
        x1_o[...] = x1.astype(MXU)
        xn1 = (x1 * _rstd(x1) * gple_r[...]).astype(MXU)
        xn1_o[...] = xn1
        gt = _sigmoid(_dot(xn1, wpg_r[...]))
        e = _dot(p_r[...].astype(MXU), wp_r[...])
        x2 = x1 + gt * e
        r2 = _rstd(x2)
        xh2 = x2 * r2
        gf = gfin_r[...]
        diff = xh2 * gf - t_r[...]
        dy = diff * (1.0 / D)
        dx2 = _norm_bwd(dy, xh2, r2, gf)
        dx2_o[...] = dx2
        de_o[...] = (dx2 * gt).astype(MXU)
        dgp_o[...] = (dx2 * e * gt * (1.0 - gt)).astype(MXU)
        acc_o[0:1, :] += jnp.sum(dy * xh2, axis=0, keepdims=True)
        loss = 0.5 * jnp.sum(jnp.sum(diff * diff, axis=1, keepdims=True), axis=0, keepdims=True) * (1.0 / D)
        acc_o[1:2, :] += jnp.broadcast_to(loss, (1, D))

    row = lambda w, dt: (pl.BlockSpec((tm, w), lambda i: (i, 0)), jax.ShapeDtypeStruct((n, w), dt))
    outs = [row(D, MXU), row(VD, MXU), row(D, MXU), row(D, MXU), row(D, MXU), row(D, MXU),
            row(D, MXU), row(D, MXU), row(D, MXU), row(D, F32),
            (_const((8, D)), jax.ShapeDtypeStruct((8, D), F32))]
    return pl.pallas_call(
        body, name="tail_fwd", grid=(n // tm,),
        in_specs=[ANY] + _tail_in_specs(tm)[1:] + [
                  pl.BlockSpec((tm, VD), lambda i: (i, 0)),
                  pl.BlockSpec((tm, D), lambda i: (i, 0)),
                  pl.BlockSpec((tm, PLE), lambda i: (i, 0)),
                  pl.BlockSpec((tm, D), lambda i: (i, 0)),
                  _const((8, D)), _const((1, D)), _const((1, VD)), _const((1, D)), _const((1, D)),
                  _const((D, D)), _const((VD, D)), _const((D, D)), _const((D, D)), _const((PLE, D))],
        out_specs=[s for s, _ in outs],
        out_shape=[s for _, s in outs],
        scratch_shapes=[pltpu.VMEM((RING, tm, TAIL_W), MXU), pltpu.SemaphoreType.DMA((RING,))],
        compiler_params=_cparams("arbitrary"),
    )(proj, proj, proj, h, x, p, t, cw, cb, gh, gple, gfin, wa, wb, wo, wpg, wp)


def _tail_bwd(proj, h, dgp, dx2, x1, ya, yb, cw, cb, gh, gple, wpg, wo, wb, wa):
    n = x1.shape[0]
    tm = min(256, n)

    def body(proj_hbm, conv_r, halo_r, h_r,
             dgp_r, dx2_r, x1_r, ya_r, yb_r, cw_r, cb_r, gh_r, gple_r,
             wpg_r, wo_r, wb_r, wa_r,
             dproj_o, dcv_o, dh_o, dx1_o, dx1b_o, dya_o, dyb_o, acc_o, ring, ring_sems):
        i = pl.program_id(0)
        tail_r = _ring_fetch(proj_hbm, ring, ring_sems, i, n // tm, tm, TAIL_W)
        ba_r, za_r, o_r, zb_r, ga_r, gb_r, xa_r, ca_r, xah_r, cah_r = _tail_views(tail_r, conv_r, halo_r)

        @pl.when(i == 0)
        def _():
            acc_o[...] = jnp.zeros_like(acc_o)

        dxn1 = _dot_nt(dgp_r[...], wpg_r[...])
        x1 = _f32(x1_r)
        r1 = _rstd(x1)
        xh1 = x1 * r1
        acc_o[0:1, 0:D] += jnp.sum(dxn1 * xh1, axis=0, keepdims=True)
        dx1 = dx2_r[...] + _norm_bwd(dxn1, xh1, r1, gple_r[...])
        dx1_o[...] = dx1
        dx1b = dx1.astype(MXU)
        dx1b_o[...] = dx1b

        dmg = _dot_nt(dx1b, wo_r[...]).astype(MXU)
        sga, dsga = _gate(ga_r[...])
        sgb, dsgb = _gate(gb_r[...])
        dya = dmg * sga
        dyb = dmg * sgb
        dya_o[...] = dya
        dyb_o[...] = dyb
        dproj_o[:, C_GA:C_GA + D] = dmg * ya_r[...] * dsga
        dproj_o[:, C_GB:C_GB + D] = dmg * yb_r[...] * dsgb

        db_pre = _dot_nt(dyb, wb_r[...]).astype(MXU)
        hn = _head_norm(_f32(h_r), gh_r[...])
        hbn = jnp.concatenate([xh * g for xh, _, g in hn], axis=1).astype(MXU)
        so, dso = _gate(o_r[...])
        zb = zb_r[...]
        szb, dszb = _gate(zb)
        sb = zb * szb
        t1 = db_pre * hbn
        dproj_o[:, C_O:C_O + VD] = t1 * (sb * dso)
        dproj_o[:, C_ZB:C_ZB + VD] = t1 * (so * (szb + zb * dszb))
        dhbn = (db_pre * (so * sb)).astype(F32)
        for j, (xh, rj, g) in enumerate(hn):
            dj = dhbn[:, j * DV:(j + 1) * DV]
            acc_o[1:2, j * DV:(j + 1) * DV] += jnp.sum(dj * xh, axis=0, keepdims=True)
            dh_o[:, j * DV:(j + 1) * DV] = _norm_bwd(dj, xh, rj, g).astype(MXU)

        da_pre = _dot_nt(dya, wa_r[...]).astype(MXU)
        u, u1, u2 = _conv_inputs(i, tm, xa_r, ca_r, xah_r, cah_r)
        cwv = cw_r[...]
        cv = cwv[0:1, :] * u2 + cwv[1:2, :] * u1 + cwv[2:3, :] * u + cb_r[...]
        za = za_r[...]
        sza, dsza = _gate(za)
        sa = za * sza
        ba = ba_r[...]
        t2 = da_pre * cv.astype(MXU)
        dproj_o[:, C_BA:C_BA + D] = t2 * sa
        dproj_o[:, C_ZA:C_ZA + D] = t2 * (ba * (sza + za * dsza))
        dcv_b = da_pre * (ba * sa)
        dcv_o[...] = dcv_b
        dcv = dcv_b.astype(F32)
        acc_o[2:3, 0:D] += jnp.sum(dcv, axis=0, keepdims=True)
        acc_o[3:4, 0:D] += jnp.sum(dcv * u2, axis=0, keepdims=True)
        acc_o[4:5, 0:D] += jnp.sum(dcv * u1, axis=0, keepdims=True)
        acc_o[5:6, 0:D] += jnp.sum(dcv * u, axis=0, keepdims=True)

    row = lambda w, dt: (pl.BlockSpec((tm, w), lambda i: (i, 0)), jax.ShapeDtypeStruct((n, w), dt))
    outs = [(pl.BlockSpec((tm, TAIL_W), lambda i: (i, 0)), jax.ShapeDtypeStruct((n, NMAIN), MXU)),
            row(D, MXU), row(VD, MXU), row(D, F32), row(D, MXU), row(D, MXU), row(D, MXU),
            (_const((8, VD)), jax.ShapeDtypeStruct((8, VD), F32))]
    rowin = lambda w: pl.BlockSpec((tm, w), lambda i: (i, 0))
    return pl.pallas_call(
        body, name="tail_bwd", grid=(n // tm,),
        in_specs=[ANY] + _tail_in_specs(tm)[1:] + [
                  rowin(VD), rowin(D), rowin(D), rowin(D), rowin(D), rowin(D),
                  _const((8, D)), _const((1, D)), _const((1, VD)), _const((1, D)),
                  _const((D, D)), _const((D, D)), _const((VD, D)), _const((D, D))],
        out_specs=[s for s, _ in outs],
        out_shape=[s for _, s in outs],
        scratch_shapes=[pltpu.VMEM((RING, tm, TAIL_W), MXU), pltpu.SemaphoreType.DMA((RING,))],
        compiler_params=_cparams("arbitrary"),
    )(proj, proj, proj, h, dgp, dx2, x1, ya, yb, cw, cb, gh, gple, wpg, wo, wb, wa)


def _conv_bwd(proj, dcv, cw, dproj):
    n = dcv.shape[0]
    tm = min(512, n)
    nt = n // tm

    def body(xa_r, ca_r, dcv_r, nxt_r, cw_r, dproj_in, dxc_o):
        del dproj_in
        i = pl.program_id(0)
        dcv_v = _f32(dcv_r)
        nxt = jnp.where(i < nt - 1, _f32(nxt_r), 0.0)
        rid = lax.broadcasted_iota(jnp.int32, dcv_v.shape, 0)
        d1 = jnp.where(rid == tm - 1, nxt[0:1, :], pltpu.roll(dcv_v, tm - 1, 0))
        d2 = jnp.where(rid == tm - 2, nxt[0:1, :],
                       jnp.where(rid == tm - 1, nxt[1:2, :], pltpu.roll(dcv_v, tm - 2, 0)))
        cwv = cw_r[...]
        du = cwv[2:3, :] * dcv_v + cwv[1:2, :] * d1 + cwv[0:1, :] * d2
        dxc_o[:, 0:D] = (du * _f32(ca_r)).astype(MXU)
        dxc_o[:, D:2 * D] = (du * _f32(xa_r)).astype(MXU)

    return pl.pallas_call(
        body, name="conv_bwd", grid=(nt,),
        in_specs=[_proj_spec(tm, C_XA, 1024), _proj_spec(tm, C_CA, 1024),
                  pl.BlockSpec((tm, D), lambda i: (i, 0)),
                  pl.BlockSpec((8, D), lambda i: (jnp.minimum((i + 1) * (tm // 8), n // 8 - 1), 0)),
                  _const((8, D)), ANY],
        out_specs=pl.BlockSpec((tm, CONV_W), lambda i: (i, C_XA // CONV_W)),
        out_shape=jax.ShapeDtypeStruct(dproj.shape, dproj.dtype),
        input_output_aliases={5: 0},
        compiler_params=_cparams("arbitrary"),
    )(proj, proj, dcv, dcv, cw, dproj)


def _position():
    return lax.axis_index("x"), lax.axis_index("y"), lax.axis_index("c")


def _flip(v, bit):
    return 1 - v if bit else v


def _part_start(core, stride, tiled):
    return pl.multiple_of(core * stride, stride) if tiled else core * stride


def _scatter_copies(srcs, dsts, strides, send_sems, recv_sems, local_sems):
    x, y, c = _position()
    me = 4 * x + 2 * y + c
    na = len(srcs)
    copies = []
    for r in range(N_DEV):
        px, py, pc = _flip(x, (r >> 2) & 1), _flip(y, (r >> 1) & 1), _flip(c, r & 1)
        for a in range(na):
            rows = dsts[a].shape[1]
            src = srcs[a].at[2 * px + py, pl.ds(_part_start(pc, strides[a], len(dsts[a].shape) == 3), rows)]
            dst = dsts[a].at[me]
            if r == 0:
                copies.append(pltpu.make_async_copy(src, dst, local_sems.at[a]))
            else:
                k = (r - 1) * na + a
                copies.append(pltpu.make_async_remote_copy(
                    src_ref=src, dst_ref=dst, send_sem=send_sems.at[k], recv_sem=recv_sems.at[k],
                    device_id=(px, py, pc), device_id_type=MESH))
    return copies


def _input_grad(dproj, dgates, w_t, wg_t, x, dx1, g_mix, g_in, g_rest):
    n = x.shape[0]
    tm, tk = min(1024, n), 2048
    nk = NMAIN // tk
    nt = n // tm

    def body(dp_r, dg_r, w_r, wg_r, x_r, dx1_r, g_r, gin, grest,
             gx_o, acc_o, oin, orest, acc, send_sems, recv_sems, local_sems):
        i = pl.program_id(0)
        kk = pl.program_id(1)
        copies = _scatter_copies((gin, grest), (oin, orest), strides, send_sems, recv_sems, local_sems)

        @pl.when((i == 0) & (kk == 0))
        def _():
            acc_o[...] = jnp.zeros_like(acc_o)
            for cp in copies:
                cp.start()

        @pl.when(kk == 0)
        def _():
            acc[...] = _dot(dg_r[...].astype(MXU), wg_r[...])

        acc[...] += _dot(dp_r[...], w_r[...])

        @pl.when(kk == nk - 1)
        def _():
            dhn = acc[...]
            xv = x_r[...]
            r0 = _rstd(xv)
            xh = xv * r0
            acc_o[0:1, :] += jnp.sum(dhn * xh, axis=0, keepdims=True)
            gx_o[...] = dx1_r[...] + _norm_bwd(dhn, xh, r0, g_r[...])

        @pl.when((i == nt - 1) & (kk == nk - 1))
        def _():
            for cp in copies:
                cp.wait()

    nrem = 2 * (N_DEV - 1)
    r_in, r_rest = HALF_SLABS, g_rest.shape[1] // 2
    strides = (HALF_START, r_rest)
    return pl.pallas_call(
        body, name="input_grad", grid=(nt, nk),
        in_specs=[pl.BlockSpec((tm, tk), lambda i, kk: (i, kk)),
                  pl.BlockSpec((tm, GATE_W), lambda i, kk: (i, 0)),
                  pl.BlockSpec((tk, D), lambda i, kk: (kk, 0)),
                  pl.BlockSpec((GATE_W, D), lambda i, kk: (0, 0)),
                  pl.BlockSpec((tm, D), lambda i, kk: (i, 0)),
                  pl.BlockSpec((tm, D), lambda i, kk: (i, 0)),
                  pl.BlockSpec((1, D), lambda i, kk: (0, 0)),
                  ANY, ANY],
        out_specs=[pl.BlockSpec((tm, D), lambda i, kk: (i, 0)),
                   pl.BlockSpec((8, D), lambda i, kk: (0, 0)),
                   ANY, ANY],
        out_shape=[jax.ShapeDtypeStruct((n, D), F32), jax.ShapeDtypeStruct((8, D), F32),
                   jax.ShapeDtypeStruct((N_DEV, r_in) + g_in.shape[2:], g_in.dtype),
                   jax.ShapeDtypeStruct((N_DEV, r_rest) + g_rest.shape[2:], g_rest.dtype)],
        scratch_shapes=[pltpu.VMEM((tm, D), F32),
                        pltpu.SemaphoreType.DMA((nrem,)), pltpu.SemaphoreType.DMA((nrem,)),
                        pltpu.SemaphoreType.DMA((2,))],
        compiler_params=_cparams("arbitrary", "arbitrary"),
    )(dproj, dgates, w_t, wg_t, x, dx1, g_mix, g_in, g_rest)


def _pack_small(acc_f, acc_b, acc_x, gsum):
    def body(f_r, b_r, x_r, s_r, o_r):
        o_r[...] = jnp.zeros_like(o_r)
        o_r[0:1, :] = x_r[0:1, :]
        o_r[1:2, :] = b_r[2:3, 0:D]
        o_r[2:3, :] = b_r[1:2, 0:D]
        o_r[3:4, :] = b_r[1:2, D:2 * D]
        o_r[4:5, :] = b_r[0:1, 0:D]
        o_r[5:6, :] = f_r[0:1, :]
        o_r[6:9, :] = b_r[3:6, 0:D]
        lane = lax.broadcasted_iota(jnp.int32, (1, D), 1)
        o_r[9:10, :] = jnp.where(lane < 2 * NH, s_r[0:1, :], jnp.where(lane == 2 * NH, f_r[1:2, :], 0.0))

    return pl.pallas_call(
        body, name="pack_small",
        out_shape=jax.ShapeDtypeStruct((SMALL_ROWS, D), F32),
    )(acc_f, acc_b, acc_x, gsum)


def _sum_slots(r, name, tr=64, half=None):
    s, rows, w = r.shape
    tr = min(tr, rows)
    assert rows % tr == 0
    nt = rows // tr

    def body(*refs):
        r_ref, o_ref = refs[-2:]
        tot = r_ref[0].astype(F32)
        for k in range(1, s):
            tot = tot + r_ref[k].astype(F32)
        o_ref[...] = tot

    if half is None:
        return pl.pallas_call(
            body, name=name, grid=(nt,),
            in_specs=[pl.BlockSpec((s, tr, w), lambda i: (0, i, 0))],
            out_specs=pl.BlockSpec((tr, w), lambda i: (i, 0)),
            out_shape=jax.ShapeDtypeStruct((rows, w), F32),
            compiler_params=_cparams("arbitrary"),
        )(r)
    return pl.pallas_call(
        body, name=name,
        grid_spec=pltpu.PrefetchScalarGridSpec(
            num_scalar_prefetch=1, grid=(nt,),
            in_specs=[pl.BlockSpec((s, tr, w), lambda i, hf: (0, i, 0))],
            out_specs=pl.BlockSpec((tr, w), lambda i, hf: (hf[0] * nt + i, 0))),
        out_shape=jax.ShapeDtypeStruct((2 * rows, w), F32),
        compiler_params=_cparams("arbitrary"),
    )(half, r)


def _cast_slabs(a, tr=163):
    rows = a.shape[0]
    assert rows % tr == 0

    def body(a_ref, o_ref):
        o_ref[...] = a_ref[...].astype(MXU)

    spec = pl.BlockSpec((tr,) + a.shape[1:], lambda i: (i, 0, 0))
    return pl.pallas_call(
        body, name="cast_w_in", grid=(rows // tr,), in_specs=[spec], out_specs=spec,
        out_shape=jax.ShapeDtypeStruct(a.shape, MXU), compiler_params=_cparams("arbitrary"),
    )(a)


def _sum_slabs(r, name, tr=69):
    s, rows = r.shape[:2]
    assert rows % tr == 0

    def body(r_ref, o_ref):
        tot = r_ref[0].astype(F32)
        for k in range(1, s):
            tot = tot + r_ref[k].astype(F32)
        o_ref[...] = tot

    return pl.pallas_call(
        body, name=name, grid=(rows // tr,),
        in_specs=[pl.BlockSpec((s, tr) + r.shape[2:], lambda i: (0, i, 0, 0))],
        out_specs=pl.BlockSpec((tr,) + r.shape[2:], lambda i: (i, 0, 0)),
        out_shape=jax.ShapeDtypeStruct(r.shape[1:], F32),
        compiler_params=_cparams("arbitrary"),
    )(r)


def _adamw_body(w_r, g_r, m_r, v_r, d_o, m_o, v_o):
    c1 = 1.0 - ADAM_B1 ** ADAM_STEP
    c2 = 1.0 - ADAM_B2 ** ADAM_STEP
    gv = g_r[...]
    mn = ADAM_B1 * m_r[...] + (1.0 - ADAM_B1) * gv
    vn = ADAM_B2 * v_r[...] + (1.0 - ADAM_B2) * (gv * gv)
    m_o[...] = mn
    v_o[...] = vn
    d_o[...] = -ADAM_LR * ((mn / c1) / (jnp.sqrt(vn / c2) + ADAM_EPS) + ADAM_WD * w_r[...])


def _adamw_slabs(w, g, m, v, name, tr):
    rows = w.shape[0]
    assert rows % tr == 0 and w.shape == g.shape

    def body(*refs):
        _adamw_body(*refs)

    spec = pl.BlockSpec((tr,) + w.shape[1:], lambda i: (i, 0, 0))
    shp = jax.ShapeDtypeStruct(w.shape, F32)
    return pl.pallas_call(
        body, name=name, grid=(rows // tr,),
        in_specs=[spec] * 4, out_specs=[spec] * 3, out_shape=[shp] * 3,
        compiler_params=_cparams("arbitrary"),
    )(w, g, m, v)


def _adamw(w, g, m, v, name, g_row0=0, tr=64):
    lead = w.ndim == 3
    rows, cols = w.shape[-2:]
    tr = min(tr, rows)
    assert rows % tr == 0 and g_row0 % tr == 0 and g.shape[1] == cols

    def body(*refs):
        _adamw_body(*refs)

    if lead:
        spec = pl.BlockSpec((None, tr, cols), lambda i: (0, i, 0))
    else:
        spec = pl.BlockSpec((tr, cols), lambda i: (i, 0))
    g_spec = pl.BlockSpec((tr, cols), lambda i: (g_row0 // tr + i, 0))
    shp = jax.ShapeDtypeStruct(w.shape, F32)
    return pl.pallas_call(
        body, name=name, grid=(rows // tr,),
        in_specs=[spec, g_spec, spec, spec], out_specs=[spec] * 3, out_shape=[shp] * 3,
        compiler_params=_cparams("arbitrary"),
    )(w, g, m, v)


def _allgather_chips(halved, whole):
    arrs = [a for a, _ in halved] + list(whole)
    parts = [p for _, p in halved]
    nh, na = len(halved), len(arrs)
    relations = ((1, 0), (0, 1), (1, 1))
    n_ici = 3 * na

    def body(*refs):
        ins, outs = refs[:na], refs[na:2 * na]
        send_sems, recv_sems = refs[2 * na:]
        x, y, c = _position()
        mine = 2 * x + y

        def rows(a, core, stride, size):
            return pl.ds(_part_start(core, stride, len(ins[a].shape) == 2), size)

        def remote(src, dst, k, to):
            return pltpu.make_async_remote_copy(src_ref=src, dst_ref=dst, send_sem=send_sems.at[k],
                                                recv_sem=recv_sems.at[k], device_id=to, device_id_type=MESH)

        first, passed = [], []
        for j, (rx, ry) in enumerate(relations):
            px, py = _flip(x, rx), _flip(y, ry)
            for a in range(na):
                if a < nh:
                    s1, n1, s2, n2 = parts[a]
                    first.append(remote(ins[a].at[rows(a, c, s1, n1)], outs[a].at[mine, rows(a, c, s1, n1)],
                                        j * na + a, (px, py, c)))
                    landed = outs[a].at[2 * px + py, rows(a, c, s2, n2)]
                    passed.append(remote(landed, landed, n_ici + j * nh + a, (x, y, 1 - c)))
                else:
                    first.append(remote(ins[a], outs[a].at[mine], j * na + a, (px, py, c)))
        for cp in first:
            cp.start()
        k = 0
        for j in range(3):
            for a in range(na):
                first[j * na + a].wait_recv()
                if a < nh:
                    passed[k].start()
                    k += 1
        for cp in passed:
            cp.wait_recv()
        for cp in first + passed:
            cp.wait_send()

    nsem = n_ici + 3 * nh
    outs = pl.pallas_call(
        body, name="allgather_weights",
        in_specs=[ANY] * na, out_specs=[ANY] * na,
        out_shape=[jax.ShapeDtypeStruct((N_CHIPS,) + a.shape, a.dtype) for a in arrs],
        scratch_shapes=[pltpu.SemaphoreType.DMA((nsem,)), pltpu.SemaphoreType.DMA((nsem,))],
    )(*arrs)
    chip = 2 * lax.axis_index("x") + lax.axis_index("y")
    return [lax.dynamic_update_slice(o, a[None], (chip,) + (0,) * a.ndim) for o, a in zip(outs, arrs)]


def _join_halves(piece, full_rest, small):
    def body(pc_ref, rest_in, sm, oin, orest, osm, send_sems, recv_sems, local_sem):
        del rest_in
        x, y, c = _position()
        me = 4 * x + 2 * y + c
        sib = (x, y, 1 - c)
        rows = orest.shape[0] // 2
        mine = orest.at[pl.ds(pl.multiple_of(c * rows, rows), rows)]
        copies = [
            pltpu.make_async_remote_copy(
                src_ref=pc_ref.at[pl.ds(c, HALF_START)], dst_ref=oin.at[pl.ds(c * HALF_SLABS, HALF_START)],
                send_sem=send_sems.at[0], recv_sem=recv_sems.at[0], device_id=sib, device_id_type=MESH),
            pltpu.make_async_remote_copy(
                src_ref=mine, dst_ref=mine, send_sem=send_sems.at[1], recv_sem=recv_sems.at[1],
                device_id=sib, device_id_type=MESH)]
        na = 2
        copies.append(pltpu.make_async_copy(sm, osm.at[me], local_sem))
        for r in range(1, N_DEV):
            to = (_flip(x, (r >> 2) & 1), _flip(y, (r >> 1) & 1), _flip(c, r & 1))
            copies.append(pltpu.make_async_remote_copy(
                src_ref=sm, dst_ref=osm.at[me], send_sem=send_sems.at[na + r - 1],
                recv_sem=recv_sems.at[na + r - 1], device_id=to, device_id_type=MESH))
        for cp in copies:
            cp.start()
        for cp in copies:
            cp.wait()

    nsem = 2 + N_DEV - 1
    return pl.pallas_call(
        body, name="join_halves",
        in_specs=[ANY] * 3, out_specs=[ANY] * 3,
        out_shape=[jax.ShapeDtypeStruct((SLABS_CHIP,) + piece.shape[1:], piece.dtype),
                   jax.ShapeDtypeStruct(full_rest.shape, full_rest.dtype),
                   jax.ShapeDtypeStruct((N_DEV,) + small.shape, small.dtype)],
        scratch_shapes=[pltpu.SemaphoreType.DMA((nsem,)), pltpu.SemaphoreType.DMA((nsem,)),
                        pltpu.SemaphoreType.DMA],
        input_output_aliases={1: 1},
    )(piece, full_rest, small)


def _to_internal(w_slabs):
    order = sorted(SEGMENTS, key=lambda s: s[2])
    main = jnp.concatenate([w_slabs[g0 // 2:(g0 + w) // 2] for g0, w, _ in order], axis=0)
    gate = w_slabs[GATE_COL // 2:GATE_COL // 2 + NH].reshape(2 * NH, D)
    return main.reshape(NMAIN, D), jnp.pad(gate, ((0, GATE_W - 2 * NH), (0, 0)))


def _to_global(main_t, gate_t):
    main = main_t.reshape(NMAIN // 2, 16, D // 8)
    parts = sorted([(g0, main[i0 // 2:(i0 + w) // 2]) for g0, w, i0 in SEGMENTS]
                   + [(GATE_COL, gate_t[0:2 * NH].reshape(NH, 16, D // 8))], key=lambda s: s[0])
    return jnp.concatenate([p for _, p in parts], axis=0)


def _pack_rows(wa, wb, wo, wpg, wp):
    return jnp.concatenate([wa, wb, wo, wpg, wp.reshape(PACK_ROWS[4], D)], axis=0)


def _pad_rows(a, rows=8):
    return jnp.pad(a, ((0, rows - a.shape[0]), (0, 0)))


def kernel(x, p, g_mix, w_in, conv_w, conv_b, w_a_out, b_gates, g_head, w_b_out, w_o, g_ple, w_ple_gate, w_ple, g_final, loss_target, m_g_mix, m_w_in, m_conv_w, m_conv_b, m_w_a_out, m_b_gates, m_g_head, m_w_b_out, m_w_o, m_g_ple, m_w_ple_gate, m_w_ple, m_g_final, v_g_mix, v_w_in, v_conv_w, v_conv_b, v_w_a_out, v_b_gates, v_g_head, v_w_b_out, v_w_o, v_g_ple, v_w_ple_gate, v_w_ple, v_g_final):
    chip = 2 * lax.axis_index("x") + lax.axis_index("y")
    xs, ps, ts = x[0], p[0, 0], loss_target[0]
    g_fin = g_final.reshape(1, D)

    pack_w = _pack_rows(w_a_out[0], w_b_out[0], w_o[0], w_ple_gate[0], w_ple[0])
    w_slabs = _cast_slabs(jnp.transpose(w_in, (2, 0, 1)).reshape(SLABS_CHIP, 16, D // 8))
    half_rows = PACK_TOTAL // 2
    g_win, g_pack, g_cw = _allgather_chips(
        [(w_slabs, (HALF_START, HALF_SLABS, HALF_SLABS, HALF_START)),
         (pack_w.astype(MXU), (half_rows, half_rows, half_rows, half_rows))], [_pad_rows(conv_w[0])])
    w_t, wg_t = _to_internal(g_win.reshape(N_IN // 2, 16, D // 8))
    offs = [0, 256, 768, 1024, 1280, 1344]
    wa, wb, wo, wpg = [g_pack[:, offs[k]:offs[k + 1]].reshape(-1, D) for k in range(4)]
    wp = jnp.transpose(g_pack[:, offs[4]:offs[5]].reshape(N_CHIPS, PLE, PLE), (1, 0, 2)).reshape(PLE, D)
    cw = jnp.transpose(g_cw, (1, 0, 2)).reshape(8, D)

    bias = jnp.pad(b_gates, ((0, 0), (0, GATE_W - 2 * NH)))
    proj, hn, gates = _proj(xs, g_mix, w_t, wg_t)
    h, stats, cs, ns, ms = _mlstm_fwd(proj, gates, bias)
    (a_pre, b_pre, mg, xn1, de, dgp, ya, yb, x1, dx2, acc_f) = _tail_fwd(
        proj, h, xs, ps, ts, cw, conv_b, g_head, g_ple, g_fin, wa, wb, wo, wpg, wp)
    dproj, dcv, dh, dx1, dx1b, dya, dyb, acc_b = _tail_bwd(
        proj, h, dgp, dx2, x1, ya, yb, cw, conv_b, g_head, g_ple, wpg, wo, wb, wa)
    dproj = _conv_bwd(proj, dcv, cw, dproj)
    dproj, dgates, gsum = _mlstm_bwd(proj, gates, bias, h, dh, stats, cs, ns, ms, dproj)
    d_main = _matmul_tn(dproj, hn, "dw_in", out_dtype=WIRE, tk=4096)
    d_gate = _matmul_tn(dgates, hn, "dw_gate", out_dtype=WIRE)
    d_wa = _matmul_tn(a_pre, dya, "dw_a_out", out_dtype=WIRE)
    d_wb = _matmul_tn(b_pre, dyb, "dw_b_out", out_dtype=WIRE)
    d_wo = _matmul_tn(mg, dx1b, "dw_o", out_dtype=WIRE)
    d_wpg = _matmul_tn(xn1, dgp, "dw_ple_gate", out_dtype=WIRE)
    d_wp = _matmul_tn(ps, de, "dw_ple", out_dtype=WIRE)

    g_in = _to_global(d_main, d_gate).reshape(N_CHIPS, SLABS_CHIP, 16, D // 8)
    d_wp_c = jnp.transpose(d_wp.reshape(PLE, N_CHIPS, PLE), (1, 0, 2)).reshape(N_CHIPS, PACK_ROWS[4], D)
    g_rest = jnp.concatenate(
        [d_wa.reshape(N_CHIPS, -1, D), d_wb.reshape(N_CHIPS, -1, D), d_wo.reshape(N_CHIPS, -1, D),
         d_wpg.reshape(N_CHIPS, -1, D), d_wp_c], axis=1)
    grad_x, acc_x, r_in, r_rest = _input_grad(dproj, dgates, w_t, wg_t, xs, dx1, g_mix, g_in, g_rest)
    small = _pack_small(acc_f, acc_b, acc_x, gsum)
    core = lax.axis_index("c").astype(jnp.int32)
    piece = _sum_slabs(r_in, "sum_w_in")
    gw_in, gw_rest, r_small = _join_halves(
        piece, _sum_slots(r_rest, "sum_rest", tr=96, half=core.reshape(1)), small)
    gw_in = lax.dynamic_update_slice(gw_in, piece, (core * HALF_START, 0, 0)).reshape(COLS_CHIP, 8, D // 8)
    gs = _sum_slots(r_small, "sum_small", tr=SMALL_ROWS)

    big = []
    row0 = 0
    for name, w, m, v in (("w_a_out", w_a_out, m_w_a_out, v_w_a_out), ("w_b_out", w_b_out, m_w_b_out, v_w_b_out),
                          ("w_o", w_o, m_w_o, v_w_o), ("w_ple_gate", w_ple_gate, m_w_ple_gate, v_w_ple_gate)):
        big.append((name, w, m, v, gw_rest, row0))
        row0 += w.shape[1]
    g_wp = gw_rest[row0:row0 + PACK_ROWS[4]].reshape(PLE, PLE)
    big.append(("w_ple", w_ple, m_w_ple, v_w_ple, g_wp, 0))
    upd = {name: _adamw(w, g, m, v, "adamw_" + name, g_row0=r0) for name, w, m, v, g, r0 in big}
    g_big = {name: (g if name == "w_ple" else g[r0:r0 + w.shape[1]])[None] for name, w, m, v, g, r0 in big}
    slabs = lambda a: jnp.transpose(a, (2, 0, 1)).reshape(COLS_CHIP, 8, D // 8)
    unslab = lambda a: jnp.transpose(a, (1, 2, 0)).reshape(1, D, COLS_CHIP)
    upd["w_in"] = [unslab(u) for u in _adamw_slabs(
        slabs(w_in), gw_in, slabs(m_w_in), slabs(v_w_in), "adamw_w_in", tr=326)]
    g_big["w_in"] = unslab(gw_in)

    lane = lax.broadcasted_iota(jnp.int32, (1, D), 1)
    g_small = jnp.concatenate([gs[0:6], jnp.where(lane < 2 * NH, gs[9:10], 0.0), jnp.zeros((1, D), F32)], axis=0)

    def small_pack(gm, cb_, bg, gh, gp, gf):
        return jnp.concatenate([gm, cb_, gh.reshape(2, D), gp, gf.reshape(1, D),
                                jnp.pad(bg, ((0, 0), (0, D - 2 * NH))), jnp.zeros((1, D), F32)], axis=0)

    ws = small_pack(g_mix, conv_b, b_gates, g_head, g_ple, g_final)
    ms_ = small_pack(m_g_mix, m_conv_b, m_b_gates, m_g_head, m_g_ple, m_g_final)
    vs = small_pack(v_g_mix, v_conv_b, v_b_gates, v_g_head, v_g_ple, v_g_final)
    upd_s = _adamw(ws, g_small, ms_, vs, "adamw_small")
    g_cw_mine = _pad_rows(lax.dynamic_slice(gs[6:9], (0, chip * PLE), (3, PLE)))
    upd_c = _adamw(_pad_rows(conv_w[0]), g_cw_mine, _pad_rows(m_conv_w[0]), _pad_rows(v_conv_w[0]),
                   "adamw_conv_w")

    def leaves(bigs, sm, cwv):
        return [sm[0:1], bigs["w_in"], cwv[0:3][None], sm[1:2], bigs["w_a_out"], sm[6:7, 0:2 * NH],
                sm[2:4].reshape(1, VD), bigs["w_b_out"], bigs["w_o"], sm[4:5], bigs["w_ple_gate"],
                bigs["w_ple"], sm[5]]

    loss = gs[9, 2 * NH]
    outs = [loss, grad_x[None]] + leaves(g_big, g_small, g_cw_mine)
    for k in range(3):
        outs += leaves({name: u[k] for name, u in upd.items()}, upd_s[k], upd_c[k])
    return tuple(outs)
```

```python
import jax
import jax.numpy as jnp
from jax import lax
from jax.experimental import pallas as pl
from jax.experimental.pallas import tpu as pltpu

F32 = jnp.float32
MXU = jnp.bfloat16
WIRE = jnp.bfloat16

D = 1024
NH, DK, DV = 4, 256, 512
VD = NH * DV
PLE = 256
LCH = 256
EPS = 1e-6
N_IN = 14344
NMAIN = 14336
GATE_W = 128
N_CHIPS, N_DEV = 4, 8

C_BA, C_ZA, C_O, C_ZB, C_GA, C_GB = 0, 1024, 2048, 4096, 6144, 7168
QK = NH * DK
C_Q, C_K, C_V, C_XA, C_CA = 8192, 9216, 10240, 12288, 13312
QKV_W = 2 * QK + VD
TAIL_W = 8192
CONV_W = 2048
SEGMENTS = (
    (0, 1024, C_XA), (1024, 1024, C_BA), (2048, 1024, C_CA), (3072, 1024, C_ZA),
    (4096, QKV_W, C_Q), (8192, 4096, C_O), (12296, 2048, C_GA),
)
GATE_COL = 12288

COLS_CHIP = N_IN // N_CHIPS
SLABS_CHIP = COLS_CHIP // 2
HALF_START = SLABS_CHIP // 2
HALF_SLABS = SLABS_CHIP - HALF_START

PACK_ROWS = (256, 512, 256, 256, 64)
PACK_TOTAL = sum(PACK_ROWS)
SMALL_ROWS = 16

ADAM_LR, ADAM_B1, ADAM_B2, ADAM_EPS, ADAM_WD, ADAM_STEP = 0.001, 0.9, 0.999, 1e-08, 0.01, 10

VMEM_LIMIT = 56 * 1024 * 1024
MESH = pl.DeviceIdType.MESH
ANY = pl.BlockSpec(memory_space=pl.ANY)


def _cparams(*sem):
    return pltpu.CompilerParams(dimension_semantics=sem, vmem_limit_bytes=VMEM_LIMIT)


def _dot(a, b):
    return jnp.dot(a, b, preferred_element_type=F32)


def _dot_nt(a, b):
    return lax.dot_general(a, b, (((1,), (1,)), ((), ())), preferred_element_type=F32)


def _dot_tn(a, b):
    return lax.dot_general(a, b, (((0,), (0,)), ((), ())), preferred_element_type=F32)


def _sigmoid(x):
    return 1.0 / (1.0 + jnp.exp(-x))


def _logsig(x):
    return jnp.minimum(x, 0.0) - jnp.log(1.0 + jnp.exp(-jnp.abs(x)))


GATE_FLOOR = -80.0


def _gate(v):
    e = jnp.exp(-jnp.maximum(v, GATE_FLOOR))
    s = 1.0 / (1.0 + e)
    return s, e * s * s


def _rstd(x):
    return lax.rsqrt(jnp.mean(x * x, axis=-1, keepdims=True) + EPS)


def _norm_bwd(dy, xhat, r, g):
    dxh = dy * g
    return r * (dxh - xhat * jnp.mean(dxh * xhat, axis=-1, keepdims=True))


def _f32(ref):
    return ref[...].astype(F32)


def _proj(x, g_mix, w_t, wg_t, tm=1024, tn=2048):
    n = x.shape[0]
    m = w_t.shape[0]
    tm = min(tm, n)

    def body(x_ref, g_ref, b_ref, wg_ref, o_ref, hn_ref, gate_ref):
        @pl.when(pl.program_id(1) == 0)
        def _():
            xv = x_ref[...]
            hn = (xv * _rstd(xv) * g_ref[...]).astype(MXU)
            hn_ref[...] = hn
            gate_ref[...] = _dot_nt(hn, wg_ref[...])

        o_ref[...] = _dot_nt(hn_ref[...], b_ref[...]).astype(MXU)

    return pl.pallas_call(
        body, name="proj", grid=(n // tm, m // tn),
        in_specs=[pl.BlockSpec((tm, D), lambda i, j: (i, 0)),
                  pl.BlockSpec((1, D), lambda i, j: (0, 0)),
                  pl.BlockSpec((tn, D), lambda i, j: (j, 0)),
                  pl.BlockSpec((GATE_W, D), lambda i, j: (0, 0))],
        out_specs=[pl.BlockSpec((tm, tn), lambda i, j: (i, j)),
                   pl.BlockSpec((tm, D), lambda i, j: (i, 0)),
                   pl.BlockSpec((tm, GATE_W), lambda i, j: (i, 0))],
        out_shape=[jax.ShapeDtypeStruct((n, m), MXU), jax.ShapeDtypeStruct((n, D), MXU),
                   jax.ShapeDtypeStruct((n, GATE_W), F32)],
        compiler_params=_cparams("arbitrary", "arbitrary"),
    )(x, g_mix, w_t, wg_t)


def _matmul_tn(a, b, name, out_dtype=F32, ta=1024, tb=1024, tk=2048):
    n, ka = a.shape
    kb = b.shape[1]
    ta, tb, tk = min(ta, ka), min(tb, kb), min(tk, n)
    nk = n // tk

    def body(a_ref, b_ref, o_ref, acc):
        kk = pl.program_id(2)

        @pl.when(kk == 0)
        def _():
            acc[...] = jnp.zeros_like(acc)

        acc[...] += _dot_tn(a_ref[...].astype(MXU), b_ref[...].astype(MXU))

        @pl.when(kk == nk - 1)
        def _():
            o_ref[...] = acc[...].astype(out_dtype)

    return pl.pallas_call(
        body, name=name, grid=(ka // ta, kb // tb, nk),
        in_specs=[pl.BlockSpec((tk, ta), lambda i, j, kk: (kk, i)),
                  pl.BlockSpec((tk, tb), lambda i, j, kk: (kk, j))],
        out_specs=pl.BlockSpec((ta, tb), lambda i, j, kk: (i, j)),
        out_shape=jax.ShapeDtypeStruct((ka, kb), out_dtype),
        scratch_shapes=[pltpu.VMEM((ta, tb), F32)],
        compiler_params=_cparams("arbitrary", "arbitrary", "arbitrary"),
    )(a, b)


def _sub_row(block_t, j):
    sub = lax.broadcasted_iota(jnp.int32, block_t.shape, 0)
    return jnp.sum(jnp.where(sub == j, block_t, 0.0), axis=0, keepdims=True)


def _chunk_cumsum(g):
    n = g.shape[0]
    r = lax.broadcasted_iota(jnp.int32, (n, n), 0)
    c = lax.broadcasted_iota(jnp.int32, (n, n), 1)
    return jnp.dot(jnp.where(r >= c, 1.0, 0.0), _logsig(g), precision=lax.Precision.HIGHEST,
                   preferred_element_type=F32)


def _chunk_decay(li_col, li_row, b_col, b_row, m_prev):
    n = li_col.shape[0]
    r = lax.broadcasted_iota(jnp.int32, (n, n), 0)
    c = lax.broadcasted_iota(jnp.int32, (n, n), 1)
    tri = r >= c
    b_last = b_col[n - 1:n, :]
    dmat = jnp.where(tri, b_col - b_row + li_row, -jnp.inf)
    a_col = b_col + m_prev
    g_col = b_last - b_col + li_col
    m_new = jnp.maximum(b_last + m_prev, jnp.max(g_col, axis=0, keepdims=True))
    w_col = jnp.exp(g_col - m_new)
    decay = jnp.exp(b_last + m_prev - m_new)
    return tri, dmat, a_col, m_new, w_col, decay


def _qkv_specs(row_of):
    q_spec = pl.BlockSpec((LCH, QK), lambda c: (row_of(c), C_Q // QK))
    k_spec = pl.BlockSpec((LCH, QK), lambda c: (row_of(c), C_K // QK))
    v_spec = pl.BlockSpec((LCH, VD), lambda c: (row_of(c), C_V // VD))
    return q_spec, k_spec, v_spec


def _state_specs(row_of):
    return [pl.BlockSpec((NH, None, DK, DV), lambda c: (0, row_of(c), 0, 0)),
            pl.BlockSpec((NH, None, 1, DK), lambda c: (0, row_of(c), 0, 0)),
            pl.BlockSpec((NH, None, 1, GATE_W), lambda c: (0, row_of(c), 0, 0))]


def _lane_put(col, lane_id, width=GATE_W):
    lane = lax.broadcasted_iota(jnp.int32, (col.shape[0], width), 1)
    return jnp.where(lane == lane_id, col, 0.0)


def _lane_get(block, lane_id):
    lane = lax.broadcasted_iota(jnp.int32, block.shape, 1)
    return jnp.sum(jnp.where(lane == lane_id, block, 0.0), axis=1, keepdims=True)


def _mlstm_fwd(proj, gates, bias):
    n = proj.shape[0]
    nc = n // LCH

    def body(q_ref, k_ref, v_ref, g_ref, bias_ref,
             h_ref, st_ref, cs_ref, ns_ref, ms_ref, c_scr, n_scr, m_scr):
        @pl.when(pl.program_id(0) == 0)
        def _():
            c_scr[...] = jnp.zeros_like(c_scr)
            n_scr[...] = jnp.zeros_like(n_scr)
            m_scr[...] = jnp.full_like(m_scr, -jnp.inf)

        g = g_ref[...] + bias_ref[...]
        gt = g.T[0:8, :]
        b = _chunk_cumsum(g)
        bt = b.T[0:8, :]
        stats = jnp.zeros((LCH, GATE_W), F32)
        for hd in range(NH):
            m_all = m_scr[hd]
            m_prev = m_all[0:1, 0:1]
            tri, dmat, a_col, m_new, w_col, decay = _chunk_decay(
                _lane_get(g, hd), _sub_row(gt, hd), _lane_get(b, NH + hd), _sub_row(bt, NH + hd), m_prev)
            m_col = jnp.maximum(a_col, jnp.max(dmat, axis=1, keepdims=True))
            dl = jnp.exp(dmat - m_col)
            inter = jnp.exp(a_col - m_col)

            qs = q_ref[:, hd * DK:(hd + 1) * DK] * (DK ** -0.5)
            kk = k_ref[:, hd * DK:(hd + 1) * DK]
            vv = v_ref[:, hd * DV:(hd + 1) * DV]
            cst = c_scr[hd]
            nst = n_scr[hd]
            cs_ref[hd] = cst
            ns_ref[hd] = nst
            ms_ref[hd] = m_all

            sc = _dot_nt(qs, kk) * dl
            num = _dot(sc.astype(MXU), vv) + inter * _dot(qs, cst.astype(MXU))
            den = (jnp.sum(sc, axis=1, keepdims=True)
                   + inter * jnp.sum(qs.astype(F32) * nst, axis=1, keepdims=True))
            nrm = jnp.maximum(jnp.abs(den), jnp.exp(-m_col))
            h_ref[:, hd * DV:(hd + 1) * DV] = (num * (1.0 / nrm)).astype(h_ref.dtype)
            stats = stats + _lane_put(den, hd) + _lane_put(m_col, NH + hd)

            kw = kk.astype(F32) * w_col
            c_scr[hd] = decay * cst + _dot_tn(kw.astype(MXU), vv)
            n_scr[hd] = decay * nst + jnp.sum(kw, axis=0, keepdims=True)
            m_scr[hd] = jnp.broadcast_to(m_new, (1, GATE_W))
        st_ref[...] = stats

    q_spec, k_spec, v_spec = _qkv_specs(lambda c: c)
    return pl.pallas_call(
        body, name="mlstm_fwd", grid=(nc,),
        in_specs=[q_spec, k_spec, v_spec,
                  pl.BlockSpec((LCH, GATE_W), lambda c: (c, 0)),
                  pl.BlockSpec((1, GATE_W), lambda c: (0, 0))],
        out_specs=[pl.BlockSpec((LCH, VD), lambda c: (c, 0)),
                   pl.BlockSpec((LCH, GATE_W), lambda c: (c, 0))] + _state_specs(lambda c: c),
        out_shape=[jax.ShapeDtypeStruct((n, VD), MXU),
                   jax.ShapeDtypeStruct((n, GATE_W), F32),
                   jax.ShapeDtypeStruct((NH, nc, DK, DV), F32),
                   jax.ShapeDtypeStruct((NH, nc, 1, DK), F32),
                   jax.ShapeDtypeStruct((NH, nc, 1, GATE_W), F32)],
        scratch_shapes=[pltpu.VMEM((NH, DK, DV), F32), pltpu.VMEM((NH, 1, DK), F32),
                        pltpu.VMEM((NH, 1, GATE_W), F32)],
        compiler_params=_cparams("arbitrary"),
    )(proj, proj, proj, gates, bias)


def _mlstm_bwd(proj, gates, bias, h, dh, stats, cs, ns, ms, dproj):
    n = proj.shape[0]
    nc = n // LCH

    def body(q_ref, k_ref, v_ref, g_ref, bias_ref, h_ref, dh_ref, st_ref,
             cs_ref, ns_ref, ms_ref, dproj_in,
             dqkv_ref, dg_ref, gsum_ref, dc_scr, dn_scr):
        del dproj_in

        @pl.when(pl.program_id(0) == 0)
        def _():
            dc_scr[...] = jnp.zeros_like(dc_scr)
            dn_scr[...] = jnp.zeros_like(dn_scr)
            gsum_ref[...] = jnp.zeros_like(gsum_ref)

        g = g_ref[...] + bias_ref[...]
        gt = g.T[0:8, :]
        b = _chunk_cumsum(g)
        bt = b.T[0:8, :]
        stats = st_ref[...]
        r = lax.broadcasted_iota(jnp.int32, (LCH, LCH), 0)
        c = lax.broadcasted_iota(jnp.int32, (LCH, LCH), 1)
        eye = r == c
        to_col = lambda row: jnp.sum(jnp.where(eye, row, 0.0), axis=1, keepdims=True)
        last = lax.broadcasted_iota(jnp.int32, (LCH, 1), 0) == LCH - 1
        dg = jnp.zeros((LCH, GATE_W), F32)
        for hd in range(NH):
            fr_col = _lane_get(g, NH + hd)
            m_prev = ms_ref[hd][0:1, 0:1]
            tri, dmat, a_col, m_new, w_col, decay = _chunk_decay(
                _lane_get(g, hd), _sub_row(gt, hd), _lane_get(b, NH + hd), _sub_row(bt, NH + hd), m_prev)
            m_col = _lane_get(stats, NH + hd)
            dl = jnp.exp(dmat - m_col)
            inter = jnp.exp(a_col - m_col)

            qs = q_ref[:, hd * DK:(hd + 1) * DK] * (DK ** -0.5)
            kk = k_ref[:, hd * DK:(hd + 1) * DK]
            vv = v_ref[:, hd * DV:(hd + 1) * DV]
            qf = qs.astype(F32)
            kf = kk.astype(F32)
            cst = cs_ref[hd]
            nst = ns_ref[hd]
            cb = cst.astype(MXU)
            dcn = dc_scr[hd]
            dnn = dn_scr[hd]
            dcb = dcn.astype(MXU)

            den = _lane_get(stats, hd)
            floor = jnp.exp(-m_col)
            nrm = jnp.maximum(jnp.abs(den), floor)
            dhv = dh_ref[:, hd * DV:(hd + 1) * DV].astype(F32)
            rn = 1.0 / nrm
            dnum = dhv * rn
            dnum_b = dnum.astype(MXU)
            dhh = jnp.sum(dhv * h_ref[:, hd * DV:(hd + 1) * DV].astype(F32), axis=1, keepdims=True)
            dden = jnp.where(jnp.abs(den) > floor, -dhh * rn * jnp.sign(den), 0.0)

            sc = _dot_nt(qs, kk) * dl
            dsc = _dot_nt(dnum_b, vv) + dden
            da = (dl * dsc).astype(MXU)
            gmat = sc * dsc

            dq = _dot(da, kk) + inter * (_dot_nt(dnum_b, cb) + dden * nst)
            dk_state = w_col * (_dot_nt(vv, dcb) + dnn)
            dk = _dot_tn(da, qs) + dk_state
            kw = (kf * w_col).astype(MXU)
            dv = _dot_tn(sc.astype(MXU), dnum_b) + _dot(kw, dcb)
            dqkv_ref[:, hd * DK:(hd + 1) * DK] = (dq * (DK ** -0.5)).astype(dqkv_ref.dtype)
            dqkv_ref[:, QK + hd * DK:QK + (hd + 1) * DK] = dk.astype(dqkv_ref.dtype)
            dqkv_ref[:, 2 * QK + hd * DV:2 * QK + (hd + 1) * DV] = dv.astype(dqkv_ref.dtype)

            num_i = _dot(qs, cb)
            den_i = jnp.sum(qf * nst, axis=1, keepdims=True)
            e_col = inter * (jnp.sum(dnum * num_i, axis=1, keepdims=True) + dden * den_i)
            h_col = jnp.sum(kf * dk_state, axis=1, keepdims=True)
            f_dec = decay * (jnp.sum(jnp.sum(cst * dcn, axis=1, keepdims=True), axis=0, keepdims=True)
                             + jnp.sum(nst * dnn, axis=1, keepdims=True))
            row_g = jnp.sum(gmat, axis=1, keepdims=True)
            col_g = to_col(jnp.sum(gmat, axis=0, keepdims=True))
            db_col = row_g - col_g + e_col - h_col
            db_col = db_col + jnp.where(last, jnp.sum(h_col, axis=0, keepdims=True) + f_dec, 0.0)
            dli_col = col_g + h_col
            dlf_row = jnp.sum(jnp.where(tri, db_col, 0.0), axis=0, keepdims=True)
            df_col = to_col(dlf_row) * _sigmoid(-fr_col)
            dg = dg + _lane_put(dli_col, hd) + _lane_put(df_col, NH + hd)

            dc_scr[hd] = decay * dcn + _dot_tn((qf * inter).astype(MXU), dnum_b)
            dn_scr[hd] = decay * dnn + jnp.sum(qf * (inter * dden), axis=0, keepdims=True)
        dg_ref[...] = dg
        gsum_ref[0:1, 0:GATE_W] += jnp.sum(dg, axis=0, keepdims=True)

    rev = lambda c: nc - 1 - c
    q_spec, k_spec, v_spec = _qkv_specs(rev)
    hv_spec = pl.BlockSpec((LCH, VD), lambda c: (rev(c), 0))
    gate_spec = pl.BlockSpec((LCH, GATE_W), lambda c: (rev(c), 0))
    return pl.pallas_call(
        body, name="mlstm_bwd", grid=(nc,),
        in_specs=[q_spec, k_spec, v_spec, gate_spec,
                  pl.BlockSpec((1, GATE_W), lambda c: (0, 0)),
                  hv_spec, hv_spec, gate_spec] + _state_specs(rev) + [ANY],
        out_specs=[pl.BlockSpec((LCH, QKV_W), lambda c: (rev(c), C_Q // QKV_W)),
                   gate_spec,
                   pl.BlockSpec((8, D), lambda c: (0, 0))],
        out_shape=[jax.ShapeDtypeStruct(dproj.shape, dproj.dtype),
                   jax.ShapeDtypeStruct((n, GATE_W), F32),
                   jax.ShapeDtypeStruct((8, D), F32)],
        scratch_shapes=[pltpu.VMEM((NH, DK, DV), F32), pltpu.VMEM((NH, 1, DK), F32)],
        input_output_aliases={11: 0},
        compiler_params=_cparams("arbitrary"),
    )(proj, proj, proj, gates, bias, h, dh, stats, cs, ns, ms, dproj)


def _proj_spec(tm, col, width):
    return pl.BlockSpec((tm, width), lambda i: (i, col // width))


def _tail_in_specs(tm):
    return [_proj_spec(tm, 0, TAIL_W), _proj_spec(tm, C_XA, CONV_W),
            pl.BlockSpec((8, CONV_W), lambda i: (jnp.maximum(i * (tm // 8) - 1, 0), C_XA // CONV_W))]


def _tail_views(tail_r, conv_r, halo_r):
    cols = lambda ref, c0, w: ref.at[:, pl.ds(c0, w)]
    return (cols(tail_r, C_BA, D), cols(tail_r, C_ZA, D), cols(tail_r, C_O, VD), cols(tail_r, C_ZB, VD),
            cols(tail_r, C_GA, D), cols(tail_r, C_GB, D), cols(conv_r, 0, D), cols(conv_r, D, D),
            cols(halo_r, 0, D), cols(halo_r, D, D))


def _const(shape):
    return pl.BlockSpec(shape, lambda i: (0,) * len(shape))


def _conv_inputs(i, tm, xa_ref, ca_ref, xah_ref, cah_ref):
    u = _f32(xa_ref) * _f32(ca_ref)
    uh = jnp.where(i > 0, _f32(xah_ref) * _f32(cah_ref), 0.0)
    rid = lax.broadcasted_iota(jnp.int32, u.shape, 0)
    u1 = jnp.where(rid == 0, uh[7:8, :], pltpu.roll(u, 1, 0))
    u2 = jnp.where(rid == 0, uh[6:7, :], jnp.where(rid == 1, uh[7:8, :], pltpu.roll(u, 2, 0)))
    return u, u1, u2


def _head_norm(hh, gh):
    out = []
    for j in range(NH):
        hj = hh[:, j * DV:(j + 1) * DV]
        rj = _rstd(hj)
        out.append((hj * rj, rj, gh[:, j * DV:(j + 1) * DV]))
    return out


def _tail_fwd(proj, h, x, p, t, cw, cb, gh, gple, gfin, wa, wb, wo, wpg, wp):
    n = x.shape[0]
    tm = min(256, n)

    def body(tail_r, conv_r, halo_r, h_r, x_r, p_r, t_r,
             cw_r, cb_r, gh_r, gple_r, gfin_r, wa_r, wb_r, wo_r, wpg_r, wp_r,
             apre_o, bpre_o, mg_o, xn1_o, de_o, dgp_o, ya_o, yb_o, x1_o, dx2_o, acc_o):
        ba_r, za_r, o_r, zb_r, ga_r, gb_r, xa_r, ca_r, xah_r, cah_r = _tail_views(tail_r, conv_r, halo_r)
        i = pl.program_id(0)

        @pl.when(i == 0)
        def _():
            acc_o[...] = jnp.zeros_like(acc_o)

        u, u1, u2 = _conv_inputs(i, tm, xa_r, ca_r, xah_r, cah_r)
        cwv = cw_r[...]
        cv = cwv[0:1, :] * u2 + cwv[1:2, :] * u1 + cwv[2:3, :] * u + cb_r[...]
        za = za_r[...]
        a_pre = ba_r[...] * cv.astype(MXU) * (za * _sigmoid(za))
        apre_o[...] = a_pre
        ya = _dot(a_pre, wa_r[...]).astype(MXU)

        hn = _head_norm(_f32(h_r), gh_r[...])
        hbn = jnp.concatenate([xh * g for xh, _, g in hn], axis=1)
        zb = zb_r[...]
        b_pre = _sigmoid(o_r[...]) * hbn.astype(MXU) * (zb * _sigmoid(zb))
        bpre_o[...] = b_pre
        yb = _dot(b_pre, wb_r[...]).astype(MXU)
        ya_o[...] = ya
        yb_o[...] = yb

        mg = _sigmoid(ga_r[...]) * ya + _sigmoid(gb_r[...]) * yb
        mg_o[...] = mg
        x1 = x_r[...] + _dot(mg, wo_r[...])
        x1_o[...] = x1.astype(MXU)
        xn1 = (x1 * _rstd(x1) * gple_r[...]).astype(MXU)
        xn1_o[...] = xn1
        gt = _sigmoid(_dot(xn1, wpg_r[...]))
        e = _dot(p_r[...].astype(MXU), wp_r[...])
        x2 = x1 + gt * e
        r2 = _rstd(x2)
        xh2 = x2 * r2
        gf = gfin_r[...]
        diff = xh2 * gf - t_r[...]
        dy = diff * (1.0 / D)
        dx2 = _norm_bwd(dy, xh2, r2, gf)
        dx2_o[...] = dx2
        de_o[...] = (dx2 * gt).astype(MXU)
        dgp_o[...] = (dx2 * e * gt * (1.0 - gt)).astype(MXU)
        acc_o[0:1, :] += jnp.sum(dy * xh2, axis=0, keepdims=True)
        loss = 0.5 * jnp.sum(jnp.sum(diff * diff, axis=1, keepdims=True), axis=0, keepdims=True) * (1.0 / D)
        acc_o[1:2, :] += jnp.broadcast_to(loss, (1, D))

    row = lambda w, dt: (pl.BlockSpec((tm, w), lambda i: (i, 0)), jax.ShapeDtypeStruct((n, w), dt))
    outs = [row(D, MXU), row(VD, MXU), row(D, MXU), row(D, MXU), row(D, MXU), row(D, MXU),
            row(D, MXU), row(D, MXU), row(D, MXU), row(D, F32),
            (_const((8, D)), jax.ShapeDtypeStruct((8, D), F32))]
    return pl.pallas_call(
        body, name="tail_fwd", grid=(n // tm,),
        in_specs=_tail_in_specs(tm) + [
                  pl.BlockSpec((tm, VD), lambda i: (i, 0)),
                  pl.BlockSpec((tm, D), lambda i: (i, 0)),
                  pl.BlockSpec((tm, PLE), lambda i: (i, 0)),
                  pl.BlockSpec((tm, D), lambda i: (i, 0)),
                  _const((8, D)), _const((1, D)), _const((1, VD)), _const((1, D)), _const((1, D)),
                  _const((D, D)), _const((VD, D)), _const((D, D)), _const((D, D)), _const((PLE, D))],
        out_specs=[s for s, _ in outs],
        out_shape=[s for _, s in outs],
        compiler_params=_cparams("arbitrary"),
    )(proj, proj, proj, h, x, p, t, cw, cb, gh, gple, gfin, wa, wb, wo, wpg, wp)


def _tail_bwd(proj, h, dgp, dx2, x1, ya, yb, cw, cb, gh, gple, wpg, wo, wb, wa):
    n = x1.shape[0]
    tm = min(256, n)

    def body(tail_r, conv_r, halo_r, h_r,
             dgp_r, dx2_r, x1_r, ya_r, yb_r, cw_r, cb_r, gh_r, gple_r,
             wpg_r, wo_r, wb_r, wa_r,
             dproj_o, dcv_o, dh_o, dx1_o, dx1b_o, dya_o, dyb_o, acc_o):
        ba_r, za_r, o_r, zb_r, ga_r, gb_r, xa_r, ca_r, xah_r, cah_r = _tail_views(tail_r, conv_r, halo_r)
        i = pl.program_id(0)

        @pl.when(i == 0)
        def _():
            acc_o[...] = jnp.zeros_like(acc_o)

        dxn1 = _dot_nt(dgp_r[...], wpg_r[...])
        x1 = _f32(x1_r)
        r1 = _rstd(x1)
        xh1 = x1 * r1
        acc_o[0:1, 0:D] += jnp.sum(dxn1 * xh1, axis=0, keepdims=True)
        dx1 = dx2_r[...] + _norm_bwd(dxn1, xh1, r1, gple_r[...])
        dx1_o[...] = dx1
        dx1b = dx1.astype(MXU)
        dx1b_o[...] = dx1b

        dmg = _dot_nt(dx1b, wo_r[...]).astype(MXU)
        sga, dsga = _gate(ga_r[...])
        sgb, dsgb = _gate(gb_r[...])
        dya = dmg * sga
        dyb = dmg * sgb
        dya_o[...] = dya
        dyb_o[...] = dyb
        dproj_o[:, C_GA:C_GA + D] = dmg * ya_r[...] * dsga
        dproj_o[:, C_GB:C_GB + D] = dmg * yb_r[...] * dsgb

        db_pre = _dot_nt(dyb, wb_r[...]).astype(MXU)
        hn = _head_norm(_f32(h_r), gh_r[...])
        hbn = jnp.concatenate([xh * g for xh, _, g in hn], axis=1).astype(MXU)
        so, dso = _gate(o_r[...])
        zb = zb_r[...]
        szb, dszb = _gate(zb)
        sb = zb * szb
        t1 = db_pre * hbn
        dproj_o[:, C_O:C_O + VD] = t1 * (sb * dso)
        dproj_o[:, C_ZB:C_ZB + VD] = t1 * (so * (szb + zb * dszb))
        dhbn = (db_pre * (so * sb)).astype(F32)
        for j, (xh, rj, g) in enumerate(hn):
            dj = dhbn[:, j * DV:(j + 1) * DV]
            acc_o[1:2, j * DV:(j + 1) * DV] += jnp.sum(dj * xh, axis=0, keepdims=True)
            dh_o[:, j * DV:(j + 1) * DV] = _norm_bwd(dj, xh, rj, g).astype(MXU)

        da_pre = _dot_nt(dya, wa_r[...]).astype(MXU)
        u, u1, u2 = _conv_inputs(i, tm, xa_r, ca_r, xah_r, cah_r)
        cwv = cw_r[...]
        cv = cwv[0:1, :] * u2 + cwv[1:2, :] * u1 + cwv[2:3, :] * u + cb_r[...]
        za = za_r[...]
        sza, dsza = _gate(za)
        sa = za * sza
        ba = ba_r[...]
        t2 = da_pre * cv.astype(MXU)
        dproj_o[:, C_BA:C_BA + D] = t2 * sa
        dproj_o[:, C_ZA:C_ZA + D] = t2 * (ba * (sza + za * dsza))
        dcv_b = da_pre * (ba * sa)
        dcv_o[...] = dcv_b
        dcv = dcv_b.astype(F32)
        acc_o[2:3, 0:D] += jnp.sum(dcv, axis=0, keepdims=True)
        acc_o[3:4, 0:D] += jnp.sum(dcv * u2, axis=0, keepdims=True)
        acc_o[4:5, 0:D] += jnp.sum(dcv * u1, axis=0, keepdims=True)
        acc_o[5:6, 0:D] += jnp.sum(dcv * u, axis=0, keepdims=True)

    row = lambda w, dt: (pl.BlockSpec((tm, w), lambda i: (i, 0)), jax.ShapeDtypeStruct((n, w), dt))
    outs = [(pl.BlockSpec((tm, TAIL_W), lambda i: (i, 0)), jax.ShapeDtypeStruct((n, NMAIN), MXU)),
            row(D, MXU), row(VD, MXU), row(D, F32), row(D, MXU), row(D, MXU), row(D, MXU),
            (_const((8, VD)), jax.ShapeDtypeStruct((8, VD), F32))]
    rowin = lambda w: pl.BlockSpec((tm, w), lambda i: (i, 0))
    return pl.pallas_call(
        body, name="tail_bwd", grid=(n // tm,),
        in_specs=_tail_in_specs(tm) + [
                  rowin(VD), rowin(D), rowin(D), rowin(D), rowin(D), rowin(D),
                  _const((8, D)), _const((1, D)), _const((1, VD)), _const((1, D)),
                  _const((D, D)), _const((D, D)), _const((VD, D)), _const((D, D))],
        out_specs=[s for s, _ in outs],
        out_shape=[s for _, s in outs],
        compiler_params=_cparams("arbitrary"),
    )(proj, proj, proj, h, dgp, dx2, x1, ya, yb, cw, cb, gh, gple, wpg, wo, wb, wa)


def _conv_bwd(proj, dcv, cw, dproj):
    n = dcv.shape[0]
    tm = min(512, n)
    nt = n // tm

    def body(xa_r, ca_r, dcv_r, nxt_r, cw_r, dproj_in, dxc_o):
        del dproj_in
        i = pl.program_id(0)
        dcv_v = _f32(dcv_r)
        nxt = jnp.where(i < nt - 1, _f32(nxt_r), 0.0)
        rid = lax.broadcasted_iota(jnp.int32, dcv_v.shape, 0)
        d1 = jnp.where(rid == tm - 1, nxt[0:1, :], pltpu.roll(dcv_v, tm - 1, 0))
        d2 = jnp.where(rid == tm - 2, nxt[0:1, :],
                       jnp.where(rid == tm - 1, nxt[1:2, :], pltpu.roll(dcv_v, tm - 2, 0)))
        cwv = cw_r[...]
        du = cwv[2:3, :] * dcv_v + cwv[1:2, :] * d1 + cwv[0:1, :] * d2
        dxc_o[:, 0:D] = (du * _f32(ca_r)).astype(MXU)
        dxc_o[:, D:2 * D] = (du * _f32(xa_r)).astype(MXU)

    return pl.pallas_call(
        body, name="conv_bwd", grid=(nt,),
        in_specs=[_proj_spec(tm, C_XA, 1024), _proj_spec(tm, C_CA, 1024),
                  pl.BlockSpec((tm, D), lambda i: (i, 0)),
                  pl.BlockSpec((8, D), lambda i: (jnp.minimum((i + 1) * (tm // 8), n // 8 - 1), 0)),
                  _const((8, D)), ANY],
        out_specs=pl.BlockSpec((tm, CONV_W), lambda i: (i, C_XA // CONV_W)),
        out_shape=jax.ShapeDtypeStruct(dproj.shape, dproj.dtype),
        input_output_aliases={5: 0},
        compiler_params=_cparams("arbitrary"),
    )(proj, proj, dcv, dcv, cw, dproj)


def _position():
    return lax.axis_index("x"), lax.axis_index("y"), lax.axis_index("c")


def _flip(v, bit):
    return 1 - v if bit else v


def _part_start(core, stride, tiled):
    return pl.multiple_of(core * stride, stride) if tiled else core * stride


def _scatter_copies(srcs, dsts, strides, send_sems, recv_sems, local_sems):
    x, y, c = _position()
    me = 4 * x + 2 * y + c
    na = len(srcs)
    copies = []
    for r in range(N_DEV):
        px, py, pc = _flip(x, (r >> 2) & 1), _flip(y, (r >> 1) & 1), _flip(c, r & 1)
        for a in range(na):
            rows = dsts[a].shape[1]
            src = srcs[a].at[2 * px + py, pl.ds(_part_start(pc, strides[a], len(dsts[a].shape) == 3), rows)]
            dst = dsts[a].at[me]
            if r == 0:
                copies.append(pltpu.make_async_copy(src, dst, local_sems.at[a]))
            else:
                k = (r - 1) * na + a
                copies.append(pltpu.make_async_remote_copy(
                    src_ref=src, dst_ref=dst, send_sem=send_sems.at[k], recv_sem=recv_sems.at[k],
                    device_id=(px, py, pc), device_id_type=MESH))
    return copies


def _input_grad(dproj, dgates, w_t, wg_t, x, dx1, g_mix, g_in, g_rest):
    n = x.shape[0]
    tm, tk = min(1024, n), 2048
    nk = NMAIN // tk
    nt = n // tm

    def body(dp_r, dg_r, w_r, wg_r, x_r, dx1_r, g_r, gin, grest,
             gx_o, acc_o, oin, orest, acc, send_sems, recv_sems, local_sems):
        i = pl.program_id(0)
        kk = pl.program_id(1)
        copies = _scatter_copies((gin, grest), (oin, orest), strides, send_sems, recv_sems, local_sems)

        @pl.when((i == 0) & (kk == 0))
        def _():
            acc_o[...] = jnp.zeros_like(acc_o)
            for cp in copies:
                cp.start()

        @pl.when(kk == 0)
        def _():
            acc[...] = _dot(dg_r[...].astype(MXU), wg_r[...])

        acc[...] += _dot(dp_r[...], w_r[...])

        @pl.when(kk == nk - 1)
        def _():
            dhn = acc[...]
            xv = x_r[...]
            r0 = _rstd(xv)
            xh = xv * r0
            acc_o[0:1, :] += jnp.sum(dhn * xh, axis=0, keepdims=True)
            gx_o[...] = dx1_r[...] + _norm_bwd(dhn, xh, r0, g_r[...])

        @pl.when((i == nt - 1) & (kk == nk - 1))
        def _():
            for cp in copies:
                cp.wait()

    nrem = 2 * (N_DEV - 1)
    r_in, r_rest = HALF_SLABS, g_rest.shape[1] // 2
    strides = (HALF_START, r_rest)
    return pl.pallas_call(
        body, name="input_grad", grid=(nt, nk),
        in_specs=[pl.BlockSpec((tm, tk), lambda i, kk: (i, kk)),
                  pl.BlockSpec((tm, GATE_W), lambda i, kk: (i, 0)),
                  pl.BlockSpec((tk, D), lambda i, kk: (kk, 0)),
                  pl.BlockSpec((GATE_W, D), lambda i, kk: (0, 0)),
                  pl.BlockSpec((tm, D), lambda i, kk: (i, 0)),
                  pl.BlockSpec((tm, D), lambda i, kk: (i, 0)),
                  pl.BlockSpec((1, D), lambda i, kk: (0, 0)),
                  ANY, ANY],
        out_specs=[pl.BlockSpec((tm, D), lambda i, kk: (i, 0)),
                   pl.BlockSpec((8, D), lambda i, kk: (0, 0)),
                   ANY, ANY],
        out_shape=[jax.ShapeDtypeStruct((n, D), F32), jax.ShapeDtypeStruct((8, D), F32),
                   jax.ShapeDtypeStruct((N_DEV, r_in) + g_in.shape[2:], g_in.dtype),
                   jax.ShapeDtypeStruct((N_DEV, r_rest) + g_rest.shape[2:], g_rest.dtype)],
        scratch_shapes=[pltpu.VMEM((tm, D), F32),
                        pltpu.SemaphoreType.DMA((nrem,)), pltpu.SemaphoreType.DMA((nrem,)),
                        pltpu.SemaphoreType.DMA((2,))],
        compiler_params=_cparams("arbitrary", "arbitrary"),
    )(dproj, dgates, w_t, wg_t, x, dx1, g_mix, g_in, g_rest)


def _pack_small(acc_f, acc_b, acc_x, gsum):
    def body(f_r, b_r, x_r, s_r, o_r):
        o_r[...] = jnp.zeros_like(o_r)
        o_r[0:1, :] = x_r[0:1, :]
        o_r[1:2, :] = b_r[2:3, 0:D]
        o_r[2:3, :] = b_r[1:2, 0:D]
        o_r[3:4, :] = b_r[1:2, D:2 * D]
        o_r[4:5, :] = b_r[0:1, 0:D]
        o_r[5:6, :] = f_r[0:1, :]
        o_r[6:9, :] = b_r[3:6, 0:D]
        lane = lax.broadcasted_iota(jnp.int32, (1, D), 1)
        o_r[9:10, :] = jnp.where(lane < 2 * NH, s_r[0:1, :], jnp.where(lane == 2 * NH, f_r[1:2, :], 0.0))

    return pl.pallas_call(
        body, name="pack_small",
        out_shape=jax.ShapeDtypeStruct((SMALL_ROWS, D), F32),
    )(acc_f, acc_b, acc_x, gsum)


def _sum_slots(r, name, tr=64, half=None):
    s, rows, w = r.shape
    tr = min(tr, rows)
    assert rows % tr == 0
    nt = rows // tr

    def body(*refs):
        r_ref, o_ref = refs[-2:]
        tot = r_ref[0].astype(F32)
        for k in range(1, s):
            tot = tot + r_ref[k].astype(F32)
        o_ref[...] = tot

    if half is None:
        return pl.pallas_call(
            body, name=name, grid=(nt,),
            in_specs=[pl.BlockSpec((s, tr, w), lambda i: (0, i, 0))],
            out_specs=pl.BlockSpec((tr, w), lambda i: (i, 0)),
            out_shape=jax.ShapeDtypeStruct((rows, w), F32),
            compiler_params=_cparams("arbitrary"),
        )(r)
    return pl.pallas_call(
        body, name=name,
        grid_spec=pltpu.PrefetchScalarGridSpec(
            num_scalar_prefetch=1, grid=(nt,),
            in_specs=[pl.BlockSpec((s, tr, w), lambda i, hf: (0, i, 0))],
            out_specs=pl.BlockSpec((tr, w), lambda i, hf: (hf[0] * nt + i, 0))),
        out_shape=jax.ShapeDtypeStruct((2 * rows, w), F32),
        compiler_params=_cparams("arbitrary"),
    )(half, r)


def _cast_slabs(a, tr=163):
    rows = a.shape[0]
    assert rows % tr == 0

    def body(a_ref, o_ref):
        o_ref[...] = a_ref[...].astype(MXU)

    spec = pl.BlockSpec((tr,) + a.shape[1:], lambda i: (i, 0, 0))
    return pl.pallas_call(
        body, name="cast_w_in", grid=(rows // tr,), in_specs=[spec], out_specs=spec,
        out_shape=jax.ShapeDtypeStruct(a.shape, MXU), compiler_params=_cparams("arbitrary"),
    )(a)


def _sum_slabs(r, name, tr=69):
    s, rows = r.shape[:2]
    assert rows % tr == 0

    def body(r_ref, o_ref):
        tot = r_ref[0].astype(F32)
        for k in range(1, s):
            tot = tot + r_ref[k].astype(F32)
        o_ref[...] = tot

    return pl.pallas_call(
        body, name=name, grid=(rows // tr,),
        in_specs=[pl.BlockSpec((s, tr) + r.shape[2:], lambda i: (0, i, 0, 0))],
        out_specs=pl.BlockSpec((tr,) + r.shape[2:], lambda i: (i, 0, 0)),
        out_shape=jax.ShapeDtypeStruct(r.shape[1:], F32),
        compiler_params=_cparams("arbitrary"),
    )(r)


def _adamw_body(w_r, g_r, m_r, v_r, d_o, m_o, v_o):
    c1 = 1.0 - ADAM_B1 ** ADAM_STEP
    c2 = 1.0 - ADAM_B2 ** ADAM_STEP
    gv = g_r[...]
    mn = ADAM_B1 * m_r[...] + (1.0 - ADAM_B1) * gv
    vn = ADAM_B2 * v_r[...] + (1.0 - ADAM_B2) * (gv * gv)
    m_o[...] = mn
    v_o[...] = vn
    d_o[...] = -ADAM_LR * ((mn / c1) / (jnp.sqrt(vn / c2) + ADAM_EPS) + ADAM_WD * w_r[...])


def _adamw_slabs(w, g, m, v, name, tr):
    rows = w.shape[0]
    assert rows % tr == 0 and w.shape == g.shape

    def body(*refs):
        _adamw_body(*refs)

    spec = pl.BlockSpec((tr,) + w.shape[1:], lambda i: (i, 0, 0))
    shp = jax.ShapeDtypeStruct(w.shape, F32)
    return pl.pallas_call(
        body, name=name, grid=(rows // tr,),
        in_specs=[spec] * 4, out_specs=[spec] * 3, out_shape=[shp] * 3,
        compiler_params=_cparams("arbitrary"),
    )(w, g, m, v)


def _adamw(w, g, m, v, name, g_row0=0, tr=64):
    lead = w.ndim == 3
    rows, cols = w.shape[-2:]
    tr = min(tr, rows)
    assert rows % tr == 0 and g_row0 % tr == 0 and g.shape[1] == cols

    def body(*refs):
        _adamw_body(*refs)

    if lead:
        spec = pl.BlockSpec((None, tr, cols), lambda i: (0, i, 0))
    else:
        spec = pl.BlockSpec((tr, cols), lambda i: (i, 0))
    g_spec = pl.BlockSpec((tr, cols), lambda i: (g_row0 // tr + i, 0))
    shp = jax.ShapeDtypeStruct(w.shape, F32)
    return pl.pallas_call(
        body, name=name, grid=(rows // tr,),
        in_specs=[spec, g_spec, spec, spec], out_specs=[spec] * 3, out_shape=[shp] * 3,
        compiler_params=_cparams("arbitrary"),
    )(w, g, m, v)


def _allgather_chips(halved, whole):
    arrs = [a for a, _ in halved] + list(whole)
    parts = [p for _, p in halved]
    nh, na = len(halved), len(arrs)
    relations = ((1, 0), (0, 1), (1, 1))
    n_ici = 3 * na

    def body(*refs):
        ins, outs = refs[:na], refs[na:2 * na]
        send_sems, recv_sems = refs[2 * na:]
        x, y, c = _position()
        mine = 2 * x + y

        def rows(a, core, stride, size):
            return pl.ds(_part_start(core, stride, len(ins[a].shape) == 2), size)

        def remote(src, dst, k, to):
            return pltpu.make_async_remote_copy(src_ref=src, dst_ref=dst, send_sem=send_sems.at[k],
                                                recv_sem=recv_sems.at[k], device_id=to, device_id_type=MESH)

        first, passed = [], []
        for j, (rx, ry) in enumerate(relations):
            px, py = _flip(x, rx), _flip(y, ry)
            for a in range(na):
                if a < nh:
                    s1, n1, s2, n2 = parts[a]
                    first.append(remote(ins[a].at[rows(a, c, s1, n1)], outs[a].at[mine, rows(a, c, s1, n1)],
                                        j * na + a, (px, py, c)))
                    landed = outs[a].at[2 * px + py, rows(a, c, s2, n2)]
                    passed.append(remote(landed, landed, n_ici + j * nh + a, (x, y, 1 - c)))
                else:
                    first.append(remote(ins[a], outs[a].at[mine], j * na + a, (px, py, c)))
        for cp in first:
            cp.start()
        k = 0
        for j in range(3):
            for a in range(na):
                first[j * na + a].wait_recv()
                if a < nh:
                    passed[k].start()
                    k += 1
        for cp in passed:
            cp.wait_recv()
        for cp in first + passed:
            cp.wait_send()

    nsem = n_ici + 3 * nh
    outs = pl.pallas_call(
        body, name="allgather_weights",
        in_specs=[ANY] * na, out_specs=[ANY] * na,
        out_shape=[jax.ShapeDtypeStruct((N_CHIPS,) + a.shape, a.dtype) for a in arrs],
        scratch_shapes=[pltpu.SemaphoreType.DMA((nsem,)), pltpu.SemaphoreType.DMA((nsem,))],
    )(*arrs)
    chip = 2 * lax.axis_index("x") + lax.axis_index("y")
    return [lax.dynamic_update_slice(o, a[None], (chip,) + (0,) * a.ndim) for o, a in zip(outs, arrs)]


def _join_halves(piece, full_rest, small):
    def body(pc_ref, rest_in, sm, oin, orest, osm, send_sems, recv_sems, local_sem):
        del rest_in
        x, y, c = _position()
        me = 4 * x + 2 * y + c
        sib = (x, y, 1 - c)
        rows = orest.shape[0] // 2
        mine = orest.at[pl.ds(pl.multiple_of(c * rows, rows), rows)]
        copies = [
            pltpu.make_async_remote_copy(
                src_ref=pc_ref.at[pl.ds(c, HALF_START)], dst_ref=oin.at[pl.ds(c * HALF_SLABS, HALF_START)],
                send_sem=send_sems.at[0], recv_sem=recv_sems.at[0], device_id=sib, device_id_type=MESH),
            pltpu.make_async_remote_copy(
                src_ref=mine, dst_ref=mine, send_sem=send_sems.at[1], recv_sem=recv_sems.at[1],
                device_id=sib, device_id_type=MESH)]
        na = 2
        copies.append(pltpu.make_async_copy(sm, osm.at[me], local_sem))
        for r in range(1, N_DEV):
            to = (_flip(x, (r >> 2) & 1), _flip(y, (r >> 1) & 1), _flip(c, r & 1))
            copies.append(pltpu.make_async_remote_copy(
                src_ref=sm, dst_ref=osm.at[me], send_sem=send_sems.at[na + r - 1],
                recv_sem=recv_sems.at[na + r - 1], device_id=to, device_id_type=MESH))
        for cp in copies:
            cp.start()
        for cp in copies:
            cp.wait()

    nsem = 2 + N_DEV - 1
    return pl.pallas_call(
        body, name="join_halves",
        in_specs=[ANY] * 3, out_specs=[ANY] * 3,
        out_shape=[jax.ShapeDtypeStruct((SLABS_CHIP,) + piece.shape[1:], piece.dtype),
                   jax.ShapeDtypeStruct(full_rest.shape, full_rest.dtype),
                   jax.ShapeDtypeStruct((N_DEV,) + small.shape, small.dtype)],
        scratch_shapes=[pltpu.SemaphoreType.DMA((nsem,)), pltpu.SemaphoreType.DMA((nsem,)),
                        pltpu.SemaphoreType.DMA],
        input_output_aliases={1: 1},
    )(piece, full_rest, small)


def _to_internal(w_slabs):
    order = sorted(SEGMENTS, key=lambda s: s[2])
    main = jnp.concatenate([w_slabs[g0 // 2:(g0 + w) // 2] for g0, w, _ in order], axis=0)
    gate = w_slabs[GATE_COL // 2:GATE_COL // 2 + NH].reshape(2 * NH, D)
    return main.reshape(NMAIN, D), jnp.pad(gate, ((0, GATE_W - 2 * NH), (0, 0)))


def _to_global(main_t, gate_t):
    main = main_t.reshape(NMAIN // 2, 16, D // 8)
    parts = sorted([(g0, main[i0 // 2:(i0 + w) // 2]) for g0, w, i0 in SEGMENTS]
                   + [(GATE_COL, gate_t[0:2 * NH].reshape(NH, 16, D // 8))], key=lambda s: s[0])
    return jnp.concatenate([p for _, p in parts], axis=0)


def _pack_rows(wa, wb, wo, wpg, wp):
    return jnp.concatenate([wa, wb, wo, wpg, wp.reshape(PACK_ROWS[4], D)], axis=0)


def _pad_rows(a, rows=8):
    return jnp.pad(a, ((0, rows - a.shape[0]), (0, 0)))


def kernel(x, p, g_mix, w_in, conv_w, conv_b, w_a_out, b_gates, g_head, w_b_out, w_o, g_ple, w_ple_gate, w_ple, g_final, loss_target, m_g_mix, m_w_in, m_conv_w, m_conv_b, m_w_a_out, m_b_gates, m_g_head, m_w_b_out, m_w_o, m_g_ple, m_w_ple_gate, m_w_ple, m_g_final, v_g_mix, v_w_in, v_conv_w, v_conv_b, v_w_a_out, v_b_gates, v_g_head, v_w_b_out, v_w_o, v_g_ple, v_w_ple_gate, v_w_ple, v_g_final):
    chip = 2 * lax.axis_index("x") + lax.axis_index("y")
    xs, ps, ts = x[0], p[0, 0], loss_target[0]
    g_fin = g_final.reshape(1, D)

    pack_w = _pack_rows(w_a_out[0], w_b_out[0], w_o[0], w_ple_gate[0], w_ple[0])
    w_slabs = _cast_slabs(jnp.transpose(w_in, (2, 0, 1)).reshape(SLABS_CHIP, 16, D // 8))
    half_rows = PACK_TOTAL // 2
    g_win, g_pack, g_cw = _allgather_chips(
        [(w_slabs, (HALF_START, HALF_SLABS, HALF_SLABS, HALF_START)),
         (pack_w.astype(MXU), (half_rows, half_rows, half_rows, half_rows))], [_pad_rows(conv_w[0])])
    w_t, wg_t = _to_internal(g_win.reshape(N_IN // 2, 16, D // 8))
    offs = [0, 256, 768, 1024, 1280, 1344]
    wa, wb, wo, wpg = [g_pack[:, offs[k]:offs[k + 1]].reshape(-1, D) for k in range(4)]
    wp = jnp.transpose(g_pack[:, offs[4]:offs[5]].reshape(N_CHIPS, PLE, PLE), (1, 0, 2)).reshape(PLE, D)
    cw = jnp.transpose(g_cw, (1, 0, 2)).reshape(8, D)

    bias = jnp.pad(b_gates, ((0, 0), (0, GATE_W - 2 * NH)))
    proj, hn, gates = _proj(xs, g_mix, w_t, wg_t)
    h, stats, cs, ns, ms = _mlstm_fwd(proj, gates, bias)
    (a_pre, b_pre, mg, xn1, de, dgp, ya, yb, x1, dx2, acc_f) = _tail_fwd(
        proj, h, xs, ps, ts, cw, conv_b, g_head, g_ple, g_fin, wa, wb, wo, wpg, wp)
    dproj, dcv, dh, dx1, dx1b, dya, dyb, acc_b = _tail_bwd(
        proj, h, dgp, dx2, x1, ya, yb, cw, conv_b, g_head, g_ple, wpg, wo, wb, wa)
    dproj = _conv_bwd(proj, dcv, cw, dproj)
    dproj, dgates, gsum = _mlstm_bwd(proj, gates, bias, h, dh, stats, cs, ns, ms, dproj)
    d_main = _matmul_tn(dproj, hn, "dw_in", out_dtype=WIRE, tk=4096)
    d_gate = _matmul_tn(dgates, hn, "dw_gate", out_dtype=WIRE)
    d_wa = _matmul_tn(a_pre, dya, "dw_a_out", out_dtype=WIRE)
    d_wb = _matmul_tn(b_pre, dyb, "dw_b_out", out_dtype=WIRE)
    d_wo = _matmul_tn(mg, dx1b, "dw_o", out_dtype=WIRE)
    d_wpg = _matmul_tn(xn1, dgp, "dw_ple_gate", out_dtype=WIRE)
    d_wp = _matmul_tn(ps, de, "dw_ple", out_dtype=WIRE)

    g_in = _to_global(d_main, d_gate).reshape(N_CHIPS, SLABS_CHIP, 16, D // 8)
    d_wp_c = jnp.transpose(d_wp.reshape(PLE, N_CHIPS, PLE), (1, 0, 2)).reshape(N_CHIPS, PACK_ROWS[4], D)
    g_rest = jnp.concatenate(
        [d_wa.reshape(N_CHIPS, -1, D), d_wb.reshape(N_CHIPS, -1, D), d_wo.reshape(N_CHIPS, -1, D),
         d_wpg.reshape(N_CHIPS, -1, D), d_wp_c], axis=1)
    grad_x, acc_x, r_in, r_rest = _input_grad(dproj, dgates, w_t, wg_t, xs, dx1, g_mix, g_in, g_rest)
    small = _pack_small(acc_f, acc_b, acc_x, gsum)
    core = lax.axis_index("c").astype(jnp.int32)
    piece = _sum_slabs(r_in, "sum_w_in")
    gw_in, gw_rest, r_small = _join_halves(
        piece, _sum_slots(r_rest, "sum_rest", tr=96, half=core.reshape(1)), small)
    gw_in = lax.dynamic_update_slice(gw_in, piece, (core * HALF_START, 0, 0)).reshape(COLS_CHIP, 8, D // 8)
    gs = _sum_slots(r_small, "sum_small", tr=SMALL_ROWS)

    big = []
    row0 = 0
    for name, w, m, v in (("w_a_out", w_a_out, m_w_a_out, v_w_a_out), ("w_b_out", w_b_out, m_w_b_out, v_w_b_out),
                          ("w_o", w_o, m_w_o, v_w_o), ("w_ple_gate", w_ple_gate, m_w_ple_gate, v_w_ple_gate)):
        big.append((name, w, m, v, gw_rest, row0))
        row0 += w.shape[1]
    g_wp = gw_rest[row0:row0 + PACK_ROWS[4]].reshape(PLE, PLE)
    big.append(("w_ple", w_ple, m_w_ple, v_w_ple, g_wp, 0))
    upd = {name: _adamw(w, g, m, v, "adamw_" + name, g_row0=r0) for name, w, m, v, g, r0 in big}
    g_big = {name: (g if name == "w_ple" else g[r0:r0 + w.shape[1]])[None] for name, w, m, v, g, r0 in big}
    slabs = lambda a: jnp.transpose(a, (2, 0, 1)).reshape(COLS_CHIP, 8, D // 8)
    unslab = lambda a: jnp.transpose(a, (1, 2, 0)).reshape(1, D, COLS_CHIP)
    upd["w_in"] = [unslab(u) for u in _adamw_slabs(
        slabs(w_in), gw_in, slabs(m_w_in), slabs(v_w_in), "adamw_w_in", tr=326)]
    g_big["w_in"] = unslab(gw_in)

    lane = lax.broadcasted_iota(jnp.int32, (1, D), 1)
    g_small = jnp.concatenate([gs[0:6], jnp.where(lane < 2 * NH, gs[9:10], 0.0), jnp.zeros((1, D), F32)], axis=0)

    def small_pack(gm, cb_, bg, gh, gp, gf):
        return jnp.concatenate([gm, cb_, gh.reshape(2, D), gp, gf.reshape(1, D),
                                jnp.pad(bg, ((0, 0), (0, D - 2 * NH))), jnp.zeros((1, D), F32)], axis=0)

    ws = small_pack(g_mix, conv_b, b_gates, g_head, g_ple, g_final)
    ms_ = small_pack(m_g_mix, m_conv_b, m_b_gates, m_g_head, m_g_ple, m_g_final)
    vs = small_pack(v_g_mix, v_conv_b, v_b_gates, v_g_head, v_g_ple, v_g_final)
    upd_s = _adamw(ws, g_small, ms_, vs, "adamw_small")
    g_cw_mine = _pad_rows(lax.dynamic_slice(gs[6:9], (0, chip * PLE), (3, PLE)))
    upd_c = _adamw(_pad_rows(conv_w[0]), g_cw_mine, _pad_rows(m_conv_w[0]), _pad_rows(v_conv_w[0]),
                   "adamw_conv_w")

    def leaves(bigs, sm, cwv):
        return [sm[0:1], bigs["w_in"], cwv[0:3][None], sm[1:2], bigs["w_a_out"], sm[6:7, 0:2 * NH],
                sm[2:4].reshape(1, VD), bigs["w_b_out"], bigs["w_o"], sm[4:5], bigs["w_ple_gate"],
                bigs["w_ple"], sm[5]]

    loss = gs[9, 2 * NH]
    outs = [loss, grad_x[None]] + leaves(g_big, g_small, g_cw_mine)
    for k in range(3):
        outs += leaves({name: u[k] for name, u in upd.items()}, upd_s[k], upd_c[k])
    return tuple(outs)
```

```python
import jax
import jax.numpy as jnp
from jax import lax
from jax.experimental import pallas as pl
from jax.experimental.pallas import tpu as pltpu

F32 = jnp.float32
MXU = jnp.bfloat16
WIRE = jnp.bfloat16

D = 1024
NH, DK, DV = 4, 256, 512
VD = NH * DV
PLE = 256
LCH = 256
EPS = 1e-6
N_IN = 14344
NMAIN = 14336
GATE_W = 128
N_CHIPS, N_DEV = 4, 8

C_BA, C_ZA, C_O, C_ZB, C_GA, C_GB = 0, 1024, 2048, 4096, 6144, 7168
QK = NH * DK
C_Q, C_K, C_V, C_XA, C_CA = 8192, 9216, 10240, 12288, 13312
QKV_W = 2 * QK + VD
TAIL_W = 8192
CONV_W = 2048
SEGMENTS = (
    (0, 1024, C_XA), (1024, 1024, C_BA), (2048, 1024, C_CA), (3072, 1024, C_ZA),
    (4096, QKV_W, C_Q), (8192, 4096, C_O), (12296, 2048, C_GA),
)
GATE_COL = 12288

COLS_CHIP = N_IN // N_CHIPS
SLABS_CHIP = COLS_CHIP // 2
HALF_START = SLABS_CHIP // 2
HALF_SLABS = SLABS_CHIP - HALF_START

PACK_ROWS = (256, 512, 256, 256, 64)
PACK_TOTAL = sum(PACK_ROWS)
SMALL_ROWS = 16

ADAM_LR, ADAM_B1, ADAM_B2, ADAM_EPS, ADAM_WD, ADAM_STEP = 0.001, 0.9, 0.999, 1e-08, 0.01, 10

VMEM_LIMIT = 56 * 1024 * 1024
MESH = pl.DeviceIdType.MESH
ANY = pl.BlockSpec(memory_space=pl.ANY)


def _cparams(*sem):
    return pltpu.CompilerParams(dimension_semantics=sem, vmem_limit_bytes=VMEM_LIMIT)


def _dot(a, b):
    return jnp.dot(a, b, preferred_element_type=F32)


def _dot_nt(a, b):
    return lax.dot_general(a, b, (((1,), (1,)), ((), ())), preferred_element_type=F32)


def _dot_tn(a, b):
    return lax.dot_general(a, b, (((0,), (0,)), ((), ())), preferred_element_type=F32)


def _sigmoid(x):
    return 1.0 / (1.0 + jnp.exp(-x))


def _logsig(x):
    return jnp.minimum(x, 0.0) - jnp.log(1.0 + jnp.exp(-jnp.abs(x)))


GATE_FLOOR = -80.0


def _gate(v):
    e = jnp.exp(-jnp.maximum(v, GATE_FLOOR))
    s = 1.0 / (1.0 + e)
    return s, e * s * s


def _rstd(x):
    return lax.rsqrt(jnp.mean(x * x, axis=-1, keepdims=True) + EPS)


def _norm_bwd(dy, xhat, r, g):
    dxh = dy * g
    return r * (dxh - xhat * jnp.mean(dxh * xhat, axis=-1, keepdims=True))


def _f32(ref):
    return ref[...].astype(F32)


def _proj(x, g_mix, w_t, wg_t, tm=1024, tn=2048):
    n = x.shape[0]
    m = w_t.shape[0]
    tm = min(tm, n)

    def body(x_ref, g_ref, b_ref, wg_ref, o_ref, hn_ref, gate_ref):
        @pl.when(pl.program_id(1) == 0)
        def _():
            xv = x_ref[...]
            hn = (xv * _rstd(xv) * g_ref[...]).astype(MXU)
            hn_ref[...] = hn
            gate_ref[...] = _dot_nt(hn, wg_ref[...])

        o_ref[...] = _dot_nt(hn_ref[...], b_ref[...]).astype(MXU)

    return pl.pallas_call(
        body, name="proj", grid=(n // tm, m // tn),
        in_specs=[pl.BlockSpec((tm, D), lambda i, j: (i, 0)),
                  pl.BlockSpec((1, D), lambda i, j: (0, 0)),
                  pl.BlockSpec((tn, D), lambda i, j: (j, 0)),
                  pl.BlockSpec((GATE_W, D), lambda i, j: (0, 0))],
        out_specs=[pl.BlockSpec((tm, tn), lambda i, j: (i, j)),
                   pl.BlockSpec((tm, D), lambda i, j: (i, 0)),
                   pl.BlockSpec((tm, GATE_W), lambda i, j: (i, 0))],
        out_shape=[jax.ShapeDtypeStruct((n, m), MXU), jax.ShapeDtypeStruct((n, D), MXU),
                   jax.ShapeDtypeStruct((n, GATE_W), F32)],
        compiler_params=_cparams("arbitrary", "arbitrary"),
    )(x, g_mix, w_t, wg_t)


def _matmul_tn(a, b, name, out_dtype=F32, ta=1024, tb=1024, tk=2048):
    n, ka = a.shape
    kb = b.shape[1]
    ta, tb, tk = min(ta, ka), min(tb, kb), min(tk, n)
    nk = n // tk

    def body(a_ref, b_ref, o_ref, acc):
        kk = pl.program_id(2)

        @pl.when(kk == 0)
        def _():
            acc[...] = jnp.zeros_like(acc)

        acc[...] += _dot_tn(a_ref[...].astype(MXU), b_ref[...].astype(MXU))

        @pl.when(kk == nk - 1)
        def _():
            o_ref[...] = acc[...].astype(out_dtype)

    return pl.pallas_call(
        body, name=name, grid=(ka // ta, kb // tb, nk),
        in_specs=[pl.BlockSpec((tk, ta), lambda i, j, kk: (kk, i)),
                  pl.BlockSpec((tk, tb), lambda i, j, kk: (kk, j))],
        out_specs=pl.BlockSpec((ta, tb), lambda i, j, kk: (i, j)),
        out_shape=jax.ShapeDtypeStruct((ka, kb), out_dtype),
        scratch_shapes=[pltpu.VMEM((ta, tb), F32)],
        compiler_params=_cparams("arbitrary", "arbitrary", "arbitrary"),
    )(a, b)


def _sub_row(block_t, j):
    sub = lax.broadcasted_iota(jnp.int32, block_t.shape, 0)
    return jnp.sum(jnp.where(sub == j, block_t, 0.0), axis=0, keepdims=True)


def _chunk_cumsum(g):
    n = g.shape[0]
    r = lax.broadcasted_iota(jnp.int32, (n, n), 0)
    c = lax.broadcasted_iota(jnp.int32, (n, n), 1)
    return jnp.dot(jnp.where(r >= c, 1.0, 0.0), _logsig(g), precision=lax.Precision.HIGHEST,
                   preferred_element_type=F32)


def _chunk_decay(li_col, li_row, b_col, b_row, m_prev):
    n = li_col.shape[0]
    r = lax.broadcasted_iota(jnp.int32, (n, n), 0)
    c = lax.broadcasted_iota(jnp.int32, (n, n), 1)
    tri = r >= c
    b_last = b_col[n - 1:n, :]
    dmat = jnp.where(tri, b_col - b_row + li_row, -jnp.inf)
    a_col = b_col + m_prev
    g_col = b_last - b_col + li_col
    m_new = jnp.maximum(b_last + m_prev, jnp.max(g_col, axis=0, keepdims=True))
    w_col = jnp.exp(g_col - m_new)
    decay = jnp.exp(b_last + m_prev - m_new)
    return tri, dmat, a_col, m_new, w_col, decay


def _qkv_specs(row_of):
    q_spec = pl.BlockSpec((LCH, QK), lambda c: (row_of(c), C_Q // QK))
    k_spec = pl.BlockSpec((LCH, QK), lambda c: (row_of(c), C_K // QK))
    v_spec = pl.BlockSpec((LCH, VD), lambda c: (row_of(c), C_V // VD))
    return q_spec, k_spec, v_spec


def _state_specs(row_of):
    return [pl.BlockSpec((NH, None, DK, DV), lambda c: (0, row_of(c), 0, 0)),
            pl.BlockSpec((NH, None, 1, DK), lambda c: (0, row_of(c), 0, 0)),
            pl.BlockSpec((NH, None, 1, GATE_W), lambda c: (0, row_of(c), 0, 0))]


def _lane_put(col, lane_id, width=GATE_W):
    lane = lax.broadcasted_iota(jnp.int32, (col.shape[0], width), 1)
    return jnp.where(lane == lane_id, col, 0.0)


def _lane_get(block, lane_id):
    lane = lax.broadcasted_iota(jnp.int32, block.shape, 1)
    return jnp.sum(jnp.where(lane == lane_id, block, 0.0), axis=1, keepdims=True)


def _mlstm_fwd(proj, gates, bias):
    n = proj.shape[0]
    nc = n // LCH

    def body(q_ref, k_ref, v_ref, g_ref, bias_ref,
             h_ref, st_ref, cs_ref, ns_ref, ms_ref, c_scr, n_scr, m_scr):
        @pl.when(pl.program_id(0) == 0)
        def _():
            c_scr[...] = jnp.zeros_like(c_scr)
            n_scr[...] = jnp.zeros_like(n_scr)
            m_scr[...] = jnp.full_like(m_scr, -jnp.inf)

        g = g_ref[...] + bias_ref[...]
        gt = g.T[0:8, :]
        b = _chunk_cumsum(g)
        bt = b.T[0:8, :]
        stats = jnp.zeros((LCH, GATE_W), F32)
        for hd in range(NH):
            m_all = m_scr[hd]
            m_prev = m_all[0:1, 0:1]
            tri, dmat, a_col, m_new, w_col, decay = _chunk_decay(
                _lane_get(g, hd), _sub_row(gt, hd), _lane_get(b, NH + hd), _sub_row(bt, NH + hd), m_prev)
            m_col = jnp.maximum(a_col, jnp.max(dmat, axis=1, keepdims=True))
            dl = jnp.exp(dmat - m_col)
            inter = jnp.exp(a_col - m_col)

            qs = q_ref[:, hd * DK:(hd + 1) * DK] * (DK ** -0.5)
            kk = k_ref[:, hd * DK:(hd + 1) * DK]
            vv = v_ref[:, hd * DV:(hd + 1) * DV]
            cst = c_scr[hd]
            nst = n_scr[hd]
            cs_ref[hd] = cst
            ns_ref[hd] = nst
            ms_ref[hd] = m_all

            sc = _dot_nt(qs, kk) * dl
            num = _dot(sc.astype(MXU), vv) + inter * _dot(qs, cst.astype(MXU))
            den = (jnp.sum(sc, axis=1, keepdims=True)
                   + inter * jnp.sum(qs.astype(F32) * nst, axis=1, keepdims=True))
            nrm = jnp.maximum(jnp.abs(den), jnp.exp(-m_col))
            h_ref[:, hd * DV:(hd + 1) * DV] = (num * (1.0 / nrm)).astype(h_ref.dtype)
            stats = stats + _lane_put(den, hd) + _lane_put(m_col, NH + hd)

            kw = kk.astype(F32) * w_col
            c_scr[hd] = decay * cst + _dot_tn(kw.astype(MXU), vv)
            n_scr[hd] = decay * nst + jnp.sum(kw, axis=0, keepdims=True)
            m_scr[hd] = jnp.broadcast_to(m_new, (1, GATE_W))
        st_ref[...] = stats

    q_spec, k_spec, v_spec = _qkv_specs(lambda c: c)
    return pl.pallas_call(
        body, name="mlstm_fwd", grid=(nc,),
        in_specs=[q_spec, k_spec, v_spec,
                  pl.BlockSpec((LCH, GATE_W), lambda c: (c, 0)),
                  pl.BlockSpec((1, GATE_W), lambda c: (0, 0))],
        out_specs=[pl.BlockSpec((LCH, VD), lambda c: (c, 0)),
                   pl.BlockSpec((LCH, GATE_W), lambda c: (c, 0))] + _state_specs(lambda c: c),
        out_shape=[jax.ShapeDtypeStruct((n, VD), MXU),
                   jax.ShapeDtypeStruct((n, GATE_W), F32),
                   jax.ShapeDtypeStruct((NH, nc, DK, DV), F32),
                   jax.ShapeDtypeStruct((NH, nc, 1, DK), F32),
                   jax.ShapeDtypeStruct((NH, nc, 1, GATE_W), F32)],
        scratch_shapes=[pltpu.VMEM((NH, DK, DV), F32), pltpu.VMEM((NH, 1, DK), F32),
                        pltpu.VMEM((NH, 1, GATE_W), F32)],
        compiler_params=_cparams("arbitrary"),
    )(proj, proj, proj, gates, bias)


def _mlstm_bwd(proj, gates, bias, h, dh, stats, cs, ns, ms, dproj):
    n = proj.shape[0]
    nc = n // LCH

    def body(q_ref, k_ref, v_ref, g_ref, bias_ref, h_ref, dh_ref, st_ref,
             cs_ref, ns_ref, ms_ref, dproj_in,
             dqkv_ref, dg_ref, gsum_ref, dc_scr, dn_scr):
        del dproj_in

        @pl.when(pl.program_id(0) == 0)
        def _():
            dc_scr[...] = jnp.zeros_like(dc_scr)
            dn_scr[...] = jnp.zeros_like(dn_scr)
            gsum_ref[...] = jnp.zeros_like(gsum_ref)

        g = g_ref[...] + bias_ref[...]
        gt = g.T[0:8, :]
        b = _chunk_cumsum(g)
        bt = b.T[0:8, :]
        stats = st_ref[...]
        r = lax.broadcasted_iota(jnp.int32, (LCH, LCH), 0)
        c = lax.broadcasted_iota(jnp.int32, (LCH, LCH), 1)
        sub8 = lax.broadcasted_iota(jnp.int32, (8, LCH), 0)
        last = lax.broadcasted_iota(jnp.int32, (LCH, 1), 0) == LCH - 1
        dli_all = jnp.zeros((LCH, GATE_W), F32)
        db_all = jnp.zeros((LCH, GATE_W), F32)
        colsum_t = jnp.zeros((8, LCH), F32)
        for hd in range(NH):
            m_prev = ms_ref[hd][0:1, 0:1]
            tri, dmat, a_col, m_new, w_col, decay = _chunk_decay(
                _lane_get(g, hd), _sub_row(gt, hd), _lane_get(b, NH + hd), _sub_row(bt, NH + hd), m_prev)
            m_col = _lane_get(stats, NH + hd)
            dl = jnp.exp(dmat - m_col)
            inter = jnp.exp(a_col - m_col)

            qs = q_ref[:, hd * DK:(hd + 1) * DK] * (DK ** -0.5)
            kk = k_ref[:, hd * DK:(hd + 1) * DK]
            vv = v_ref[:, hd * DV:(hd + 1) * DV]
            qf = qs.astype(F32)
            kf = kk.astype(F32)
            cst = cs_ref[hd]
            nst = ns_ref[hd]
            cb = cst.astype(MXU)
            dcn = dc_scr[hd]
            dnn = dn_scr[hd]
            dcb = dcn.astype(MXU)

            den = _lane_get(stats, hd)
            floor = jnp.exp(-m_col)
            nrm = jnp.maximum(jnp.abs(den), floor)
            dhv = dh_ref[:, hd * DV:(hd + 1) * DV].astype(F32)
            rn = 1.0 / nrm
            dnum = dhv * rn
            dnum_b = dnum.astype(MXU)
            dhh = jnp.sum(dhv * h_ref[:, hd * DV:(hd + 1) * DV].astype(F32), axis=1, keepdims=True)
            dden = jnp.where(jnp.abs(den) > floor, -dhh * rn * jnp.sign(den), 0.0)

            sc = _dot_nt(qs, kk) * dl
            dsc = _dot_nt(dnum_b, vv) + dden
            da = (dl * dsc).astype(MXU)
            gmat = sc * dsc

            dq = _dot(da, kk) + inter * (_dot_nt(dnum_b, cb) + dden * nst)
            dk_state = w_col * (_dot_nt(vv, dcb) + dnn)
            dk = _dot_tn(da, qs) + dk_state
            kw = (kf * w_col).astype(MXU)
            dv = _dot_tn(sc.astype(MXU), dnum_b) + _dot(kw, dcb)
            dqkv_ref[:, hd * DK:(hd + 1) * DK] = (dq * (DK ** -0.5)).astype(dqkv_ref.dtype)
            dqkv_ref[:, QK + hd * DK:QK + (hd + 1) * DK] = dk.astype(dqkv_ref.dtype)
            dqkv_ref[:, 2 * QK + hd * DV:2 * QK + (hd + 1) * DV] = dv.astype(dqkv_ref.dtype)

            num_i = _dot(qs, cb)
            den_i = jnp.sum(qf * nst, axis=1, keepdims=True)
            e_col = inter * (jnp.sum(dnum * num_i, axis=1, keepdims=True) + dden * den_i)
            h_col = jnp.sum(kf * dk_state, axis=1, keepdims=True)
            f_dec = decay * (jnp.sum(jnp.sum(cst * dcn, axis=1, keepdims=True), axis=0, keepdims=True)
                             + jnp.sum(nst * dnn, axis=1, keepdims=True))
            row_g = jnp.sum(gmat, axis=1, keepdims=True)
            col_g = jnp.sum(gmat, axis=0, keepdims=True)
            colsum_t = colsum_t + jnp.where(sub8 == hd, col_g, 0.0) - jnp.where(sub8 == NH + hd, col_g, 0.0)
            db_col = row_g + e_col - h_col
            db_col = db_col + jnp.where(last, jnp.sum(h_col, axis=0, keepdims=True) + f_dec, 0.0)
            dli_all = dli_all + _lane_put(h_col, hd)
            db_all = db_all + _lane_put(db_col, NH + hd)

            dc_scr[hd] = decay * dcn + _dot_tn((qf * inter).astype(MXU), dnum_b)
            dn_scr[hd] = decay * dnn + jnp.sum(qf * (inter * dden), axis=0, keepdims=True)
        colsum = jnp.concatenate([colsum_t, jnp.zeros((GATE_W - 8, LCH), F32)], axis=0).T
        lane = lax.broadcasted_iota(jnp.int32, (LCH, GATE_W), 1)
        dli_all = dli_all + jnp.where(lane < NH, colsum, 0.0)
        db_all = db_all + jnp.where(lane >= NH, colsum, 0.0)
        dlf_all = jnp.dot(jnp.where(c >= r, 1.0, 0.0), db_all, precision=lax.Precision.HIGHEST,
                          preferred_element_type=F32)
        dg = dli_all + dlf_all * _sigmoid(-g)
        dg_ref[...] = dg
        gsum_ref[0:1, 0:GATE_W] += jnp.sum(dg, axis=0, keepdims=True)

    rev = lambda c: nc - 1 - c
    q_spec, k_spec, v_spec = _qkv_specs(rev)
    hv_spec = pl.BlockSpec((LCH, VD), lambda c: (rev(c), 0))
    gate_spec = pl.BlockSpec((LCH, GATE_W), lambda c: (rev(c), 0))
    return pl.pallas_call(
        body, name="mlstm_bwd", grid=(nc,),
        in_specs=[q_spec, k_spec, v_spec, gate_spec,
                  pl.BlockSpec((1, GATE_W), lambda c: (0, 0)),
                  hv_spec, hv_spec, gate_spec] + _state_specs(rev) + [ANY],
        out_specs=[pl.BlockSpec((LCH, QKV_W), lambda c: (rev(c), C_Q // QKV_W)),
                   gate_spec,
                   pl.BlockSpec((8, D), lambda c: (0, 0))],
        out_shape=[jax.ShapeDtypeStruct(dproj.shape, dproj.dtype),
                   jax.ShapeDtypeStruct((n, GATE_W), F32),
                   jax.ShapeDtypeStruct((8, D), F32)],
        scratch_shapes=[pltpu.VMEM((NH, DK, DV), F32), pltpu.VMEM((NH, 1, DK), F32)],
        input_output_aliases={11: 0},
        compiler_params=_cparams("arbitrary"),
    )(proj, proj, proj, gates, bias, h, dh, stats, cs, ns, ms, dproj)


def _proj_spec(tm, col, width):
    return pl.BlockSpec((tm, width), lambda i: (i, col // width))


def _tail_in_specs(tm):
    return [_proj_spec(tm, 0, TAIL_W), _proj_spec(tm, C_XA, CONV_W),
            pl.BlockSpec((8, CONV_W), lambda i: (jnp.maximum(i * (tm // 8) - 1, 0), C_XA // CONV_W))]


def _tail_views(tail_r, conv_r, halo_r):
    cols = lambda ref, c0, w: ref.at[:, pl.ds(c0, w)]
    return (cols(tail_r, C_BA, D), cols(tail_r, C_ZA, D), cols(tail_r, C_O, VD), cols(tail_r, C_ZB, VD),
            cols(tail_r, C_GA, D), cols(tail_r, C_GB, D), cols(conv_r, 0, D), cols(conv_r, D, D),
            cols(halo_r, 0, D), cols(halo_r, D, D))


def _const(shape):
    return pl.BlockSpec(shape, lambda i: (0,) * len(shape))


def _conv_inputs(i, tm, xa_ref, ca_ref, xah_ref, cah_ref):
    u = _f32(xa_ref) * _f32(ca_ref)
    uh = jnp.where(i > 0, _f32(xah_ref) * _f32(cah_ref), 0.0)
    rid = lax.broadcasted_iota(jnp.int32, u.shape, 0)
    u1 = jnp.where(rid == 0, uh[7:8, :], pltpu.roll(u, 1, 0))
    u2 = jnp.where(rid == 0, uh[6:7, :], jnp.where(rid == 1, uh[7:8, :], pltpu.roll(u, 2, 0)))
    return u, u1, u2


def _head_norm(hh, gh):
    out = []
    for j in range(NH):
        hj = hh[:, j * DV:(j + 1) * DV]
        rj = _rstd(hj)
        out.append((hj * rj, rj, gh[:, j * DV:(j + 1) * DV]))
    return out


def _tail_fwd(proj, h, x, p, t, cw, cb, gh, gple, gfin, wa, wb, wo, wpg, wp):
    n = x.shape[0]
    tm = min(256, n)

    def body(tail_r, conv_r, halo_r, h_r, x_r, p_r, t_r,
             cw_r, cb_r, gh_r, gple_r, gfin_r, wa_r, wb_r, wo_r, wpg_r, wp_r,
             apre_o, bpre_o, mg_o, xn1_o, de_o, dgp_o, ya_o, yb_o, x1_o, dx2_o, acc_o):
        ba_r, za_r, o_r, zb_r, ga_r, gb_r, xa_r, ca_r, xah_r, cah_r = _tail_views(tail_r, conv_r, halo_r)
        i = pl.program_id(0)

        @pl.when(i == 0)
        def _():
            acc_o[...] = jnp.zeros_like(acc_o)

        u, u1, u2 = _conv_inputs(i, tm, xa_r, ca_r, xah_r, cah_r)
        cwv = cw_r[...]
        cv = cwv[0:1, :] * u2 + cwv[1:2, :] * u1 + cwv[2:3, :] * u + cb_r[...]
        za = za_r[...]
        a_pre = ba_r[...] * cv.astype(MXU) * (za * _sigmoid(za))
        apre_o[...] = a_pre
        ya = _dot(a_pre, wa_r[...]).astype(MXU)

        hn = _head_norm(_f32(h_r), gh_r[...])
        hbn = jnp.concatenate([xh * g for xh, _, g in hn], axis=1)
        zb = zb_r[...]
        b_pre = _sigmoid(o_r[...]) * hbn.astype(MXU) * (zb * _sigmoid(zb))
        bpre_o[...] = b_pre
        yb = _dot(b_pre, wb_r[...]).astype(MXU)
        ya_o[...] = ya
        yb_o[...] = yb

        mg = _sigmoid(ga_r[...]) * ya + _sigmoid(gb_r[...]) * yb
        mg_o[...] = mg
        x1 = x_r[...] + _dot(mg, wo_r[...])
        x1_o[...] = x1.astype(MXU)
        xn1 = (x1 * _rstd(x1) * gple_r[...]).astype(MXU)
        xn1_o[...] = xn1
        gt = _sigmoid(_dot(xn1, wpg_r[...]))
        e = _dot(p_r[...].astype(MXU), wp_r[...])
        x2 = x1 + gt * e
        r2 = _rstd(x2)
        xh2 = x2 * r2
        gf = gfin_r[...]
        diff = xh2 * gf - t_r[...]
        dy = diff * (1.0 / D)
        dx2 = _norm_bwd(dy, xh2, r2, gf)
        dx2_o[...] = dx2
        de_o[...] = (dx2 * gt).astype(MXU)
        dgp_o[...] = (dx2 * e * gt * (1.0 - gt)).astype(MXU)
        acc_o[0:1, :] += jnp.sum(dy * xh2, axis=0, keepdims=True)
        loss = 0.5 * jnp.sum(jnp.sum(diff * diff, axis=1, keepdims=True), axis=0, keepdims=True) * (1.0 / D)
        acc_o[1:2, :] += jnp.broadcast_to(loss, (1, D))

    row = lambda w, dt: (pl.BlockSpec((tm, w), lambda i: (i, 0)), jax.ShapeDtypeStruct((n, w), dt))
    outs = [row(D, MXU), row(VD, MXU), row(D, MXU), row(D, MXU), row(D, MXU), row(D, MXU),
            row(D, MXU), row(D, MXU), row(D, MXU), row(D, F32),
            (_const((8, D)), jax.ShapeDtypeStruct((8, D), F32))]
    return pl.pallas_call(
        body, name="tail_fwd", grid=(n // tm,),
        in_specs=_tail_in_specs(tm) + [
                  pl.BlockSpec((tm, VD), lambda i: (i, 0)),
                  pl.BlockSpec((tm, D), lambda i: (i, 0)),
                  pl.BlockSpec((tm, PLE), lambda i: (i, 0)),
                  pl.BlockSpec((tm, D), lambda i: (i, 0)),
                  _const((8, D)), _const((1, D)), _const((1, VD)), _const((1, D)), _const((1, D)),
                  _const((D, D)), _const((VD, D)), _const((D, D)), _const((D, D)), _const((PLE, D))],
        out_specs=[s for s, _ in outs],
        out_shape=[s for _, s in outs],
        compiler_params=_cparams("arbitrary"),
    )(proj, proj, proj, h, x, p, t, cw, cb, gh, gple, gfin, wa, wb, wo, wpg, wp)


def _tail_bwd(proj, h, dgp, dx2, x1, ya, yb, cw, cb, gh, gple, wpg, wo, wb, wa):
    n = x1.shape[0]
    tm = min(256, n)

    def body(tail_r, conv_r, halo_r, h_r,
             dgp_r, dx2_r, x1_r, ya_r, yb_r, cw_r, cb_r, gh_r, gple_r,
             wpg_r, wo_r, wb_r, wa_r,
             dproj_o, dcv_o, dh_o, dx1_o, dx1b_o, dya_o, dyb_o, acc_o):
        ba_r, za_r, o_r, zb_r, ga_r, gb_r, xa_r, ca_r, xah_r, cah_r = _tail_views(tail_r, conv_r, halo_r)
        i = pl.program_id(0)

        @pl.when(i == 0)
        def _():
            acc_o[...] = jnp.zeros_like(acc_o)

        dxn1 = _dot_nt(dgp_r[...], wpg_r[...])
        x1 = _f32(x1_r)
        r1 = _rstd(x1)
        xh1 = x1 * r1
        acc_o[0:1, 0:D] += jnp.sum(dxn1 * xh1, axis=0, keepdims=True)
        dx1 = dx2_r[...] + _norm_bwd(dxn1, xh1, r1, gple_r[...])
        dx1_o[...] = dx1
        dx1b = dx1.astype(MXU)
        dx1b_o[...] = dx1b

        dmg = _dot_nt(dx1b, wo_r[...]).astype(MXU)
        sga, dsga = _gate(ga_r[...])
        sgb, dsgb = _gate(gb_r[...])
        dya = dmg * sga
        dyb = dmg * sgb
        dya_o[...] = dya
        dyb_o[...] = dyb
        dproj_o[:, C_GA:C_GA + D] = dmg * ya_r[...] * dsga
        dproj_o[:, C_GB:C_GB + D] = dmg * yb_r[...] * dsgb

        db_pre = _dot_nt(dyb, wb_r[...]).astype(MXU)
        hn = _head_norm(_f32(h_r), gh_r[...])
        hbn = jnp.concatenate([xh * g for xh, _, g in hn], axis=1).astype(MXU)
        so, dso = _gate(o_r[...])
        zb = zb_r[...]
        szb, dszb = _gate(zb)
        sb = zb * szb
        t1 = db_pre * hbn
        dproj_o[:, C_O:C_O + VD] = t1 * (sb * dso)
        dproj_o[:, C_ZB:C_ZB + VD] = t1 * (so * (szb + zb * dszb))
        dhbn = (db_pre * (so * sb)).astype(F32)
        for j, (xh, rj, g) in enumerate(hn):
            dj = dhbn[:, j * DV:(j + 1) * DV]
            acc_o[1:2, j * DV:(j + 1) * DV] += jnp.sum(dj * xh, axis=0, keepdims=True)
            dh_o[:, j * DV:(j + 1) * DV] = _norm_bwd(dj, xh, rj, g).astype(MXU)

        da_pre = _dot_nt(dya, wa_r[...]).astype(MXU)
        u, u1, u2 = _conv_inputs(i, tm, xa_r, ca_r, xah_r, cah_r)
        cwv = cw_r[...]
        cv = cwv[0:1, :] * u2 + cwv[1:2, :] * u1 + cwv[2:3, :] * u + cb_r[...]
        za = za_r[...]
        sza, dsza = _gate(za)
        sa = za * sza
        ba = ba_r[...]
        t2 = da_pre * cv.astype(MXU)
        dproj_o[:, C_BA:C_BA + D] = t2 * sa
        dproj_o[:, C_ZA:C_ZA + D] = t2 * (ba * (sza + za * dsza))
        dcv_b = da_pre * (ba * sa)
        dcv_o[...] = dcv_b
        dcv = dcv_b.astype(F32)
        acc_o[2:3, 0:D] += jnp.sum(dcv, axis=0, keepdims=True)
        acc_o[3:4, 0:D] += jnp.sum(dcv * u2, axis=0, keepdims=True)
        acc_o[4:5, 0:D] += jnp.sum(dcv * u1, axis=0, keepdims=True)
        acc_o[5:6, 0:D] += jnp.sum(dcv * u, axis=0, keepdims=True)

    row = lambda w, dt: (pl.BlockSpec((tm, w), lambda i: (i, 0)), jax.ShapeDtypeStruct((n, w), dt))
    outs = [(pl.BlockSpec((tm, TAIL_W), lambda i: (i, 0)), jax.ShapeDtypeStruct((n, NMAIN), MXU)),
            row(D, MXU), row(VD, MXU), row(D, F32), row(D, MXU), row(D, MXU), row(D, MXU),
            (_const((8, VD)), jax.ShapeDtypeStruct((8, VD), F32))]
    rowin = lambda w: pl.BlockSpec((tm, w), lambda i: (i, 0))
    return pl.pallas_call(
        body, name="tail_bwd", grid=(n // tm,),
        in_specs=_tail_in_specs(tm) + [
                  rowin(VD), rowin(D), rowin(D), rowin(D), rowin(D), rowin(D),
                  _const((8, D)), _const((1, D)), _const((1, VD)), _const((1, D)),
                  _const((D, D)), _const((D, D)), _const((VD, D)), _const((D, D))],
        out_specs=[s for s, _ in outs],
        out_shape=[s for _, s in outs],
        compiler_params=_cparams("arbitrary"),
    )(proj, proj, proj, h, dgp, dx2, x1, ya, yb, cw, cb, gh, gple, wpg, wo, wb, wa)


def _conv_bwd(proj, dcv, cw, dproj):
    n = dcv.shape[0]
    tm = min(512, n)
    nt = n // tm

    def body(xa_r, ca_r, dcv_r, nxt_r, cw_r, dproj_in, dxc_o):
        del dproj_in
        i = pl.program_id(0)
        dcv_v = _f32(dcv_r)
        nxt = jnp.where(i < nt - 1, _f32(nxt_r), 0.0)
        rid = lax.broadcasted_iota(jnp.int32, dcv_v.shape, 0)
        d1 = jnp.where(rid == tm - 1, nxt[0:1, :], pltpu.roll(dcv_v, tm - 1, 0))
        d2 = jnp.where(rid == tm - 2, nxt[0:1, :],
                       jnp.where(rid == tm - 1, nxt[1:2, :], pltpu.roll(dcv_v, tm - 2, 0)))
        cwv = cw_r[...]
        du = cwv[2:3, :] * dcv_v + cwv[1:2, :] * d1 + cwv[0:1, :] * d2
        dxc_o[:, 0:D] = (du * _f32(ca_r)).astype(MXU)
        dxc_o[:, D:2 * D] = (du * _f32(xa_r)).astype(MXU)

    return pl.pallas_call(
        body, name="conv_bwd", grid=(nt,),
        in_specs=[_proj_spec(tm, C_XA, 1024), _proj_spec(tm, C_CA, 1024),
                  pl.BlockSpec((tm, D), lambda i: (i, 0)),
                  pl.BlockSpec((8, D), lambda i: (jnp.minimum((i + 1) * (tm // 8), n // 8 - 1), 0)),
                  _const((8, D)), ANY],
        out_specs=pl.BlockSpec((tm, CONV_W), lambda i: (i, C_XA // CONV_W)),
        out_shape=jax.ShapeDtypeStruct(dproj.shape, dproj.dtype),
        input_output_aliases={5: 0},
        compiler_params=_cparams("arbitrary"),
    )(proj, proj, dcv, dcv, cw, dproj)


def _position():
    return lax.axis_index("x"), lax.axis_index("y"), lax.axis_index("c")


def _flip(v, bit):
    return 1 - v if bit else v


def _part_start(core, stride, tiled):
    return pl.multiple_of(core * stride, stride) if tiled else core * stride


def _scatter_copies(srcs, dsts, strides, send_sems, recv_sems, local_sems):
    x, y, c = _position()
    me = 4 * x + 2 * y + c
    na = len(srcs)
    copies = []
    for r in range(N_DEV):
        px, py, pc = _flip(x, (r >> 2) & 1), _flip(y, (r >> 1) & 1), _flip(c, r & 1)
        for a in range(na):
            rows = dsts[a].shape[1]
            src = srcs[a].at[2 * px + py, pl.ds(_part_start(pc, strides[a], len(dsts[a].shape) == 3), rows)]
            dst = dsts[a].at[me]
            if r == 0:
                copies.append(pltpu.make_async_copy(src, dst, local_sems.at[a]))
            else:
                k = (r - 1) * na + a
                copies.append(pltpu.make_async_remote_copy(
                    src_ref=src, dst_ref=dst, send_sem=send_sems.at[k], recv_sem=recv_sems.at[k],
                    device_id=(px, py, pc), device_id_type=MESH))
    return copies


def _input_grad(dproj, dgates, w_t, wg_t, x, dx1, g_mix, g_in, g_rest):
    n = x.shape[0]
    tm, tk = min(1024, n), 2048
    nk = NMAIN // tk
    nt = n // tm

    def body(dp_r, dg_r, w_r, wg_r, x_r, dx1_r, g_r, gin, grest,
             gx_o, acc_o, oin, orest, acc, send_sems, recv_sems, local_sems):
        i = pl.program_id(0)
        kk = pl.program_id(1)
        copies = _scatter_copies((gin, grest), (oin, orest), strides, send_sems, recv_sems, local_sems)

        @pl.when((i == 0) & (kk == 0))
        def _():
            acc_o[...] = jnp.zeros_like(acc_o)
            for cp in copies:
                cp.start()

        @pl.when(kk == 0)
        def _():
            acc[...] = _dot(dg_r[...].astype(MXU), wg_r[...])

        acc[...] += _dot(dp_r[...], w_r[...])

        @pl.when(kk == nk - 1)
        def _():
            dhn = acc[...]
            xv = x_r[...]
            r0 = _rstd(xv)
            xh = xv * r0
            acc_o[0:1, :] += jnp.sum(dhn * xh, axis=0, keepdims=True)
            gx_o[...] = dx1_r[...] + _norm_bwd(dhn, xh, r0, g_r[...])

        @pl.when((i == nt - 1) & (kk == nk - 1))
        def _():
            for cp in copies:
                cp.wait()

    nrem = 2 * (N_DEV - 1)
    r_in, r_rest = HALF_SLABS, g_rest.shape[1] // 2
    strides = (HALF_START, r_rest)
    return pl.pallas_call(
        body, name="input_grad", grid=(nt, nk),
        in_specs=[pl.BlockSpec((tm, tk), lambda i, kk: (i, kk)),
                  pl.BlockSpec((tm, GATE_W), lambda i, kk: (i, 0)),
                  pl.BlockSpec((tk, D), lambda i, kk: (kk, 0)),
                  pl.BlockSpec((GATE_W, D), lambda i, kk: (0, 0)),
                  pl.BlockSpec((tm, D), lambda i, kk: (i, 0)),
                  pl.BlockSpec((tm, D), lambda i, kk: (i, 0)),
                  pl.BlockSpec((1, D), lambda i, kk: (0, 0)),
                  ANY, ANY],
        out_specs=[pl.BlockSpec((tm, D), lambda i, kk: (i, 0)),
                   pl.BlockSpec((8, D), lambda i, kk: (0, 0)),
                   ANY, ANY],
        out_shape=[jax.ShapeDtypeStruct((n, D), F32), jax.ShapeDtypeStruct((8, D), F32),
                   jax.ShapeDtypeStruct((N_DEV, r_in) + g_in.shape[2:], g_in.dtype),
                   jax.ShapeDtypeStruct((N_DEV, r_rest) + g_rest.shape[2:], g_rest.dtype)],
        scratch_shapes=[pltpu.VMEM((tm, D), F32),
                        pltpu.SemaphoreType.DMA((nrem,)), pltpu.SemaphoreType.DMA((nrem,)),
                        pltpu.SemaphoreType.DMA((2,))],
        compiler_params=_cparams("arbitrary", "arbitrary"),
    )(dproj, dgates, w_t, wg_t, x, dx1, g_mix, g_in, g_rest)


def _pack_small(acc_f, acc_b, acc_x, gsum):
    def body(f_r, b_r, x_r, s_r, o_r):
        o_r[...] = jnp.zeros_like(o_r)
        o_r[0:1, :] = x_r[0:1, :]
        o_r[1:2, :] = b_r[2:3, 0:D]
        o_r[2:3, :] = b_r[1:2, 0:D]
        o_r[3:4, :] = b_r[1:2, D:2 * D]
        o_r[4:5, :] = b_r[0:1, 0:D]
        o_r[5:6, :] = f_r[0:1, :]
        o_r[6:9, :] = b_r[3:6, 0:D]
        lane = lax.broadcasted_iota(jnp.int32, (1, D), 1)
        o_r[9:10, :] = jnp.where(lane < 2 * NH, s_r[0:1, :], jnp.where(lane == 2 * NH, f_r[1:2, :], 0.0))

    return pl.pallas_call(
        body, name="pack_small",
        out_shape=jax.ShapeDtypeStruct((SMALL_ROWS, D), F32),
    )(acc_f, acc_b, acc_x, gsum)


def _sum_slots(r, name, tr=64, half=None):
    s, rows, w = r.shape
    tr = min(tr, rows)
    assert rows % tr == 0
    nt = rows // tr

    def body(*refs):
        r_ref, o_ref = refs[-2:]
        tot = r_ref[0].astype(F32)
        for k in range(1, s):
            tot = tot + r_ref[k].astype(F32)
        o_ref[...] = tot

    if half is None:
        return pl.pallas_call(
            body, name=name, grid=(nt,),
            in_specs=[pl.BlockSpec((s, tr, w), lambda i: (0, i, 0))],
            out_specs=pl.BlockSpec((tr, w), lambda i: (i, 0)),
            out_shape=jax.ShapeDtypeStruct((rows, w), F32),
            compiler_params=_cparams("arbitrary"),
        )(r)
    return pl.pallas_call(
        body, name=name,
        grid_spec=pltpu.PrefetchScalarGridSpec(
            num_scalar_prefetch=1, grid=(nt,),
            in_specs=[pl.BlockSpec((s, tr, w), lambda i, hf: (0, i, 0))],
            out_specs=pl.BlockSpec((tr, w), lambda i, hf: (hf[0] * nt + i, 0))),
        out_shape=jax.ShapeDtypeStruct((2 * rows, w), F32),
        compiler_params=_cparams("arbitrary"),
    )(half, r)


def _cast_slabs(a, tr=163):
    rows = a.shape[0]
    assert rows % tr == 0

    def body(a_ref, o_ref):
        o_ref[...] = a_ref[...].astype(MXU)

    spec = pl.BlockSpec((tr,) + a.shape[1:], lambda i: (i, 0, 0))
    return pl.pallas_call(
        body, name="cast_w_in", grid=(rows // tr,), in_specs=[spec], out_specs=spec,
        out_shape=jax.ShapeDtypeStruct(a.shape, MXU), compiler_params=_cparams("arbitrary"),
    )(a)


def _sum_slabs(r, name, tr=69):
    s, rows = r.shape[:2]
    assert rows % tr == 0

    def body(r_ref, o_ref):
        tot = r_ref[0].astype(F32)
        for k in range(1, s):
            tot = tot + r_ref[k].astype(F32)
        o_ref[...] = tot

    return pl.pallas_call(
        body, name=name, grid=(rows // tr,),
        in_specs=[pl.BlockSpec((s, tr) + r.shape[2:], lambda i: (0, i, 0, 0))],
        out_specs=pl.BlockSpec((tr,) + r.shape[2:], lambda i: (i, 0, 0)),
        out_shape=jax.ShapeDtypeStruct(r.shape[1:], F32),
        compiler_params=_cparams("arbitrary"),
    )(r)


def _adamw_body(w_r, g_r, m_r, v_r, d_o, m_o, v_o):
    c1 = 1.0 - ADAM_B1 ** ADAM_STEP
    c2 = 1.0 - ADAM_B2 ** ADAM_STEP
    gv = g_r[...]
    mn = ADAM_B1 * m_r[...] + (1.0 - ADAM_B1) * gv
    vn = ADAM_B2 * v_r[...] + (1.0 - ADAM_B2) * (gv * gv)
    m_o[...] = mn
    v_o[...] = vn
    d_o[...] = -ADAM_LR * ((mn / c1) / (jnp.sqrt(vn / c2) + ADAM_EPS) + ADAM_WD * w_r[...])


def _adamw_slabs(w, g, m, v, name, tr):
    rows = w.shape[0]
    assert rows % tr == 0 and w.shape == g.shape

    def body(*refs):
        _adamw_body(*refs)

    spec = pl.BlockSpec((tr,) + w.shape[1:], lambda i: (i, 0, 0))
    shp = jax.ShapeDtypeStruct(w.shape, F32)
    return pl.pallas_call(
        body, name=name, grid=(rows // tr,),
        in_specs=[spec] * 4, out_specs=[spec] * 3, out_shape=[shp] * 3,
        compiler_params=_cparams("arbitrary"),
    )(w, g, m, v)


def _adamw(w, g, m, v, name, g_row0=0, tr=64):
    lead = w.ndim == 3
    rows, cols = w.shape[-2:]
    tr = min(tr, rows)
    assert rows % tr == 0 and g_row0 % tr == 0 and g.shape[1] == cols

    def body(*refs):
        _adamw_body(*refs)

    if lead:
        spec = pl.BlockSpec((None, tr, cols), lambda i: (0, i, 0))
    else:
        spec = pl.BlockSpec((tr, cols), lambda i: (i, 0))
    g_spec = pl.BlockSpec((tr, cols), lambda i: (g_row0 // tr + i, 0))
    shp = jax.ShapeDtypeStruct(w.shape, F32)
    return pl.pallas_call(
        body, name=name, grid=(rows // tr,),
        in_specs=[spec, g_spec, spec, spec], out_specs=[spec] * 3, out_shape=[shp] * 3,
        compiler_params=_cparams("arbitrary"),
    )(w, g, m, v)


def _allgather_chips(halved, whole):
    arrs = [a for a, _ in halved] + list(whole)
    parts = [p for _, p in halved]
    nh, na = len(halved), len(arrs)
    relations = ((1, 0), (0, 1), (1, 1))
    n_ici = 3 * na

    def body(*refs):
        ins, outs = refs[:na], refs[na:2 * na]
        send_sems, recv_sems = refs[2 * na:]
        x, y, c = _position()
        mine = 2 * x + y

        def rows(a, core, stride, size):
            return pl.ds(_part_start(core, stride, len(ins[a].shape) == 2), size)

        def remote(src, dst, k, to):
            return pltpu.make_async_remote_copy(src_ref=src, dst_ref=dst, send_sem=send_sems.at[k],
                                                recv_sem=recv_sems.at[k], device_id=to, device_id_type=MESH)

        first, passed = [], []
        for j, (rx, ry) in enumerate(relations):
            px, py = _flip(x, rx), _flip(y, ry)
            for a in range(na):
                if a < nh:
                    s1, n1, s2, n2 = parts[a]
                    first.append(remote(ins[a].at[rows(a, c, s1, n1)], outs[a].at[mine, rows(a, c, s1, n1)],
                                        j * na + a, (px, py, c)))
                    landed = outs[a].at[2 * px + py, rows(a, c, s2, n2)]
                    passed.append(remote(landed, landed, n_ici + j * nh + a, (x, y, 1 - c)))
                else:
                    first.append(remote(ins[a], outs[a].at[mine], j * na + a, (px, py, c)))
        for cp in first:
            cp.start()
        k = 0
        for j in range(3):
            for a in range(na):
                first[j * na + a].wait_recv()
                if a < nh:
                    passed[k].start()
                    k += 1
        for cp in passed:
            cp.wait_recv()
        for cp in first + passed:
            cp.wait_send()

    nsem = n_ici + 3 * nh
    outs = pl.pallas_call(
        body, name="allgather_weights",
        in_specs=[ANY] * na, out_specs=[ANY] * na,
        out_shape=[jax.ShapeDtypeStruct((N_CHIPS,) + a.shape, a.dtype) for a in arrs],
        scratch_shapes=[pltpu.SemaphoreType.DMA((nsem,)), pltpu.SemaphoreType.DMA((nsem,))],
    )(*arrs)
    chip = 2 * lax.axis_index("x") + lax.axis_index("y")
    return [lax.dynamic_update_slice(o, a[None], (chip,) + (0,) * a.ndim) for o, a in zip(outs, arrs)]


def _join_halves(piece, full_rest, small):
    def body(pc_ref, rest_in, sm, oin, orest, osm, send_sems, recv_sems, local_sem):
        del rest_in
        x, y, c = _position()
        me = 4 * x + 2 * y + c
        sib = (x, y, 1 - c)
        rows = orest.shape[0] // 2
        mine = orest.at[pl.ds(pl.multiple_of(c * rows, rows), rows)]
        copies = [
            pltpu.make_async_remote_copy(
                src_ref=pc_ref.at[pl.ds(c, HALF_START)], dst_ref=oin.at[pl.ds(c * HALF_SLABS, HALF_START)],
                send_sem=send_sems.at[0], recv_sem=recv_sems.at[0], device_id=sib, device_id_type=MESH),
            pltpu.make_async_remote_copy(
                src_ref=mine, dst_ref=mine, send_sem=send_sems.at[1], recv_sem=recv_sems.at[1],
                device_id=sib, device_id_type=MESH)]
        na = 2
        copies.append(pltpu.make_async_copy(sm, osm.at[me], local_sem))
        for r in range(1, N_DEV):
            to = (_flip(x, (r >> 2) & 1), _flip(y, (r >> 1) & 1), _flip(c, r & 1))
            copies.append(pltpu.make_async_remote_copy(
                src_ref=sm, dst_ref=osm.at[me], send_sem=send_sems.at[na + r - 1],
                recv_sem=recv_sems.at[na + r - 1], device_id=to, device_id_type=MESH))
        for cp in copies:
            cp.start()
        for cp in copies:
            cp.wait()

    nsem = 2 + N_DEV - 1
    return pl.pallas_call(
        body, name="join_halves",
        in_specs=[ANY] * 3, out_specs=[ANY] * 3,
        out_shape=[jax.ShapeDtypeStruct((SLABS_CHIP,) + piece.shape[1:], piece.dtype),
                   jax.ShapeDtypeStruct(full_rest.shape, full_rest.dtype),
                   jax.ShapeDtypeStruct((N_DEV,) + small.shape, small.dtype)],
        scratch_shapes=[pltpu.SemaphoreType.DMA((nsem,)), pltpu.SemaphoreType.DMA((nsem,)),
                        pltpu.SemaphoreType.DMA],
        input_output_aliases={1: 1},
    )(piece, full_rest, small)


def _to_internal(w_slabs):
    order = sorted(SEGMENTS, key=lambda s: s[2])
    main = jnp.concatenate([w_slabs[g0 // 2:(g0 + w) // 2] for g0, w, _ in order], axis=0)
    gate = w_slabs[GATE_COL // 2:GATE_COL // 2 + NH].reshape(2 * NH, D)
    return main.reshape(NMAIN, D), jnp.pad(gate, ((0, GATE_W - 2 * NH), (0, 0)))


def _to_global(main_t, gate_t):
    main = main_t.reshape(NMAIN // 2, 16, D // 8)
    parts = sorted([(g0, main[i0 // 2:(i0 + w) // 2]) for g0, w, i0 in SEGMENTS]
                   + [(GATE_COL, gate_t[0:2 * NH].reshape(NH, 16, D // 8))], key=lambda s: s[0])
    return jnp.concatenate([p for _, p in parts], axis=0)


def _pack_rows(wa, wb, wo, wpg, wp):
    return jnp.concatenate([wa, wb, wo, wpg, wp.reshape(PACK_ROWS[4], D)], axis=0)


def _pad_rows(a, rows=8):
    return jnp.pad(a, ((0, rows - a.shape[0]), (0, 0)))


def kernel(x, p, g_mix, w_in, conv_w, conv_b, w_a_out, b_gates, g_head, w_b_out, w_o, g_ple, w_ple_gate, w_ple, g_final, loss_target, m_g_mix, m_w_in, m_conv_w, m_conv_b, m_w_a_out, m_b_gates, m_g_head, m_w_b_out, m_w_o, m_g_ple, m_w_ple_gate, m_w_ple, m_g_final, v_g_mix, v_w_in, v_conv_w, v_conv_b, v_w_a_out, v_b_gates, v_g_head, v_w_b_out, v_w_o, v_g_ple, v_w_ple_gate, v_w_ple, v_g_final):
    chip = 2 * lax.axis_index("x") + lax.axis_index("y")
    xs, ps, ts = x[0], p[0, 0], loss_target[0]
    g_fin = g_final.reshape(1, D)

    pack_w = _pack_rows(w_a_out[0], w_b_out[0], w_o[0], w_ple_gate[0], w_ple[0])
    w_slabs = _cast_slabs(jnp.transpose(w_in, (2, 0, 1)).reshape(SLABS_CHIP, 16, D // 8))
    half_rows = PACK_TOTAL // 2
    g_win, g_pack, g_cw = _allgather_chips(
        [(w_slabs, (HALF_START, HALF_SLABS, HALF_SLABS, HALF_START)),
         (pack_w.astype(MXU), (half_rows, half_rows, half_rows, half_rows))], [_pad_rows(conv_w[0])])
    w_t, wg_t = _to_internal(g_win.reshape(N_IN // 2, 16, D // 8))
    offs = [0, 256, 768, 1024, 1280, 1344]
    wa, wb, wo, wpg = [g_pack[:, offs[k]:offs[k + 1]].reshape(-1, D) for k in range(4)]
    wp = jnp.transpose(g_pack[:, offs[4]:offs[5]].reshape(N_CHIPS, PLE, PLE), (1, 0, 2)).reshape(PLE, D)
    cw = jnp.transpose(g_cw, (1, 0, 2)).reshape(8, D)

    bias = jnp.pad(b_gates, ((0, 0), (0, GATE_W - 2 * NH)))
    proj, hn, gates = _proj(xs, g_mix, w_t, wg_t)
    h, stats, cs, ns, ms = _mlstm_fwd(proj, gates, bias)
    (a_pre, b_pre, mg, xn1, de, dgp, ya, yb, x1, dx2, acc_f) = _tail_fwd(
        proj, h, xs, ps, ts, cw, conv_b, g_head, g_ple, g_fin, wa, wb, wo, wpg, wp)
    dproj, dcv, dh, dx1, dx1b, dya, dyb, acc_b = _tail_bwd(
        proj, h, dgp, dx2, x1, ya, yb, cw, conv_b, g_head, g_ple, wpg, wo, wb, wa)
    dproj = _conv_bwd(proj, dcv, cw, dproj)
    dproj, dgates, gsum = _mlstm_bwd(proj, gates, bias, h, dh, stats, cs, ns, ms, dproj)
    d_main = _matmul_tn(dproj, hn, "dw_in", out_dtype=WIRE, tk=4096)
    d_gate = _matmul_tn(dgates, hn, "dw_gate", out_dtype=WIRE)
    d_wa = _matmul_tn(a_pre, dya, "dw_a_out", out_dtype=WIRE)
    d_wb = _matmul_tn(b_pre, dyb, "dw_b_out", out_dtype=WIRE)
    d_wo = _matmul_tn(mg, dx1b, "dw_o", out_dtype=WIRE)
    d_wpg = _matmul_tn(xn1, dgp, "dw_ple_gate", out_dtype=WIRE)
    d_wp = _matmul_tn(ps, de, "dw_ple", out_dtype=WIRE)

    g_in = _to_global(d_main, d_gate).reshape(N_CHIPS, SLABS_CHIP, 16, D // 8)
    d_wp_c = jnp.transpose(d_wp.reshape(PLE, N_CHIPS, PLE), (1, 0, 2)).reshape(N_CHIPS, PACK_ROWS[4], D)
    g_rest = jnp.concatenate(
        [d_wa.reshape(N_CHIPS, -1, D), d_wb.reshape(N_CHIPS, -1, D), d_wo.reshape(N_CHIPS, -1, D),
         d_wpg.reshape(N_CHIPS, -1, D), d_wp_c], axis=1)
    grad_x, acc_x, r_in, r_rest = _input_grad(dproj, dgates, w_t, wg_t, xs, dx1, g_mix, g_in, g_rest)
    small = _pack_small(acc_f, acc_b, acc_x, gsum)
    core = lax.axis_index("c").astype(jnp.int32)
    piece = _sum_slabs(r_in, "sum_w_in")
    gw_in, gw_rest, r_small = _join_halves(
        piece, _sum_slots(r_rest, "sum_rest", tr=96, half=core.reshape(1)), small)
    gw_in = lax.dynamic_update_slice(gw_in, piece, (core * HALF_START, 0, 0)).reshape(COLS_CHIP, 8, D // 8)
    gs = _sum_slots(r_small, "sum_small", tr=SMALL_ROWS)

    big = []
    row0 = 0
    for name, w, m, v in (("w_a_out", w_a_out, m_w_a_out, v_w_a_out), ("w_b_out", w_b_out, m_w_b_out, v_w_b_out),
                          ("w_o", w_o, m_w_o, v_w_o), ("w_ple_gate", w_ple_gate, m_w_ple_gate, v_w_ple_gate)):
        big.append((name, w, m, v, gw_rest, row0))
        row0 += w.shape[1]
    g_wp = gw_rest[row0:row0 + PACK_ROWS[4]].reshape(PLE, PLE)
    big.append(("w_ple", w_ple, m_w_ple, v_w_ple, g_wp, 0))
    upd = {name: _adamw(w, g, m, v, "adamw_" + name, g_row0=r0) for name, w, m, v, g, r0 in big}
    g_big = {name: (g if name == "w_ple" else g[r0:r0 + w.shape[1]])[None] for name, w, m, v, g, r0 in big}
    slabs = lambda a: jnp.transpose(a, (2, 0, 1)).reshape(COLS_CHIP, 8, D // 8)
    unslab = lambda a: jnp.transpose(a, (1, 2, 0)).reshape(1, D, COLS_CHIP)
    upd["w_in"] = [unslab(u) for u in _adamw_slabs(
        slabs(w_in), gw_in, slabs(m_w_in), slabs(v_w_in), "adamw_w_in", tr=326)]
    g_big["w_in"] = unslab(gw_in)

    lane = lax.broadcasted_iota(jnp.int32, (1, D), 1)
    g_small = jnp.concatenate([gs[0:6], jnp.where(lane < 2 * NH, gs[9:10], 0.0), jnp.zeros((1, D), F32)], axis=0)

    def small_pack(gm, cb_, bg, gh, gp, gf):
        return jnp.concatenate([gm, cb_, gh.reshape(2, D), gp, gf.reshape(1, D),
                                jnp.pad(bg, ((0, 0), (0, D - 2 * NH))), jnp.zeros((1, D), F32)], axis=0)

    ws = small_pack(g_mix, conv_b, b_gates, g_head, g_ple, g_final)
    ms_ = small_pack(m_g_mix, m_conv_b, m_b_gates, m_g_head, m_g_ple, m_g_final)
    vs = small_pack(v_g_mix, v_conv_b, v_b_gates, v_g_head, v_g_ple, v_g_final)
    upd_s = _adamw(ws, g_small, ms_, vs, "adamw_small")
    g_cw_mine = _pad_rows(lax.dynamic_slice(gs[6:9], (0, chip * PLE), (3, PLE)))
    upd_c = _adamw(_pad_rows(conv_w[0]), g_cw_mine, _pad_rows(m_conv_w[0]), _pad_rows(v_conv_w[0]),
                   "adamw_conv_w")

    def leaves(bigs, sm, cwv):
        return [sm[0:1], bigs["w_in"], cwv[0:3][None], sm[1:2], bigs["w_a_out"], sm[6:7, 0:2 * NH],
                sm[2:4].reshape(1, VD), bigs["w_b_out"], bigs["w_o"], sm[4:5], bigs["w_ple_gate"],
                bigs["w_ple"], sm[5]]

    loss = gs[9, 2 * NH]
    outs = [loss, grad_x[None]] + leaves(g_big, g_small, g_cw_mine)
    for k in range(3):
        outs += leaves({name: u[k] for name, u in upd.items()}, upd_s[k], upd_c[k])
    return tuple(outs)
```

```python
import jax
import jax.numpy as jnp
from jax import lax
from jax.experimental import pallas as pl
from jax.experimental.pallas import tpu as pltpu

F32 = jnp.float32
MXU = jnp.bfloat16
WIRE = jnp.bfloat16

D = 1024
NH, DK, DV = 4, 256, 512
VD = NH * DV
PLE = 256
LCH = 256
EPS = 1e-6
N_IN = 14344
NMAIN = 14336
GATE_W = 128
N_CHIPS, N_DEV = 4, 8

C_BA, C_ZA, C_O, C_ZB, C_GA, C_GB = 0, 1024, 2048, 4096, 6144, 7168
QK = NH * DK
C_Q, C_K, C_V, C_XA, C_CA = 8192, 9216, 10240, 12288, 13312
QKV_W = 2 * QK + VD
TAIL_W = 8192
CONV_W = 2048
SEGMENTS = (
    (0, 1024, C_XA), (1024, 1024, C_BA), (2048, 1024, C_CA), (3072, 1024, C_ZA),
    (4096, QKV_W, C_Q), (8192, 4096, C_O), (12296, 2048, C_GA),
)
GATE_COL = 12288

COLS_CHIP = N_IN // N_CHIPS
SLABS_CHIP = COLS_CHIP // 2
HALF_START = SLABS_CHIP // 2
HALF_SLABS = SLABS_CHIP - HALF_START

PACK_ROWS = (256, 512, 256, 256, 64)
SMALL_ROWS = 16

ADAM_LR, ADAM_B1, ADAM_B2, ADAM_EPS, ADAM_WD, ADAM_STEP = 0.001, 0.9, 0.999, 1e-08, 0.01, 10

VMEM_LIMIT = 56 * 1024 * 1024
MESH = pl.DeviceIdType.MESH
ANY = pl.BlockSpec(memory_space=pl.ANY)


def _cparams(*sem):
    return pltpu.CompilerParams(dimension_semantics=sem, vmem_limit_bytes=VMEM_LIMIT)


def _dot(a, b):
    return jnp.dot(a, b, preferred_element_type=F32)


def _dot_nt(a, b):
    return lax.dot_general(a, b, (((1,), (1,)), ((), ())), preferred_element_type=F32)


def _dot_tn(a, b):
    return lax.dot_general(a, b, (((0,), (0,)), ((), ())), preferred_element_type=F32)


def _sigmoid(x):
    return 1.0 / (1.0 + jnp.exp(-x))


def _logsig(x):
    return jnp.minimum(x, 0.0) - jnp.log(1.0 + jnp.exp(-jnp.abs(x)))


GATE_FLOOR = -80.0


def _gate(v):
    e = jnp.exp(-jnp.maximum(v, GATE_FLOOR))
    s = 1.0 / (1.0 + e)
    return s, e * s * s


def _rstd(x):
    return lax.rsqrt(jnp.mean(x * x, axis=-1, keepdims=True) + EPS)


def _norm_bwd(dy, xhat, r, g):
    dxh = dy * g
    return r * (dxh - xhat * jnp.mean(dxh * xhat, axis=-1, keepdims=True))


def _f32(ref):
    return ref[...].astype(F32)


def _proj(x, g_mix, w_t, wg_t, tm=1024, tn=2048):
    n = x.shape[0]
    m = w_t.shape[0]
    tm = min(tm, n)

    def body(x_ref, g_ref, b_ref, wg_ref, o_ref, hn_ref, gate_ref):
        @pl.when(pl.program_id(1) == 0)
        def _():
            xv = x_ref[...]
            hn = (xv * _rstd(xv) * g_ref[...]).astype(MXU)
            hn_ref[...] = hn
            gate_ref[...] = _dot_nt(hn, wg_ref[...])

        o_ref[...] = _dot_nt(hn_ref[...], b_ref[...]).astype(MXU)

    return pl.pallas_call(
        body, name="proj", grid=(n // tm, m // tn),
        in_specs=[pl.BlockSpec((tm, D), lambda i, j: (i, 0)),
                  pl.BlockSpec((1, D), lambda i, j: (0, 0)),
                  pl.BlockSpec((tn, D), lambda i, j: (j, 0)),
                  pl.BlockSpec((GATE_W, D), lambda i, j: (0, 0))],
        out_specs=[pl.BlockSpec((tm, tn), lambda i, j: (i, j)),
                   pl.BlockSpec((tm, D), lambda i, j: (i, 0)),
                   pl.BlockSpec((tm, GATE_W), lambda i, j: (i, 0))],
        out_shape=[jax.ShapeDtypeStruct((n, m), MXU), jax.ShapeDtypeStruct((n, D), MXU),
                   jax.ShapeDtypeStruct((n, GATE_W), F32)],
        compiler_params=_cparams("arbitrary", "arbitrary"),
    )(x, g_mix, w_t, wg_t)


def _matmul_tn(a, b, name, out_dtype=F32, ta=1024, tb=1024, tk=2048):
    n, ka = a.shape
    kb = b.shape[1]
    ta, tb, tk = min(ta, ka), min(tb, kb), min(tk, n)
    nk = n // tk

    def body(a_ref, b_ref, o_ref, acc):
        kk = pl.program_id(2)

        @pl.when(kk == 0)
        def _():
            acc[...] = jnp.zeros_like(acc)

        acc[...] += _dot_tn(a_ref[...].astype(MXU), b_ref[...].astype(MXU))

        @pl.when(kk == nk - 1)
        def _():
            o_ref[...] = acc[...].astype(out_dtype)

    return pl.pallas_call(
        body, name=name, grid=(ka // ta, kb // tb, nk),
        in_specs=[pl.BlockSpec((tk, ta), lambda i, j, kk: (kk, i)),
                  pl.BlockSpec((tk, tb), lambda i, j, kk: (kk, j))],
        out_specs=pl.BlockSpec((ta, tb), lambda i, j, kk: (i, j)),
        out_shape=jax.ShapeDtypeStruct((ka, kb), out_dtype),
        scratch_shapes=[pltpu.VMEM((ta, tb), F32)],
        compiler_params=_cparams("arbitrary", "arbitrary", "arbitrary"),
    )(a, b)


def _sub_row(block_t, j):
    sub = lax.broadcasted_iota(jnp.int32, block_t.shape, 0)
    return jnp.sum(jnp.where(sub == j, block_t, 0.0), axis=0, keepdims=True)


def _chunk_cumsum(g):
    n = g.shape[0]
    r = lax.broadcasted_iota(jnp.int32, (n, n), 0)
    c = lax.broadcasted_iota(jnp.int32, (n, n), 1)
    return jnp.dot(jnp.where(r >= c, 1.0, 0.0), _logsig(g), precision=lax.Precision.HIGHEST,
                   preferred_element_type=F32)


def _chunk_decay(li_col, li_row, b_col, b_row, m_prev):
    n = li_col.shape[0]
    r = lax.broadcasted_iota(jnp.int32, (n, n), 0)
    c = lax.broadcasted_iota(jnp.int32, (n, n), 1)
    tri = r >= c
    b_last = b_col[n - 1:n, :]
    rr = li_row - b_row
    a_col = b_col + m_prev
    g_col = b_last - b_col + li_col
    m_new = jnp.maximum(b_last + m_prev, jnp.max(g_col, axis=0, keepdims=True))
    w_col = jnp.exp(g_col - m_new)
    decay = jnp.exp(b_last + m_prev - m_new)
    return tri, rr, a_col, m_new, w_col, decay


def _qkv_specs(row_of):
    q_spec = pl.BlockSpec((LCH, QK), lambda c: (row_of(c), C_Q // QK))
    k_spec = pl.BlockSpec((LCH, QK), lambda c: (row_of(c), C_K // QK))
    v_spec = pl.BlockSpec((LCH, VD), lambda c: (row_of(c), C_V // VD))
    return q_spec, k_spec, v_spec


def _state_specs(row_of):
    return [pl.BlockSpec((NH, None, DK, DV), lambda c: (0, row_of(c), 0, 0)),
            pl.BlockSpec((NH, None, 1, DK), lambda c: (0, row_of(c), 0, 0)),
            pl.BlockSpec((NH, None, 1, GATE_W), lambda c: (0, row_of(c), 0, 0))]


def _lane_put(col, lane_id, width=GATE_W):
    lane = lax.broadcasted_iota(jnp.int32, (col.shape[0], width), 1)
    return jnp.where(lane == lane_id, col, 0.0)


def _lane_get(block, lane_id):
    lane = lax.broadcasted_iota(jnp.int32, block.shape, 1)
    return jnp.sum(jnp.where(lane == lane_id, block, 0.0), axis=1, keepdims=True)


def _mlstm_fwd(proj, gates, bias):
    n = proj.shape[0]
    nc = n // LCH

    def body(q_ref, k_ref, v_ref, g_ref, bias_ref,
             h_ref, st_ref, cs_ref, ns_ref, ms_ref, c_scr, n_scr, m_scr):
        @pl.when(pl.program_id(0) == 0)
        def _():
            c_scr[...] = jnp.zeros_like(c_scr)
            n_scr[...] = jnp.zeros_like(n_scr)
            m_scr[...] = jnp.full_like(m_scr, -jnp.inf)

        g = g_ref[...] + bias_ref[...]
        gt = g.T[0:8, :]
        b = _chunk_cumsum(g)
        bt = b.T[0:8, :]
        stats = jnp.zeros((LCH, GATE_W), F32)
        for hd in range(NH):
            m_all = m_scr[hd]
            m_prev = m_all[0:1, 0:1]
            b_col = _lane_get(b, NH + hd)
            tri, rr, a_col, m_new, w_col, decay = _chunk_decay(
                _lane_get(g, hd), _sub_row(gt, hd), b_col, _sub_row(bt, NH + hd), m_prev)
            dmat = jnp.where(tri, b_col + rr, -jnp.inf)
            m_col = jnp.maximum(a_col, jnp.max(dmat, axis=1, keepdims=True))
            dl = jnp.exp(dmat - m_col)
            inter = jnp.exp(a_col - m_col)

            qs = q_ref[:, hd * DK:(hd + 1) * DK] * (DK ** -0.5)
            kk = k_ref[:, hd * DK:(hd + 1) * DK]
            vv = v_ref[:, hd * DV:(hd + 1) * DV]
            cst = c_scr[hd]
            nst = n_scr[hd]
            cs_ref[hd] = cst
            ns_ref[hd] = nst
            ms_ref[hd] = m_all

            sc = _dot_nt(qs, kk) * dl
            num = _dot(sc.astype(MXU), vv) + inter * _dot(qs, cst.astype(MXU))
            den = (jnp.sum(sc, axis=1, keepdims=True)
                   + inter * jnp.sum(qs.astype(F32) * nst, axis=1, keepdims=True))
            nrm = jnp.maximum(jnp.abs(den), jnp.exp(-m_col))
            h_ref[:, hd * DV:(hd + 1) * DV] = (num * (1.0 / nrm)).astype(h_ref.dtype)
            stats = stats + _lane_put(den, hd) + _lane_put(m_col, NH + hd)

            kw = kk.astype(F32) * w_col
            c_scr[hd] = decay * cst + _dot_tn(kw.astype(MXU), vv)
            n_scr[hd] = decay * nst + jnp.sum(kw, axis=0, keepdims=True)
            m_scr[hd] = jnp.broadcast_to(m_new, (1, GATE_W))
        st_ref[...] = stats

    q_spec, k_spec, v_spec = _qkv_specs(lambda c: c)
    return pl.pallas_call(
        body, name="mlstm_fwd", grid=(nc,),
        in_specs=[q_spec, k_spec, v_spec,
                  pl.BlockSpec((LCH, GATE_W), lambda c: (c, 0)),
                  pl.BlockSpec((1, GATE_W), lambda c: (0, 0))],
        out_specs=[pl.BlockSpec((LCH, VD), lambda c: (c, 0)),
                   pl.BlockSpec((LCH, GATE_W), lambda c: (c, 0))] + _state_specs(lambda c: c),
        out_shape=[jax.ShapeDtypeStruct((n, VD), MXU),
                   jax.ShapeDtypeStruct((n, GATE_W), F32),
                   jax.ShapeDtypeStruct((NH, nc, DK, DV), F32),
                   jax.ShapeDtypeStruct((NH, nc, 1, DK), F32),
                   jax.ShapeDtypeStruct((NH, nc, 1, GATE_W), F32)],
        scratch_shapes=[pltpu.VMEM((NH, DK, DV), F32), pltpu.VMEM((NH, 1, DK), F32),
                        pltpu.VMEM((NH, 1, GATE_W), F32)],
        compiler_params=_cparams("arbitrary"),
    )(proj, proj, proj, gates, bias)


def _mlstm_bwd(proj, gates, bias, h, dh, stats, cs, ns, ms, dproj):
    n = proj.shape[0]
    nc = n // LCH

    def body(q_ref, k_ref, v_ref, g_ref, bias_ref, h_ref, dh_ref, st_ref,
             cs_ref, ns_ref, ms_ref, dproj_in,
             dqkv_ref, dg_ref, gsum_ref, dc_scr, dn_scr):
        del dproj_in

        @pl.when(pl.program_id(0) == 0)
        def _():
            dc_scr[...] = jnp.zeros_like(dc_scr)
            dn_scr[...] = jnp.zeros_like(dn_scr)
            gsum_ref[...] = jnp.zeros_like(gsum_ref)

        g = g_ref[...] + bias_ref[...]
        gt = g.T[0:8, :]
        b = _chunk_cumsum(g)
        bt = b.T[0:8, :]
        stats = st_ref[...]
        r = lax.broadcasted_iota(jnp.int32, (LCH, LCH), 0)
        c = lax.broadcasted_iota(jnp.int32, (LCH, LCH), 1)
        sub8 = lax.broadcasted_iota(jnp.int32, (8, LCH), 0)
        last = lax.broadcasted_iota(jnp.int32, (LCH, 1), 0) == LCH - 1
        dli_all = jnp.zeros((LCH, GATE_W), F32)
        db_all = jnp.zeros((LCH, GATE_W), F32)
        colsum_t = jnp.zeros((8, LCH), F32)
        for hd in range(NH):
            m_prev = ms_ref[hd][0:1, 0:1]
            b_col = _lane_get(b, NH + hd)
            tri, rr, a_col, m_new, w_col, decay = _chunk_decay(
                _lane_get(g, hd), _sub_row(gt, hd), b_col, _sub_row(bt, NH + hd), m_prev)
            m_col = _lane_get(stats, NH + hd)
            dl = jnp.where(tri, jnp.exp((b_col - m_col) + rr), 0.0)
            inter = jnp.exp(a_col - m_col)

            qs = q_ref[:, hd * DK:(hd + 1) * DK] * (DK ** -0.5)
            kk = k_ref[:, hd * DK:(hd + 1) * DK]
            vv = v_ref[:, hd * DV:(hd + 1) * DV]
            qf = qs.astype(F32)
            kf = kk.astype(F32)
            cst = cs_ref[hd]
            nst = ns_ref[hd]
            cb = cst.astype(MXU)
            dcn = dc_scr[hd]
            dnn = dn_scr[hd]
            dcb = dcn.astype(MXU)

            den = _lane_get(stats, hd)
            floor = jnp.exp(-m_col)
            nrm = jnp.maximum(jnp.abs(den), floor)
            dhv = dh_ref[:, hd * DV:(hd + 1) * DV].astype(F32)
            rn = 1.0 / nrm
            dnum = dhv * rn
            dnum_b = dnum.astype(MXU)
            dhh = jnp.sum(dhv * h_ref[:, hd * DV:(hd + 1) * DV].astype(F32), axis=1, keepdims=True)
            dden = jnp.where(jnp.abs(den) > floor, -dhh * rn * jnp.sign(den), 0.0)

            sc = _dot_nt(qs, kk) * dl
            dsc = _dot_nt(dnum_b, vv) + dden
            da = (dl * dsc).astype(MXU)
            gmat = sc * dsc

            dq = _dot(da, kk) + inter * (_dot_nt(dnum_b, cb) + dden * nst)
            dk_state = w_col * (_dot_nt(vv, dcb) + dnn)
            dk = _dot_tn(da, qs) + dk_state
            kw = (kf * w_col).astype(MXU)
            dv = _dot_tn(sc.astype(MXU), dnum_b) + _dot(kw, dcb)
            dqkv_ref[:, hd * DK:(hd + 1) * DK] = (dq * (DK ** -0.5)).astype(dqkv_ref.dtype)
            dqkv_ref[:, QK + hd * DK:QK + (hd + 1) * DK] = dk.astype(dqkv_ref.dtype)
            dqkv_ref[:, 2 * QK + hd * DV:2 * QK + (hd + 1) * DV] = dv.astype(dqkv_ref.dtype)

            num_i = _dot(qs, cb)
            den_i = jnp.sum(qf * nst, axis=1, keepdims=True)
            e_col = inter * (jnp.sum(dnum * num_i, axis=1, keepdims=True) + dden * den_i)
            h_col = jnp.sum(kf * dk_state, axis=1, keepdims=True)
            f_dec = decay * (jnp.sum(jnp.sum(cst * dcn, axis=1, keepdims=True), axis=0, keepdims=True)
                             + jnp.sum(nst * dnn, axis=1, keepdims=True))
            row_g = jnp.sum(gmat, axis=1, keepdims=True)
            col_g = jnp.sum(gmat, axis=0, keepdims=True)
            colsum_t = colsum_t + jnp.where(sub8 == hd, col_g, 0.0) - jnp.where(sub8 == NH + hd, col_g, 0.0)
            db_col = row_g + e_col - h_col
            db_col = db_col + jnp.where(last, jnp.sum(h_col, axis=0, keepdims=True) + f_dec, 0.0)
            dli_all = dli_all + _lane_put(h_col, hd)
            db_all = db_all + _lane_put(db_col, NH + hd)

            dc_scr[hd] = decay * dcn + _dot_tn((qf * inter).astype(MXU), dnum_b)
            dn_scr[hd] = decay * dnn + jnp.sum(qf * (inter * dden), axis=0, keepdims=True)
        colsum = jnp.concatenate([colsum_t, jnp.zeros((GATE_W - 8, LCH), F32)], axis=0).T
        lane = lax.broadcasted_iota(jnp.int32, (LCH, GATE_W), 1)
        dli_all = dli_all + jnp.where(lane < NH, colsum, 0.0)
        db_all = db_all + jnp.where(lane >= NH, colsum, 0.0)
        dlf_all = jnp.dot(jnp.where(c >= r, 1.0, 0.0), db_all, precision=lax.Precision.HIGHEST,
                          preferred_element_type=F32)
        dg = dli_all + dlf_all * _sigmoid(-g)
        dg_ref[...] = dg
        gsum_ref[0:1, 0:GATE_W] += jnp.sum(dg, axis=0, keepdims=True)

    rev = lambda c: nc - 1 - c
    q_spec, k_spec, v_spec = _qkv_specs(rev)
    hv_spec = pl.BlockSpec((LCH, VD), lambda c: (rev(c), 0))
    gate_spec = pl.BlockSpec((LCH, GATE_W), lambda c: (rev(c), 0))
    return pl.pallas_call(
        body, name="mlstm_bwd", grid=(nc,),
        in_specs=[q_spec, k_spec, v_spec, gate_spec,
                  pl.BlockSpec((1, GATE_W), lambda c: (0, 0)),
                  hv_spec, hv_spec, gate_spec] + _state_specs(rev) + [ANY],
        out_specs=[pl.BlockSpec((LCH, QKV_W), lambda c: (rev(c), C_Q // QKV_W)),
                   gate_spec,
                   pl.BlockSpec((8, D), lambda c: (0, 0))],
        out_shape=[jax.ShapeDtypeStruct(dproj.shape, dproj.dtype),
                   jax.ShapeDtypeStruct((n, GATE_W), F32),
                   jax.ShapeDtypeStruct((8, D), F32)],
        scratch_shapes=[pltpu.VMEM((NH, DK, DV), F32), pltpu.VMEM((NH, 1, DK), F32)],
        input_output_aliases={11: 0},
        compiler_params=_cparams("arbitrary"),
    )(proj, proj, proj, gates, bias, h, dh, stats, cs, ns, ms, dproj)


def _proj_spec(tm, col, width):
    return pl.BlockSpec((tm, width), lambda i: (i, col // width))


def _tail_in_specs(tm):
    return [_proj_spec(tm, 0, TAIL_W), _proj_spec(tm, C_XA, CONV_W),
            pl.BlockSpec((8, CONV_W), lambda i: (jnp.maximum(i * (tm // 8) - 1, 0), C_XA // CONV_W))]


def _tail_views(tail_r, conv_r, halo_r):
    cols = lambda ref, c0, w: ref.at[:, pl.ds(c0, w)]
    return (cols(tail_r, C_BA, D), cols(tail_r, C_ZA, D), cols(tail_r, C_O, VD), cols(tail_r, C_ZB, VD),
            cols(tail_r, C_GA, D), cols(tail_r, C_GB, D), cols(conv_r, 0, D), cols(conv_r, D, D),
            cols(halo_r, 0, D), cols(halo_r, D, D))


def _const(shape):
    return pl.BlockSpec(shape, lambda i: (0,) * len(shape))


def _conv_inputs(i, tm, xa_ref, ca_ref, xah_ref, cah_ref):
    u = _f32(xa_ref) * _f32(ca_ref)
    uh = jnp.where(i > 0, _f32(xah_ref) * _f32(cah_ref), 0.0)
    rid = lax.broadcasted_iota(jnp.int32, u.shape, 0)
    u1 = jnp.where(rid == 0, uh[7:8, :], pltpu.roll(u, 1, 0))
    u2 = jnp.where(rid == 0, uh[6:7, :], jnp.where(rid == 1, uh[7:8, :], pltpu.roll(u, 2, 0)))
    return u, u1, u2


def _head_norm(hh, gh):
    out = []
    for j in range(NH):
        hj = hh[:, j * DV:(j + 1) * DV]
        rj = _rstd(hj)
        out.append((hj * rj, rj, gh[:, j * DV:(j + 1) * DV]))
    return out


def _tail_fwd(proj, h, x, p, t, cw, cb, gh, gple, gfin, wa, wb, wo, wpg, wp):
    n = x.shape[0]
    tm = min(256, n)

    def body(tail_r, conv_r, halo_r, h_r, x_r, p_r, t_r,
             cw_r, cb_r, gh_r, gple_r, gfin_r, wa_r, wb_r, wo_r, wpg_r, wp_r,
             apre_o, bpre_o, mg_o, xn1_o, de_o, dgp_o, ya_o, yb_o, x1_o, dx2_o, acc_o):
        ba_r, za_r, o_r, zb_r, ga_r, gb_r, xa_r, ca_r, xah_r, cah_r = _tail_views(tail_r, conv_r, halo_r)
        i = pl.program_id(0)

        @pl.when(i == 0)
        def _():
            acc_o[...] = jnp.zeros_like(acc_o)

        u, u1, u2 = _conv_inputs(i, tm, xa_r, ca_r, xah_r, cah_r)
        cwv = cw_r[...]
        cv = cwv[0:1, :] * u2 + cwv[1:2, :] * u1 + cwv[2:3, :] * u + cb_r[...]
        za = za_r[...]
        a_pre = ba_r[...] * cv.astype(MXU) * (za * _sigmoid(za))
        apre_o[...] = a_pre
        ya = _dot(a_pre, wa_r[...]).astype(MXU)

        hn = _head_norm(_f32(h_r), gh_r[...])
        hbn = jnp.concatenate([xh * g for xh, _, g in hn], axis=1)
        zb = zb_r[...]
        b_pre = _sigmoid(o_r[...]) * hbn.astype(MXU) * (zb * _sigmoid(zb))
        bpre_o[...] = b_pre
        yb = _dot(b_pre, wb_r[...]).astype(MXU)
        ya_o[...] = ya
        yb_o[...] = yb

        mg = _sigmoid(ga_r[...]) * ya + _sigmoid(gb_r[...]) * yb
        mg_o[...] = mg
        x1 = x_r[...] + _dot(mg, wo_r[...])
        x1_o[...] = x1.astype(MXU)
        xn1 = (x1 * _rstd(x1) * gple_r[...]).astype(MXU)
        xn1_o[...] = xn1
        gt = _sigmoid(_dot(xn1, wpg_r[...]))
        e = _dot(p_r[...].astype(MXU), wp_r[...])
        x2 = x1 + gt * e
        r2 = _rstd(x2)
        xh2 = x2 * r2
        gf = gfin_r[...]
        diff = xh2 * gf - t_r[...]
        dy = diff * (1.0 / D)
        dx2 = _norm_bwd(dy, xh2, r2, gf)
        dx2_o[...] = dx2
        de_o[...] = (dx2 * gt).astype(MXU)
        dgp_o[...] = (dx2 * e * gt * (1.0 - gt)).astype(MXU)
        acc_o[0:1, :] += jnp.sum(dy * xh2, axis=0, keepdims=True)
        loss = 0.5 * jnp.sum(jnp.sum(diff * diff, axis=1, keepdims=True), axis=0, keepdims=True) * (1.0 / D)
        acc_o[1:2, :] += jnp.broadcast_to(loss, (1, D))

    row = lambda w, dt: (pl.BlockSpec((tm, w), lambda i: (i, 0)), jax.ShapeDtypeStruct((n, w), dt))
    outs = [row(D, MXU), row(VD, MXU), row(D, MXU), row(D, MXU), row(D, MXU), row(D, MXU),
            row(D, MXU), row(D, MXU), row(D, MXU), row(D, F32),
            (_const((8, D)), jax.ShapeDtypeStruct((8, D), F32))]
    return pl.pallas_call(
        body, name="tail_fwd", grid=(n // tm,),
        in_specs=_tail_in_specs(tm) + [
                  pl.BlockSpec((tm, VD), lambda i: (i, 0)),
                  pl.BlockSpec((tm, D), lambda i: (i, 0)),
                  pl.BlockSpec((tm, PLE), lambda i: (i, 0)),
                  pl.BlockSpec((tm, D), lambda i: (i, 0)),
                  _const((8, D)), _const((1, D)), _const((1, VD)), _const((1, D)), _const((1, D)),
                  _const((D, D)), _const((VD, D)), _const((D, D)), _const((D, D)), _const((PLE, D))],
        out_specs=[s for s, _ in outs],
        out_shape=[s for _, s in outs],
        compiler_params=_cparams("arbitrary"),
    )(proj, proj, proj, h, x, p, t, cw, cb, gh, gple, gfin, wa, wb, wo, wpg, wp)


def _tail_bwd(proj, h, dgp, dx2, x1, ya, yb, cw, cb, gh, gple, wpg, wo, wb, wa):
    n = x1.shape[0]
    tm = min(256, n)

    def body(tail_r, conv_r, halo_r, h_r,
             dgp_r, dx2_r, x1_r, ya_r, yb_r, cw_r, cb_r, gh_r, gple_r,
             wpg_r, wo_r, wb_r, wa_r,
             dproj_o, dcv_o, dh_o, dx1_o, dx1b_o, dya_o, dyb_o, acc_o):
        ba_r, za_r, o_r, zb_r, ga_r, gb_r, xa_r, ca_r, xah_r, cah_r = _tail_views(tail_r, conv_r, halo_r)
        i = pl.program_id(0)

        @pl.when(i == 0)
        def _():
            acc_o[...] = jnp.zeros_like(acc_o)

        dxn1 = _dot_nt(dgp_r[...], wpg_r[...])
        x1 = _f32(x1_r)
        r1 = _rstd(x1)
        xh1 = x1 * r1
        acc_o[0:1, 0:D] += jnp.sum(dxn1 * xh1, axis=0, keepdims=True)
        dx1 = dx2_r[...] + _norm_bwd(dxn1, xh1, r1, gple_r[...])
        dx1_o[...] = dx1
        dx1b = dx1.astype(MXU)
        dx1b_o[...] = dx1b

        dmg = _dot_nt(dx1b, wo_r[...]).astype(MXU)
        sga, dsga = _gate(ga_r[...])
        sgb, dsgb = _gate(gb_r[...])
        dya = dmg * sga
        dyb = dmg * sgb
        dya_o[...] = dya
        dyb_o[...] = dyb
        dproj_o[:, C_GA:C_GA + D] = dmg * ya_r[...] * dsga
        dproj_o[:, C_GB:C_GB + D] = dmg * yb_r[...] * dsgb

        db_pre = _dot_nt(dyb, wb_r[...]).astype(MXU)
        hn = _head_norm(_f32(h_r), gh_r[...])
        hbn = jnp.concatenate([xh * g for xh, _, g in hn], axis=1).astype(MXU)
        so, dso = _gate(o_r[...])
        zb = zb_r[...]
        szb, dszb = _gate(zb)
        sb = zb * szb
        t1 = db_pre * hbn
        dproj_o[:, C_O:C_O + VD] = t1 * (sb * dso)
        dproj_o[:, C_ZB:C_ZB + VD] = t1 * (so * (szb + zb * dszb))
        dhbn = (db_pre * (so * sb)).astype(F32)
        for j, (xh, rj, g) in enumerate(hn):
            dj = dhbn[:, j * DV:(j + 1) * DV]
            acc_o[1:2, j * DV:(j + 1) * DV] += jnp.sum(dj * xh, axis=0, keepdims=True)
            dh_o[:, j * DV:(j + 1) * DV] = _norm_bwd(dj, xh, rj, g).astype(MXU)

        da_pre = _dot_nt(dya, wa_r[...]).astype(MXU)
        u, u1, u2 = _conv_inputs(i, tm, xa_r, ca_r, xah_r, cah_r)
        cwv = cw_r[...]
        cv = cwv[0:1, :] * u2 + cwv[1:2, :] * u1 + cwv[2:3, :] * u + cb_r[...]
        za = za_r[...]
        sza, dsza = _gate(za)
        sa = za * sza
        ba = ba_r[...]
        t2 = da_pre * cv.astype(MXU)
        dproj_o[:, C_BA:C_BA + D] = t2 * sa
        dproj_o[:, C_ZA:C_ZA + D] = t2 * (ba * (sza + za * dsza))
        dcv_b = da_pre * (ba * sa)
        dcv_o[...] = dcv_b
        dcv = dcv_b.astype(F32)
        acc_o[2:3, 0:D] += jnp.sum(dcv, axis=0, keepdims=True)
        acc_o[3:4, 0:D] += jnp.sum(dcv * u2, axis=0, keepdims=True)
        acc_o[4:5, 0:D] += jnp.sum(dcv * u1, axis=0, keepdims=True)
        acc_o[5:6, 0:D] += jnp.sum(dcv * u, axis=0, keepdims=True)

    row = lambda w, dt: (pl.BlockSpec((tm, w), lambda i: (i, 0)), jax.ShapeDtypeStruct((n, w), dt))
    outs = [(pl.BlockSpec((tm, TAIL_W), lambda i: (i, 0)), jax.ShapeDtypeStruct((n, NMAIN), MXU)),
            row(D, MXU), row(VD, MXU), row(D, F32), row(D, MXU), row(D, MXU), row(D, MXU),
            (_const((8, VD)), jax.ShapeDtypeStruct((8, VD), F32))]
    rowin = lambda w: pl.BlockSpec((tm, w), lambda i: (i, 0))
    return pl.pallas_call(
        body, name="tail_bwd", grid=(n // tm,),
        in_specs=_tail_in_specs(tm) + [
                  rowin(VD), rowin(D), rowin(D), rowin(D), rowin(D), rowin(D),
                  _const((8, D)), _const((1, D)), _const((1, VD)), _const((1, D)),
                  _const((D, D)), _const((D, D)), _const((VD, D)), _const((D, D))],
        out_specs=[s for s, _ in outs],
        out_shape=[s for _, s in outs],
        compiler_params=_cparams("arbitrary"),
    )(proj, proj, proj, h, dgp, dx2, x1, ya, yb, cw, cb, gh, gple, wpg, wo, wb, wa)


def _conv_bwd(proj, dcv, cw, dproj):
    n = dcv.shape[0]
    tm = min(512, n)
    nt = n // tm

    def body(xa_r, ca_r, dcv_r, nxt_r, cw_r, dproj_in, dxc_o):
        del dproj_in
        i = pl.program_id(0)
        dcv_v = _f32(dcv_r)
        nxt = jnp.where(i < nt - 1, _f32(nxt_r), 0.0)
        rid = lax.broadcasted_iota(jnp.int32, dcv_v.shape, 0)
        d1 = jnp.where(rid == tm - 1, nxt[0:1, :], pltpu.roll(dcv_v, tm - 1, 0))
        d2 = jnp.where(rid == tm - 2, nxt[0:1, :],
                       jnp.where(rid == tm - 1, nxt[1:2, :], pltpu.roll(dcv_v, tm - 2, 0)))
        cwv = cw_r[...]
        du = cwv[2:3, :] * dcv_v + cwv[1:2, :] * d1 + cwv[0:1, :] * d2
        dxc_o[:, 0:D] = (du * _f32(ca_r)).astype(MXU)
        dxc_o[:, D:2 * D] = (du * _f32(xa_r)).astype(MXU)

    return pl.pallas_call(
        body, name="conv_bwd", grid=(nt,),
        in_specs=[_proj_spec(tm, C_XA, 1024), _proj_spec(tm, C_CA, 1024),
                  pl.BlockSpec((tm, D), lambda i: (i, 0)),
                  pl.BlockSpec((8, D), lambda i: (jnp.minimum((i + 1) * (tm // 8), n // 8 - 1), 0)),
                  _const((8, D)), ANY],
        out_specs=pl.BlockSpec((tm, CONV_W), lambda i: (i, C_XA // CONV_W)),
        out_shape=jax.ShapeDtypeStruct(dproj.shape, dproj.dtype),
        input_output_aliases={5: 0},
        compiler_params=_cparams("arbitrary"),
    )(proj, proj, dcv, dcv, cw, dproj)


def _position():
    return lax.axis_index("x"), lax.axis_index("y"), lax.axis_index("c")


def _flip(v, bit):
    return 1 - v if bit else v


def _part_start(core, stride, tiled):
    return pl.multiple_of(core * stride, stride) if tiled else core * stride


def _scatter_copies(srcs, dsts, strides, send_sems, recv_sems, local_sems):
    x, y, c = _position()
    me = 4 * x + 2 * y + c
    na = len(srcs)
    copies = []
    for r in range(N_DEV):
        px, py, pc = _flip(x, (r >> 2) & 1), _flip(y, (r >> 1) & 1), _flip(c, r & 1)
        for a in range(na):
            rows = dsts[a].shape[1]
            src = srcs[a].at[2 * px + py, pl.ds(_part_start(pc, strides[a], len(dsts[a].shape) == 3), rows)]
            dst = dsts[a].at[me]
            if r == 0:
                copies.append(pltpu.make_async_copy(src, dst, local_sems.at[a]))
            else:
                k = (r - 1) * na + a
                copies.append(pltpu.make_async_remote_copy(
                    src_ref=src, dst_ref=dst, send_sem=send_sems.at[k], recv_sem=recv_sems.at[k],
                    device_id=(px, py, pc), device_id_type=MESH))
    return copies


def _input_grad(dproj, dgates, w_t, wg_t, x, dx1, g_mix, g_in, g_rest):
    n = x.shape[0]
    tm, tk = min(1024, n), 2048
    nk = NMAIN // tk
    nt = n // tm

    def body(dp_r, dg_r, w_r, wg_r, x_r, dx1_r, g_r, gin, grest,
             gx_o, acc_o, oin, orest, acc, send_sems, recv_sems, local_sems):
        i = pl.program_id(0)
        kk = pl.program_id(1)
        copies = _scatter_copies((gin, grest), (oin, orest), strides, send_sems, recv_sems, local_sems)

        @pl.when((i == 0) & (kk == 0))
        def _():
            acc_o[...] = jnp.zeros_like(acc_o)
            for cp in copies:
                cp.start()

        @pl.when(kk == 0)
        def _():
            acc[...] = _dot(dg_r[...].astype(MXU), wg_r[...])

        acc[...] += _dot(dp_r[...], w_r[...])

        @pl.when(kk == nk - 1)
        def _():
            dhn = acc[...]
            xv = x_r[...]
            r0 = _rstd(xv)
            xh = xv * r0
            acc_o[0:1, :] += jnp.sum(dhn * xh, axis=0, keepdims=True)
            gx_o[...] = dx1_r[...] + _norm_bwd(dhn, xh, r0, g_r[...])

        @pl.when((i == nt - 1) & (kk == nk - 1))
        def _():
            for cp in copies:
                cp.wait()

    nrem = 2 * (N_DEV - 1)
    r_in, r_rest = HALF_SLABS, g_rest.shape[1] // 2
    strides = (HALF_START, r_rest)
    return pl.pallas_call(
        body, name="input_grad", grid=(nt, nk),
        in_specs=[pl.BlockSpec((tm, tk), lambda i, kk: (i, kk)),
                  pl.BlockSpec((tm, GATE_W), lambda i, kk: (i, 0)),
                  pl.BlockSpec((tk, D), lambda i, kk: (kk, 0)),
                  pl.BlockSpec((GATE_W, D), lambda i, kk: (0, 0)),
                  pl.BlockSpec((tm, D), lambda i, kk: (i, 0)),
                  pl.BlockSpec((tm, D), lambda i, kk: (i, 0)),
                  pl.BlockSpec((1, D), lambda i, kk: (0, 0)),
                  ANY, ANY],
        out_specs=[pl.BlockSpec((tm, D), lambda i, kk: (i, 0)),
                   pl.BlockSpec((8, D), lambda i, kk: (0, 0)),
                   ANY, ANY],
        out_shape=[jax.ShapeDtypeStruct((n, D), F32), jax.ShapeDtypeStruct((8, D), F32),
                   jax.ShapeDtypeStruct((N_DEV, r_in) + g_in.shape[2:], g_in.dtype),
                   jax.ShapeDtypeStruct((N_DEV, r_rest) + g_rest.shape[2:], g_rest.dtype)],
        scratch_shapes=[pltpu.VMEM((tm, D), F32),
                        pltpu.SemaphoreType.DMA((nrem,)), pltpu.SemaphoreType.DMA((nrem,)),
                        pltpu.SemaphoreType.DMA((2,))],
        compiler_params=_cparams("arbitrary", "arbitrary"),
    )(dproj, dgates, w_t, wg_t, x, dx1, g_mix, g_in, g_rest)


def _pack_small(acc_f, acc_b, acc_x, gsum):
    def body(f_r, b_r, x_r, s_r, o_r):
        o_r[...] = jnp.zeros_like(o_r)
        o_r[0:1, :] = x_r[0:1, :]
        o_r[1:2, :] = b_r[2:3, 0:D]
        o_r[2:3, :] = b_r[1:2, 0:D]
        o_r[3:4, :] = b_r[1:2, D:2 * D]
        o_r[4:5, :] = b_r[0:1, 0:D]
        o_r[5:6, :] = f_r[0:1, :]
        o_r[6:9, :] = b_r[3:6, 0:D]
        lane = lax.broadcasted_iota(jnp.int32, (1, D), 1)
        o_r[9:10, :] = jnp.where(lane < 2 * NH, s_r[0:1, :], jnp.where(lane == 2 * NH, f_r[1:2, :], 0.0))

    return pl.pallas_call(
        body, name="pack_small",
        out_shape=jax.ShapeDtypeStruct((SMALL_ROWS, D), F32),
    )(acc_f, acc_b, acc_x, gsum)


def _sum_slots(r, name, tr=64, half=None):
    s, rows, w = r.shape
    tr = min(tr, rows)
    assert rows % tr == 0
    nt = rows // tr

    def body(*refs):
        r_ref, o_ref = refs[-2:]
        tot = r_ref[0].astype(F32)
        for k in range(1, s):
            tot = tot + r_ref[k].astype(F32)
        o_ref[...] = tot

    if half is None:
        return pl.pallas_call(
            body, name=name, grid=(nt,),
            in_specs=[pl.BlockSpec((s, tr, w), lambda i: (0, i, 0))],
            out_specs=pl.BlockSpec((tr, w), lambda i: (i, 0)),
            out_shape=jax.ShapeDtypeStruct((rows, w), F32),
            compiler_params=_cparams("arbitrary"),
        )(r)
    return pl.pallas_call(
        body, name=name,
        grid_spec=pltpu.PrefetchScalarGridSpec(
            num_scalar_prefetch=1, grid=(nt,),
            in_specs=[pl.BlockSpec((s, tr, w), lambda i, hf: (0, i, 0))],
            out_specs=pl.BlockSpec((tr, w), lambda i, hf: (hf[0] * nt + i, 0))),
        out_shape=jax.ShapeDtypeStruct((2 * rows, w), F32),
        compiler_params=_cparams("arbitrary"),
    )(half, r)


def _cast_slabs(a, tr=163):
    rows = a.shape[0]
    assert rows % tr == 0

    def body(a_ref, o_ref):
        o_ref[...] = a_ref[...].astype(MXU)

    spec = pl.BlockSpec((tr,) + a.shape[1:], lambda i: (i, 0, 0))
    return pl.pallas_call(
        body, name="cast_w_in", grid=(rows // tr,), in_specs=[spec], out_specs=spec,
        out_shape=jax.ShapeDtypeStruct(a.shape, MXU), compiler_params=_cparams("arbitrary"),
    )(a)


def _sum_slabs(r, name, tr=69):
    s, rows = r.shape[:2]
    assert rows % tr == 0

    def body(r_ref, o_ref):
        tot = r_ref[0].astype(F32)
        for k in range(1, s):
            tot = tot + r_ref[k].astype(F32)
        o_ref[...] = tot

    return pl.pallas_call(
        body, name=name, grid=(rows // tr,),
        in_specs=[pl.BlockSpec((s, tr) + r.shape[2:], lambda i: (0, i, 0, 0))],
        out_specs=pl.BlockSpec((tr,) + r.shape[2:], lambda i: (i, 0, 0)),
        out_shape=jax.ShapeDtypeStruct(r.shape[1:], F32),
        compiler_params=_cparams("arbitrary"),
    )(r)


def _adamw_body(w_r, g_r, m_r, v_r, d_o, m_o, v_o):
    c1 = 1.0 - ADAM_B1 ** ADAM_STEP
    c2 = 1.0 - ADAM_B2 ** ADAM_STEP
    gv = g_r[...]
    mn = ADAM_B1 * m_r[...] + (1.0 - ADAM_B1) * gv
    vn = ADAM_B2 * v_r[...] + (1.0 - ADAM_B2) * (gv * gv)
    m_o[...] = mn
    v_o[...] = vn
    d_o[...] = -ADAM_LR * ((mn / c1) / (jnp.sqrt(vn / c2) + ADAM_EPS) + ADAM_WD * w_r[...])


def _adamw_slabs(w, g, m, v, name, tr):
    rows = w.shape[0]
    assert rows % tr == 0 and w.shape == g.shape

    def body(*refs):
        _adamw_body(*refs)

    spec = pl.BlockSpec((tr,) + w.shape[1:], lambda i: (i, 0, 0))
    shp = jax.ShapeDtypeStruct(w.shape, F32)
    return pl.pallas_call(
        body, name=name, grid=(rows // tr,),
        in_specs=[spec] * 4, out_specs=[spec] * 3, out_shape=[shp] * 3,
        compiler_params=_cparams("arbitrary"),
    )(w, g, m, v)


def _adamw(w, g, m, v, name, g_row0=0, tr=64):
    lead = w.ndim == 3
    rows, cols = w.shape[-2:]
    tr = min(tr, rows)
    assert rows % tr == 0 and g_row0 % tr == 0 and g.shape[1] == cols

    def body(*refs):
        _adamw_body(*refs)

    if lead:
        spec = pl.BlockSpec((None, tr, cols), lambda i: (0, i, 0))
    else:
        spec = pl.BlockSpec((tr, cols), lambda i: (i, 0))
    g_spec = pl.BlockSpec((tr, cols), lambda i: (g_row0 // tr + i, 0))
    shp = jax.ShapeDtypeStruct(w.shape, F32)
    return pl.pallas_call(
        body, name=name, grid=(rows // tr,),
        in_specs=[spec, g_spec, spec, spec], out_specs=[spec] * 3, out_shape=[shp] * 3,
        compiler_params=_cparams("arbitrary"),
    )(w, g, m, v)


def _allgather_chips(halved, whole):
    arrs = [a for a, _ in halved] + list(whole)
    parts = [p for _, p in halved]
    nh, na = len(halved), len(arrs)
    relations = ((1, 0), (0, 1), (1, 1))
    n_ici = 3 * na

    def body(*refs):
        ins, outs = refs[:na], refs[na:2 * na]
        send_sems, recv_sems = refs[2 * na:]
        x, y, c = _position()
        mine = 2 * x + y

        def rows(a, core, stride, size):
            return pl.ds(_part_start(core, stride, len(ins[a].shape) == 2), size)

        def remote(src, dst, k, to):
            return pltpu.make_async_remote_copy(src_ref=src, dst_ref=dst, send_sem=send_sems.at[k],
                                                recv_sem=recv_sems.at[k], device_id=to, device_id_type=MESH)

        first, passed = [], []
        for j, (rx, ry) in enumerate(relations):
            px, py = _flip(x, rx), _flip(y, ry)
            for a in range(na):
                if a < nh:
                    s1, n1, s2, n2 = parts[a]
                    first.append(remote(ins[a].at[rows(a, c, s1, n1)], outs[a].at[mine, rows(a, c, s1, n1)],
                                        j * na + a, (px, py, c)))
                    landed = outs[a].at[2 * px + py, rows(a, c, s2, n2)]
                    passed.append(remote(landed, landed, n_ici + j * nh + a, (x, y, 1 - c)))
                else:
                    first.append(remote(ins[a], outs[a].at[mine], j * na + a, (px, py, c)))
        for cp in first:
            cp.start()
        k = 0
        for j in range(3):
            for a in range(na):
                first[j * na + a].wait_recv()
                if a < nh:
                    passed[k].start()
                    k += 1
        for cp in passed:
            cp.wait_recv()
        for cp in first + passed:
            cp.wait_send()

    nsem = n_ici + 3 * nh
    outs = pl.pallas_call(
        body, name="allgather_weights",
        in_specs=[ANY] * na, out_specs=[ANY] * na,
        out_shape=[jax.ShapeDtypeStruct((N_CHIPS,) + a.shape, a.dtype) for a in arrs],
        scratch_shapes=[pltpu.SemaphoreType.DMA((nsem,)), pltpu.SemaphoreType.DMA((nsem,))],
    )(*arrs)
    chip = 2 * lax.axis_index("x") + lax.axis_index("y")
    return [lax.dynamic_update_slice(o, a[None], (chip,) + (0,) * a.ndim) for o, a in zip(outs, arrs)]


def _join_halves(piece, full_rest, small):
    def body(pc_ref, rest_in, sm, oin, orest, osm, send_sems, recv_sems, local_sem):
        del rest_in
        x, y, c = _position()
        me = 4 * x + 2 * y + c
        sib = (x, y, 1 - c)
        rows = orest.shape[0] // 2
        mine = orest.at[pl.ds(pl.multiple_of(c * rows, rows), rows)]
        copies = [
            pltpu.make_async_remote_copy(
                src_ref=pc_ref.at[pl.ds(c, HALF_START)], dst_ref=oin.at[pl.ds(c * HALF_SLABS, HALF_START)],
                send_sem=send_sems.at[0], recv_sem=recv_sems.at[0], device_id=sib, device_id_type=MESH),
            pltpu.make_async_remote_copy(
                src_ref=mine, dst_ref=mine, send_sem=send_sems.at[1], recv_sem=recv_sems.at[1],
                device_id=sib, device_id_type=MESH)]
        na = 2
        copies.append(pltpu.make_async_copy(sm, osm.at[me], local_sem))
        for r in range(1, N_DEV):
            to = (_flip(x, (r >> 2) & 1), _flip(y, (r >> 1) & 1), _flip(c, r & 1))
            copies.append(pltpu.make_async_remote_copy(
                src_ref=sm, dst_ref=osm.at[me], send_sem=send_sems.at[na + r - 1],
                recv_sem=recv_sems.at[na + r - 1], device_id=to, device_id_type=MESH))
        for cp in copies:
            cp.start()
        for cp in copies:
            cp.wait()

    nsem = 2 + N_DEV - 1
    return pl.pallas_call(
        body, name="join_halves",
        in_specs=[ANY] * 3, out_specs=[ANY] * 3,
        out_shape=[jax.ShapeDtypeStruct((SLABS_CHIP,) + piece.shape[1:], piece.dtype),
                   jax.ShapeDtypeStruct(full_rest.shape, full_rest.dtype),
                   jax.ShapeDtypeStruct((N_DEV,) + small.shape, small.dtype)],
        scratch_shapes=[pltpu.SemaphoreType.DMA((nsem,)), pltpu.SemaphoreType.DMA((nsem,)),
                        pltpu.SemaphoreType.DMA],
        input_output_aliases={1: 1},
    )(piece, full_rest, small)


def _to_internal(w_slabs):
    order = sorted(SEGMENTS, key=lambda s: s[2])
    main = jnp.concatenate([w_slabs[g0 // 2:(g0 + w) // 2] for g0, w, _ in order], axis=0)
    gate = w_slabs[GATE_COL // 2:GATE_COL // 2 + NH].reshape(2 * NH, D)
    return main.reshape(NMAIN, D), jnp.pad(gate, ((0, GATE_W - 2 * NH), (0, 0)))


def _to_global(main_t, gate_t):
    main = main_t.reshape(NMAIN // 2, 16, D // 8)
    parts = sorted([(g0, main[i0 // 2:(i0 + w) // 2]) for g0, w, i0 in SEGMENTS]
                   + [(GATE_COL, gate_t[0:2 * NH].reshape(NH, 16, D // 8))], key=lambda s: s[0])
    return jnp.concatenate([p for _, p in parts], axis=0)


def _pad_rows(a, rows=8):
    return jnp.pad(a, ((0, rows - a.shape[0]), (0, 0)))


def kernel(x, p, g_mix, w_in, conv_w, conv_b, w_a_out, b_gates, g_head, w_b_out, w_o, g_ple, w_ple_gate, w_ple, g_final, loss_target, m_g_mix, m_w_in, m_conv_w, m_conv_b, m_w_a_out, m_b_gates, m_g_head, m_w_b_out, m_w_o, m_g_ple, m_w_ple_gate, m_w_ple, m_g_final, v_g_mix, v_w_in, v_conv_w, v_conv_b, v_w_a_out, v_b_gates, v_g_head, v_w_b_out, v_w_o, v_g_ple, v_w_ple_gate, v_w_ple, v_g_final):
    chip = 2 * lax.axis_index("x") + lax.axis_index("y")
    xs, ps, ts = x[0], p[0, 0], loss_target[0]
    g_fin = g_final.reshape(1, D)

    w_slabs = _cast_slabs(jnp.transpose(w_in, (2, 0, 1)).reshape(SLABS_CHIP, 16, D // 8))
    rest = [w[0].astype(MXU) for w in (w_a_out, w_b_out, w_o, w_ple_gate, w_ple)]
    even = lambda a: (a.shape[0] // 2,) * 4
    g_win, g_wa, g_wb, g_wo, g_wpg, g_wp, g_cw = _allgather_chips(
        [(w_slabs, (HALF_START, HALF_SLABS, HALF_SLABS, HALF_START))] + [(a, even(a)) for a in rest],
        [_pad_rows(conv_w[0])])
    w_t, wg_t = _to_internal(g_win.reshape(N_IN // 2, 16, D // 8))
    wa, wb, wo, wpg = [g.reshape(-1, D) for g in (g_wa, g_wb, g_wo, g_wpg)]
    wp = jnp.transpose(g_wp, (1, 0, 2)).reshape(PLE, D)
    cw = jnp.transpose(g_cw, (1, 0, 2)).reshape(8, D)

    bias = jnp.pad(b_gates, ((0, 0), (0, GATE_W - 2 * NH)))
    proj, hn, gates = _proj(xs, g_mix, w_t, wg_t)
    h, stats, cs, ns, ms = _mlstm_fwd(proj, gates, bias)
    (a_pre, b_pre, mg, xn1, de, dgp, ya, yb, x1, dx2, acc_f) = _tail_fwd(
        proj, h, xs, ps, ts, cw, conv_b, g_head, g_ple, g_fin, wa, wb, wo, wpg, wp)
    dproj, dcv, dh, dx1, dx1b, dya, dyb, acc_b = _tail_bwd(
        proj, h, dgp, dx2, x1, ya, yb, cw, conv_b, g_head, g_ple, wpg, wo, wb, wa)
    dproj = _conv_bwd(proj, dcv, cw, dproj)
    dproj, dgates, gsum = _mlstm_bwd(proj, gates, bias, h, dh, stats, cs, ns, ms, dproj)
    d_main = _matmul_tn(dproj, hn, "dw_in", out_dtype=WIRE, tk=4096)
    d_gate = _matmul_tn(dgates, hn, "dw_gate", out_dtype=WIRE)
    d_wa = _matmul_tn(a_pre, dya, "dw_a_out", out_dtype=WIRE)
    d_wb = _matmul_tn(b_pre, dyb, "dw_b_out", out_dtype=WIRE)
    d_wo = _matmul_tn(mg, dx1b, "dw_o", out_dtype=WIRE)
    d_wpg = _matmul_tn(xn1, dgp, "dw_ple_gate", out_dtype=WIRE)
    d_wp = _matmul_tn(ps, de, "dw_ple", out_dtype=WIRE)

    g_in = _to_global(d_main, d_gate).reshape(N_CHIPS, SLABS_CHIP, 16, D // 8)
    d_wp_c = jnp.transpose(d_wp.reshape(PLE, N_CHIPS, PLE), (1, 0, 2)).reshape(N_CHIPS, PACK_ROWS[4], D)
    g_rest = jnp.concatenate(
        [d_wa.reshape(N_CHIPS, -1, D), d_wb.reshape(N_CHIPS, -1, D), d_wo.reshape(N_CHIPS, -1, D),
         d_wpg.reshape(N_CHIPS, -1, D), d_wp_c], axis=1)
    grad_x, acc_x, r_in, r_rest = _input_grad(dproj, dgates, w_t, wg_t, xs, dx1, g_mix, g_in, g_rest)
    small = _pack_small(acc_f, acc_b, acc_x, gsum)
    core = lax.axis_index("c").astype(jnp.int32)
    piece = _sum_slabs(r_in, "sum_w_in")
    gw_in, gw_rest, r_small = _join_halves(
        piece, _sum_slots(r_rest, "sum_rest", tr=96, half=core.reshape(1)), small)
    gw_in = lax.dynamic_update_slice(gw_in, piece, (core * HALF_START, 0, 0)).reshape(COLS_CHIP, 8, D // 8)
    gs = _sum_slots(r_small, "sum_small", tr=SMALL_ROWS)

    big = []
    row0 = 0
    for name, w, m, v in (("w_a_out", w_a_out, m_w_a_out, v_w_a_out), ("w_b_out", w_b_out, m_w_b_out, v_w_b_out),
                          ("w_o", w_o, m_w_o, v_w_o), ("w_ple_gate", w_ple_gate, m_w_ple_gate, v_w_ple_gate)):
        big.append((name, w, m, v, gw_rest, row0))
        row0 += w.shape[1]
    g_wp = gw_rest[row0:row0 + PACK_ROWS[4]].reshape(PLE, PLE)
    big.append(("w_ple", w_ple, m_w_ple, v_w_ple, g_wp, 0))
    upd = {name: _adamw(w, g, m, v, "adamw_" + name, g_row0=r0) for name, w, m, v, g, r0 in big}
    g_big = {name: (g if name == "w_ple" else g[r0:r0 + w.shape[1]])[None] for name, w, m, v, g, r0 in big}
    slabs = lambda a: jnp.transpose(a, (2, 0, 1)).reshape(COLS_CHIP, 8, D // 8)
    unslab = lambda a: jnp.transpose(a, (1, 2, 0)).reshape(1, D, COLS_CHIP)
    upd["w_in"] = [unslab(u) for u in _adamw_slabs(
        slabs(w_in), gw_in, slabs(m_w_in), slabs(v_w_in), "adamw_w_in", tr=326)]
    g_big["w_in"] = unslab(gw_in)

    lane = lax.broadcasted_iota(jnp.int32, (1, D), 1)
    g_small = jnp.concatenate([gs[0:6], jnp.where(lane < 2 * NH, gs[9:10], 0.0), jnp.zeros((1, D), F32)], axis=0)

    def small_pack(gm, cb_, bg, gh, gp, gf):
        return jnp.concatenate([gm, cb_, gh.reshape(2, D), gp, gf.reshape(1, D),
                                jnp.pad(bg, ((0, 0), (0, D - 2 * NH))), jnp.zeros((1, D), F32)], axis=0)

    ws = small_pack(g_mix, conv_b, b_gates, g_head, g_ple, g_final)
    ms_ = small_pack(m_g_mix, m_conv_b, m_b_gates, m_g_head, m_g_ple, m_g_final)
    vs = small_pack(v_g_mix, v_conv_b, v_b_gates, v_g_head, v_g_ple, v_g_final)
    upd_s = _adamw(ws, g_small, ms_, vs, "adamw_small")
    g_cw_mine = _pad_rows(lax.dynamic_slice(gs[6:9], (0, chip * PLE), (3, PLE)))
    upd_c = _adamw(_pad_rows(conv_w[0]), g_cw_mine, _pad_rows(m_conv_w[0]), _pad_rows(v_conv_w[0]),
                   "adamw_conv_w")

    def leaves(bigs, sm, cwv):
        return [sm[0:1], bigs["w_in"], cwv[0:3][None], sm[1:2], bigs["w_a_out"], sm[6:7, 0:2 * NH],
                sm[2:4].reshape(1, VD), bigs["w_b_out"], bigs["w_o"], sm[4:5], bigs["w_ple_gate"],
                bigs["w_ple"], sm[5]]

    loss = gs[9, 2 * NH]
    outs = [loss, grad_x[None]] + leaves(g_big, g_small, g_cw_mine)
    for k in range(3):
        outs += leaves({name: u[k] for name, u in upd.items()}, upd_s[k], upd_c[k])
    return tuple(outs)
```

```python
import jax
import jax.numpy as jnp
from jax import lax
from jax.experimental import pallas as pl
from jax.experimental.pallas import tpu as pltpu

F32 = jnp.float32
MXU = jnp.bfloat16
WIRE = jnp.bfloat16

D = 1024
NH, DK, DV = 4, 256, 512
VD = NH * DV
PLE = 256
LCH = 256
EPS = 1e-6
N_IN = 14344
NMAIN = 14336
GATE_W = 128
N_CHIPS, N_DEV = 4, 8

C_BA, C_ZA, C_O, C_ZB, C_GA, C_GB = 0, 1024, 2048, 4096, 6144, 7168
QK = NH * DK
C_Q, C_K, C_V, C_XA, C_CA = 8192, 9216, 10240, 12288, 13312
QKV_W = 2 * QK + VD
TAIL_W = 8192
CONV_W = 2048
SEGMENTS = (
    (0, 1024, C_XA), (1024, 1024, C_BA), (2048, 1024, C_CA), (3072, 1024, C_ZA),
    (4096, QKV_W, C_Q), (8192, 4096, C_O), (12296, 2048, C_GA),
)
GATE_COL = 12288

COLS_CHIP = N_IN // N_CHIPS
SLABS_CHIP = COLS_CHIP // 2
HALF_START = SLABS_CHIP // 2
HALF_SLABS = SLABS_CHIP - HALF_START

PACK_ROWS = (256, 512, 256, 256, 64)
SMALL_ROWS = 16

ADAM_LR, ADAM_B1, ADAM_B2, ADAM_EPS, ADAM_WD, ADAM_STEP = 0.001, 0.9, 0.999, 1e-08, 0.01, 10

VMEM_LIMIT = 56 * 1024 * 1024
MESH = pl.DeviceIdType.MESH
ANY = pl.BlockSpec(memory_space=pl.ANY)


def _cparams(*sem):
    return pltpu.CompilerParams(dimension_semantics=sem, vmem_limit_bytes=VMEM_LIMIT)


def _dot(a, b):
    return jnp.dot(a, b, preferred_element_type=F32)


def _dot_nt(a, b):
    return lax.dot_general(a, b, (((1,), (1,)), ((), ())), preferred_element_type=F32)


def _dot_tn(a, b):
    return lax.dot_general(a, b, (((0,), (0,)), ((), ())), preferred_element_type=F32)


def _sigmoid(x):
    return 1.0 / (1.0 + jnp.exp(-x))


def _logsig(x):
    return jnp.minimum(x, 0.0) - jnp.log(1.0 + jnp.exp(-jnp.abs(x)))


GATE_FLOOR = -80.0


def _gate(v):
    e = jnp.exp(-jnp.maximum(v, GATE_FLOOR))
    s = 1.0 / (1.0 + e)
    return s, e * s * s


def _rstd(x):
    return lax.rsqrt(jnp.mean(x * x, axis=-1, keepdims=True) + EPS)


def _norm_bwd(dy, xhat, r, g):
    dxh = dy * g
    return r * (dxh - xhat * jnp.mean(dxh * xhat, axis=-1, keepdims=True))


def _f32(ref):
    return ref[...].astype(F32)


def _proj(x, g_mix, w_t, wg_t, tm=1024, tn=2048):
    n = x.shape[0]
    m = w_t.shape[0]
    tm = min(tm, n)

    def body(x_ref, g_ref, b_ref, wg_ref, o_ref, hn_ref, gate_ref):
        @pl.when(pl.program_id(1) == 0)
        def _():
            xv = x_ref[...]
            hn = (xv * _rstd(xv) * g_ref[...]).astype(MXU)
            hn_ref[...] = hn
            gate_ref[...] = _dot_nt(hn, wg_ref[...])

        o_ref[...] = _dot_nt(hn_ref[...], b_ref[...]).astype(MXU)

    return pl.pallas_call(
        body, name="proj", grid=(n // tm, m // tn),
        in_specs=[pl.BlockSpec((tm, D), lambda i, j: (i, 0)),
                  pl.BlockSpec((1, D), lambda i, j: (0, 0)),
                  pl.BlockSpec((tn, D), lambda i, j: (j, 0)),
                  pl.BlockSpec((GATE_W, D), lambda i, j: (0, 0))],
        out_specs=[pl.BlockSpec((tm, tn), lambda i, j: (i, j)),
                   pl.BlockSpec((tm, D), lambda i, j: (i, 0)),
                   pl.BlockSpec((tm, GATE_W), lambda i, j: (i, 0))],
        out_shape=[jax.ShapeDtypeStruct((n, m), MXU), jax.ShapeDtypeStruct((n, D), MXU),
                   jax.ShapeDtypeStruct((n, GATE_W), F32)],
        compiler_params=_cparams("arbitrary", "arbitrary"),
    )(x, g_mix, w_t, wg_t)


def _matmul_tn(a, b, name, out_dtype=F32, ta=1024, tb=1024, tk=2048):
    n, ka = a.shape
    kb = b.shape[1]
    ta, tb, tk = min(ta, ka), min(tb, kb), min(tk, n)
    nk = n // tk

    def body(a_ref, b_ref, o_ref, acc):
        kk = pl.program_id(2)

        @pl.when(kk == 0)
        def _():
            acc[...] = jnp.zeros_like(acc)

        acc[...] += _dot_tn(a_ref[...].astype(MXU), b_ref[...].astype(MXU))

        @pl.when(kk == nk - 1)
        def _():
            o_ref[...] = acc[...].astype(out_dtype)

    return pl.pallas_call(
        body, name=name, grid=(ka // ta, kb // tb, nk),
        in_specs=[pl.BlockSpec((tk, ta), lambda i, j, kk: (kk, i)),
                  pl.BlockSpec((tk, tb), lambda i, j, kk: (kk, j))],
        out_specs=pl.BlockSpec((ta, tb), lambda i, j, kk: (i, j)),
        out_shape=jax.ShapeDtypeStruct((ka, kb), out_dtype),
        scratch_shapes=[pltpu.VMEM((ta, tb), F32)],
        compiler_params=_cparams("arbitrary", "arbitrary", "arbitrary"),
    )(a, b)


def _sub_row(block_t, j):
    sub = lax.broadcasted_iota(jnp.int32, block_t.shape, 0)
    return jnp.sum(jnp.where(sub == j, block_t, 0.0), axis=0, keepdims=True)


def _chunk_cumsum(g):
    n = g.shape[0]
    r = lax.broadcasted_iota(jnp.int32, (n, n), 0)
    c = lax.broadcasted_iota(jnp.int32, (n, n), 1)
    return jnp.dot(jnp.where(r >= c, 1.0, 0.0), _logsig(g), precision=lax.Precision.HIGHEST,
                   preferred_element_type=F32)


def _chunk_decay(li_col, li_row, b_col, b_row, m_prev):
    n = li_col.shape[0]
    r = lax.broadcasted_iota(jnp.int32, (n, n), 0)
    c = lax.broadcasted_iota(jnp.int32, (n, n), 1)
    tri = r >= c
    b_last = b_col[n - 1:n, :]
    rr = li_row - b_row
    a_col = b_col + m_prev
    g_col = b_last - b_col + li_col
    m_new = jnp.maximum(b_last + m_prev, jnp.max(g_col, axis=0, keepdims=True))
    w_col = jnp.exp(g_col - m_new)
    decay = jnp.exp(b_last + m_prev - m_new)
    return tri, rr, a_col, m_new, w_col, decay


def _qkv_specs(row_of):
    q_spec = pl.BlockSpec((LCH, QK), lambda c: (row_of(c), C_Q // QK))
    k_spec = pl.BlockSpec((LCH, QK), lambda c: (row_of(c), C_K // QK))
    v_spec = pl.BlockSpec((LCH, VD), lambda c: (row_of(c), C_V // VD))
    return q_spec, k_spec, v_spec


def _state_specs(row_of):
    return [pl.BlockSpec((NH, None, DK, DV), lambda c: (0, row_of(c), 0, 0)),
            pl.BlockSpec((NH, None, 1, DK), lambda c: (0, row_of(c), 0, 0)),
            pl.BlockSpec((NH, None, 1, GATE_W), lambda c: (0, row_of(c), 0, 0))]


def _lane_put(col, lane_id, width=GATE_W):
    lane = lax.broadcasted_iota(jnp.int32, (col.shape[0], width), 1)
    return jnp.where(lane == lane_id, col, 0.0)


def _lane_get(block, lane_id):
    lane = lax.broadcasted_iota(jnp.int32, block.shape, 1)
    return jnp.sum(jnp.where(lane == lane_id, block, 0.0), axis=1, keepdims=True)


def _mlstm_fwd(proj, gates, bias):
    n = proj.shape[0]
    nc = n // LCH

    def body(q_ref, k_ref, v_ref, g_ref, bias_ref,
             h_ref, st_ref, cs_ref, ns_ref, ms_ref, c_scr, n_scr, m_scr):
        @pl.when(pl.program_id(0) == 0)
        def _():
            c_scr[...] = jnp.zeros_like(c_scr)
            n_scr[...] = jnp.zeros_like(n_scr)
            m_scr[...] = jnp.full_like(m_scr, -jnp.inf)

        g = g_ref[...] + bias_ref[...]
        gt = g.T[0:8, :]
        b = _chunk_cumsum(g)
        bt = b.T[0:8, :]
        stats = jnp.zeros((LCH, GATE_W), F32)
        for hd in range(NH):
            m_all = m_scr[hd]
            m_prev = m_all[0:1, 0:1]
            b_col = _lane_get(b, NH + hd)
            tri, rr, a_col, m_new, w_col, decay = _chunk_decay(
                _lane_get(g, hd), _sub_row(gt, hd), b_col, _sub_row(bt, NH + hd), m_prev)
            dmat = jnp.where(tri, b_col + rr, -jnp.inf)
            m_col = jnp.maximum(a_col, jnp.max(dmat, axis=1, keepdims=True))
            dl = jnp.exp(dmat - m_col)
            inter = jnp.exp(a_col - m_col)

            qs = q_ref[:, hd * DK:(hd + 1) * DK] * (DK ** -0.5)
            kk = k_ref[:, hd * DK:(hd + 1) * DK]
            vv = v_ref[:, hd * DV:(hd + 1) * DV]
            cst = c_scr[hd]
            nst = n_scr[hd]
            cs_ref[hd] = cst
            ns_ref[hd] = nst
            ms_ref[hd] = m_all

            sc = _dot_nt(qs, kk) * dl
            num = _dot(sc.astype(MXU), vv) + inter * _dot(qs, cst.astype(MXU))
            den = (jnp.sum(sc, axis=1, keepdims=True)
                   + inter * jnp.sum(qs.astype(F32) * nst, axis=1, keepdims=True))
            nrm = jnp.maximum(jnp.abs(den), jnp.exp(-m_col))
            h_ref[:, hd * DV:(hd + 1) * DV] = (num * (1.0 / nrm)).astype(h_ref.dtype)
            stats = stats + _lane_put(den, hd) + _lane_put(m_col, NH + hd)

            kw = kk.astype(F32) * w_col
            c_scr[hd] = decay * cst + _dot_tn(kw.astype(MXU), vv)
            n_scr[hd] = decay * nst + jnp.sum(kw, axis=0, keepdims=True)
            m_scr[hd] = jnp.broadcast_to(m_new, (1, GATE_W))
        st_ref[...] = stats

    q_spec, k_spec, v_spec = _qkv_specs(lambda c: c)
    return pl.pallas_call(
        body, name="mlstm_fwd", grid=(nc,),
        in_specs=[q_spec, k_spec, v_spec,
                  pl.BlockSpec((LCH, GATE_W), lambda c: (c, 0)),
                  pl.BlockSpec((1, GATE_W), lambda c: (0, 0))],
        out_specs=[pl.BlockSpec((LCH, VD), lambda c: (c, 0)),
                   pl.BlockSpec((LCH, GATE_W), lambda c: (c, 0))] + _state_specs(lambda c: c),
        out_shape=[jax.ShapeDtypeStruct((n, VD), MXU),
                   jax.ShapeDtypeStruct((n, GATE_W), F32),
                   jax.ShapeDtypeStruct((NH, nc, DK, DV), F32),
                   jax.ShapeDtypeStruct((NH, nc, 1, DK), F32),
                   jax.ShapeDtypeStruct((NH, nc, 1, GATE_W), F32)],
        scratch_shapes=[pltpu.VMEM((NH, DK, DV), F32), pltpu.VMEM((NH, 1, DK), F32),
                        pltpu.VMEM((NH, 1, GATE_W), F32)],
        compiler_params=_cparams("arbitrary"),
    )(proj, proj, proj, gates, bias)


def _mlstm_bwd(proj, gates, bias, h, dh, stats, cs, ns, ms, dproj):
    n = proj.shape[0]
    nc = n // LCH

    def body(q_ref, k_ref, v_ref, g_ref, bias_ref, h_ref, dh_ref, st_ref,
             cs_ref, ns_ref, ms_ref, dproj_in,
             dqkv_ref, dg_ref, gsum_ref, dc_scr, dn_scr):
        del dproj_in

        @pl.when(pl.program_id(0) == 0)
        def _():
            dc_scr[...] = jnp.zeros_like(dc_scr)
            dn_scr[...] = jnp.zeros_like(dn_scr)
            gsum_ref[...] = jnp.zeros_like(gsum_ref)

        g = g_ref[...] + bias_ref[...]
        gt = g.T[0:8, :]
        b = _chunk_cumsum(g)
        bt = b.T[0:8, :]
        stats = st_ref[...]
        r = lax.broadcasted_iota(jnp.int32, (LCH, LCH), 0)
        c = lax.broadcasted_iota(jnp.int32, (LCH, LCH), 1)
        sub8 = lax.broadcasted_iota(jnp.int32, (8, LCH), 0)
        last = lax.broadcasted_iota(jnp.int32, (LCH, 1), 0) == LCH - 1
        dli_all = jnp.zeros((LCH, GATE_W), F32)
        db_all = jnp.zeros((LCH, GATE_W), F32)
        colsum_t = jnp.zeros((8, LCH), F32)
        for hd in range(NH):
            m_prev = ms_ref[hd][0:1, 0:1]
            b_col = _lane_get(b, NH + hd)
            tri, rr, a_col, m_new, w_col, decay = _chunk_decay(
                _lane_get(g, hd), _sub_row(gt, hd), b_col, _sub_row(bt, NH + hd), m_prev)
            m_col = _lane_get(stats, NH + hd)
            dl = jnp.where(tri, jnp.exp((b_col - m_col) + rr), 0.0)
            inter = jnp.exp(a_col - m_col)

            qs = q_ref[:, hd * DK:(hd + 1) * DK] * (DK ** -0.5)
            kk = k_ref[:, hd * DK:(hd + 1) * DK]
            vv = v_ref[:, hd * DV:(hd + 1) * DV]
            qf = qs.astype(F32)
            kf = kk.astype(F32)
            cst = cs_ref[hd]
            nst = ns_ref[hd]
            cb = cst.astype(MXU)
            dcn = dc_scr[hd]
            dnn = dn_scr[hd]
            dcb = dcn.astype(MXU)

            den = _lane_get(stats, hd)
            floor = jnp.exp(-m_col)
            nrm = jnp.maximum(jnp.abs(den), floor)
            dhv = dh_ref[:, hd * DV:(hd + 1) * DV].astype(F32)
            rn = 1.0 / nrm
            dnum = dhv * rn
            dnum_b = dnum.astype(MXU)
            dhh = jnp.sum(dhv * h_ref[:, hd * DV:(hd + 1) * DV].astype(F32), axis=1, keepdims=True)
            dden = jnp.where(jnp.abs(den) > floor, -dhh * rn * jnp.sign(den), 0.0)

            sc = _dot_nt(qs, kk) * dl
            dsc = _dot_nt(dnum_b, vv) + dden
            da = (dl * dsc).astype(MXU)
            gmat = sc * dsc

            dq_state = _dot_nt(dnum_b, cb) + dden * nst
            dq = _dot(da, kk) + inter * dq_state
            dk_state = w_col * (_dot_nt(vv, dcb) + dnn)
            dk = _dot_tn(da, qs) + dk_state
            kw = (kf * w_col).astype(MXU)
            dv = _dot_tn(sc.astype(MXU), dnum_b) + _dot(kw, dcb)
            dqkv_ref[:, hd * DK:(hd + 1) * DK] = (dq * (DK ** -0.5)).astype(dqkv_ref.dtype)
            dqkv_ref[:, QK + hd * DK:QK + (hd + 1) * DK] = dk.astype(dqkv_ref.dtype)
            dqkv_ref[:, 2 * QK + hd * DV:2 * QK + (hd + 1) * DV] = dv.astype(dqkv_ref.dtype)

            e_col = inter * jnp.sum(qf * dq_state, axis=1, keepdims=True)
            h_col = jnp.sum(kf * dk_state, axis=1, keepdims=True)
            f_dec = decay * (jnp.sum(jnp.sum(cst * dcn, axis=1, keepdims=True), axis=0, keepdims=True)
                             + jnp.sum(nst * dnn, axis=1, keepdims=True))
            row_g = jnp.sum(gmat, axis=1, keepdims=True)
            col_g = jnp.sum(gmat, axis=0, keepdims=True)
            colsum_t = colsum_t + jnp.where(sub8 == hd, col_g, 0.0) - jnp.where(sub8 == NH + hd, col_g, 0.0)
            db_col = row_g + e_col - h_col
            db_col = db_col + jnp.where(last, jnp.sum(h_col, axis=0, keepdims=True) + f_dec, 0.0)
            dli_all = dli_all + _lane_put(h_col, hd)
            db_all = db_all + _lane_put(db_col, NH + hd)

            dc_scr[hd] = decay * dcn + _dot_tn((qf * inter).astype(MXU), dnum_b)
            dn_scr[hd] = decay * dnn + jnp.sum(qf * (inter * dden), axis=0, keepdims=True)
        colsum = jnp.concatenate([colsum_t, jnp.zeros((GATE_W - 8, LCH), F32)], axis=0).T
        lane = lax.broadcasted_iota(jnp.int32, (LCH, GATE_W), 1)
        dli_all = dli_all + jnp.where(lane < NH, colsum, 0.0)
        db_all = db_all + jnp.where(lane >= NH, colsum, 0.0)
        dlf_all = jnp.dot(jnp.where(c >= r, 1.0, 0.0), db_all, precision=lax.Precision.HIGHEST,
                          preferred_element_type=F32)
        dg = dli_all + dlf_all * _sigmoid(-g)
        dg_ref[...] = dg
        gsum_ref[0:1, 0:GATE_W] += jnp.sum(dg, axis=0, keepdims=True)

    rev = lambda c: nc - 1 - c
    q_spec, k_spec, v_spec = _qkv_specs(rev)
    hv_spec = pl.BlockSpec((LCH, VD), lambda c: (rev(c), 0))
    gate_spec = pl.BlockSpec((LCH, GATE_W), lambda c: (rev(c), 0))
    return pl.pallas_call(
        body, name="mlstm_bwd", grid=(nc,),
        in_specs=[q_spec, k_spec, v_spec, gate_spec,
                  pl.BlockSpec((1, GATE_W), lambda c: (0, 0)),
                  hv_spec, hv_spec, gate_spec] + _state_specs(rev) + [ANY],
        out_specs=[pl.BlockSpec((LCH, QKV_W), lambda c: (rev(c), C_Q // QKV_W)),
                   gate_spec,
                   pl.BlockSpec((8, D), lambda c: (0, 0))],
        out_shape=[jax.ShapeDtypeStruct(dproj.shape, dproj.dtype),
                   jax.ShapeDtypeStruct((n, GATE_W), F32),
                   jax.ShapeDtypeStruct((8, D), F32)],
        scratch_shapes=[pltpu.VMEM((NH, DK, DV), F32), pltpu.VMEM((NH, 1, DK), F32)],
        input_output_aliases={11: 0},
        compiler_params=_cparams("arbitrary"),
    )(proj, proj, proj, gates, bias, h, dh, stats, cs, ns, ms, dproj)


def _proj_spec(tm, col, width):
    return pl.BlockSpec((tm, width), lambda i: (i, col // width))


def _tail_in_specs(tm):
    return [_proj_spec(tm, 0, TAIL_W), _proj_spec(tm, C_XA, CONV_W),
            pl.BlockSpec((8, CONV_W), lambda i: (jnp.maximum(i * (tm // 8) - 1, 0), C_XA // CONV_W))]


def _tail_views(tail_r, conv_r, halo_r):
    cols = lambda ref, c0, w: ref.at[:, pl.ds(c0, w)]
    return (cols(tail_r, C_BA, D), cols(tail_r, C_ZA, D), cols(tail_r, C_O, VD), cols(tail_r, C_ZB, VD),
            cols(tail_r, C_GA, D), cols(tail_r, C_GB, D), cols(conv_r, 0, D), cols(conv_r, D, D),
            cols(halo_r, 0, D), cols(halo_r, D, D))


def _const(shape):
    return pl.BlockSpec(shape, lambda i: (0,) * len(shape))


def _conv_inputs(i, tm, xa_ref, ca_ref, xah_ref, cah_ref):
    u = _f32(xa_ref) * _f32(ca_ref)
    uh = jnp.where(i > 0, _f32(xah_ref) * _f32(cah_ref), 0.0)
    rid = lax.broadcasted_iota(jnp.int32, u.shape, 0)
    u1 = jnp.where(rid == 0, uh[7:8, :], pltpu.roll(u, 1, 0))
    u2 = jnp.where(rid == 0, uh[6:7, :], jnp.where(rid == 1, uh[7:8, :], pltpu.roll(u, 2, 0)))
    return u, u1, u2


def _head_norm(hh, gh):
    out = []
    for j in range(NH):
        hj = hh[:, j * DV:(j + 1) * DV]
        rj = _rstd(hj)
        out.append((hj * rj, rj, gh[:, j * DV:(j + 1) * DV]))
    return out


def _tail_fwd(proj, h, x, p, t, cw, cb, gh, gple, gfin, wa, wb, wo, wpg, wp):
    n = x.shape[0]
    tm = min(256, n)

    def body(tail_r, conv_r, halo_r, h_r, x_r, p_r, t_r,
             cw_r, cb_r, gh_r, gple_r, gfin_r, wa_r, wb_r, wo_r, wpg_r, wp_r,
             apre_o, bpre_o, mg_o, xn1_o, de_o, dgp_o, ya_o, yb_o, x1_o, dx2_o, acc_o):
        ba_r, za_r, o_r, zb_r, ga_r, gb_r, xa_r, ca_r, xah_r, cah_r = _tail_views(tail_r, conv_r, halo_r)
        i = pl.program_id(0)

        @pl.when(i == 0)
        def _():
            acc_o[...] = jnp.zeros_like(acc_o)

        u, u1, u2 = _conv_inputs(i, tm, xa_r, ca_r, xah_r, cah_r)
        cwv = cw_r[...]
        cv = cwv[0:1, :] * u2 + cwv[1:2, :] * u1 + cwv[2:3, :] * u + cb_r[...]
        za = za_r[...]
        a_pre = ba_r[...] * cv.astype(MXU) * (za * _sigmoid(za))
        apre_o[...] = a_pre
        ya = _dot(a_pre, wa_r[...]).astype(MXU)

        hn = _head_norm(_f32(h_r), gh_r[...])
        hbn = jnp.concatenate([xh * g for xh, _, g in hn], axis=1)
        zb = zb_r[...]
        b_pre = _sigmoid(o_r[...]) * hbn.astype(MXU) * (zb * _sigmoid(zb))
        bpre_o[...] = b_pre
        yb = _dot(b_pre, wb_r[...]).astype(MXU)
        ya_o[...] = ya
        yb_o[...] = yb

        mg = _sigmoid(ga_r[...]) * ya + _sigmoid(gb_r[...]) * yb
        mg_o[...] = mg
        x1 = x_r[...] + _dot(mg, wo_r[...])
        x1_o[...] = x1.astype(MXU)
        xn1 = (x1 * _rstd(x1) * gple_r[...]).astype(MXU)
        xn1_o[...] = xn1
        gt = _sigmoid(_dot(xn1, wpg_r[...]))
        e = _dot(p_r[...].astype(MXU), wp_r[...])
        x2 = x1 + gt * e
        r2 = _rstd(x2)
        xh2 = x2 * r2
        gf = gfin_r[...]
        diff = xh2 * gf - t_r[...]
        dy = diff * (1.0 / D)
        dx2 = _norm_bwd(dy, xh2, r2, gf)
        dx2_o[...] = dx2
        de_o[...] = (dx2 * gt).astype(MXU)
        dgp_o[...] = (dx2 * e * gt * (1.0 - gt)).astype(MXU)
        acc_o[0:1, :] += jnp.sum(dy * xh2, axis=0, keepdims=True)
        loss = 0.5 * jnp.sum(jnp.sum(diff * diff, axis=1, keepdims=True), axis=0, keepdims=True) * (1.0 / D)
        acc_o[1:2, :] += jnp.broadcast_to(loss, (1, D))

    row = lambda w, dt: (pl.BlockSpec((tm, w), lambda i: (i, 0)), jax.ShapeDtypeStruct((n, w), dt))
    outs = [row(D, MXU), row(VD, MXU), row(D, MXU), row(D, MXU), row(D, MXU), row(D, MXU),
            row(D, MXU), row(D, MXU), row(D, MXU), row(D, F32),
            (_const((8, D)), jax.ShapeDtypeStruct((8, D), F32))]
    return pl.pallas_call(
        body, name="tail_fwd", grid=(n // tm,),
        in_specs=_tail_in_specs(tm) + [
                  pl.BlockSpec((tm, VD), lambda i: (i, 0)),
                  pl.BlockSpec((tm, D), lambda i: (i, 0)),
                  pl.BlockSpec((tm, PLE), lambda i: (i, 0)),
                  pl.BlockSpec((tm, D), lambda i: (i, 0)),
                  _const((8, D)), _const((1, D)), _const((1, VD)), _const((1, D)), _const((1, D)),
                  _const((D, D)), _const((VD, D)), _const((D, D)), _const((D, D)), _const((PLE, D))],
        out_specs=[s for s, _ in outs],
        out_shape=[s for _, s in outs],
        compiler_params=_cparams("arbitrary"),
    )(proj, proj, proj, h, x, p, t, cw, cb, gh, gple, gfin, wa, wb, wo, wpg, wp)


def _tail_bwd(proj, h, dgp, dx2, x1, ya, yb, cw, cb, gh, gple, wpg, wo, wb, wa):
    n = x1.shape[0]
    tm = min(256, n)

    def body(tail_r, conv_r, halo_r, h_r,
             dgp_r, dx2_r, x1_r, ya_r, yb_r, cw_r, cb_r, gh_r, gple_r,
             wpg_r, wo_r, wb_r, wa_r,
             dproj_o, dcv_o, dh_o, dx1_o, dx1b_o, dya_o, dyb_o, acc_o):
        ba_r, za_r, o_r, zb_r, ga_r, gb_r, xa_r, ca_r, xah_r, cah_r = _tail_views(tail_r, conv_r, halo_r)
        i = pl.program_id(0)

        @pl.when(i == 0)
        def _():
            acc_o[...] = jnp.zeros_like(acc_o)

        dxn1 = _dot_nt(dgp_r[...], wpg_r[...])
        x1 = _f32(x1_r)
        r1 = _rstd(x1)
        xh1 = x1 * r1
        acc_o[0:1, 0:D] += jnp.sum(dxn1 * xh1, axis=0, keepdims=True)
        dx1 = dx2_r[...] + _norm_bwd(dxn1, xh1, r1, gple_r[...])
        dx1_o[...] = dx1
        dx1b = dx1.astype(MXU)
        dx1b_o[...] = dx1b

        dmg = _dot_nt(dx1b, wo_r[...]).astype(MXU)
        sga, dsga = _gate(ga_r[...])
        sgb, dsgb = _gate(gb_r[...])
        dya = dmg * sga
        dyb = dmg * sgb
        dya_o[...] = dya
        dyb_o[...] = dyb
        dproj_o[:, C_GA:C_GA + D] = dmg * ya_r[...] * dsga
        dproj_o[:, C_GB:C_GB + D] = dmg * yb_r[...] * dsgb

        db_pre = _dot_nt(dyb, wb_r[...]).astype(MXU)
        hn = _head_norm(_f32(h_r), gh_r[...])
        hbn = jnp.concatenate([xh * g for xh, _, g in hn], axis=1).astype(MXU)
        so, dso = _gate(o_r[...])
        zb = zb_r[...]
        szb, dszb = _gate(zb)
        sb = zb * szb
        t1 = db_pre * hbn
        dproj_o[:, C_O:C_O + VD] = t1 * (sb * dso)
        dproj_o[:, C_ZB:C_ZB + VD] = t1 * (so * (szb + zb * dszb))
        dhbn = (db_pre * (so * sb)).astype(F32)
        for j, (xh, rj, g) in enumerate(hn):
            dj = dhbn[:, j * DV:(j + 1) * DV]
            acc_o[1:2, j * DV:(j + 1) * DV] += jnp.sum(dj * xh, axis=0, keepdims=True)
            dh_o[:, j * DV:(j + 1) * DV] = _norm_bwd(dj, xh, rj, g).astype(MXU)

        da_pre = _dot_nt(dya, wa_r[...]).astype(MXU)
        u, u1, u2 = _conv_inputs(i, tm, xa_r, ca_r, xah_r, cah_r)
        cwv = cw_r[...]
        cv = cwv[0:1, :] * u2 + cwv[1:2, :] * u1 + cwv[2:3, :] * u + cb_r[...]
        za = za_r[...]
        sza, dsza = _gate(za)
        sa = za * sza
        ba = ba_r[...]
        t2 = da_pre * cv.astype(MXU)
        dproj_o[:, C_BA:C_BA + D] = t2 * sa
        dproj_o[:, C_ZA:C_ZA + D] = t2 * (ba * (sza + za * dsza))
        dcv_b = da_pre * (ba * sa)
        dcv_o[...] = dcv_b
        dcv = dcv_b.astype(F32)
        acc_o[2:3, 0:D] += jnp.sum(dcv, axis=0, keepdims=True)
        acc_o[3:4, 0:D] += jnp.sum(dcv * u2, axis=0, keepdims=True)
        acc_o[4:5, 0:D] += jnp.sum(dcv * u1, axis=0, keepdims=True)
        acc_o[5:6, 0:D] += jnp.sum(dcv * u, axis=0, keepdims=True)

    row = lambda w, dt: (pl.BlockSpec((tm, w), lambda i: (i, 0)), jax.ShapeDtypeStruct((n, w), dt))
    outs = [(pl.BlockSpec((tm, TAIL_W), lambda i: (i, 0)), jax.ShapeDtypeStruct((n, NMAIN), MXU)),
            row(D, MXU), row(VD, MXU), row(D, F32), row(D, MXU), row(D, MXU), row(D, MXU),
            (_const((8, VD)), jax.ShapeDtypeStruct((8, VD), F32))]
    rowin = lambda w: pl.BlockSpec((tm, w), lambda i: (i, 0))
    return pl.pallas_call(
        body, name="tail_bwd", grid=(n // tm,),
        in_specs=_tail_in_specs(tm) + [
                  rowin(VD), rowin(D), rowin(D), rowin(D), rowin(D), rowin(D),
                  _const((8, D)), _const((1, D)), _const((1, VD)), _const((1, D)),
                  _const((D, D)), _const((D, D)), _const((VD, D)), _const((D, D))],
        out_specs=[s for s, _ in outs],
        out_shape=[s for _, s in outs],
        compiler_params=_cparams("arbitrary"),
    )(proj, proj, proj, h, dgp, dx2, x1, ya, yb, cw, cb, gh, gple, wpg, wo, wb, wa)


def _conv_bwd(proj, dcv, cw, dproj):
    n = dcv.shape[0]
    tm = min(512, n)
    nt = n // tm

    def body(xa_r, ca_r, dcv_r, nxt_r, cw_r, dproj_in, dxc_o):
        del dproj_in
        i = pl.program_id(0)
        dcv_v = _f32(dcv_r)
        nxt = jnp.where(i < nt - 1, _f32(nxt_r), 0.0)
        rid = lax.broadcasted_iota(jnp.int32, dcv_v.shape, 0)
        d1 = jnp.where(rid == tm - 1, nxt[0:1, :], pltpu.roll(dcv_v, tm - 1, 0))
        d2 = jnp.where(rid == tm - 2, nxt[0:1, :],
                       jnp.where(rid == tm - 1, nxt[1:2, :], pltpu.roll(dcv_v, tm - 2, 0)))
        cwv = cw_r[...]
        du = cwv[2:3, :] * dcv_v + cwv[1:2, :] * d1 + cwv[0:1, :] * d2
        dxc_o[:, 0:D] = (du * _f32(ca_r)).astype(MXU)
        dxc_o[:, D:2 * D] = (du * _f32(xa_r)).astype(MXU)

    return pl.pallas_call(
        body, name="conv_bwd", grid=(nt,),
        in_specs=[_proj_spec(tm, C_XA, 1024), _proj_spec(tm, C_CA, 1024),
                  pl.BlockSpec((tm, D), lambda i: (i, 0)),
                  pl.BlockSpec((8, D), lambda i: (jnp.minimum((i + 1) * (tm // 8), n // 8 - 1), 0)),
                  _const((8, D)), ANY],
        out_specs=pl.BlockSpec((tm, CONV_W), lambda i: (i, C_XA // CONV_W)),
        out_shape=jax.ShapeDtypeStruct(dproj.shape, dproj.dtype),
        input_output_aliases={5: 0},
        compiler_params=_cparams("arbitrary"),
    )(proj, proj, dcv, dcv, cw, dproj)


def _position():
    return lax.axis_index("x"), lax.axis_index("y"), lax.axis_index("c")


def _flip(v, bit):
    return 1 - v if bit else v


def _part_start(core, stride, tiled):
    return pl.multiple_of(core * stride, stride) if tiled else core * stride


def _scatter_copies(srcs, dsts, strides, send_sems, recv_sems, local_sems):
    x, y, c = _position()
    me = 4 * x + 2 * y + c
    na = len(srcs)
    copies = []
    for r in range(N_DEV):
        px, py, pc = _flip(x, (r >> 2) & 1), _flip(y, (r >> 1) & 1), _flip(c, r & 1)
        for a in range(na):
            rows = dsts[a].shape[1]
            src = srcs[a].at[2 * px + py, pl.ds(_part_start(pc, strides[a], len(dsts[a].shape) == 3), rows)]
            dst = dsts[a].at[me]
            if r == 0:
                copies.append(pltpu.make_async_copy(src, dst, local_sems.at[a]))
            else:
                k = (r - 1) * na + a
                copies.append(pltpu.make_async_remote_copy(
                    src_ref=src, dst_ref=dst, send_sem=send_sems.at[k], recv_sem=recv_sems.at[k],
                    device_id=(px, py, pc), device_id_type=MESH))
    return copies


def _input_grad(dproj, dgates, w_t, wg_t, x, dx1, g_mix, g_in, g_rest):
    n = x.shape[0]
    tm, tk = min(1024, n), 2048
    nk = NMAIN // tk
    nt = n // tm

    def body(dp_r, dg_r, w_r, wg_r, x_r, dx1_r, g_r, gin, grest,
             gx_o, acc_o, oin, orest, acc, send_sems, recv_sems, local_sems):
        i = pl.program_id(0)
        kk = pl.program_id(1)
        copies = _scatter_copies((gin, grest), (oin, orest), strides, send_sems, recv_sems, local_sems)

        @pl.when((i == 0) & (kk == 0))
        def _():
            acc_o[...] = jnp.zeros_like(acc_o)
            for cp in copies:
                cp.start()

        @pl.when(kk == 0)
        def _():
            acc[...] = _dot(dg_r[...].astype(MXU), wg_r[...])

        acc[...] += _dot(dp_r[...], w_r[...])

        @pl.when(kk == nk - 1)
        def _():
            dhn = acc[...]
            xv = x_r[...]
            r0 = _rstd(xv)
            xh = xv * r0
            acc_o[0:1, :] += jnp.sum(dhn * xh, axis=0, keepdims=True)
            gx_o[...] = dx1_r[...] + _norm_bwd(dhn, xh, r0, g_r[...])

        @pl.when((i == nt - 1) & (kk == nk - 1))
        def _():
            for cp in copies:
                cp.wait()

    nrem = 2 * (N_DEV - 1)
    r_in, r_rest = HALF_SLABS, g_rest.shape[1] // 2
    strides = (HALF_START, r_rest)
    return pl.pallas_call(
        body, name="input_grad", grid=(nt, nk),
        in_specs=[pl.BlockSpec((tm, tk), lambda i, kk: (i, kk)),
                  pl.BlockSpec((tm, GATE_W), lambda i, kk: (i, 0)),
                  pl.BlockSpec((tk, D), lambda i, kk: (kk, 0)),
                  pl.BlockSpec((GATE_W, D), lambda i, kk: (0, 0)),
                  pl.BlockSpec((tm, D), lambda i, kk: (i, 0)),
                  pl.BlockSpec((tm, D), lambda i, kk: (i, 0)),
                  pl.BlockSpec((1, D), lambda i, kk: (0, 0)),
                  ANY, ANY],
        out_specs=[pl.BlockSpec((tm, D), lambda i, kk: (i, 0)),
                   pl.BlockSpec((8, D), lambda i, kk: (0, 0)),
                   ANY, ANY],
        out_shape=[jax.ShapeDtypeStruct((n, D), F32), jax.ShapeDtypeStruct((8, D), F32),
                   jax.ShapeDtypeStruct((N_DEV, r_in) + g_in.shape[2:], g_in.dtype),
                   jax.ShapeDtypeStruct((N_DEV, r_rest) + g_rest.shape[2:], g_rest.dtype)],
        scratch_shapes=[pltpu.VMEM((tm, D), F32),
                        pltpu.SemaphoreType.DMA((nrem,)), pltpu.SemaphoreType.DMA((nrem,)),
                        pltpu.SemaphoreType.DMA((2,))],
        compiler_params=_cparams("arbitrary", "arbitrary"),
    )(dproj, dgates, w_t, wg_t, x, dx1, g_mix, g_in, g_rest)


def _pack_small(acc_f, acc_b, acc_x, gsum):
    def body(f_r, b_r, x_r, s_r, o_r):
        o_r[...] = jnp.zeros_like(o_r)
        o_r[0:1, :] = x_r[0:1, :]
        o_r[1:2, :] = b_r[2:3, 0:D]
        o_r[2:3, :] = b_r[1:2, 0:D]
        o_r[3:4, :] = b_r[1:2, D:2 * D]
        o_r[4:5, :] = b_r[0:1, 0:D]
        o_r[5:6, :] = f_r[0:1, :]
        o_r[6:9, :] = b_r[3:6, 0:D]
        lane = lax.broadcasted_iota(jnp.int32, (1, D), 1)
        o_r[9:10, :] = jnp.where(lane < 2 * NH, s_r[0:1, :], jnp.where(lane == 2 * NH, f_r[1:2, :], 0.0))

    return pl.pallas_call(
        body, name="pack_small",
        out_shape=jax.ShapeDtypeStruct((SMALL_ROWS, D), F32),
    )(acc_f, acc_b, acc_x, gsum)


def _sum_slots(r, name, tr=64, half=None):
    s, rows, w = r.shape
    tr = min(tr, rows)
    assert rows % tr == 0
    nt = rows // tr

    def body(*refs):
        r_ref, o_ref = refs[-2:]
        tot = r_ref[0].astype(F32)
        for k in range(1, s):
            tot = tot + r_ref[k].astype(F32)
        o_ref[...] = tot

    if half is None:
        return pl.pallas_call(
            body, name=name, grid=(nt,),
            in_specs=[pl.BlockSpec((s, tr, w), lambda i: (0, i, 0))],
            out_specs=pl.BlockSpec((tr, w), lambda i: (i, 0)),
            out_shape=jax.ShapeDtypeStruct((rows, w), F32),
            compiler_params=_cparams("arbitrary"),
        )(r)
    return pl.pallas_call(
        body, name=name,
        grid_spec=pltpu.PrefetchScalarGridSpec(
            num_scalar_prefetch=1, grid=(nt,),
            in_specs=[pl.BlockSpec((s, tr, w), lambda i, hf: (0, i, 0))],
            out_specs=pl.BlockSpec((tr, w), lambda i, hf: (hf[0] * nt + i, 0))),
        out_shape=jax.ShapeDtypeStruct((2 * rows, w), F32),
        compiler_params=_cparams("arbitrary"),
    )(half, r)


def _cast_slabs(a, tr=163):
    rows = a.shape[0]
    assert rows % tr == 0

    def body(a_ref, o_ref):
        o_ref[...] = a_ref[...].astype(MXU)

    spec = pl.BlockSpec((tr,) + a.shape[1:], lambda i: (i, 0, 0))
    return pl.pallas_call(
        body, name="cast_w_in", grid=(rows // tr,), in_specs=[spec], out_specs=spec,
        out_shape=jax.ShapeDtypeStruct(a.shape, MXU), compiler_params=_cparams("arbitrary"),
    )(a)


def _sum_slabs(r, name, tr=69):
    s, rows = r.shape[:2]
    assert rows % tr == 0

    def body(r_ref, o_ref):
        tot = r_ref[0].astype(F32)
        for k in range(1, s):
            tot = tot + r_ref[k].astype(F32)
        o_ref[...] = tot

    return pl.pallas_call(
        body, name=name, grid=(rows // tr,),
        in_specs=[pl.BlockSpec((s, tr) + r.shape[2:], lambda i: (0, i, 0, 0))],
        out_specs=pl.BlockSpec((tr,) + r.shape[2:], lambda i: (i, 0, 0)),
        out_shape=jax.ShapeDtypeStruct(r.shape[1:], F32),
        compiler_params=_cparams("arbitrary"),
    )(r)


def _adamw_body(w_r, g_r, m_r, v_r, d_o, m_o, v_o):
    c1 = 1.0 - ADAM_B1 ** ADAM_STEP
    c2 = 1.0 - ADAM_B2 ** ADAM_STEP
    gv = g_r[...]
    mn = ADAM_B1 * m_r[...] + (1.0 - ADAM_B1) * gv
    vn = ADAM_B2 * v_r[...] + (1.0 - ADAM_B2) * (gv * gv)
    m_o[...] = mn
    v_o[...] = vn
    d_o[...] = -ADAM_LR * ((mn / c1) / (jnp.sqrt(vn / c2) + ADAM_EPS) + ADAM_WD * w_r[...])


def _adamw_slabs(w, g, m, v, name, tr):
    rows = w.shape[0]
    assert rows % tr == 0 and w.shape == g.shape

    def body(*refs):
        _adamw_body(*refs)

    spec = pl.BlockSpec((tr,) + w.shape[1:], lambda i: (i, 0, 0))
    shp = jax.ShapeDtypeStruct(w.shape, F32)
    return pl.pallas_call(
        body, name=name, grid=(rows // tr,),
        in_specs=[spec] * 4, out_specs=[spec] * 3, out_shape=[shp] * 3,
        compiler_params=_cparams("arbitrary"),
    )(w, g, m, v)


def _adamw(w, g, m, v, name, g_row0=0, tr=64):
    lead = w.ndim == 3
    rows, cols = w.shape[-2:]
    tr = min(tr, rows)
    assert rows % tr == 0 and g_row0 % tr == 0 and g.shape[1] == cols

    def body(*refs):
        _adamw_body(*refs)

    if lead:
        spec = pl.BlockSpec((None, tr, cols), lambda i: (0, i, 0))
    else:
        spec = pl.BlockSpec((tr, cols), lambda i: (i, 0))
    g_spec = pl.BlockSpec((tr, cols), lambda i: (g_row0 // tr + i, 0))
    shp = jax.ShapeDtypeStruct(w.shape, F32)
    return pl.pallas_call(
        body, name=name, grid=(rows // tr,),
        in_specs=[spec, g_spec, spec, spec], out_specs=[spec] * 3, out_shape=[shp] * 3,
        compiler_params=_cparams("arbitrary"),
    )(w, g, m, v)


def _allgather_chips(halved, whole):
    arrs = [a for a, _ in halved] + list(whole)
    parts = [p for _, p in halved]
    nh, na = len(halved), len(arrs)
    relations = ((1, 0), (0, 1), (1, 1))
    n_ici = 3 * na

    def body(*refs):
        ins, outs = refs[:na], refs[na:2 * na]
        send_sems, recv_sems = refs[2 * na:]
        x, y, c = _position()
        mine = 2 * x + y

        def rows(a, core, stride, size):
            return pl.ds(_part_start(core, stride, len(ins[a].shape) == 2), size)

        def remote(src, dst, k, to):
            return pltpu.make_async_remote_copy(src_ref=src, dst_ref=dst, send_sem=send_sems.at[k],
                                                recv_sem=recv_sems.at[k], device_id=to, device_id_type=MESH)

        first, passed = [], []
        for j, (rx, ry) in enumerate(relations):
            px, py = _flip(x, rx), _flip(y, ry)
            for a in range(na):
                if a < nh:
                    s1, n1, s2, n2 = parts[a]
                    first.append(remote(ins[a].at[rows(a, c, s1, n1)], outs[a].at[mine, rows(a, c, s1, n1)],
                                        j * na + a, (px, py, c)))
                    landed = outs[a].at[2 * px + py, rows(a, c, s2, n2)]
                    passed.append(remote(landed, landed, n_ici + j * nh + a, (x, y, 1 - c)))
                else:
                    first.append(remote(ins[a], outs[a].at[mine], j * na + a, (px, py, c)))
        for cp in first:
            cp.start()
        k = 0
        for j in range(3):
            for a in range(na):
                first[j * na + a].wait_recv()
                if a < nh:
                    passed[k].start()
                    k += 1
        for cp in passed:
            cp.wait_recv()
        for cp in first + passed:
            cp.wait_send()

    nsem = n_ici + 3 * nh
    outs = pl.pallas_call(
        body, name="allgather_weights",
        in_specs=[ANY] * na, out_specs=[ANY] * na,
        out_shape=[jax.ShapeDtypeStruct((N_CHIPS,) + a.shape, a.dtype) for a in arrs],
        scratch_shapes=[pltpu.SemaphoreType.DMA((nsem,)), pltpu.SemaphoreType.DMA((nsem,))],
    )(*arrs)
    chip = 2 * lax.axis_index("x") + lax.axis_index("y")
    return [lax.dynamic_update_slice(o, a[None], (chip,) + (0,) * a.ndim) for o, a in zip(outs, arrs)]


def _join_halves(piece, full_rest, small):
    def body(pc_ref, rest_in, sm, oin, orest, osm, send_sems, recv_sems, local_sem):
        del rest_in
        x, y, c = _position()
        me = 4 * x + 2 * y + c
        sib = (x, y, 1 - c)
        rows = orest.shape[0] // 2
        mine = orest.at[pl.ds(pl.multiple_of(c * rows, rows), rows)]
        copies = [
            pltpu.make_async_remote_copy(
                src_ref=pc_ref.at[pl.ds(c, HALF_START)], dst_ref=oin.at[pl.ds(c * HALF_SLABS, HALF_START)],
                send_sem=send_sems.at[0], recv_sem=recv_sems.at[0], device_id=sib, device_id_type=MESH),
            pltpu.make_async_remote_copy(
                src_ref=mine, dst_ref=mine, send_sem=send_sems.at[1], recv_sem=recv_sems.at[1],
                device_id=sib, device_id_type=MESH)]
        na = 2
        copies.append(pltpu.make_async_copy(sm, osm.at[me], local_sem))
        for r in range(1, N_DEV):
            to = (_flip(x, (r >> 2) & 1), _flip(y, (r >> 1) & 1), _flip(c, r & 1))
            copies.append(pltpu.make_async_remote_copy(
                src_ref=sm, dst_ref=osm.at[me], send_sem=send_sems.at[na + r - 1],
                recv_sem=recv_sems.at[na + r - 1], device_id=to, device_id_type=MESH))
        for cp in copies:
            cp.start()
        for cp in copies:
            cp.wait()

    nsem = 2 + N_DEV - 1
    return pl.pallas_call(
        body, name="join_halves",
        in_specs=[ANY] * 3, out_specs=[ANY] * 3,
        out_shape=[jax.ShapeDtypeStruct((SLABS_CHIP,) + piece.shape[1:], piece.dtype),
                   jax.ShapeDtypeStruct(full_rest.shape, full_rest.dtype),
                   jax.ShapeDtypeStruct((N_DEV,) + small.shape, small.dtype)],
        scratch_shapes=[pltpu.SemaphoreType.DMA((nsem,)), pltpu.SemaphoreType.DMA((nsem,)),
                        pltpu.SemaphoreType.DMA],
        input_output_aliases={1: 1},
    )(piece, full_rest, small)


def _to_internal(w_slabs):
    order = sorted(SEGMENTS, key=lambda s: s[2])
    main = jnp.concatenate([w_slabs[g0 // 2:(g0 + w) // 2] for g0, w, _ in order], axis=0)
    gate = w_slabs[GATE_COL // 2:GATE_COL // 2 + NH].reshape(2 * NH, D)
    return main.reshape(NMAIN, D), jnp.pad(gate, ((0, GATE_W - 2 * NH), (0, 0)))


def _to_global(main_t, gate_t):
    main = main_t.reshape(NMAIN // 2, 16, D // 8)
    parts = sorted([(g0, main[i0 // 2:(i0 + w) // 2]) for g0, w, i0 in SEGMENTS]
                   + [(GATE_COL, gate_t[0:2 * NH].reshape(NH, 16, D // 8))], key=lambda s: s[0])
    return jnp.concatenate([p for _, p in parts], axis=0)


def _pad_rows(a, rows=8):
    return jnp.pad(a, ((0, rows - a.shape[0]), (0, 0)))


def kernel(x, p, g_mix, w_in, conv_w, conv_b, w_a_out, b_gates, g_head, w_b_out, w_o, g_ple, w_ple_gate, w_ple, g_final, loss_target, m_g_mix, m_w_in, m_conv_w, m_conv_b, m_w_a_out, m_b_gates, m_g_head, m_w_b_out, m_w_o, m_g_ple, m_w_ple_gate, m_w_ple, m_g_final, v_g_mix, v_w_in, v_conv_w, v_conv_b, v_w_a_out, v_b_gates, v_g_head, v_w_b_out, v_w_o, v_g_ple, v_w_ple_gate, v_w_ple, v_g_final):
    chip = 2 * lax.axis_index("x") + lax.axis_index("y")
    xs, ps, ts = x[0], p[0, 0], loss_target[0]
    g_fin = g_final.reshape(1, D)

    w_slabs = _cast_slabs(jnp.transpose(w_in, (2, 0, 1)).reshape(SLABS_CHIP, 16, D // 8))
    rest = [w[0].astype(MXU) for w in (w_a_out, w_b_out, w_o, w_ple_gate, w_ple)]
    even = lambda a: (a.shape[0] // 2,) * 4
    g_win, g_wa, g_wb, g_wo, g_wpg, g_wp, g_cw = _allgather_chips(
        [(w_slabs, (HALF_START, HALF_SLABS, HALF_SLABS, HALF_START))] + [(a, even(a)) for a in rest],
        [_pad_rows(conv_w[0])])
    w_t, wg_t = _to_internal(g_win.reshape(N_IN // 2, 16, D // 8))
    wa, wb, wo, wpg = [g.reshape(-1, D) for g in (g_wa, g_wb, g_wo, g_wpg)]
    wp = jnp.transpose(g_wp, (1, 0, 2)).reshape(PLE, D)
    cw = jnp.transpose(g_cw, (1, 0, 2)).reshape(8, D)

    bias = jnp.pad(b_gates, ((0, 0), (0, GATE_W - 2 * NH)))
    proj, hn, gates = _proj(xs, g_mix, w_t, wg_t)
    h, stats, cs, ns, ms = _mlstm_fwd(proj, gates, bias)
    (a_pre, b_pre, mg, xn1, de, dgp, ya, yb, x1, dx2, acc_f) = _tail_fwd(
        proj, h, xs, ps, ts, cw, conv_b, g_head, g_ple, g_fin, wa, wb, wo, wpg, wp)
    dproj, dcv, dh, dx1, dx1b, dya, dyb, acc_b = _tail_bwd(
        proj, h, dgp, dx2, x1, ya, yb, cw, conv_b, g_head, g_ple, wpg, wo, wb, wa)
    dproj = _conv_bwd(proj, dcv, cw, dproj)
    dproj, dgates, gsum = _mlstm_bwd(proj, gates, bias, h, dh, stats, cs, ns, ms, dproj)
    d_main = _matmul_tn(dproj, hn, "dw_in", out_dtype=WIRE, tk=4096)
    d_gate = _matmul_tn(dgates, hn, "dw_gate", out_dtype=WIRE)
    d_wa = _matmul_tn(a_pre, dya, "dw_a_out", out_dtype=WIRE)
    d_wb = _matmul_tn(b_pre, dyb, "dw_b_out", out_dtype=WIRE)
    d_wo = _matmul_tn(mg, dx1b, "dw_o", out_dtype=WIRE)
    d_wpg = _matmul_tn(xn1, dgp, "dw_ple_gate", out_dtype=WIRE)
    d_wp = _matmul_tn(ps, de, "dw_ple", out_dtype=WIRE)

    g_in = _to_global(d_main, d_gate).reshape(N_CHIPS, SLABS_CHIP, 16, D // 8)
    d_wp_c = jnp.transpose(d_wp.reshape(PLE, N_CHIPS, PLE), (1, 0, 2)).reshape(N_CHIPS, PACK_ROWS[4], D)
    g_rest = jnp.concatenate(
        [d_wa.reshape(N_CHIPS, -1, D), d_wb.reshape(N_CHIPS, -1, D), d_wo.reshape(N_CHIPS, -1, D),
         d_wpg.reshape(N_CHIPS, -1, D), d_wp_c], axis=1)
    grad_x, acc_x, r_in, r_rest = _input_grad(dproj, dgates, w_t, wg_t, xs, dx1, g_mix, g_in, g_rest)
    small = _pack_small(acc_f, acc_b, acc_x, gsum)
    core = lax.axis_index("c").astype(jnp.int32)
    piece = _sum_slabs(r_in, "sum_w_in")
    gw_in, gw_rest, r_small = _join_halves(
        piece, _sum_slots(r_rest, "sum_rest", tr=96, half=core.reshape(1)), small)
    gw_in = lax.dynamic_update_slice(gw_in, piece, (core * HALF_START, 0, 0)).reshape(COLS_CHIP, 8, D // 8)
    gs = _sum_slots(r_small, "sum_small", tr=SMALL_ROWS)

    big = []
    row0 = 0
    for name, w, m, v in (("w_a_out", w_a_out, m_w_a_out, v_w_a_out), ("w_b_out", w_b_out, m_w_b_out, v_w_b_out),
                          ("w_o", w_o, m_w_o, v_w_o), ("w_ple_gate", w_ple_gate, m_w_ple_gate, v_w_ple_gate)):
        big.append((name, w, m, v, gw_rest, row0))
        row0 += w.shape[1]
    g_wp = gw_rest[row0:row0 + PACK_ROWS[4]].reshape(PLE, PLE)
    big.append(("w_ple", w_ple, m_w_ple, v_w_ple, g_wp, 0))
    upd = {name: _adamw(w, g, m, v, "adamw_" + name, g_row0=r0) for name, w, m, v, g, r0 in big}
    g_big = {name: (g if name == "w_ple" else g[r0:r0 + w.shape[1]])[None] for name, w, m, v, g, r0 in big}
    slabs = lambda a: jnp.transpose(a, (2, 0, 1)).reshape(COLS_CHIP, 8, D // 8)
    unslab = lambda a: jnp.transpose(a, (1, 2, 0)).reshape(1, D, COLS_CHIP)
    upd["w_in"] = [unslab(u) for u in _adamw_slabs(
        slabs(w_in), gw_in, slabs(m_w_in), slabs(v_w_in), "adamw_w_in", tr=326)]
    g_big["w_in"] = unslab(gw_in)

    lane = lax.broadcasted_iota(jnp.int32, (1, D), 1)
    g_small = jnp.concatenate([gs[0:6], jnp.where(lane < 2 * NH, gs[9:10], 0.0), jnp.zeros((1, D), F32)], axis=0)

    def small_pack(gm, cb_, bg, gh, gp, gf):
        return jnp.concatenate([gm, cb_, gh.reshape(2, D), gp, gf.reshape(1, D),
                                jnp.pad(bg, ((0, 0), (0, D - 2 * NH))), jnp.zeros((1, D), F32)], axis=0)

    ws = small_pack(g_mix, conv_b, b_gates, g_head, g_ple, g_final)
    ms_ = small_pack(m_g_mix, m_conv_b, m_b_gates, m_g_head, m_g_ple, m_g_final)
    vs = small_pack(v_g_mix, v_conv_b, v_b_gates, v_g_head, v_g_ple, v_g_final)
    upd_s = _adamw(ws, g_small, ms_, vs, "adamw_small")
    g_cw_mine = _pad_rows(lax.dynamic_slice(gs[6:9], (0, chip * PLE), (3, PLE)))
    upd_c = _adamw(_pad_rows(conv_w[0]), g_cw_mine, _pad_rows(m_conv_w[0]), _pad_rows(v_conv_w[0]),
                   "adamw_conv_w")

    def leaves(bigs, sm, cwv):
        return [sm[0:1], bigs["w_in"], cwv[0:3][None], sm[1:2], bigs["w_a_out"], sm[6:7, 0:2 * NH],
                sm[2:4].reshape(1, VD), bigs["w_b_out"], bigs["w_o"], sm[4:5], bigs["w_ple_gate"],
                bigs["w_ple"], sm[5]]

    loss = gs[9, 2 * NH]
    outs = [loss, grad_x[None]] + leaves(g_big, g_small, g_cw_mine)
    for k in range(3):
        outs += leaves({name: u[k] for name, u in upd.items()}, upd_s[k], upd_c[k])
    return tuple(outs)
```

```python
import jax
import jax.numpy as jnp
from jax import lax
from jax.experimental import pallas as pl
from jax.experimental.pallas import tpu as pltpu

F32 = jnp.float32
MXU = jnp.bfloat16
WIRE = jnp.bfloat16

D = 1024
NH, DK, DV = 4, 256, 512
VD = NH * DV
PLE = 256
LCH = 256
EPS = 1e-6
N_IN = 14344
NMAIN = 14336
GATE_W = 128
N_CHIPS, N_DEV = 4, 8

C_BA, C_ZA, C_O, C_ZB, C_GA, C_GB = 0, 1024, 2048, 4096, 6144, 7168
QK = NH * DK
C_Q, C_K, C_V, C_XA, C_CA = 8192, 9216, 10240, 12288, 13312
QKV_W = 2 * QK + VD
TAIL_W = 8192
CONV_W = 2048
SEGMENTS = (
    (0, 1024, C_XA), (1024, 1024, C_BA), (2048, 1024, C_CA), (3072, 1024, C_ZA),
    (4096, QKV_W, C_Q), (8192, 4096, C_O), (12296, 2048, C_GA),
)
GATE_COL = 12288

COLS_CHIP = N_IN // N_CHIPS
SLABS_CHIP = COLS_CHIP // 2
HALF_START = SLABS_CHIP // 2
HALF_SLABS = SLABS_CHIP - HALF_START

PACK_ROWS = (256, 512, 256, 256, 64)
SMALL_ROWS = 16

ADAM_LR, ADAM_B1, ADAM_B2, ADAM_EPS, ADAM_WD, ADAM_STEP = 0.001, 0.9, 0.999, 1e-08, 0.01, 10

VMEM_LIMIT = 56 * 1024 * 1024
MESH = pl.DeviceIdType.MESH
ANY = pl.BlockSpec(memory_space=pl.ANY)


def _cparams(*sem):
    return pltpu.CompilerParams(dimension_semantics=sem, vmem_limit_bytes=VMEM_LIMIT)


def _dot(a, b):
    return jnp.dot(a, b, preferred_element_type=F32)


def _dot_nt(a, b):
    return lax.dot_general(a, b, (((1,), (1,)), ((), ())), preferred_element_type=F32)


def _dot_tn(a, b):
    return lax.dot_general(a, b, (((0,), (0,)), ((), ())), preferred_element_type=F32)


def _sigmoid(x):
    return 1.0 / (1.0 + jnp.exp(-x))


def _logsig(x):
    return jnp.minimum(x, 0.0) - jnp.log(1.0 + jnp.exp(-jnp.abs(x)))


GATE_FLOOR = -80.0


def _gate(v):
    e = jnp.exp(-jnp.maximum(v, GATE_FLOOR))
    s = 1.0 / (1.0 + e)
    return s, e * s * s


def _rstd(x):
    return lax.rsqrt(jnp.mean(x * x, axis=-1, keepdims=True) + EPS)


def _norm_bwd(dy, xhat, r, g):
    dxh = dy * g
    return r * (dxh - xhat * jnp.mean(dxh * xhat, axis=-1, keepdims=True))


def _f32(ref):
    return ref[...].astype(F32)


def _proj(x, g_mix, w_t, wg_t, tm=1024, tn=3584):
    n = x.shape[0]
    m = w_t.shape[0]
    tm = min(tm, n)

    def body(x_ref, g_ref, b_ref, wg_ref, o_ref, hn_ref, gate_ref):
        @pl.when(pl.program_id(1) == 0)
        def _():
            xv = x_ref[...]
            hn = (xv * _rstd(xv) * g_ref[...]).astype(MXU)
            hn_ref[...] = hn
            gate_ref[...] = _dot_nt(hn, wg_ref[...])

        o_ref[...] = _dot_nt(hn_ref[...], b_ref[...]).astype(MXU)

    return pl.pallas_call(
        body, name="proj", grid=(n // tm, m // tn),
        in_specs=[pl.BlockSpec((tm, D), lambda i, j: (i, 0)),
                  pl.BlockSpec((1, D), lambda i, j: (0, 0)),
                  pl.BlockSpec((tn, D), lambda i, j: (j, 0)),
                  pl.BlockSpec((GATE_W, D), lambda i, j: (0, 0))],
        out_specs=[pl.BlockSpec((tm, tn), lambda i, j: (i, j)),
                   pl.BlockSpec((tm, D), lambda i, j: (i, 0)),
                   pl.BlockSpec((tm, GATE_W), lambda i, j: (i, 0))],
        out_shape=[jax.ShapeDtypeStruct((n, m), MXU), jax.ShapeDtypeStruct((n, D), MXU),
                   jax.ShapeDtypeStruct((n, GATE_W), F32)],
        compiler_params=_cparams("arbitrary", "arbitrary"),
    )(x, g_mix, w_t, wg_t)


def _matmul_tn(a, b, name, out_dtype=F32, ta=1024, tb=1024, tk=2048):
    n, ka = a.shape
    kb = b.shape[1]
    ta, tb, tk = min(ta, ka), min(tb, kb), min(tk, n)
    nk = n // tk

    def body(a_ref, b_ref, o_ref, acc):
        kk = pl.program_id(2)

        @pl.when(kk == 0)
        def _():
            acc[...] = jnp.zeros_like(acc)

        acc[...] += _dot_tn(a_ref[...].astype(MXU), b_ref[...].astype(MXU))

        @pl.when(kk == nk - 1)
        def _():
            o_ref[...] = acc[...].astype(out_dtype)

    return pl.pallas_call(
        body, name=name, grid=(ka // ta, kb // tb, nk),
        in_specs=[pl.BlockSpec((tk, ta), lambda i, j, kk: (kk, i)),
                  pl.BlockSpec((tk, tb), lambda i, j, kk: (kk, j))],
        out_specs=pl.BlockSpec((ta, tb), lambda i, j, kk: (i, j)),
        out_shape=jax.ShapeDtypeStruct((ka, kb), out_dtype),
        scratch_shapes=[pltpu.VMEM((ta, tb), F32)],
        compiler_params=_cparams("arbitrary", "arbitrary", "arbitrary"),
    )(a, b)


def _sub_row(block_t, j):
    sub = lax.broadcasted_iota(jnp.int32, block_t.shape, 0)
    return jnp.sum(jnp.where(sub == j, block_t, 0.0), axis=0, keepdims=True)


def _chunk_cumsum(g):
    n = g.shape[0]
    r = lax.broadcasted_iota(jnp.int32, (n, n), 0)
    c = lax.broadcasted_iota(jnp.int32, (n, n), 1)
    return jnp.dot(jnp.where(r >= c, 1.0, 0.0), _logsig(g), precision=lax.Precision.HIGHEST,
                   preferred_element_type=F32)


def _chunk_decay(li_col, li_row, b_col, b_row, m_prev):
    n = li_col.shape[0]
    r = lax.broadcasted_iota(jnp.int32, (n, n), 0)
    c = lax.broadcasted_iota(jnp.int32, (n, n), 1)
    tri = r >= c
    b_last = b_col[n - 1:n, :]
    rr = li_row - b_row
    a_col = b_col + m_prev
    g_col = b_last - b_col + li_col
    m_new = jnp.maximum(b_last + m_prev, jnp.max(g_col, axis=0, keepdims=True))
    w_col = jnp.exp(g_col - m_new)
    decay = jnp.exp(b_last + m_prev - m_new)
    return tri, rr, a_col, m_new, w_col, decay


def _qkv_specs(row_of):
    q_spec = pl.BlockSpec((LCH, QK), lambda c: (row_of(c), C_Q // QK))
    k_spec = pl.BlockSpec((LCH, QK), lambda c: (row_of(c), C_K // QK))
    v_spec = pl.BlockSpec((LCH, VD), lambda c: (row_of(c), C_V // VD))
    return q_spec, k_spec, v_spec


def _state_specs(row_of):
    return [pl.BlockSpec((NH, None, DK, DV), lambda c: (0, row_of(c), 0, 0)),
            pl.BlockSpec((NH, None, 1, DK), lambda c: (0, row_of(c), 0, 0)),
            pl.BlockSpec((NH, None, 1, GATE_W), lambda c: (0, row_of(c), 0, 0))]


def _lane_put(col, lane_id, width=GATE_W):
    lane = lax.broadcasted_iota(jnp.int32, (col.shape[0], width), 1)
    return jnp.where(lane == lane_id, col, 0.0)


def _lane_get(block, lane_id):
    lane = lax.broadcasted_iota(jnp.int32, block.shape, 1)
    return jnp.sum(jnp.where(lane == lane_id, block, 0.0), axis=1, keepdims=True)


def _mlstm_fwd(proj, gates, bias):
    n = proj.shape[0]
    nc = n // LCH

    def body(q_ref, k_ref, v_ref, g_ref, bias_ref,
             h_ref, st_ref, cs_ref, ns_ref, ms_ref, c_scr, n_scr, m_scr):
        @pl.when(pl.program_id(0) == 0)
        def _():
            c_scr[...] = jnp.zeros_like(c_scr)
            n_scr[...] = jnp.zeros_like(n_scr)
            m_scr[...] = jnp.full_like(m_scr, -jnp.inf)

        g = g_ref[...] + bias_ref[...]
        gt = g.T[0:8, :]
        b = _chunk_cumsum(g)
        bt = b.T[0:8, :]
        stats = jnp.zeros((LCH, GATE_W), F32)
        for hd in range(NH):
            m_all = m_scr[hd]
            m_prev = m_all[0:1, 0:1]
            b_col = _lane_get(b, NH + hd)
            tri, rr, a_col, m_new, w_col, decay = _chunk_decay(
                _lane_get(g, hd), _sub_row(gt, hd), b_col, _sub_row(bt, NH + hd), m_prev)
            dmat = jnp.where(tri, b_col + rr, -jnp.inf)
            m_col = jnp.maximum(a_col, jnp.max(dmat, axis=1, keepdims=True))
            dl = jnp.exp(dmat - m_col)
            inter = jnp.exp(a_col - m_col)

            qs = q_ref[:, hd * DK:(hd + 1) * DK] * (DK ** -0.5)
            kk = k_ref[:, hd * DK:(hd + 1) * DK]
            vv = v_ref[:, hd * DV:(hd + 1) * DV]
            cst = c_scr[hd]
            nst = n_scr[hd]
            cs_ref[hd] = cst
            ns_ref[hd] = nst
            ms_ref[hd] = m_all

            sc = _dot_nt(qs, kk) * dl
            num = _dot(sc.astype(MXU), vv) + inter * _dot(qs, cst.astype(MXU))
            den = (jnp.sum(sc, axis=1, keepdims=True)
                   + inter * jnp.sum(qs.astype(F32) * nst, axis=1, keepdims=True))
            nrm = jnp.maximum(jnp.abs(den), jnp.exp(-m_col))
            h_ref[:, hd * DV:(hd + 1) * DV] = (num * (1.0 / nrm)).astype(h_ref.dtype)
            stats = stats + _lane_put(den, hd) + _lane_put(m_col, NH + hd)

            kw = kk.astype(F32) * w_col
            c_scr[hd] = decay * cst + _dot_tn(kw.astype(MXU), vv)
            n_scr[hd] = decay * nst + jnp.sum(kw, axis=0, keepdims=True)
            m_scr[hd] = jnp.broadcast_to(m_new, (1, GATE_W))
        st_ref[...] = stats

    q_spec, k_spec, v_spec = _qkv_specs(lambda c: c)
    return pl.pallas_call(
        body, name="mlstm_fwd", grid=(nc,),
        in_specs=[q_spec, k_spec, v_spec,
                  pl.BlockSpec((LCH, GATE_W), lambda c: (c, 0)),
                  pl.BlockSpec((1, GATE_W), lambda c: (0, 0))],
        out_specs=[pl.BlockSpec((LCH, VD), lambda c: (c, 0)),
                   pl.BlockSpec((LCH, GATE_W), lambda c: (c, 0))] + _state_specs(lambda c: c),
        out_shape=[jax.ShapeDtypeStruct((n, VD), MXU),
                   jax.ShapeDtypeStruct((n, GATE_W), F32),
                   jax.ShapeDtypeStruct((NH, nc, DK, DV), F32),
                   jax.ShapeDtypeStruct((NH, nc, 1, DK), F32),
                   jax.ShapeDtypeStruct((NH, nc, 1, GATE_W), F32)],
        scratch_shapes=[pltpu.VMEM((NH, DK, DV), F32), pltpu.VMEM((NH, 1, DK), F32),
                        pltpu.VMEM((NH, 1, GATE_W), F32)],
        compiler_params=_cparams("arbitrary"),
    )(proj, proj, proj, gates, bias)


def _mlstm_bwd(proj, gates, bias, h, dh, stats, cs, ns, ms, dproj):
    n = proj.shape[0]
    nc = n // LCH

    def body(q_ref, k_ref, v_ref, g_ref, bias_ref, h_ref, dh_ref, st_ref,
             cs_ref, ns_ref, ms_ref, dproj_in,
             dqkv_ref, dg_ref, gsum_ref, dc_scr, dn_scr):
        del dproj_in

        @pl.when(pl.program_id(0) == 0)
        def _():
            dc_scr[...] = jnp.zeros_like(dc_scr)
            dn_scr[...] = jnp.zeros_like(dn_scr)
            gsum_ref[...] = jnp.zeros_like(gsum_ref)

        g = g_ref[...] + bias_ref[...]
        gt = g.T[0:8, :]
        b = _chunk_cumsum(g)
        bt = b.T[0:8, :]
        stats = st_ref[...]
        r = lax.broadcasted_iota(jnp.int32, (LCH, LCH), 0)
        c = lax.broadcasted_iota(jnp.int32, (LCH, LCH), 1)
        sub8 = lax.broadcasted_iota(jnp.int32, (8, LCH), 0)
        last = lax.broadcasted_iota(jnp.int32, (LCH, 1), 0) == LCH - 1
        dli_all = jnp.zeros((LCH, GATE_W), F32)
        db_all = jnp.zeros((LCH, GATE_W), F32)
        colsum_t = jnp.zeros((8, LCH), F32)
        for hd in range(NH):
            m_prev = ms_ref[hd][0:1, 0:1]
            b_col = _lane_get(b, NH + hd)
            tri, rr, a_col, m_new, w_col, decay = _chunk_decay(
                _lane_get(g, hd), _sub_row(gt, hd), b_col, _sub_row(bt, NH + hd), m_prev)
            m_col = _lane_get(stats, NH + hd)
            dl = jnp.where(tri, jnp.exp((b_col - m_col) + rr), 0.0)
            inter = jnp.exp(a_col - m_col)

            qs = q_ref[:, hd * DK:(hd + 1) * DK] * (DK ** -0.5)
            kk = k_ref[:, hd * DK:(hd + 1) * DK]
            vv = v_ref[:, hd * DV:(hd + 1) * DV]
            qf = qs.astype(F32)
            kf = kk.astype(F32)
            cst = cs_ref[hd]
            nst = ns_ref[hd]
            cb = cst.astype(MXU)
            dcn = dc_scr[hd]
            dnn = dn_scr[hd]
            dcb = dcn.astype(MXU)

            den = _lane_get(stats, hd)
            floor = jnp.exp(-m_col)
            nrm = jnp.maximum(jnp.abs(den), floor)
            dhv = dh_ref[:, hd * DV:(hd + 1) * DV].astype(F32)
            rn = 1.0 / nrm
            dnum = dhv * rn
            dnum_b = dnum.astype(MXU)
            dhh = jnp.sum(dhv * h_ref[:, hd * DV:(hd + 1) * DV].astype(F32), axis=1, keepdims=True)
            dden = jnp.where(jnp.abs(den) > floor, -dhh * rn * jnp.sign(den), 0.0)

            sc = _dot_nt(qs, kk) * dl
            dsc = _dot_nt(dnum_b, vv) + dden
            da = (dl * dsc).astype(MXU)
            gmat = sc * dsc

            dq_state = _dot_nt(dnum_b, cb) + dden * nst
            dq = _dot(da, kk) + inter * dq_state
            dk_state = w_col * (_dot_nt(vv, dcb) + dnn)
            dk = _dot_tn(da, qs) + dk_state
            kw = (kf * w_col).astype(MXU)
            dv = _dot_tn(sc.astype(MXU), dnum_b) + _dot(kw, dcb)
            dqkv_ref[:, hd * DK:(hd + 1) * DK] = (dq * (DK ** -0.5)).astype(dqkv_ref.dtype)
            dqkv_ref[:, QK + hd * DK:QK + (hd + 1) * DK] = dk.astype(dqkv_ref.dtype)
            dqkv_ref[:, 2 * QK + hd * DV:2 * QK + (hd + 1) * DV] = dv.astype(dqkv_ref.dtype)

            e_col = inter * jnp.sum(qf * dq_state, axis=1, keepdims=True)
            h_col = jnp.sum(kf * dk_state, axis=1, keepdims=True)
            f_dec = decay * (jnp.sum(jnp.sum(cst * dcn, axis=1, keepdims=True), axis=0, keepdims=True)
                             + jnp.sum(nst * dnn, axis=1, keepdims=True))
            row_g = jnp.sum(gmat, axis=1, keepdims=True)
            col_g = jnp.sum(gmat, axis=0, keepdims=True)
            colsum_t = colsum_t + jnp.where(sub8 == hd, col_g, 0.0) - jnp.where(sub8 == NH + hd, col_g, 0.0)
            db_col = row_g + e_col - h_col
            db_col = db_col + jnp.where(last, jnp.sum(h_col, axis=0, keepdims=True) + f_dec, 0.0)
            dli_all = dli_all + _lane_put(h_col, hd)
            db_all = db_all + _lane_put(db_col, NH + hd)

            dc_scr[hd] = decay * dcn + _dot_tn((qf * inter).astype(MXU), dnum_b)
            dn_scr[hd] = decay * dnn + jnp.sum(qf * (inter * dden), axis=0, keepdims=True)
        colsum = jnp.concatenate([colsum_t, jnp.zeros((GATE_W - 8, LCH), F32)], axis=0).T
        lane = lax.broadcasted_iota(jnp.int32, (LCH, GATE_W), 1)
        dli_all = dli_all + jnp.where(lane < NH, colsum, 0.0)
        db_all = db_all + jnp.where(lane >= NH, colsum, 0.0)
        dlf_all = jnp.dot(jnp.where(c >= r, 1.0, 0.0), db_all, precision=lax.Precision.HIGHEST,
                          preferred_element_type=F32)
        dg = dli_all + dlf_all * _sigmoid(-g)
        dg_ref[...] = dg
        gsum_ref[0:1, 0:GATE_W] += jnp.sum(dg, axis=0, keepdims=True)

    rev = lambda c: nc - 1 - c
    q_spec, k_spec, v_spec = _qkv_specs(rev)
    hv_spec = pl.BlockSpec((LCH, VD), lambda c: (rev(c), 0))
    gate_spec = pl.BlockSpec((LCH, GATE_W), lambda c: (rev(c), 0))
    return pl.pallas_call(
        body, name="mlstm_bwd", grid=(nc,),
        in_specs=[q_spec, k_spec, v_spec, gate_spec,
                  pl.BlockSpec((1, GATE_W), lambda c: (0, 0)),
                  hv_spec, hv_spec, gate_spec] + _state_specs(rev) + [ANY],
        out_specs=[pl.BlockSpec((LCH, QKV_W), lambda c: (rev(c), C_Q // QKV_W)),
                   gate_spec,
                   pl.BlockSpec((8, D), lambda c: (0, 0))],
        out_shape=[jax.ShapeDtypeStruct(dproj.shape, dproj.dtype),
                   jax.ShapeDtypeStruct((n, GATE_W), F32),
                   jax.ShapeDtypeStruct((8, D), F32)],
        scratch_shapes=[pltpu.VMEM((NH, DK, DV), F32), pltpu.VMEM((NH, 1, DK), F32)],
        input_output_aliases={11: 0},
        compiler_params=_cparams("arbitrary"),
    )(proj, proj, proj, gates, bias, h, dh, stats, cs, ns, ms, dproj)


def _proj_spec(tm, col, width):
    return pl.BlockSpec((tm, width), lambda i: (i, col // width))


def _tail_in_specs(tm):
    return [_proj_spec(tm, 0, TAIL_W), _proj_spec(tm, C_XA, CONV_W),
            pl.BlockSpec((8, CONV_W), lambda i: (jnp.maximum(i * (tm // 8) - 1, 0), C_XA // CONV_W))]


def _tail_views(tail_r, conv_r, halo_r):
    cols = lambda ref, c0, w: ref.at[:, pl.ds(c0, w)]
    return (cols(tail_r, C_BA, D), cols(tail_r, C_ZA, D), cols(tail_r, C_O, VD), cols(tail_r, C_ZB, VD),
            cols(tail_r, C_GA, D), cols(tail_r, C_GB, D), cols(conv_r, 0, D), cols(conv_r, D, D),
            cols(halo_r, 0, D), cols(halo_r, D, D))


def _const(shape):
    return pl.BlockSpec(shape, lambda i: (0,) * len(shape))


def _conv_inputs(i, tm, xa_ref, ca_ref, xah_ref, cah_ref):
    u = _f32(xa_ref) * _f32(ca_ref)
    uh = jnp.where(i > 0, _f32(xah_ref) * _f32(cah_ref), 0.0)
    rid = lax.broadcasted_iota(jnp.int32, u.shape, 0)
    u1 = jnp.where(rid == 0, uh[7:8, :], pltpu.roll(u, 1, 0))
    u2 = jnp.where(rid == 0, uh[6:7, :], jnp.where(rid == 1, uh[7:8, :], pltpu.roll(u, 2, 0)))
    return u, u1, u2


def _head_norm(hh, gh):
    out = []
    for j in range(NH):
        hj = hh[:, j * DV:(j + 1) * DV]
        rj = _rstd(hj)
        out.append((hj * rj, rj, gh[:, j * DV:(j + 1) * DV]))
    return out


def _tail_fwd(proj, h, x, p, t, cw, cb, gh, gple, gfin, wa, wb, wo, wpg, wp):
    n = x.shape[0]
    tm = min(256, n)

    def body(tail_r, conv_r, halo_r, h_r, x_r, p_r, t_r,
             cw_r, cb_r, gh_r, gple_r, gfin_r, wa_r, wb_r, wo_r, wpg_r, wp_r,
             apre_o, bpre_o, mg_o, xn1_o, de_o, dgp_o, ya_o, yb_o, x1_o, dx2_o, acc_o):
        ba_r, za_r, o_r, zb_r, ga_r, gb_r, xa_r, ca_r, xah_r, cah_r = _tail_views(tail_r, conv_r, halo_r)
        i = pl.program_id(0)

        @pl.when(i == 0)
        def _():
            acc_o[...] = jnp.zeros_like(acc_o)

        u, u1, u2 = _conv_inputs(i, tm, xa_r, ca_r, xah_r, cah_r)
        cwv = cw_r[...]
        cv = cwv[0:1, :] * u2 + cwv[1:2, :] * u1 + cwv[2:3, :] * u + cb_r[...]
        za = za_r[...]
        a_pre = ba_r[...] * cv.astype(MXU) * (za * _sigmoid(za))
        apre_o[...] = a_pre
        ya = _dot(a_pre, wa_r[...]).astype(MXU)

        hn = _head_norm(_f32(h_r), gh_r[...])
        hbn = jnp.concatenate([xh * g for xh, _, g in hn], axis=1)
        zb = zb_r[...]
        b_pre = _sigmoid(o_r[...]) * hbn.astype(MXU) * (zb * _sigmoid(zb))
        bpre_o[...] = b_pre
        yb = _dot(b_pre, wb_r[...]).astype(MXU)
        ya_o[...] = ya
        yb_o[...] = yb

        mg = _sigmoid(ga_r[...]) * ya + _sigmoid(gb_r[...]) * yb
        mg_o[...] = mg
        x1 = x_r[...] + _dot(mg, wo_r[...])
        x1_o[...] = x1.astype(MXU)
        xn1 = (x1 * _rstd(x1) * gple_r[...]).astype(MXU)
        xn1_o[...] = xn1
        gt = _sigmoid(_dot(xn1, wpg_r[...]))
        e = _dot(p_r[...].astype(MXU), wp_r[...])
        x2 = x1 + gt * e
        r2 = _rstd(x2)
        xh2 = x2 * r2
        gf = gfin_r[...]
        diff = xh2 * gf - t_r[...]
        dy = diff * (1.0 / D)
        dx2 = _norm_bwd(dy, xh2, r2, gf)
        dx2_o[...] = dx2
        de_o[...] = (dx2 * gt).astype(MXU)
        dgp_o[...] = (dx2 * e * gt * (1.0 - gt)).astype(MXU)
        acc_o[0:1, :] += jnp.sum(dy * xh2, axis=0, keepdims=True)
        loss = 0.5 * jnp.sum(jnp.sum(diff * diff, axis=1, keepdims=True), axis=0, keepdims=True) * (1.0 / D)
        acc_o[1:2, :] += jnp.broadcast_to(loss, (1, D))

    row = lambda w, dt: (pl.BlockSpec((tm, w), lambda i: (i, 0)), jax.ShapeDtypeStruct((n, w), dt))
    outs = [row(D, MXU), row(VD, MXU), row(D, MXU), row(D, MXU), row(D, MXU), row(D, MXU),
            row(D, MXU), row(D, MXU), row(D, MXU), row(D, F32),
            (_const((8, D)), jax.ShapeDtypeStruct((8, D), F32))]
    return pl.pallas_call(
        body, name="tail_fwd", grid=(n // tm,),
        in_specs=_tail_in_specs(tm) + [
                  pl.BlockSpec((tm, VD), lambda i: (i, 0)),
                  pl.BlockSpec((tm, D), lambda i: (i, 0)),
                  pl.BlockSpec((tm, PLE), lambda i: (i, 0)),
                  pl.BlockSpec((tm, D), lambda i: (i, 0)),
                  _const((8, D)), _const((1, D)), _const((1, VD)), _const((1, D)), _const((1, D)),
                  _const((D, D)), _const((VD, D)), _const((D, D)), _const((D, D)), _const((PLE, D))],
        out_specs=[s for s, _ in outs],
        out_shape=[s for _, s in outs],
        compiler_params=_cparams("arbitrary"),
    )(proj, proj, proj, h, x, p, t, cw, cb, gh, gple, gfin, wa, wb, wo, wpg, wp)


def _tail_bwd(proj, h, dgp, dx2, x1, ya, yb, cw, cb, gh, gple, wpg, wo, wb, wa):
    n = x1.shape[0]
    tm = min(256, n)

    def body(tail_r, conv_r, halo_r, h_r,
             dgp_r, dx2_r, x1_r, ya_r, yb_r, cw_r, cb_r, gh_r, gple_r,
             wpg_r, wo_r, wb_r, wa_r,
             dproj_o, dcv_o, dh_o, dx1_o, dx1b_o, dya_o, dyb_o, acc_o):
        ba_r, za_r, o_r, zb_r, ga_r, gb_r, xa_r, ca_r, xah_r, cah_r = _tail_views(tail_r, conv_r, halo_r)
        i = pl.program_id(0)

        @pl.when(i == 0)
        def _():
            acc_o[...] = jnp.zeros_like(acc_o)

        dxn1 = _dot_nt(dgp_r[...], wpg_r[...])
        x1 = _f32(x1_r)
        r1 = _rstd(x1)
        xh1 = x1 * r1
        acc_o[0:1, 0:D] += jnp.sum(dxn1 * xh1, axis=0, keepdims=True)
        dx1 = dx2_r[...] + _norm_bwd(dxn1, xh1, r1, gple_r[...])
        dx1_o[...] = dx1
        dx1b = dx1.astype(MXU)
        dx1b_o[...] = dx1b

        dmg = _dot_nt(dx1b, wo_r[...]).astype(MXU)
        sga, dsga = _gate(ga_r[...])
        sgb, dsgb = _gate(gb_r[...])
        dya = dmg * sga
        dyb = dmg * sgb
        dya_o[...] = dya
        dyb_o[...] = dyb
        dproj_o[:, C_GA:C_GA + D] = dmg * ya_r[...] * dsga
        dproj_o[:, C_GB:C_GB + D] = dmg * yb_r[...] * dsgb

        db_pre = _dot_nt(dyb, wb_r[...]).astype(MXU)
        hn = _head_norm(_f32(h_r), gh_r[...])
        hbn = jnp.concatenate([xh * g for xh, _, g in hn], axis=1).astype(MXU)
        so, dso = _gate(o_r[...])
        zb = zb_r[...]
        szb, dszb = _gate(zb)
        sb = zb * szb
        t1 = db_pre * hbn
        dproj_o[:, C_O:C_O + VD] = t1 * (sb * dso)
        dproj_o[:, C_ZB:C_ZB + VD] = t1 * (so * (szb + zb * dszb))
        dhbn = (db_pre * (so * sb)).astype(F32)
        for j, (xh, rj, g) in enumerate(hn):
            dj = dhbn[:, j * DV:(j + 1) * DV]
            acc_o[1:2, j * DV:(j + 1) * DV] += jnp.sum(dj * xh, axis=0, keepdims=True)
            dh_o[:, j * DV:(j + 1) * DV] = _norm_bwd(dj, xh, rj, g).astype(MXU)

        da_pre = _dot_nt(dya, wa_r[...]).astype(MXU)
        u, u1, u2 = _conv_inputs(i, tm, xa_r, ca_r, xah_r, cah_r)
        cwv = cw_r[...]
        cv = cwv[0:1, :] * u2 + cwv[1:2, :] * u1 + cwv[2:3, :] * u + cb_r[...]
        za = za_r[...]
        sza, dsza = _gate(za)
        sa = za * sza
        ba = ba_r[...]
        t2 = da_pre * cv.astype(MXU)
        dproj_o[:, C_BA:C_BA + D] = t2 * sa
        dproj_o[:, C_ZA:C_ZA + D] = t2 * (ba * (sza + za * dsza))
        dcv_b = da_pre * (ba * sa)
        dcv_o[...] = dcv_b
        dcv = dcv_b.astype(F32)
        acc_o[2:3, 0:D] += jnp.sum(dcv, axis=0, keepdims=True)
        acc_o[3:4, 0:D] += jnp.sum(dcv * u2, axis=0, keepdims=True)
        acc_o[4:5, 0:D] += jnp.sum(dcv * u1, axis=0, keepdims=True)
        acc_o[5:6, 0:D] += jnp.sum(dcv * u, axis=0, keepdims=True)

    row = lambda w, dt: (pl.BlockSpec((tm, w), lambda i: (i, 0)), jax.ShapeDtypeStruct((n, w), dt))
    outs = [(pl.BlockSpec((tm, TAIL_W), lambda i: (i, 0)), jax.ShapeDtypeStruct((n, NMAIN), MXU)),
            row(D, MXU), row(VD, MXU), row(D, F32), row(D, MXU), row(D, MXU), row(D, MXU),
            (_const((8, VD)), jax.ShapeDtypeStruct((8, VD), F32))]
    rowin = lambda w: pl.BlockSpec((tm, w), lambda i: (i, 0))
    return pl.pallas_call(
        body, name="tail_bwd", grid=(n // tm,),
        in_specs=_tail_in_specs(tm) + [
                  rowin(VD), rowin(D), rowin(D), rowin(D), rowin(D), rowin(D),
                  _const((8, D)), _const((1, D)), _const((1, VD)), _const((1, D)),
                  _const((D, D)), _const((D, D)), _const((VD, D)), _const((D, D))],
        out_specs=[s for s, _ in outs],
        out_shape=[s for _, s in outs],
        compiler_params=_cparams("arbitrary"),
    )(proj, proj, proj, h, dgp, dx2, x1, ya, yb, cw, cb, gh, gple, wpg, wo, wb, wa)


def _conv_bwd(proj, dcv, cw, dproj):
    n = dcv.shape[0]
    tm = min(512, n)
    nt = n // tm

    def body(xa_r, ca_r, dcv_r, nxt_r, cw_r, dproj_in, dxc_o):
        del dproj_in
        i = pl.program_id(0)
        dcv_v = _f32(dcv_r)
        nxt = jnp.where(i < nt - 1, _f32(nxt_r), 0.0)
        rid = lax.broadcasted_iota(jnp.int32, dcv_v.shape, 0)
        d1 = jnp.where(rid == tm - 1, nxt[0:1, :], pltpu.roll(dcv_v, tm - 1, 0))
        d2 = jnp.where(rid == tm - 2, nxt[0:1, :],
                       jnp.where(rid == tm - 1, nxt[1:2, :], pltpu.roll(dcv_v, tm - 2, 0)))
        cwv = cw_r[...]
        du = cwv[2:3, :] * dcv_v + cwv[1:2, :] * d1 + cwv[0:1, :] * d2
        dxc_o[:, 0:D] = (du * _f32(ca_r)).astype(MXU)
        dxc_o[:, D:2 * D] = (du * _f32(xa_r)).astype(MXU)

    return pl.pallas_call(
        body, name="conv_bwd", grid=(nt,),
        in_specs=[_proj_spec(tm, C_XA, 1024), _proj_spec(tm, C_CA, 1024),
                  pl.BlockSpec((tm, D), lambda i: (i, 0)),
                  pl.BlockSpec((8, D), lambda i: (jnp.minimum((i + 1) * (tm // 8), n // 8 - 1), 0)),
                  _const((8, D)), ANY],
        out_specs=pl.BlockSpec((tm, CONV_W), lambda i: (i, C_XA // CONV_W)),
        out_shape=jax.ShapeDtypeStruct(dproj.shape, dproj.dtype),
        input_output_aliases={5: 0},
        compiler_params=_cparams("arbitrary"),
    )(proj, proj, dcv, dcv, cw, dproj)


def _position():
    return lax.axis_index("x"), lax.axis_index("y"), lax.axis_index("c")


def _flip(v, bit):
    return 1 - v if bit else v


def _part_start(core, stride, tiled):
    return pl.multiple_of(core * stride, stride) if tiled else core * stride


def _scatter_copies(srcs, dsts, strides, send_sems, recv_sems, local_sems):
    x, y, c = _position()
    me = 4 * x + 2 * y + c
    na = len(srcs)
    copies = []
    for r in range(N_DEV):
        px, py, pc = _flip(x, (r >> 2) & 1), _flip(y, (r >> 1) & 1), _flip(c, r & 1)
        for a in range(na):
            rows = dsts[a].shape[1]
            src = srcs[a].at[2 * px + py, pl.ds(_part_start(pc, strides[a], len(dsts[a].shape) == 3), rows)]
            dst = dsts[a].at[me]
            if r == 0:
                copies.append(pltpu.make_async_copy(src, dst, local_sems.at[a]))
            else:
                k = (r - 1) * na + a
                copies.append(pltpu.make_async_remote_copy(
                    src_ref=src, dst_ref=dst, send_sem=send_sems.at[k], recv_sem=recv_sems.at[k],
                    device_id=(px, py, pc), device_id_type=MESH))
    return copies


def _input_grad(dproj, dgates, w_t, wg_t, x, dx1, g_mix, g_in, g_rest):
    n = x.shape[0]
    tm, tk = min(1024, n), 2048
    nk = NMAIN // tk
    nt = n // tm

    def body(dp_r, dg_r, w_r, wg_r, x_r, dx1_r, g_r, gin, grest,
             gx_o, acc_o, oin, orest, acc, send_sems, recv_sems, local_sems):
        i = pl.program_id(0)
        kk = pl.program_id(1)
        copies = _scatter_copies((gin, grest), (oin, orest), strides, send_sems, recv_sems, local_sems)

        @pl.when((i == 0) & (kk == 0))
        def _():
            acc_o[...] = jnp.zeros_like(acc_o)
            for cp in copies:
                cp.start()

        @pl.when(kk == 0)
        def _():
            acc[...] = _dot(dg_r[...].astype(MXU), wg_r[...])

        acc[...] += _dot(dp_r[...], w_r[...])

        @pl.when(kk == nk - 1)
        def _():
            dhn = acc[...]
            xv = x_r[...]
            r0 = _rstd(xv)
            xh = xv * r0
            acc_o[0:1, :] += jnp.sum(dhn * xh, axis=0, keepdims=True)
            gx_o[...] = dx1_r[...] + _norm_bwd(dhn, xh, r0, g_r[...])

        @pl.when((i == nt - 1) & (kk == nk - 1))
        def _():
            for cp in copies:
                cp.wait()

    nrem = 2 * (N_DEV - 1)
    r_in, r_rest = HALF_SLABS, g_rest.shape[1] // 2
    strides = (HALF_START, r_rest)
    return pl.pallas_call(
        body, name="input_grad", grid=(nt, nk),
        in_specs=[pl.BlockSpec((tm, tk), lambda i, kk: (i, kk)),
                  pl.BlockSpec((tm, GATE_W), lambda i, kk: (i, 0)),
                  pl.BlockSpec((tk, D), lambda i, kk: (kk, 0)),
                  pl.BlockSpec((GATE_W, D), lambda i, kk: (0, 0)),
                  pl.BlockSpec((tm, D), lambda i, kk: (i, 0)),
                  pl.BlockSpec((tm, D), lambda i, kk: (i, 0)),
                  pl.BlockSpec((1, D), lambda i, kk: (0, 0)),
                  ANY, ANY],
        out_specs=[pl.BlockSpec((tm, D), lambda i, kk: (i, 0)),
                   pl.BlockSpec((8, D), lambda i, kk: (0, 0)),
                   ANY, ANY],
        out_shape=[jax.ShapeDtypeStruct((n, D), F32), jax.ShapeDtypeStruct((8, D), F32),
                   jax.ShapeDtypeStruct((N_DEV, r_in) + g_in.shape[2:], g_in.dtype),
                   jax.ShapeDtypeStruct((N_DEV, r_rest) + g_rest.shape[2:], g_rest.dtype)],
        scratch_shapes=[pltpu.VMEM((tm, D), F32),
                        pltpu.SemaphoreType.DMA((nrem,)), pltpu.SemaphoreType.DMA((nrem,)),
                        pltpu.SemaphoreType.DMA((2,))],
        compiler_params=_cparams("arbitrary", "arbitrary"),
    )(dproj, dgates, w_t, wg_t, x, dx1, g_mix, g_in, g_rest)


def _pack_small(acc_f, acc_b, acc_x, gsum):
    def body(f_r, b_r, x_r, s_r, o_r):
        o_r[...] = jnp.zeros_like(o_r)
        o_r[0:1, :] = x_r[0:1, :]
        o_r[1:2, :] = b_r[2:3, 0:D]
        o_r[2:3, :] = b_r[1:2, 0:D]
        o_r[3:4, :] = b_r[1:2, D:2 * D]
        o_r[4:5, :] = b_r[0:1, 0:D]
        o_r[5:6, :] = f_r[0:1, :]
        o_r[6:9, :] = b_r[3:6, 0:D]
        lane = lax.broadcasted_iota(jnp.int32, (1, D), 1)
        o_r[9:10, :] = jnp.where(lane < 2 * NH, s_r[0:1, :], jnp.where(lane == 2 * NH, f_r[1:2, :], 0.0))

    return pl.pallas_call(
        body, name="pack_small",
        out_shape=jax.ShapeDtypeStruct((SMALL_ROWS, D), F32),
    )(acc_f, acc_b, acc_x, gsum)


def _sum_slots(r, name, tr=64, half=None):
    s, rows, w = r.shape
    tr = min(tr, rows)
    assert rows % tr == 0
    nt = rows // tr

    def body(*refs):
        r_ref, o_ref = refs[-2:]
        tot = r_ref[0].astype(F32)
        for k in range(1, s):
            tot = tot + r_ref[k].astype(F32)
        o_ref[...] = tot

    if half is None:
        return pl.pallas_call(
            body, name=name, grid=(nt,),
            in_specs=[pl.BlockSpec((s, tr, w), lambda i: (0, i, 0))],
            out_specs=pl.BlockSpec((tr, w), lambda i: (i, 0)),
            out_shape=jax.ShapeDtypeStruct((rows, w), F32),
            compiler_params=_cparams("arbitrary"),
        )(r)
    return pl.pallas_call(
        body, name=name,
        grid_spec=pltpu.PrefetchScalarGridSpec(
            num_scalar_prefetch=1, grid=(nt,),
            in_specs=[pl.BlockSpec((s, tr, w), lambda i, hf: (0, i, 0))],
            out_specs=pl.BlockSpec((tr, w), lambda i, hf: (hf[0] * nt + i, 0))),
        out_shape=jax.ShapeDtypeStruct((2 * rows, w), F32),
        compiler_params=_cparams("arbitrary"),
    )(half, r)


def _cast_slabs(a, tr=163):
    rows = a.shape[0]
    assert rows % tr == 0

    def body(a_ref, o_ref):
        o_ref[...] = a_ref[...].astype(MXU)

    spec = pl.BlockSpec((tr,) + a.shape[1:], lambda i: (i, 0, 0))
    return pl.pallas_call(
        body, name="cast_w_in", grid=(rows // tr,), in_specs=[spec], out_specs=spec,
        out_shape=jax.ShapeDtypeStruct(a.shape, MXU), compiler_params=_cparams("arbitrary"),
    )(a)


def _sum_slabs(r, name, tr=69):
    s, rows = r.shape[:2]
    assert rows % tr == 0

    def body(r_ref, o_ref):
        tot = r_ref[0].astype(F32)
        for k in range(1, s):
            tot = tot + r_ref[k].astype(F32)
        o_ref[...] = tot

    return pl.pallas_call(
        body, name=name, grid=(rows // tr,),
        in_specs=[pl.BlockSpec((s, tr) + r.shape[2:], lambda i: (0, i, 0, 0))],
        out_specs=pl.BlockSpec((tr,) + r.shape[2:], lambda i: (i, 0, 0)),
        out_shape=jax.ShapeDtypeStruct(r.shape[1:], F32),
        compiler_params=_cparams("arbitrary"),
    )(r)


def _adamw_body(w_r, g_r, m_r, v_r, d_o, m_o, v_o):
    c1 = 1.0 - ADAM_B1 ** ADAM_STEP
    c2 = 1.0 - ADAM_B2 ** ADAM_STEP
    gv = g_r[...]
    mn = ADAM_B1 * m_r[...] + (1.0 - ADAM_B1) * gv
    vn = ADAM_B2 * v_r[...] + (1.0 - ADAM_B2) * (gv * gv)
    m_o[...] = mn
    v_o[...] = vn
    d_o[...] = -ADAM_LR * ((mn / c1) / (jnp.sqrt(vn / c2) + ADAM_EPS) + ADAM_WD * w_r[...])


def _adamw_slabs(w, g, m, v, name, tr):
    rows = w.shape[0]
    assert rows % tr == 0 and w.shape == g.shape

    def body(*refs):
        _adamw_body(*refs)

    spec = pl.BlockSpec((tr,) + w.shape[1:], lambda i: (i, 0, 0))
    shp = jax.ShapeDtypeStruct(w.shape, F32)
    return pl.pallas_call(
        body, name=name, grid=(rows // tr,),
        in_specs=[spec] * 4, out_specs=[spec] * 3, out_shape=[shp] * 3,
        compiler_params=_cparams("arbitrary"),
    )(w, g, m, v)


def _adamw(w, g, m, v, name, g_row0=0, tr=64):
    lead = w.ndim == 3
    rows, cols = w.shape[-2:]
    tr = min(tr, rows)
    assert rows % tr == 0 and g_row0 % tr == 0 and g.shape[1] == cols

    def body(*refs):
        _adamw_body(*refs)

    if lead:
        spec = pl.BlockSpec((None, tr, cols), lambda i: (0, i, 0))
    else:
        spec = pl.BlockSpec((tr, cols), lambda i: (i, 0))
    g_spec = pl.BlockSpec((tr, cols), lambda i: (g_row0 // tr + i, 0))
    shp = jax.ShapeDtypeStruct(w.shape, F32)
    return pl.pallas_call(
        body, name=name, grid=(rows // tr,),
        in_specs=[spec, g_spec, spec, spec], out_specs=[spec] * 3, out_shape=[shp] * 3,
        compiler_params=_cparams("arbitrary"),
    )(w, g, m, v)


def _allgather_chips(halved, whole):
    arrs = [a for a, _ in halved] + list(whole)
    parts = [p for _, p in halved]
    nh, na = len(halved), len(arrs)
    relations = ((1, 0), (0, 1), (1, 1))
    n_ici = 3 * na

    def body(*refs):
        ins, outs = refs[:na], refs[na:2 * na]
        send_sems, recv_sems = refs[2 * na:]
        x, y, c = _position()
        mine = 2 * x + y

        def rows(a, core, stride, size):
            return pl.ds(_part_start(core, stride, len(ins[a].shape) == 2), size)

        def remote(src, dst, k, to):
            return pltpu.make_async_remote_copy(src_ref=src, dst_ref=dst, send_sem=send_sems.at[k],
                                                recv_sem=recv_sems.at[k], device_id=to, device_id_type=MESH)

        first, passed = [], []
        for j, (rx, ry) in enumerate(relations):
            px, py = _flip(x, rx), _flip(y, ry)
            for a in range(na):
                if a < nh:
                    s1, n1, s2, n2 = parts[a]
                    first.append(remote(ins[a].at[rows(a, c, s1, n1)], outs[a].at[mine, rows(a, c, s1, n1)],
                                        j * na + a, (px, py, c)))
                    landed = outs[a].at[2 * px + py, rows(a, c, s2, n2)]
                    passed.append(remote(landed, landed, n_ici + j * nh + a, (x, y, 1 - c)))
                else:
                    first.append(remote(ins[a], outs[a].at[mine], j * na + a, (px, py, c)))
        for cp in first:
            cp.start()
        k = 0
        for j in range(3):
            for a in range(na):
                first[j * na + a].wait_recv()
                if a < nh:
                    passed[k].start()
                    k += 1
        for cp in passed:
            cp.wait_recv()
        for cp in first + passed:
            cp.wait_send()

    nsem = n_ici + 3 * nh
    outs = pl.pallas_call(
        body, name="allgather_weights",
        in_specs=[ANY] * na, out_specs=[ANY] * na,
        out_shape=[jax.ShapeDtypeStruct((N_CHIPS,) + a.shape, a.dtype) for a in arrs],
        scratch_shapes=[pltpu.SemaphoreType.DMA((nsem,)), pltpu.SemaphoreType.DMA((nsem,))],
    )(*arrs)
    chip = 2 * lax.axis_index("x") + lax.axis_index("y")
    return [lax.dynamic_update_slice(o, a[None], (chip,) + (0,) * a.ndim) for o, a in zip(outs, arrs)]


def _join_halves(piece, full_rest, small):
    def body(pc_ref, rest_in, sm, oin, orest, osm, send_sems, recv_sems, local_sem):
        del rest_in
        x, y, c = _position()
        me = 4 * x + 2 * y + c
        sib = (x, y, 1 - c)
        rows = orest.shape[0] // 2
        mine = orest.at[pl.ds(pl.multiple_of(c * rows, rows), rows)]
        copies = [
            pltpu.make_async_remote_copy(
                src_ref=pc_ref.at[pl.ds(c, HALF_START)], dst_ref=oin.at[pl.ds(c * HALF_SLABS, HALF_START)],
                send_sem=send_sems.at[0], recv_sem=recv_sems.at[0], device_id=sib, device_id_type=MESH),
            pltpu.make_async_remote_copy(
                src_ref=mine, dst_ref=mine, send_sem=send_sems.at[1], recv_sem=recv_sems.at[1],
                device_id=sib, device_id_type=MESH)]
        na = 2
        copies.append(pltpu.make_async_copy(sm, osm.at[me], local_sem))
        for r in range(1, N_DEV):
            to = (_flip(x, (r >> 2) & 1), _flip(y, (r >> 1) & 1), _flip(c, r & 1))
            copies.append(pltpu.make_async_remote_copy(
                src_ref=sm, dst_ref=osm.at[me], send_sem=send_sems.at[na + r - 1],
                recv_sem=recv_sems.at[na + r - 1], device_id=to, device_id_type=MESH))
        for cp in copies:
            cp.start()
        for cp in copies:
            cp.wait()

    nsem = 2 + N_DEV - 1
    return pl.pallas_call(
        body, name="join_halves",
        in_specs=[ANY] * 3, out_specs=[ANY] * 3,
        out_shape=[jax.ShapeDtypeStruct((SLABS_CHIP,) + piece.shape[1:], piece.dtype),
                   jax.ShapeDtypeStruct(full_rest.shape, full_rest.dtype),
                   jax.ShapeDtypeStruct((N_DEV,) + small.shape, small.dtype)],
        scratch_shapes=[pltpu.SemaphoreType.DMA((nsem,)), pltpu.SemaphoreType.DMA((nsem,)),
                        pltpu.SemaphoreType.DMA],
        input_output_aliases={1: 1},
    )(piece, full_rest, small)


def _to_internal(w_slabs):
    order = sorted(SEGMENTS, key=lambda s: s[2])
    main = jnp.concatenate([w_slabs[g0 // 2:(g0 + w) // 2] for g0, w, _ in order], axis=0)
    gate = w_slabs[GATE_COL // 2:GATE_COL // 2 + NH].reshape(2 * NH, D)
    return main.reshape(NMAIN, D), jnp.pad(gate, ((0, GATE_W - 2 * NH), (0, 0)))


def _to_global(main_t, gate_t):
    main = main_t.reshape(NMAIN // 2, 16, D // 8)
    parts = sorted([(g0, main[i0 // 2:(i0 + w) // 2]) for g0, w, i0 in SEGMENTS]
                   + [(GATE_COL, gate_t[0:2 * NH].reshape(NH, 16, D // 8))], key=lambda s: s[0])
    return jnp.concatenate([p for _, p in parts], axis=0)


def _pad_rows(a, rows=8):
    return jnp.pad(a, ((0, rows - a.shape[0]), (0, 0)))


def kernel(x, p, g_mix, w_in, conv_w, conv_b, w_a_out, b_gates, g_head, w_b_out, w_o, g_ple, w_ple_gate, w_ple, g_final, loss_target, m_g_mix, m_w_in, m_conv_w, m_conv_b, m_w_a_out, m_b_gates, m_g_head, m_w_b_out, m_w_o, m_g_ple, m_w_ple_gate, m_w_ple, m_g_final, v_g_mix, v_w_in, v_conv_w, v_conv_b, v_w_a_out, v_b_gates, v_g_head, v_w_b_out, v_w_o, v_g_ple, v_w_ple_gate, v_w_ple, v_g_final):
    chip = 2 * lax.axis_index("x") + lax.axis_index("y")
    xs, ps, ts = x[0], p[0, 0], loss_target[0]
    g_fin = g_final.reshape(1, D)

    w_slabs = _cast_slabs(jnp.transpose(w_in, (2, 0, 1)).reshape(SLABS_CHIP, 16, D // 8))
    rest = [w[0].astype(MXU) for w in (w_a_out, w_b_out, w_o, w_ple_gate, w_ple)]
    even = lambda a: (a.shape[0] // 2,) * 4
    g_win, g_wa, g_wb, g_wo, g_wpg, g_wp, g_cw = _allgather_chips(
        [(w_slabs, (HALF_START, HALF_SLABS, HALF_SLABS, HALF_START))] + [(a, even(a)) for a in rest],
        [_pad_rows(conv_w[0])])
    w_t, wg_t = _to_internal(g_win.reshape(N_IN // 2, 16, D // 8))
    wa, wb, wo, wpg = [g.reshape(-1, D) for g in (g_wa, g_wb, g_wo, g_wpg)]
    wp = jnp.transpose(g_wp, (1, 0, 2)).reshape(PLE, D)
    cw = jnp.transpose(g_cw, (1, 0, 2)).reshape(8, D)

    bias = jnp.pad(b_gates, ((0, 0), (0, GATE_W - 2 * NH)))
    proj, hn, gates = _proj(xs, g_mix, w_t, wg_t)
    h, stats, cs, ns, ms = _mlstm_fwd(proj, gates, bias)
    (a_pre, b_pre, mg, xn1, de, dgp, ya, yb, x1, dx2, acc_f) = _tail_fwd(
        proj, h, xs, ps, ts, cw, conv_b, g_head, g_ple, g_fin, wa, wb, wo, wpg, wp)
    dproj, dcv, dh, dx1, dx1b, dya, dyb, acc_b = _tail_bwd(
        proj, h, dgp, dx2, x1, ya, yb, cw, conv_b, g_head, g_ple, wpg, wo, wb, wa)
    dproj = _conv_bwd(proj, dcv, cw, dproj)
    dproj, dgates, gsum = _mlstm_bwd(proj, gates, bias, h, dh, stats, cs, ns, ms, dproj)
    d_main = _matmul_tn(dproj, hn, "dw_in", out_dtype=WIRE, tk=4096)
    d_gate = _matmul_tn(dgates, hn, "dw_gate", out_dtype=WIRE)
    d_wa = _matmul_tn(a_pre, dya, "dw_a_out", out_dtype=WIRE)
    d_wb = _matmul_tn(b_pre, dyb, "dw_b_out", out_dtype=WIRE)
    d_wo = _matmul_tn(mg, dx1b, "dw_o", out_dtype=WIRE)
    d_wpg = _matmul_tn(xn1, dgp, "dw_ple_gate", out_dtype=WIRE)
    d_wp = _matmul_tn(ps, de, "dw_ple", out_dtype=WIRE)

    g_in = _to_global(d_main, d_gate).reshape(N_CHIPS, SLABS_CHIP, 16, D // 8)
    d_wp_c = jnp.transpose(d_wp.reshape(PLE, N_CHIPS, PLE), (1, 0, 2)).reshape(N_CHIPS, PACK_ROWS[4], D)
    g_rest = jnp.concatenate(
        [d_wa.reshape(N_CHIPS, -1, D), d_wb.reshape(N_CHIPS, -1, D), d_wo.reshape(N_CHIPS, -1, D),
         d_wpg.reshape(N_CHIPS, -1, D), d_wp_c], axis=1)
    grad_x, acc_x, r_in, r_rest = _input_grad(dproj, dgates, w_t, wg_t, xs, dx1, g_mix, g_in, g_rest)
    small = _pack_small(acc_f, acc_b, acc_x, gsum)
    core = lax.axis_index("c").astype(jnp.int32)
    piece = _sum_slabs(r_in, "sum_w_in")
    gw_in, gw_rest, r_small = _join_halves(
        piece, _sum_slots(r_rest, "sum_rest", tr=96, half=core.reshape(1)), small)
    gw_in = lax.dynamic_update_slice(gw_in, piece, (core * HALF_START, 0, 0)).reshape(COLS_CHIP, 8, D // 8)
    gs = _sum_slots(r_small, "sum_small", tr=SMALL_ROWS)

    big = []
    row0 = 0
    for name, w, m, v in (("w_a_out", w_a_out, m_w_a_out, v_w_a_out), ("w_b_out", w_b_out, m_w_b_out, v_w_b_out),
                          ("w_o", w_o, m_w_o, v_w_o), ("w_ple_gate", w_ple_gate, m_w_ple_gate, v_w_ple_gate)):
        big.append((name, w, m, v, gw_rest, row0))
        row0 += w.shape[1]
    g_wp = gw_rest[row0:row0 + PACK_ROWS[4]].reshape(PLE, PLE)
    big.append(("w_ple", w_ple, m_w_ple, v_w_ple, g_wp, 0))
    upd = {name: _adamw(w, g, m, v, "adamw_" + name, g_row0=r0) for name, w, m, v, g, r0 in big}
    g_big = {name: (g if name == "w_ple" else g[r0:r0 + w.shape[1]])[None] for name, w, m, v, g, r0 in big}
    slabs = lambda a: jnp.transpose(a, (2, 0, 1)).reshape(COLS_CHIP, 8, D // 8)
    unslab = lambda a: jnp.transpose(a, (1, 2, 0)).reshape(1, D, COLS_CHIP)
    upd["w_in"] = [unslab(u) for u in _adamw_slabs(
        slabs(w_in), gw_in, slabs(m_w_in), slabs(v_w_in), "adamw_w_in", tr=326)]
    g_big["w_in"] = unslab(gw_in)

    lane = lax.broadcasted_iota(jnp.int32, (1, D), 1)
    g_small = jnp.concatenate([gs[0:6], jnp.where(lane < 2 * NH, gs[9:10], 0.0), jnp.zeros((1, D), F32)], axis=0)

    def small_pack(gm, cb_, bg, gh, gp, gf):
        return jnp.concatenate([gm, cb_, gh.reshape(2, D), gp, gf.reshape(1, D),
                                jnp.pad(bg, ((0, 0), (0, D - 2 * NH))), jnp.zeros((1, D), F32)], axis=0)

    ws = small_pack(g_mix, conv_b, b_gates, g_head, g_ple, g_final)
    ms_ = small_pack(m_g_mix, m_conv_b, m_b_gates, m_g_head, m_g_ple, m_g_final)
    vs = small_pack(v_g_mix, v_conv_b, v_b_gates, v_g_head, v_g_ple, v_g_final)
    upd_s = _adamw(ws, g_small, ms_, vs, "adamw_small")
    g_cw_mine = _pad_rows(lax.dynamic_slice(gs[6:9], (0, chip * PLE), (3, PLE)))
    upd_c = _adamw(_pad_rows(conv_w[0]), g_cw_mine, _pad_rows(m_conv_w[0]), _pad_rows(v_conv_w[0]),
                   "adamw_conv_w")

    def leaves(bigs, sm, cwv):
        return [sm[0:1], bigs["w_in"], cwv[0:3][None], sm[1:2], bigs["w_a_out"], sm[6:7, 0:2 * NH],
                sm[2:4].reshape(1, VD), bigs["w_b_out"], bigs["w_o"], sm[4:5], bigs["w_ple_gate"],
                bigs["w_ple"], sm[5]]

    loss = gs[9, 2 * NH]
    outs = [loss, grad_x[None]] + leaves(g_big, g_small, g_cw_mine)
    for k in range(3):
        outs += leaves({name: u[k] for name, u in upd.items()}, upd_s[k], upd_c[k])
    return tuple(outs)
```

```python
import jax
import jax.numpy as jnp
from jax import lax
from jax.experimental import pallas as pl
from jax.experimental.pallas import tpu as pltpu

F32 = jnp.float32
MXU = jnp.bfloat16
WIRE = jnp.bfloat16

D = 1024
NH, DK, DV = 4, 256, 512
VD = NH * DV
PLE = 256
LCH = 256
EPS = 1e-6
N_IN = 14344
NMAIN = 14336
GATE_W = 128
N_CHIPS, N_DEV = 4, 8

C_BA, C_ZA, C_O, C_ZB, C_GA, C_GB = 0, 1024, 2048, 4096, 6144, 7168
QK = NH * DK
C_Q, C_K, C_V, C_XA, C_CA = 8192, 9216, 10240, 12288, 13312
QKV_W = 2 * QK + VD
TAIL_W = 8192
CONV_W = 2048
SEGMENTS = (
    (0, 1024, C_XA), (1024, 1024, C_BA), (2048, 1024, C_CA), (3072, 1024, C_ZA),
    (4096, QKV_W, C_Q), (8192, 4096, C_O), (12296, 2048, C_GA),
)
GATE_COL = 12288

COLS_CHIP = N_IN // N_CHIPS
SLABS_CHIP = COLS_CHIP // 2
HALF_START = SLABS_CHIP // 2
HALF_SLABS = SLABS_CHIP - HALF_START

PACK_ROWS = (256, 512, 256, 256, 64)
SMALL_ROWS = 16

ADAM_LR, ADAM_B1, ADAM_B2, ADAM_EPS, ADAM_WD, ADAM_STEP = 0.001, 0.9, 0.999, 1e-08, 0.01, 10

VMEM_LIMIT = 56 * 1024 * 1024
MESH = pl.DeviceIdType.MESH
ANY = pl.BlockSpec(memory_space=pl.ANY)


def _cparams(*sem):
    return pltpu.CompilerParams(dimension_semantics=sem, vmem_limit_bytes=VMEM_LIMIT)


def _dot(a, b):
    return jnp.dot(a, b, preferred_element_type=F32)


def _dot_nt(a, b):
    return lax.dot_general(a, b, (((1,), (1,)), ((), ())), preferred_element_type=F32)


def _dot_tn(a, b):
    return lax.dot_general(a, b, (((0,), (0,)), ((), ())), preferred_element_type=F32)


def _sigmoid(x):
    return 1.0 / (1.0 + jnp.exp(-x))


def _logsig(x):
    return jnp.minimum(x, 0.0) - jnp.log(1.0 + jnp.exp(-jnp.abs(x)))


GATE_FLOOR = -80.0


def _gate(v):
    e = jnp.exp(-jnp.maximum(v, GATE_FLOOR))
    s = 1.0 / (1.0 + e)
    return s, e * s * s


def _rstd(x):
    return lax.rsqrt(jnp.mean(x * x, axis=-1, keepdims=True) + EPS)


def _norm_bwd(dy, xhat, r, g):
    dxh = dy * g
    return r * (dxh - xhat * jnp.mean(dxh * xhat, axis=-1, keepdims=True))


def _f32(ref):
    return ref[...].astype(F32)


def _proj(x, g_mix, w_t, wg_t, tm=1024, tn=3584):
    n = x.shape[0]
    m = w_t.shape[0]
    tm = min(tm, n)

    def body(x_ref, g_ref, b_ref, wg_ref, o_ref, hn_ref, gate_ref):
        @pl.when(pl.program_id(1) == 0)
        def _():
            xv = x_ref[...]
            hn = (xv * _rstd(xv) * g_ref[...]).astype(MXU)
            hn_ref[...] = hn
            gate_ref[...] = _dot_nt(hn, wg_ref[...])

        o_ref[...] = _dot_nt(hn_ref[...], b_ref[...]).astype(MXU)

    return pl.pallas_call(
        body, name="proj", grid=(n // tm, m // tn),
        in_specs=[pl.BlockSpec((tm, D), lambda i, j: (i, 0)),
                  pl.BlockSpec((1, D), lambda i, j: (0, 0)),
                  pl.BlockSpec((tn, D), lambda i, j: (j, 0)),
                  pl.BlockSpec((GATE_W, D), lambda i, j: (0, 0))],
        out_specs=[pl.BlockSpec((tm, tn), lambda i, j: (i, j)),
                   pl.BlockSpec((tm, D), lambda i, j: (i, 0)),
                   pl.BlockSpec((tm, GATE_W), lambda i, j: (i, 0))],
        out_shape=[jax.ShapeDtypeStruct((n, m), MXU), jax.ShapeDtypeStruct((n, D), MXU),
                   jax.ShapeDtypeStruct((n, GATE_W), F32)],
        compiler_params=_cparams("arbitrary", "arbitrary"),
    )(x, g_mix, w_t, wg_t)


def _matmul_tn(a, b, name, out_dtype=F32, ta=1024, tb=1024, tk=4096):
    n, ka = a.shape
    kb = b.shape[1]
    ta, tb, tk = min(ta, ka), min(tb, kb), min(tk, n)
    nk = n // tk

    def body(a_ref, b_ref, o_ref, acc):
        kk = pl.program_id(2)

        @pl.when(kk == 0)
        def _():
            acc[...] = jnp.zeros_like(acc)

        acc[...] += _dot_tn(a_ref[...].astype(MXU), b_ref[...].astype(MXU))

        @pl.when(kk == nk - 1)
        def _():
            o_ref[...] = acc[...].astype(out_dtype)

    return pl.pallas_call(
        body, name=name, grid=(ka // ta, kb // tb, nk),
        in_specs=[pl.BlockSpec((tk, ta), lambda i, j, kk: (kk, i)),
                  pl.BlockSpec((tk, tb), lambda i, j, kk: (kk, j))],
        out_specs=pl.BlockSpec((ta, tb), lambda i, j, kk: (i, j)),
        out_shape=jax.ShapeDtypeStruct((ka, kb), out_dtype),
        scratch_shapes=[pltpu.VMEM((ta, tb), F32)],
        compiler_params=_cparams("arbitrary", "arbitrary", "arbitrary"),
    )(a, b)


def _sub_row(block_t, j):
    sub = lax.broadcasted_iota(jnp.int32, block_t.shape, 0)
    return jnp.sum(jnp.where(sub == j, block_t, 0.0), axis=0, keepdims=True)


def _chunk_cumsum(g):
    n = g.shape[0]
    r = lax.broadcasted_iota(jnp.int32, (n, n), 0)
    c = lax.broadcasted_iota(jnp.int32, (n, n), 1)
    return jnp.dot(jnp.where(r >= c, 1.0, 0.0), _logsig(g), precision=lax.Precision.HIGHEST,
                   preferred_element_type=F32)


def _chunk_decay(li_col, li_row, b_col, b_row, m_prev):
    n = li_col.shape[0]
    r = lax.broadcasted_iota(jnp.int32, (n, n), 0)
    c = lax.broadcasted_iota(jnp.int32, (n, n), 1)
    tri = r >= c
    b_last = b_col[n - 1:n, :]
    rr = li_row - b_row
    a_col = b_col + m_prev
    g_col = b_last - b_col + li_col
    m_new = jnp.maximum(b_last + m_prev, jnp.max(g_col, axis=0, keepdims=True))
    w_col = jnp.exp(g_col - m_new)
    decay = jnp.exp(b_last + m_prev - m_new)
    return tri, rr, a_col, m_new, w_col, decay


def _qkv_specs(row_of):
    q_spec = pl.BlockSpec((LCH, QK), lambda c: (row_of(c), C_Q // QK))
    k_spec = pl.BlockSpec((LCH, QK), lambda c: (row_of(c), C_K // QK))
    v_spec = pl.BlockSpec((LCH, VD), lambda c: (row_of(c), C_V // VD))
    return q_spec, k_spec, v_spec


def _state_specs(row_of):
    return [pl.BlockSpec((NH, None, DK, DV), lambda c: (0, row_of(c), 0, 0)),
            pl.BlockSpec((NH, None, 1, DK), lambda c: (0, row_of(c), 0, 0)),
            pl.BlockSpec((NH, None, 1, GATE_W), lambda c: (0, row_of(c), 0, 0))]


def _lane_put(col, lane_id, width=GATE_W):
    lane = lax.broadcasted_iota(jnp.int32, (col.shape[0], width), 1)
    return jnp.where(lane == lane_id, col, 0.0)


def _lane_get(block, lane_id):
    lane = lax.broadcasted_iota(jnp.int32, block.shape, 1)
    return jnp.sum(jnp.where(lane == lane_id, block, 0.0), axis=1, keepdims=True)


def _mlstm_fwd(proj, gates, bias):
    n = proj.shape[0]
    nc = n // LCH

    def body(q_ref, k_ref, v_ref, g_ref, bias_ref,
             h_ref, st_ref, cs_ref, ns_ref, ms_ref, c_scr, n_scr, m_scr):
        @pl.when(pl.program_id(0) == 0)
        def _():
            c_scr[...] = jnp.zeros_like(c_scr)
            n_scr[...] = jnp.zeros_like(n_scr)
            m_scr[...] = jnp.full_like(m_scr, -jnp.inf)

        g = g_ref[...] + bias_ref[...]
        gt = g.T[0:8, :]
        b = _chunk_cumsum(g)
        bt = b.T[0:8, :]
        stats = jnp.zeros((LCH, GATE_W), F32)
        for hd in range(NH):
            m_all = m_scr[hd]
            m_prev = m_all[0:1, 0:1]
            b_col = _lane_get(b, NH + hd)
            tri, rr, a_col, m_new, w_col, decay = _chunk_decay(
                _lane_get(g, hd), _sub_row(gt, hd), b_col, _sub_row(bt, NH + hd), m_prev)
            dmat = jnp.where(tri, b_col + rr, -jnp.inf)
            m_col = jnp.maximum(a_col, jnp.max(dmat, axis=1, keepdims=True))
            dl = jnp.exp(dmat - m_col)
            inter = jnp.exp(a_col - m_col)

            qs = q_ref[:, hd * DK:(hd + 1) * DK] * (DK ** -0.5)
            kk = k_ref[:, hd * DK:(hd + 1) * DK]
            vv = v_ref[:, hd * DV:(hd + 1) * DV]
            cst = c_scr[hd]
            nst = n_scr[hd]
            cs_ref[hd] = cst
            ns_ref[hd] = nst
            ms_ref[hd] = m_all

            sc = _dot_nt(qs, kk) * dl
            num = _dot(sc.astype(MXU), vv) + inter * _dot(qs, cst.astype(MXU))
            den = (jnp.sum(sc, axis=1, keepdims=True)
                   + inter * jnp.sum(qs.astype(F32) * nst, axis=1, keepdims=True))
            nrm = jnp.maximum(jnp.abs(den), jnp.exp(-m_col))
            h_ref[:, hd * DV:(hd + 1) * DV] = (num * (1.0 / nrm)).astype(h_ref.dtype)
            stats = stats + _lane_put(den, hd) + _lane_put(m_col, NH + hd)

            kw = kk.astype(F32) * w_col
            c_scr[hd] = decay * cst + _dot_tn(kw.astype(MXU), vv)
            n_scr[hd] = decay * nst + jnp.sum(kw, axis=0, keepdims=True)
            m_scr[hd] = jnp.broadcast_to(m_new, (1, GATE_W))
        st_ref[...] = stats

    q_spec, k_spec, v_spec = _qkv_specs(lambda c: c)
    return pl.pallas_call(
        body, name="mlstm_fwd", grid=(nc,),
        in_specs=[q_spec, k_spec, v_spec,
                  pl.BlockSpec((LCH, GATE_W), lambda c: (c, 0)),
                  pl.BlockSpec((1, GATE_W), lambda c: (0, 0))],
        out_specs=[pl.BlockSpec((LCH, VD), lambda c: (c, 0)),
                   pl.BlockSpec((LCH, GATE_W), lambda c: (c, 0))] + _state_specs(lambda c: c),
        out_shape=[jax.ShapeDtypeStruct((n, VD), MXU),
                   jax.ShapeDtypeStruct((n, GATE_W), F32),
                   jax.ShapeDtypeStruct((NH, nc, DK, DV), F32),
                   jax.ShapeDtypeStruct((NH, nc, 1, DK), F32),
                   jax.ShapeDtypeStruct((NH, nc, 1, GATE_W), F32)],
        scratch_shapes=[pltpu.VMEM((NH, DK, DV), F32), pltpu.VMEM((NH, 1, DK), F32),
                        pltpu.VMEM((NH, 1, GATE_W), F32)],
        compiler_params=_cparams("arbitrary"),
    )(proj, proj, proj, gates, bias)


def _mlstm_bwd(proj, gates, bias, h, dh, stats, cs, ns, ms, dproj):
    n = proj.shape[0]
    nc = n // LCH

    def body(q_ref, k_ref, v_ref, g_ref, bias_ref, h_ref, dh_ref, st_ref,
             cs_ref, ns_ref, ms_ref, dproj_in,
             dqkv_ref, dg_ref, gsum_ref, dc_scr, dn_scr):
        del dproj_in

        @pl.when(pl.program_id(0) == 0)
        def _():
            dc_scr[...] = jnp.zeros_like(dc_scr)
            dn_scr[...] = jnp.zeros_like(dn_scr)
            gsum_ref[...] = jnp.zeros_like(gsum_ref)

        g = g_ref[...] + bias_ref[...]
        gt = g.T[0:8, :]
        b = _chunk_cumsum(g)
        bt = b.T[0:8, :]
        stats = st_ref[...]
        r = lax.broadcasted_iota(jnp.int32, (LCH, LCH), 0)
        c = lax.broadcasted_iota(jnp.int32, (LCH, LCH), 1)
        sub8 = lax.broadcasted_iota(jnp.int32, (8, LCH), 0)
        last = lax.broadcasted_iota(jnp.int32, (LCH, 1), 0) == LCH - 1
        dli_all = jnp.zeros((LCH, GATE_W), F32)
        db_all = jnp.zeros((LCH, GATE_W), F32)
        colsum_t = jnp.zeros((8, LCH), F32)
        for hd in range(NH):
            m_prev = ms_ref[hd][0:1, 0:1]
            b_col = _lane_get(b, NH + hd)
            tri, rr, a_col, m_new, w_col, decay = _chunk_decay(
                _lane_get(g, hd), _sub_row(gt, hd), b_col, _sub_row(bt, NH + hd), m_prev)
            m_col = _lane_get(stats, NH + hd)
            dl = jnp.where(tri, jnp.exp((b_col - m_col) + rr), 0.0)
            inter = jnp.exp(a_col - m_col)

            qs = q_ref[:, hd * DK:(hd + 1) * DK] * (DK ** -0.5)
            kk = k_ref[:, hd * DK:(hd + 1) * DK]
            vv = v_ref[:, hd * DV:(hd + 1) * DV]
            qf = qs.astype(F32)
            kf = kk.astype(F32)
            cst = cs_ref[hd]
            nst = ns_ref[hd]
            cb = cst.astype(MXU)
            dcn = dc_scr[hd]
            dnn = dn_scr[hd]
            dcb = dcn.astype(MXU)

            den = _lane_get(stats, hd)
            floor = jnp.exp(-m_col)
            nrm = jnp.maximum(jnp.abs(den), floor)
            dhv = dh_ref[:, hd * DV:(hd + 1) * DV].astype(F32)
            rn = 1.0 / nrm
            dnum = dhv * rn
            dnum_b = dnum.astype(MXU)
            dhh = jnp.sum(dhv * h_ref[:, hd * DV:(hd + 1) * DV].astype(F32), axis=1, keepdims=True)
            dden = jnp.where(jnp.abs(den) > floor, -dhh * rn * jnp.sign(den), 0.0)

            sc = _dot_nt(qs, kk) * dl
            dsc = _dot_nt(dnum_b, vv) + dden
            da = (dl * dsc).astype(MXU)
            gmat = sc * dsc

            dq_state = _dot_nt(dnum_b, cb) + dden * nst
            dq = _dot(da, kk) + inter * dq_state
            dk_state = w_col * (_dot_nt(vv, dcb) + dnn)
            dk = _dot_tn(da, qs) + dk_state
            kw = (kf * w_col).astype(MXU)
            dv = _dot_tn(sc.astype(MXU), dnum_b) + _dot(kw, dcb)
            dqkv_ref[:, hd * DK:(hd + 1) * DK] = (dq * (DK ** -0.5)).astype(dqkv_ref.dtype)
            dqkv_ref[:, QK + hd * DK:QK + (hd + 1) * DK] = dk.astype(dqkv_ref.dtype)
            dqkv_ref[:, 2 * QK + hd * DV:2 * QK + (hd + 1) * DV] = dv.astype(dqkv_ref.dtype)

            e_col = inter * jnp.sum(qf * dq_state, axis=1, keepdims=True)
            h_col = jnp.sum(kf * dk_state, axis=1, keepdims=True)
            f_dec = decay * (jnp.sum(jnp.sum(cst * dcn, axis=1, keepdims=True), axis=0, keepdims=True)
                             + jnp.sum(nst * dnn, axis=1, keepdims=True))
            row_g = jnp.sum(gmat, axis=1, keepdims=True)
            col_g = jnp.sum(gmat, axis=0, keepdims=True)
            colsum_t = colsum_t + jnp.where(sub8 == hd, col_g, 0.0) - jnp.where(sub8 == NH + hd, col_g, 0.0)
            db_col = row_g + e_col - h_col
            db_col = db_col + jnp.where(last, jnp.sum(h_col, axis=0, keepdims=True) + f_dec, 0.0)
            dli_all = dli_all + _lane_put(h_col, hd)
            db_all = db_all + _lane_put(db_col, NH + hd)

            dc_scr[hd] = decay * dcn + _dot_tn((qf * inter).astype(MXU), dnum_b)
            dn_scr[hd] = decay * dnn + jnp.sum(qf * (inter * dden), axis=0, keepdims=True)
        colsum = jnp.concatenate([colsum_t, jnp.zeros((GATE_W - 8, LCH), F32)], axis=0).T
        lane = lax.broadcasted_iota(jnp.int32, (LCH, GATE_W), 1)
        dli_all = dli_all + jnp.where(lane < NH, colsum, 0.0)
        db_all = db_all + jnp.where(lane >= NH, colsum, 0.0)
        dlf_all = jnp.dot(jnp.where(c >= r, 1.0, 0.0), db_all, precision=lax.Precision.HIGHEST,
                          preferred_element_type=F32)
        dg = dli_all + dlf_all * _sigmoid(-g)
        dg_ref[...] = dg
        gsum_ref[0:1, 0:GATE_W] += jnp.sum(dg, axis=0, keepdims=True)

    rev = lambda c: nc - 1 - c
    q_spec, k_spec, v_spec = _qkv_specs(rev)
    hv_spec = pl.BlockSpec((LCH, VD), lambda c: (rev(c), 0))
    gate_spec = pl.BlockSpec((LCH, GATE_W), lambda c: (rev(c), 0))
    return pl.pallas_call(
        body, name="mlstm_bwd", grid=(nc,),
        in_specs=[q_spec, k_spec, v_spec, gate_spec,
                  pl.BlockSpec((1, GATE_W), lambda c: (0, 0)),
                  hv_spec, hv_spec, gate_spec] + _state_specs(rev) + [ANY],
        out_specs=[pl.BlockSpec((LCH, QKV_W), lambda c: (rev(c), C_Q // QKV_W)),
                   gate_spec,
                   pl.BlockSpec((8, D), lambda c: (0, 0))],
        out_shape=[jax.ShapeDtypeStruct(dproj.shape, dproj.dtype),
                   jax.ShapeDtypeStruct((n, GATE_W), F32),
                   jax.ShapeDtypeStruct((8, D), F32)],
        scratch_shapes=[pltpu.VMEM((NH, DK, DV), F32), pltpu.VMEM((NH, 1, DK), F32)],
        input_output_aliases={11: 0},
        compiler_params=_cparams("arbitrary"),
    )(proj, proj, proj, gates, bias, h, dh, stats, cs, ns, ms, dproj)


def _proj_spec(tm, col, width):
    return pl.BlockSpec((tm, width), lambda i: (i, col // width))


def _tail_in_specs(tm):
    return [_proj_spec(tm, 0, TAIL_W), _proj_spec(tm, C_XA, CONV_W),
            pl.BlockSpec((8, CONV_W), lambda i: (jnp.maximum(i * (tm // 8) - 1, 0), C_XA // CONV_W))]


def _tail_views(tail_r, conv_r, halo_r):
    cols = lambda ref, c0, w: ref.at[:, pl.ds(c0, w)]
    return (cols(tail_r, C_BA, D), cols(tail_r, C_ZA, D), cols(tail_r, C_O, VD), cols(tail_r, C_ZB, VD),
            cols(tail_r, C_GA, D), cols(tail_r, C_GB, D), cols(conv_r, 0, D), cols(conv_r, D, D),
            cols(halo_r, 0, D), cols(halo_r, D, D))


def _const(shape):
    return pl.BlockSpec(shape, lambda i: (0,) * len(shape))


def _conv_inputs(i, tm, xa_ref, ca_ref, xah_ref, cah_ref):
    u = _f32(xa_ref) * _f32(ca_ref)
    uh = jnp.where(i > 0, _f32(xah_ref) * _f32(cah_ref), 0.0)
    rid = lax.broadcasted_iota(jnp.int32, u.shape, 0)
    u1 = jnp.where(rid == 0, uh[7:8, :], pltpu.roll(u, 1, 0))
    u2 = jnp.where(rid == 0, uh[6:7, :], jnp.where(rid == 1, uh[7:8, :], pltpu.roll(u, 2, 0)))
    return u, u1, u2


def _head_norm(hh, gh):
    out = []
    for j in range(NH):
        hj = hh[:, j * DV:(j + 1) * DV]
        rj = _rstd(hj)
        out.append((hj * rj, rj, gh[:, j * DV:(j + 1) * DV]))
    return out


def _tail_fwd(proj, h, x, p, t, cw, cb, gh, gple, gfin, wa, wb, wo, wpg, wp):
    n = x.shape[0]
    tm = min(256, n)

    def body(tail_r, conv_r, halo_r, h_r, x_r, p_r, t_r,
             cw_r, cb_r, gh_r, gple_r, gfin_r, wa_r, wb_r, wo_r, wpg_r, wp_r,
             apre_o, bpre_o, mg_o, xn1_o, de_o, dgp_o, ya_o, yb_o, x1_o, dx2_o, acc_o):
        ba_r, za_r, o_r, zb_r, ga_r, gb_r, xa_r, ca_r, xah_r, cah_r = _tail_views(tail_r, conv_r, halo_r)
        i = pl.program_id(0)

        @pl.when(i == 0)
        def _():
            acc_o[...] = jnp.zeros_like(acc_o)

        u, u1, u2 = _conv_inputs(i, tm, xa_r, ca_r, xah_r, cah_r)
        cwv = cw_r[...]
        cv = cwv[0:1, :] * u2 + cwv[1:2, :] * u1 + cwv[2:3, :] * u + cb_r[...]
        za = za_r[...]
        a_pre = ba_r[...] * cv.astype(MXU) * (za * _sigmoid(za))
        apre_o[...] = a_pre
        ya = _dot(a_pre, wa_r[...]).astype(MXU)

        hn = _head_norm(_f32(h_r), gh_r[...])
        hbn = jnp.concatenate([xh * g for xh, _, g in hn], axis=1)
        zb = zb_r[...]
        b_pre = _sigmoid(o_r[...]) * hbn.astype(MXU) * (zb * _sigmoid(zb))
        bpre_o[...] = b_pre
        yb = _dot(b_pre, wb_r[...]).astype(MXU)
        ya_o[...] = ya
        yb_o[...] = yb

        mg = _sigmoid(ga_r[...]) * ya + _sigmoid(gb_r[...]) * yb
        mg_o[...] = mg
        x1 = x_r[...] + _dot(mg, wo_r[...])
        x1_o[...] = x1.astype(MXU)
        xn1 = (x1 * _rstd(x1) * gple_r[...]).astype(MXU)
        xn1_o[...] = xn1
        gt = _sigmoid(_dot(xn1, wpg_r[...]))
        e = _dot(p_r[...].astype(MXU), wp_r[...])
        x2 = x1 + gt * e
        r2 = _rstd(x2)
        xh2 = x2 * r2
        gf = gfin_r[...]
        diff = xh2 * gf - t_r[...]
        dy = diff * (1.0 / D)
        dx2 = _norm_bwd(dy, xh2, r2, gf)
        dx2_o[...] = dx2
        de_o[...] = (dx2 * gt).astype(MXU)
        dgp_o[...] = (dx2 * e * gt * (1.0 - gt)).astype(MXU)
        acc_o[0:1, :] += jnp.sum(dy * xh2, axis=0, keepdims=True)
        loss = 0.5 * jnp.sum(jnp.sum(diff * diff, axis=1, keepdims=True), axis=0, keepdims=True) * (1.0 / D)
        acc_o[1:2, :] += jnp.broadcast_to(loss, (1, D))

    row = lambda w, dt: (pl.BlockSpec((tm, w), lambda i: (i, 0)), jax.ShapeDtypeStruct((n, w), dt))
    outs = [row(D, MXU), row(VD, MXU), row(D, MXU), row(D, MXU), row(D, MXU), row(D, MXU),
            row(D, MXU), row(D, MXU), row(D, MXU), row(D, F32),
            (_const((8, D)), jax.ShapeDtypeStruct((8, D), F32))]
    return pl.pallas_call(
        body, name="tail_fwd", grid=(n // tm,),
        in_specs=_tail_in_specs(tm) + [
                  pl.BlockSpec((tm, VD), lambda i: (i, 0)),
                  pl.BlockSpec((tm, D), lambda i: (i, 0)),
                  pl.BlockSpec((tm, PLE), lambda i: (i, 0)),
                  pl.BlockSpec((tm, D), lambda i: (i, 0)),
                  _const((8, D)), _const((1, D)), _const((1, VD)), _const((1, D)), _const((1, D)),
                  _const((D, D)), _const((VD, D)), _const((D, D)), _const((D, D)), _const((PLE, D))],
        out_specs=[s for s, _ in outs],
        out_shape=[s for _, s in outs],
        compiler_params=_cparams("arbitrary"),
    )(proj, proj, proj, h, x, p, t, cw, cb, gh, gple, gfin, wa, wb, wo, wpg, wp)


def _tail_bwd(proj, h, dgp, dx2, x1, ya, yb, cw, cb, gh, gple, wpg, wo, wb, wa):
    n = x1.shape[0]
    tm = min(256, n)

    def body(tail_r, conv_r, halo_r, h_r,
             dgp_r, dx2_r, x1_r, ya_r, yb_r, cw_r, cb_r, gh_r, gple_r,
             wpg_r, wo_r, wb_r, wa_r,
             dproj_o, dcv_o, dh_o, dx1_o, dx1b_o, dya_o, dyb_o, acc_o):
        ba_r, za_r, o_r, zb_r, ga_r, gb_r, xa_r, ca_r, xah_r, cah_r = _tail_views(tail_r, conv_r, halo_r)
        i = pl.program_id(0)

        @pl.when(i == 0)
        def _():
            acc_o[...] = jnp.zeros_like(acc_o)

        dxn1 = _dot_nt(dgp_r[...], wpg_r[...])
        x1 = _f32(x1_r)
        r1 = _rstd(x1)
        xh1 = x1 * r1
        acc_o[0:1, 0:D] += jnp.sum(dxn1 * xh1, axis=0, keepdims=True)
        dx1 = dx2_r[...] + _norm_bwd(dxn1, xh1, r1, gple_r[...])
        dx1_o[...] = dx1
        dx1b = dx1.astype(MXU)
        dx1b_o[...] = dx1b

        dmg = _dot_nt(dx1b, wo_r[...]).astype(MXU)
        sga, dsga = _gate(ga_r[...])
        sgb, dsgb = _gate(gb_r[...])
        dya = dmg * sga
        dyb = dmg * sgb
        dya_o[...] = dya
        dyb_o[...] = dyb
        dproj_o[:, C_GA:C_GA + D] = dmg * ya_r[...] * dsga
        dproj_o[:, C_GB:C_GB + D] = dmg * yb_r[...] * dsgb

        db_pre = _dot_nt(dyb, wb_r[...]).astype(MXU)
        hn = _head_norm(_f32(h_r), gh_r[...])
        hbn = jnp.concatenate([xh * g for xh, _, g in hn], axis=1).astype(MXU)
        so, dso = _gate(o_r[...])
        zb = zb_r[...]
        szb, dszb = _gate(zb)
        sb = zb * szb
        t1 = db_pre * hbn
        dproj_o[:, C_O:C_O + VD] = t1 * (sb * dso)
        dproj_o[:, C_ZB:C_ZB + VD] = t1 * (so * (szb + zb * dszb))
        dhbn = (db_pre * (so * sb)).astype(F32)
        for j, (xh, rj, g) in enumerate(hn):
            dj = dhbn[:, j * DV:(j + 1) * DV]
            acc_o[1:2, j * DV:(j + 1) * DV] += jnp.sum(dj * xh, axis=0, keepdims=True)
            dh_o[:, j * DV:(j + 1) * DV] = _norm_bwd(dj, xh, rj, g).astype(MXU)

        da_pre = _dot_nt(dya, wa_r[...]).astype(MXU)
        u, u1, u2 = _conv_inputs(i, tm, xa_r, ca_r, xah_r, cah_r)
        cwv = cw_r[...]
        cv = cwv[0:1, :] * u2 + cwv[1:2, :] * u1 + cwv[2:3, :] * u + cb_r[...]
        za = za_r[...]
        sza, dsza = _gate(za)
        sa = za * sza
        ba = ba_r[...]
        t2 = da_pre * cv.astype(MXU)
        dproj_o[:, C_BA:C_BA + D] = t2 * sa
        dproj_o[:, C_ZA:C_ZA + D] = t2 * (ba * (sza + za * dsza))
        dcv_b = da_pre * (ba * sa)
        dcv_o[...] = dcv_b
        dcv = dcv_b.astype(F32)
        acc_o[2:3, 0:D] += jnp.sum(dcv, axis=0, keepdims=True)
        acc_o[3:4, 0:D] += jnp.sum(dcv * u2, axis=0, keepdims=True)
        acc_o[4:5, 0:D] += jnp.sum(dcv * u1, axis=0, keepdims=True)
        acc_o[5:6, 0:D] += jnp.sum(dcv * u, axis=0, keepdims=True)

    row = lambda w, dt: (pl.BlockSpec((tm, w), lambda i: (i, 0)), jax.ShapeDtypeStruct((n, w), dt))
    outs = [(pl.BlockSpec((tm, TAIL_W), lambda i: (i, 0)), jax.ShapeDtypeStruct((n, NMAIN), MXU)),
            row(D, MXU), row(VD, MXU), row(D, F32), row(D, MXU), row(D, MXU), row(D, MXU),
            (_const((8, VD)), jax.ShapeDtypeStruct((8, VD), F32))]
    rowin = lambda w: pl.BlockSpec((tm, w), lambda i: (i, 0))
    return pl.pallas_call(
        body, name="tail_bwd", grid=(n // tm,),
        in_specs=_tail_in_specs(tm) + [
                  rowin(VD), rowin(D), rowin(D), rowin(D), rowin(D), rowin(D),
                  _const((8, D)), _const((1, D)), _const((1, VD)), _const((1, D)),
                  _const((D, D)), _const((D, D)), _const((VD, D)), _const((D, D))],
        out_specs=[s for s, _ in outs],
        out_shape=[s for _, s in outs],
        compiler_params=_cparams("arbitrary"),
    )(proj, proj, proj, h, dgp, dx2, x1, ya, yb, cw, cb, gh, gple, wpg, wo, wb, wa)


def _conv_bwd(proj, dcv, cw, dproj):
    n = dcv.shape[0]
    tm = min(1024, n)
    nt = n // tm

    def body(xa_r, ca_r, dcv_r, nxt_r, cw_r, dproj_in, dxc_o):
        del dproj_in
        i = pl.program_id(0)
        dcv_v = _f32(dcv_r)
        nxt = jnp.where(i < nt - 1, _f32(nxt_r), 0.0)
        rid = lax.broadcasted_iota(jnp.int32, dcv_v.shape, 0)
        d1 = jnp.where(rid == tm - 1, nxt[0:1, :], pltpu.roll(dcv_v, tm - 1, 0))
        d2 = jnp.where(rid == tm - 2, nxt[0:1, :],
                       jnp.where(rid == tm - 1, nxt[1:2, :], pltpu.roll(dcv_v, tm - 2, 0)))
        cwv = cw_r[...]
        du = cwv[2:3, :] * dcv_v + cwv[1:2, :] * d1 + cwv[0:1, :] * d2
        dxc_o[:, 0:D] = (du * _f32(ca_r)).astype(MXU)
        dxc_o[:, D:2 * D] = (du * _f32(xa_r)).astype(MXU)

    return pl.pallas_call(
        body, name="conv_bwd", grid=(nt,),
        in_specs=[_proj_spec(tm, C_XA, 1024), _proj_spec(tm, C_CA, 1024),
                  pl.BlockSpec((tm, D), lambda i: (i, 0)),
                  pl.BlockSpec((8, D), lambda i: (jnp.minimum((i + 1) * (tm // 8), n // 8 - 1), 0)),
                  _const((8, D)), ANY],
        out_specs=pl.BlockSpec((tm, CONV_W), lambda i: (i, C_XA // CONV_W)),
        out_shape=jax.ShapeDtypeStruct(dproj.shape, dproj.dtype),
        input_output_aliases={5: 0},
        compiler_params=_cparams("arbitrary"),
    )(proj, proj, dcv, dcv, cw, dproj)


def _position():
    return lax.axis_index("x"), lax.axis_index("y"), lax.axis_index("c")


def _flip(v, bit):
    return 1 - v if bit else v


def _part_start(core, stride, tiled):
    return pl.multiple_of(core * stride, stride) if tiled else core * stride


def _scatter_copies(srcs, dsts, strides, send_sems, recv_sems, local_sems):
    x, y, c = _position()
    me = 4 * x + 2 * y + c
    na = len(srcs)
    copies = []
    for r in range(N_DEV):
        px, py, pc = _flip(x, (r >> 2) & 1), _flip(y, (r >> 1) & 1), _flip(c, r & 1)
        for a in range(na):
            rows = dsts[a].shape[1]
            src = srcs[a].at[2 * px + py, pl.ds(_part_start(pc, strides[a], len(dsts[a].shape) == 3), rows)]
            dst = dsts[a].at[me]
            if r == 0:
                copies.append(pltpu.make_async_copy(src, dst, local_sems.at[a]))
            else:
                k = (r - 1) * na + a
                copies.append(pltpu.make_async_remote_copy(
                    src_ref=src, dst_ref=dst, send_sem=send_sems.at[k], recv_sem=recv_sems.at[k],
                    device_id=(px, py, pc), device_id_type=MESH))
    return copies


def _input_grad(dproj, dgates, w_t, wg_t, x, dx1, g_mix, g_in, g_rest):
    n = x.shape[0]
    tm, tk = min(1024, n), 2048
    nk = NMAIN // tk
    nt = n // tm

    def body(dp_r, dg_r, w_r, wg_r, x_r, dx1_r, g_r, gin, grest,
             gx_o, acc_o, oin, orest, acc, send_sems, recv_sems, local_sems):
        i = pl.program_id(0)
        kk = pl.program_id(1)
        copies = _scatter_copies((gin, grest), (oin, orest), strides, send_sems, recv_sems, local_sems)

        @pl.when((i == 0) & (kk == 0))
        def _():
            acc_o[...] = jnp.zeros_like(acc_o)
            for cp in copies:
                cp.start()

        @pl.when(kk == 0)
        def _():
            acc[...] = _dot(dg_r[...].astype(MXU), wg_r[...])

        acc[...] += _dot(dp_r[...], w_r[...])

        @pl.when(kk == nk - 1)
        def _():
            dhn = acc[...]
            xv = x_r[...]
            r0 = _rstd(xv)
            xh = xv * r0
            acc_o[0:1, :] += jnp.sum(dhn * xh, axis=0, keepdims=True)
            gx_o[...] = dx1_r[...] + _norm_bwd(dhn, xh, r0, g_r[...])

        @pl.when((i == nt - 1) & (kk == nk - 1))
        def _():
            for cp in copies:
                cp.wait()

    nrem = 2 * (N_DEV - 1)
    r_in, r_rest = HALF_SLABS, g_rest.shape[1] // 2
    strides = (HALF_START, r_rest)
    return pl.pallas_call(
        body, name="input_grad", grid=(nt, nk),
        in_specs=[pl.BlockSpec((tm, tk), lambda i, kk: (i, kk)),
                  pl.BlockSpec((tm, GATE_W), lambda i, kk: (i, 0)),
                  pl.BlockSpec((tk, D), lambda i, kk: (kk, 0)),
                  pl.BlockSpec((GATE_W, D), lambda i, kk: (0, 0)),
                  pl.BlockSpec((tm, D), lambda i, kk: (i, 0)),
                  pl.BlockSpec((tm, D), lambda i, kk: (i, 0)),
                  pl.BlockSpec((1, D), lambda i, kk: (0, 0)),
                  ANY, ANY],
        out_specs=[pl.BlockSpec((tm, D), lambda i, kk: (i, 0)),
                   pl.BlockSpec((8, D), lambda i, kk: (0, 0)),
                   ANY, ANY],
        out_shape=[jax.ShapeDtypeStruct((n, D), F32), jax.ShapeDtypeStruct((8, D), F32),
                   jax.ShapeDtypeStruct((N_DEV, r_in) + g_in.shape[2:], g_in.dtype),
                   jax.ShapeDtypeStruct((N_DEV, r_rest) + g_rest.shape[2:], g_rest.dtype)],
        scratch_shapes=[pltpu.VMEM((tm, D), F32),
                        pltpu.SemaphoreType.DMA((nrem,)), pltpu.SemaphoreType.DMA((nrem,)),
                        pltpu.SemaphoreType.DMA((2,))],
        compiler_params=_cparams("arbitrary", "arbitrary"),
    )(dproj, dgates, w_t, wg_t, x, dx1, g_mix, g_in, g_rest)


def _pack_small(acc_f, acc_b, acc_x, gsum):
    def body(f_r, b_r, x_r, s_r, o_r):
        o_r[...] = jnp.zeros_like(o_r)
        o_r[0:1, :] = x_r[0:1, :]
        o_r[1:2, :] = b_r[2:3, 0:D]
        o_r[2:3, :] = b_r[1:2, 0:D]
        o_r[3:4, :] = b_r[1:2, D:2 * D]
        o_r[4:5, :] = b_r[0:1, 0:D]
        o_r[5:6, :] = f_r[0:1, :]
        o_r[6:9, :] = b_r[3:6, 0:D]
        lane = lax.broadcasted_iota(jnp.int32, (1, D), 1)
        o_r[9:10, :] = jnp.where(lane < 2 * NH, s_r[0:1, :], jnp.where(lane == 2 * NH, f_r[1:2, :], 0.0))

    return pl.pallas_call(
        body, name="pack_small",
        out_shape=jax.ShapeDtypeStruct((SMALL_ROWS, D), F32),
    )(acc_f, acc_b, acc_x, gsum)


def _sum_slots(r, name, tr=64, half=None):
    s, rows, w = r.shape
    tr = min(tr, rows)
    assert rows % tr == 0
    nt = rows // tr

    def body(*refs):
        r_ref, o_ref = refs[-2:]
        tot = r_ref[0].astype(F32)
        for k in range(1, s):
            tot = tot + r_ref[k].astype(F32)
        o_ref[...] = tot

    if half is None:
        return pl.pallas_call(
            body, name=name, grid=(nt,),
            in_specs=[pl.BlockSpec((s, tr, w), lambda i: (0, i, 0))],
            out_specs=pl.BlockSpec((tr, w), lambda i: (i, 0)),
            out_shape=jax.ShapeDtypeStruct((rows, w), F32),
            compiler_params=_cparams("arbitrary"),
        )(r)
    return pl.pallas_call(
        body, name=name,
        grid_spec=pltpu.PrefetchScalarGridSpec(
            num_scalar_prefetch=1, grid=(nt,),
            in_specs=[pl.BlockSpec((s, tr, w), lambda i, hf: (0, i, 0))],
            out_specs=pl.BlockSpec((tr, w), lambda i, hf: (hf[0] * nt + i, 0))),
        out_shape=jax.ShapeDtypeStruct((2 * rows, w), F32),
        compiler_params=_cparams("arbitrary"),
    )(half, r)


def _cast_slabs(a, tr=163):
    rows = a.shape[0]
    assert rows % tr == 0

    def body(a_ref, o_ref):
        o_ref[...] = a_ref[...].astype(MXU)

    spec = pl.BlockSpec((tr,) + a.shape[1:], lambda i: (i, 0, 0))
    return pl.pallas_call(
        body, name="cast_w_in", grid=(rows // tr,), in_specs=[spec], out_specs=spec,
        out_shape=jax.ShapeDtypeStruct(a.shape, MXU), compiler_params=_cparams("arbitrary"),
    )(a)


def _sum_slabs(r, name, tr=69):
    s, rows = r.shape[:2]
    assert rows % tr == 0

    def body(r_ref, o_ref):
        tot = r_ref[0].astype(F32)
        for k in range(1, s):
            tot = tot + r_ref[k].astype(F32)
        o_ref[...] = tot

    return pl.pallas_call(
        body, name=name, grid=(rows // tr,),
        in_specs=[pl.BlockSpec((s, tr) + r.shape[2:], lambda i: (0, i, 0, 0))],
        out_specs=pl.BlockSpec((tr,) + r.shape[2:], lambda i: (i, 0, 0)),
        out_shape=jax.ShapeDtypeStruct(r.shape[1:], F32),
        compiler_params=_cparams("arbitrary"),
    )(r)


def _adamw_body(w_r, g_r, m_r, v_r, d_o, m_o, v_o):
    c1 = 1.0 - ADAM_B1 ** ADAM_STEP
    c2 = 1.0 - ADAM_B2 ** ADAM_STEP
    gv = g_r[...]
    mn = ADAM_B1 * m_r[...] + (1.0 - ADAM_B1) * gv
    vn = ADAM_B2 * v_r[...] + (1.0 - ADAM_B2) * (gv * gv)
    m_o[...] = mn
    v_o[...] = vn
    d_o[...] = -ADAM_LR * ((mn / c1) / (jnp.sqrt(vn / c2) + ADAM_EPS) + ADAM_WD * w_r[...])


def _adamw_slabs(w, g, m, v, name, tr):
    rows = w.shape[0]
    assert rows % tr == 0 and w.shape == g.shape

    def body(*refs):
        _adamw_body(*refs)

    spec = pl.BlockSpec((tr,) + w.shape[1:], lambda i: (i, 0, 0))
    shp = jax.ShapeDtypeStruct(w.shape, F32)
    return pl.pallas_call(
        body, name=name, grid=(rows // tr,),
        in_specs=[spec] * 4, out_specs=[spec] * 3, out_shape=[shp] * 3,
        compiler_params=_cparams("arbitrary"),
    )(w, g, m, v)


def _adamw(w, g, m, v, name, g_row0=0, tr=256):
    lead = w.ndim == 3
    rows, cols = w.shape[-2:]
    tr = min(tr, rows)
    assert rows % tr == 0 and g_row0 % tr == 0 and g.shape[1] == cols

    def body(*refs):
        _adamw_body(*refs)

    if lead:
        spec = pl.BlockSpec((None, tr, cols), lambda i: (0, i, 0))
    else:
        spec = pl.BlockSpec((tr, cols), lambda i: (i, 0))
    g_spec = pl.BlockSpec((tr, cols), lambda i: (g_row0 // tr + i, 0))
    shp = jax.ShapeDtypeStruct(w.shape, F32)
    return pl.pallas_call(
        body, name=name, grid=(rows // tr,),
        in_specs=[spec, g_spec, spec, spec], out_specs=[spec] * 3, out_shape=[shp] * 3,
        compiler_params=_cparams("arbitrary"),
    )(w, g, m, v)


def _allgather_chips(halved, whole):
    arrs = [a for a, _ in halved] + list(whole)
    parts = [p for _, p in halved]
    nh, na = len(halved), len(arrs)
    relations = ((1, 0), (0, 1), (1, 1))
    n_ici = 3 * na

    def body(*refs):
        ins, outs = refs[:na], refs[na:2 * na]
        send_sems, recv_sems = refs[2 * na:]
        x, y, c = _position()
        mine = 2 * x + y

        def rows(a, core, stride, size):
            return pl.ds(_part_start(core, stride, len(ins[a].shape) == 2), size)

        def remote(src, dst, k, to):
            return pltpu.make_async_remote_copy(src_ref=src, dst_ref=dst, send_sem=send_sems.at[k],
                                                recv_sem=recv_sems.at[k], device_id=to, device_id_type=MESH)

        first, passed = [], []
        for j, (rx, ry) in enumerate(relations):
            px, py = _flip(x, rx), _flip(y, ry)
            for a in range(na):
                if a < nh:
                    s1, n1, s2, n2 = parts[a]
                    first.append(remote(ins[a].at[rows(a, c, s1, n1)], outs[a].at[mine, rows(a, c, s1, n1)],
                                        j * na + a, (px, py, c)))
                    landed = outs[a].at[2 * px + py, rows(a, c, s2, n2)]
                    passed.append(remote(landed, landed, n_ici + j * nh + a, (x, y, 1 - c)))
                else:
                    first.append(remote(ins[a], outs[a].at[mine], j * na + a, (px, py, c)))
        for cp in first:
            cp.start()
        k = 0
        for j in range(3):
            for a in range(na):
                first[j * na + a].wait_recv()
                if a < nh:
                    passed[k].start()
                    k += 1
        for cp in passed:
            cp.wait_recv()
        for cp in first + passed:
            cp.wait_send()

    nsem = n_ici + 3 * nh
    outs = pl.pallas_call(
        body, name="allgather_weights",
        in_specs=[ANY] * na, out_specs=[ANY] * na,
        out_shape=[jax.ShapeDtypeStruct((N_CHIPS,) + a.shape, a.dtype) for a in arrs],
        scratch_shapes=[pltpu.SemaphoreType.DMA((nsem,)), pltpu.SemaphoreType.DMA((nsem,))],
    )(*arrs)
    chip = 2 * lax.axis_index("x") + lax.axis_index("y")
    return [lax.dynamic_update_slice(o, a[None], (chip,) + (0,) * a.ndim) for o, a in zip(outs, arrs)]


def _join_halves(piece, full_rest, small):
    def body(pc_ref, rest_in, sm, oin, orest, osm, send_sems, recv_sems, local_sem):
        del rest_in
        x, y, c = _position()
        me = 4 * x + 2 * y + c
        sib = (x, y, 1 - c)
        rows = orest.shape[0] // 2
        mine = orest.at[pl.ds(pl.multiple_of(c * rows, rows), rows)]
        copies = [
            pltpu.make_async_remote_copy(
                src_ref=pc_ref.at[pl.ds(c, HALF_START)], dst_ref=oin.at[pl.ds(c * HALF_SLABS, HALF_START)],
                send_sem=send_sems.at[0], recv_sem=recv_sems.at[0], device_id=sib, device_id_type=MESH),
            pltpu.make_async_remote_copy(
                src_ref=mine, dst_ref=mine, send_sem=send_sems.at[1], recv_sem=recv_sems.at[1],
                device_id=sib, device_id_type=MESH)]
        na = 2
        copies.append(pltpu.make_async_copy(sm, osm.at[me], local_sem))
        for r in range(1, N_DEV):
            to = (_flip(x, (r >> 2) & 1), _flip(y, (r >> 1) & 1), _flip(c, r & 1))
            copies.append(pltpu.make_async_remote_copy(
                src_ref=sm, dst_ref=osm.at[me], send_sem=send_sems.at[na + r - 1],
                recv_sem=recv_sems.at[na + r - 1], device_id=to, device_id_type=MESH))
        for cp in copies:
            cp.start()
        for cp in copies:
            cp.wait()

    nsem = 2 + N_DEV - 1
    return pl.pallas_call(
        body, name="join_halves",
        in_specs=[ANY] * 3, out_specs=[ANY] * 3,
        out_shape=[jax.ShapeDtypeStruct((SLABS_CHIP,) + piece.shape[1:], piece.dtype),
                   jax.ShapeDtypeStruct(full_rest.shape, full_rest.dtype),
                   jax.ShapeDtypeStruct((N_DEV,) + small.shape, small.dtype)],
        scratch_shapes=[pltpu.SemaphoreType.DMA((nsem,)), pltpu.SemaphoreType.DMA((nsem,)),
                        pltpu.SemaphoreType.DMA],
        input_output_aliases={1: 1},
    )(piece, full_rest, small)


def _to_internal(w_slabs):
    order = sorted(SEGMENTS, key=lambda s: s[2])
    main = jnp.concatenate([w_slabs[g0 // 2:(g0 + w) // 2] for g0, w, _ in order], axis=0)
    gate = w_slabs[GATE_COL // 2:GATE_COL // 2 + NH].reshape(2 * NH, D)
    return main.reshape(NMAIN, D), jnp.pad(gate, ((0, GATE_W - 2 * NH), (0, 0)))


def _to_global(main_t, gate_t):
    main = main_t.reshape(NMAIN // 2, 16, D // 8)
    parts = sorted([(g0, main[i0 // 2:(i0 + w) // 2]) for g0, w, i0 in SEGMENTS]
                   + [(GATE_COL, gate_t[0:2 * NH].reshape(NH, 16, D // 8))], key=lambda s: s[0])
    return jnp.concatenate([p for _, p in parts], axis=0)


def _pad_rows(a, rows=8):
    return jnp.pad(a, ((0, rows - a.shape[0]), (0, 0)))


def kernel(x, p, g_mix, w_in, conv_w, conv_b, w_a_out, b_gates, g_head, w_b_out, w_o, g_ple, w_ple_gate, w_ple, g_final, loss_target, m_g_mix, m_w_in, m_conv_w, m_conv_b, m_w_a_out, m_b_gates, m_g_head, m_w_b_out, m_w_o, m_g_ple, m_w_ple_gate, m_w_ple, m_g_final, v_g_mix, v_w_in, v_conv_w, v_conv_b, v_w_a_out, v_b_gates, v_g_head, v_w_b_out, v_w_o, v_g_ple, v_w_ple_gate, v_w_ple, v_g_final):
    chip = 2 * lax.axis_index("x") + lax.axis_index("y")
    xs, ps, ts = x[0], p[0, 0], loss_target[0]
    g_fin = g_final.reshape(1, D)

    w_slabs = _cast_slabs(jnp.transpose(w_in, (2, 0, 1)).reshape(SLABS_CHIP, 16, D // 8))
    rest = [w[0].astype(MXU) for w in (w_a_out, w_b_out, w_o, w_ple_gate, w_ple)]
    even = lambda a: (a.shape[0] // 2,) * 4
    g_win, g_wa, g_wb, g_wo, g_wpg, g_wp, g_cw = _allgather_chips(
        [(w_slabs, (HALF_START, HALF_SLABS, HALF_SLABS, HALF_START))] + [(a, even(a)) for a in rest],
        [_pad_rows(conv_w[0])])
    w_t, wg_t = _to_internal(g_win.reshape(N_IN // 2, 16, D // 8))
    wa, wb, wo, wpg = [g.reshape(-1, D) for g in (g_wa, g_wb, g_wo, g_wpg)]
    wp = jnp.transpose(g_wp, (1, 0, 2)).reshape(PLE, D)
    cw = jnp.transpose(g_cw, (1, 0, 2)).reshape(8, D)

    bias = jnp.pad(b_gates, ((0, 0), (0, GATE_W - 2 * NH)))
    proj, hn, gates = _proj(xs, g_mix, w_t, wg_t)
    h, stats, cs, ns, ms = _mlstm_fwd(proj, gates, bias)
    (a_pre, b_pre, mg, xn1, de, dgp, ya, yb, x1, dx2, acc_f) = _tail_fwd(
        proj, h, xs, ps, ts, cw, conv_b, g_head, g_ple, g_fin, wa, wb, wo, wpg, wp)
    dproj, dcv, dh, dx1, dx1b, dya, dyb, acc_b = _tail_bwd(
        proj, h, dgp, dx2, x1, ya, yb, cw, conv_b, g_head, g_ple, wpg, wo, wb, wa)
    dproj = _conv_bwd(proj, dcv, cw, dproj)
    dproj, dgates, gsum = _mlstm_bwd(proj, gates, bias, h, dh, stats, cs, ns, ms, dproj)
    d_main = _matmul_tn(dproj, hn, "dw_in", out_dtype=WIRE, tk=4096)
    d_gate = _matmul_tn(dgates, hn, "dw_gate", out_dtype=WIRE)
    d_wa = _matmul_tn(a_pre, dya, "dw_a_out", out_dtype=WIRE)
    d_wb = _matmul_tn(b_pre, dyb, "dw_b_out", out_dtype=WIRE)
    d_wo = _matmul_tn(mg, dx1b, "dw_o", out_dtype=WIRE)
    d_wpg = _matmul_tn(xn1, dgp, "dw_ple_gate", out_dtype=WIRE)
    d_wp = _matmul_tn(ps, de, "dw_ple", out_dtype=WIRE)

    g_in = _to_global(d_main, d_gate).reshape(N_CHIPS, SLABS_CHIP, 16, D // 8)
    d_wp_c = jnp.transpose(d_wp.reshape(PLE, N_CHIPS, PLE), (1, 0, 2)).reshape(N_CHIPS, PACK_ROWS[4], D)
    g_rest = jnp.concatenate(
        [d_wa.reshape(N_CHIPS, -1, D), d_wb.reshape(N_CHIPS, -1, D), d_wo.reshape(N_CHIPS, -1, D),
         d_wpg.reshape(N_CHIPS, -1, D), d_wp_c], axis=1)
    grad_x, acc_x, r_in, r_rest = _input_grad(dproj, dgates, w_t, wg_t, xs, dx1, g_mix, g_in, g_rest)
    small = _pack_small(acc_f, acc_b, acc_x, gsum)
    core = lax.axis_index("c").astype(jnp.int32)
    piece = _sum_slabs(r_in, "sum_w_in")
    gw_in, gw_rest, r_small = _join_halves(
        piece, _sum_slots(r_rest, "sum_rest", tr=96, half=core.reshape(1)), small)
    gw_in = lax.dynamic_update_slice(gw_in, piece, (core * HALF_START, 0, 0)).reshape(COLS_CHIP, 8, D // 8)
    gs = _sum_slots(r_small, "sum_small", tr=SMALL_ROWS)

    big = []
    row0 = 0
    for name, w, m, v in (("w_a_out", w_a_out, m_w_a_out, v_w_a_out), ("w_b_out", w_b_out, m_w_b_out, v_w_b_out),
                          ("w_o", w_o, m_w_o, v_w_o), ("w_ple_gate", w_ple_gate, m_w_ple_gate, v_w_ple_gate)):
        big.append((name, w, m, v, gw_rest, row0))
        row0 += w.shape[1]
    g_wp = gw_rest[row0:row0 + PACK_ROWS[4]].reshape(PLE, PLE)
    big.append(("w_ple", w_ple, m_w_ple, v_w_ple, g_wp, 0))
    upd = {name: _adamw(w, g, m, v, "adamw_" + name, g_row0=r0) for name, w, m, v, g, r0 in big}
    g_big = {name: (g if name == "w_ple" else g[r0:r0 + w.shape[1]])[None] for name, w, m, v, g, r0 in big}
    slabs = lambda a: jnp.transpose(a, (2, 0, 1)).reshape(COLS_CHIP, 8, D // 8)
    unslab = lambda a: jnp.transpose(a, (1, 2, 0)).reshape(1, D, COLS_CHIP)
    upd["w_in"] = [unslab(u) for u in _adamw_slabs(
        slabs(w_in), gw_in, slabs(m_w_in), slabs(v_w_in), "adamw_w_in", tr=326)]
    g_big["w_in"] = unslab(gw_in)

    lane = lax.broadcasted_iota(jnp.int32, (1, D), 1)
    g_small = jnp.concatenate([gs[0:6], jnp.where(lane < 2 * NH, gs[9:10], 0.0), jnp.zeros((1, D), F32)], axis=0)

    def small_pack(gm, cb_, bg, gh, gp, gf):
        return jnp.concatenate([gm, cb_, gh.reshape(2, D), gp, gf.reshape(1, D),
                                jnp.pad(bg, ((0, 0), (0, D - 2 * NH))), jnp.zeros((1, D), F32)], axis=0)

    ws = small_pack(g_mix, conv_b, b_gates, g_head, g_ple, g_final)
    ms_ = small_pack(m_g_mix, m_conv_b, m_b_gates, m_g_head, m_g_ple, m_g_final)
    vs = small_pack(v_g_mix, v_conv_b, v_b_gates, v_g_head, v_g_ple, v_g_final)
    upd_s = _adamw(ws, g_small, ms_, vs, "adamw_small")
    g_cw_mine = _pad_rows(lax.dynamic_slice(gs[6:9], (0, chip * PLE), (3, PLE)))
    upd_c = _adamw(_pad_rows(conv_w[0]), g_cw_mine, _pad_rows(m_conv_w[0]), _pad_rows(v_conv_w[0]),
                   "adamw_conv_w")

    def leaves(bigs, sm, cwv):
        return [sm[0:1], bigs["w_in"], cwv[0:3][None], sm[1:2], bigs["w_a_out"], sm[6:7, 0:2 * NH],
                sm[2:4].reshape(1, VD), bigs["w_b_out"], bigs["w_o"], sm[4:5], bigs["w_ple_gate"],
                bigs["w_ple"], sm[5]]

    loss = gs[9, 2 * NH]
    outs = [loss, grad_x[None]] + leaves(g_big, g_small, g_cw_mine)
    for k in range(3):
        outs += leaves({name: u[k] for name, u in upd.items()}, upd_s[k], upd_c[k])
    return tuple(outs)
```

```python
import jax
import jax.numpy as jnp
from jax import lax
from jax.experimental import pallas as pl
from jax.experimental.pallas import tpu as pltpu

F32 = jnp.float32
MXU = jnp.bfloat16
WIRE = jnp.bfloat16

D = 1024
NH, DK, DV = 4, 256, 512
VD = NH * DV
PLE = 256
LCH = 256
EPS = 1e-6
N_IN = 14344
NMAIN = 14336
GATE_W = 128
N_CHIPS, N_DEV = 4, 8

C_BA, C_ZA, C_O, C_ZB, C_GA, C_GB = 0, 1024, 2048, 4096, 6144, 7168
QK = NH * DK
C_Q, C_K, C_V, C_XA, C_CA = 8192, 9216, 10240, 12288, 13312
QKV_W = 2 * QK + VD
TAIL_W = 8192
CONV_W = 2048
SEGMENTS = (
    (0, 1024, C_XA), (1024, 1024, C_BA), (2048, 1024, C_CA), (3072, 1024, C_ZA),
    (4096, QKV_W, C_Q), (8192, 4096, C_O), (12296, 2048, C_GA),
)
GATE_COL = 12288

COLS_CHIP = N_IN // N_CHIPS
SLABS_CHIP = COLS_CHIP // 2
HALF_START = SLABS_CHIP // 2
HALF_SLABS = SLABS_CHIP - HALF_START

PACK_ROWS = (256, 512, 256, 256, 64)
SMALL_ROWS = 16

ADAM_LR, ADAM_B1, ADAM_B2, ADAM_EPS, ADAM_WD, ADAM_STEP = 0.001, 0.9, 0.999, 1e-08, 0.01, 10

VMEM_LIMIT = 56 * 1024 * 1024
MESH = pl.DeviceIdType.MESH
ANY = pl.BlockSpec(memory_space=pl.ANY)


def _cparams(*sem):
    return pltpu.CompilerParams(dimension_semantics=sem, vmem_limit_bytes=VMEM_LIMIT)


def _dot(a, b):
    return jnp.dot(a, b, preferred_element_type=F32)


def _dot_nt(a, b):
    return lax.dot_general(a, b, (((1,), (1,)), ((), ())), preferred_element_type=F32)


def _dot_tn(a, b):
    return lax.dot_general(a, b, (((0,), (0,)), ((), ())), preferred_element_type=F32)


def _sigmoid(x):
    return 1.0 / (1.0 + jnp.exp(-x))


def _logsig(x):
    return jnp.minimum(x, 0.0) - jnp.log(1.0 + jnp.exp(-jnp.abs(x)))


GATE_FLOOR = -80.0


def _gate(v):
    e = jnp.exp(-jnp.maximum(v, GATE_FLOOR))
    s = 1.0 / (1.0 + e)
    return s, e * s * s


def _rstd(x):
    return lax.rsqrt(jnp.mean(x * x, axis=-1, keepdims=True) + EPS)


def _norm_bwd(dy, xhat, r, g):
    dxh = dy * g
    return r * (dxh - xhat * jnp.mean(dxh * xhat, axis=-1, keepdims=True))


def _f32(ref):
    return ref[...].astype(F32)


def _proj(x, g_mix, w_t, wg_t, tm=1024, tn=3584):
    n = x.shape[0]
    m = w_t.shape[0]
    tm = min(tm, n)

    def body(x_ref, g_ref, b_ref, wg_ref, o_ref, hn_ref, gate_ref):
        @pl.when(pl.program_id(1) == 0)
        def _():
            xv = x_ref[...]
            hn = (xv * _rstd(xv) * g_ref[...]).astype(MXU)
            hn_ref[...] = hn
            gate_ref[...] = _dot_nt(hn, wg_ref[...])

        o_ref[...] = _dot_nt(hn_ref[...], b_ref[...]).astype(MXU)

    return pl.pallas_call(
        body, name="proj", grid=(n // tm, m // tn),
        in_specs=[pl.BlockSpec((tm, D), lambda i, j: (i, 0)),
                  pl.BlockSpec((1, D), lambda i, j: (0, 0)),
                  pl.BlockSpec((tn, D), lambda i, j: (j, 0)),
                  pl.BlockSpec((GATE_W, D), lambda i, j: (0, 0))],
        out_specs=[pl.BlockSpec((tm, tn), lambda i, j: (i, j)),
                   pl.BlockSpec((tm, D), lambda i, j: (i, 0)),
                   pl.BlockSpec((tm, GATE_W), lambda i, j: (i, 0))],
        out_shape=[jax.ShapeDtypeStruct((n, m), MXU), jax.ShapeDtypeStruct((n, D), MXU),
                   jax.ShapeDtypeStruct((n, GATE_W), F32)],
        compiler_params=_cparams("arbitrary", "arbitrary"),
    )(x, g_mix, w_t, wg_t)


def _matmul_tn(a, b, name, out_dtype=F32, ta=1024, tb=1024, tk=4096):
    n, ka = a.shape
    kb = b.shape[1]
    ta, tb, tk = min(ta, ka), min(tb, kb), min(tk, n)
    nk = n // tk

    def body(a_ref, b_ref, o_ref, acc):
        kk = pl.program_id(2)

        @pl.when(kk == 0)
        def _():
            acc[...] = jnp.zeros_like(acc)

        acc[...] += _dot_tn(a_ref[...].astype(MXU), b_ref[...].astype(MXU))

        @pl.when(kk == nk - 1)
        def _():
            o_ref[...] = acc[...].astype(out_dtype)

    return pl.pallas_call(
        body, name=name, grid=(ka // ta, kb // tb, nk),
        in_specs=[pl.BlockSpec((tk, ta), lambda i, j, kk: (kk, i)),
                  pl.BlockSpec((tk, tb), lambda i, j, kk: (kk, j))],
        out_specs=pl.BlockSpec((ta, tb), lambda i, j, kk: (i, j)),
        out_shape=jax.ShapeDtypeStruct((ka, kb), out_dtype),
        scratch_shapes=[pltpu.VMEM((ta, tb), F32)],
        compiler_params=_cparams("arbitrary", "arbitrary", "arbitrary"),
    )(a, b)


def _sub_row(block_t, j):
    sub = lax.broadcasted_iota(jnp.int32, block_t.shape, 0)
    return jnp.sum(jnp.where(sub == j, block_t, 0.0), axis=0, keepdims=True)


def _chunk_cumsum(g):
    n = g.shape[0]
    r = lax.broadcasted_iota(jnp.int32, (n, n), 0)
    c = lax.broadcasted_iota(jnp.int32, (n, n), 1)
    return jnp.dot(jnp.where(r >= c, 1.0, 0.0), _logsig(g), precision=lax.Precision.HIGHEST,
                   preferred_element_type=F32)


def _chunk_decay(li_col, li_row, b_col, b_row, m_prev):
    n = li_col.shape[0]
    r = lax.broadcasted_iota(jnp.int32, (n, n), 0)
    c = lax.broadcasted_iota(jnp.int32, (n, n), 1)
    tri = r >= c
    b_last = b_col[n - 1:n, :]
    rr = li_row - b_row
    a_col = b_col + m_prev
    g_col = b_last - b_col + li_col
    m_new = jnp.maximum(b_last + m_prev, jnp.max(g_col, axis=0, keepdims=True))
    w_col = jnp.exp(g_col - m_new)
    decay = jnp.exp(b_last + m_prev - m_new)
    return tri, rr, a_col, m_new, w_col, decay


def _qkv_specs(row_of):
    q_spec = pl.BlockSpec((LCH, QK), lambda c: (row_of(c), C_Q // QK))
    k_spec = pl.BlockSpec((LCH, QK), lambda c: (row_of(c), C_K // QK))
    v_spec = pl.BlockSpec((LCH, VD), lambda c: (row_of(c), C_V // VD))
    return q_spec, k_spec, v_spec


def _state_specs(row_of):
    return [pl.BlockSpec((NH, None, DK, DV), lambda c: (0, row_of(c), 0, 0)),
            pl.BlockSpec((NH, None, 1, DK), lambda c: (0, row_of(c), 0, 0)),
            pl.BlockSpec((NH, None, 1, GATE_W), lambda c: (0, row_of(c), 0, 0))]


def _lane_put(col, lane_id, width=GATE_W):
    lane = lax.broadcasted_iota(jnp.int32, (col.shape[0], width), 1)
    return jnp.where(lane == lane_id, col, 0.0)


def _lane_get(block, lane_id):
    lane = lax.broadcasted_iota(jnp.int32, block.shape, 1)
    return jnp.sum(jnp.where(lane == lane_id, block, 0.0), axis=1, keepdims=True)


def _mlstm_fwd(proj, gates, bias):
    n = proj.shape[0]
    nc = n // LCH

    def body(q_ref, k_ref, v_ref, g_ref, bias_ref,
             h_ref, st_ref, cs_ref, ns_ref, ms_ref, c_scr, n_scr, m_scr):
        @pl.when(pl.program_id(0) == 0)
        def _():
            c_scr[...] = jnp.zeros_like(c_scr)
            n_scr[...] = jnp.zeros_like(n_scr)
            m_scr[...] = jnp.full_like(m_scr, -jnp.inf)

        g = g_ref[...] + bias_ref[...]
        gt = g.T[0:8, :]
        b = _chunk_cumsum(g)
        bt = b.T[0:8, :]
        stats = jnp.zeros((LCH, GATE_W), F32)
        for hd in range(NH):
            m_all = m_scr[hd]
            m_prev = m_all[0:1, 0:1]
            b_col = _lane_get(b, NH + hd)
            tri, rr, a_col, m_new, w_col, decay = _chunk_decay(
                _lane_get(g, hd), _sub_row(gt, hd), b_col, _sub_row(bt, NH + hd), m_prev)
            dmat = jnp.where(tri, b_col + rr, -jnp.inf)
            m_col = jnp.maximum(a_col, jnp.max(dmat, axis=1, keepdims=True))
            dl = jnp.exp(dmat - m_col)
            inter = jnp.exp(a_col - m_col)

            qs = q_ref[:, hd * DK:(hd + 1) * DK] * (DK ** -0.5)
            kk = k_ref[:, hd * DK:(hd + 1) * DK]
            vv = v_ref[:, hd * DV:(hd + 1) * DV]
            cst = c_scr[hd]
            nst = n_scr[hd]
            cs_ref[hd] = cst
            ns_ref[hd] = nst
            ms_ref[hd] = m_all

            sc = _dot_nt(qs, kk) * dl
            num = _dot(sc.astype(MXU), vv) + inter * _dot(qs, cst.astype(MXU))
            den = (jnp.sum(sc, axis=1, keepdims=True)
                   + inter * jnp.sum(qs.astype(F32) * nst, axis=1, keepdims=True))
            nrm = jnp.maximum(jnp.abs(den), jnp.exp(-m_col))
            h_ref[:, hd * DV:(hd + 1) * DV] = (num * (1.0 / nrm)).astype(h_ref.dtype)
            stats = stats + _lane_put(den, hd) + _lane_put(m_col, NH + hd)

            kw = kk.astype(F32) * w_col
            c_scr[hd] = decay * cst + _dot_tn(kw.astype(MXU), vv)
            n_scr[hd] = decay * nst + jnp.sum(kw, axis=0, keepdims=True)
            m_scr[hd] = jnp.broadcast_to(m_new, (1, GATE_W))
        st_ref[...] = stats

    q_spec, k_spec, v_spec = _qkv_specs(lambda c: c)
    return pl.pallas_call(
        body, name="mlstm_fwd", grid=(nc,),
        in_specs=[q_spec, k_spec, v_spec,
                  pl.BlockSpec((LCH, GATE_W), lambda c: (c, 0)),
                  pl.BlockSpec((1, GATE_W), lambda c: (0, 0))],
        out_specs=[pl.BlockSpec((LCH, VD), lambda c: (c, 0)),
                   pl.BlockSpec((LCH, GATE_W), lambda c: (c, 0))] + _state_specs(lambda c: c),
        out_shape=[jax.ShapeDtypeStruct((n, VD), MXU),
                   jax.ShapeDtypeStruct((n, GATE_W), F32),
                   jax.ShapeDtypeStruct((NH, nc, DK, DV), F32),
                   jax.ShapeDtypeStruct((NH, nc, 1, DK), F32),
                   jax.ShapeDtypeStruct((NH, nc, 1, GATE_W), F32)],
        scratch_shapes=[pltpu.VMEM((NH, DK, DV), F32), pltpu.VMEM((NH, 1, DK), F32),
                        pltpu.VMEM((NH, 1, GATE_W), F32)],
        compiler_params=_cparams("arbitrary"),
    )(proj, proj, proj, gates, bias)


def _mlstm_bwd(proj, gates, bias, h, dh, stats, cs, ns, ms, dproj):
    n = proj.shape[0]
    nc = n // LCH

    def body(q_ref, k_ref, v_ref, g_ref, bias_ref, h_ref, dh_ref, st_ref,
             cs_ref, ns_ref, ms_ref, dproj_in,
             dqkv_ref, dg_ref, gsum_ref, dc_scr, dn_scr):
        del dproj_in

        @pl.when(pl.program_id(0) == 0)
        def _():
            dc_scr[...] = jnp.zeros_like(dc_scr)
            dn_scr[...] = jnp.zeros_like(dn_scr)
            gsum_ref[...] = jnp.zeros_like(gsum_ref)

        g = g_ref[...] + bias_ref[...]
        gt = g.T[0:8, :]
        b = _chunk_cumsum(g)
        bt = b.T[0:8, :]
        stats = st_ref[...]
        r = lax.broadcasted_iota(jnp.int32, (LCH, LCH), 0)
        c = lax.broadcasted_iota(jnp.int32, (LCH, LCH), 1)
        sub8 = lax.broadcasted_iota(jnp.int32, (8, LCH), 0)
        last = lax.broadcasted_iota(jnp.int32, (LCH, 1), 0) == LCH - 1
        dli_all = jnp.zeros((LCH, GATE_W), F32)
        db_all = jnp.zeros((LCH, GATE_W), F32)
        colsum_t = jnp.zeros((8, LCH), F32)
        for hd in range(NH):
            m_prev = ms_ref[hd][0:1, 0:1]
            b_col = _lane_get(b, NH + hd)
            tri, rr, a_col, m_new, w_col, decay = _chunk_decay(
                _lane_get(g, hd), _sub_row(gt, hd), b_col, _sub_row(bt, NH + hd), m_prev)
            m_col = _lane_get(stats, NH + hd)
            dl = jnp.where(tri, jnp.exp((b_col - m_col) + rr), 0.0)
            inter = jnp.exp(a_col - m_col)

            qs = q_ref[:, hd * DK:(hd + 1) * DK] * (DK ** -0.5)
            kk = k_ref[:, hd * DK:(hd + 1) * DK]
            vv = v_ref[:, hd * DV:(hd + 1) * DV]
            qf = qs.astype(F32)
            kf = kk.astype(F32)
            cst = cs_ref[hd]
            nst = ns_ref[hd]
            cb = cst.astype(MXU)
            dcn = dc_scr[hd]
            dnn = dn_scr[hd]
            dcb = dcn.astype(MXU)

            den = _lane_get(stats, hd)
            floor = jnp.exp(-m_col)
            nrm = jnp.maximum(jnp.abs(den), floor)
            dhv = dh_ref[:, hd * DV:(hd + 1) * DV].astype(F32)
            rn = 1.0 / nrm
            dnum = dhv * rn
            dnum_b = dnum.astype(MXU)
            dhh = jnp.sum(dhv * h_ref[:, hd * DV:(hd + 1) * DV].astype(F32), axis=1, keepdims=True)
            dden = jnp.where(jnp.abs(den) > floor, -dhh * rn * jnp.sign(den), 0.0)

            sc = _dot_nt(qs, kk) * dl
            dsc = _dot_nt(dnum_b, vv) + dden
            da = (dl * dsc).astype(MXU)
            gmat = sc * dsc

            dq_state = _dot_nt(dnum_b, cb) + dden * nst
            dq = _dot(da, kk) + inter * dq_state
            dk_state = w_col * (_dot_nt(vv, dcb) + dnn)
            dk = _dot_tn(da, qs) + dk_state
            kw = (kf * w_col).astype(MXU)
            dv = _dot_tn(sc.astype(MXU), dnum_b) + _dot(kw, dcb)
            dqkv_ref[:, hd * DK:(hd + 1) * DK] = (dq * (DK ** -0.5)).astype(dqkv_ref.dtype)
            dqkv_ref[:, QK + hd * DK:QK + (hd + 1) * DK] = dk.astype(dqkv_ref.dtype)
            dqkv_ref[:, 2 * QK + hd * DV:2 * QK + (hd + 1) * DV] = dv.astype(dqkv_ref.dtype)

            e_col = inter * jnp.sum(qf * dq_state, axis=1, keepdims=True)
            h_col = jnp.sum(kf * dk_state, axis=1, keepdims=True)
            f_dec = decay * (jnp.sum(jnp.sum(cst * dcn, axis=1, keepdims=True), axis=0, keepdims=True)
                             + jnp.sum(nst * dnn, axis=1, keepdims=True))
            row_g = jnp.sum(gmat, axis=1, keepdims=True)
            col_g = jnp.sum(gmat, axis=0, keepdims=True)
            colsum_t = colsum_t + jnp.where(sub8 == hd, col_g, 0.0) - jnp.where(sub8 == NH + hd, col_g, 0.0)
            db_col = row_g + e_col - h_col
            db_col = db_col + jnp.where(last, jnp.sum(h_col, axis=0, keepdims=True) + f_dec, 0.0)
            dli_all = dli_all + _lane_put(h_col, hd)
            db_all = db_all + _lane_put(db_col, NH + hd)

            dc_scr[hd] = decay * dcn + _dot_tn((qf * inter).astype(MXU), dnum_b)
            dn_scr[hd] = decay * dnn + jnp.sum(qf * (inter * dden), axis=0, keepdims=True)
        colsum = jnp.concatenate([colsum_t, jnp.zeros((GATE_W - 8, LCH), F32)], axis=0).T
        lane = lax.broadcasted_iota(jnp.int32, (LCH, GATE_W), 1)
        dli_all = dli_all + jnp.where(lane < NH, colsum, 0.0)
        db_all = db_all + jnp.where(lane >= NH, colsum, 0.0)
        dlf_all = jnp.dot(jnp.where(c >= r, 1.0, 0.0), db_all, precision=lax.Precision.HIGHEST,
                          preferred_element_type=F32)
        dg = dli_all + dlf_all * _sigmoid(-g)
        dg_ref[...] = dg
        gsum_ref[0:1, 0:GATE_W] += jnp.sum(dg, axis=0, keepdims=True)

    rev = lambda c: nc - 1 - c
    q_spec, k_spec, v_spec = _qkv_specs(rev)
    hv_spec = pl.BlockSpec((LCH, VD), lambda c: (rev(c), 0))
    gate_spec = pl.BlockSpec((LCH, GATE_W), lambda c: (rev(c), 0))
    return pl.pallas_call(
        body, name="mlstm_bwd", grid=(nc,),
        in_specs=[q_spec, k_spec, v_spec, gate_spec,
                  pl.BlockSpec((1, GATE_W), lambda c: (0, 0)),
                  hv_spec, hv_spec, gate_spec] + _state_specs(rev) + [ANY],
        out_specs=[pl.BlockSpec((LCH, QKV_W), lambda c: (rev(c), C_Q // QKV_W)),
                   gate_spec,
                   pl.BlockSpec((8, D), lambda c: (0, 0))],
        out_shape=[jax.ShapeDtypeStruct(dproj.shape, dproj.dtype),
                   jax.ShapeDtypeStruct((n, GATE_W), F32),
                   jax.ShapeDtypeStruct((8, D), F32)],
        scratch_shapes=[pltpu.VMEM((NH, DK, DV), F32), pltpu.VMEM((NH, 1, DK), F32)],
        input_output_aliases={11: 0},
        compiler_params=_cparams("arbitrary"),
    )(proj, proj, proj, gates, bias, h, dh, stats, cs, ns, ms, dproj)


def _proj_spec(tm, col, width):
    return pl.BlockSpec((tm, width), lambda i: (i, col // width))


def _tail_in_specs(tm):
    return [_proj_spec(tm, 0, TAIL_W), _proj_spec(tm, C_XA, CONV_W),
            pl.BlockSpec((8, CONV_W), lambda i: (jnp.maximum(i * (tm // 8) - 1, 0), C_XA // CONV_W))]


def _tail_views(tail_r, conv_r, halo_r):
    cols = lambda ref, c0, w: ref.at[:, pl.ds(c0, w)]
    return (cols(tail_r, C_BA, D), cols(tail_r, C_ZA, D), cols(tail_r, C_O, VD), cols(tail_r, C_ZB, VD),
            cols(tail_r, C_GA, D), cols(tail_r, C_GB, D), cols(conv_r, 0, D), cols(conv_r, D, D),
            cols(halo_r, 0, D), cols(halo_r, D, D))


def _const(shape):
    return pl.BlockSpec(shape, lambda i: (0,) * len(shape))


def _conv_inputs(i, tm, xa_ref, ca_ref, xah_ref, cah_ref):
    u = _f32(xa_ref) * _f32(ca_ref)
    uh = jnp.where(i > 0, _f32(xah_ref) * _f32(cah_ref), 0.0)
    rid = lax.broadcasted_iota(jnp.int32, u.shape, 0)
    u1 = jnp.where(rid == 0, uh[7:8, :], pltpu.roll(u, 1, 0))
    u2 = jnp.where(rid == 0, uh[6:7, :], jnp.where(rid == 1, uh[7:8, :], pltpu.roll(u, 2, 0)))
    return u, u1, u2


def _head_norm(hh, gh):
    out = []
    for j in range(NH):
        hj = hh[:, j * DV:(j + 1) * DV]
        rj = _rstd(hj)
        out.append((hj * rj, rj, gh[:, j * DV:(j + 1) * DV]))
    return out


def _tail_fwd(proj, h, x, p, t, cw, cb, gh, gple, gfin, wa, wb, wo, wpg, wp):
    n = x.shape[0]
    tm = min(256, n)

    def body(tail_r, conv_r, halo_r, h_r, x_r, p_r, t_r,
             cw_r, cb_r, gh_r, gple_r, gfin_r, wa_r, wb_r, wo_r, wpg_r, wp_r,
             apre_o, bpre_o, mg_o, xn1_o, de_o, dgp_o, ya_o, yb_o, x1_o, dx2_o, acc_o):
        ba_r, za_r, o_r, zb_r, ga_r, gb_r, xa_r, ca_r, xah_r, cah_r = _tail_views(tail_r, conv_r, halo_r)
        i = pl.program_id(0)

        @pl.when(i == 0)
        def _():
            acc_o[...] = jnp.zeros_like(acc_o)

        u, u1, u2 = _conv_inputs(i, tm, xa_r, ca_r, xah_r, cah_r)
        cwv = cw_r[...]
        cv = cwv[0:1, :] * u2 + cwv[1:2, :] * u1 + cwv[2:3, :] * u + cb_r[...]
        za = za_r[...]
        a_pre = ba_r[...] * cv.astype(MXU) * (za * _sigmoid(za))
        apre_o[...] = a_pre
        ya = _dot(a_pre, wa_r[...]).astype(MXU)

        hn = _head_norm(_f32(h_r), gh_r[...])
        hbn = jnp.concatenate([xh * g for xh, _, g in hn], axis=1)
        zb = zb_r[...]
        b_pre = _sigmoid(o_r[...]) * hbn.astype(MXU) * (zb * _sigmoid(zb))
        bpre_o[...] = b_pre
        yb = _dot(b_pre, wb_r[...]).astype(MXU)
        ya_o[...] = ya
        yb_o[...] = yb

        mg = _sigmoid(ga_r[...]) * ya + _sigmoid(gb_r[...]) * yb
        mg_o[...] = mg
        x1 = x_r[...] + _dot(mg, wo_r[...])
        x1_o[...] = x1.astype(MXU)
        xn1 = (x1 * _rstd(x1) * gple_r[...]).astype(MXU)
        xn1_o[...] = xn1
        gt = _sigmoid(_dot(xn1, wpg_r[...]))
        e = _dot(p_r[...].astype(MXU), wp_r[...])
        x2 = x1 + gt * e
        r2 = _rstd(x2)
        xh2 = x2 * r2
        gf = gfin_r[...]
        diff = xh2 * gf - t_r[...]
        dy = diff * (1.0 / D)
        dx2 = _norm_bwd(dy, xh2, r2, gf)
        dx2_o[...] = dx2
        de_o[...] = (dx2 * gt).astype(MXU)
        dgp_o[...] = (dx2 * e * gt * (1.0 - gt)).astype(MXU)
        acc_o[0:1, :] += jnp.sum(dy * xh2, axis=0, keepdims=True)
        loss = 0.5 * jnp.sum(jnp.sum(diff * diff, axis=1, keepdims=True), axis=0, keepdims=True) * (1.0 / D)
        acc_o[1:2, :] += jnp.broadcast_to(loss, (1, D))

    row = lambda w, dt: (pl.BlockSpec((tm, w), lambda i: (i, 0)), jax.ShapeDtypeStruct((n, w), dt))
    outs = [row(D, MXU), row(VD, MXU), row(D, MXU), row(D, MXU), row(D, MXU), row(D, MXU),
            row(D, MXU), row(D, MXU), row(D, MXU), row(D, F32),
            (_const((8, D)), jax.ShapeDtypeStruct((8, D), F32))]
    return pl.pallas_call(
        body, name="tail_fwd", grid=(n // tm,),
        in_specs=_tail_in_specs(tm) + [
                  pl.BlockSpec((tm, VD), lambda i: (i, 0)),
                  pl.BlockSpec((tm, D), lambda i: (i, 0)),
                  pl.BlockSpec((tm, PLE), lambda i: (i, 0)),
                  pl.BlockSpec((tm, D), lambda i: (i, 0)),
                  _const((8, D)), _const((1, D)), _const((1, VD)), _const((1, D)), _const((1, D)),
                  _const((D, D)), _const((VD, D)), _const((D, D)), _const((D, D)), _const((PLE, D))],
        out_specs=[s for s, _ in outs],
        out_shape=[s for _, s in outs],
        compiler_params=_cparams("arbitrary"),
    )(proj, proj, proj, h, x, p, t, cw, cb, gh, gple, gfin, wa, wb, wo, wpg, wp)


def _tail_bwd(proj, h, dgp, dx2, x1, ya, yb, cw, cb, gh, gple, wpg, wo, wb, wa):
    n = x1.shape[0]
    tm = min(256, n)

    def body(tail_r, conv_r, halo_r, h_r,
             dgp_r, dx2_r, x1_r, ya_r, yb_r, cw_r, cb_r, gh_r, gple_r,
             wpg_r, wo_r, wb_r, wa_r,
             dproj_o, dcv_o, dh_o, dx1_o, dx1b_o, dya_o, dyb_o, acc_o):
        ba_r, za_r, o_r, zb_r, ga_r, gb_r, xa_r, ca_r, xah_r, cah_r = _tail_views(tail_r, conv_r, halo_r)
        i = pl.program_id(0)

        @pl.when(i == 0)
        def _():
            acc_o[...] = jnp.zeros_like(acc_o)

        dxn1 = _dot_nt(dgp_r[...], wpg_r[...])
        x1 = _f32(x1_r)
        r1 = _rstd(x1)
        xh1 = x1 * r1
        acc_o[0:1, 0:D] += jnp.sum(dxn1 * xh1, axis=0, keepdims=True)
        dx1 = dx2_r[...] + _norm_bwd(dxn1, xh1, r1, gple_r[...])
        dx1_o[...] = dx1
        dx1b = dx1.astype(MXU)
        dx1b_o[...] = dx1b

        dmg = _dot_nt(dx1b, wo_r[...]).astype(MXU)
        sga, dsga = _gate(ga_r[...])
        sgb, dsgb = _gate(gb_r[...])
        dya = dmg * sga
        dyb = dmg * sgb
        dya_o[...] = dya
        dyb_o[...] = dyb
        dproj_o[:, C_GA:C_GA + D] = dmg * ya_r[...] * dsga
        dproj_o[:, C_GB:C_GB + D] = dmg * yb_r[...] * dsgb

        db_pre = _dot_nt(dyb, wb_r[...]).astype(MXU)
        hn = _head_norm(_f32(h_r), gh_r[...])
        hbn = jnp.concatenate([xh * g for xh, _, g in hn], axis=1).astype(MXU)
        so, dso = _gate(o_r[...])
        zb = zb_r[...]
        szb, dszb = _gate(zb)
        sb = zb * szb
        t1 = db_pre * hbn
        dproj_o[:, C_O:C_O + VD] = t1 * (sb * dso)
        dproj_o[:, C_ZB:C_ZB + VD] = t1 * (so * (szb + zb * dszb))
        dhbn = (db_pre * (so * sb)).astype(F32)
        for j, (xh, rj, g) in enumerate(hn):
            dj = dhbn[:, j * DV:(j + 1) * DV]
            acc_o[1:2, j * DV:(j + 1) * DV] += jnp.sum(dj * xh, axis=0, keepdims=True)
            dh_o[:, j * DV:(j + 1) * DV] = _norm_bwd(dj, xh, rj, g).astype(MXU)

        da_pre = _dot_nt(dya, wa_r[...]).astype(MXU)
        u, u1, u2 = _conv_inputs(i, tm, xa_r, ca_r, xah_r, cah_r)
        cwv = cw_r[...]
        cv = cwv[0:1, :] * u2 + cwv[1:2, :] * u1 + cwv[2:3, :] * u + cb_r[...]
        za = za_r[...]
        sza, dsza = _gate(za)
        sa = za * sza
        ba = ba_r[...]
        t2 = da_pre * cv.astype(MXU)
        dproj_o[:, C_BA:C_BA + D] = t2 * sa
        dproj_o[:, C_ZA:C_ZA + D] = t2 * (ba * (sza + za * dsza))
        dcv_b = da_pre * (ba * sa)
        dcv_o[...] = dcv_b
        dcv = dcv_b.astype(F32)
        acc_o[2:3, 0:D] += jnp.sum(dcv, axis=0, keepdims=True)
        acc_o[3:4, 0:D] += jnp.sum(dcv * u2, axis=0, keepdims=True)
        acc_o[4:5, 0:D] += jnp.sum(dcv * u1, axis=0, keepdims=True)
        acc_o[5:6, 0:D] += jnp.sum(dcv * u, axis=0, keepdims=True)

    row = lambda w, dt: (pl.BlockSpec((tm, w), lambda i: (i, 0)), jax.ShapeDtypeStruct((n, w), dt))
    outs = [(pl.BlockSpec((tm, TAIL_W), lambda i: (i, 0)), jax.ShapeDtypeStruct((n, NMAIN), MXU)),
            row(D, MXU), row(VD, MXU), row(D, F32), row(D, MXU), row(D, MXU), row(D, MXU),
            (_const((8, VD)), jax.ShapeDtypeStruct((8, VD), F32))]
    rowin = lambda w: pl.BlockSpec((tm, w), lambda i: (i, 0))
    return pl.pallas_call(
        body, name="tail_bwd", grid=(n // tm,),
        in_specs=_tail_in_specs(tm) + [
                  rowin(VD), rowin(D), rowin(D), rowin(D), rowin(D), rowin(D),
                  _const((8, D)), _const((1, D)), _const((1, VD)), _const((1, D)),
                  _const((D, D)), _const((D, D)), _const((VD, D)), _const((D, D))],
        out_specs=[s for s, _ in outs],
        out_shape=[s for _, s in outs],
        compiler_params=_cparams("arbitrary"),
    )(proj, proj, proj, h, dgp, dx2, x1, ya, yb, cw, cb, gh, gple, wpg, wo, wb, wa)


def _conv_bwd(proj, dcv, cw, dproj):
    n = dcv.shape[0]
    tm = min(1024, n)
    nt = n // tm

    def body(xa_r, ca_r, dcv_r, nxt_r, cw_r, dproj_in, dxc_o):
        del dproj_in
        i = pl.program_id(0)
        dcv_v = _f32(dcv_r)
        nxt = jnp.where(i < nt - 1, _f32(nxt_r), 0.0)
        rid = lax.broadcasted_iota(jnp.int32, dcv_v.shape, 0)
        d1 = jnp.where(rid == tm - 1, nxt[0:1, :], pltpu.roll(dcv_v, tm - 1, 0))
        d2 = jnp.where(rid == tm - 2, nxt[0:1, :],
                       jnp.where(rid == tm - 1, nxt[1:2, :], pltpu.roll(dcv_v, tm - 2, 0)))
        cwv = cw_r[...]
        du = cwv[2:3, :] * dcv_v + cwv[1:2, :] * d1 + cwv[0:1, :] * d2
        dxc_o[:, 0:D] = (du * _f32(ca_r)).astype(MXU)
        dxc_o[:, D:2 * D] = (du * _f32(xa_r)).astype(MXU)

    return pl.pallas_call(
        body, name="conv_bwd", grid=(nt,),
        in_specs=[_proj_spec(tm, C_XA, 1024), _proj_spec(tm, C_CA, 1024),
                  pl.BlockSpec((tm, D), lambda i: (i, 0)),
                  pl.BlockSpec((8, D), lambda i: (jnp.minimum((i + 1) * (tm // 8), n // 8 - 1), 0)),
                  _const((8, D)), ANY],
        out_specs=pl.BlockSpec((tm, CONV_W), lambda i: (i, C_XA // CONV_W)),
        out_shape=jax.ShapeDtypeStruct(dproj.shape, dproj.dtype),
        input_output_aliases={5: 0},
        compiler_params=_cparams("arbitrary"),
    )(proj, proj, dcv, dcv, cw, dproj)


def _position():
    return lax.axis_index("x"), lax.axis_index("y"), lax.axis_index("c")


def _flip(v, bit):
    return 1 - v if bit else v


def _part_start(core, stride, tiled):
    return pl.multiple_of(core * stride, stride) if tiled else core * stride


def _scatter_copies(srcs, dsts, strides, send_sems, recv_sems, local_sems):
    x, y, c = _position()
    me = 4 * x + 2 * y + c
    na = len(srcs)
    copies = []
    for r in range(N_DEV):
        px, py, pc = _flip(x, (r >> 2) & 1), _flip(y, (r >> 1) & 1), _flip(c, r & 1)
        for a in range(na):
            rows = dsts[a].shape[1]
            src = srcs[a].at[2 * px + py, pl.ds(_part_start(pc, strides[a], len(dsts[a].shape) == 3), rows)]
            dst = dsts[a].at[me]
            if r == 0:
                copies.append(pltpu.make_async_copy(src, dst, local_sems.at[a]))
            else:
                k = (r - 1) * na + a
                copies.append(pltpu.make_async_remote_copy(
                    src_ref=src, dst_ref=dst, send_sem=send_sems.at[k], recv_sem=recv_sems.at[k],
                    device_id=(px, py, pc), device_id_type=MESH))
    return copies


def _input_grad(dproj, dgates, w_t, wg_t, x, dx1, g_mix, g_in, g_rest):
    n = x.shape[0]
    tm, tk = min(1024, n), 2048
    nk = NMAIN // tk
    nt = n // tm

    def body(dp_r, dg_r, w_r, wg_r, x_r, dx1_r, g_r, gin, grest,
             gx_o, acc_o, oin, orest, acc, send_sems, recv_sems, local_sems):
        i = pl.program_id(0)
        kk = pl.program_id(1)
        copies = _scatter_copies((gin, grest), (oin, orest), strides, send_sems, recv_sems, local_sems)

        @pl.when((i == 0) & (kk == 0))
        def _():
            acc_o[...] = jnp.zeros_like(acc_o)
            for cp in copies:
                cp.start()

        @pl.when(kk == 0)
        def _():
            acc[...] = _dot(dg_r[...].astype(MXU), wg_r[...])

        acc[...] += _dot(dp_r[...], w_r[...])

        @pl.when(kk == nk - 1)
        def _():
            dhn = acc[...]
            xv = x_r[...]
            r0 = _rstd(xv)
            xh = xv * r0
            acc_o[0:1, :] += jnp.sum(dhn * xh, axis=0, keepdims=True)
            gx_o[...] = dx1_r[...] + _norm_bwd(dhn, xh, r0, g_r[...])

        @pl.when((i == nt - 1) & (kk == nk - 1))
        def _():
            for cp in copies:
                cp.wait()

    nrem = 2 * (N_DEV - 1)
    r_in, r_rest = HALF_SLABS, g_rest.shape[1] // 2
    strides = (HALF_START, r_rest)
    return pl.pallas_call(
        body, name="input_grad", grid=(nt, nk),
        in_specs=[pl.BlockSpec((tm, tk), lambda i, kk: (i, kk)),
                  pl.BlockSpec((tm, GATE_W), lambda i, kk: (i, 0)),
                  pl.BlockSpec((tk, D), lambda i, kk: (kk, 0)),
                  pl.BlockSpec((GATE_W, D), lambda i, kk: (0, 0)),
                  pl.BlockSpec((tm, D), lambda i, kk: (i, 0)),
                  pl.BlockSpec((tm, D), lambda i, kk: (i, 0)),
                  pl.BlockSpec((1, D), lambda i, kk: (0, 0)),
                  ANY, ANY],
        out_specs=[pl.BlockSpec((tm, D), lambda i, kk: (i, 0)),
                   pl.BlockSpec((8, D), lambda i, kk: (0, 0)),
                   ANY, ANY],
        out_shape=[jax.ShapeDtypeStruct((n, D), F32), jax.ShapeDtypeStruct((8, D), F32),
                   jax.ShapeDtypeStruct((N_DEV, r_in) + g_in.shape[2:], g_in.dtype),
                   jax.ShapeDtypeStruct((N_DEV, r_rest) + g_rest.shape[2:], g_rest.dtype)],
        scratch_shapes=[pltpu.VMEM((tm, D), F32),
                        pltpu.SemaphoreType.DMA((nrem,)), pltpu.SemaphoreType.DMA((nrem,)),
                        pltpu.SemaphoreType.DMA((2,))],
        compiler_params=_cparams("arbitrary", "arbitrary"),
    )(dproj, dgates, w_t, wg_t, x, dx1, g_mix, g_in, g_rest)


def _pack_small(acc_f, acc_b, acc_x, gsum):
    def body(f_r, b_r, x_r, s_r, o_r):
        o_r[...] = jnp.zeros_like(o_r)
        o_r[0:1, :] = x_r[0:1, :]
        o_r[1:2, :] = b_r[2:3, 0:D]
        o_r[2:3, :] = b_r[1:2, 0:D]
        o_r[3:4, :] = b_r[1:2, D:2 * D]
        o_r[4:5, :] = b_r[0:1, 0:D]
        o_r[5:6, :] = f_r[0:1, :]
        o_r[6:9, :] = b_r[3:6, 0:D]
        lane = lax.broadcasted_iota(jnp.int32, (1, D), 1)
        o_r[9:10, :] = jnp.where(lane < 2 * NH, s_r[0:1, :], jnp.where(lane == 2 * NH, f_r[1:2, :], 0.0))

    return pl.pallas_call(
        body, name="pack_small",
        out_shape=jax.ShapeDtypeStruct((SMALL_ROWS, D), F32),
    )(acc_f, acc_b, acc_x, gsum)


def _sum_slots(r, name, tr=64, half=None):
    s, rows, w = r.shape
    tr = min(tr, rows)
    assert rows % tr == 0
    nt = rows // tr

    def body(*refs):
        r_ref, o_ref = refs[-2:]
        tot = r_ref[0].astype(F32)
        for k in range(1, s):
            tot = tot + r_ref[k].astype(F32)
        o_ref[...] = tot

    if half is None:
        return pl.pallas_call(
            body, name=name, grid=(nt,),
            in_specs=[pl.BlockSpec((s, tr, w), lambda i: (0, i, 0))],
            out_specs=pl.BlockSpec((tr, w), lambda i: (i, 0)),
            out_shape=jax.ShapeDtypeStruct((rows, w), F32),
            compiler_params=_cparams("arbitrary"),
        )(r)
    return pl.pallas_call(
        body, name=name,
        grid_spec=pltpu.PrefetchScalarGridSpec(
            num_scalar_prefetch=1, grid=(nt,),
            in_specs=[pl.BlockSpec((s, tr, w), lambda i, hf: (0, i, 0))],
            out_specs=pl.BlockSpec((tr, w), lambda i, hf: (hf[0] * nt + i, 0))),
        out_shape=jax.ShapeDtypeStruct((2 * rows, w), F32),
        compiler_params=_cparams("arbitrary"),
    )(half, r)


def _cast_slabs(a, tr=163):
    rows = a.shape[0]
    assert rows % tr == 0

    def body(a_ref, o_ref):
        o_ref[...] = a_ref[...].astype(MXU)

    spec = pl.BlockSpec((tr,) + a.shape[1:], lambda i: (i, 0, 0))
    return pl.pallas_call(
        body, name="cast_w_in", grid=(rows // tr,), in_specs=[spec], out_specs=spec,
        out_shape=jax.ShapeDtypeStruct(a.shape, MXU), compiler_params=_cparams("arbitrary"),
    )(a)


def _sum_slabs(r, name, tr=69):
    s, rows = r.shape[:2]
    assert rows % tr == 0

    def body(r_ref, o_ref):
        tot = r_ref[0].astype(F32)
        for k in range(1, s):
            tot = tot + r_ref[k].astype(F32)
        o_ref[...] = tot

    return pl.pallas_call(
        body, name=name, grid=(rows // tr,),
        in_specs=[pl.BlockSpec((s, tr) + r.shape[2:], lambda i: (0, i, 0, 0))],
        out_specs=pl.BlockSpec((tr,) + r.shape[2:], lambda i: (i, 0, 0)),
        out_shape=jax.ShapeDtypeStruct(r.shape[1:], F32),
        compiler_params=_cparams("arbitrary"),
    )(r)


def _adamw_body(w_r, g_r, m_r, v_r, d_o, m_o, v_o):
    c1 = 1.0 - ADAM_B1 ** ADAM_STEP
    c2 = 1.0 - ADAM_B2 ** ADAM_STEP
    gv = g_r[...]
    mn = ADAM_B1 * m_r[...] + (1.0 - ADAM_B1) * gv
    vn = ADAM_B2 * v_r[...] + (1.0 - ADAM_B2) * (gv * gv)
    m_o[...] = mn
    v_o[...] = vn
    d_o[...] = -ADAM_LR * ((mn / c1) / (jnp.sqrt(vn / c2) + ADAM_EPS) + ADAM_WD * w_r[...])


def _adamw_slabs(w, g, m, v, name, tr):
    rows = w.shape[0]
    assert rows % tr == 0 and w.shape == g.shape

    def body(*refs):
        _adamw_body(*refs)

    spec = pl.BlockSpec((tr,) + w.shape[1:], lambda i: (i, 0, 0))
    shp = jax.ShapeDtypeStruct(w.shape, F32)
    return pl.pallas_call(
        body, name=name, grid=(rows // tr,),
        in_specs=[spec] * 4, out_specs=[spec] * 3, out_shape=[shp] * 3,
        compiler_params=_cparams("arbitrary"),
    )(w, g, m, v)


def _adamw(w, g, m, v, name, g_row0=0, tr=256):
    lead = w.ndim == 3
    rows, cols = w.shape[-2:]
    tr = min(tr, rows)
    assert rows % tr == 0 and g_row0 % tr == 0 and g.shape[1] == cols

    def body(*refs):
        _adamw_body(*refs)

    if lead:
        spec = pl.BlockSpec((None, tr, cols), lambda i: (0, i, 0))
    else:
        spec = pl.BlockSpec((tr, cols), lambda i: (i, 0))
    g_spec = pl.BlockSpec((tr, cols), lambda i: (g_row0 // tr + i, 0))
    shp = jax.ShapeDtypeStruct(w.shape, F32)
    return pl.pallas_call(
        body, name=name, grid=(rows // tr,),
        in_specs=[spec, g_spec, spec, spec], out_specs=[spec] * 3, out_shape=[shp] * 3,
        compiler_params=_cparams("arbitrary"),
    )(w, g, m, v)


def _allgather_chips(halved, whole):
    arrs = [a for a, _ in halved] + list(whole)
    parts = [p for _, p in halved]
    nh, na = len(halved), len(arrs)
    n_ici = 3 * na

    def body(*refs):
        ins, outs = refs[:na], refs[na:2 * na]
        send_sems, recv_sems = refs[2 * na:]
        x, y, c = _position()
        mine = 2 * x + y

        def rows(a, core, stride, size, off=0):
            start = core * stride + off
            return pl.ds(pl.multiple_of(start, 16) if len(ins[a].shape) == 2 else start, size)

        def remote(src, dst, k, to):
            return pltpu.make_async_remote_copy(src_ref=src, dst_ref=dst, send_sem=send_sems.at[k],
                                                recv_sem=recv_sems.at[k], device_id=to, device_id_type=MESH)

        nbr_x, nbr_y, diag = (_flip(x, 1), y), (x, _flip(y, 1)), (_flip(x, 1), _flip(y, 1))
        chip = lambda p: 2 * p[0] + p[1]
        sends, passed = [], []
        from_x, from_y, fwd_x, fwd_y, pass_x, pass_y, pass_d = {}, {}, {}, {}, {}, {}, {}
        for a in range(na):
            if a < nh:
                s1, n1, s2, n2 = parts[a]
                h1 = (n1 + 1) // 2
                part = rows(a, c, s1, n1)
                from_x[a] = remote(ins[a].at[part], outs[a].at[mine, part], a, (*nbr_x, c))
                from_y[a] = remote(ins[a].at[part], outs[a].at[mine, part], na + a, (*nbr_y, c))
                lo = outs[a].at[chip(nbr_y), rows(a, c, s1, h1)]
                fwd_x[a] = remote(lo, lo, n_ici + a, (*nbr_x, c))
                hi = outs[a].at[chip(nbr_x), rows(a, c, s1, n1 - h1, h1)]
                fwd_y[a] = remote(hi, hi, n_ici + nh + a, (*nbr_y, c))
                for j, (who, table) in enumerate(((nbr_x, pass_x), (nbr_y, pass_y), (diag, pass_d))):
                    landed = outs[a].at[chip(who), rows(a, c, s2, n2)]
                    table[a] = remote(landed, landed, n_ici + (2 + j) * nh + a, (x, y, 1 - c))
                sends += [from_x[a], from_y[a]]
            else:
                sends += [remote(ins[a], outs[a].at[mine], j * na + a, (*who, c))
                          for j, who in enumerate((nbr_x, nbr_y, diag))]
        for cp in sends:
            cp.start()
        for a in range(nh):
            from_y[a].wait_recv()
            fwd_x[a].start()
            pass_y[a].start()
        for a in range(nh):
            from_x[a].wait_recv()
            fwd_y[a].start()
            pass_x[a].start()
        for a in range(nh):
            fwd_x[a].wait_recv()
            fwd_y[a].wait_recv()
            pass_d[a].start()
        for a in range(nh, na):
            for cp in sends[2 * nh + 3 * (a - nh):2 * nh + 3 * (a - nh) + 3]:
                cp.wait_recv()
        every = sends + [t[a] for t in (fwd_x, fwd_y, pass_x, pass_y, pass_d) for a in range(nh)]
        for a in range(nh):
            for t in (pass_x, pass_y, pass_d):
                t[a].wait_recv()
        for cp in every:
            cp.wait_send()

    nsem = n_ici + 5 * nh
    outs = pl.pallas_call(
        body, name="allgather_weights",
        in_specs=[ANY] * na, out_specs=[ANY] * na,
        out_shape=[jax.ShapeDtypeStruct((N_CHIPS,) + a.shape, a.dtype) for a in arrs],
        scratch_shapes=[pltpu.SemaphoreType.DMA((nsem,)), pltpu.SemaphoreType.DMA((nsem,))],
    )(*arrs)
    chip = 2 * lax.axis_index("x") + lax.axis_index("y")
    return [lax.dynamic_update_slice(o, a[None], (chip,) + (0,) * a.ndim) for o, a in zip(outs, arrs)]


def _join_halves(piece, full_rest, small):
    def body(pc_ref, rest_in, sm, oin, orest, osm, send_sems, recv_sems, local_sem):
        del rest_in
        x, y, c = _position()
        me = 4 * x + 2 * y + c
        sib = (x, y, 1 - c)
        rows = orest.shape[0] // 2
        mine = orest.at[pl.ds(pl.multiple_of(c * rows, rows), rows)]
        copies = [
            pltpu.make_async_remote_copy(
                src_ref=pc_ref.at[pl.ds(c, HALF_START)], dst_ref=oin.at[pl.ds(c * HALF_SLABS, HALF_START)],
                send_sem=send_sems.at[0], recv_sem=recv_sems.at[0], device_id=sib, device_id_type=MESH),
            pltpu.make_async_remote_copy(
                src_ref=mine, dst_ref=mine, send_sem=send_sems.at[1], recv_sem=recv_sems.at[1],
                device_id=sib, device_id_type=MESH)]
        na = 2
        copies.append(pltpu.make_async_copy(sm, osm.at[me], local_sem))
        for r in range(1, N_DEV):
            to = (_flip(x, (r >> 2) & 1), _flip(y, (r >> 1) & 1), _flip(c, r & 1))
            copies.append(pltpu.make_async_remote_copy(
                src_ref=sm, dst_ref=osm.at[me], send_sem=send_sems.at[na + r - 1],
                recv_sem=recv_sems.at[na + r - 1], device_id=to, device_id_type=MESH))
        for cp in copies:
            cp.start()
        for cp in copies:
            cp.wait()

    nsem = 2 + N_DEV - 1
    return pl.pallas_call(
        body, name="join_halves",
        in_specs=[ANY] * 3, out_specs=[ANY] * 3,
        out_shape=[jax.ShapeDtypeStruct((SLABS_CHIP,) + piece.shape[1:], piece.dtype),
                   jax.ShapeDtypeStruct(full_rest.shape, full_rest.dtype),
                   jax.ShapeDtypeStruct((N_DEV,) + small.shape, small.dtype)],
        scratch_shapes=[pltpu.SemaphoreType.DMA((nsem,)), pltpu.SemaphoreType.DMA((nsem,)),
                        pltpu.SemaphoreType.DMA],
        input_output_aliases={1: 1},
    )(piece, full_rest, small)


def _to_internal(w_slabs):
    order = sorted(SEGMENTS, key=lambda s: s[2])
    main = jnp.concatenate([w_slabs[g0 // 2:(g0 + w) // 2] for g0, w, _ in order], axis=0)
    gate = w_slabs[GATE_COL // 2:GATE_COL // 2 + NH].reshape(2 * NH, D)
    return main.reshape(NMAIN, D), jnp.pad(gate, ((0, GATE_W - 2 * NH), (0, 0)))


def _to_global(main_t, gate_t):
    main = main_t.reshape(NMAIN // 2, 16, D // 8)
    parts = sorted([(g0, main[i0 // 2:(i0 + w) // 2]) for g0, w, i0 in SEGMENTS]
                   + [(GATE_COL, gate_t[0:2 * NH].reshape(NH, 16, D // 8))], key=lambda s: s[0])
    return jnp.concatenate([p for _, p in parts], axis=0)


def _pad_rows(a, rows=8):
    return jnp.pad(a, ((0, rows - a.shape[0]), (0, 0)))


def kernel(x, p, g_mix, w_in, conv_w, conv_b, w_a_out, b_gates, g_head, w_b_out, w_o, g_ple, w_ple_gate, w_ple, g_final, loss_target, m_g_mix, m_w_in, m_conv_w, m_conv_b, m_w_a_out, m_b_gates, m_g_head, m_w_b_out, m_w_o, m_g_ple, m_w_ple_gate, m_w_ple, m_g_final, v_g_mix, v_w_in, v_conv_w, v_conv_b, v_w_a_out, v_b_gates, v_g_head, v_w_b_out, v_w_o, v_g_ple, v_w_ple_gate, v_w_ple, v_g_final):
    chip = 2 * lax.axis_index("x") + lax.axis_index("y")
    xs, ps, ts = x[0], p[0, 0], loss_target[0]
    g_fin = g_final.reshape(1, D)

    w_slabs = _cast_slabs(jnp.transpose(w_in, (2, 0, 1)).reshape(SLABS_CHIP, 16, D // 8))
    rest = [w[0].astype(MXU) for w in (w_a_out, w_b_out, w_o, w_ple_gate, w_ple)]
    even = lambda a: (a.shape[0] // 2,) * 4
    g_win, g_wa, g_wb, g_wo, g_wpg, g_wp, g_cw = _allgather_chips(
        [(w_slabs, (HALF_START, HALF_SLABS, HALF_SLABS, HALF_START))] + [(a, even(a)) for a in rest],
        [_pad_rows(conv_w[0])])
    w_t, wg_t = _to_internal(g_win.reshape(N_IN // 2, 16, D // 8))
    wa, wb, wo, wpg = [g.reshape(-1, D) for g in (g_wa, g_wb, g_wo, g_wpg)]
    wp = jnp.transpose(g_wp, (1, 0, 2)).reshape(PLE, D)
    cw = jnp.transpose(g_cw, (1, 0, 2)).reshape(8, D)

    bias = jnp.pad(b_gates, ((0, 0), (0, GATE_W - 2 * NH)))
    proj, hn, gates = _proj(xs, g_mix, w_t, wg_t)
    h, stats, cs, ns, ms = _mlstm_fwd(proj, gates, bias)
    (a_pre, b_pre, mg, xn1, de, dgp, ya, yb, x1, dx2, acc_f) = _tail_fwd(
        proj, h, xs, ps, ts, cw, conv_b, g_head, g_ple, g_fin, wa, wb, wo, wpg, wp)
    dproj, dcv, dh, dx1, dx1b, dya, dyb, acc_b = _tail_bwd(
        proj, h, dgp, dx2, x1, ya, yb, cw, conv_b, g_head, g_ple, wpg, wo, wb, wa)
    dproj = _conv_bwd(proj, dcv, cw, dproj)
    dproj, dgates, gsum = _mlstm_bwd(proj, gates, bias, h, dh, stats, cs, ns, ms, dproj)
    d_main = _matmul_tn(dproj, hn, "dw_in", out_dtype=WIRE, tk=4096)
    d_gate = _matmul_tn(dgates, hn, "dw_gate", out_dtype=WIRE)
    d_wa = _matmul_tn(a_pre, dya, "dw_a_out", out_dtype=WIRE)
    d_wb = _matmul_tn(b_pre, dyb, "dw_b_out", out_dtype=WIRE)
    d_wo = _matmul_tn(mg, dx1b, "dw_o", out_dtype=WIRE)
    d_wpg = _matmul_tn(xn1, dgp, "dw_ple_gate", out_dtype=WIRE)
    d_wp = _matmul_tn(ps, de, "dw_ple", out_dtype=WIRE)

    g_in = _to_global(d_main, d_gate).reshape(N_CHIPS, SLABS_CHIP, 16, D // 8)
    d_wp_c = jnp.transpose(d_wp.reshape(PLE, N_CHIPS, PLE), (1, 0, 2)).reshape(N_CHIPS, PACK_ROWS[4], D)
    g_rest = jnp.concatenate(
        [d_wa.reshape(N_CHIPS, -1, D), d_wb.reshape(N_CHIPS, -1, D), d_wo.reshape(N_CHIPS, -1, D),
         d_wpg.reshape(N_CHIPS, -1, D), d_wp_c], axis=1)
    grad_x, acc_x, r_in, r_rest = _input_grad(dproj, dgates, w_t, wg_t, xs, dx1, g_mix, g_in, g_rest)
    small = _pack_small(acc_f, acc_b, acc_x, gsum)
    core = lax.axis_index("c").astype(jnp.int32)
    piece = _sum_slabs(r_in, "sum_w_in")
    gw_in, gw_rest, r_small = _join_halves(
        piece, _sum_slots(r_rest, "sum_rest", tr=96, half=core.reshape(1)), small)
    gw_in = lax.dynamic_update_slice(gw_in, piece, (core * HALF_START, 0, 0)).reshape(COLS_CHIP, 8, D // 8)
    gs = _sum_slots(r_small, "sum_small", tr=SMALL_ROWS)

    big = []
    row0 = 0
    for name, w, m, v in (("w_a_out", w_a_out, m_w_a_out, v_w_a_out), ("w_b_out", w_b_out, m_w_b_out, v_w_b_out),
                          ("w_o", w_o, m_w_o, v_w_o), ("w_ple_gate", w_ple_gate, m_w_ple_gate, v_w_ple_gate)):
        big.append((name, w, m, v, gw_rest, row0))
        row0 += w.shape[1]
    g_wp = gw_rest[row0:row0 + PACK_ROWS[4]].reshape(PLE, PLE)
    big.append(("w_ple", w_ple, m_w_ple, v_w_ple, g_wp, 0))
    upd = {name: _adamw(w, g, m, v, "adamw_" + name, g_row0=r0) for name, w, m, v, g, r0 in big}
    g_big = {name: (g if name == "w_ple" else g[r0:r0 + w.shape[1]])[None] for name, w, m, v, g, r0 in big}
    slabs = lambda a: jnp.transpose(a, (2, 0, 1)).reshape(COLS_CHIP, 8, D // 8)
    unslab = lambda a: jnp.transpose(a, (1, 2, 0)).reshape(1, D, COLS_CHIP)
    upd["w_in"] = [unslab(u) for u in _adamw_slabs(
        slabs(w_in), gw_in, slabs(m_w_in), slabs(v_w_in), "adamw_w_in", tr=326)]
    g_big["w_in"] = unslab(gw_in)

    lane = lax.broadcasted_iota(jnp.int32, (1, D), 1)
    g_small = jnp.concatenate([gs[0:6], jnp.where(lane < 2 * NH, gs[9:10], 0.0), jnp.zeros((1, D), F32)], axis=0)

    def small_pack(gm, cb_, bg, gh, gp, gf):
        return jnp.concatenate([gm, cb_, gh.reshape(2, D), gp, gf.reshape(1, D),
                                jnp.pad(bg, ((0, 0), (0, D - 2 * NH))), jnp.zeros((1, D), F32)], axis=0)

    ws = small_pack(g_mix, conv_b, b_gates, g_head, g_ple, g_final)
    ms_ = small_pack(m_g_mix, m_conv_b, m_b_gates, m_g_head, m_g_ple, m_g_final)
    vs = small_pack(v_g_mix, v_conv_b, v_b_gates, v_g_head, v_g_ple, v_g_final)
    upd_s = _adamw(ws, g_small, ms_, vs, "adamw_small")
    g_cw_mine = _pad_rows(lax.dynamic_slice(gs[6:9], (0, chip * PLE), (3, PLE)))
    upd_c = _adamw(_pad_rows(conv_w[0]), g_cw_mine, _pad_rows(m_conv_w[0]), _pad_rows(v_conv_w[0]),
                   "adamw_conv_w")

    def leaves(bigs, sm, cwv):
        return [sm[0:1], bigs["w_in"], cwv[0:3][None], sm[1:2], bigs["w_a_out"], sm[6:7, 0:2 * NH],
                sm[2:4].reshape(1, VD), bigs["w_b_out"], bigs["w_o"], sm[4:5], bigs["w_ple_gate"],
                bigs["w_ple"], sm[5]]

    loss = gs[9, 2 * NH]
    outs = [loss, grad_x[None]] + leaves(g_big, g_small, g_cw_mine)
    for k in range(3):
        outs += leaves({name: u[k] for name, u in upd.items()}, upd_s[k], upd_c[k])
    return tuple(outs)
```

```python
import jax
import jax.numpy as jnp
from jax import lax
from jax.experimental import pallas as pl
from jax.experimental.pallas import tpu as pltpu

F32 = jnp.float32
MXU = jnp.bfloat16
WIRE = jnp.bfloat16

D = 1024
NH, DK, DV = 4, 256, 512
VD = NH * DV
PLE = 256
LCH = 256
EPS = 1e-6
N_IN = 14344
NMAIN = 14336
GATE_W = 128
N_CHIPS, N_DEV = 4, 8

C_BA, C_ZA, C_O, C_ZB, C_GA, C_GB = 0, 1024, 2048, 4096, 6144, 7168
QK = NH * DK
C_Q, C_K, C_V, C_XA, C_CA = 8192, 9216, 10240, 12288, 13312
QKV_W = 2 * QK + VD
TAIL_W = 8192
CONV_W = 2048
SEGMENTS = (
    (0, 1024, C_XA), (1024, 1024, C_BA), (2048, 1024, C_CA), (3072, 1024, C_ZA),
    (4096, QKV_W, C_Q), (8192, 4096, C_O), (12296, 2048, C_GA),
)
GATE_COL = 12288

COLS_CHIP = N_IN // N_CHIPS
SLABS_CHIP = COLS_CHIP // 2
HALF_START = SLABS_CHIP // 2
HALF_SLABS = SLABS_CHIP - HALF_START

PACK_ROWS = (256, 512, 256, 256, 64)
SMALL_ROWS = 16

ADAM_LR, ADAM_B1, ADAM_B2, ADAM_EPS, ADAM_WD, ADAM_STEP = 0.001, 0.9, 0.999, 1e-08, 0.01, 10

VMEM_LIMIT = 56 * 1024 * 1024
MESH = pl.DeviceIdType.MESH
ANY = pl.BlockSpec(memory_space=pl.ANY)


def _cparams(*sem):
    return pltpu.CompilerParams(dimension_semantics=sem, vmem_limit_bytes=VMEM_LIMIT)


def _dot(a, b):
    return jnp.dot(a, b, preferred_element_type=F32)


def _dot_nt(a, b):
    return lax.dot_general(a, b, (((1,), (1,)), ((), ())), preferred_element_type=F32)


def _dot_tn(a, b):
    return lax.dot_general(a, b, (((0,), (0,)), ((), ())), preferred_element_type=F32)


def _sigmoid(x):
    return 1.0 / (1.0 + jnp.exp(-x))


def _logsig(x):
    return jnp.minimum(x, 0.0) - jnp.log(1.0 + jnp.exp(-jnp.abs(x)))


GATE_FLOOR = -80.0


def _gate(v):
    e = jnp.exp(-jnp.maximum(v, GATE_FLOOR))
    s = 1.0 / (1.0 + e)
    return s, e * s * s


def _rstd(x):
    return lax.rsqrt(jnp.mean(x * x, axis=-1, keepdims=True) + EPS)


def _norm_bwd(dy, xhat, r, g):
    dxh = dy * g
    return r * (dxh - xhat * jnp.mean(dxh * xhat, axis=-1, keepdims=True))


def _f32(ref):
    return ref[...].astype(F32)


def _proj(x, g_mix, w_t, wg_t, tm=1024, tn=3584):
    n = x.shape[0]
    m = w_t.shape[0]
    tm = min(tm, n)

    def body(x_ref, g_ref, b_ref, wg_ref, o_ref, hn_ref, gate_ref):
        @pl.when(pl.program_id(1) == 0)
        def _():
            xv = x_ref[...]
            hn = (xv * _rstd(xv) * g_ref[...]).astype(MXU)
            hn_ref[...] = hn
            gate_ref[...] = _dot_nt(hn, wg_ref[...])

        o_ref[...] = _dot_nt(hn_ref[...], b_ref[...]).astype(MXU)

    return pl.pallas_call(
        body, name="proj", grid=(n // tm, m // tn),
        in_specs=[pl.BlockSpec((tm, D), lambda i, j: (i, 0)),
                  pl.BlockSpec((1, D), lambda i, j: (0, 0)),
                  pl.BlockSpec((tn, D), lambda i, j: (j, 0)),
                  pl.BlockSpec((GATE_W, D), lambda i, j: (0, 0))],
        out_specs=[pl.BlockSpec((tm, tn), lambda i, j: (i, j)),
                   pl.BlockSpec((tm, D), lambda i, j: (i, 0)),
                   pl.BlockSpec((tm, GATE_W), lambda i, j: (i, 0))],
        out_shape=[jax.ShapeDtypeStruct((n, m), MXU), jax.ShapeDtypeStruct((n, D), MXU),
                   jax.ShapeDtypeStruct((n, GATE_W), F32)],
        compiler_params=_cparams("arbitrary", "arbitrary"),
    )(x, g_mix, w_t, wg_t)


def _matmul_tn(a, b, name, out_dtype=F32, ta=1024, tb=1024, tk=4096):
    n, ka = a.shape
    kb = b.shape[1]
    ta, tb, tk = min(ta, ka), min(tb, kb), min(tk, n)
    nk = n // tk

    def body(a_ref, b_ref, o_ref, acc):
        kk = pl.program_id(2)

        @pl.when(kk == 0)
        def _():
            acc[...] = jnp.zeros_like(acc)

        acc[...] += _dot_tn(a_ref[...].astype(MXU), b_ref[...].astype(MXU))

        @pl.when(kk == nk - 1)
        def _():
            o_ref[...] = acc[...].astype(out_dtype)

    return pl.pallas_call(
        body, name=name, grid=(ka // ta, kb // tb, nk),
        in_specs=[pl.BlockSpec((tk, ta), lambda i, j, kk: (kk, i)),
                  pl.BlockSpec((tk, tb), lambda i, j, kk: (kk, j))],
        out_specs=pl.BlockSpec((ta, tb), lambda i, j, kk: (i, j)),
        out_shape=jax.ShapeDtypeStruct((ka, kb), out_dtype),
        scratch_shapes=[pltpu.VMEM((ta, tb), F32)],
        compiler_params=_cparams("arbitrary", "arbitrary", "arbitrary"),
    )(a, b)


def _sub_row(block_t, j):
    sub = lax.broadcasted_iota(jnp.int32, block_t.shape, 0)
    return jnp.sum(jnp.where(sub == j, block_t, 0.0), axis=0, keepdims=True)


def _chunk_cumsum(g):
    n = g.shape[0]
    r = lax.broadcasted_iota(jnp.int32, (n, n), 0)
    c = lax.broadcasted_iota(jnp.int32, (n, n), 1)
    return jnp.dot(jnp.where(r >= c, 1.0, 0.0), _logsig(g), precision=lax.Precision.HIGHEST,
                   preferred_element_type=F32)


def _chunk_decay(li_col, li_row, b_col, b_row, m_prev):
    n = li_col.shape[0]
    r = lax.broadcasted_iota(jnp.int32, (n, n), 0)
    c = lax.broadcasted_iota(jnp.int32, (n, n), 1)
    tri = r >= c
    b_last = b_col[n - 1:n, :]
    rr = li_row - b_row
    a_col = b_col + m_prev
    g_col = b_last - b_col + li_col
    m_new = jnp.maximum(b_last + m_prev, jnp.max(g_col, axis=0, keepdims=True))
    w_col = jnp.exp(g_col - m_new)
    decay = jnp.exp(b_last + m_prev - m_new)
    return tri, rr, a_col, m_new, w_col, decay


def _qkv_specs(row_of):
    q_spec = pl.BlockSpec((LCH, QK), lambda c: (row_of(c), C_Q // QK))
    k_spec = pl.BlockSpec((LCH, QK), lambda c: (row_of(c), C_K // QK))
    v_spec = pl.BlockSpec((LCH, VD), lambda c: (row_of(c), C_V // VD))
    return q_spec, k_spec, v_spec


def _state_specs(row_of):
    return [pl.BlockSpec((NH, None, DK, DV), lambda c: (0, row_of(c), 0, 0)),
            pl.BlockSpec((NH, None, 1, DK), lambda c: (0, row_of(c), 0, 0)),
            pl.BlockSpec((NH, None, 1, GATE_W), lambda c: (0, row_of(c), 0, 0))]


def _lane_put(col, lane_id, width=GATE_W):
    lane = lax.broadcasted_iota(jnp.int32, (col.shape[0], width), 1)
    return jnp.where(lane == lane_id, col, 0.0)


def _lane_get(block, lane_id):
    lane = lax.broadcasted_iota(jnp.int32, block.shape, 1)
    return jnp.sum(jnp.where(lane == lane_id, block, 0.0), axis=1, keepdims=True)


def _mlstm_fwd(proj, gates, bias):
    n = proj.shape[0]
    nc = n // LCH

    def body(q_ref, k_ref, v_ref, g_ref, bias_ref,
             h_ref, st_ref, cs_ref, ns_ref, ms_ref, c_scr, n_scr, m_scr):
        @pl.when(pl.program_id(0) == 0)
        def _():
            c_scr[...] = jnp.zeros_like(c_scr)
            n_scr[...] = jnp.zeros_like(n_scr)
            m_scr[...] = jnp.full_like(m_scr, -jnp.inf)

        g = g_ref[...] + bias_ref[...]
        gt = g.T[0:8, :]
        b = _chunk_cumsum(g)
        bt = b.T[0:8, :]
        stats = jnp.zeros((LCH, GATE_W), F32)
        for hd in range(NH):
            m_all = m_scr[hd]
            m_prev = m_all[0:1, 0:1]
            b_col = _lane_get(b, NH + hd)
            tri, rr, a_col, m_new, w_col, decay = _chunk_decay(
                _lane_get(g, hd), _sub_row(gt, hd), b_col, _sub_row(bt, NH + hd), m_prev)
            dmat = jnp.where(tri, b_col + rr, -jnp.inf)
            m_col = jnp.maximum(a_col, jnp.max(dmat, axis=1, keepdims=True))
            dl = jnp.exp(dmat - m_col)
            inter = jnp.exp(a_col - m_col)

            qs = q_ref[:, hd * DK:(hd + 1) * DK] * (DK ** -0.5)
            kk = k_ref[:, hd * DK:(hd + 1) * DK]
            vv = v_ref[:, hd * DV:(hd + 1) * DV]
            cst = c_scr[hd]
            nst = n_scr[hd]
            cs_ref[hd] = cst
            ns_ref[hd] = nst
            ms_ref[hd] = m_all

            sc = _dot_nt(qs, kk) * dl
            num = _dot(sc.astype(MXU), vv) + inter * _dot(qs, cst.astype(MXU))
            den = (jnp.sum(sc, axis=1, keepdims=True)
                   + inter * jnp.sum(qs.astype(F32) * nst, axis=1, keepdims=True))
            nrm = jnp.maximum(jnp.abs(den), jnp.exp(-m_col))
            h_ref[:, hd * DV:(hd + 1) * DV] = (num * (1.0 / nrm)).astype(h_ref.dtype)
            stats = stats + _lane_put(den, hd) + _lane_put(m_col, NH + hd)

            kw = kk.astype(F32) * w_col
            c_scr[hd] = decay * cst + _dot_tn(kw.astype(MXU), vv)
            n_scr[hd] = decay * nst + jnp.sum(kw, axis=0, keepdims=True)
            m_scr[hd] = jnp.broadcast_to(m_new, (1, GATE_W))
        st_ref[...] = stats

    q_spec, k_spec, v_spec = _qkv_specs(lambda c: c)
    return pl.pallas_call(
        body, name="mlstm_fwd", grid=(nc,),
        in_specs=[q_spec, k_spec, v_spec,
                  pl.BlockSpec((LCH, GATE_W), lambda c: (c, 0)),
                  pl.BlockSpec((1, GATE_W), lambda c: (0, 0))],
        out_specs=[pl.BlockSpec((LCH, VD), lambda c: (c, 0)),
                   pl.BlockSpec((LCH, GATE_W), lambda c: (c, 0))] + _state_specs(lambda c: c),
        out_shape=[jax.ShapeDtypeStruct((n, VD), MXU),
                   jax.ShapeDtypeStruct((n, GATE_W), F32),
                   jax.ShapeDtypeStruct((NH, nc, DK, DV), F32),
                   jax.ShapeDtypeStruct((NH, nc, 1, DK), F32),
                   jax.ShapeDtypeStruct((NH, nc, 1, GATE_W), F32)],
        scratch_shapes=[pltpu.VMEM((NH, DK, DV), F32), pltpu.VMEM((NH, 1, DK), F32),
                        pltpu.VMEM((NH, 1, GATE_W), F32)],
        compiler_params=_cparams("arbitrary"),
    )(proj, proj, proj, gates, bias)


def _mlstm_bwd(proj, gates, bias, h, dh, stats, cs, ns, ms, dproj):
    n = proj.shape[0]
    nc = n // LCH

    def body(q_ref, k_ref, v_ref, g_ref, bias_ref, h_ref, dh_ref, st_ref,
             cs_ref, ns_ref, ms_ref, dproj_in,
             dqkv_ref, dg_ref, gsum_ref, dc_scr, dn_scr):
        del dproj_in

        @pl.when(pl.program_id(0) == 0)
        def _():
            dc_scr[...] = jnp.zeros_like(dc_scr)
            dn_scr[...] = jnp.zeros_like(dn_scr)
            gsum_ref[...] = jnp.zeros_like(gsum_ref)

        g = g_ref[...] + bias_ref[...]
        gt = g.T[0:8, :]
        b = _chunk_cumsum(g)
        bt = b.T[0:8, :]
        stats = st_ref[...]
        r = lax.broadcasted_iota(jnp.int32, (LCH, LCH), 0)
        c = lax.broadcasted_iota(jnp.int32, (LCH, LCH), 1)
        sub8 = lax.broadcasted_iota(jnp.int32, (8, LCH), 0)
        last = lax.broadcasted_iota(jnp.int32, (LCH, 1), 0) == LCH - 1
        dli_all = jnp.zeros((LCH, GATE_W), F32)
        db_all = jnp.zeros((LCH, GATE_W), F32)
        colsum_t = jnp.zeros((8, LCH), F32)
        for hd in range(NH):
            m_prev = ms_ref[hd][0:1, 0:1]
            b_col = _lane_get(b, NH + hd)
            tri, rr, a_col, m_new, w_col, decay = _chunk_decay(
                _lane_get(g, hd), _sub_row(gt, hd), b_col, _sub_row(bt, NH + hd), m_prev)
            m_col = _lane_get(stats, NH + hd)
            dl = jnp.where(tri, jnp.exp((b_col - m_col) + rr), 0.0)
            inter = jnp.exp(a_col - m_col)

            qs = q_ref[:, hd * DK:(hd + 1) * DK] * (DK ** -0.5)
            kk = k_ref[:, hd * DK:(hd + 1) * DK]
            vv = v_ref[:, hd * DV:(hd + 1) * DV]
            qf = qs.astype(F32)
            kf = kk.astype(F32)
            cst = cs_ref[hd]
            nst = ns_ref[hd]
            cb = cst.astype(MXU)
            dcn = dc_scr[hd]
            dnn = dn_scr[hd]
            dcb = dcn.astype(MXU)

            den = _lane_get(stats, hd)
            floor = jnp.exp(-m_col)
            nrm = jnp.maximum(jnp.abs(den), floor)
            dhv = dh_ref[:, hd * DV:(hd + 1) * DV].astype(F32)
            rn = 1.0 / nrm
            dnum = dhv * rn
            dnum_b = dnum.astype(MXU)
            dhh = jnp.sum(dhv * h_ref[:, hd * DV:(hd + 1) * DV].astype(F32), axis=1, keepdims=True)
            dden = jnp.where(jnp.abs(den) > floor, -dhh * rn * jnp.sign(den), 0.0)

            sc = _dot_nt(qs, kk) * dl
            dsc = _dot_nt(dnum_b, vv) + dden
            da = (dl * dsc).astype(MXU)
            gmat = sc * dsc

            dq_state = _dot_nt(dnum_b, cb) + dden * nst
            dq = _dot(da, kk) + inter * dq_state
            dk_state = w_col * (_dot_nt(vv, dcb) + dnn)
            dk = _dot_tn(da, qs) + dk_state
            kw = (kf * w_col).astype(MXU)
            dv = _dot_tn(sc.astype(MXU), dnum_b) + _dot(kw, dcb)
            dqkv_ref[:, hd * DK:(hd + 1) * DK] = (dq * (DK ** -0.5)).astype(dqkv_ref.dtype)
            dqkv_ref[:, QK + hd * DK:QK + (hd + 1) * DK] = dk.astype(dqkv_ref.dtype)
            dqkv_ref[:, 2 * QK + hd * DV:2 * QK + (hd + 1) * DV] = dv.astype(dqkv_ref.dtype)

            e_col = inter * jnp.sum(qf * dq_state, axis=1, keepdims=True)
            h_col = jnp.sum(kf * dk_state, axis=1, keepdims=True)
            f_dec = decay * (jnp.sum(jnp.sum(cst * dcn, axis=1, keepdims=True), axis=0, keepdims=True)
                             + jnp.sum(nst * dnn, axis=1, keepdims=True))
            row_g = jnp.sum(gmat, axis=1, keepdims=True)
            col_g = jnp.sum(gmat, axis=0, keepdims=True)
            colsum_t = colsum_t + jnp.where(sub8 == hd, col_g, 0.0) - jnp.where(sub8 == NH + hd, col_g, 0.0)
            db_col = row_g + e_col - h_col
            db_col = db_col + jnp.where(last, jnp.sum(h_col, axis=0, keepdims=True) + f_dec, 0.0)
            dli_all = dli_all + _lane_put(h_col, hd)
            db_all = db_all + _lane_put(db_col, NH + hd)

            dc_scr[hd] = decay * dcn + _dot_tn((qf * inter).astype(MXU), dnum_b)
            dn_scr[hd] = decay * dnn + jnp.sum(qf * (inter * dden), axis=0, keepdims=True)
        colsum = jnp.concatenate([colsum_t, jnp.zeros((GATE_W - 8, LCH), F32)], axis=0).T
        lane = lax.broadcasted_iota(jnp.int32, (LCH, GATE_W), 1)
        dli_all = dli_all + jnp.where(lane < NH, colsum, 0.0)
        db_all = db_all + jnp.where(lane >= NH, colsum, 0.0)
        dlf_all = jnp.dot(jnp.where(c >= r, 1.0, 0.0), db_all, precision=lax.Precision.HIGHEST,
                          preferred_element_type=F32)
        dg = dli_all + dlf_all * _sigmoid(-g)
        dg_ref[...] = dg
        gsum_ref[0:1, 0:GATE_W] += jnp.sum(dg, axis=0, keepdims=True)

    rev = lambda c: nc - 1 - c
    q_spec, k_spec, v_spec = _qkv_specs(rev)
    hv_spec = pl.BlockSpec((LCH, VD), lambda c: (rev(c), 0))
    gate_spec = pl.BlockSpec((LCH, GATE_W), lambda c: (rev(c), 0))
    return pl.pallas_call(
        body, name="mlstm_bwd", grid=(nc,),
        in_specs=[q_spec, k_spec, v_spec, gate_spec,
                  pl.BlockSpec((1, GATE_W), lambda c: (0, 0)),
                  hv_spec, hv_spec, gate_spec] + _state_specs(rev) + [ANY],
        out_specs=[pl.BlockSpec((LCH, QKV_W), lambda c: (rev(c), C_Q // QKV_W)),
                   gate_spec,
                   pl.BlockSpec((8, D), lambda c: (0, 0))],
        out_shape=[jax.ShapeDtypeStruct(dproj.shape, dproj.dtype),
                   jax.ShapeDtypeStruct((n, GATE_W), F32),
                   jax.ShapeDtypeStruct((8, D), F32)],
        scratch_shapes=[pltpu.VMEM((NH, DK, DV), F32), pltpu.VMEM((NH, 1, DK), F32)],
        input_output_aliases={11: 0},
        compiler_params=_cparams("arbitrary"),
    )(proj, proj, proj, gates, bias, h, dh, stats, cs, ns, ms, dproj)


def _proj_spec(tm, col, width):
    return pl.BlockSpec((tm, width), lambda i: (i, col // width))


def _tail_in_specs(tm):
    return [_proj_spec(tm, 0, TAIL_W), _proj_spec(tm, C_XA, CONV_W),
            pl.BlockSpec((8, CONV_W), lambda i: (jnp.maximum(i * (tm // 8) - 1, 0), C_XA // CONV_W))]


def _tail_views(tail_r, conv_r, halo_r):
    cols = lambda ref, c0, w: ref.at[:, pl.ds(c0, w)]
    return (cols(tail_r, C_BA, D), cols(tail_r, C_ZA, D), cols(tail_r, C_O, VD), cols(tail_r, C_ZB, VD),
            cols(tail_r, C_GA, D), cols(tail_r, C_GB, D), cols(conv_r, 0, D), cols(conv_r, D, D),
            cols(halo_r, 0, D), cols(halo_r, D, D))


def _const(shape):
    return pl.BlockSpec(shape, lambda i: (0,) * len(shape))


def _conv_inputs(i, tm, xa_ref, ca_ref, xah_ref, cah_ref):
    u = _f32(xa_ref) * _f32(ca_ref)
    uh = jnp.where(i > 0, _f32(xah_ref) * _f32(cah_ref), 0.0)
    rid = lax.broadcasted_iota(jnp.int32, u.shape, 0)
    u1 = jnp.where(rid == 0, uh[7:8, :], pltpu.roll(u, 1, 0))
    u2 = jnp.where(rid == 0, uh[6:7, :], jnp.where(rid == 1, uh[7:8, :], pltpu.roll(u, 2, 0)))
    return u, u1, u2


def _head_norm(hh, gh):
    out = []
    for j in range(NH):
        hj = hh[:, j * DV:(j + 1) * DV]
        rj = _rstd(hj)
        out.append((hj * rj, rj, gh[:, j * DV:(j + 1) * DV]))
    return out


def _tail_fwd(proj, h, x, p, t, cw, cb, gh, gple, gfin, wa, wb, wo, wpg, wp):
    n = x.shape[0]
    tm = min(256, n)

    def body(tail_r, conv_r, halo_r, h_r, x_r, p_r, t_r,
             cw_r, cb_r, gh_r, gple_r, gfin_r, wa_r, wb_r, wo_r, wpg_r, wp_r,
             apre_o, bpre_o, mg_o, xn1_o, de_o, dgp_o, ya_o, yb_o, x1_o, dx2_o, acc_o):
        ba_r, za_r, o_r, zb_r, ga_r, gb_r, xa_r, ca_r, xah_r, cah_r = _tail_views(tail_r, conv_r, halo_r)
        i = pl.program_id(0)

        @pl.when(i == 0)
        def _():
            acc_o[...] = jnp.zeros_like(acc_o)

        u, u1, u2 = _conv_inputs(i, tm, xa_r, ca_r, xah_r, cah_r)
        cwv = cw_r[...]
        cv = cwv[0:1, :] * u2 + cwv[1:2, :] * u1 + cwv[2:3, :] * u + cb_r[...]
        za = za_r[...]
        a_pre = ba_r[...] * cv.astype(MXU) * (za * _sigmoid(za))
        apre_o[...] = a_pre
        ya = _dot(a_pre, wa_r[...]).astype(MXU)

        hn = _head_norm(_f32(h_r), gh_r[...])
        hbn = jnp.concatenate([xh * g for xh, _, g in hn], axis=1)
        zb = zb_r[...]
        b_pre = _sigmoid(o_r[...]) * hbn.astype(MXU) * (zb * _sigmoid(zb))
        bpre_o[...] = b_pre
        yb = _dot(b_pre, wb_r[...]).astype(MXU)
        ya_o[...] = ya
        yb_o[...] = yb

        mg = _sigmoid(ga_r[...]) * ya + _sigmoid(gb_r[...]) * yb
        mg_o[...] = mg
        x1 = x_r[...] + _dot(mg, wo_r[...])
        x1_o[...] = x1.astype(MXU)
        xn1 = (x1 * _rstd(x1) * gple_r[...]).astype(MXU)
        xn1_o[...] = xn1
        gt = _sigmoid(_dot(xn1, wpg_r[...]))
        e = _dot(p_r[...].astype(MXU), wp_r[...])
        x2 = x1 + gt * e
        r2 = _rstd(x2)
        xh2 = x2 * r2
        gf = gfin_r[...]
        diff = xh2 * gf - t_r[...]
        dy = diff * (1.0 / D)
        dx2 = _norm_bwd(dy, xh2, r2, gf)
        dx2_o[...] = dx2
        de_o[...] = (dx2 * gt).astype(MXU)
        dgp_o[...] = (dx2 * e * gt * (1.0 - gt)).astype(MXU)
        acc_o[0:1, :] += jnp.sum(dy * xh2, axis=0, keepdims=True)
        loss = 0.5 * jnp.sum(jnp.sum(diff * diff, axis=1, keepdims=True), axis=0, keepdims=True) * (1.0 / D)
        acc_o[1:2, :] += jnp.broadcast_to(loss, (1, D))

    row = lambda w, dt: (pl.BlockSpec((tm, w), lambda i: (i, 0)), jax.ShapeDtypeStruct((n, w), dt))
    outs = [row(D, MXU), row(VD, MXU), row(D, MXU), row(D, MXU), row(D, MXU), row(D, MXU),
            row(D, MXU), row(D, MXU), row(D, MXU), row(D, F32),
            (_const((8, D)), jax.ShapeDtypeStruct((8, D), F32))]
    return pl.pallas_call(
        body, name="tail_fwd", grid=(n // tm,),
        in_specs=_tail_in_specs(tm) + [
                  pl.BlockSpec((tm, VD), lambda i: (i, 0)),
                  pl.BlockSpec((tm, D), lambda i: (i, 0)),
                  pl.BlockSpec((tm, PLE), lambda i: (i, 0)),
                  pl.BlockSpec((tm, D), lambda i: (i, 0)),
                  _const((8, D)), _const((1, D)), _const((1, VD)), _const((1, D)), _const((1, D)),
                  _const((D, D)), _const((VD, D)), _const((D, D)), _const((D, D)), _const((PLE, D))],
        out_specs=[s for s, _ in outs],
        out_shape=[s for _, s in outs],
        compiler_params=_cparams("arbitrary"),
    )(proj, proj, proj, h, x, p, t, cw, cb, gh, gple, gfin, wa, wb, wo, wpg, wp)


def _tail_bwd(proj, h, dgp, dx2, x1, ya, yb, cw, cb, gh, gple, wpg, wo, wb, wa):
    n = x1.shape[0]
    tm = min(256, n)

    def body(tail_r, conv_r, halo_r, h_r,
             dgp_r, dx2_r, x1_r, ya_r, yb_r, cw_r, cb_r, gh_r, gple_r,
             wpg_r, wo_r, wb_r, wa_r,
             dproj_o, dcv_o, dh_o, dx1_o, dx1b_o, dya_o, dyb_o, acc_o):
        ba_r, za_r, o_r, zb_r, ga_r, gb_r, xa_r, ca_r, xah_r, cah_r = _tail_views(tail_r, conv_r, halo_r)
        i = pl.program_id(0)

        @pl.when(i == 0)
        def _():
            acc_o[...] = jnp.zeros_like(acc_o)

        dxn1 = _dot_nt(dgp_r[...], wpg_r[...])
        x1 = _f32(x1_r)
        r1 = _rstd(x1)
        xh1 = x1 * r1
        acc_o[0:1, 0:D] += jnp.sum(dxn1 * xh1, axis=0, keepdims=True)
        dx1 = dx2_r[...] + _norm_bwd(dxn1, xh1, r1, gple_r[...])
        dx1_o[...] = dx1
        dx1b = dx1.astype(MXU)
        dx1b_o[...] = dx1b

        dmg = _dot_nt(dx1b, wo_r[...]).astype(MXU)
        sga, dsga = _gate(ga_r[...])
        sgb, dsgb = _gate(gb_r[...])
        dya = dmg * sga
        dyb = dmg * sgb
        dya_o[...] = dya
        dyb_o[...] = dyb
        dproj_o[:, C_GA:C_GA + D] = dmg * ya_r[...] * dsga
        dproj_o[:, C_GB:C_GB + D] = dmg * yb_r[...] * dsgb

        db_pre = _dot_nt(dyb, wb_r[...]).astype(MXU)
        hn = _head_norm(_f32(h_r), gh_r[...])
        hbn = jnp.concatenate([xh * g for xh, _, g in hn], axis=1).astype(MXU)
        so, dso = _gate(o_r[...])
        zb = zb_r[...]
        szb, dszb = _gate(zb)
        sb = zb * szb
        t1 = db_pre * hbn
        dproj_o[:, C_O:C_O + VD] = t1 * (sb * dso)
        dproj_o[:, C_ZB:C_ZB + VD] = t1 * (so * (szb + zb * dszb))
        dhbn = (db_pre * (so * sb)).astype(F32)
        for j, (xh, rj, g) in enumerate(hn):
            dj = dhbn[:, j * DV:(j + 1) * DV]
            acc_o[1:2, j * DV:(j + 1) * DV] += jnp.sum(dj * xh, axis=0, keepdims=True)
            dh_o[:, j * DV:(j + 1) * DV] = _norm_bwd(dj, xh, rj, g).astype(MXU)

        da_pre = _dot_nt(dya, wa_r[...]).astype(MXU)
        u, u1, u2 = _conv_inputs(i, tm, xa_r, ca_r, xah_r, cah_r)
        cwv = cw_r[...]
        cv = cwv[0:1, :] * u2 + cwv[1:2, :] * u1 + cwv[2:3, :] * u + cb_r[...]
        za = za_r[...]
        sza, dsza = _gate(za)
        sa = za * sza
        ba = ba_r[...]
        t2 = da_pre * cv.astype(MXU)
        dproj_o[:, C_BA:C_BA + D] = t2 * sa
        dproj_o[:, C_ZA:C_ZA + D] = t2 * (ba * (sza + za * dsza))
        dcv_b = da_pre * (ba * sa)
        dcv_o[...] = dcv_b
        dcv = dcv_b.astype(F32)
        acc_o[2:3, 0:D] += jnp.sum(dcv, axis=0, keepdims=True)
        acc_o[3:4, 0:D] += jnp.sum(dcv * u2, axis=0, keepdims=True)
        acc_o[4:5, 0:D] += jnp.sum(dcv * u1, axis=0, keepdims=True)
        acc_o[5:6, 0:D] += jnp.sum(dcv * u, axis=0, keepdims=True)

    row = lambda w, dt: (pl.BlockSpec((tm, w), lambda i: (i, 0)), jax.ShapeDtypeStruct((n, w), dt))
    outs = [(pl.BlockSpec((tm, TAIL_W), lambda i: (i, 0)), jax.ShapeDtypeStruct((n, NMAIN), MXU)),
            row(D, MXU), row(VD, MXU), row(D, F32), row(D, MXU), row(D, MXU), row(D, MXU),
            (_const((8, VD)), jax.ShapeDtypeStruct((8, VD), F32))]
    rowin = lambda w: pl.BlockSpec((tm, w), lambda i: (i, 0))
    return pl.pallas_call(
        body, name="tail_bwd", grid=(n // tm,),
        in_specs=_tail_in_specs(tm) + [
                  rowin(VD), rowin(D), rowin(D), rowin(D), rowin(D), rowin(D),
                  _const((8, D)), _const((1, D)), _const((1, VD)), _const((1, D)),
                  _const((D, D)), _const((D, D)), _const((VD, D)), _const((D, D))],
        out_specs=[s for s, _ in outs],
        out_shape=[s for _, s in outs],
        compiler_params=_cparams("arbitrary"),
    )(proj, proj, proj, h, dgp, dx2, x1, ya, yb, cw, cb, gh, gple, wpg, wo, wb, wa)


def _conv_bwd(proj, dcv, cw, dproj):
    n = dcv.shape[0]
    tm = min(1024, n)
    nt = n // tm

    def body(xa_r, ca_r, dcv_r, nxt_r, cw_r, dproj_in, dxc_o):
        del dproj_in
        i = pl.program_id(0)
        dcv_v = _f32(dcv_r)
        nxt = jnp.where(i < nt - 1, _f32(nxt_r), 0.0)
        rid = lax.broadcasted_iota(jnp.int32, dcv_v.shape, 0)
        d1 = jnp.where(rid == tm - 1, nxt[0:1, :], pltpu.roll(dcv_v, tm - 1, 0))
        d2 = jnp.where(rid == tm - 2, nxt[0:1, :],
                       jnp.where(rid == tm - 1, nxt[1:2, :], pltpu.roll(dcv_v, tm - 2, 0)))
        cwv = cw_r[...]
        du = cwv[2:3, :] * dcv_v + cwv[1:2, :] * d1 + cwv[0:1, :] * d2
        dxc_o[:, 0:D] = (du * _f32(ca_r)).astype(MXU)
        dxc_o[:, D:2 * D] = (du * _f32(xa_r)).astype(MXU)

    return pl.pallas_call(
        body, name="conv_bwd", grid=(nt,),
        in_specs=[_proj_spec(tm, C_XA, 1024), _proj_spec(tm, C_CA, 1024),
                  pl.BlockSpec((tm, D), lambda i: (i, 0)),
                  pl.BlockSpec((8, D), lambda i: (jnp.minimum((i + 1) * (tm // 8), n // 8 - 1), 0)),
                  _const((8, D)), ANY],
        out_specs=pl.BlockSpec((tm, CONV_W), lambda i: (i, C_XA // CONV_W)),
        out_shape=jax.ShapeDtypeStruct(dproj.shape, dproj.dtype),
        input_output_aliases={5: 0},
        compiler_params=_cparams("arbitrary"),
    )(proj, proj, dcv, dcv, cw, dproj)


def _position():
    return lax.axis_index("x"), lax.axis_index("y"), lax.axis_index("c")


def _flip(v, bit):
    return 1 - v if bit else v


def _part_start(core, stride, tiled):
    return pl.multiple_of(core * stride, stride) if tiled else core * stride


def _scatter_copies(srcs, dsts, strides, send_sems, recv_sems, local_sems):
    x, y, c = _position()
    me = 4 * x + 2 * y + c
    na = len(srcs)
    copies = []
    for r in range(N_DEV):
        px, py, pc = _flip(x, (r >> 2) & 1), _flip(y, (r >> 1) & 1), _flip(c, r & 1)
        for a in range(na):
            rows = dsts[a].shape[1]
            src = srcs[a].at[2 * px + py, pl.ds(_part_start(pc, strides[a], len(dsts[a].shape) == 3), rows)]
            dst = dsts[a].at[me]
            if r == 0:
                copies.append(pltpu.make_async_copy(src, dst, local_sems.at[a]))
            else:
                k = (r - 1) * na + a
                copies.append(pltpu.make_async_remote_copy(
                    src_ref=src, dst_ref=dst, send_sem=send_sems.at[k], recv_sem=recv_sems.at[k],
                    device_id=(px, py, pc), device_id_type=MESH))
    return copies


def _input_grad(dproj, dgates, w_t, wg_t, x, dx1, g_mix, g_in):
    n = x.shape[0]
    tm, tk = min(1024, n), 2048
    nk = NMAIN // tk
    nt = n // tm

    def body(dp_r, dg_r, w_r, wg_r, x_r, dx1_r, g_r, gin,
             gx_o, acc_o, oin, acc, send_sems, recv_sems, local_sems):
        i = pl.program_id(0)
        kk = pl.program_id(1)
        copies = _scatter_copies((gin,), (oin,), (HALF_START,), send_sems, recv_sems, local_sems)

        @pl.when((i == 0) & (kk == 0))
        def _():
            acc_o[...] = jnp.zeros_like(acc_o)
            for cp in copies:
                cp.start()

        @pl.when(kk == 0)
        def _():
            acc[...] = _dot(dg_r[...].astype(MXU), wg_r[...])

        acc[...] += _dot(dp_r[...], w_r[...])

        @pl.when(kk == nk - 1)
        def _():
            dhn = acc[...]
            xv = x_r[...]
            r0 = _rstd(xv)
            xh = xv * r0
            acc_o[0:1, :] += jnp.sum(dhn * xh, axis=0, keepdims=True)
            gx_o[...] = dx1_r[...] + _norm_bwd(dhn, xh, r0, g_r[...])

        @pl.when((i == nt - 1) & (kk == nk - 1))
        def _():
            for cp in copies:
                cp.wait()

    nrem = N_DEV - 1
    r_in = HALF_SLABS
    return pl.pallas_call(
        body, name="input_grad", grid=(nt, nk),
        in_specs=[pl.BlockSpec((tm, tk), lambda i, kk: (i, kk)),
                  pl.BlockSpec((tm, GATE_W), lambda i, kk: (i, 0)),
                  pl.BlockSpec((tk, D), lambda i, kk: (kk, 0)),
                  pl.BlockSpec((GATE_W, D), lambda i, kk: (0, 0)),
                  pl.BlockSpec((tm, D), lambda i, kk: (i, 0)),
                  pl.BlockSpec((tm, D), lambda i, kk: (i, 0)),
                  pl.BlockSpec((1, D), lambda i, kk: (0, 0)),
                  ANY],
        out_specs=[pl.BlockSpec((tm, D), lambda i, kk: (i, 0)),
                   pl.BlockSpec((8, D), lambda i, kk: (0, 0)),
                   ANY],
        out_shape=[jax.ShapeDtypeStruct((n, D), F32), jax.ShapeDtypeStruct((8, D), F32),
                   jax.ShapeDtypeStruct((N_DEV, r_in) + g_in.shape[2:], g_in.dtype)],
        scratch_shapes=[pltpu.VMEM((tm, D), F32),
                        pltpu.SemaphoreType.DMA((nrem,)), pltpu.SemaphoreType.DMA((nrem,)),
                        pltpu.SemaphoreType.DMA((1,))],
        compiler_params=_cparams("arbitrary", "arbitrary"),
    )(dproj, dgates, w_t, wg_t, x, dx1, g_mix, g_in)


def _dw_in(dproj, hn, g_rest, ta=1024, tk=4096):
    n = hn.shape[0]
    tk = min(tk, n)
    ni, nk = NMAIN // ta, n // tk
    r_rest = g_rest.shape[1] // 2

    def body(a_ref, b_ref, grest, o_ref, orest, acc, send_sems, recv_sems, local_sems):
        i = pl.program_id(0)
        kk = pl.program_id(1)
        copies = _scatter_copies((grest,), (orest,), (r_rest,), send_sems, recv_sems, local_sems)

        @pl.when((i == 0) & (kk == 0))
        def _():
            for cp in copies:
                cp.start()

        @pl.when(kk == 0)
        def _():
            acc[...] = jnp.zeros_like(acc)

        acc[...] += _dot_tn(a_ref[...], b_ref[...])

        @pl.when(kk == nk - 1)
        def _():
            o_ref[...] = acc[...].astype(WIRE)

        @pl.when((i == ni - 1) & (kk == nk - 1))
        def _():
            for cp in copies:
                cp.wait()

    nrem = N_DEV - 1
    return pl.pallas_call(
        body, name="dw_in", grid=(ni, nk),
        in_specs=[pl.BlockSpec((tk, ta), lambda i, kk: (kk, i)),
                  pl.BlockSpec((tk, D), lambda i, kk: (kk, 0)), ANY],
        out_specs=[pl.BlockSpec((ta, D), lambda i, kk: (i, 0)), ANY],
        out_shape=[jax.ShapeDtypeStruct((NMAIN, D), WIRE),
                   jax.ShapeDtypeStruct((N_DEV, r_rest) + g_rest.shape[2:], g_rest.dtype)],
        scratch_shapes=[pltpu.VMEM((ta, D), F32),
                        pltpu.SemaphoreType.DMA((nrem,)), pltpu.SemaphoreType.DMA((nrem,)),
                        pltpu.SemaphoreType.DMA((1,))],
        compiler_params=_cparams("arbitrary", "arbitrary"),
    )(dproj, hn, g_rest)


def _pack_small(acc_f, acc_b, acc_x, gsum):
    def body(f_r, b_r, x_r, s_r, o_r):
        o_r[...] = jnp.zeros_like(o_r)
        o_r[0:1, :] = x_r[0:1, :]
        o_r[1:2, :] = b_r[2:3, 0:D]
        o_r[2:3, :] = b_r[1:2, 0:D]
        o_r[3:4, :] = b_r[1:2, D:2 * D]
        o_r[4:5, :] = b_r[0:1, 0:D]
        o_r[5:6, :] = f_r[0:1, :]
        o_r[6:9, :] = b_r[3:6, 0:D]
        lane = lax.broadcasted_iota(jnp.int32, (1, D), 1)
        o_r[9:10, :] = jnp.where(lane < 2 * NH, s_r[0:1, :], jnp.where(lane == 2 * NH, f_r[1:2, :], 0.0))

    return pl.pallas_call(
        body, name="pack_small",
        out_shape=jax.ShapeDtypeStruct((SMALL_ROWS, D), F32),
    )(acc_f, acc_b, acc_x, gsum)


def _sum_slots(r, name, tr=64, half=None):
    s, rows, w = r.shape
    tr = min(tr, rows)
    assert rows % tr == 0
    nt = rows // tr

    def body(*refs):
        r_ref, o_ref = refs[-2:]
        tot = r_ref[0].astype(F32)
        for k in range(1, s):
            tot = tot + r_ref[k].astype(F32)
        o_ref[...] = tot

    if half is None:
        return pl.pallas_call(
            body, name=name, grid=(nt,),
            in_specs=[pl.BlockSpec((s, tr, w), lambda i: (0, i, 0))],
            out_specs=pl.BlockSpec((tr, w), lambda i: (i, 0)),
            out_shape=jax.ShapeDtypeStruct((rows, w), F32),
            compiler_params=_cparams("arbitrary"),
        )(r)
    return pl.pallas_call(
        body, name=name,
        grid_spec=pltpu.PrefetchScalarGridSpec(
            num_scalar_prefetch=1, grid=(nt,),
            in_specs=[pl.BlockSpec((s, tr, w), lambda i, hf: (0, i, 0))],
            out_specs=pl.BlockSpec((tr, w), lambda i, hf: (hf[0] * nt + i, 0))),
        out_shape=jax.ShapeDtypeStruct((2 * rows, w), F32),
        compiler_params=_cparams("arbitrary"),
    )(half, r)


def _cast_slabs(a, tr=163):
    rows = a.shape[0]
    assert rows % tr == 0

    def body(a_ref, o_ref):
        o_ref[...] = a_ref[...].astype(MXU)

    spec = pl.BlockSpec((tr,) + a.shape[1:], lambda i: (i, 0, 0))
    return pl.pallas_call(
        body, name="cast_w_in", grid=(rows // tr,), in_specs=[spec], out_specs=spec,
        out_shape=jax.ShapeDtypeStruct(a.shape, MXU), compiler_params=_cparams("arbitrary"),
    )(a)


def _sum_slabs(r, name, tr=69):
    s, rows = r.shape[:2]
    assert rows % tr == 0

    def body(r_ref, o_ref):
        tot = r_ref[0].astype(F32)
        for k in range(1, s):
            tot = tot + r_ref[k].astype(F32)
        o_ref[...] = tot

    return pl.pallas_call(
        body, name=name, grid=(rows // tr,),
        in_specs=[pl.BlockSpec((s, tr) + r.shape[2:], lambda i: (0, i, 0, 0))],
        out_specs=pl.BlockSpec((tr,) + r.shape[2:], lambda i: (i, 0, 0)),
        out_shape=jax.ShapeDtypeStruct(r.shape[1:], F32),
        compiler_params=_cparams("arbitrary"),
    )(r)


def _adamw_body(w_r, g_r, m_r, v_r, d_o, m_o, v_o):
    c1 = 1.0 - ADAM_B1 ** ADAM_STEP
    c2 = 1.0 - ADAM_B2 ** ADAM_STEP
    gv = g_r[...]
    mn = ADAM_B1 * m_r[...] + (1.0 - ADAM_B1) * gv
    vn = ADAM_B2 * v_r[...] + (1.0 - ADAM_B2) * (gv * gv)
    m_o[...] = mn
    v_o[...] = vn
    d_o[...] = -ADAM_LR * ((mn / c1) / (jnp.sqrt(vn / c2) + ADAM_EPS) + ADAM_WD * w_r[...])


def _adamw_slabs(w, g, m, v, name, tr):
    rows = w.shape[0]
    assert rows % tr == 0 and w.shape == g.shape

    def body(*refs):
        _adamw_body(*refs)

    spec = pl.BlockSpec((tr,) + w.shape[1:], lambda i: (i, 0, 0))
    shp = jax.ShapeDtypeStruct(w.shape, F32)
    return pl.pallas_call(
        body, name=name, grid=(rows // tr,),
        in_specs=[spec] * 4, out_specs=[spec] * 3, out_shape=[shp] * 3,
        compiler_params=_cparams("arbitrary"),
    )(w, g, m, v)


def _adamw(w, g, m, v, name, g_row0=0, tr=256):
    lead = w.ndim == 3
    rows, cols = w.shape[-2:]
    tr = min(tr, rows)
    assert rows % tr == 0 and g_row0 % tr == 0 and g.shape[1] == cols

    def body(*refs):
        _adamw_body(*refs)

    if lead:
        spec = pl.BlockSpec((None, tr, cols), lambda i: (0, i, 0))
    else:
        spec = pl.BlockSpec((tr, cols), lambda i: (i, 0))
    g_spec = pl.BlockSpec((tr, cols), lambda i: (g_row0 // tr + i, 0))
    shp = jax.ShapeDtypeStruct(w.shape, F32)
    return pl.pallas_call(
        body, name=name, grid=(rows // tr,),
        in_specs=[spec, g_spec, spec, spec], out_specs=[spec] * 3, out_shape=[shp] * 3,
        compiler_params=_cparams("arbitrary"),
    )(w, g, m, v)


def _allgather_chips(halved, whole):
    arrs = [a for a, _ in halved] + list(whole)
    parts = [p for _, p in halved]
    nh, na = len(halved), len(arrs)
    n_ici = 3 * na

    def body(*refs):
        ins, outs = refs[:na], refs[na:2 * na]
        send_sems, recv_sems = refs[2 * na:]
        x, y, c = _position()
        mine = 2 * x + y

        def rows(a, core, stride, size, off=0):
            start = core * stride + off
            return pl.ds(pl.multiple_of(start, 16) if len(ins[a].shape) == 2 else start, size)

        def remote(src, dst, k, to):
            return pltpu.make_async_remote_copy(src_ref=src, dst_ref=dst, send_sem=send_sems.at[k],
                                                recv_sem=recv_sems.at[k], device_id=to, device_id_type=MESH)

        nbr_x, nbr_y, diag = (_flip(x, 1), y), (x, _flip(y, 1)), (_flip(x, 1), _flip(y, 1))
        chip = lambda p: 2 * p[0] + p[1]
        sends, passed = [], []
        from_x, from_y, fwd_x, fwd_y, pass_x, pass_y, pass_d = {}, {}, {}, {}, {}, {}, {}
        for a in range(na):
            if a < nh:
                s1, n1, s2, n2 = parts[a]
                h1 = (n1 + 1) // 2
                part = rows(a, c, s1, n1)
                from_x[a] = remote(ins[a].at[part], outs[a].at[mine, part], a, (*nbr_x, c))
                from_y[a] = remote(ins[a].at[part], outs[a].at[mine, part], na + a, (*nbr_y, c))
                lo = outs[a].at[chip(nbr_y), rows(a, c, s1, h1)]
                fwd_x[a] = remote(lo, lo, n_ici + a, (*nbr_x, c))
                hi = outs[a].at[chip(nbr_x), rows(a, c, s1, n1 - h1, h1)]
                fwd_y[a] = remote(hi, hi, n_ici + nh + a, (*nbr_y, c))
                for j, (who, table) in enumerate(((nbr_x, pass_x), (nbr_y, pass_y), (diag, pass_d))):
                    landed = outs[a].at[chip(who), rows(a, c, s2, n2)]
                    table[a] = remote(landed, landed, n_ici + (2 + j) * nh + a, (x, y, 1 - c))
                sends += [from_x[a], from_y[a]]
            else:
                sends += [remote(ins[a], outs[a].at[mine], j * na + a, (*who, c))
                          for j, who in enumerate((nbr_x, nbr_y, diag))]
        for cp in sends:
            cp.start()
        for a in range(nh):
            from_y[a].wait_recv()
            fwd_x[a].start()
            pass_y[a].start()
        for a in range(nh):
            from_x[a].wait_recv()
            fwd_y[a].start()
            pass_x[a].start()
        for a in range(nh):
            fwd_x[a].wait_recv()
            fwd_y[a].wait_recv()
            pass_d[a].start()
        for a in range(nh, na):
            for cp in sends[2 * nh + 3 * (a - nh):2 * nh + 3 * (a - nh) + 3]:
                cp.wait_recv()
        every = sends + [t[a] for t in (fwd_x, fwd_y, pass_x, pass_y, pass_d) for a in range(nh)]
        for a in range(nh):
            for t in (pass_x, pass_y, pass_d):
                t[a].wait_recv()
        for cp in every:
            cp.wait_send()

    nsem = n_ici + 5 * nh
    outs = pl.pallas_call(
        body, name="allgather_weights",
        in_specs=[ANY] * na, out_specs=[ANY] * na,
        out_shape=[jax.ShapeDtypeStruct((N_CHIPS,) + a.shape, a.dtype) for a in arrs],
        scratch_shapes=[pltpu.SemaphoreType.DMA((nsem,)), pltpu.SemaphoreType.DMA((nsem,))],
    )(*arrs)
    chip = 2 * lax.axis_index("x") + lax.axis_index("y")
    return [lax.dynamic_update_slice(o, a[None], (chip,) + (0,) * a.ndim) for o, a in zip(outs, arrs)]


def _join_halves(piece, full_rest, small):
    def body(pc_ref, rest_in, sm, oin, orest, osm, send_sems, recv_sems, local_sem):
        del rest_in
        x, y, c = _position()
        me = 4 * x + 2 * y + c
        sib = (x, y, 1 - c)
        rows = orest.shape[0] // 2
        mine = orest.at[pl.ds(pl.multiple_of(c * rows, rows), rows)]
        copies = [
            pltpu.make_async_remote_copy(
                src_ref=pc_ref.at[pl.ds(c, HALF_START)], dst_ref=oin.at[pl.ds(c * HALF_SLABS, HALF_START)],
                send_sem=send_sems.at[0], recv_sem=recv_sems.at[0], device_id=sib, device_id_type=MESH),
            pltpu.make_async_remote_copy(
                src_ref=mine, dst_ref=mine, send_sem=send_sems.at[1], recv_sem=recv_sems.at[1],
                device_id=sib, device_id_type=MESH)]
        na = 2
        copies.append(pltpu.make_async_copy(sm, osm.at[me], local_sem))
        for r in range(1, N_DEV):
            to = (_flip(x, (r >> 2) & 1), _flip(y, (r >> 1) & 1), _flip(c, r & 1))
            copies.append(pltpu.make_async_remote_copy(
                src_ref=sm, dst_ref=osm.at[me], send_sem=send_sems.at[na + r - 1],
                recv_sem=recv_sems.at[na + r - 1], device_id=to, device_id_type=MESH))
        for cp in copies:
            cp.start()
        for cp in copies:
            cp.wait()

    nsem = 2 + N_DEV - 1
    return pl.pallas_call(
        body, name="join_halves",
        in_specs=[ANY] * 3, out_specs=[ANY] * 3,
        out_shape=[jax.ShapeDtypeStruct((SLABS_CHIP,) + piece.shape[1:], piece.dtype),
                   jax.ShapeDtypeStruct(full_rest.shape, full_rest.dtype),
                   jax.ShapeDtypeStruct((N_DEV,) + small.shape, small.dtype)],
        scratch_shapes=[pltpu.SemaphoreType.DMA((nsem,)), pltpu.SemaphoreType.DMA((nsem,)),
                        pltpu.SemaphoreType.DMA],
        input_output_aliases={1: 1},
    )(piece, full_rest, small)


def _to_internal(w_slabs):
    order = sorted(SEGMENTS, key=lambda s: s[2])
    main = jnp.concatenate([w_slabs[g0 // 2:(g0 + w) // 2] for g0, w, _ in order], axis=0)
    gate = w_slabs[GATE_COL // 2:GATE_COL // 2 + NH].reshape(2 * NH, D)
    return main.reshape(NMAIN, D), jnp.pad(gate, ((0, GATE_W - 2 * NH), (0, 0)))


def _to_global(main_t, gate_t):
    main = main_t.reshape(NMAIN // 2, 16, D // 8)
    parts = sorted([(g0, main[i0 // 2:(i0 + w) // 2]) for g0, w, i0 in SEGMENTS]
                   + [(GATE_COL, gate_t[0:2 * NH].reshape(NH, 16, D // 8))], key=lambda s: s[0])
    return jnp.concatenate([p for _, p in parts], axis=0)


def _pad_rows(a, rows=8):
    return jnp.pad(a, ((0, rows - a.shape[0]), (0, 0)))


def kernel(x, p, g_mix, w_in, conv_w, conv_b, w_a_out, b_gates, g_head, w_b_out, w_o, g_ple, w_ple_gate, w_ple, g_final, loss_target, m_g_mix, m_w_in, m_conv_w, m_conv_b, m_w_a_out, m_b_gates, m_g_head, m_w_b_out, m_w_o, m_g_ple, m_w_ple_gate, m_w_ple, m_g_final, v_g_mix, v_w_in, v_conv_w, v_conv_b, v_w_a_out, v_b_gates, v_g_head, v_w_b_out, v_w_o, v_g_ple, v_w_ple_gate, v_w_ple, v_g_final):
    chip = 2 * lax.axis_index("x") + lax.axis_index("y")
    xs, ps, ts = x[0], p[0, 0], loss_target[0]
    g_fin = g_final.reshape(1, D)

    w_slabs = _cast_slabs(jnp.transpose(w_in, (2, 0, 1)).reshape(SLABS_CHIP, 16, D // 8))
    rest = [w[0].astype(MXU) for w in (w_a_out, w_b_out, w_o, w_ple_gate, w_ple)]
    even = lambda a: (a.shape[0] // 2,) * 4
    g_win, g_wa, g_wb, g_wo, g_wpg, g_wp, g_cw = _allgather_chips(
        [(w_slabs, (HALF_START, HALF_SLABS, HALF_SLABS, HALF_START))] + [(a, even(a)) for a in rest],
        [_pad_rows(conv_w[0])])
    w_t, wg_t = _to_internal(g_win.reshape(N_IN // 2, 16, D // 8))
    wa, wb, wo, wpg = [g.reshape(-1, D) for g in (g_wa, g_wb, g_wo, g_wpg)]
    wp = jnp.transpose(g_wp, (1, 0, 2)).reshape(PLE, D)
    cw = jnp.transpose(g_cw, (1, 0, 2)).reshape(8, D)

    bias = jnp.pad(b_gates, ((0, 0), (0, GATE_W - 2 * NH)))
    proj, hn, gates = _proj(xs, g_mix, w_t, wg_t)
    h, stats, cs, ns, ms = _mlstm_fwd(proj, gates, bias)
    (a_pre, b_pre, mg, xn1, de, dgp, ya, yb, x1, dx2, acc_f) = _tail_fwd(
        proj, h, xs, ps, ts, cw, conv_b, g_head, g_ple, g_fin, wa, wb, wo, wpg, wp)
    dproj, dcv, dh, dx1, dx1b, dya, dyb, acc_b = _tail_bwd(
        proj, h, dgp, dx2, x1, ya, yb, cw, conv_b, g_head, g_ple, wpg, wo, wb, wa)
    dproj = _conv_bwd(proj, dcv, cw, dproj)
    dproj, dgates, gsum = _mlstm_bwd(proj, gates, bias, h, dh, stats, cs, ns, ms, dproj)
    d_gate = _matmul_tn(dgates, hn, "dw_gate", out_dtype=WIRE)
    d_wa = _matmul_tn(a_pre, dya, "dw_a_out", out_dtype=WIRE)
    d_wb = _matmul_tn(b_pre, dyb, "dw_b_out", out_dtype=WIRE)
    d_wo = _matmul_tn(mg, dx1b, "dw_o", out_dtype=WIRE)
    d_wpg = _matmul_tn(xn1, dgp, "dw_ple_gate", out_dtype=WIRE)
    d_wp = _matmul_tn(ps, de, "dw_ple", out_dtype=WIRE)

    d_wp_c = jnp.transpose(d_wp.reshape(PLE, N_CHIPS, PLE), (1, 0, 2)).reshape(N_CHIPS, PACK_ROWS[4], D)
    g_rest = jnp.concatenate(
        [d_wa.reshape(N_CHIPS, -1, D), d_wb.reshape(N_CHIPS, -1, D), d_wo.reshape(N_CHIPS, -1, D),
         d_wpg.reshape(N_CHIPS, -1, D), d_wp_c], axis=1)
    d_main, r_rest = _dw_in(dproj, hn, g_rest)
    g_in = _to_global(d_main, d_gate).reshape(N_CHIPS, SLABS_CHIP, 16, D // 8)
    grad_x, acc_x, r_in = _input_grad(dproj, dgates, w_t, wg_t, xs, dx1, g_mix, g_in)
    small = _pack_small(acc_f, acc_b, acc_x, gsum)
    core = lax.axis_index("c").astype(jnp.int32)
    piece = _sum_slabs(r_in, "sum_w_in")
    gw_in, gw_rest, r_small = _join_halves(
        piece, _sum_slots(r_rest, "sum_rest", tr=96, half=core.reshape(1)), small)
    gw_in = lax.dynamic_update_slice(gw_in, piece, (core * HALF_START, 0, 0)).reshape(COLS_CHIP, 8, D // 8)
    gs = _sum_slots(r_small, "sum_small", tr=SMALL_ROWS)

    big = []
    row0 = 0
    for name, w, m, v in (("w_a_out", w_a_out, m_w_a_out, v_w_a_out), ("w_b_out", w_b_out, m_w_b_out, v_w_b_out),
                          ("w_o", w_o, m_w_o, v_w_o), ("w_ple_gate", w_ple_gate, m_w_ple_gate, v_w_ple_gate)):
        big.append((name, w, m, v, gw_rest, row0))
        row0 += w.shape[1]
    g_wp = gw_rest[row0:row0 + PACK_ROWS[4]].reshape(PLE, PLE)
    big.append(("w_ple", w_ple, m_w_ple, v_w_ple, g_wp, 0))
    upd = {name: _adamw(w, g, m, v, "adamw_" + name, g_row0=r0) for name, w, m, v, g, r0 in big}
    g_big = {name: (g if name == "w_ple" else g[r0:r0 + w.shape[1]])[None] for name, w, m, v, g, r0 in big}
    slabs = lambda a: jnp.transpose(a, (2, 0, 1)).reshape(COLS_CHIP, 8, D // 8)
    unslab = lambda a: jnp.transpose(a, (1, 2, 0)).reshape(1, D, COLS_CHIP)
    upd["w_in"] = [unslab(u) for u in _adamw_slabs(
        slabs(w_in), gw_in, slabs(m_w_in), slabs(v_w_in), "adamw_w_in", tr=326)]
    g_big["w_in"] = unslab(gw_in)

    lane = lax.broadcasted_iota(jnp.int32, (1, D), 1)
    g_small = jnp.concatenate([gs[0:6], jnp.where(lane < 2 * NH, gs[9:10], 0.0), jnp.zeros((1, D), F32)], axis=0)

    def small_pack(gm, cb_, bg, gh, gp, gf):
        return jnp.concatenate([gm, cb_, gh.reshape(2, D), gp, gf.reshape(1, D),
                                jnp.pad(bg, ((0, 0), (0, D - 2 * NH))), jnp.zeros((1, D), F32)], axis=0)

    ws = small_pack(g_mix, conv_b, b_gates, g_head, g_ple, g_final)
    ms_ = small_pack(m_g_mix, m_conv_b, m_b_gates, m_g_head, m_g_ple, m_g_final)
    vs = small_pack(v_g_mix, v_conv_b, v_b_gates, v_g_head, v_g_ple, v_g_final)
    upd_s = _adamw(ws, g_small, ms_, vs, "adamw_small")
    g_cw_mine = _pad_rows(lax.dynamic_slice(gs[6:9], (0, chip * PLE), (3, PLE)))
    upd_c = _adamw(_pad_rows(conv_w[0]), g_cw_mine, _pad_rows(m_conv_w[0]), _pad_rows(v_conv_w[0]),
                   "adamw_conv_w")

    def leaves(bigs, sm, cwv):
        return [sm[0:1], bigs["w_in"], cwv[0:3][None], sm[1:2], bigs["w_a_out"], sm[6:7, 0:2 * NH],
                sm[2:4].reshape(1, VD), bigs["w_b_out"], bigs["w_o"], sm[4:5], bigs["w_ple_gate"],
                bigs["w_ple"], sm[5]]

    loss = gs[9, 2 * NH]
    outs = [loss, grad_x[None]] + leaves(g_big, g_small, g_cw_mine)
    for k in range(3):
        outs += leaves({name: u[k] for name, u in upd.items()}, upd_s[k], upd_c[k])
    return tuple(outs)
```
